```python
import jax, jax.numpy as jnp
from jax import lax
import numpy as np

D_MODEL = 1024
BATCH = 8
SEQ = 2048
DEPTH = 2

HEAD_DIM = 64
N_ATTN_HEADS = 8
ATTN_WIDTH = N_ATTN_HEADS * HEAD_DIM
DILATED_PATTERNS = ((128, 1), (512, 4), (2048, 16))
ROPE_THETA = 10000.0
NEG_INF = -1e30

N_DELTA_HEADS = 4
DELTA_DK = 128
DELTA_DV = 128
DELTA_K_WIDTH = N_DELTA_HEADS * DELTA_DK
DELTA_V_WIDTH = N_DELTA_HEADS * DELTA_DV
DELTA_CONV = 4
DELTA_CHUNK = 64

MIX_WIDTH = ATTN_WIDTH + DELTA_V_WIDTH
IN_COLS = 3 * ATTN_WIDTH + 2 * DELTA_K_WIDTH + 2 * DELTA_V_WIDTH + 2 * N_DELTA_HEADS

D_FF = 2816
FFN_CONV = 3
EPS = 1e-6

kernel_name = 'hybrid_dilated_swa_gated_deltanet_convglu'


def rms_norm(x, w):
    xf = x.astype(jnp.float32)
    y = xf * lax.rsqrt(jnp.mean(xf * xf, axis=-1, keepdims=True) + EPS)
    return (y * w.astype(jnp.float32)).astype(x.dtype)


def rope_tables(seq, dim):
    inv = 1.0 / (ROPE_THETA ** (jnp.arange(0, dim, 2, dtype=jnp.float32) / dim))
    ang = jnp.arange(seq, dtype=jnp.float32)[:, None] * inv[None, :]
    return jnp.cos(ang), jnp.sin(ang)


def apply_rope(x, cos, sin):
    x1, x2 = jnp.split(x, 2, axis=-1)
    c = cos[None, :, None, :]
    s = sin[None, :, None, :]
    return jnp.concatenate([x1 * c - x2 * s, x1 * s + x2 * c], axis=-1)


def causal_dwconv(x, w):
    K, C = w.shape
    return lax.conv_general_dilated(
        x, w[:, None, :].astype(x.dtype), window_strides=(1,), padding=[(K - 1, 0)],
        dimension_numbers=('NWC', 'WIO', 'NWC'), feature_group_count=C)


def dilated_branch(q, k, v, window, dilation):
    B, S, H, D = q.shape
    n = window // dilation
    L = S // dilation
    nb = -(-L // n)
    Lp = nb * n

    def split(t):
        return t.reshape(B, L, dilation, H, D).transpose(0, 2, 3, 1, 4)

    qs = jnp.pad(split(q), ((0, 0), (0, 0), (0, 0), (0, Lp - L), (0, 0)))
    qs = qs.reshape(B, dilation, H, nb, n, D)

    def kv_blocks(t):
        tp = jnp.pad(split(t), ((0, 0), (0, 0), (0, 0), (n, Lp - L), (0, 0)))
        prev = tp[:, :, :, :Lp].reshape(B, dilation, H, nb, n, D)
        cur = tp[:, :, :, n:].reshape(B, dilation, H, nb, n, D)
        return jnp.concatenate([prev, cur], axis=4)

    kb = kv_blocks(k)
    vb = kv_blocks(v)
    s = jnp.einsum('brhnqd,brhnkd->brhnqk', qs, kb) * (D ** -0.5)
    a = jnp.arange(n)[:, None]
    c = jnp.arange(2 * n)[None, :]
    key_pos = (jnp.arange(nb)[:, None, None] - 1) * n + c[None]
    valid = ((c >= a) & (c <= a + n))[None] & (key_pos >= 0)
    s = jnp.where(valid, s, NEG_INF)
    m = jnp.max(s, axis=-1, keepdims=True)
    e = jnp.exp(s - m)
    l = jnp.sum(e, axis=-1, keepdims=True)
    o = jnp.einsum('brhnqk,brhnkd->brhnqd', e, vb) / l
    lse = (m + jnp.log(l))[..., 0]

    o = o.reshape(B, dilation, H, Lp, D)[:, :, :, :L]
    o = o.transpose(0, 3, 1, 2, 4).reshape(B, S, H, D)
    lse = lse.reshape(B, dilation, H, Lp)[:, :, :, :L]
    lse = lse.transpose(0, 3, 1, 2).reshape(B, S, H)
    return o, lse


def dilated_attention(q, k, v):
    outs, lses = [], []
    for window, dilation in DILATED_PATTERNS:
        o, lse = dilated_branch(q, k, v, window, dilation)
        outs.append(o)
        lses.append(lse)
    wts = jax.nn.softmax(jnp.stack(lses, axis=0), axis=0)
    return jnp.sum(wts[..., None] * jnp.stack(outs, axis=0), axis=0)


def gated_delta_rule(q, k, v, beta, g):
    B, S, H, K = q.shape
    V = v.shape[-1]
    C = DELTA_CHUNK
    N = S // C
    q = q * lax.rsqrt(jnp.sum(q * q, axis=-1, keepdims=True) + EPS) * (K ** -0.5)
    k = k * lax.rsqrt(jnp.sum(k * k, axis=-1, keepdims=True) + EPS)

    def chunk(t):
        return jnp.moveaxis(t.reshape((B, N, C, H) + t.shape[3:]), 3, 1)

    q, k, v, beta, g = chunk(q), chunk(k), chunk(v), chunk(beta), chunk(g)
    g = jnp.cumsum(g, axis=-1)
    tril = jnp.tril(jnp.ones((C, C), dtype=bool))
    strict = jnp.tril(jnp.ones((C, C), dtype=bool), -1)
    decay = jnp.exp(jnp.where(tril, g[..., :, None] - g[..., None, :], -jnp.inf))
    kb = k * beta[..., None]
    A = jnp.where(strict, jnp.einsum('bhnik,bhnjk->bhnij', kb, k) * decay, 0.0)
    eye = jnp.eye(C, dtype=A.dtype)
    T = lax.linalg.triangular_solve(A + eye, jnp.broadcast_to(eye, A.shape), left_side=True,
                                    lower=True, unit_diagonal=True)
    u = jnp.einsum('bhnij,bhnjv->bhniv', T, v * beta[..., None])
    w = jnp.einsum('bhnij,bhnjk->bhnik', T, kb * jnp.exp(g)[..., None])
    qk = jnp.einsum('bhnik,bhnjk->bhnij', q, k) * decay
    q_dec = q * jnp.exp(g)[..., None]
    k_dec = k * jnp.exp(g[..., -1:] - g)[..., None]
    g_last = jnp.exp(g[..., -1])

    def step(state, xs):
        q_i, k_i, u_i, w_i, qk_i, gl = xs
        v_new = u_i - jnp.einsum('bhck,bhkv->bhcv', w_i, state)
        o = jnp.einsum('bhck,bhkv->bhcv', q_i, state) + jnp.einsum('bhij,bhjv->bhiv', qk_i, v_new)
        state = state * gl[..., None, None] + jnp.einsum('bhck,bhcv->bhkv', k_i, v_new)
        return state, o

    xs = (jnp.moveaxis(q_dec, 2, 0), jnp.moveaxis(k_dec, 2, 0), jnp.moveaxis(u, 2, 0),
          jnp.moveaxis(w, 2, 0), jnp.moveaxis(qk, 2, 0), jnp.moveaxis(g_last, 2, 0))
    _, o = lax.scan(step, jnp.zeros((B, H, K, V), jnp.float32), xs)
    return jnp.transpose(o, (1, 0, 3, 2, 4)).reshape(B, S, H, V)


def hybrid_mixer(h, w_in, dn_conv_w, dn_a_log, dn_dt_bias, dn_norm_w, w_out, cos, sin):
    B, S, _ = h.shape
    proj = h @ w_in
    cuts = [ATTN_WIDTH, 2 * ATTN_WIDTH, 3 * ATTN_WIDTH,
            3 * ATTN_WIDTH + 2 * DELTA_K_WIDTH + DELTA_V_WIDTH,
            3 * ATTN_WIDTH + 2 * DELTA_K_WIDTH + 2 * DELTA_V_WIDTH,
            3 * ATTN_WIDTH + 2 * DELTA_K_WIDTH + 2 * DELTA_V_WIDTH + N_DELTA_HEADS]
    aq, ak, av, dn_qkv, dn_z, dn_b, dn_a = jnp.split(proj, cuts, axis=-1)

    aq = apply_rope(aq.reshape(B, S, N_ATTN_HEADS, HEAD_DIM).astype(jnp.float32), cos, sin)
    ak = apply_rope(ak.reshape(B, S, N_ATTN_HEADS, HEAD_DIM).astype(jnp.float32), cos, sin)
    av = av.reshape(B, S, N_ATTN_HEADS, HEAD_DIM).astype(jnp.float32)
    attn_out = dilated_attention(aq, ak, av).reshape(B, S, ATTN_WIDTH).astype(h.dtype)

    qkv = jax.nn.silu(causal_dwconv(dn_qkv, dn_conv_w)).astype(jnp.float32)
    dq, dk, dv = jnp.split(qkv, [DELTA_K_WIDTH, 2 * DELTA_K_WIDTH], axis=-1)
    beta = jax.nn.sigmoid(dn_b.astype(jnp.float32))
    g = -jnp.exp(dn_a_log.astype(jnp.float32)) * jax.nn.softplus(
        dn_a.astype(jnp.float32) + dn_dt_bias.astype(jnp.float32))
    o = gated_delta_rule(dq.reshape(B, S, N_DELTA_HEADS, DELTA_DK),
                         dk.reshape(B, S, N_DELTA_HEADS, DELTA_DK),
                         dv.reshape(B, S, N_DELTA_HEADS, DELTA_DV), beta, g)
    z = dn_z.reshape(B, S, N_DELTA_HEADS, DELTA_DV).astype(jnp.float32)
    dn_out = (rms_norm(o, dn_norm_w) * jax.nn.silu(z)).reshape(B, S, DELTA_V_WIDTH).astype(h.dtype)

    return jnp.concatenate([attn_out, dn_out], axis=-1) @ w_out


def conv_glu_ffn(h, ffn_w_in, ffn_conv_w, ffn_conv_b, ffn_w_out):
    u = causal_dwconv(h @ ffn_w_in, ffn_conv_w) + ffn_conv_b
    gate, up = jnp.split(u, 2, axis=-1)
    return (jax.nn.gelu(gate, approximate=True) * up) @ ffn_w_out


def _fwd_setup_inputs(seed: int = 0) -> dict:
    key = jax.random.key(seed)
    ks = jax.random.split(key, 16)
    f32 = jnp.float32
    nrm = lambda k, shape, scale: jax.random.normal(k, shape, f32) * scale
    dt = jnp.exp(jax.random.uniform(ks[4], (DEPTH, N_DELTA_HEADS), f32) *
                 (jnp.log(0.1) - jnp.log(0.001)) + jnp.log(0.001))
    return {
        'x': nrm(ks[0], (BATCH, SEQ, D_MODEL), 1.0),
        'w_in': nrm(ks[1], (DEPTH, D_MODEL, IN_COLS), D_MODEL ** -0.5),
        'dn_conv_w': nrm(ks[2], (DEPTH, DELTA_CONV, 2 * DELTA_K_WIDTH + DELTA_V_WIDTH), DELTA_CONV ** -0.5),
        'dn_a_log': jnp.log(jax.random.uniform(ks[3], (DEPTH, N_DELTA_HEADS), f32, 1.0, 16.0)),
        'dn_dt_bias': dt + jnp.log(-jnp.expm1(-dt)),
        'dn_norm_w': 1.0 + nrm(ks[5], (DEPTH, DELTA_DV), 0.05),
        'w_out': nrm(ks[6], (DEPTH, MIX_WIDTH, D_MODEL), MIX_WIDTH ** -0.5),
        'ffn_w_in': nrm(ks[7], (DEPTH, D_MODEL, 2 * D_FF), D_MODEL ** -0.5),
        'ffn_conv_w': nrm(ks[8], (DEPTH, FFN_CONV, 2 * D_FF), FFN_CONV ** -0.5),
        'ffn_conv_b': nrm(ks[9], (DEPTH, 2 * D_FF), 0.01),
        'ffn_w_out': nrm(ks[10], (DEPTH, D_FF, D_MODEL), D_FF ** -0.5),
        'norm_pre_mix': 1.0 + nrm(ks[11], (DEPTH, D_MODEL), 0.05),
        'norm_post_mix': 1.0 + nrm(ks[12], (DEPTH, D_MODEL), 0.05),
        'norm_pre_ffn': 1.0 + nrm(ks[13], (DEPTH, D_MODEL), 0.05),
        'norm_post_ffn': 1.0 + nrm(ks[14], (DEPTH, D_MODEL), 0.05),
    }


def _fwd_reference(x, w_in, dn_conv_w, dn_a_log, dn_dt_bias, dn_norm_w, w_out, ffn_w_in, ffn_conv_w,
              ffn_conv_b, ffn_w_out, norm_pre_mix, norm_post_mix, norm_pre_ffn, norm_post_ffn):
    cos, sin = rope_tables(x.shape[1], HEAD_DIM)
    for l in range(DEPTH):
        h = rms_norm(x, norm_pre_mix[l])
        mix = hybrid_mixer(h, w_in[l], dn_conv_w[l], dn_a_log[l], dn_dt_bias[l], dn_norm_w[l],
                           w_out[l], cos, sin)
        x = x + rms_norm(mix, norm_post_mix[l]).astype(x.dtype)
        h = rms_norm(x, norm_pre_ffn[l])
        f = conv_glu_ffn(h, ffn_w_in[l], ffn_conv_w[l], ffn_conv_b[l], ffn_w_out[l])
        x = x + rms_norm(f, norm_post_ffn[l]).astype(x.dtype)
    return x


import jax as _jax
import jax.numpy as _jnp

TWIN_FORMAT = 'train_step'
FWD_PARAMS = ['x', 'w_in', 'dn_conv_w', 'dn_a_log', 'dn_dt_bias', 'dn_norm_w', 'w_out', 'ffn_w_in', 'ffn_conv_w', 'ffn_conv_b', 'ffn_w_out', 'norm_pre_mix', 'norm_post_mix', 'norm_pre_ffn', 'norm_post_ffn']
TWIN_WEIGHTS = ['w_in', 'dn_conv_w', 'dn_a_log', 'dn_dt_bias', 'dn_norm_w', 'w_out', 'ffn_w_in', 'ffn_conv_w', 'ffn_conv_b', 'ffn_w_out', 'norm_pre_mix', 'norm_post_mix', 'norm_pre_ffn', 'norm_post_ffn']
TWIN_DIFF_INPUT = 'x'
TWIN_INPUTS = ['x', 'w_in', 'dn_conv_w', 'dn_a_log', 'dn_dt_bias', 'dn_norm_w', 'w_out', 'ffn_w_in', 'ffn_conv_w', 'ffn_conv_b', 'ffn_w_out', 'norm_pre_mix', 'norm_post_mix', 'norm_pre_ffn', 'norm_post_ffn', 'loss_target', 'm_w_in', 'm_dn_conv_w', 'm_dn_a_log', 'm_dn_dt_bias', 'm_dn_norm_w', 'm_w_out', 'm_ffn_w_in', 'm_ffn_conv_w', 'm_ffn_conv_b', 'm_ffn_w_out', 'm_norm_pre_mix', 'm_norm_post_mix', 'm_norm_pre_ffn', 'm_norm_post_ffn', 'v_w_in', 'v_dn_conv_w', 'v_dn_a_log', 'v_dn_dt_bias', 'v_dn_norm_w', 'v_w_out', 'v_ffn_w_in', 'v_ffn_conv_w', 'v_ffn_conv_b', 'v_ffn_w_out', 'v_norm_pre_mix', 'v_norm_post_mix', 'v_norm_pre_ffn', 'v_norm_post_ffn']
TWIN_OUTPUTS = ['loss', 'grad_x', 'grad_w_in', 'grad_dn_conv_w', 'grad_dn_a_log', 'grad_dn_dt_bias', 'grad_dn_norm_w', 'grad_w_out', 'grad_ffn_w_in', 'grad_ffn_conv_w', 'grad_ffn_conv_b', 'grad_ffn_w_out', 'grad_norm_pre_mix', 'grad_norm_post_mix', 'grad_norm_pre_ffn', 'grad_norm_post_ffn', 'delta_w_in', 'delta_dn_conv_w', 'delta_dn_a_log', 'delta_dn_dt_bias', 'delta_dn_norm_w', 'delta_w_out', 'delta_ffn_w_in', 'delta_ffn_conv_w', 'delta_ffn_conv_b', 'delta_ffn_w_out', 'delta_norm_pre_mix', 'delta_norm_post_mix', 'delta_norm_pre_ffn', 'delta_norm_post_ffn', 'new_m_w_in', 'new_m_dn_conv_w', 'new_m_dn_a_log', 'new_m_dn_dt_bias', 'new_m_dn_norm_w', 'new_m_w_out', 'new_m_ffn_w_in', 'new_m_ffn_conv_w', 'new_m_ffn_conv_b', 'new_m_ffn_w_out', 'new_m_norm_pre_mix', 'new_m_norm_post_mix', 'new_m_norm_pre_ffn', 'new_m_norm_post_ffn', 'new_v_w_in', 'new_v_dn_conv_w', 'new_v_dn_a_log', 'new_v_dn_dt_bias', 'new_v_dn_norm_w', 'new_v_w_out', 'new_v_ffn_w_in', 'new_v_ffn_conv_w', 'new_v_ffn_conv_b', 'new_v_ffn_w_out', 'new_v_norm_pre_mix', 'new_v_norm_post_mix', 'new_v_norm_pre_ffn', 'new_v_norm_post_ffn']
TWIN_LEAF_KINDS = {'loss': 'loss', 'grad_x': 'grad_x', 'grad_w_in': 'grad_w', 'grad_dn_conv_w': 'grad_w', 'grad_dn_a_log': 'grad_w', 'grad_dn_dt_bias': 'grad_w', 'grad_dn_norm_w': 'grad_w', 'grad_w_out': 'grad_w', 'grad_ffn_w_in': 'grad_w', 'grad_ffn_conv_w': 'grad_w', 'grad_ffn_conv_b': 'grad_w', 'grad_ffn_w_out': 'grad_w', 'grad_norm_pre_mix': 'grad_w', 'grad_norm_post_mix': 'grad_w', 'grad_norm_pre_ffn': 'grad_w', 'grad_norm_post_ffn': 'grad_w', 'delta_w_in': 'delta_w', 'delta_dn_conv_w': 'delta_w', 'delta_dn_a_log': 'delta_w', 'delta_dn_dt_bias': 'delta_w', 'delta_dn_norm_w': 'delta_w', 'delta_w_out': 'delta_w', 'delta_ffn_w_in': 'delta_w', 'delta_ffn_conv_w': 'delta_w', 'delta_ffn_conv_b': 'delta_w', 'delta_ffn_w_out': 'delta_w', 'delta_norm_pre_mix': 'delta_w', 'delta_norm_post_mix': 'delta_w', 'delta_norm_pre_ffn': 'delta_w', 'delta_norm_post_ffn': 'delta_w', 'new_m_w_in': 'new_m', 'new_m_dn_conv_w': 'new_m', 'new_m_dn_a_log': 'new_m', 'new_m_dn_dt_bias': 'new_m', 'new_m_dn_norm_w': 'new_m', 'new_m_w_out': 'new_m', 'new_m_ffn_w_in': 'new_m', 'new_m_ffn_conv_w': 'new_m', 'new_m_ffn_conv_b': 'new_m', 'new_m_ffn_w_out': 'new_m', 'new_m_norm_pre_mix': 'new_m', 'new_m_norm_post_mix': 'new_m', 'new_m_norm_pre_ffn': 'new_m', 'new_m_norm_post_ffn': 'new_m', 'new_v_w_in': 'new_v', 'new_v_dn_conv_w': 'new_v', 'new_v_dn_a_log': 'new_v', 'new_v_dn_dt_bias': 'new_v', 'new_v_dn_norm_w': 'new_v', 'new_v_w_out': 'new_v', 'new_v_ffn_w_in': 'new_v', 'new_v_ffn_conv_w': 'new_v', 'new_v_ffn_conv_b': 'new_v', 'new_v_ffn_w_out': 'new_v', 'new_v_norm_pre_mix': 'new_v', 'new_v_norm_post_mix': 'new_v', 'new_v_norm_pre_ffn': 'new_v', 'new_v_norm_post_ffn': 'new_v'}


def _forward(args):
    return _fwd_reference(*[args[k] for k in FWD_PARAMS])


def _output_shape():
    out = _jax.eval_shape(lambda: _forward(_fwd_setup_inputs(0)))
    return out.shape, out.dtype

N_MICROBATCH = 1
ADAM_LR = 0.001
ADAM_B1 = 0.9
ADAM_B2 = 0.999
ADAM_EPS = 1e-08
ADAM_WD = 0.01
ADAM_STEP = 10
PER_EXAMPLE_BATCH_AXIS = {'x': 0, 'loss_target': 0}
SHARED_INPUTS = []
_WEIGHT_DTYPES = {'w_in': _jnp.float32, 'dn_conv_w': _jnp.float32, 'dn_a_log': _jnp.float32, 'dn_dt_bias': _jnp.float32, 'dn_norm_w': _jnp.float32, 'w_out': _jnp.float32, 'ffn_w_in': _jnp.float32, 'ffn_conv_w': _jnp.float32, 'ffn_conv_b': _jnp.float32, 'ffn_w_out': _jnp.float32, 'norm_pre_mix': _jnp.float32, 'norm_post_mix': _jnp.float32, 'norm_pre_ffn': _jnp.float32, 'norm_post_ffn': _jnp.float32}
MOMENT_SCALE = {'w_in': 6.397261e-01, 'dn_conv_w': 9.231319e-01, 'dn_a_log': 5.302679e+00, 'dn_dt_bias': 5.045048e+00, 'dn_norm_w': 3.801861e+00, 'w_out': 1.446878e+00, 'ffn_w_in': 3.393966e-01, 'ffn_conv_w': 3.908108e-01, 'ffn_conv_b': 1.427790e+00, 'ffn_w_out': 6.650564e-01, 'norm_pre_mix': 1.227712e+00, 'norm_post_mix': 1.607556e+01, 'norm_pre_ffn': 7.555332e-01, 'norm_post_ffn': 1.595694e+01}


def _to_microbatches(a, axis):
    t = _jnp.moveaxis(a, axis, 0)
    t = t.reshape((N_MICROBATCH, t.shape[0] // N_MICROBATCH) + t.shape[1:])
    return _jnp.moveaxis(t, 1, axis + 1)


def setup_inputs(seed: int = 0) -> dict:
    inp = _fwd_setup_inputs(seed)
    key = _jax.random.fold_in(_jax.random.key(seed), 7919)
    shape, _ = _output_shape()
    out = dict(inp)
    out["loss_target"] = _jax.random.normal(_jax.random.fold_in(key, 0), shape, _jnp.float32)
    for i, name in enumerate(TWIN_WEIGHTS):
        w = inp[name].astype(_jnp.float32)
        if MOMENT_SCALE is None:
            s = _jnp.sqrt(_jnp.mean(_jnp.square(w)) + 1e-30)
        else:
            s = MOMENT_SCALE[name]
        km, kv = _jax.random.split(_jax.random.fold_in(key, i + 1))
        out[name] = w
        out["m_" + name] = s * _jax.random.normal(km, w.shape, _jnp.float32)
        out["v_" + name] = (s * s) * _jax.random.uniform(kv, w.shape, _jnp.float32, 0.5, 1.5)
    if N_MICROBATCH > 1:
        for name, axis in PER_EXAMPLE_BATCH_AXIS.items():
            out[name] = _to_microbatches(out[name], axis)
    return {'x': out['x'], 'w_in': out['w_in'], 'dn_conv_w': out['dn_conv_w'], 'dn_a_log': out['dn_a_log'], 'dn_dt_bias': out['dn_dt_bias'], 'dn_norm_w': out['dn_norm_w'], 'w_out': out['w_out'], 'ffn_w_in': out['ffn_w_in'], 'ffn_conv_w': out['ffn_conv_w'], 'ffn_conv_b': out['ffn_conv_b'], 'ffn_w_out': out['ffn_w_out'], 'norm_pre_mix': out['norm_pre_mix'], 'norm_post_mix': out['norm_post_mix'], 'norm_pre_ffn': out['norm_pre_ffn'], 'norm_post_ffn': out['norm_post_ffn'], 'loss_target': out['loss_target'], 'm_w_in': out['m_w_in'], 'm_dn_conv_w': out['m_dn_conv_w'], 'm_dn_a_log': out['m_dn_a_log'], 'm_dn_dt_bias': out['m_dn_dt_bias'], 'm_dn_norm_w': out['m_dn_norm_w'], 'm_w_out': out['m_w_out'], 'm_ffn_w_in': out['m_ffn_w_in'], 'm_ffn_conv_w': out['m_ffn_conv_w'], 'm_ffn_conv_b': out['m_ffn_conv_b'], 'm_ffn_w_out': out['m_ffn_w_out'], 'm_norm_pre_mix': out['m_norm_pre_mix'], 'm_norm_post_mix': out['m_norm_post_mix'], 'm_norm_pre_ffn': out['m_norm_pre_ffn'], 'm_norm_post_ffn': out['m_norm_post_ffn'], 'v_w_in': out['v_w_in'], 'v_dn_conv_w': out['v_dn_conv_w'], 'v_dn_a_log': out['v_dn_a_log'], 'v_dn_dt_bias': out['v_dn_dt_bias'], 'v_dn_norm_w': out['v_dn_norm_w'], 'v_w_out': out['v_w_out'], 'v_ffn_w_in': out['v_ffn_w_in'], 'v_ffn_conv_w': out['v_ffn_conv_w'], 'v_ffn_conv_b': out['v_ffn_conv_b'], 'v_ffn_w_out': out['v_ffn_w_out'], 'v_norm_pre_mix': out['v_norm_pre_mix'], 'v_norm_post_mix': out['v_norm_post_mix'], 'v_norm_pre_ffn': out['v_norm_pre_ffn'], 'v_norm_post_ffn': out['v_norm_post_ffn']}


def _loss(weights, diff, rest, loss_target):
    with _jax.named_scope("forward"):
        args = {**rest, TWIN_DIFF_INPUT: diff, **{k: w.astype(_WEIGHT_DTYPES[k]) for k, w in weights.items()}}
        y = _forward(args)
    with _jax.named_scope("loss_head"):
        err = _jnp.square(y.astype(_jnp.float32) - loss_target)
        return 0.5 * _jnp.sum(_jnp.mean(err, axis=-1)) if err.ndim else 0.5 * err


def _adamw(w, g, m, v):
    m = ADAM_B1 * m + (1.0 - ADAM_B1) * g
    v = ADAM_B2 * v + (1.0 - ADAM_B2) * _jnp.square(g)
    m_hat = m / (1.0 - ADAM_B1 ** ADAM_STEP)
    v_hat = v / (1.0 - ADAM_B2 ** ADAM_STEP)
    delta = -ADAM_LR * (m_hat / (_jnp.sqrt(v_hat) + ADAM_EPS) + ADAM_WD * w)
    return delta, m, v


def reference(x, w_in, dn_conv_w, dn_a_log, dn_dt_bias, dn_norm_w, w_out, ffn_w_in, ffn_conv_w, ffn_conv_b, ffn_w_out, norm_pre_mix, norm_post_mix, norm_pre_ffn, norm_post_ffn, loss_target, m_w_in, m_dn_conv_w, m_dn_a_log, m_dn_dt_bias, m_dn_norm_w, m_w_out, m_ffn_w_in, m_ffn_conv_w, m_ffn_conv_b, m_ffn_w_out, m_norm_pre_mix, m_norm_post_mix, m_norm_pre_ffn, m_norm_post_ffn, v_w_in, v_dn_conv_w, v_dn_a_log, v_dn_dt_bias, v_dn_norm_w, v_w_out, v_ffn_w_in, v_ffn_conv_w, v_ffn_conv_b, v_ffn_w_out, v_norm_pre_mix, v_norm_post_mix, v_norm_pre_ffn, v_norm_post_ffn):
    given = dict(x=x, w_in=w_in, dn_conv_w=dn_conv_w, dn_a_log=dn_a_log, dn_dt_bias=dn_dt_bias, dn_norm_w=dn_norm_w, w_out=w_out, ffn_w_in=ffn_w_in, ffn_conv_w=ffn_conv_w, ffn_conv_b=ffn_conv_b, ffn_w_out=ffn_w_out, norm_pre_mix=norm_pre_mix, norm_post_mix=norm_post_mix, norm_pre_ffn=norm_pre_ffn, norm_post_ffn=norm_post_ffn, loss_target=loss_target, m_w_in=m_w_in, m_dn_conv_w=m_dn_conv_w, m_dn_a_log=m_dn_a_log, m_dn_dt_bias=m_dn_dt_bias, m_dn_norm_w=m_dn_norm_w, m_w_out=m_w_out, m_ffn_w_in=m_ffn_w_in, m_ffn_conv_w=m_ffn_conv_w, m_ffn_conv_b=m_ffn_conv_b, m_ffn_w_out=m_ffn_w_out, m_norm_pre_mix=m_norm_pre_mix, m_norm_post_mix=m_norm_post_mix, m_norm_pre_ffn=m_norm_pre_ffn, m_norm_post_ffn=m_norm_post_ffn, v_w_in=v_w_in, v_dn_conv_w=v_dn_conv_w, v_dn_a_log=v_dn_a_log, v_dn_dt_bias=v_dn_dt_bias, v_dn_norm_w=v_dn_norm_w, v_w_out=v_w_out, v_ffn_w_in=v_ffn_w_in, v_ffn_conv_w=v_ffn_conv_w, v_ffn_conv_b=v_ffn_conv_b, v_ffn_w_out=v_ffn_w_out, v_norm_pre_mix=v_norm_pre_mix, v_norm_post_mix=v_norm_post_mix, v_norm_pre_ffn=v_norm_pre_ffn, v_norm_post_ffn=v_norm_post_ffn)
    weights = {n: given[n] for n in TWIN_WEIGHTS}
    shared = {n: given[n] for n in SHARED_INPUTS}
    per_example = {n: given[n] for n in ['x']}
    grad_fn = _jax.value_and_grad(_loss, argnums=(0, 1))

    def one_microbatch(ex, loss_target):
        ex = dict(ex)
        diff = ex.pop(TWIN_DIFF_INPUT)
        return grad_fn(weights, diff, {**shared, **ex}, loss_target)

    if N_MICROBATCH == 1:
        loss, (grad_w, grad_x) = one_microbatch(per_example, given["loss_target"])
    else:
        def body(carry, xs):
            loss_sum, grad_sum = carry
            l_k, (gw_k, gx_k) = one_microbatch(xs[0], xs[1])
            with _jax.named_scope("update"):
                return (loss_sum + l_k, _jax.tree.map(_jnp.add, grad_sum, gw_k)), gx_k

        init = (_jnp.zeros((), _jnp.float32), _jax.tree.map(_jnp.zeros_like, weights))
        (loss, grad_w), grad_x = _jax.lax.scan(body, init, (per_example, given["loss_target"]))
    with _jax.named_scope("update"):
        delta_w, new_m, new_v = {}, {}, {}
        for n in TWIN_WEIGHTS:
            delta_w[n], new_m[n], new_v[n] = _adamw(weights[n], grad_w[n], given["m_" + n], given["v_" + n])
    return (loss, grad_x, *[grad_w[n] for n in TWIN_WEIGHTS], *[delta_w[n] for n in TWIN_WEIGHTS],
            *[new_m[n] for n in TWIN_WEIGHTS], *[new_v[n] for n in TWIN_WEIGHTS])
```

```python
import functools
import jax
import jax.numpy as jnp
from jax import lax
from jax.experimental import pallas as pl
from jax.experimental.pallas import tpu as pltpu

f32, bf16 = jnp.float32, jnp.bfloat16
SDS = jax.ShapeDtypeStruct
HI = lax.Precision.HIGHEST
MESH = pl.DeviceIdType.MESH
ANY = pl.BlockSpec(memory_space=pl.ANY)

D_MODEL = 1024
N_HEADS_A, HEAD_DIM = 8, 64
ATTN_W = 512
N_HEADS_D, DK = 4, 128
CHUNK = 64
D_FF = 2816
IN_COLS = 3592
PROJ_W = 3840
BRANCHES = ((1, 16), (4, 4), (16, 1))
EPS = 1e-6
NEG = -1e30
ROW_TILE = 256
VMEM_LIMIT = 56 * 1024 * 1024

ADAM_LR, ADAM_B1, ADAM_B2, ADAM_EPS, ADAM_WD, ADAM_STEP = 0.001, 0.9, 0.999, 1e-08, 0.01, 10


def _params(*sem):
    return pltpu.CompilerParams(dimension_semantics=sem, vmem_limit_bytes=VMEM_LIMIT)


def _mm(a, b, mode, tm, tn, out_dtype, name):
    if mode == "nn":
        (M, K), N = a.shape, b.shape[1]
        dims = (((1,), (0,)), ((), ()))
        a_spec = pl.BlockSpec((tm, K), lambda i, j: (i, 0))
        b_spec = pl.BlockSpec((K, tn), lambda i, j: (0, j))
    elif mode == "nt":
        (M, K), N = a.shape, b.shape[0]
        dims = (((1,), (1,)), ((), ()))
        a_spec = pl.BlockSpec((tm, K), lambda i, j: (i, 0))
        b_spec = pl.BlockSpec((tn, K), lambda i, j: (j, 0))
    else:
        (K, M), N = a.shape, b.shape[1]
        dims = (((0,), (0,)), ((), ()))
        a_spec = pl.BlockSpec((K, tm), lambda i, j: (0, i))
        b_spec = pl.BlockSpec((K, tn), lambda i, j: (0, j))
    assert M % tm == 0 and N % tn == 0, (name, M, N, tm, tn)

    def body(a_ref, b_ref, o_ref):
        o_ref[...] = lax.dot_general(a_ref[...].astype(bf16), b_ref[...].astype(bf16), dims,
                                     preferred_element_type=f32).astype(o_ref.dtype)

    return pl.pallas_call(
        body, grid=(M // tm, N // tn), in_specs=[a_spec, b_spec],
        out_specs=pl.BlockSpec((tm, tn), lambda i, j: (i, j)),
        out_shape=SDS((M, N), out_dtype), name=name, compiler_params=_params("parallel", "arbitrary"))(a, b)


def _row_spec(r, tm):
    if isinstance(r, tuple):
        arr, width, cb = r
        return arr, pl.BlockSpec((tm, width), lambda i, cb=cb: (i, cb))
    return r, pl.BlockSpec((tm, r.shape[1]), lambda i: (i, 0))


def _full_spec(p):
    return pl.BlockSpec(p.shape, lambda i: (0,) * p.ndim)


def _rows(fn, rows, params, outs, name, tm=ROW_TILE):
    arrs, specs = zip(*[_row_spec(r, tm) for r in rows])
    S = arrs[0].shape[0]
    nr, npar = len(rows), len(params)

    def body(*refs):
        vals = fn(*[r[...].astype(f32) for r in refs[:nr]], *[p[...] for p in refs[nr:nr + npar]])
        for o_ref, v in zip(refs[nr + npar:], vals):
            o_ref[...] = v.astype(o_ref.dtype)

    return pl.pallas_call(
        body, grid=(S // tm,), in_specs=list(specs) + [_full_spec(p) for p in params],
        out_specs=[pl.BlockSpec((tm, w), lambda i: (i, 0)) for w, _ in outs],
        out_shape=[SDS((S, w), dt) for w, dt in outs], name=name, compiler_params=_params("parallel"))(*arrs, *params)


def _rows_vjp(fn, rows, params, cts, wrt_rows, wrt_params, name, adds=None, tm=ROW_TILE):
    adds = adds or {}
    arrs, specs = zip(*[_row_spec(r, tm) for r in rows])
    carrs, cspecs = zip(*[_row_spec(c, tm) for c in cts])
    add_keys = sorted(adds)
    aarrs = [adds[k] for k in add_keys]
    S = arrs[0].shape[0]
    nr, npar, nc, na = len(rows), len(params), len(cts), len(aarrs)
    widths = [specs[k].block_shape[1] for k in wrt_rows]

    def body(*refs):
        i = pl.program_id(0)
        rv = [r[...].astype(f32) for r in refs[:nr]]
        pv = [p[...] for p in refs[nr:nr + npar]]
        cv = tuple(c[...].astype(f32) for c in refs[nr + npar:nr + npar + nc])
        av = dict(zip(add_keys, refs[nr + npar + nc:nr + npar + nc + na]))
        o = refs[nr + npar + nc + na:]
        _, vjp = jax.vjp(fn, *rv, *pv)
        g = vjp(cv)
        for n, k in enumerate(wrt_rows):
            val = g[k]
            if k in av:
                val = val + av[k][...]
            o[n][...] = val
        for n, k in enumerate(wrt_params):
            ref = o[len(wrt_rows) + n]

            @pl.when(i == 0)
            def _(ref=ref):
                ref[...] = jnp.zeros_like(ref)

            ref[...] += g[nr + k]

    res = pl.pallas_call(
        body, grid=(S // tm,),
        in_specs=list(specs) + [_full_spec(p) for p in params] + list(cspecs)
        + [pl.BlockSpec((tm, a.shape[1]), lambda i: (i, 0)) for a in aarrs],
        out_specs=[pl.BlockSpec((tm, w), lambda i: (i, 0)) for w in widths] + [_full_spec(params[k]) for k in wrt_params],
        out_shape=[SDS((S, w), f32) for w in widths] + [SDS(params[k].shape, f32) for k in wrt_params],
        name=name, compiler_params=_params("arbitrary"))(*arrs, *params, *carrs, *aarrs)
    return res[:len(wrt_rows)], res[len(wrt_rows):]


def _rms(x, w):
    return x * lax.rsqrt(jnp.mean(x * x, axis=-1, keepdims=True) + EPS) * w


def _rms_fn(x, w):
    return (_rms(x, w),)


def _res_rms_fn(f, res, w):
    return (res + _rms(f, w),)


def _combine_fn(o1, o2, o3, l1, l2, l3):
    m = jnp.maximum(jnp.maximum(l1, l2), l3)
    e1, e2, e3 = jnp.exp(l1 - m), jnp.exp(l2 - m), jnp.exp(l3 - m)
    return ((e1 * o1 + e2 * o2 + e3 * o3) / (e1 + e2 + e3),)


def _swap_halves(x):
    lane = lax.broadcasted_iota(jnp.int32, x.shape, 1)
    first = (lane % HEAD_DIM) < (HEAD_DIM // 2)
    n = x.shape[1]
    return jnp.where(first, pltpu.roll(x, n - HEAD_DIM // 2, 1), pltpu.roll(x, HEAD_DIM // 2, 1))


def _rope_fwd_fn(q, k, cos, sgn_sin):
    scale = HEAD_DIM ** -0.5
    return ((q * cos + _swap_halves(q) * sgn_sin) * scale, k * cos + _swap_halves(k) * sgn_sin)


def _rope_bwd_fn(dq1, dq2, dq3, dk1, dk2, dk3, dv1, dv2, dv3, cos, sgn_sin):
    dq = (dq1 + dq2 + dq3) * (HEAD_DIM ** -0.5)
    dk = dk1 + dk2 + dk3
    return (dq * cos + _swap_halves(dq * sgn_sin), dk * cos + _swap_halves(dk * sgn_sin), dv1 + dv2 + dv3)


def _band_masks(j, nb):
    a = lax.broadcasted_iota(jnp.int32, (128, 128), 0)
    c = lax.broadcasted_iota(jnp.int32, (128, 128), 1)
    return c <= a, jnp.logical_and(c >= a, (j % nb) != 0)


def _nt(a, b):
    return lax.dot_general(a, b, (((1,), (1,)), ((), ())), preferred_element_type=f32)


def _tn(a, b):
    return lax.dot_general(a, b, (((0,), (0,)), ((), ())), preferred_element_type=f32)


def _attn_fwd(q, k, v, nb, name):
    H, S, E = q.shape

    def body(q_ref, k_ref, v_ref, o_ref, l_ref):
        def blk(j, carry):
            st = pl.multiple_of(j * 128, 128)
            pst = pl.multiple_of(jnp.maximum(j - 1, 0) * 128, 128)
            qb = q_ref[0, pl.ds(st, 128), :]
            mc, mp = _band_masks(j, nb)
            sc = jnp.where(mc, _nt(qb, k_ref[0, pl.ds(st, 128), :]), NEG)
            sp = jnp.where(mp, _nt(qb, k_ref[0, pl.ds(pst, 128), :]), NEG)
            m = jnp.maximum(jnp.max(sc, axis=1, keepdims=True), jnp.max(sp, axis=1, keepdims=True))
            pc, pp = jnp.exp(sc - m), jnp.exp(sp - m)
            l = jnp.sum(pc, axis=1, keepdims=True) + jnp.sum(pp, axis=1, keepdims=True)
            o = (jnp.dot(pc.astype(bf16), v_ref[0, pl.ds(st, 128), :], preferred_element_type=f32)
                 + jnp.dot(pp.astype(bf16), v_ref[0, pl.ds(pst, 128), :], preferred_element_type=f32)) / l
            o_ref[0, pl.ds(st, 128), :] = o
            l_ref[0, pl.ds(st, 128), :] = jnp.broadcast_to(m + jnp.log(l), (128, E))
            return carry

        lax.fori_loop(0, S // 128, blk, 0)

    spec = pl.BlockSpec((1, S, E), lambda h: (h, 0, 0))
    return pl.pallas_call(body, grid=(H,), in_specs=[spec] * 3, out_specs=[spec] * 2,
                          out_shape=[SDS((H, S, E), f32)] * 2, name=name, compiler_params=_params("parallel"))(q, k, v)


def _attn_bwd(q, k, v, do, o, lse, dlse, nb, name):
    H, S, E = q.shape

    def body(q_ref, k_ref, v_ref, do_ref, o_ref, l_ref, dl_ref, dq_ref, dk_ref, dv_ref):
        dk_ref[...] = jnp.zeros_like(dk_ref)
        dv_ref[...] = jnp.zeros_like(dv_ref)

        def blk(j, carry):
            st = pl.multiple_of(j * 128, 128)
            pst = pl.multiple_of(jnp.maximum(j - 1, 0) * 128, 128)
            cur, prev = pl.ds(st, 128), pl.ds(pst, 128)
            qb = q_ref[0, cur, :]
            kc, kp, vc, vp = k_ref[0, cur, :], k_ref[0, prev, :], v_ref[0, cur, :], v_ref[0, prev, :]
            dob = do_ref[0, cur, :]
            lse_b = l_ref[0, cur, :][:, :1]
            corr = jnp.sum(dl_ref[0, cur, :] - dob * o_ref[0, cur, :], axis=1, keepdims=True)
            mc, mp = _band_masks(j, nb)
            pc = jnp.exp(jnp.where(mc, _nt(qb, kc), NEG) - lse_b)
            pp = jnp.exp(jnp.where(mp, _nt(qb, kp), NEG) - lse_b)
            dob = dob.astype(bf16)
            dsc = (pc * (_nt(dob, vc) + corr)).astype(bf16)
            dsp = (pp * (_nt(dob, vp) + corr)).astype(bf16)
            dq_ref[0, cur, :] = (jnp.dot(dsc, kc, preferred_element_type=f32)
                                 + jnp.dot(dsp, kp, preferred_element_type=f32))
            dv_ref[0, cur, :] += _tn(pc.astype(bf16), dob)
            dk_ref[0, cur, :] += _tn(dsc, qb)
            dv_ref[0, prev, :] += _tn(pp.astype(bf16), dob)
            dk_ref[0, prev, :] += _tn(dsp, qb)
            return carry

        lax.fori_loop(0, S // 128, blk, 0)

    spec = pl.BlockSpec((1, S, E), lambda h: (h, 0, 0))
    return pl.pallas_call(body, grid=(H,), in_specs=[spec] * 7, out_specs=[spec] * 3,
                          out_shape=[SDS((H, S, E), f32)] * 3, name=name,
                          compiler_params=_params("parallel"))(q, k, v, do, o, lse, dlse)


def _perm(t, d):
    S = t.shape[0]
    return t.reshape(S // d, d, N_HEADS_A, HEAD_DIM).transpose(2, 1, 0, 3).reshape(N_HEADS_A, S, HEAD_DIM)


def _unperm(t, d):
    S = t.shape[1]
    return t.reshape(N_HEADS_A, d, S // d, HEAD_DIM).transpose(2, 1, 0, 3).reshape(S, ATTN_W)


def _conv_val(x, w, K, rows):
    acc = x * w[K - 1:K, :]
    for s in range(1, K):
        acc = acc + jnp.where(rows >= s, pltpu.roll(x, s, 0), 0.0) * w[K - 1 - s:K - s, :]
    return acc


def _colconv_fwd(xs, ws, bs, K, fn, nblk, tc, outs, name):
    S = xs[0][0].shape[0]
    n = len(xs)
    has_b = bs is not None

    def body(*refs):
        rows = lax.broadcasted_iota(jnp.int32, (S, tc), 0)
        cs = []
        for k in range(n):
            c = _conv_val(refs[k][...], refs[n + k][...], K, rows)
            if has_b:
                c = c + refs[2 * n + k][...]
            cs.append(c)
        for o_ref, val in zip(refs[(3 if has_b else 2) * n:], fn(*cs)):
            o_ref[...] = val.astype(o_ref.dtype)

    def cspec(rows_, cb0):
        return pl.BlockSpec((rows_, tc), lambda j, cb0=cb0: (0, cb0 + j))

    in_specs = [cspec(S, cb) for _, cb in xs] + [cspec(K, cb) for _, cb in ws]
    args = [a for a, _ in xs] + [a for a, _ in ws]
    if has_b:
        in_specs += [cspec(1, cb) for _, cb in bs]
        args += [a for a, _ in bs]
    return pl.pallas_call(
        body, grid=(nblk,), in_specs=in_specs, out_specs=[cspec(S, 0) for _ in outs],
        out_shape=[SDS((S, nblk * tc), dt) for dt in outs], name=name, compiler_params=_params("parallel"))(*args)


def _colconv_bwd(xs, ws, bs, K, fn, douts, nblk, tc, name):
    S = xs[0][0].shape[0]
    n, nd = len(xs), len(douts)
    has_b = bs is not None
    nin = (3 if has_b else 2) * n

    def body(*refs):
        rows = lax.broadcasted_iota(jnp.int32, (S, tc), 0)
        x = [refs[k][...] for k in range(n)]
        w = [refs[n + k][...] for k in range(n)]
        cs = []
        for k in range(n):
            c = _conv_val(x[k], w[k], K, rows)
            if has_b:
                c = c + refs[2 * n + k][...]
            cs.append(c)
        _, vjp = jax.vjp(fn, *cs)
        dcs = vjp(tuple(r[...].astype(f32) for r in refs[nin:nin + nd]))
        o = refs[nin + nd:]
        for k in range(n):
            dc = dcs[k]
            dx = dc * w[k][K - 1:K, :]
            o[n + k][K - 1:K, :] = jnp.sum(dc * x[k], axis=0, keepdims=True)
            for s in range(1, K):
                dx = dx + jnp.where(rows < S - s, pltpu.roll(dc, S - s, 0), 0.0) * w[k][K - 1 - s:K - s, :]
                xsh = jnp.where(rows >= s, pltpu.roll(x[k], s, 0), 0.0)
                o[n + k][K - 1 - s:K - s, :] = jnp.sum(dc * xsh, axis=0, keepdims=True)
            o[k][...] = dx
            if has_b:
                o[2 * n + k][...] = jnp.sum(dc, axis=0, keepdims=True)

    def cspec(rows_, cb0):
        return pl.BlockSpec((rows_, tc), lambda j, cb0=cb0: (0, cb0 + j))

    in_specs = [cspec(S, cb) for _, cb in xs] + [cspec(K, cb) for _, cb in ws]
    args = [a for a, _ in xs] + [a for a, _ in ws]
    if has_b:
        in_specs += [cspec(1, cb) for _, cb in bs]
        args += [a for a, _ in bs]
    in_specs += [cspec(S, 0) for _ in douts]
    args += list(douts)
    W = nblk * tc
    out_specs = [cspec(S, 0)] * n + [cspec(K, 0)] * n + ([cspec(1, 0)] * n if has_b else [])
    out_shape = [SDS((S, W), f32)] * n + [SDS((K, W), f32)] * n + ([SDS((1, W), f32)] * n if has_b else [])
    res = pl.pallas_call(body, grid=(nblk,), in_specs=in_specs, out_specs=out_specs, out_shape=out_shape,
                         name=name, compiler_params=_params("parallel"))(*args)
    return res[:n], res[n:2 * n], res[2 * n:]


def _silu_fn(c):
    return (c * jax.nn.sigmoid(c),)


def _geglu_fn(gate, up):
    gelu = 0.5 * gate * (1.0 + jnp.tanh(0.7978845608028654 * (gate + 0.044715 * gate * gate * gate)))
    return (gelu * up,)


def _softplus(x):
    u = jnp.exp(jnp.minimum(x, 20.0))
    small = u * (1.0 - 0.5 * u)
    return jnp.where(x > 20.0, x, jnp.where(u < 1e-4, small, jnp.log(1.0 + u)))


def _bdot(a, b):
    return jnp.dot(a.astype(bf16), b.astype(bf16), preferred_element_type=f32)


def _hdot(a, b):
    return jnp.dot(a, b, precision=HI, preferred_element_type=f32)


def _unit_lower_inverse(A):
    n = A.shape[0]
    eye = (lax.broadcasted_iota(jnp.int32, (n, n), 0) == lax.broadcasted_iota(jnp.int32, (n, n), 1)).astype(f32)
    P = -A
    T = eye + P
    for _ in range(5):
        P = _hdot(P, P)
        T = T + _hdot(T, P)
    return T


def _dn_chunk(qc, kc, vc, zc, bac, alog, dtb, nw, St, h):
    C = CHUNK
    lane = lax.broadcasted_iota(jnp.int32, (1, 128), 1)

    def sel(arr, idx):
        return jnp.sum(jnp.where(lane == idx, arr, 0.0), axis=1, keepdims=True)

    beta = jax.nn.sigmoid(sel(bac, h))
    g = -jnp.exp(sel(alog, h)) * _softplus(sel(bac, N_HEADS_D + h) + sel(dtb, h))
    qn = qc * lax.rsqrt(jnp.sum(qc * qc, axis=-1, keepdims=True) + EPS) * (DK ** -0.5)
    kn = kc * lax.rsqrt(jnp.sum(kc * kc, axis=-1, keepdims=True) + EPS)
    ii = lax.broadcasted_iota(jnp.int32, (C, C), 0)
    jj = lax.broadcasted_iota(jnp.int32, (C, C), 1)
    tril, strict = ii >= jj, ii > jj
    gsq = jnp.broadcast_to(g, (C, C))
    gcol = _hdot(tril.astype(f32), gsq)
    grow = _hdot(jnp.ones((C, C), f32), jnp.where(ii <= jj, gsq, 0.0))
    decay = jnp.exp(jnp.where(tril, gcol - grow, NEG))
    gc = gcol[:, :1]
    glast = gcol[C - 1:C, :1]
    kb = kn * beta
    A = jnp.where(strict, _nt(kb.astype(bf16), kn.astype(bf16)) * decay, 0.0)
    T = _unit_lower_inverse(A)
    u = _bdot(T, vc * beta)
    w = _bdot(T, kb * jnp.exp(gc))
    qk = _nt(qn.astype(bf16), kn.astype(bf16)) * decay
    qd = qn * jnp.exp(gc)
    kd = kn * jnp.exp(glast - gc)
    vnew = u - _bdot(w, St)
    o = _bdot(qd, St) + _bdot(qk, vnew)
    Snew = St * jnp.exp(glast) + _tn(kd.astype(bf16), vnew.astype(bf16))
    out = _rms(o, nw) * (zc * jax.nn.sigmoid(zc))
    return out, Snew


def _dn_specs(S):
    def col(off):
        return pl.BlockSpec((S, 128), lambda h, off=off: (0, off + h))

    par = pl.BlockSpec((1, 128), lambda h: (0, 0))
    return [col(0), col(N_HEADS_D), col(2 * N_HEADS_D), col(3072 // 128),
            pl.BlockSpec((S, 128), lambda h: (0, 3584 // 128)), par, par, par]


def _dn_fwd(qkv, proj, alog, dtb, nw, name):
    S = qkv.shape[0]

    def body(q_ref, k_ref, v_ref, z_ref, ba_ref, al_ref, dt_ref, nw_ref, o_ref):
        h = pl.program_id(0)

        def step(n, St):
            r = pl.ds(pl.multiple_of(n * CHUNK, CHUNK), CHUNK)
            out, Sn = _dn_chunk(q_ref[r, :], k_ref[r, :], v_ref[r, :], z_ref[r, :], ba_ref[r, :],
                                al_ref[...], dt_ref[...], nw_ref[...], St, h)
            o_ref[r, :] = out.astype(o_ref.dtype)
            return Sn

        lax.fori_loop(0, S // CHUNK, step, jnp.zeros((DK, DK), f32))

    return pl.pallas_call(
        body, grid=(N_HEADS_D,), in_specs=_dn_specs(S), out_specs=pl.BlockSpec((S, 128), lambda h: (0, h)),
        out_shape=SDS((S, N_HEADS_D * DK), bf16), name=name,
        compiler_params=_params("arbitrary"))(qkv, qkv, qkv, proj, proj, alog, dtb, nw)


def _dn_bwd(qkv, proj, alog, dtb, nw, dmix_in, name):
    S = qkv.shape[0]
    NCH = S // CHUNK

    def body(q_ref, k_ref, v_ref, z_ref, ba_ref, al_ref, dt_ref, nw_ref, do_ref,
             dq_ref, dk_ref, dv_ref, dz_ref, dba_ref, dal_ref, ddt_ref, dnw_ref, st_ref):
        h = pl.program_id(0)

        @pl.when(h == 0)
        def _():
            dba_ref[...] = jnp.zeros_like(dba_ref)
            dal_ref[...] = jnp.zeros_like(dal_ref)
            ddt_ref[...] = jnp.zeros_like(ddt_ref)
            dnw_ref[...] = jnp.zeros_like(dnw_ref)

        def chunk(n, St):
            r = pl.ds(pl.multiple_of(n * CHUNK, CHUNK), CHUNK)
            return (q_ref[r, :], k_ref[r, :], v_ref[r, :], z_ref[r, :], ba_ref[r, :],
                    al_ref[...], dt_ref[...], nw_ref[...], St), r

        def fwd(n, St):
            st_ref[n] = St
            args, _ = chunk(n, St)
            return _dn_chunk(*args, h)[1]

        lax.fori_loop(0, NCH, fwd, jnp.zeros((DK, DK), f32))

        def bwd(t, carry):
            dS, dal, ddt, dnw = carry
            n = NCH - 1 - t
            args, r = chunk(n, st_ref[n])
            _, vjp = jax.vjp(lambda *a: _dn_chunk(*a, h), *args)
            g = vjp((do_ref[r, :], dS))
            dq_ref[r, :] = g[0]
            dk_ref[r, :] = g[1]
            dv_ref[r, :] = g[2]
            dz_ref[r, :] = g[3]
            dba_ref[r, :] += g[4]
            return g[8], dal + g[5], ddt + g[6], dnw + g[7]

        zero = jnp.zeros((1, 128), f32)
        _, dal, ddt, dnw = lax.fori_loop(0, NCH, bwd, (jnp.zeros((DK, DK), f32), zero, zero, zero))
        dal_ref[...] += dal
        ddt_ref[...] += ddt
        dnw_ref[...] += dnw

    hcol = pl.BlockSpec((S, 128), lambda h: (0, h))
    whole = pl.BlockSpec((S, 128), lambda h: (0, 0))
    par = pl.BlockSpec((1, 128), lambda h: (0, 0))
    W = N_HEADS_D * DK
    return pl.pallas_call(
        body, grid=(N_HEADS_D,),
        in_specs=_dn_specs(S) + [pl.BlockSpec((S, 128), lambda h: (0, ATTN_W // 128 + h))],
        out_specs=[hcol] * 4 + [whole, par, par, par],
        out_shape=[SDS((S, W), f32)] * 4 + [SDS((S, 128), f32)] + [SDS((1, 128), f32)] * 3,
        scratch_shapes=[pltpu.VMEM((NCH, DK, DK), f32)], name=name,
        compiler_params=_params("arbitrary"))(qkv, qkv, qkv, proj, proj, alog, dtb, nw, dmix_in)


def _loss_head(y, t, name):
    S, D = y.shape
    tm = ROW_TILE

    def body(y_ref, t_ref, dy_ref, l_ref):
        i = pl.program_id(0)
        d = y_ref[...] - t_ref[...]
        dy_ref[...] = d * (1.0 / D)
        part = jnp.sum(jnp.sum(d * d, axis=1, keepdims=True), axis=0, keepdims=True) * (0.5 / D)

        @pl.when(i == 0)
        def _():
            l_ref[...] = jnp.zeros_like(l_ref)

        l_ref[...] += jnp.broadcast_to(part, l_ref.shape)

    spec = pl.BlockSpec((tm, D), lambda i: (i, 0))
    dy, l = pl.pallas_call(body, grid=(S // tm,), in_specs=[spec, spec],
                           out_specs=[spec, pl.BlockSpec((1, 128), lambda i: (0, 0))],
                           out_shape=[SDS((S, D), f32), SDS((1, 128), f32)], name=name,
                           compiler_params=_params("arbitrary"))(y, t)
    return l[0, 0], dy


def _adamw(w, g, m, v, tr, name):
    L, R, C = w.shape
    assert R % tr == 0

    def body(w_ref, g_ref, m_ref, v_ref, d_ref, mo_ref, vo_ref):
        gv = g_ref[...]
        m2 = ADAM_B1 * m_ref[...] + (1.0 - ADAM_B1) * gv
        v2 = ADAM_B2 * v_ref[...] + (1.0 - ADAM_B2) * (gv * gv)
        m_hat = m2 / (1.0 - ADAM_B1 ** ADAM_STEP)
        v_hat = v2 / (1.0 - ADAM_B2 ** ADAM_STEP)
        d_ref[...] = -ADAM_LR * (m_hat / (jnp.sqrt(v_hat) + ADAM_EPS) + ADAM_WD * w_ref[...])
        mo_ref[...] = m2
        vo_ref[...] = v2

    spec = pl.BlockSpec((1, tr, C), lambda l, i: (l, i, 0))
    return pl.pallas_call(body, grid=(L, R // tr), in_specs=[spec] * 4, out_specs=[spec] * 3,
                          out_shape=[SDS((L, R, C), f32)] * 3, name=name,
                          compiler_params=_params("parallel", "parallel"))(w, g, m, v)


def _rope_tables(S):
    inv = 1.0 / (10000.0 ** (jnp.arange(0, HEAD_DIM, 2, dtype=f32) / HEAD_DIM))
    ang = jnp.arange(S, dtype=f32)[:, None] * inv[None, :]
    cos, sin = jnp.cos(ang), jnp.sin(ang)
    return (jnp.tile(jnp.concatenate([cos, cos], axis=1), (1, N_HEADS_A)),
            jnp.tile(jnp.concatenate([-sin, sin], axis=1), (1, N_HEADS_A)))


def _layer_fwd(x, W, cos, sgn_sin, l):
    n = f"l{l}_"
    (h1,) = _rows(_rms_fn, [x], [W["norm_pre_mix"]], [(D_MODEL, bf16)], n + "pre_mix_norm")
    proj = _mm(h1, W["w_in"], "nn", 512, 768, f32, n + "in_proj")
    qr, kr = _rows(_rope_fwd_fn, [(proj, ATTN_W, 0), (proj, ATTN_W, 1), cos, sgn_sin], [],
                   [(ATTN_W, bf16), (ATTN_W, bf16)], n + "rope")
    vb = proj[:, 2 * ATTN_W:3 * ATTN_W].astype(bf16)
    br = []
    for b, (d, nb) in enumerate(BRANCHES):
        qd, kd, vd = _perm(qr, d), _perm(kr, d), _perm(vb, d)
        o, lse = _attn_fwd(qd, kd, vd, nb, n + f"attn_fwd{b}")
        br.append(dict(q=qd, k=kd, v=vd, o=o, lse=lse, O=_unperm(o, d), L=_unperm(lse, d)))
    comb_in = [b_["O"] for b_ in br] + [b_["L"] for b_ in br]
    (attn_out,) = _rows(_combine_fn, comb_in, [], [(ATTN_W, bf16)], n + "attn_combine")
    (qkv,) = _colconv_fwd([(proj, 3)], [(W["dn_conv_w"], 0)], None, 4, _silu_fn, 3, 512, [f32], n + "dn_conv")
    dn_out = _dn_fwd(qkv, proj, W["dn_a_log"], W["dn_dt_bias"], W["dn_norm_w"], n + "dn_fwd")
    mix_in = jnp.concatenate([attn_out, dn_out], axis=1)
    mix = _mm(mix_in, W["w_out"], "nn", 512, 512, f32, n + "out_proj")
    (x1,) = _rows(_res_rms_fn, [mix, x], [W["norm_post_mix"]], [(D_MODEL, f32)], n + "post_mix_norm")
    (h2,) = _rows(_rms_fn, [x1], [W["norm_pre_ffn"]], [(D_MODEL, bf16)], n + "pre_ffn_norm")
    u0 = _mm(h2, W["ffn_w_in"], "nn", 512, 512, f32, n + "ffn_in")
    nb_ff = D_FF // 256
    (act,) = _colconv_fwd([(u0, 0), (u0, nb_ff)], [(W["ffn_conv_w"], 0), (W["ffn_conv_w"], nb_ff)],
                          [(W["ffn_conv_b"], 0), (W["ffn_conv_b"], nb_ff)], 3, _geglu_fn, nb_ff, 256, [bf16],
                          n + "ffn_conv_glu")
    f = _mm(act, W["ffn_w_out"], "nn", 512, 512, f32, n + "ffn_out")
    (x2,) = _rows(_res_rms_fn, [f, x1], [W["norm_post_ffn"]], [(D_MODEL, f32)], n + "post_ffn_norm")
    saved = dict(x=x, h1=h1, proj=proj, br=br, comb_in=comb_in, qkv=qkv, mix_in=mix_in, mix=mix, x1=x1, h2=h2,
                 u0=u0, act=act, f=f)
    return x2, saved


def _layer_bwd(dx2, sv, W, cos, sgn_sin, l):
    n = f"l{l}_"
    S = dx2.shape[0]
    g = {}
    (df,), (g["norm_post_ffn"],) = _rows_vjp(_rms_fn, [sv["f"]], [W["norm_post_ffn"]], [dx2], [0], [0],
                                             n + "post_ffn_norm_bwd")
    dact = _mm(df, W["ffn_w_out"], "nt", 512, 1408, f32, n + "ffn_out_dx")
    g["ffn_w_out"] = _mm(sv["act"], df, "tn", 256, 1024, f32, n + "ffn_out_dw")
    nb_ff = D_FF // 256
    u0 = sv["u0"]
    dxs, dws, dbs = _colconv_bwd([(u0, 0), (u0, nb_ff)], [(W["ffn_conv_w"], 0), (W["ffn_conv_w"], nb_ff)],
                                 [(W["ffn_conv_b"], 0), (W["ffn_conv_b"], nb_ff)], 3, _geglu_fn, [dact], nb_ff, 256,
                                 n + "ffn_conv_glu_bwd")
    du0 = jnp.concatenate(dxs, axis=1)
    g["ffn_conv_w"] = jnp.concatenate(dws, axis=1)
    g["ffn_conv_b"] = jnp.concatenate(dbs, axis=1)
    dh2 = _mm(du0, W["ffn_w_in"], "nt", 256, 512, f32, n + "ffn_in_dx")
    g["ffn_w_in"] = _mm(sv["h2"], du0, "tn", 512, 512, f32, n + "ffn_in_dw")
    (dx1,), (g["norm_pre_ffn"],) = _rows_vjp(_rms_fn, [sv["x1"]], [W["norm_pre_ffn"]], [dh2], [0], [0],
                                             n + "pre_ffn_norm_bwd", adds={0: dx2})
    (dmix,), (g["norm_post_mix"],) = _rows_vjp(_rms_fn, [sv["mix"]], [W["norm_post_mix"]], [dx1], [0], [0],
                                               n + "post_mix_norm_bwd")
    dmix_in = _mm(dmix, W["w_out"], "nt", 512, 512, f32, n + "out_proj_dx")
    g["w_out"] = _mm(sv["mix_in"], dmix, "tn", 512, 512, f32, n + "out_proj_dw")

    dq, dk, dv, dz, dba, g["dn_a_log"], g["dn_dt_bias"], g["dn_norm_w"] = _dn_bwd(
        sv["qkv"], sv["proj"], W["dn_a_log"], W["dn_dt_bias"], W["dn_norm_w"], dmix_in, n + "dn_bwd")
    dqkv = jnp.concatenate([dq, dk, dv], axis=1)
    (dqkv0,), (g["dn_conv_w"],), _ = _colconv_bwd([(sv["proj"], 3)], [(W["dn_conv_w"], 0)], None, 4, _silu_fn,
                                                 [dqkv], 3, 512, n + "dn_conv_bwd")

    dcomb, _ = _rows_vjp(_combine_fn, sv["comb_in"], [], [(dmix_in, ATTN_W, 0)], [0, 1, 2, 3, 4, 5], [],
                         n + "attn_combine_bwd")
    parts = []
    for b, (d, nb) in enumerate(BRANCHES):
        s = sv["br"][b]
        parts.append(_attn_bwd(s["q"], s["k"], s["v"], _perm(dcomb[b], d), s["o"], s["lse"], _perm(dcomb[3 + b], d),
                               nb, n + f"attn_bwd{b}"))
    grads_qkv = [_unperm(parts[b][i], BRANCHES[b][0]) for i in range(3) for b in range(3)]
    daq, dak, dav = _rows(_rope_bwd_fn, grads_qkv + [cos, sgn_sin], [], [(ATTN_W, f32)] * 3, n + "rope_bwd")
    dproj = jnp.concatenate([daq, dak, dav, dqkv0, dz, dba, jnp.zeros((S, PROJ_W - 3712), f32)], axis=1)
    dh1 = _mm(dproj, W["w_in"], "nt", 256, 512, f32, n + "in_proj_dx")
    g["w_in"] = _mm(sv["h1"], dproj, "tn", 512, 768, f32, n + "in_proj_dw")[:, :IN_COLS]
    (dx,), (g["norm_pre_mix"],) = _rows_vjp(_rms_fn, [sv["x"]], [W["norm_pre_mix"]], [dh1], [0], [0],
                                            n + "pre_mix_norm_bwd", adds={0: dx1})
    return dx, g


def _local_step(x, target, layers):
    cos, sgn_sin = _rope_tables(x.shape[0])
    saved = []
    for l, W in enumerate(layers):
        x, sv = _layer_fwd(x, W, cos, sgn_sin, l)
        saved.append(sv)
    loss, dx = _loss_head(x, target, "loss_head")
    grads = [None] * len(layers)
    for l in reversed(range(len(layers))):
        dx, grads[l] = _layer_bwd(dx, saved[l], layers[l], cos, sgn_sin, l)
    return loss, dx, grads


def _pos():
    x, y, c = lax.axis_index("x"), lax.axis_index("y"), lax.axis_index("c")
    return x, y, c, [(1 - x, y), (x, 1 - y), (1 - x, 1 - y)]


def _rcopy(src, dst, send_sem, recv_sem, dev):
    return pltpu.make_async_remote_copy(src_ref=src, dst_ref=dst, send_sem=send_sem, recv_sem=recv_sem,
                                        device_id=dev, device_id_type=MESH)


def _half_rows(ref, h, which, axis):
    if h is None:
        return ref
    rows = pl.ds(pl.multiple_of(which * h, 16), h)
    return ref.at[:, rows, :] if axis == 1 else ref.at[rows, :]


def _dma_sems(*counts):
    return [pltpu.SemaphoreType.DMA((k,)) for k in counts]


def _all_gather(arrs, halves, name):
    n = len(arrs)

    def body(*refs):
        ins, outs = refs[:n], refs[n:2 * n]
        send1, recv1, send2, recv2, lsem = refs[2 * n:]
        x, y, c, chips = _pos()
        me, sib, s_me = (x, y, c), (x, y, 1 - c), 2 * x + y
        locs = [pltpu.make_async_copy(ins[i], outs[i].at[s_me], lsem.at[i]) for i in range(n)]
        for cp in locs:
            cp.start()
        sends = []
        for i in range(n):
            for j, chip in enumerate(chips):
                cp = _rcopy(_half_rows(ins[i], halves[i], c, 1), _half_rows(outs[i].at[s_me], halves[i], c, 1),
                            send1.at[3 * i + j], recv1.at[3 * i + j], (*chip, c))
                cp.start()
                sends.append(cp)
        for i in range(n):
            for j, (px, py) in enumerate(chips):
                k = 3 * i + j
                landed = _half_rows(outs[i].at[2 * px + py], halves[i], c, 1)
                _rcopy(landed, landed, send1.at[k], recv1.at[k], me).wait_recv()
                if halves[i] is not None:
                    cp = _rcopy(landed, landed, send2.at[k], recv2.at[k], sib)
                    cp.start()
                    sends.append(cp)
        for i in range(n):
            if halves[i] is None:
                continue
            for j, (px, py) in enumerate(chips):
                k = 3 * i + j
                other = _half_rows(outs[i].at[2 * px + py], halves[i], 1 - c, 1)
                _rcopy(other, other, send2.at[k], recv2.at[k], me).wait_recv()
        for cp in sends:
            cp.wait_send()
        for cp in locs:
            cp.wait()

    return pl.pallas_call(
        body, in_specs=[ANY] * n, out_specs=[ANY] * n,
        out_shape=[SDS((4,) + a.shape, a.dtype) for a in arrs],
        scratch_shapes=_dma_sems(3 * n, 3 * n, 3 * n, 3 * n, n), name=name)(*arrs)


def _exchange_halves(gs, name):
    n = len(gs)

    def body(*refs):
        ins, outs = refs[:n], refs[n:2 * n]
        send, recv = refs[2 * n:]
        x, y, c, _ = _pos()
        sends = []
        for k in range(n):
            cp = _rcopy(_half_rows(ins[k], gs[k].shape[1] // 2, 1 - c, 1), outs[k], send.at[k], recv.at[k], (x, y, 1 - c))
            cp.start()
            sends.append(cp)
        for k in range(n):
            _rcopy(outs[k], outs[k], send.at[k], recv.at[k], (x, y, c)).wait_recv()
        for cp in sends:
            cp.wait_send()

    return pl.pallas_call(
        body, in_specs=[ANY] * n, out_specs=[ANY] * n,
        out_shape=[SDS((4, g.shape[1] // 2, g.shape[2]), g.dtype) for g in gs],
        scratch_shapes=_dma_sems(n, n), name=name)(*gs)


def _scatter_partials(ps, name):
    n = len(ps)

    def body(*refs):
        ins, outs = refs[:n], refs[n:2 * n]
        send, recv = refs[2 * n:]
        x, y, c, chips = _pos()
        sends = []
        for k in range(n):
            for j, (px, py) in enumerate(chips):
                cp = _rcopy(ins[k].at[2 * px + py], outs[k].at[j], send.at[3 * k + j], recv.at[3 * k + j], (px, py, c))
                cp.start()
                sends.append(cp)
        for k in range(n):
            for j in range(3):
                _rcopy(outs[k].at[j], outs[k].at[j], send.at[3 * k + j], recv.at[3 * k + j], (x, y, c)).wait_recv()
        for cp in sends:
            cp.wait_send()

    return pl.pallas_call(
        body, in_specs=[ANY] * n, out_specs=[ANY] * n,
        out_shape=[SDS((3,) + p.shape[1:], p.dtype) for p in ps],
        scratch_shapes=_dma_sems(3 * n, 3 * n), name=name)(*ps)


def _join_halves(rs, name):
    n = len(rs)

    def body(*refs):
        ins, outs = refs[:n], refs[n:2 * n]
        send, recv, lsem = refs[2 * n:]
        x, y, c, _ = _pos()
        started = []
        for k in range(n):
            h = rs[k].shape[0]
            mine = _half_rows(outs[k], h, c, 0)
            loc = pltpu.make_async_copy(ins[k], mine, lsem.at[k])
            loc.start()
            cp = _rcopy(ins[k], mine, send.at[k], recv.at[k], (x, y, 1 - c))
            cp.start()
            started.append((loc, cp))
        for k in range(n):
            other = _half_rows(outs[k], rs[k].shape[0], 1 - c, 0)
            _rcopy(other, other, send.at[k], recv.at[k], (x, y, c)).wait_recv()
        for loc, cp in started:
            cp.wait_send()
            loc.wait()

    return pl.pallas_call(
        body, in_specs=[ANY] * n, out_specs=[ANY] * n,
        out_shape=[SDS((2 * r.shape[0], r.shape[1]), r.dtype) for r in rs],
        scratch_shapes=_dma_sems(n, n, n), name=name)(*rs)


def _all_reduce_small(pack, name):
    R = pack.shape[0]

    def body(in_ref, out_ref, buf, send, recv):
        x, y, c, _ = _pos()
        me = 4 * x + 2 * y + c
        buf[me] = in_ref[...]
        sends = []
        for k in range(1, 8):
            peer = me ^ k
            cp = _rcopy(buf.at[me], buf.at[me], send.at[k - 1], recv.at[k - 1], ((peer >> 2) & 1, (peer >> 1) & 1, peer & 1))
            cp.start()
            sends.append(cp)
        for k in range(1, 8):
            _rcopy(buf.at[me ^ k], buf.at[me ^ k], send.at[k - 1], recv.at[k - 1], (x, y, c)).wait_recv()
        for cp in sends:
            cp.wait_send()
        acc = buf[0]
        for d in range(1, 8):
            acc = acc + buf[d]
        out_ref[...] = acc

    return pl.pallas_call(
        body, out_shape=SDS((R, 128), f32),
        in_specs=[pl.BlockSpec(memory_space=pltpu.VMEM)], out_specs=pl.BlockSpec(memory_space=pltpu.VMEM),
        scratch_shapes=[pltpu.VMEM((8, R, 128), f32)] + _dma_sems(7, 7), name=name)(pack)


def _add_sibling(g, recv, c_arr, tr, name):
    _, R, C = g.shape
    h = R // 2
    nrb = h // tr
    assert h % tr == 0

    def body(c_ref, g_ref, r_ref, o_ref):
        o_ref[...] = (g_ref[...] + r_ref[...]).astype(o_ref.dtype)

    spec = pl.BlockSpec((1, tr, C), lambda s, r, c_ref: (s, r, 0))
    grid_spec = pltpu.PrefetchScalarGridSpec(
        num_scalar_prefetch=1, grid=(4, nrb),
        in_specs=[pl.BlockSpec((1, tr, C), lambda s, r, c_ref: (s, c_ref[0] * nrb + r, 0)), spec], out_specs=spec)
    return pl.pallas_call(body, grid_spec=grid_spec, out_shape=SDS((4, h, C), bf16), name=name,
                          compiler_params=_params("parallel", "parallel"))(c_arr, g, recv)


def _add_chips(p, recv, s_arr, tr, name):
    _, h, C = p.shape
    assert h % tr == 0

    def body(s_ref, p_ref, r_ref, o_ref):
        o_ref[...] = (p_ref[0].astype(f32) + r_ref[0].astype(f32)) + (r_ref[1].astype(f32) + r_ref[2].astype(f32))

    grid_spec = pltpu.PrefetchScalarGridSpec(
        num_scalar_prefetch=1, grid=(h // tr,),
        in_specs=[pl.BlockSpec((1, tr, C), lambda r, s_ref: (s_ref[0], r, 0)),
                  pl.BlockSpec((3, tr, C), lambda r, s_ref: (0, r, 0))],
        out_specs=pl.BlockSpec((tr, C), lambda r, s_ref: (r, 0)))
    return pl.pallas_call(body, grid_spec=grid_spec, out_shape=SDS((h, C), f32), name=name,
                          compiler_params=_params("parallel"))(s_arr, p, recv)


_BIG = (("w_in", 1024, 256), ("w_out", 256, 128), ("ffn_w_in", 1024, 256), ("ffn_w_out", 704, 352))
_SMALL = ("dn_conv_w", "ffn_conv_w", "ffn_conv_b", "norm_pre_mix", "norm_post_mix", "norm_pre_ffn", "norm_post_ffn",
          "dn_norm_w", "dn_a_log", "dn_dt_bias")
_WEIGHTS = ("w_in", "dn_conv_w", "dn_a_log", "dn_dt_bias", "dn_norm_w", "w_out", "ffn_w_in", "ffn_conv_w", "ffn_conv_b",
            "ffn_w_out", "norm_pre_mix", "norm_post_mix", "norm_pre_ffn", "norm_post_ffn")
_ADAM_ROWS = {"w_in": 256, "w_out": 256, "ffn_w_in": 128, "ffn_w_out": 176}


def _shard_major(name, g):
    if name == "w_in":
        return jnp.stack([g[:, 898 * s:898 * (s + 1)] for s in range(4)])
    if name == "ffn_w_in":
        return g.reshape(1024, 4, 1408).transpose(1, 0, 2)
    return g.reshape(4, g.shape[0] // 4, g.shape[1])


def kernel(x, w_in, dn_conv_w, dn_a_log, dn_dt_bias, dn_norm_w, w_out, ffn_w_in, ffn_conv_w, ffn_conv_b, ffn_w_out, norm_pre_mix, norm_post_mix, norm_pre_ffn, norm_post_ffn, loss_target, m_w_in, m_dn_conv_w, m_dn_a_log, m_dn_dt_bias, m_dn_norm_w, m_w_out, m_ffn_w_in, m_ffn_conv_w, m_ffn_conv_b, m_ffn_w_out, m_norm_pre_mix, m_norm_post_mix, m_norm_pre_ffn, m_norm_post_ffn, v_w_in, v_dn_conv_w, v_dn_a_log, v_dn_dt_bias, v_dn_norm_w, v_w_out, v_ffn_w_in, v_ffn_conv_w, v_ffn_conv_b, v_ffn_w_out, v_norm_pre_mix, v_norm_post_mix, v_norm_pre_ffn, v_norm_post_ffn):
    w = dict(w_in=w_in, dn_conv_w=dn_conv_w, dn_a_log=dn_a_log, dn_dt_bias=dn_dt_bias, dn_norm_w=dn_norm_w, w_out=w_out,
             ffn_w_in=ffn_w_in, ffn_conv_w=ffn_conv_w, ffn_conv_b=ffn_conv_b, ffn_w_out=ffn_w_out, norm_pre_mix=norm_pre_mix,
             norm_post_mix=norm_post_mix, norm_pre_ffn=norm_pre_ffn, norm_post_ffn=norm_post_ffn)
    m = dict(w_in=m_w_in, dn_conv_w=m_dn_conv_w, dn_a_log=m_dn_a_log, dn_dt_bias=m_dn_dt_bias, dn_norm_w=m_dn_norm_w,
             w_out=m_w_out, ffn_w_in=m_ffn_w_in, ffn_conv_w=m_ffn_conv_w, ffn_conv_b=m_ffn_conv_b, ffn_w_out=m_ffn_w_out,
             norm_pre_mix=m_norm_pre_mix, norm_post_mix=m_norm_post_mix, norm_pre_ffn=m_norm_pre_ffn,
             norm_post_ffn=m_norm_post_ffn)
    v = dict(w_in=v_w_in, dn_conv_w=v_dn_conv_w, dn_a_log=v_dn_a_log, dn_dt_bias=v_dn_dt_bias, dn_norm_w=v_dn_norm_w,
             w_out=v_w_out, ffn_w_in=v_ffn_w_in, ffn_conv_w=v_ffn_conv_w, ffn_conv_b=v_ffn_conv_b, ffn_w_out=v_ffn_w_out,
             norm_pre_mix=v_norm_pre_mix, norm_post_mix=v_norm_post_mix, norm_pre_ffn=v_norm_pre_ffn,
             norm_post_ffn=v_norm_post_ffn)
    xi, yi, ci = lax.axis_index("x"), lax.axis_index("y"), lax.axis_index("c")
    s_me = 2 * xi + yi
    c_arr = jnp.reshape(ci, (1,)).astype(jnp.int32)
    s_arr = jnp.reshape(s_me, (1,)).astype(jnp.int32)

    gathered = _all_gather([w_in.astype(bf16), w_out.astype(bf16), ffn_w_in.astype(bf16), ffn_w_out.astype(bf16),
                            dn_conv_w, ffn_conv_w], [512, 128, 512, 352, None, None], "weights_all_gather")
    cat = lambda a: jnp.concatenate([a[s] for s in range(4)], axis=-1)
    rows = lambda a: jnp.concatenate([a[s] for s in range(4)], axis=1)
    full = dict(w_in=jnp.pad(cat(gathered[0]), ((0, 0), (0, 0), (0, PROJ_W - IN_COLS))), w_out=rows(gathered[1]),
                ffn_w_in=cat(gathered[2]), ffn_w_out=rows(gathered[3]), dn_conv_w=cat(gathered[4]),
                ffn_conv_w=cat(gathered[5]))
    lanes = lambda a: jnp.pad(a, ((0, 0), (0, 128 - a.shape[1])))
    full.update(dn_a_log=lanes(dn_a_log), dn_dt_bias=lanes(dn_dt_bias), dn_norm_w=dn_norm_w, ffn_conv_b=ffn_conv_b,
                norm_pre_mix=norm_pre_mix, norm_post_mix=norm_post_mix, norm_pre_ffn=norm_pre_ffn,
                norm_post_ffn=norm_post_ffn)
    layers = [{k: (a[l] if a.ndim == 3 else a[l:l + 1]) for k, a in full.items()} for l in range(2)]

    loss_local, dx, grads = _local_step(x[0], loss_target[0], layers)
    loss = lax.psum(loss_local, ("x", "y", "c"))

    keys = [(name, l) for name, _, _ in _BIG for l in range(2)]
    tiles = {name: tr for name, _, tr in _BIG}
    gs = [_shard_major(name, grads[l][name]) for name, l in keys]
    from_sib = _exchange_halves(gs, "grads_to_sibling")
    chip_part = [_add_sibling(g, r, c_arr, tiles[name], f"add_sibling_{name}{l}") for g, r, (name, l) in zip(gs, from_sib, keys)]
    from_chips = _scatter_partials(chip_part, "grads_to_owner_chip")
    reduced = [_add_chips(p, r, s_arr, tiles[name], f"add_chips_{name}{l}") for p, r, (name, l) in zip(chip_part, from_chips, keys)]
    joined = _join_halves(reduced, "grads_join_halves")
    g_out = {name: jnp.stack([joined[2 * i], joined[2 * i + 1]]) for i, (name, _, _) in enumerate(_BIG)}

    small = {}
    for name in _SMALL:
        per_layer = [grads[l][name] for l in range(2)]
        if name in ("dn_a_log", "dn_dt_bias"):
            per_layer = [p[:, :N_HEADS_D] for p in per_layer]
        small[name] = jnp.stack(per_layer).reshape((2,) + (w[name].shape[1:] if name not in ("dn_conv_w", "ffn_conv_w")
                                                           else per_layer[0].shape))
    flat = jnp.concatenate([small[name].reshape(-1) for name in _SMALL])
    n_rows = -(-flat.shape[0] // 1024) * 8
    summed = _all_reduce_small(jnp.pad(flat, (0, n_rows * 128 - flat.shape[0])).reshape(n_rows, 128),
                               "small_grads_all_reduce").reshape(-1)
    off = 0
    for name in _SMALL:
        size = small[name].size
        g_out[name] = summed[off:off + size].reshape(small[name].shape)
        off += size
    g_out["dn_conv_w"] = lax.dynamic_slice_in_dim(g_out["dn_conv_w"], s_me * 384, 384, axis=2)
    g_out["ffn_conv_w"] = lax.dynamic_slice_in_dim(g_out["ffn_conv_w"], s_me * 1408, 1408, axis=2)

    deltas, new_m, new_v = {}, {}, {}
    for name in _WEIGHTS:
        shape = w[name].shape
        as3 = (lambda a: a) if len(shape) == 3 else (lambda a: a.reshape(shape[0], 1, shape[1]))
        tr = _ADAM_ROWS.get(name, as3(w[name]).shape[1])
        d_, m_, v_ = _adamw(as3(w[name]), as3(g_out[name]), as3(m[name]), as3(v[name]), tr, f"adamw_{name}")
        deltas[name], new_m[name], new_v[name] = d_.reshape(shape), m_.reshape(shape), v_.reshape(shape)

    return (loss, dx[None], *[g_out[k] for k in _WEIGHTS], *[deltas[k] for k in _WEIGHTS],
            *[new_m[k] for k in _WEIGHTS], *[new_v[k] for k in _WEIGHTS])
```

```python
import jax
import jax.numpy as jnp
from jax import lax
from jax.experimental import pallas as pl
from jax.experimental.pallas import tpu as pltpu

f32, bf16 = jnp.float32, jnp.bfloat16
SDS = jax.ShapeDtypeStruct
HI = lax.Precision.HIGHEST
MESH = pl.DeviceIdType.MESH
ANY = pl.BlockSpec(memory_space=pl.ANY)

D_MODEL = 1024
N_HEADS_A, HEAD_DIM = 8, 64
ATTN_W = 512
N_HEADS_D, DK = 4, 128
CHUNK = 64
D_FF = 2816
IN_COLS = 3592
PROJ_W = 3840
BRANCHES = ((1, 16), (4, 4), (16, 1))
EPS = 1e-6
NEG = -1e30
ROW_TILE = 256
VMEM_LIMIT = 56 * 1024 * 1024

ADAM_LR, ADAM_B1, ADAM_B2, ADAM_EPS, ADAM_WD, ADAM_STEP = 0.001, 0.9, 0.999, 1e-08, 0.01, 10


def _params(*sem):
    return pltpu.CompilerParams(dimension_semantics=sem, vmem_limit_bytes=VMEM_LIMIT)


def _mm(a, b, mode, tm, tn, out_dtype, name, column_shards=False):
    if mode == "nn":
        (M, K), N = a.shape, b.shape[1]
        dims = (((1,), (0,)), ((), ()))
        a_spec = pl.BlockSpec((tm, K), lambda i, j: (i, 0))
        b_spec = pl.BlockSpec((K, tn), lambda i, j: (0, j))
    elif mode == "nt":
        (M, K), N = a.shape, b.shape[0]
        dims = (((1,), (1,)), ((), ()))
        a_spec = pl.BlockSpec((tm, K), lambda i, j: (i, 0))
        b_spec = pl.BlockSpec((tn, K), lambda i, j: (j, 0))
    else:
        (K, M), N = a.shape, b.shape[1]
        dims = (((0,), (0,)), ((), ()))
        a_spec = pl.BlockSpec((K, tm), lambda i, j: (0, i))
        b_spec = pl.BlockSpec((K, tn), lambda i, j: (0, j))
    assert M % tm == 0 and N % tn == 0, (name, M, N, tm, tn)

    def body(a_ref, b_ref, o_ref):
        o_ref[...] = lax.dot_general(a_ref[...].astype(bf16), b_ref[...].astype(bf16), dims,
                                     preferred_element_type=f32).astype(o_ref.dtype)

    if column_shards:
        out_spec, out_shape = pl.BlockSpec((None, tm, tn), lambda i, j: (j, i, 0)), SDS((N // tn, M, tn), out_dtype)
    else:
        out_spec, out_shape = pl.BlockSpec((tm, tn), lambda i, j: (i, j)), SDS((M, N), out_dtype)
    return pl.pallas_call(body, grid=(M // tm, N // tn), in_specs=[a_spec, b_spec], out_specs=out_spec,
                          out_shape=out_shape, name=name, compiler_params=_params("parallel", "arbitrary"))(a, b)


def _row_spec(r, tm):
    if isinstance(r, tuple):
        arr, width, cb = r
        return arr, pl.BlockSpec((tm, width), lambda i, cb=cb: (i, cb))
    return r, pl.BlockSpec((tm, r.shape[1]), lambda i: (i, 0))


def _full_spec(p):
    return pl.BlockSpec(p.shape, lambda i: (0,) * p.ndim)


def _rows(fn, rows, params, outs, name, tm=ROW_TILE):
    arrs, specs = zip(*[_row_spec(r, tm) for r in rows])
    S = arrs[0].shape[0]
    nr, npar = len(rows), len(params)

    def body(*refs):
        vals = fn(*[r[...].astype(f32) for r in refs[:nr]], *[p[...] for p in refs[nr:nr + npar]])
        for o_ref, v in zip(refs[nr + npar:], vals):
            o_ref[...] = v.astype(o_ref.dtype)

    return pl.pallas_call(
        body, grid=(S // tm,), in_specs=list(specs) + [_full_spec(p) for p in params],
        out_specs=[pl.BlockSpec((tm, w), lambda i: (i, 0)) for w, _ in outs],
        out_shape=[SDS((S, w), dt) for w, dt in outs], name=name, compiler_params=_params("parallel"))(*arrs, *params)


def _rows_vjp(fn, rows, params, cts, wrt_rows, wrt_params, name, adds=None, tm=ROW_TILE):
    adds = adds or {}
    arrs, specs = zip(*[_row_spec(r, tm) for r in rows])
    carrs, cspecs = zip(*[_row_spec(c, tm) for c in cts])
    add_keys = sorted(adds)
    aarrs = [adds[k] for k in add_keys]
    S = arrs[0].shape[0]
    nr, npar, nc, na = len(rows), len(params), len(cts), len(aarrs)
    widths = [specs[k].block_shape[1] for k in wrt_rows]

    def body(*refs):
        i = pl.program_id(0)
        rv = [r[...].astype(f32) for r in refs[:nr]]
        pv = [p[...] for p in refs[nr:nr + npar]]
        cv = tuple(c[...].astype(f32) for c in refs[nr + npar:nr + npar + nc])
        av = dict(zip(add_keys, refs[nr + npar + nc:nr + npar + nc + na]))
        o = refs[nr + npar + nc + na:]
        _, vjp = jax.vjp(fn, *rv, *pv)
        g = vjp(cv)
        for n, k in enumerate(wrt_rows):
            val = g[k]
            if k in av:
                val = val + av[k][...]
            o[n][...] = val
        for n, k in enumerate(wrt_params):
            ref = o[len(wrt_rows) + n]

            @pl.when(i == 0)
            def _(ref=ref):
                ref[...] = jnp.zeros_like(ref)

            ref[...] += g[nr + k]

    res = pl.pallas_call(
        body, grid=(S // tm,),
        in_specs=list(specs) + [_full_spec(p) for p in params] + list(cspecs)
        + [pl.BlockSpec((tm, a.shape[1]), lambda i: (i, 0)) for a in aarrs],
        out_specs=[pl.BlockSpec((tm, w), lambda i: (i, 0)) for w in widths] + [_full_spec(params[k]) for k in wrt_params],
        out_shape=[SDS((S, w), f32) for w in widths] + [SDS(params[k].shape, f32) for k in wrt_params],
        name=name, compiler_params=_params("arbitrary"))(*arrs, *params, *carrs, *aarrs)
    return res[:len(wrt_rows)], res[len(wrt_rows):]


def _rms(x, w):
    return x * lax.rsqrt(jnp.mean(x * x, axis=-1, keepdims=True) + EPS) * w


def _rms_fn(x, w):
    return (_rms(x, w),)


def _res_rms_fn(f, res, w):
    return (res + _rms(f, w),)


def _swap_halves(x):
    lane = lax.broadcasted_iota(jnp.int32, x.shape, 1)
    first = (lane % HEAD_DIM) < (HEAD_DIM // 2)
    n = x.shape[1]
    return jnp.where(first, pltpu.roll(x, n - HEAD_DIM // 2, 1), pltpu.roll(x, HEAD_DIM // 2, 1))


def _rope_fwd_fn(q, k, cos, sgn_sin):
    scale = HEAD_DIM ** -0.5
    return ((q * cos + _swap_halves(q) * sgn_sin) * scale, k * cos + _swap_halves(k) * sgn_sin)


def _rope_bwd_fn(dq, dk, cos, sgn_sin):
    dq = dq * (HEAD_DIM ** -0.5)
    return (dq * cos + _swap_halves(dq * sgn_sin), dk * cos + _swap_halves(dk * sgn_sin))


def _nt(a, b):
    return lax.dot_general(a, b, (((1,), (1,)), ((), ())), preferred_element_type=f32)


def _tn(a, b):
    return lax.dot_general(a, b, (((0,), (0,)), ((), ())), preferred_element_type=f32)


def _band_rows(j, d, nb):
    r, i = j // nb, j % nb
    if d == 1:
        cur = pl.ds(pl.multiple_of(i * 128, 128), 128)
        prev = pl.ds(pl.multiple_of(jnp.maximum(i - 1, 0) * 128, 128), 128)
    else:
        cur = pl.ds(i * (128 * d) + r, 128, stride=d)
        prev = pl.ds(jnp.maximum(i - 1, 0) * (128 * d) + r, 128, stride=d)
    a = lax.broadcasted_iota(jnp.int32, (128, 128), 0)
    c = lax.broadcasted_iota(jnp.int32, (128, 128), 1)
    return cur, prev, c <= a, jnp.logical_and(c >= a, i != 0)


def _attn_fwd(qr, kr, proj, name):
    S = qr.shape[0]
    nblk = S // 128

    def body(q_ref, k_ref, v_ref, out_ref, lse_ref, *scr):
        head_a = lax.broadcasted_iota(jnp.int32, (1, 128), 1) < HEAD_DIM
        for b, (d, nb) in enumerate(BRANCHES):
            ob_ref, lb_ref = scr[2 * b], scr[2 * b + 1]

            def blk(j, carry, d=d, nb=nb, ob_ref=ob_ref, lb_ref=lb_ref):
                cur, prev, mc, mp = _band_rows(j, d, nb)
                q = q_ref[cur, :]
                kc, kp = k_ref[cur, :].astype(bf16), k_ref[prev, :].astype(bf16)
                vc, vp = v_ref[cur, :].astype(bf16), v_ref[prev, :].astype(bf16)
                res = []
                for m in (head_a, jnp.logical_not(head_a)):
                    qm = jnp.where(m, q, 0.0).astype(bf16)
                    sc = jnp.where(mc, _nt(qm, kc), NEG)
                    sp = jnp.where(mp, _nt(qm, kp), NEG)
                    mx = jnp.maximum(jnp.max(sc, axis=1, keepdims=True), jnp.max(sp, axis=1, keepdims=True))
                    pc, pp = jnp.exp(sc - mx), jnp.exp(sp - mx)
                    l = jnp.sum(pc, axis=1, keepdims=True) + jnp.sum(pp, axis=1, keepdims=True)
                    o = (jnp.dot(pc.astype(bf16), vc, preferred_element_type=f32)
                         + jnp.dot(pp.astype(bf16), vp, preferred_element_type=f32)) / l
                    res.append((o, mx + jnp.log(l)))
                ob_ref[cur, :] = jnp.where(head_a, res[0][0], res[1][0])
                lb_ref[cur, :] = jnp.where(head_a, res[0][1], res[1][1])
                return carry

            lax.fori_loop(0, nblk, blk, 0)
        l0, l1, l2 = scr[1][...], scr[3][...], scr[5][...]
        mx = jnp.maximum(jnp.maximum(l0, l1), l2)
        e0, e1, e2 = jnp.exp(l0 - mx), jnp.exp(l1 - mx), jnp.exp(l2 - mx)
        den = e0 + e1 + e2
        out_ref[...] = ((e0 * scr[0][...] + e1 * scr[2][...] + e2 * scr[4][...]) / den).astype(out_ref.dtype)
        lse_ref[...] = mx + jnp.log(den)

    pair = pl.BlockSpec((S, 128), lambda h: (0, h))
    return pl.pallas_call(
        body, grid=(N_HEADS_A // 2,),
        in_specs=[pair, pair, pl.BlockSpec((S, 128), lambda h: (0, 2 * ATTN_W // 128 + h))], out_specs=[pair, pair],
        out_shape=[SDS((S, ATTN_W), bf16), SDS((S, ATTN_W), f32)], scratch_shapes=[pltpu.VMEM((S, 128), f32)] * 6,
        name=name, compiler_params=_params("parallel"))(qr, kr, proj)


def _attn_bwd(qr, kr, proj, dmix_in, out, lse, name):
    S = qr.shape[0]
    nblk = S // 128

    def body(q_ref, k_ref, v_ref, do_ref, out_ref, lse_ref, dq_ref, dk_ref, dv_ref, t_ref):
        head_a = lax.broadcasted_iota(jnp.int32, (1, 128), 1) < HEAD_DIM
        x = do_ref[...] * out_ref[...].astype(f32)
        t_ref[...] = jnp.where(head_a, jnp.sum(jnp.where(head_a, x, 0.0), axis=1, keepdims=True),
                               jnp.sum(jnp.where(head_a, 0.0, x), axis=1, keepdims=True))
        dq_ref[...] = jnp.zeros_like(dq_ref)
        dk_ref[...] = jnp.zeros_like(dk_ref)
        dv_ref[...] = jnp.zeros_like(dv_ref)
        for d, nb in BRANCHES:
            def blk(j, carry, d=d, nb=nb):
                cur, prev, mc, mp = _band_rows(j, d, nb)
                q, do = q_ref[cur, :], do_ref[cur, :]
                kc, kp = k_ref[cur, :].astype(bf16), k_ref[prev, :].astype(bf16)
                vc, vp = v_ref[cur, :].astype(bf16), v_ref[prev, :].astype(bf16)
                t, lse_b = t_ref[cur, :], lse_ref[cur, :]
                dq = dkc = dkp = dvc = dvp = jnp.zeros((128, 128), f32)
                for m, off in ((head_a, 0), (jnp.logical_not(head_a), HEAD_DIM)):
                    th, lh = t[:, off:off + 1], lse_b[:, off:off + 1]
                    qm = jnp.where(m, q, 0.0).astype(bf16)
                    dom = jnp.where(m, do, 0.0).astype(bf16)
                    pc = jnp.exp(jnp.where(mc, _nt(qm, kc), NEG) - lh)
                    pp = jnp.exp(jnp.where(mp, _nt(qm, kp), NEG) - lh)
                    dsc = (pc * (_nt(dom, vc) - th)).astype(bf16)
                    dsp = (pp * (_nt(dom, vp) - th)).astype(bf16)
                    dq = dq + jnp.where(m, jnp.dot(dsc, kc, preferred_element_type=f32)
                                        + jnp.dot(dsp, kp, preferred_element_type=f32), 0.0)
                    dvc, dvp = dvc + _tn(pc.astype(bf16), dom), dvp + _tn(pp.astype(bf16), dom)
                    dkc, dkp = dkc + _tn(dsc, qm), dkp + _tn(dsp, qm)
                dq_ref[cur, :] += dq
                dk_ref[cur, :] += dkc
                dv_ref[cur, :] += dvc
                dk_ref[prev, :] += dkp
                dv_ref[prev, :] += dvp
                return carry

            lax.fori_loop(0, nblk, blk, 0)

    pair = pl.BlockSpec((S, 128), lambda h: (0, h))
    return pl.pallas_call(
        body, grid=(N_HEADS_A // 2,),
        in_specs=[pair, pair, pl.BlockSpec((S, 128), lambda h: (0, 2 * ATTN_W // 128 + h)), pair, pair, pair],
        out_specs=[pair] * 3, out_shape=[SDS((S, ATTN_W), f32)] * 3, scratch_shapes=[pltpu.VMEM((S, 128), f32)],
        name=name, compiler_params=_params("parallel"))(qr, kr, proj, dmix_in, out, lse)


def _conv_val(x, w, K, rows):
    acc = x * w[K - 1:K, :]
    for s in range(1, K):
        acc = acc + jnp.where(rows >= s, pltpu.roll(x, s, 0), 0.0) * w[K - 1 - s:K - s, :]
    return acc


def _colconv_fwd(xs, ws, bs, K, fn, nblk, tc, outs, name):
    S = xs[0][0].shape[0]
    n = len(xs)
    has_b = bs is not None

    def body(*refs):
        rows = lax.broadcasted_iota(jnp.int32, (S, tc), 0)
        cs = []
        for k in range(n):
            c = _conv_val(refs[k][...], refs[n + k][...], K, rows)
            if has_b:
                c = c + refs[2 * n + k][...]
            cs.append(c)
        for o_ref, val in zip(refs[(3 if has_b else 2) * n:], fn(*cs)):
            o_ref[...] = val.astype(o_ref.dtype)

    def cspec(rows_, cb0):
        return pl.BlockSpec((rows_, tc), lambda j, cb0=cb0: (0, cb0 + j))

    in_specs = [cspec(S, cb) for _, cb in xs] + [cspec(K, cb) for _, cb in ws]
    args = [a for a, _ in xs] + [a for a, _ in ws]
    if has_b:
        in_specs += [cspec(1, cb) for _, cb in bs]
        args += [a for a, _ in bs]
    return pl.pallas_call(
        body, grid=(nblk,), in_specs=in_specs, out_specs=[cspec(S, 0) for _ in outs],
        out_shape=[SDS((S, nblk * tc), dt) for dt in outs], name=name, compiler_params=_params("parallel"))(*args)


def _colconv_bwd(xs, ws, bs, K, fn, douts, nblk, tc, name):
    S = xs[0][0].shape[0]
    n, nd = len(xs), len(douts)
    has_b = bs is not None
    nin = (3 if has_b else 2) * n

    def body(*refs):
        rows = lax.broadcasted_iota(jnp.int32, (S, tc), 0)
        x = [refs[k][...] for k in range(n)]
        w = [refs[n + k][...] for k in range(n)]
        cs = []
        for k in range(n):
            c = _conv_val(x[k], w[k], K, rows)
            if has_b:
                c = c + refs[2 * n + k][...]
            cs.append(c)
        _, vjp = jax.vjp(fn, *cs)
        dcs = vjp(tuple(r[...].astype(f32) for r in refs[nin:nin + nd]))
        o = refs[nin + nd:]
        for k in range(n):
            dc = dcs[k]
            dx = dc * w[k][K - 1:K, :]
            o[n + k][K - 1:K, :] = jnp.sum(dc * x[k], axis=0, keepdims=True)
            for s in range(1, K):
                dx = dx + jnp.where(rows < S - s, pltpu.roll(dc, S - s, 0), 0.0) * w[k][K - 1 - s:K - s, :]
                xsh = jnp.where(rows >= s, pltpu.roll(x[k], s, 0), 0.0)
                o[n + k][K - 1 - s:K - s, :] = jnp.sum(dc * xsh, axis=0, keepdims=True)
            o[k][...] = dx
            if has_b:
                o[2 * n + k][...] = jnp.sum(dc, axis=0, keepdims=True)

    def cspec(rows_, cb0):
        return pl.BlockSpec((rows_, tc), lambda j, cb0=cb0: (0, cb0 + j))

    in_specs = [cspec(S, cb) for _, cb in xs] + [cspec(K, cb) for _, cb in ws]
    args = [a for a, _ in xs] + [a for a, _ in ws]
    if has_b:
        in_specs += [cspec(1, cb) for _, cb in bs]
        args += [a for a, _ in bs]
    in_specs += [cspec(S, 0) for _ in douts]
    args += list(douts)
    W = nblk * tc
    out_specs = [cspec(S, 0)] * n + [cspec(K, 0)] * n + ([cspec(1, 0)] * n if has_b else [])
    out_shape = [SDS((S, W), f32)] * n + [SDS((K, W), f32)] * n + ([SDS((1, W), f32)] * n if has_b else [])
    res = pl.pallas_call(body, grid=(nblk,), in_specs=in_specs, out_specs=out_specs, out_shape=out_shape,
                         name=name, compiler_params=_params("parallel"))(*args)
    return res[:n], res[n:2 * n], res[2 * n:]


def _silu_fn(c):
    return (c * jax.nn.sigmoid(c),)


def _geglu_fn(gate, up):
    gelu = 0.5 * gate * (1.0 + jnp.tanh(0.7978845608028654 * (gate + 0.044715 * gate * gate * gate)))
    return (gelu * up,)


def _softplus(x):
    u = jnp.exp(jnp.minimum(x, 20.0))
    small = u * (1.0 - 0.5 * u)
    return jnp.where(x > 20.0, x, jnp.where(u < 1e-4, small, jnp.log(1.0 + u)))


def _bdot(a, b):
    return jnp.dot(a.astype(bf16), b.astype(bf16), preferred_element_type=f32)


def _hdot(a, b):
    return jnp.dot(a, b, precision=HI, preferred_element_type=f32)


def _unit_lower_inverse(A):
    n = A.shape[0]
    eye = (lax.broadcasted_iota(jnp.int32, (n, n), 0) == lax.broadcasted_iota(jnp.int32, (n, n), 1)).astype(f32)
    P = -A
    T = eye + P
    for _ in range(5):
        P = _hdot(P, P)
        T = T + _hdot(T, P)
    return T


def _dn_chunk(qc, kc, vc, zc, bac, alog, dtb, nw, St, h):
    C = CHUNK
    lane = lax.broadcasted_iota(jnp.int32, (1, 128), 1)

    def sel(arr, idx):
        return jnp.sum(jnp.where(lane == idx, arr, 0.0), axis=1, keepdims=True)

    beta = jax.nn.sigmoid(sel(bac, h))
    g = -jnp.exp(sel(alog, h)) * _softplus(sel(bac, N_HEADS_D + h) + sel(dtb, h))
    qn = qc * lax.rsqrt(jnp.sum(qc * qc, axis=-1, keepdims=True) + EPS) * (DK ** -0.5)
    kn = kc * lax.rsqrt(jnp.sum(kc * kc, axis=-1, keepdims=True) + EPS)
    ii = lax.broadcasted_iota(jnp.int32, (C, C), 0)
    jj = lax.broadcasted_iota(jnp.int32, (C, C), 1)
    tril, strict = ii >= jj, ii > jj
    gsq = jnp.broadcast_to(g, (C, C))
    gcol = _hdot(tril.astype(f32), gsq)
    grow = _hdot(jnp.ones((C, C), f32), jnp.where(ii <= jj, gsq, 0.0))
    decay = jnp.exp(jnp.where(tril, gcol - grow, NEG))
    gc = gcol[:, :1]
    glast = gcol[C - 1:C, :1]
    kb = kn * beta
    A = jnp.where(strict, _nt(kb.astype(bf16), kn.astype(bf16)) * decay, 0.0)
    T = _unit_lower_inverse(A)
    u = _bdot(T, vc * beta)
    w = _bdot(T, kb * jnp.exp(gc))
    qk = _nt(qn.astype(bf16), kn.astype(bf16)) * decay
    qd = qn * jnp.exp(gc)
    kd = kn * jnp.exp(glast - gc)
    vnew = u - _bdot(w, St)
    o = _bdot(qd, St) + _bdot(qk, vnew)
    Snew = St * jnp.exp(glast) + _tn(kd.astype(bf16), vnew.astype(bf16))
    out = _rms(o, nw) * (zc * jax.nn.sigmoid(zc))
    return out, Snew


def _dn_specs(S, hg):
    w = DK * hg

    def col(first):
        return pl.BlockSpec((S, w), lambda g, first=first: (0, first // w + g))

    par = pl.BlockSpec((1, 128), lambda g: (0, 0))
    return [col(0), col(N_HEADS_D * DK), col(2 * N_HEADS_D * DK), col(3072),
            pl.BlockSpec((S, 128), lambda g: (0, 3584 // 128)), par, par, par]


def _dn_fwd(qkv, proj, alog, dtb, nw, name, hg=N_HEADS_D):
    S = qkv.shape[0]

    def body(q_ref, k_ref, v_ref, z_ref, ba_ref, al_ref, dt_ref, nw_ref, o_ref, st_ref):
        g = pl.program_id(0)
        st_ref[...] = jnp.zeros_like(st_ref)

        def step(n, carry):
            r = pl.ds(pl.multiple_of(n * CHUNK, CHUNK), CHUNK)
            bac = ba_ref[r, :]
            for hh in range(hg):
                c = slice(DK * hh, DK * (hh + 1))
                out, Sn = _dn_chunk(q_ref[r, c], k_ref[r, c], v_ref[r, c], z_ref[r, c], bac,
                                    al_ref[...], dt_ref[...], nw_ref[...], st_ref[hh], g * hg + hh)
                o_ref[r, c] = out.astype(o_ref.dtype)
                st_ref[hh] = Sn
            return carry

        lax.fori_loop(0, S // CHUNK, step, 0)

    return pl.pallas_call(
        body, grid=(N_HEADS_D // hg,), in_specs=_dn_specs(S, hg),
        out_specs=pl.BlockSpec((S, DK * hg), lambda g: (0, g)),
        out_shape=SDS((S, N_HEADS_D * DK), bf16), scratch_shapes=[pltpu.VMEM((hg, DK, DK), f32)], name=name,
        compiler_params=_params("arbitrary"))(qkv, qkv, qkv, proj, proj, alog, dtb, nw)


def _dn_bwd(qkv, proj, alog, dtb, nw, dmix_in, name, hg=2):
    S = qkv.shape[0]
    NCH = S // CHUNK

    def body(q_ref, k_ref, v_ref, z_ref, ba_ref, al_ref, dt_ref, nw_ref, do_ref,
             dq_ref, dk_ref, dv_ref, dz_ref, dba_ref, dal_ref, ddt_ref, dnw_ref, st_ref, ds_ref):
        g = pl.program_id(0)

        @pl.when(g == 0)
        def _():
            dba_ref[...] = jnp.zeros_like(dba_ref)
            dal_ref[...] = jnp.zeros_like(dal_ref)
            ddt_ref[...] = jnp.zeros_like(ddt_ref)
            dnw_ref[...] = jnp.zeros_like(dnw_ref)

        def chunk(n, hh, St):
            r = pl.ds(pl.multiple_of(n * CHUNK, CHUNK), CHUNK)
            c = slice(DK * hh, DK * (hh + 1))
            return (q_ref[r, c], k_ref[r, c], v_ref[r, c], z_ref[r, c], ba_ref[r, :],
                    al_ref[...], dt_ref[...], nw_ref[...], St), r, c

        ds_ref[...] = jnp.zeros_like(ds_ref)

        def fwd(n, carry):
            for hh in range(hg):
                St = ds_ref[hh]
                st_ref[n, hh] = St
                args, _, _ = chunk(n, hh, St)
                ds_ref[hh] = _dn_chunk(*args, g * hg + hh)[1]
            return carry

        lax.fori_loop(0, NCH, fwd, 0)
        ds_ref[...] = jnp.zeros_like(ds_ref)

        def bwd(t, carry):
            dal, ddt, dnw = carry
            n = NCH - 1 - t
            dba = jnp.zeros((CHUNK, 128), f32)
            for hh in range(hg):
                args, r, c = chunk(n, hh, st_ref[n, hh])
                _, vjp = jax.vjp(lambda *a, hh=hh: _dn_chunk(*a, g * hg + hh), *args)
                gr = vjp((do_ref[r, c], ds_ref[hh]))
                dq_ref[r, c] = gr[0]
                dk_ref[r, c] = gr[1]
                dv_ref[r, c] = gr[2]
                dz_ref[r, c] = gr[3]
                ds_ref[hh] = gr[8]
                dba, dal, ddt, dnw = dba + gr[4], dal + gr[5], ddt + gr[6], dnw + gr[7]
            dba_ref[r, :] += dba
            return dal, ddt, dnw

        zero = jnp.zeros((1, 128), f32)
        dal, ddt, dnw = lax.fori_loop(0, NCH, bwd, (zero, zero, zero))
        dal_ref[...] += dal
        ddt_ref[...] += ddt
        dnw_ref[...] += dnw

    w = DK * hg
    hcol = pl.BlockSpec((S, w), lambda g: (0, g))
    whole = pl.BlockSpec((S, 128), lambda g: (0, 0))
    par = pl.BlockSpec((1, 128), lambda g: (0, 0))
    W = N_HEADS_D * DK
    return pl.pallas_call(
        body, grid=(N_HEADS_D // hg,),
        in_specs=_dn_specs(S, hg) + [pl.BlockSpec((S, w), lambda g: (0, ATTN_W // w + g))],
        out_specs=[hcol] * 4 + [whole, par, par, par],
        out_shape=[SDS((S, W), f32)] * 4 + [SDS((S, 128), f32)] + [SDS((1, 128), f32)] * 3,
        scratch_shapes=[pltpu.VMEM((NCH, hg, DK, DK), f32), pltpu.VMEM((hg, DK, DK), f32)], name=name,
        compiler_params=_params("arbitrary"))(qkv, qkv, qkv, proj, proj, alog, dtb, nw, dmix_in)


def _loss_head(y, t, name):
    S, D = y.shape
    tm = ROW_TILE

    def body(y_ref, t_ref, dy_ref, l_ref):
        i = pl.program_id(0)
        d = y_ref[...] - t_ref[...]
        dy_ref[...] = d * (1.0 / D)
        part = jnp.sum(jnp.sum(d * d, axis=1, keepdims=True), axis=0, keepdims=True) * (0.5 / D)

        @pl.when(i == 0)
        def _():
            l_ref[...] = jnp.zeros_like(l_ref)

        l_ref[...] += jnp.broadcast_to(part, l_ref.shape)

    spec = pl.BlockSpec((tm, D), lambda i: (i, 0))
    dy, l = pl.pallas_call(body, grid=(S // tm,), in_specs=[spec, spec],
                           out_specs=[spec, pl.BlockSpec((1, 128), lambda i: (0, 0))],
                           out_shape=[SDS((S, D), f32), SDS((1, 128), f32)], name=name,
                           compiler_params=_params("arbitrary"))(y, t)
    return l[0, 0], dy


def _adamw(w, g, m, v, tr, name):
    L, R, C = w.shape
    assert R % tr == 0

    def body(w_ref, g_ref, m_ref, v_ref, d_ref, mo_ref, vo_ref):
        gv = g_ref[...]
        m2 = ADAM_B1 * m_ref[...] + (1.0 - ADAM_B1) * gv
        v2 = ADAM_B2 * v_ref[...] + (1.0 - ADAM_B2) * (gv * gv)
        m_hat = m2 / (1.0 - ADAM_B1 ** ADAM_STEP)
        v_hat = v2 / (1.0 - ADAM_B2 ** ADAM_STEP)
        d_ref[...] = -ADAM_LR * (m_hat / (jnp.sqrt(v_hat) + ADAM_EPS) + ADAM_WD * w_ref[...])
        mo_ref[...] = m2
        vo_ref[...] = v2

    spec = pl.BlockSpec((1, tr, C), lambda l, i: (l, i, 0))
    return pl.pallas_call(body, grid=(L, R // tr), in_specs=[spec] * 4, out_specs=[spec] * 3,
                          out_shape=[SDS((L, R, C), f32)] * 3, name=name,
                          compiler_params=_params("parallel", "parallel"))(w, g, m, v)


def _rope_tables(S):
    inv = 1.0 / (10000.0 ** (jnp.arange(0, HEAD_DIM, 2, dtype=f32) / HEAD_DIM))
    ang = jnp.arange(S, dtype=f32)[:, None] * inv[None, :]
    cos, sin = jnp.cos(ang), jnp.sin(ang)
    return (jnp.tile(jnp.concatenate([cos, cos], axis=1), (1, N_HEADS_A)),
            jnp.tile(jnp.concatenate([-sin, sin], axis=1), (1, N_HEADS_A)))


def _layer_fwd(x, W, cos, sgn_sin, l):
    n = f"l{l}_"
    (h1,) = _rows(_rms_fn, [x], [W["norm_pre_mix"]], [(D_MODEL, bf16)], n + "pre_mix_norm")
    proj = _mm(h1, W["w_in"], "nn", 512, 768, f32, n + "in_proj")
    qr, kr = _rows(_rope_fwd_fn, [(proj, ATTN_W, 0), (proj, ATTN_W, 1), cos, sgn_sin], [],
                   [(ATTN_W, f32), (ATTN_W, f32)], n + "rope")
    attn_out, lse = _attn_fwd(qr, kr, proj, n + "attn_fwd")
    (qkv,) = _colconv_fwd([(proj, 3)], [(W["dn_conv_w"], 0)], None, 4, _silu_fn, 3, 512, [f32], n + "dn_conv")
    dn_out = _dn_fwd(qkv, proj, W["dn_a_log"], W["dn_dt_bias"], W["dn_norm_w"], n + "dn_fwd")
    mix_in = jnp.concatenate([attn_out, dn_out], axis=1)
    mix = _mm(mix_in, W["w_out"], "nn", 512, 512, f32, n + "out_proj")
    (x1,) = _rows(_res_rms_fn, [mix, x], [W["norm_post_mix"]], [(D_MODEL, f32)], n + "post_mix_norm")
    (h2,) = _rows(_rms_fn, [x1], [W["norm_pre_ffn"]], [(D_MODEL, bf16)], n + "pre_ffn_norm")
    u0 = _mm(h2, W["ffn_w_in"], "nn", 512, 512, f32, n + "ffn_in")
    nb_ff = D_FF // 256
    (act,) = _colconv_fwd([(u0, 0), (u0, nb_ff)], [(W["ffn_conv_w"], 0), (W["ffn_conv_w"], nb_ff)],
                          [(W["ffn_conv_b"], 0), (W["ffn_conv_b"], nb_ff)], 3, _geglu_fn, nb_ff, 256, [bf16],
                          n + "ffn_conv_glu")
    f = _mm(act, W["ffn_w_out"], "nn", 512, 512, f32, n + "ffn_out")
    (x2,) = _rows(_res_rms_fn, [f, x1], [W["norm_post_ffn"]], [(D_MODEL, f32)], n + "post_ffn_norm")
    saved = dict(x=x, h1=h1, proj=proj, qr=qr, kr=kr, attn_out=attn_out, lse=lse, qkv=qkv, mix_in=mix_in, mix=mix,
                 x1=x1, h2=h2, u0=u0, act=act, f=f)
    return x2, saved


def _layer_bwd(dx2, sv, W, cos, sgn_sin, l):
    n = f"l{l}_"
    S = dx2.shape[0]
    g = {}
    (df,), (g["norm_post_ffn"],) = _rows_vjp(_rms_fn, [sv["f"]], [W["norm_post_ffn"]], [dx2], [0], [0],
                                             n + "post_ffn_norm_bwd")
    dact = _mm(df, W["ffn_w_out"], "nt", 512, 1408, f32, n + "ffn_out_dx")
    g["ffn_w_out"] = _mm(sv["act"], df, "tn", 256, 1024, f32, n + "ffn_out_dw")
    nb_ff = D_FF // 256
    u0 = sv["u0"]
    dxs, dws, dbs = _colconv_bwd([(u0, 0), (u0, nb_ff)], [(W["ffn_conv_w"], 0), (W["ffn_conv_w"], nb_ff)],
                                 [(W["ffn_conv_b"], 0), (W["ffn_conv_b"], nb_ff)], 3, _geglu_fn, [dact], nb_ff, 256,
                                 n + "ffn_conv_glu_bwd")
    du0 = jnp.concatenate(dxs, axis=1)
    g["ffn_conv_w"] = jnp.concatenate(dws, axis=1)
    g["ffn_conv_b"] = jnp.concatenate(dbs, axis=1)
    dh2 = _mm(du0, W["ffn_w_in"], "nt", 256, 512, f32, n + "ffn_in_dx")
    g["ffn_w_in"] = _mm(sv["h2"], du0, "tn", 512, D_FF // 2, f32, n + "ffn_in_dw", column_shards=True)
    (dx1,), (g["norm_pre_ffn"],) = _rows_vjp(_rms_fn, [sv["x1"]], [W["norm_pre_ffn"]], [dh2], [0], [0],
                                             n + "pre_ffn_norm_bwd", adds={0: dx2})
    (dmix,), (g["norm_post_mix"],) = _rows_vjp(_rms_fn, [sv["mix"]], [W["norm_post_mix"]], [dx1], [0], [0],
                                               n + "post_mix_norm_bwd")
    dmix_in = _mm(dmix, W["w_out"], "nt", 512, 512, f32, n + "out_proj_dx")
    g["w_out"] = _mm(sv["mix_in"], dmix, "tn", 512, 512, f32, n + "out_proj_dw")

    dq, dk, dv, dz, dba, g["dn_a_log"], g["dn_dt_bias"], g["dn_norm_w"] = _dn_bwd(
        sv["qkv"], sv["proj"], W["dn_a_log"], W["dn_dt_bias"], W["dn_norm_w"], dmix_in, n + "dn_bwd")
    dqkv = jnp.concatenate([dq, dk, dv], axis=1)
    (dqkv0,), (g["dn_conv_w"],), _ = _colconv_bwd([(sv["proj"], 3)], [(W["dn_conv_w"], 0)], None, 4, _silu_fn,
                                                 [dqkv], 3, 512, n + "dn_conv_bwd")

    dqr, dkr, dav = _attn_bwd(sv["qr"], sv["kr"], sv["proj"], dmix_in, sv["attn_out"], sv["lse"], n + "attn_bwd")
    daq, dak = _rows(_rope_bwd_fn, [dqr, dkr, cos, sgn_sin], [], [(ATTN_W, f32)] * 2, n + "rope_bwd")
    dproj = jnp.concatenate([daq, dak, dav, dqkv0, dz, dba, jnp.zeros((S, PROJ_W - 3712), f32)], axis=1)
    dh1 = _mm(dproj, W["w_in"], "nt", 256, 512, f32, n + "in_proj_dx")
    g["w_in"] = _mm(sv["h1"], dproj, "tn", 512, 768, f32, n + "in_proj_dw")
    (dx,), (g["norm_pre_mix"],) = _rows_vjp(_rms_fn, [sv["x"]], [W["norm_pre_mix"]], [dh1], [0], [0],
                                            n + "pre_mix_norm_bwd", adds={0: dx1})
    return dx, g


def _local_step(x, target, layers):
    cos, sgn_sin = _rope_tables(x.shape[0])
    saved = []
    for l, W in enumerate(layers):
        x, sv = _layer_fwd(x, W, cos, sgn_sin, l)
        saved.append(sv)
    loss, dx = _loss_head(x, target, "loss_head")
    grads = [None] * len(layers)
    for l in reversed(range(len(layers))):
        dx, grads[l] = _layer_bwd(dx, saved[l], layers[l], cos, sgn_sin, l)
    return loss, dx, grads


def _pos():
    x, y, c = lax.axis_index("x"), lax.axis_index("y"), lax.axis_index("c")
    return x, y, c, [(1 - x, y), (x, 1 - y), (1 - x, 1 - y)]


def _rcopy(src, dst, send_sem, recv_sem, dev):
    return pltpu.make_async_remote_copy(src_ref=src, dst_ref=dst, send_sem=send_sem, recv_sem=recv_sem,
                                        device_id=dev, device_id_type=MESH)


def _half_rows(ref, h, which, axis):
    if h is None:
        return ref
    rows = pl.ds(pl.multiple_of(which * h, 16), h)
    return ref.at[:, rows, :] if axis == 1 else ref.at[rows, :]


def _dma_sems(*counts):
    return [pltpu.SemaphoreType.DMA((k,)) for k in counts]


def _all_gather(arrs, halves, name):
    n = len(arrs)

    def body(*refs):
        ins, outs = refs[:n], refs[n:2 * n]
        send1, recv1, send2, recv2 = refs[2 * n:]
        x, y, c, chips = _pos()
        me, sib, s_me = (x, y, c), (x, y, 1 - c), 2 * x + y
        sends = []
        for i in range(n):
            for j, chip in enumerate(chips):
                cp = _rcopy(_half_rows(ins[i], halves[i], c, 1), _half_rows(outs[i].at[s_me], halves[i], c, 1),
                            send1.at[3 * i + j], recv1.at[3 * i + j], (*chip, c))
                cp.start()
                sends.append(cp)
        for i in range(n):
            for j, (px, py) in enumerate(chips):
                k = 3 * i + j
                landed = _half_rows(outs[i].at[2 * px + py], halves[i], c, 1)
                _rcopy(landed, landed, send1.at[k], recv1.at[k], me).wait_recv()
                if halves[i] is not None:
                    cp = _rcopy(landed, landed, send2.at[k], recv2.at[k], sib)
                    cp.start()
                    sends.append(cp)
        for i in range(n):
            if halves[i] is None:
                continue
            for j, (px, py) in enumerate(chips):
                k = 3 * i + j
                other = _half_rows(outs[i].at[2 * px + py], halves[i], 1 - c, 1)
                _rcopy(other, other, send2.at[k], recv2.at[k], me).wait_recv()
        for cp in sends:
            cp.wait_send()

    return pl.pallas_call(
        body, in_specs=[ANY] * n, out_specs=[ANY] * n,
        out_shape=[SDS((4,) + a.shape, a.dtype) for a in arrs],
        scratch_shapes=_dma_sems(3 * n, 3 * n, 3 * n, 3 * n), name=name)(*arrs)


def _exchange_halves(gs, name):
    n = len(gs)

    def body(*refs):
        ins, outs = refs[:n], refs[n:2 * n]
        send, recv = refs[2 * n:]
        x, y, c, _ = _pos()
        sends = []
        for k in range(n):
            cp = _rcopy(_half_rows(ins[k], gs[k].shape[1] // 2, 1 - c, 1), outs[k], send.at[k], recv.at[k], (x, y, 1 - c))
            cp.start()
            sends.append(cp)
        for k in range(n):
            _rcopy(outs[k], outs[k], send.at[k], recv.at[k], (x, y, c)).wait_recv()
        for cp in sends:
            cp.wait_send()

    return pl.pallas_call(
        body, in_specs=[ANY] * n, out_specs=[ANY] * n,
        out_shape=[SDS((4, g.shape[1] // 2, g.shape[2]), g.dtype) for g in gs],
        scratch_shapes=_dma_sems(n, n), name=name)(*gs)


def _scatter_partials(ps, name):
    n = len(ps)

    def body(*refs):
        ins, outs = refs[:n], refs[n:2 * n]
        send, recv = refs[2 * n:]
        x, y, c, chips = _pos()
        sends = []
        for k in range(n):
            for j, (px, py) in enumerate(chips):
                cp = _rcopy(ins[k].at[2 * px + py], outs[k].at[j], send.at[3 * k + j], recv.at[3 * k + j], (px, py, c))
                cp.start()
                sends.append(cp)
        for k in range(n):
            for j in range(3):
                _rcopy(outs[k].at[j], outs[k].at[j], send.at[3 * k + j], recv.at[3 * k + j], (x, y, c)).wait_recv()
        for cp in sends:
            cp.wait_send()

    return pl.pallas_call(
        body, in_specs=[ANY] * n, out_specs=[ANY] * n,
        out_shape=[SDS((3,) + p.shape[1:], p.dtype) for p in ps],
        scratch_shapes=_dma_sems(3 * n, 3 * n), name=name)(*ps)


def _join_halves(rs, name):
    n = len(rs)

    def body(*refs):
        outs = refs[n:2 * n]
        send, recv = refs[2 * n:]
        x, y, c, _ = _pos()
        sends = []
        for k in range(n):
            mine = _half_rows(outs[k], rs[k].shape[0] // 2, c, 0)
            cp = _rcopy(mine, mine, send.at[k], recv.at[k], (x, y, 1 - c))
            cp.start()
            sends.append(cp)
        for k in range(n):
            other = _half_rows(outs[k], rs[k].shape[0] // 2, 1 - c, 0)
            _rcopy(other, other, send.at[k], recv.at[k], (x, y, c)).wait_recv()
        for cp in sends:
            cp.wait_send()

    return pl.pallas_call(
        body, in_specs=[ANY] * n, out_specs=[ANY] * n, out_shape=[SDS(r.shape, r.dtype) for r in rs],
        input_output_aliases={k: k for k in range(n)}, scratch_shapes=_dma_sems(n, n), name=name)(*rs)


def _all_reduce_small(pack, name):
    R = pack.shape[0]

    def body(in_ref, out_ref, buf, send, recv):
        x, y, c, _ = _pos()
        me = 4 * x + 2 * y + c
        buf[me] = in_ref[...]
        sends = []
        for k in range(1, 8):
            peer = me ^ k
            cp = _rcopy(buf.at[me], buf.at[me], send.at[k - 1], recv.at[k - 1], ((peer >> 2) & 1, (peer >> 1) & 1, peer & 1))
            cp.start()
            sends.append(cp)
        for k in range(1, 8):
            _rcopy(buf.at[me ^ k], buf.at[me ^ k], send.at[k - 1], recv.at[k - 1], (x, y, c)).wait_recv()
        for cp in sends:
            cp.wait_send()
        acc = buf[0]
        for d in range(1, 8):
            acc = acc + buf[d]
        out_ref[...] = acc

    return pl.pallas_call(
        body, out_shape=SDS((R, 128), f32),
        in_specs=[pl.BlockSpec(memory_space=pltpu.VMEM)], out_specs=pl.BlockSpec(memory_space=pltpu.VMEM),
        scratch_shapes=[pltpu.VMEM((8, R, 128), f32)] + _dma_sems(7, 7), name=name)(pack)


def _add_sibling(g, recv, c_arr, tr, name):
    _, R, C = g.shape
    h = R // 2
    nrb = h // tr
    assert h % tr == 0

    def body(c_ref, g_ref, r_ref, o_ref):
        o_ref[...] = (g_ref[...] + r_ref[...]).astype(o_ref.dtype)

    spec = pl.BlockSpec((1, tr, C), lambda s, r, c_ref: (s, r, 0))
    grid_spec = pltpu.PrefetchScalarGridSpec(
        num_scalar_prefetch=1, grid=(4, nrb),
        in_specs=[pl.BlockSpec((1, tr, C), lambda s, r, c_ref: (s, c_ref[0] * nrb + r, 0)), spec], out_specs=spec)
    return pl.pallas_call(body, grid_spec=grid_spec, out_shape=SDS((4, h, C), bf16), name=name,
                          compiler_params=_params("parallel", "parallel"))(c_arr, g, recv)


def _add_chips(p, recv, sc_arr, tr, name):
    _, h, C = p.shape
    nrb = h // tr
    assert h % tr == 0

    def body(sc_ref, p_ref, r_ref, o_ref):
        o_ref[...] = (p_ref[0].astype(f32) + r_ref[0].astype(f32)) + (r_ref[1].astype(f32) + r_ref[2].astype(f32))

    grid_spec = pltpu.PrefetchScalarGridSpec(
        num_scalar_prefetch=1, grid=(nrb,),
        in_specs=[pl.BlockSpec((1, tr, C), lambda r, sc_ref: (sc_ref[0], r, 0)),
                  pl.BlockSpec((3, tr, C), lambda r, sc_ref: (0, r, 0))],
        out_specs=pl.BlockSpec((tr, C), lambda r, sc_ref: (sc_ref[1] * nrb + r, 0)))
    return pl.pallas_call(body, grid_spec=grid_spec, out_shape=SDS((2 * h, C), f32), name=name,
                          compiler_params=_params("parallel"))(sc_arr, p, recv)


_BIG = (("w_in", 1024, 256), ("w_out", 256, 128), ("ffn_w_in", 1024, 256), ("ffn_w_out", 704, 352))
_SMALL = ("dn_conv_w", "ffn_conv_w", "ffn_conv_b", "norm_pre_mix", "norm_post_mix", "norm_pre_ffn", "norm_post_ffn",
          "dn_norm_w", "dn_a_log", "dn_dt_bias")
_WEIGHTS = ("w_in", "dn_conv_w", "dn_a_log", "dn_dt_bias", "dn_norm_w", "w_out", "ffn_w_in", "ffn_conv_w", "ffn_conv_b",
            "ffn_w_out", "norm_pre_mix", "norm_post_mix", "norm_pre_ffn", "norm_post_ffn")
_ADAM_ROWS = {"w_in": 256, "w_out": 256, "ffn_w_in": 128, "ffn_w_out": 176}


def _shard_major(name, g):
    if name == "w_in":
        return jnp.stack([g[:, 898 * s:898 * (s + 1)] for s in range(4)])
    if name == "ffn_w_in":
        return g
    return g.reshape(4, g.shape[0] // 4, g.shape[1])


def kernel(x, w_in, dn_conv_w, dn_a_log, dn_dt_bias, dn_norm_w, w_out, ffn_w_in, ffn_conv_w, ffn_conv_b, ffn_w_out, norm_pre_mix, norm_post_mix, norm_pre_ffn, norm_post_ffn, loss_target, m_w_in, m_dn_conv_w, m_dn_a_log, m_dn_dt_bias, m_dn_norm_w, m_w_out, m_ffn_w_in, m_ffn_conv_w, m_ffn_conv_b, m_ffn_w_out, m_norm_pre_mix, m_norm_post_mix, m_norm_pre_ffn, m_norm_post_ffn, v_w_in, v_dn_conv_w, v_dn_a_log, v_dn_dt_bias, v_dn_norm_w, v_w_out, v_ffn_w_in, v_ffn_conv_w, v_ffn_conv_b, v_ffn_w_out, v_norm_pre_mix, v_norm_post_mix, v_norm_pre_ffn, v_norm_post_ffn):
    w = dict(w_in=w_in, dn_conv_w=dn_conv_w, dn_a_log=dn_a_log, dn_dt_bias=dn_dt_bias, dn_norm_w=dn_norm_w, w_out=w_out,
             ffn_w_in=ffn_w_in, ffn_conv_w=ffn_conv_w, ffn_conv_b=ffn_conv_b, ffn_w_out=ffn_w_out, norm_pre_mix=norm_pre_mix,
             norm_post_mix=norm_post_mix, norm_pre_ffn=norm_pre_ffn, norm_post_ffn=norm_post_ffn)
    m = dict(w_in=m_w_in, dn_conv_w=m_dn_conv_w, dn_a_log=m_dn_a_log, dn_dt_bias=m_dn_dt_bias, dn_norm_w=m_dn_norm_w,
             w_out=m_w_out, ffn_w_in=m_ffn_w_in, ffn_conv_w=m_ffn_conv_w, ffn_conv_b=m_ffn_conv_b, ffn_w_out=m_ffn_w_out,
             norm_pre_mix=m_norm_pre_mix, norm_post_mix=m_norm_post_mix, norm_pre_ffn=m_norm_pre_ffn,
             norm_post_ffn=m_norm_post_ffn)
    v = dict(w_in=v_w_in, dn_conv_w=v_dn_conv_w, dn_a_log=v_dn_a_log, dn_dt_bias=v_dn_dt_bias, dn_norm_w=v_dn_norm_w,
             w_out=v_w_out, ffn_w_in=v_ffn_w_in, ffn_conv_w=v_ffn_conv_w, ffn_conv_b=v_ffn_conv_b, ffn_w_out=v_ffn_w_out,
             norm_pre_mix=v_norm_pre_mix, norm_post_mix=v_norm_post_mix, norm_pre_ffn=v_norm_pre_ffn,
             norm_post_ffn=v_norm_post_ffn)
    xi, yi, ci = lax.axis_index("x"), lax.axis_index("y"), lax.axis_index("c")
    s_me = 2 * xi + yi
    c_arr = jnp.reshape(ci, (1,)).astype(jnp.int32)
    sc_arr = jnp.stack([s_me, ci]).astype(jnp.int32)

    own = [w_in.astype(bf16), w_out.astype(bf16), ffn_w_in.astype(bf16), ffn_w_out.astype(bf16), dn_conv_w, ffn_conv_w]
    gathered = _all_gather(own, [512, 128, 512, 352, None, None], "weights_all_gather")
    shards = [[jnp.where(s_me == s, o, a[s]) for s in range(4)] for o, a in zip(own, gathered)]
    cat = lambda k: jnp.concatenate(shards[k], axis=-1)
    rows = lambda k: jnp.concatenate(shards[k], axis=1)
    full = dict(w_in=jnp.pad(cat(0), ((0, 0), (0, 0), (0, PROJ_W - IN_COLS))), w_out=rows(1),
                ffn_w_in=cat(2), ffn_w_out=rows(3), dn_conv_w=cat(4), ffn_conv_w=cat(5))
    lanes = lambda a: jnp.pad(a, ((0, 0), (0, 128 - a.shape[1])))
    full.update(dn_a_log=lanes(dn_a_log), dn_dt_bias=lanes(dn_dt_bias), dn_norm_w=dn_norm_w, ffn_conv_b=ffn_conv_b,
                norm_pre_mix=norm_pre_mix, norm_post_mix=norm_post_mix, norm_pre_ffn=norm_pre_ffn,
                norm_post_ffn=norm_post_ffn)
    layers = [{k: (a[l] if a.ndim == 3 else a[l:l + 1]) for k, a in full.items()} for l in range(2)]

    loss_local, dx, grads = _local_step(x[0], loss_target[0], layers)
    loss = lax.psum(loss_local, ("x", "y", "c"))

    keys = [(name, l) for name, _, _ in _BIG for l in range(2)]
    tiles = {name: tr for name, _, tr in _BIG}
    gs = [_shard_major(name, grads[l][name]) for name, l in keys]
    from_sib = _exchange_halves(gs, "grads_to_sibling")
    chip_part = [_add_sibling(g, r, c_arr, tiles[name], f"add_sibling_{name}{l}") for g, r, (name, l) in zip(gs, from_sib, keys)]
    from_chips = _scatter_partials(chip_part, "grads_to_owner_chip")
    reduced = [_add_chips(p, r, sc_arr, tiles[name], f"add_chips_{name}{l}") for p, r, (name, l) in zip(chip_part, from_chips, keys)]
    joined = _join_halves(reduced, "grads_join_halves")
    g_out = {name: jnp.stack([joined[2 * i], joined[2 * i + 1]]) for i, (name, _, _) in enumerate(_BIG)}

    small = {}
    for name in _SMALL:
        per_layer = [grads[l][name] for l in range(2)]
        if name in ("dn_a_log", "dn_dt_bias"):
            per_layer = [p[:, :N_HEADS_D] for p in per_layer]
        small[name] = jnp.stack(per_layer).reshape((2,) + (w[name].shape[1:] if name not in ("dn_conv_w", "ffn_conv_w")
                                                           else per_layer[0].shape))
    flat = jnp.concatenate([small[name].reshape(-1) for name in _SMALL])
    n_rows = -(-flat.shape[0] // 1024) * 8
    summed = _all_reduce_small(jnp.pad(flat, (0, n_rows * 128 - flat.shape[0])).reshape(n_rows, 128),
                               "small_grads_all_reduce").reshape(-1)
    off = 0
    for name in _SMALL:
        size = small[name].size
        g_out[name] = summed[off:off + size].reshape(small[name].shape)
        off += size
    g_out["dn_conv_w"] = lax.dynamic_slice_in_dim(g_out["dn_conv_w"], s_me * 384, 384, axis=2)
    g_out["ffn_conv_w"] = lax.dynamic_slice_in_dim(g_out["ffn_conv_w"], s_me * 1408, 1408, axis=2)

    deltas, new_m, new_v = {}, {}, {}
    for name in _WEIGHTS:
        shape = w[name].shape
        as3 = (lambda a: a) if len(shape) == 3 else (lambda a: a.reshape(shape[0], 1, shape[1]))
        tr = _ADAM_ROWS.get(name, as3(w[name]).shape[1])
        d_, m_, v_ = _adamw(as3(w[name]), as3(g_out[name]), as3(m[name]), as3(v[name]), tr, f"adamw_{name}")
        deltas[name], new_m[name], new_v[name] = d_.reshape(shape), m_.reshape(shape), v_.reshape(shape)

    return (loss, dx[None], *[g_out[k] for k in _WEIGHTS], *[deltas[k] for k in _WEIGHTS],
            *[new_m[k] for k in _WEIGHTS], *[new_v[k] for k in _WEIGHTS])
```

```python
import jax
import jax.numpy as jnp
from jax import lax
from jax.experimental import pallas as pl
from jax.experimental.pallas import tpu as pltpu

f32, bf16 = jnp.float32, jnp.bfloat16
SDS = jax.ShapeDtypeStruct
HI = lax.Precision.HIGHEST
MESH = pl.DeviceIdType.MESH
ANY = pl.BlockSpec(memory_space=pl.ANY)

D_MODEL = 1024
N_HEADS_A, HEAD_DIM = 8, 64
ATTN_W = 512
N_HEADS_D, DK = 4, 128
CHUNK = 64
D_FF = 2816
IN_COLS = 3592
PROJ_W = 3840
BRANCHES = ((1, 16), (4, 4), (16, 1))
EPS = 1e-6
NEG = -1e30
ROW_TILE = 256
VMEM_LIMIT = 56 * 1024 * 1024

ADAM_LR, ADAM_B1, ADAM_B2, ADAM_EPS, ADAM_WD, ADAM_STEP = 0.001, 0.9, 0.999, 1e-08, 0.01, 10


def _params(*sem):
    return pltpu.CompilerParams(dimension_semantics=sem, vmem_limit_bytes=VMEM_LIMIT)


def _mm(a, b, mode, tm, tn, out_dtype, name, column_shards=False):
    if mode == "nn":
        (M, K), N = a.shape, b.shape[1]
        dims = (((1,), (0,)), ((), ()))
        a_spec = pl.BlockSpec((tm, K), lambda i, j: (i, 0))
        b_spec = pl.BlockSpec((K, tn), lambda i, j: (0, j))
    elif mode == "nt":
        (M, K), N = a.shape, b.shape[0]
        dims = (((1,), (1,)), ((), ()))
        a_spec = pl.BlockSpec((tm, K), lambda i, j: (i, 0))
        b_spec = pl.BlockSpec((tn, K), lambda i, j: (j, 0))
    else:
        (K, M), N = a.shape, b.shape[1]
        dims = (((0,), (0,)), ((), ()))
        a_spec = pl.BlockSpec((K, tm), lambda i, j: (0, i))
        b_spec = pl.BlockSpec((K, tn), lambda i, j: (0, j))
    assert M % tm == 0 and N % tn == 0, (name, M, N, tm, tn)

    def body(a_ref, b_ref, o_ref):
        o_ref[...] = lax.dot_general(a_ref[...].astype(bf16), b_ref[...].astype(bf16), dims,
                                     preferred_element_type=f32).astype(o_ref.dtype)

    if column_shards:
        out_spec, out_shape = pl.BlockSpec((None, tm, tn), lambda i, j: (j, i, 0)), SDS((N // tn, M, tn), out_dtype)
    else:
        out_spec, out_shape = pl.BlockSpec((tm, tn), lambda i, j: (i, j)), SDS((M, N), out_dtype)
    return pl.pallas_call(body, grid=(M // tm, N // tn), in_specs=[a_spec, b_spec], out_specs=out_spec,
                          out_shape=out_shape, name=name, compiler_params=_params("parallel", "arbitrary"))(a, b)


def _row_spec(r, tm):
    if isinstance(r, tuple):
        arr, width, cb = r
        return arr, pl.BlockSpec((tm, width), lambda i, j, cb=cb: (i, cb + j))
    return r, pl.BlockSpec((tm, r.shape[1]), lambda i, j: (i, j))


def _full_spec(p):
    return pl.BlockSpec(p.shape, lambda i, j: (0,) * p.ndim)


def _rows(fn, rows, params, outs, name, tm=ROW_TILE, ncol=1):
    arrs, specs = zip(*[_row_spec(r, tm) for r in rows])
    S = arrs[0].shape[0]
    nr, npar = len(rows), len(params)

    def body(*refs):
        vals = fn(*[r[...].astype(f32) for r in refs[:nr]], *[p[...] for p in refs[nr:nr + npar]])
        for o_ref, v in zip(refs[nr + npar:], vals):
            o_ref[...] = v.astype(o_ref.dtype)

    return pl.pallas_call(
        body, grid=(S // tm, ncol), in_specs=list(specs) + [_full_spec(p) for p in params],
        out_specs=[pl.BlockSpec((tm, w), lambda i, j: (i, j)) for w, _ in outs],
        out_shape=[SDS((S, w * ncol), dt) for w, dt in outs], name=name,
        compiler_params=_params("parallel", "parallel"))(*arrs, *params)


def _rows_vjp(fn, rows, params, cts, wrt_rows, wrt_params, name, adds=None, tm=ROW_TILE, ncol=1):
    adds = adds or {}
    arrs, specs = zip(*[_row_spec(r, tm) for r in rows])
    carrs, cspecs = zip(*[_row_spec(c, tm) for c in cts])
    add_keys = sorted(adds)
    aarrs = [adds[k] for k in add_keys]
    S = arrs[0].shape[0]
    nr, npar, nc, na = len(rows), len(params), len(cts), len(aarrs)
    widths = [specs[k].block_shape[1] for k in wrt_rows]

    def body(*refs):
        first = jnp.logical_and(pl.program_id(0) == 0, pl.program_id(1) == 0)
        rv = [r[...].astype(f32) for r in refs[:nr]]
        pv = [p[...] for p in refs[nr:nr + npar]]
        cv = tuple(c[...].astype(f32) for c in refs[nr + npar:nr + npar + nc])
        av = dict(zip(add_keys, refs[nr + npar + nc:nr + npar + nc + na]))
        o = refs[nr + npar + nc + na:]
        _, vjp = jax.vjp(fn, *rv, *pv)
        g = vjp(cv)
        for n, k in enumerate(wrt_rows):
            val = g[k]
            if k in av:
                val = val + av[k][...]
            o[n][...] = val
        for n, k in enumerate(wrt_params):
            ref = o[len(wrt_rows) + n]

            @pl.when(first)
            def _(ref=ref):
                ref[...] = jnp.zeros_like(ref)

            ref[...] += g[nr + k]

    res = pl.pallas_call(
        body, grid=(S // tm, ncol),
        in_specs=list(specs) + [_full_spec(p) for p in params] + list(cspecs)
        + [pl.BlockSpec((tm, a.shape[1] // ncol), lambda i, j: (i, j)) for a in aarrs],
        out_specs=[pl.BlockSpec((tm, w), lambda i, j: (i, j)) for w in widths] + [_full_spec(params[k]) for k in wrt_params],
        out_shape=[SDS((S, w * ncol), f32) for w in widths] + [SDS(params[k].shape, f32) for k in wrt_params],
        name=name, compiler_params=_params("arbitrary", "arbitrary"))(*arrs, *params, *carrs, *aarrs)
    return res[:len(wrt_rows)], res[len(wrt_rows):]


def _rms(x, w):
    return x * lax.rsqrt(jnp.mean(x * x, axis=-1, keepdims=True) + EPS) * w


def _rms_fn(x, w):
    return (_rms(x, w),)


def _res_rms_fn(f, res, w):
    return (res + _rms(f, w),)


def _swap_halves(x):
    lane = lax.broadcasted_iota(jnp.int32, x.shape, 1)
    first = (lane % HEAD_DIM) < (HEAD_DIM // 2)
    n = x.shape[1]
    return jnp.where(first, pltpu.roll(x, n - HEAD_DIM // 2, 1), pltpu.roll(x, HEAD_DIM // 2, 1))


def _rope_fwd_fn(q, k, cos, sgn_sin):
    scale = HEAD_DIM ** -0.5
    return ((q * cos + _swap_halves(q) * sgn_sin) * scale, k * cos + _swap_halves(k) * sgn_sin)


def _rope_bwd_fn(dq, dk, cos, sgn_sin):
    dq = dq * (HEAD_DIM ** -0.5)
    return (dq * cos + _swap_halves(dq * sgn_sin), dk * cos + _swap_halves(dk * sgn_sin))


def _nt(a, b):
    return lax.dot_general(a, b, (((1,), (1,)), ((), ())), preferred_element_type=f32)


def _tn(a, b):
    return lax.dot_general(a, b, (((0,), (0,)), ((), ())), preferred_element_type=f32)


def _band_rows(j, d, nb):
    r, i = j // nb, j % nb
    if d == 1:
        cur = pl.ds(pl.multiple_of(i * 128, 128), 128)
        prev = pl.ds(pl.multiple_of(jnp.maximum(i - 1, 0) * 128, 128), 128)
    else:
        cur = pl.ds(i * (128 * d) + r, 128, stride=d)
        prev = pl.ds(jnp.maximum(i - 1, 0) * (128 * d) + r, 128, stride=d)
    a = lax.broadcasted_iota(jnp.int32, (128, 128), 0)
    c = lax.broadcasted_iota(jnp.int32, (128, 128), 1)
    return cur, prev, c <= a, jnp.logical_and(c >= a, i != 0)


def _attn_fwd(qr, kr, proj, name):
    S = qr.shape[0]
    nblk = S // 128

    def body(q_ref, k_ref, v_ref, out_ref, lse_ref, *scr):
        head_a = lax.broadcasted_iota(jnp.int32, (1, 128), 1) < HEAD_DIM
        for b, (d, nb) in enumerate(BRANCHES):
            ob_ref, lb_ref = scr[2 * b], scr[2 * b + 1]

            def blk(j, carry, d=d, nb=nb, ob_ref=ob_ref, lb_ref=lb_ref):
                cur, prev, mc, mp = _band_rows(j, d, nb)
                q = q_ref[cur, :]
                kc, kp = k_ref[cur, :].astype(bf16), k_ref[prev, :].astype(bf16)
                vc, vp = v_ref[cur, :].astype(bf16), v_ref[prev, :].astype(bf16)
                res = []
                for m in (head_a, jnp.logical_not(head_a)):
                    qm = jnp.where(m, q, 0.0).astype(bf16)
                    sc = jnp.where(mc, _nt(qm, kc), NEG)
                    sp = jnp.where(mp, _nt(qm, kp), NEG)
                    mx = jnp.maximum(jnp.max(sc, axis=1, keepdims=True), jnp.max(sp, axis=1, keepdims=True))
                    pc, pp = jnp.exp(sc - mx), jnp.exp(sp - mx)
                    l = jnp.sum(pc, axis=1, keepdims=True) + jnp.sum(pp, axis=1, keepdims=True)
                    o = (jnp.dot(pc.astype(bf16), vc, preferred_element_type=f32)
                         + jnp.dot(pp.astype(bf16), vp, preferred_element_type=f32)) / l
                    res.append((o, mx + jnp.log(l)))
                ob_ref[cur, :] = jnp.where(head_a, res[0][0], res[1][0])
                lb_ref[cur, :] = jnp.where(head_a, res[0][1], res[1][1])
                return carry

            lax.fori_loop(0, nblk, blk, 0)
        l0, l1, l2 = scr[1][...], scr[3][...], scr[5][...]
        mx = jnp.maximum(jnp.maximum(l0, l1), l2)
        e0, e1, e2 = jnp.exp(l0 - mx), jnp.exp(l1 - mx), jnp.exp(l2 - mx)
        den = e0 + e1 + e2
        out_ref[...] = ((e0 * scr[0][...] + e1 * scr[2][...] + e2 * scr[4][...]) / den).astype(out_ref.dtype)
        lse_ref[...] = mx + jnp.log(den)

    pair = pl.BlockSpec((S, 128), lambda h: (0, h))
    return pl.pallas_call(
        body, grid=(N_HEADS_A // 2,),
        in_specs=[pair, pair, pl.BlockSpec((S, 128), lambda h: (0, 2 * ATTN_W // 128 + h))], out_specs=[pair, pair],
        out_shape=[SDS((S, ATTN_W), bf16), SDS((S, ATTN_W), f32)], scratch_shapes=[pltpu.VMEM((S, 128), f32)] * 6,
        name=name, compiler_params=_params("parallel"))(qr, kr, proj)


def _attn_bwd(qr, kr, proj, dmix_in, out, lse, name):
    S = qr.shape[0]
    nblk = S // 128

    def body(q_ref, k_ref, v_ref, do_ref, out_ref, lse_ref, dq_ref, dk_ref, dv_ref, t_ref):
        head_a = lax.broadcasted_iota(jnp.int32, (1, 128), 1) < HEAD_DIM
        x = do_ref[...] * out_ref[...].astype(f32)
        t_ref[...] = jnp.where(head_a, jnp.sum(jnp.where(head_a, x, 0.0), axis=1, keepdims=True),
                               jnp.sum(jnp.where(head_a, 0.0, x), axis=1, keepdims=True))
        dq_ref[...] = jnp.zeros_like(dq_ref)
        dk_ref[...] = jnp.zeros_like(dk_ref)
        dv_ref[...] = jnp.zeros_like(dv_ref)
        for d, nb in BRANCHES:
            def blk(j, carry, d=d, nb=nb):
                cur, prev, mc, mp = _band_rows(j, d, nb)
                q, do = q_ref[cur, :], do_ref[cur, :]
                kc, kp = k_ref[cur, :].astype(bf16), k_ref[prev, :].astype(bf16)
                vc, vp = v_ref[cur, :].astype(bf16), v_ref[prev, :].astype(bf16)
                t, lse_b = t_ref[cur, :], lse_ref[cur, :]
                dq = dkc = dkp = dvc = dvp = jnp.zeros((128, 128), f32)
                for m, off in ((head_a, 0), (jnp.logical_not(head_a), HEAD_DIM)):
                    th, lh = t[:, off:off + 1], lse_b[:, off:off + 1]
                    qm = jnp.where(m, q, 0.0).astype(bf16)
                    dom = jnp.where(m, do, 0.0).astype(bf16)
                    pc = jnp.exp(jnp.where(mc, _nt(qm, kc), NEG) - lh)
                    pp = jnp.exp(jnp.where(mp, _nt(qm, kp), NEG) - lh)
                    dsc = (pc * (_nt(dom, vc) - th)).astype(bf16)
                    dsp = (pp * (_nt(dom, vp) - th)).astype(bf16)
                    dq = dq + jnp.where(m, jnp.dot(dsc, kc, preferred_element_type=f32)
                                        + jnp.dot(dsp, kp, preferred_element_type=f32), 0.0)
                    dvc, dvp = dvc + _tn(pc.astype(bf16), dom), dvp + _tn(pp.astype(bf16), dom)
                    dkc, dkp = dkc + _tn(dsc, qm), dkp + _tn(dsp, qm)
                dq_ref[cur, :] += dq
                dk_ref[cur, :] += dkc
                dv_ref[cur, :] += dvc
                dk_ref[prev, :] += dkp
                dv_ref[prev, :] += dvp
                return carry

            lax.fori_loop(0, nblk, blk, 0)

    pair = pl.BlockSpec((S, 128), lambda h: (0, h))
    return pl.pallas_call(
        body, grid=(N_HEADS_A // 2,),
        in_specs=[pair, pair, pl.BlockSpec((S, 128), lambda h: (0, 2 * ATTN_W // 128 + h)), pair, pair, pair],
        out_specs=[pair] * 3, out_shape=[SDS((S, ATTN_W), f32)] * 3, scratch_shapes=[pltpu.VMEM((S, 128), f32)],
        name=name, compiler_params=_params("parallel"))(qr, kr, proj, dmix_in, out, lse)


def _conv_val(x, w, K, rows):
    acc = x * w[K - 1:K, :]
    for s in range(1, K):
        acc = acc + jnp.where(rows >= s, pltpu.roll(x, s, 0), 0.0) * w[K - 1 - s:K - s, :]
    return acc


def _colconv_fwd(xs, ws, bs, K, fn, nblk, tc, outs, name):
    S = xs[0][0].shape[0]
    n = len(xs)
    has_b = bs is not None

    def body(*refs):
        rows = lax.broadcasted_iota(jnp.int32, (S, tc), 0)
        cs = []
        for k in range(n):
            c = _conv_val(refs[k][...], refs[n + k][...], K, rows)
            if has_b:
                c = c + refs[2 * n + k][...]
            cs.append(c)
        for o_ref, val in zip(refs[(3 if has_b else 2) * n:], fn(*cs)):
            o_ref[...] = val.astype(o_ref.dtype)

    def cspec(rows_, cb0):
        return pl.BlockSpec((rows_, tc), lambda j, cb0=cb0: (0, cb0 + j))

    in_specs = [cspec(S, cb) for _, cb in xs] + [cspec(K, cb) for _, cb in ws]
    args = [a for a, _ in xs] + [a for a, _ in ws]
    if has_b:
        in_specs += [cspec(1, cb) for _, cb in bs]
        args += [a for a, _ in bs]
    return pl.pallas_call(
        body, grid=(nblk,), in_specs=in_specs, out_specs=[cspec(S, 0) for _ in outs],
        out_shape=[SDS((S, nblk * tc), dt) for dt in outs], name=name, compiler_params=_params("parallel"))(*args)


def _colconv_bwd(xs, ws, bs, K, fn, douts, nblk, tc, name):
    S = xs[0][0].shape[0]
    n, nd = len(xs), len(douts)
    has_b = bs is not None
    nin = (3 if has_b else 2) * n

    def body(*refs):
        rows = lax.broadcasted_iota(jnp.int32, (S, tc), 0)
        x = [refs[k][...] for k in range(n)]
        w = [refs[n + k][...] for k in range(n)]
        cs = []
        for k in range(n):
            c = _conv_val(x[k], w[k], K, rows)
            if has_b:
                c = c + refs[2 * n + k][...]
            cs.append(c)
        _, vjp = jax.vjp(fn, *cs)
        dcs = vjp(tuple(r[...].astype(f32) for r in refs[nin:nin + nd]))
        o = refs[nin + nd:]
        for k in range(n):
            dc = dcs[k]
            dx = dc * w[k][K - 1:K, :]
            o[n + k][K - 1:K, :] = jnp.sum(dc * x[k], axis=0, keepdims=True)
            for s in range(1, K):
                dx = dx + jnp.where(rows < S - s, pltpu.roll(dc, S - s, 0), 0.0) * w[k][K - 1 - s:K - s, :]
                xsh = jnp.where(rows >= s, pltpu.roll(x[k], s, 0), 0.0)
                o[n + k][K - 1 - s:K - s, :] = jnp.sum(dc * xsh, axis=0, keepdims=True)
            o[k][...] = dx
            if has_b:
                o[2 * n + k][...] = jnp.sum(dc, axis=0, keepdims=True)

    def cspec(rows_, cb0):
        return pl.BlockSpec((rows_, tc), lambda j, cb0=cb0: (0, cb0 + j))

    in_specs = [cspec(S, cb) for _, cb in xs] + [cspec(K, cb) for _, cb in ws]
    args = [a for a, _ in xs] + [a for a, _ in ws]
    if has_b:
        in_specs += [cspec(1, cb) for _, cb in bs]
        args += [a for a, _ in bs]
    in_specs += [cspec(S, 0) for _ in douts]
    args += list(douts)
    W = nblk * tc
    out_specs = [cspec(S, 0)] * n + [cspec(K, 0)] * n + ([cspec(1, 0)] * n if has_b else [])
    out_shape = [SDS((S, W), f32)] * n + [SDS((K, W), f32)] * n + ([SDS((1, W), f32)] * n if has_b else [])
    res = pl.pallas_call(body, grid=(nblk,), in_specs=in_specs, out_specs=out_specs, out_shape=out_shape,
                         name=name, compiler_params=_params("parallel"))(*args)
    return res[:n], res[n:2 * n], res[2 * n:]


def _silu_fn(c):
    return (c * jax.nn.sigmoid(c),)


def _geglu_fn(gate, up):
    gelu = 0.5 * gate * (1.0 + jnp.tanh(0.7978845608028654 * (gate + 0.044715 * gate * gate * gate)))
    return (gelu * up,)


def _softplus(x):
    u = jnp.exp(jnp.minimum(x, 20.0))
    small = u * (1.0 - 0.5 * u)
    return jnp.where(x > 20.0, x, jnp.where(u < 1e-4, small, jnp.log(1.0 + u)))


def _bmm(a, b, precision=None):
    return lax.dot_general(a, b, (((2,), (1,)), ((0,), (0,))), precision=precision, preferred_element_type=f32)


def _bnt(a, b):
    return lax.dot_general(a, b, (((2,), (2,)), ((0,), (0,))), preferred_element_type=f32)


def _btn(a, b):
    return lax.dot_general(a, b, (((1,), (1,)), ((0,), (0,))), preferred_element_type=f32)


def _unit_lower_inverse(A):
    n = A.shape[-1]
    eye = (lax.broadcasted_iota(jnp.int32, (1, n, n), 1) == lax.broadcasted_iota(jnp.int32, (1, n, n), 2)).astype(f32)
    P = -A
    T = eye + P
    for _ in range(5):
        P = _bmm(P, P, HI)
        T = T + _bmm(T, P, HI)
    return T


def _dn_prep_fn(q, k, v, ba, alog, dtb, h):
    G, C = q.shape[0], CHUNK
    lane = lax.broadcasted_iota(jnp.int32, (1, 1, 128), 2)

    def sel(arr, idx):
        return jnp.sum(jnp.where(lane == idx, arr, 0.0), axis=-1, keepdims=True)

    beta = jax.nn.sigmoid(sel(ba, h))
    g = -jnp.exp(sel(alog[None], h)) * _softplus(sel(ba, N_HEADS_D + h) + sel(dtb[None], h))
    qn = q * lax.rsqrt(jnp.sum(q * q, axis=-1, keepdims=True) + EPS) * (DK ** -0.5)
    kn = k * lax.rsqrt(jnp.sum(k * k, axis=-1, keepdims=True) + EPS)
    ii = lax.broadcasted_iota(jnp.int32, (1, C, C), 1)
    jj = lax.broadcasted_iota(jnp.int32, (1, C, C), 2)
    tril, strict = ii >= jj, ii > jj
    gsq = jnp.broadcast_to(g, (G, C, C))
    gcol = _bmm(jnp.broadcast_to(tril.astype(f32), (G, C, C)), gsq, HI)
    grow = _bmm(jnp.ones((G, C, C), f32), jnp.where(ii <= jj, gsq, 0.0), HI)
    decay = jnp.exp(jnp.where(tril, gcol - grow, NEG))
    gc = gcol[:, :, :1]
    glast = gcol[:, C - 1:C, :1]
    kb = kn * beta
    A = jnp.where(strict, _bnt(kb.astype(bf16), kn.astype(bf16)) * decay, 0.0)
    T = _unit_lower_inverse(A).astype(bf16)
    u = _bmm(T, (v * beta).astype(bf16))
    w = _bmm(T, (kb * jnp.exp(gc)).astype(bf16))
    qk = _bnt(qn.astype(bf16), kn.astype(bf16)) * decay
    qd = qn * jnp.exp(gc)
    kd = kn * jnp.exp(glast - gc)
    return u, w, qk, qd, kd, jnp.broadcast_to(jnp.exp(glast), (G, C, DK))


def _dn_scan_fn(u, w, qk, qd, kd, eg, St):
    b = lambda a: a.astype(bf16)
    vnew = u - _bmm(b(w), b(St))
    o = _bmm(b(qd), b(St)) + _bmm(b(qk), b(vnew))
    return o, St * eg[:, :1, :] + _btn(b(kd), b(vnew))


def _dn_post_fn(o, z, nw):
    return (_rms(o, nw) * (z * jax.nn.sigmoid(z)),)


DN_GROUP = 8


def _dn_prep_specs(S, rows):
    def col(first):
        return pl.BlockSpec((rows, DK), lambda i, h, first=first: (i, first // DK + h))

    par = pl.BlockSpec((1, 128), lambda i, h: (0, 0))
    return [col(0), col(N_HEADS_D * DK), col(2 * N_HEADS_D * DK),
            pl.BlockSpec((rows, 128), lambda i, h: (i, 3584 // 128)), par, par]


def _dn_prep(qkv, proj, alog, dtb, name):
    S = qkv.shape[0]
    G = DN_GROUP
    rows = G * CHUNK

    def body(q_ref, k_ref, v_ref, ba_ref, al_ref, dt_ref, u_ref, w_ref, qk_ref, qd_ref, kd_ref, eg_ref):
        h = pl.program_id(1)
        r3 = lambda ref: ref[...].reshape(G, CHUNK, 128)
        u, w, qk, qd, kd, eg = _dn_prep_fn(r3(q_ref), r3(k_ref), r3(v_ref), r3(ba_ref), al_ref[...], dt_ref[...], h)
        for ref, val in ((u_ref, u), (w_ref, w), (qd_ref, qd), (kd_ref, kd), (eg_ref, eg)):
            ref[...] = val.reshape(rows, DK)
        qk_ref[:, :CHUNK] = qk.reshape(rows, CHUNK)
        qk_ref[:, CHUNK:] = jnp.zeros((rows, DK - CHUNK), f32)

    out = pl.BlockSpec((rows, DK), lambda i, h: (i, h))
    return pl.pallas_call(
        body, grid=(S // rows, N_HEADS_D), in_specs=_dn_prep_specs(S, rows), out_specs=[out] * 6,
        out_shape=[SDS((S, N_HEADS_D * DK), f32)] * 6, name=name,
        compiler_params=_params("parallel", "parallel"))(qkv, qkv, qkv, proj, alog, dtb)


def _dn_prep_bwd(qkv, proj, alog, dtb, cts, name):
    S = qkv.shape[0]
    G = DN_GROUP
    rows = G * CHUNK

    def body(q_ref, k_ref, v_ref, ba_ref, al_ref, dt_ref, du_ref, dw_ref, dqk_ref, dqd_ref, dkd_ref, deg_ref,
             dq_ref, dk_ref, dv_ref, dba_ref, dal_ref, ddt_ref):
        i, h = pl.program_id(0), pl.program_id(1)
        r3 = lambda ref: ref[...].reshape(G, CHUNK, 128)
        _, vjp = jax.vjp(lambda q, k, v, ba, al, dt: _dn_prep_fn(q, k, v, ba, al, dt, h),
                         r3(q_ref), r3(k_ref), r3(v_ref), r3(ba_ref), al_ref[...], dt_ref[...])
        dqk = dqk_ref[:, :CHUNK].reshape(G, CHUNK, CHUNK)
        dq, dk, dv, dba, dal, ddt = vjp((r3(du_ref), r3(dw_ref), dqk, r3(dqd_ref), r3(dkd_ref), r3(deg_ref)))
        dq_ref[...] = dq.reshape(rows, DK)
        dk_ref[...] = dk.reshape(rows, DK)
        dv_ref[...] = dv.reshape(rows, DK)

        @pl.when(h == 0)
        def _():
            dba_ref[...] = jnp.zeros_like(dba_ref)

        @pl.when(jnp.logical_and(i == 0, h == 0))
        def _():
            dal_ref[...] = jnp.zeros_like(dal_ref)
            ddt_ref[...] = jnp.zeros_like(ddt_ref)

        dba_ref[...] += dba.reshape(rows, 128)
        dal_ref[...] += dal
        ddt_ref[...] += ddt

    hcol = pl.BlockSpec((rows, DK), lambda i, h: (i, h))
    par = pl.BlockSpec((1, 128), lambda i, h: (0, 0))
    W = N_HEADS_D * DK
    return pl.pallas_call(
        body, grid=(S // rows, N_HEADS_D), in_specs=_dn_prep_specs(S, rows) + [hcol] * 6,
        out_specs=[hcol] * 3 + [pl.BlockSpec((rows, 128), lambda i, h: (i, 0)), par, par],
        out_shape=[SDS((S, W), f32)] * 3 + [SDS((S, 128), f32), SDS((1, 128), f32), SDS((1, 128), f32)], name=name,
        compiler_params=_params("arbitrary", "arbitrary"))(qkv, qkv, qkv, proj, alog, dtb, *cts)


def _heads(x):
    return jnp.stack([x[:, DK * h:DK * (h + 1)] for h in range(N_HEADS_D)])


def _dn_scan(pre, name):
    S = pre[0].shape[0]
    NCH = S // CHUNK

    def body(u_ref, w_ref, qk_ref, qd_ref, kd_ref, eg_ref, o_ref, st_ref, s_ref):
        @pl.when(pl.program_id(0) == 0)
        def _():
            s_ref[...] = jnp.zeros_like(s_ref)

        St = s_ref[...]
        st_ref[0] = St
        o, Sn = _dn_scan_fn(_heads(u_ref[...]), _heads(w_ref[...]), _heads(qk_ref[...])[:, :, :CHUNK], _heads(qd_ref[...]),
                            _heads(kd_ref[...]), _heads(eg_ref[...]), St)
        for h in range(N_HEADS_D):
            o_ref[:, DK * h:DK * (h + 1)] = o[h]
        s_ref[...] = Sn

    blk = pl.BlockSpec((CHUNK, N_HEADS_D * DK), lambda n: (n, 0))
    return pl.pallas_call(
        body, grid=(NCH,), in_specs=[blk] * 6,
        out_specs=[blk, pl.BlockSpec((1, N_HEADS_D, DK, DK), lambda n: (n, 0, 0, 0))],
        out_shape=[SDS((S, N_HEADS_D * DK), f32), SDS((NCH, N_HEADS_D, DK, DK), f32)],
        scratch_shapes=[pltpu.VMEM((N_HEADS_D, DK, DK), f32)], name=name, compiler_params=_params("arbitrary"))(*pre)


def _dn_scan_bwd(pre, states, do, name):
    S = do.shape[0]
    NCH = S // CHUNK

    def body(u_ref, w_ref, qk_ref, qd_ref, kd_ref, eg_ref, st_ref, do_ref,
             du_ref, dw_ref, dqk_ref, dqd_ref, dkd_ref, deg_ref, ds_ref):
        @pl.when(pl.program_id(0) == 0)
        def _():
            ds_ref[...] = jnp.zeros_like(ds_ref)

        _, vjp = jax.vjp(_dn_scan_fn, _heads(u_ref[...]), _heads(w_ref[...]), _heads(qk_ref[...])[:, :, :CHUNK],
                         _heads(qd_ref[...]), _heads(kd_ref[...]), _heads(eg_ref[...]), st_ref[0])
        du, dw, dqk, dqd, dkd, deg, dS = vjp((_heads(do_ref[...]), ds_ref[...]))
        ds_ref[...] = dS
        for h in range(N_HEADS_D):
            c = slice(DK * h, DK * (h + 1))
            for ref, val in ((du_ref, du), (dw_ref, dw), (dqd_ref, dqd), (dkd_ref, dkd), (deg_ref, deg)):
                ref[:, c] = val[h]
            dqk_ref[:, DK * h:DK * h + CHUNK] = dqk[h]
            dqk_ref[:, DK * h + CHUNK:DK * (h + 1)] = jnp.zeros((CHUNK, DK - CHUNK), f32)

    blk = pl.BlockSpec((CHUNK, N_HEADS_D * DK), lambda n: (NCH - 1 - n, 0))
    return pl.pallas_call(
        body, grid=(NCH,),
        in_specs=[blk] * 6 + [pl.BlockSpec((1, N_HEADS_D, DK, DK), lambda n: (NCH - 1 - n, 0, 0, 0)), blk],
        out_specs=[blk] * 6, out_shape=[SDS((S, N_HEADS_D * DK), f32)] * 6,
        scratch_shapes=[pltpu.VMEM((N_HEADS_D, DK, DK), f32)], name=name,
        compiler_params=_params("arbitrary"))(*pre, states, do)


def _loss_head(y, t, name):
    S, D = y.shape
    tm = ROW_TILE

    def body(y_ref, t_ref, dy_ref, l_ref):
        i = pl.program_id(0)
        d = y_ref[...] - t_ref[...]
        dy_ref[...] = d * (1.0 / D)
        part = jnp.sum(jnp.sum(d * d, axis=1, keepdims=True), axis=0, keepdims=True) * (0.5 / D)

        @pl.when(i == 0)
        def _():
            l_ref[...] = jnp.zeros_like(l_ref)

        l_ref[...] += jnp.broadcast_to(part, l_ref.shape)

    spec = pl.BlockSpec((tm, D), lambda i: (i, 0))
    dy, l = pl.pallas_call(body, grid=(S // tm,), in_specs=[spec, spec],
                           out_specs=[spec, pl.BlockSpec((1, 128), lambda i: (0, 0))],
                           out_shape=[SDS((S, D), f32), SDS((1, 128), f32)], name=name,
                           compiler_params=_params("arbitrary"))(y, t)
    return l[0, 0], dy


def _adamw(w, g, m, v, tr, name):
    L, R, C = w.shape
    assert R % tr == 0

    def body(w_ref, g_ref, m_ref, v_ref, d_ref, mo_ref, vo_ref):
        gv = g_ref[...]
        m2 = ADAM_B1 * m_ref[...] + (1.0 - ADAM_B1) * gv
        v2 = ADAM_B2 * v_ref[...] + (1.0 - ADAM_B2) * (gv * gv)
        m_hat = m2 / (1.0 - ADAM_B1 ** ADAM_STEP)
        v_hat = v2 / (1.0 - ADAM_B2 ** ADAM_STEP)
        d_ref[...] = -ADAM_LR * (m_hat / (jnp.sqrt(v_hat) + ADAM_EPS) + ADAM_WD * w_ref[...])
        mo_ref[...] = m2
        vo_ref[...] = v2

    spec = pl.BlockSpec((1, tr, C), lambda l, i: (l, i, 0))
    return pl.pallas_call(body, grid=(L, R // tr), in_specs=[spec] * 4, out_specs=[spec] * 3,
                          out_shape=[SDS((L, R, C), f32)] * 3, name=name,
                          compiler_params=_params("parallel", "parallel"))(w, g, m, v)


def _rope_tables(S):
    inv = 1.0 / (10000.0 ** (jnp.arange(0, HEAD_DIM, 2, dtype=f32) / HEAD_DIM))
    ang = jnp.arange(S, dtype=f32)[:, None] * inv[None, :]
    cos, sin = jnp.cos(ang), jnp.sin(ang)
    return (jnp.tile(jnp.concatenate([cos, cos], axis=1), (1, N_HEADS_A)),
            jnp.tile(jnp.concatenate([-sin, sin], axis=1), (1, N_HEADS_A)))


def _layer_fwd(x, W, cos, sgn_sin, l):
    n = f"l{l}_"
    (h1,) = _rows(_rms_fn, [x], [W["norm_pre_mix"]], [(D_MODEL, bf16)], n + "pre_mix_norm")
    proj = _mm(h1, W["w_in"], "nn", 512, 768, f32, n + "in_proj")
    qr, kr = _rows(_rope_fwd_fn, [(proj, ATTN_W, 0), (proj, ATTN_W, 1), cos, sgn_sin], [],
                   [(ATTN_W, f32), (ATTN_W, f32)], n + "rope")
    attn_out, lse = _attn_fwd(qr, kr, proj, n + "attn_fwd")
    (qkv,) = _colconv_fwd([(proj, 3)], [(W["dn_conv_w"], 0)], None, 4, _silu_fn, 3, 512, [f32], n + "dn_conv")
    dn_pre = _dn_prep(qkv, proj, W["dn_a_log"], W["dn_dt_bias"], n + "dn_prep")
    dn_o, dn_states = _dn_scan(dn_pre, n + "dn_scan")
    (dn_out,) = _rows(_dn_post_fn, [(dn_o, DK, 0), (proj, DK, 3072 // DK)], [W["dn_norm_w"]], [(DK, bf16)], n + "dn_post",
                      ncol=N_HEADS_D)
    mix_in = jnp.concatenate([attn_out, dn_out], axis=1)
    mix = _mm(mix_in, W["w_out"], "nn", 512, 512, f32, n + "out_proj")
    (x1,) = _rows(_res_rms_fn, [mix, x], [W["norm_post_mix"]], [(D_MODEL, f32)], n + "post_mix_norm")
    (h2,) = _rows(_rms_fn, [x1], [W["norm_pre_ffn"]], [(D_MODEL, bf16)], n + "pre_ffn_norm")
    u0 = _mm(h2, W["ffn_w_in"], "nn", 512, 512, f32, n + "ffn_in")
    nb_ff = D_FF // 256
    (act,) = _colconv_fwd([(u0, 0), (u0, nb_ff)], [(W["ffn_conv_w"], 0), (W["ffn_conv_w"], nb_ff)],
                          [(W["ffn_conv_b"], 0), (W["ffn_conv_b"], nb_ff)], 3, _geglu_fn, nb_ff, 256, [bf16],
                          n + "ffn_conv_glu")
    f = _mm(act, W["ffn_w_out"], "nn", 512, 512, f32, n + "ffn_out")
    (x2,) = _rows(_res_rms_fn, [f, x1], [W["norm_post_ffn"]], [(D_MODEL, f32)], n + "post_ffn_norm")
    saved = dict(x=x, h1=h1, proj=proj, qr=qr, kr=kr, attn_out=attn_out, lse=lse, qkv=qkv, dn_pre=dn_pre, dn_o=dn_o,
                 dn_states=dn_states, mix_in=mix_in, mix=mix, x1=x1, h2=h2, u0=u0, act=act, f=f)
    return x2, saved


def _layer_bwd(dx2, sv, W, cos, sgn_sin, l):
    n = f"l{l}_"
    S = dx2.shape[0]
    g = {}
    (df,), (g["norm_post_ffn"],) = _rows_vjp(_rms_fn, [sv["f"]], [W["norm_post_ffn"]], [dx2], [0], [0],
                                             n + "post_ffn_norm_bwd")
    dact = _mm(df, W["ffn_w_out"], "nt", 512, 1408, f32, n + "ffn_out_dx")
    g["ffn_w_out"] = _mm(sv["act"], df, "tn", 256, 1024, f32, n + "ffn_out_dw")
    nb_ff = D_FF // 256
    u0 = sv["u0"]
    dxs, dws, dbs = _colconv_bwd([(u0, 0), (u0, nb_ff)], [(W["ffn_conv_w"], 0), (W["ffn_conv_w"], nb_ff)],
                                 [(W["ffn_conv_b"], 0), (W["ffn_conv_b"], nb_ff)], 3, _geglu_fn, [dact], nb_ff, 256,
                                 n + "ffn_conv_glu_bwd")
    du0 = jnp.concatenate(dxs, axis=1)
    g["ffn_conv_w"] = jnp.concatenate(dws, axis=1)
    g["ffn_conv_b"] = jnp.concatenate(dbs, axis=1)
    dh2 = _mm(du0, W["ffn_w_in"], "nt", 256, 512, f32, n + "ffn_in_dx")
    g["ffn_w_in"] = _mm(sv["h2"], du0, "tn", 512, D_FF // 2, f32, n + "ffn_in_dw", column_shards=True)
    (dx1,), (g["norm_pre_ffn"],) = _rows_vjp(_rms_fn, [sv["x1"]], [W["norm_pre_ffn"]], [dh2], [0], [0],
                                             n + "pre_ffn_norm_bwd", adds={0: dx2})
    (dmix,), (g["norm_post_mix"],) = _rows_vjp(_rms_fn, [sv["mix"]], [W["norm_post_mix"]], [dx1], [0], [0],
                                               n + "post_mix_norm_bwd")
    dmix_in = _mm(dmix, W["w_out"], "nt", 512, 512, f32, n + "out_proj_dx")
    g["w_out"] = _mm(sv["mix_in"], dmix, "tn", 512, 512, f32, n + "out_proj_dw")

    (ddn_o, dz), (g["dn_norm_w"],) = _rows_vjp(
        _dn_post_fn, [(sv["dn_o"], DK, 0), (sv["proj"], DK, 3072 // DK)], [W["dn_norm_w"]], [(dmix_in, DK, ATTN_W // DK)],
        [0, 1], [0], n + "dn_post_bwd", ncol=N_HEADS_D)
    dpre = _dn_scan_bwd(sv["dn_pre"], sv["dn_states"], ddn_o, n + "dn_scan_bwd")
    dq, dk, dv, dba, g["dn_a_log"], g["dn_dt_bias"] = _dn_prep_bwd(
        sv["qkv"], sv["proj"], W["dn_a_log"], W["dn_dt_bias"], dpre, n + "dn_prep_bwd")
    dqkv = jnp.concatenate([dq, dk, dv], axis=1)
    (dqkv0,), (g["dn_conv_w"],), _ = _colconv_bwd([(sv["proj"], 3)], [(W["dn_conv_w"], 0)], None, 4, _silu_fn,
                                                 [dqkv], 3, 512, n + "dn_conv_bwd")

    dqr, dkr, dav = _attn_bwd(sv["qr"], sv["kr"], sv["proj"], dmix_in, sv["attn_out"], sv["lse"], n + "attn_bwd")
    daq, dak = _rows(_rope_bwd_fn, [dqr, dkr, cos, sgn_sin], [], [(ATTN_W, f32)] * 2, n + "rope_bwd")
    dproj = jnp.concatenate([daq, dak, dav, dqkv0, dz, dba, jnp.zeros((S, PROJ_W - 3712), f32)], axis=1)
    dh1 = _mm(dproj, W["w_in"], "nt", 256, 512, f32, n + "in_proj_dx")
    g["w_in"] = _mm(sv["h1"], dproj, "tn", 512, 768, f32, n + "in_proj_dw")
    (dx,), (g["norm_pre_mix"],) = _rows_vjp(_rms_fn, [sv["x"]], [W["norm_pre_mix"]], [dh1], [0], [0],
                                            n + "pre_mix_norm_bwd", adds={0: dx1})
    return dx, g


def _local_step(x, target, layers):
    cos, sgn_sin = _rope_tables(x.shape[0])
    saved = []
    for l, W in enumerate(layers):
        x, sv = _layer_fwd(x, W, cos, sgn_sin, l)
        saved.append(sv)
    loss, dx = _loss_head(x, target, "loss_head")
    grads = [None] * len(layers)
    for l in reversed(range(len(layers))):
        dx, grads[l] = _layer_bwd(dx, saved[l], layers[l], cos, sgn_sin, l)
    return loss, dx, grads


def _pos():
    x, y, c = lax.axis_index("x"), lax.axis_index("y"), lax.axis_index("c")
    return x, y, c, [(1 - x, y), (x, 1 - y), (1 - x, 1 - y)]


def _rcopy(src, dst, send_sem, recv_sem, dev):
    return pltpu.make_async_remote_copy(src_ref=src, dst_ref=dst, send_sem=send_sem, recv_sem=recv_sem,
                                        device_id=dev, device_id_type=MESH)


def _half_rows(ref, h, which, axis):
    if h is None:
        return ref
    rows = pl.ds(pl.multiple_of(which * h, 16), h)
    return ref.at[:, rows, :] if axis == 1 else ref.at[rows, :]


def _dma_sems(*counts):
    return [pltpu.SemaphoreType.DMA((k,)) for k in counts]


def _all_gather(arrs, halves, name):
    n = len(arrs)

    def body(*refs):
        ins, outs = refs[:n], refs[n:2 * n]
        send1, recv1, send2, recv2 = refs[2 * n:]
        x, y, c, chips = _pos()
        me, sib, s_me = (x, y, c), (x, y, 1 - c), 2 * x + y
        sends = []
        for i in range(n):
            for j, chip in enumerate(chips):
                cp = _rcopy(_half_rows(ins[i], halves[i], c, 1), _half_rows(outs[i].at[s_me], halves[i], c, 1),
                            send1.at[3 * i + j], recv1.at[3 * i + j], (*chip, c))
                cp.start()
                sends.append(cp)
        for i in range(n):
            for j, (px, py) in enumerate(chips):
                k = 3 * i + j
                landed = _half_rows(outs[i].at[2 * px + py], halves[i], c, 1)
                _rcopy(landed, landed, send1.at[k], recv1.at[k], me).wait_recv()
                if halves[i] is not None:
                    cp = _rcopy(landed, landed, send2.at[k], recv2.at[k], sib)
                    cp.start()
                    sends.append(cp)
        for i in range(n):
            if halves[i] is None:
                continue
            for j, (px, py) in enumerate(chips):
                k = 3 * i + j
                other = _half_rows(outs[i].at[2 * px + py], halves[i], 1 - c, 1)
                _rcopy(other, other, send2.at[k], recv2.at[k], me).wait_recv()
        for cp in sends:
            cp.wait_send()

    return pl.pallas_call(
        body, in_specs=[ANY] * n, out_specs=[ANY] * n,
        out_shape=[SDS((4,) + a.shape, a.dtype) for a in arrs],
        scratch_shapes=_dma_sems(3 * n, 3 * n, 3 * n, 3 * n), name=name)(*arrs)


def _exchange_halves(gs, name):
    n = len(gs)

    def body(*refs):
        ins, outs = refs[:n], refs[n:2 * n]
        send, recv = refs[2 * n:]
        x, y, c, _ = _pos()
        sends = []
        for k in range(n):
            cp = _rcopy(_half_rows(ins[k], gs[k].shape[1] // 2, 1 - c, 1), outs[k], send.at[k], recv.at[k], (x, y, 1 - c))
            cp.start()
            sends.append(cp)
        for k in range(n):
            _rcopy(outs[k], outs[k], send.at[k], recv.at[k], (x, y, c)).wait_recv()
        for cp in sends:
            cp.wait_send()

    return pl.pallas_call(
        body, in_specs=[ANY] * n, out_specs=[ANY] * n,
        out_shape=[SDS((4, g.shape[1] // 2, g.shape[2]), g.dtype) for g in gs],
        scratch_shapes=_dma_sems(n, n), name=name)(*gs)


def _scatter_partials(ps, name):
    n = len(ps)

    def body(*refs):
        ins, outs = refs[:n], refs[n:2 * n]
        send, recv = refs[2 * n:]
        x, y, c, chips = _pos()
        sends = []
        for k in range(n):
            for j, (px, py) in enumerate(chips):
                cp = _rcopy(ins[k].at[2 * px + py], outs[k].at[j], send.at[3 * k + j], recv.at[3 * k + j], (px, py, c))
                cp.start()
                sends.append(cp)
        for k in range(n):
            for j in range(3):
                _rcopy(outs[k].at[j], outs[k].at[j], send.at[3 * k + j], recv.at[3 * k + j], (x, y, c)).wait_recv()
        for cp in sends:
            cp.wait_send()

    return pl.pallas_call(
        body, in_specs=[ANY] * n, out_specs=[ANY] * n,
        out_shape=[SDS((3,) + p.shape[1:], p.dtype) for p in ps],
        scratch_shapes=_dma_sems(3 * n, 3 * n), name=name)(*ps)


def _join_halves(rs, name):
    n = len(rs)

    def body(*refs):
        outs = refs[n:2 * n]
        send, recv = refs[2 * n:]
        x, y, c, _ = _pos()
        sends = []
        for k in range(n):
            mine = _half_rows(outs[k], rs[k].shape[0] // 2, c, 0)
            cp = _rcopy(mine, mine, send.at[k], recv.at[k], (x, y, 1 - c))
            cp.start()
            sends.append(cp)
        for k in range(n):
            other = _half_rows(outs[k], rs[k].shape[0] // 2, 1 - c, 0)
            _rcopy(other, other, send.at[k], recv.at[k], (x, y, c)).wait_recv()
        for cp in sends:
            cp.wait_send()

    return pl.pallas_call(
        body, in_specs=[ANY] * n, out_specs=[ANY] * n, out_shape=[SDS(r.shape, r.dtype) for r in rs],
        input_output_aliases={k: k for k in range(n)}, scratch_shapes=_dma_sems(n, n), name=name)(*rs)


def _all_reduce_small(pack, name):
    R = pack.shape[0]

    def body(in_ref, out_ref, buf, send, recv):
        x, y, c, _ = _pos()
        me = 4 * x + 2 * y + c
        buf[me] = in_ref[...]
        sends = []
        for k in range(1, 8):
            peer = me ^ k
            cp = _rcopy(buf.at[me], buf.at[me], send.at[k - 1], recv.at[k - 1], ((peer >> 2) & 1, (peer >> 1) & 1, peer & 1))
            cp.start()
            sends.append(cp)
        for k in range(1, 8):
            _rcopy(buf.at[me ^ k], buf.at[me ^ k], send.at[k - 1], recv.at[k - 1], (x, y, c)).wait_recv()
        for cp in sends:
            cp.wait_send()
        acc = buf[0]
        for d in range(1, 8):
            acc = acc + buf[d]
        out_ref[...] = acc

    return pl.pallas_call(
        body, out_shape=SDS((R, 128), f32),
        in_specs=[pl.BlockSpec(memory_space=pltpu.VMEM)], out_specs=pl.BlockSpec(memory_space=pltpu.VMEM),
        scratch_shapes=[pltpu.VMEM((8, R, 128), f32)] + _dma_sems(7, 7), name=name)(pack)


def _add_sibling(g, recv, c_arr, tr, name):
    _, R, C = g.shape
    h = R // 2
    nrb = h // tr
    assert h % tr == 0

    def body(c_ref, g_ref, r_ref, o_ref):
        o_ref[...] = (g_ref[...] + r_ref[...]).astype(o_ref.dtype)

    spec = pl.BlockSpec((1, tr, C), lambda s, r, c_ref: (s, r, 0))
    grid_spec = pltpu.PrefetchScalarGridSpec(
        num_scalar_prefetch=1, grid=(4, nrb),
        in_specs=[pl.BlockSpec((1, tr, C), lambda s, r, c_ref: (s, c_ref[0] * nrb + r, 0)), spec], out_specs=spec)
    return pl.pallas_call(body, grid_spec=grid_spec, out_shape=SDS((4, h, C), bf16), name=name,
                          compiler_params=_params("parallel", "parallel"))(c_arr, g, recv)


def _add_chips(p, recv, sc_arr, tr, name):
    _, h, C = p.shape
    nrb = h // tr
    assert h % tr == 0

    def body(sc_ref, p_ref, r_ref, o_ref):
        o_ref[...] = (p_ref[0].astype(f32) + r_ref[0].astype(f32)) + (r_ref[1].astype(f32) + r_ref[2].astype(f32))

    grid_spec = pltpu.PrefetchScalarGridSpec(
        num_scalar_prefetch=1, grid=(nrb,),
        in_specs=[pl.BlockSpec((1, tr, C), lambda r, sc_ref: (sc_ref[0], r, 0)),
                  pl.BlockSpec((3, tr, C), lambda r, sc_ref: (0, r, 0))],
        out_specs=pl.BlockSpec((tr, C), lambda r, sc_ref: (sc_ref[1] * nrb + r, 0)))
    return pl.pallas_call(body, grid_spec=grid_spec, out_shape=SDS((2 * h, C), f32), name=name,
                          compiler_params=_params("parallel"))(sc_arr, p, recv)


_BIG = (("w_in", 1024, 256), ("w_out", 256, 128), ("ffn_w_in", 1024, 256), ("ffn_w_out", 704, 352))
_SMALL = ("dn_conv_w", "ffn_conv_w", "ffn_conv_b", "norm_pre_mix", "norm_post_mix", "norm_pre_ffn", "norm_post_ffn",
          "dn_norm_w", "dn_a_log", "dn_dt_bias")
_WEIGHTS = ("w_in", "dn_conv_w", "dn_a_log", "dn_dt_bias", "dn_norm_w", "w_out", "ffn_w_in", "ffn_conv_w", "ffn_conv_b",
            "ffn_w_out", "norm_pre_mix", "norm_post_mix", "norm_pre_ffn", "norm_post_ffn")
_ADAM_ROWS = {"w_in": 256, "w_out": 256, "ffn_w_in": 128, "ffn_w_out": 176}


def _shard_major(name, g):
    if name == "w_in":
        return jnp.stack([g[:, 898 * s:898 * (s + 1)] for s in range(4)])
    if name == "ffn_w_in":
        return g
    return g.reshape(4, g.shape[0] // 4, g.shape[1])


def kernel(x, w_in, dn_conv_w, dn_a_log, dn_dt_bias, dn_norm_w, w_out, ffn_w_in, ffn_conv_w, ffn_conv_b, ffn_w_out, norm_pre_mix, norm_post_mix, norm_pre_ffn, norm_post_ffn, loss_target, m_w_in, m_dn_conv_w, m_dn_a_log, m_dn_dt_bias, m_dn_norm_w, m_w_out, m_ffn_w_in, m_ffn_conv_w, m_ffn_conv_b, m_ffn_w_out, m_norm_pre_mix, m_norm_post_mix, m_norm_pre_ffn, m_norm_post_ffn, v_w_in, v_dn_conv_w, v_dn_a_log, v_dn_dt_bias, v_dn_norm_w, v_w_out, v_ffn_w_in, v_ffn_conv_w, v_ffn_conv_b, v_ffn_w_out, v_norm_pre_mix, v_norm_post_mix, v_norm_pre_ffn, v_norm_post_ffn):
    w = dict(w_in=w_in, dn_conv_w=dn_conv_w, dn_a_log=dn_a_log, dn_dt_bias=dn_dt_bias, dn_norm_w=dn_norm_w, w_out=w_out,
             ffn_w_in=ffn_w_in, ffn_conv_w=ffn_conv_w, ffn_conv_b=ffn_conv_b, ffn_w_out=ffn_w_out, norm_pre_mix=norm_pre_mix,
             norm_post_mix=norm_post_mix, norm_pre_ffn=norm_pre_ffn, norm_post_ffn=norm_post_ffn)
    m = dict(w_in=m_w_in, dn_conv_w=m_dn_conv_w, dn_a_log=m_dn_a_log, dn_dt_bias=m_dn_dt_bias, dn_norm_w=m_dn_norm_w,
             w_out=m_w_out, ffn_w_in=m_ffn_w_in, ffn_conv_w=m_ffn_conv_w, ffn_conv_b=m_ffn_conv_b, ffn_w_out=m_ffn_w_out,
             norm_pre_mix=m_norm_pre_mix, norm_post_mix=m_norm_post_mix, norm_pre_ffn=m_norm_pre_ffn,
             norm_post_ffn=m_norm_post_ffn)
    v = dict(w_in=v_w_in, dn_conv_w=v_dn_conv_w, dn_a_log=v_dn_a_log, dn_dt_bias=v_dn_dt_bias, dn_norm_w=v_dn_norm_w,
             w_out=v_w_out, ffn_w_in=v_ffn_w_in, ffn_conv_w=v_ffn_conv_w, ffn_conv_b=v_ffn_conv_b, ffn_w_out=v_ffn_w_out,
             norm_pre_mix=v_norm_pre_mix, norm_post_mix=v_norm_post_mix, norm_pre_ffn=v_norm_pre_ffn,
             norm_post_ffn=v_norm_post_ffn)
    xi, yi, ci = lax.axis_index("x"), lax.axis_index("y"), lax.axis_index("c")
    s_me = 2 * xi + yi
    c_arr = jnp.reshape(ci, (1,)).astype(jnp.int32)
    sc_arr = jnp.stack([s_me, ci]).astype(jnp.int32)

    own = [w_in.astype(bf16), w_out.astype(bf16), ffn_w_in.astype(bf16), ffn_w_out.astype(bf16), dn_conv_w, ffn_conv_w]
    gathered = _all_gather(own, [512, 128, 512, 352, None, None], "weights_all_gather")
    shards = [[jnp.where(s_me == s, o, a[s]) for s in range(4)] for o, a in zip(own, gathered)]
    cat = lambda k: jnp.concatenate(shards[k], axis=-1)
    rows = lambda k: jnp.concatenate(shards[k], axis=1)
    full = dict(w_in=jnp.pad(cat(0), ((0, 0), (0, 0), (0, PROJ_W - IN_COLS))), w_out=rows(1),
                ffn_w_in=cat(2), ffn_w_out=rows(3), dn_conv_w=cat(4), ffn_conv_w=cat(5))
    lanes = lambda a: jnp.pad(a, ((0, 0), (0, 128 - a.shape[1])))
    full.update(dn_a_log=lanes(dn_a_log), dn_dt_bias=lanes(dn_dt_bias), dn_norm_w=dn_norm_w, ffn_conv_b=ffn_conv_b,
                norm_pre_mix=norm_pre_mix, norm_post_mix=norm_post_mix, norm_pre_ffn=norm_pre_ffn,
                norm_post_ffn=norm_post_ffn)
    layers = [{k: (a[l] if a.ndim == 3 else a[l:l + 1]) for k, a in full.items()} for l in range(2)]

    loss_local, dx, grads = _local_step(x[0], loss_target[0], layers)
    loss = lax.psum(loss_local, ("x", "y", "c"))

    keys = [(name, l) for name, _, _ in _BIG for l in range(2)]
    tiles = {name: tr for name, _, tr in _BIG}
    gs = [_shard_major(name, grads[l][name]) for name, l in keys]
    from_sib = _exchange_halves(gs, "grads_to_sibling")
    chip_part = [_add_sibling(g, r, c_arr, tiles[name], f"add_sibling_{name}{l}") for g, r, (name, l) in zip(gs, from_sib, keys)]
    from_chips = _scatter_partials(chip_part, "grads_to_owner_chip")
    reduced = [_add_chips(p, r, sc_arr, tiles[name], f"add_chips_{name}{l}") for p, r, (name, l) in zip(chip_part, from_chips, keys)]
    joined = _join_halves(reduced, "grads_join_halves")
    g_out = {name: jnp.stack([joined[2 * i], joined[2 * i + 1]]) for i, (name, _, _) in enumerate(_BIG)}

    small = {}
    for name in _SMALL:
        per_layer = [grads[l][name] for l in range(2)]
        if name in ("dn_a_log", "dn_dt_bias"):
            per_layer = [p[:, :N_HEADS_D] for p in per_layer]
        small[name] = jnp.stack(per_layer).reshape((2,) + (w[name].shape[1:] if name not in ("dn_conv_w", "ffn_conv_w")
                                                           else per_layer[0].shape))
    flat = jnp.concatenate([small[name].reshape(-1) for name in _SMALL])
    n_rows = -(-flat.shape[0] // 1024) * 8
    summed = _all_reduce_small(jnp.pad(flat, (0, n_rows * 128 - flat.shape[0])).reshape(n_rows, 128),
                               "small_grads_all_reduce").reshape(-1)
    off = 0
    for name in _SMALL:
        size = small[name].size
        g_out[name] = summed[off:off + size].reshape(small[name].shape)
        off += size
    g_out["dn_conv_w"] = lax.dynamic_slice_in_dim(g_out["dn_conv_w"], s_me * 384, 384, axis=2)
    g_out["ffn_conv_w"] = lax.dynamic_slice_in_dim(g_out["ffn_conv_w"], s_me * 1408, 1408, axis=2)

    deltas, new_m, new_v = {}, {}, {}
    for name in _WEIGHTS:
        shape = w[name].shape
        as3 = (lambda a: a) if len(shape) == 3 else (lambda a: a.reshape(shape[0], 1, shape[1]))
        tr = _ADAM_ROWS.get(name, as3(w[name]).shape[1])
        d_, m_, v_ = _adamw(as3(w[name]), as3(g_out[name]), as3(m[name]), as3(v[name]), tr, f"adamw_{name}")
        deltas[name], new_m[name], new_v[name] = d_.reshape(shape), m_.reshape(shape), v_.reshape(shape)

    return (loss, dx[None], *[g_out[k] for k in _WEIGHTS], *[deltas[k] for k in _WEIGHTS],
            *[new_m[k] for k in _WEIGHTS], *[new_v[k] for k in _WEIGHTS])
```

```python
import jax
import jax.numpy as jnp
from jax import lax
from jax.experimental import pallas as pl
from jax.experimental.pallas import tpu as pltpu

f32, bf16 = jnp.float32, jnp.bfloat16
SDS = jax.ShapeDtypeStruct
HI = lax.Precision.HIGH
MESH = pl.DeviceIdType.MESH
ANY = pl.BlockSpec(memory_space=pl.ANY)

D_MODEL = 1024
N_HEADS_A, HEAD_DIM = 8, 64
ATTN_W = 512
N_HEADS_D, DK = 4, 128
CHUNK = 64
D_FF = 2816
IN_COLS = 3592
PROJ_W = 3840
BRANCHES = ((1, 16), (4, 4), (16, 1))
EPS = 1e-6
NEG = -1e30
ROW_TILE = 256
VMEM_LIMIT = 56 * 1024 * 1024

ADAM_LR, ADAM_B1, ADAM_B2, ADAM_EPS, ADAM_WD, ADAM_STEP = 0.001, 0.9, 0.999, 1e-08, 0.01, 10


def _params(*sem):
    return pltpu.CompilerParams(dimension_semantics=sem, vmem_limit_bytes=VMEM_LIMIT)


def _mm(a, b, mode, tm, tn, out_dtype, name, column_shards=False):
    if mode == "nn":
        (M, K), N = a.shape, b.shape[1]
        dims = (((1,), (0,)), ((), ()))
        a_spec = pl.BlockSpec((tm, K), lambda i, j: (i, 0))
        b_spec = pl.BlockSpec((K, tn), lambda i, j: (0, j))
    elif mode == "nt":
        (M, K), N = a.shape, b.shape[0]
        dims = (((1,), (1,)), ((), ()))
        a_spec = pl.BlockSpec((tm, K), lambda i, j: (i, 0))
        b_spec = pl.BlockSpec((tn, K), lambda i, j: (j, 0))
    else:
        (K, M), N = a.shape, b.shape[1]
        dims = (((0,), (0,)), ((), ()))
        a_spec = pl.BlockSpec((K, tm), lambda i, j: (0, i))
        b_spec = pl.BlockSpec((K, tn), lambda i, j: (0, j))
    assert M % tm == 0 and N % tn == 0, (name, M, N, tm, tn)

    def body(a_ref, b_ref, o_ref):
        o_ref[...] = lax.dot_general(a_ref[...].astype(bf16), b_ref[...].astype(bf16), dims,
                                     preferred_element_type=f32).astype(o_ref.dtype)

    if column_shards:
        out_spec, out_shape = pl.BlockSpec((None, tm, tn), lambda i, j: (j, i, 0)), SDS((N // tn, M, tn), out_dtype)
    else:
        out_spec, out_shape = pl.BlockSpec((tm, tn), lambda i, j: (i, j)), SDS((M, N), out_dtype)
    return pl.pallas_call(body, grid=(M // tm, N // tn), in_specs=[a_spec, b_spec], out_specs=out_spec,
                          out_shape=out_shape, name=name, compiler_params=_params("parallel", "arbitrary"))(a, b)


def _row_spec(r, tm):
    if isinstance(r, tuple):
        arr, width, cb = r
        return arr, pl.BlockSpec((tm, width), lambda i, j, cb=cb: (i, cb + j))
    return r, pl.BlockSpec((tm, r.shape[1]), lambda i, j: (i, j))


def _full_spec(p):
    return pl.BlockSpec(p.shape, lambda i, j: (0,) * p.ndim)


def _rows(fn, rows, params, outs, name, tm=ROW_TILE, ncol=1):
    arrs, specs = zip(*[_row_spec(r, tm) for r in rows])
    S = arrs[0].shape[0]
    nr, npar = len(rows), len(params)

    def body(*refs):
        vals = fn(*[r[...].astype(f32) for r in refs[:nr]], *[p[...] for p in refs[nr:nr + npar]])
        for o_ref, v in zip(refs[nr + npar:], vals):
            o_ref[...] = v.astype(o_ref.dtype)

    return pl.pallas_call(
        body, grid=(S // tm, ncol), in_specs=list(specs) + [_full_spec(p) for p in params],
        out_specs=[pl.BlockSpec((tm, w), lambda i, j: (i, j)) for w, _ in outs],
        out_shape=[SDS((S, w * ncol), dt) for w, dt in outs], name=name,
        compiler_params=_params("parallel", "parallel"))(*arrs, *params)


def _rows_vjp(fn, rows, params, cts, wrt_rows, wrt_params, name, adds=None, tm=ROW_TILE, ncol=1, row_dtype=f32):
    adds = adds or {}
    arrs, specs = zip(*[_row_spec(r, tm) for r in rows])
    carrs, cspecs = zip(*[_row_spec(c, tm) for c in cts])
    add_keys = sorted(adds)
    aarrs = [adds[k] for k in add_keys]
    S = arrs[0].shape[0]
    nr, npar, nc, na = len(rows), len(params), len(cts), len(aarrs)
    widths = [specs[k].block_shape[1] for k in wrt_rows]

    def body(*refs):
        first = jnp.logical_and(pl.program_id(0) == 0, pl.program_id(1) == 0)
        rv = [r[...].astype(f32) for r in refs[:nr]]
        pv = [p[...] for p in refs[nr:nr + npar]]
        cv = tuple(c[...].astype(f32) for c in refs[nr + npar:nr + npar + nc])
        av = dict(zip(add_keys, refs[nr + npar + nc:nr + npar + nc + na]))
        o = refs[nr + npar + nc + na:]
        _, vjp = jax.vjp(fn, *rv, *pv)
        g = vjp(cv)
        for n, k in enumerate(wrt_rows):
            val = g[k]
            if k in av:
                val = val + av[k][...]
            o[n][...] = val.astype(o[n].dtype)
        for n, k in enumerate(wrt_params):
            ref = o[len(wrt_rows) + n]

            @pl.when(first)
            def _(ref=ref):
                ref[...] = jnp.zeros_like(ref)

            ref[...] += g[nr + k]

    res = pl.pallas_call(
        body, grid=(S // tm, ncol),
        in_specs=list(specs) + [_full_spec(p) for p in params] + list(cspecs)
        + [pl.BlockSpec((tm, a.shape[1] // ncol), lambda i, j: (i, j)) for a in aarrs],
        out_specs=[pl.BlockSpec((tm, w), lambda i, j: (i, j)) for w in widths] + [_full_spec(params[k]) for k in wrt_params],
        out_shape=[SDS((S, w * ncol), row_dtype) for w in widths] + [SDS(params[k].shape, f32) for k in wrt_params],
        name=name, compiler_params=_params("arbitrary", "arbitrary"))(*arrs, *params, *carrs, *aarrs)
    return res[:len(wrt_rows)], res[len(wrt_rows):]


def _rms(x, w):
    return x * lax.rsqrt(jnp.mean(x * x, axis=-1, keepdims=True) + EPS) * w


def _rms_fn(x, w):
    return (_rms(x, w),)


def _res_rms_fn(f, res, w):
    return (res + _rms(f, w),)


def _swap_halves(x):
    lane = lax.broadcasted_iota(jnp.int32, x.shape, 1)
    first = (lane % HEAD_DIM) < (HEAD_DIM // 2)
    n = x.shape[1]
    return jnp.where(first, pltpu.roll(x, n - HEAD_DIM // 2, 1), pltpu.roll(x, HEAD_DIM // 2, 1))


def _rope_fwd_fn(q, k, cos, sgn_sin):
    scale = HEAD_DIM ** -0.5
    return ((q * cos + _swap_halves(q) * sgn_sin) * scale, k * cos + _swap_halves(k) * sgn_sin)


def _rope_bwd_fn(dq, dk, cos, sgn_sin):
    dq = dq * (HEAD_DIM ** -0.5)
    return (dq * cos + _swap_halves(dq * sgn_sin), dk * cos + _swap_halves(dk * sgn_sin))


def _nt(a, b):
    return lax.dot_general(a, b, (((1,), (1,)), ((), ())), preferred_element_type=f32)


def _tn(a, b):
    return lax.dot_general(a, b, (((0,), (0,)), ((), ())), preferred_element_type=f32)


def _band_rows(j, d, nb):
    r, i = j // nb, j % nb
    if d == 1:
        cur = pl.ds(pl.multiple_of(i * 128, 128), 128)
        prev = pl.ds(pl.multiple_of(jnp.maximum(i - 1, 0) * 128, 128), 128)
    else:
        cur = pl.ds(i * (128 * d) + r, 128, stride=d)
        prev = pl.ds(jnp.maximum(i - 1, 0) * (128 * d) + r, 128, stride=d)
    a = lax.broadcasted_iota(jnp.int32, (128, 128), 0)
    c = lax.broadcasted_iota(jnp.int32, (128, 128), 1)
    return cur, prev, c <= a, jnp.logical_and(c >= a, i != 0)


def _attn_fwd(qr, kr, proj, name):
    S = qr.shape[0]
    nblk = S // 128

    def body(q_ref, k_ref, v_ref, out_ref, lse_ref, *scr):
        head_a = lax.broadcasted_iota(jnp.int32, (1, 128), 1) < HEAD_DIM
        for b, (d, nb) in enumerate(BRANCHES):
            ob_ref, lb_ref = scr[2 * b], scr[2 * b + 1]

            def blk(j, carry, d=d, nb=nb, ob_ref=ob_ref, lb_ref=lb_ref):
                cur, prev, mc, mp = _band_rows(j, d, nb)
                q = q_ref[cur, :]
                kc, kp = k_ref[cur, :].astype(bf16), k_ref[prev, :].astype(bf16)
                vc, vp = v_ref[cur, :].astype(bf16), v_ref[prev, :].astype(bf16)
                res = []
                for m in (head_a, jnp.logical_not(head_a)):
                    qm = jnp.where(m, q, 0.0).astype(bf16)
                    sc = jnp.where(mc, _nt(qm, kc), NEG)
                    sp = jnp.where(mp, _nt(qm, kp), NEG)
                    mx = jnp.maximum(jnp.max(sc, axis=1, keepdims=True), jnp.max(sp, axis=1, keepdims=True))
                    pc, pp = jnp.exp(sc - mx), jnp.exp(sp - mx)
                    l = jnp.sum(pc, axis=1, keepdims=True) + jnp.sum(pp, axis=1, keepdims=True)
                    o = (jnp.dot(pc.astype(bf16), vc, preferred_element_type=f32)
                         + jnp.dot(pp.astype(bf16), vp, preferred_element_type=f32)) / l
                    res.append((o, mx + jnp.log(l)))
                ob_ref[cur, :] = jnp.where(head_a, res[0][0], res[1][0])
                lb_ref[cur, :] = jnp.where(head_a, res[0][1], res[1][1])
                return carry

            lax.fori_loop(0, nblk, blk, 0)
        l0, l1, l2 = scr[1][...], scr[3][...], scr[5][...]
        mx = jnp.maximum(jnp.maximum(l0, l1), l2)
        e0, e1, e2 = jnp.exp(l0 - mx), jnp.exp(l1 - mx), jnp.exp(l2 - mx)
        den = e0 + e1 + e2
        out_ref[...] = ((e0 * scr[0][...] + e1 * scr[2][...] + e2 * scr[4][...]) / den).astype(out_ref.dtype)
        lse_ref[...] = mx + jnp.log(den)

    pair = pl.BlockSpec((S, 128), lambda h: (0, h))
    return pl.pallas_call(
        body, grid=(N_HEADS_A // 2,),
        in_specs=[pair, pair, pl.BlockSpec((S, 128), lambda h: (0, 2 * ATTN_W // 128 + h))], out_specs=[pair, pair],
        out_shape=[SDS((S, ATTN_W), bf16), SDS((S, ATTN_W), f32)], scratch_shapes=[pltpu.VMEM((S, 128), f32)] * 6,
        name=name, compiler_params=_params("parallel"))(qr, kr, proj)


def _attn_bwd(qr, kr, proj, dmix_in, out, lse, name):
    S = qr.shape[0]
    nblk = S // 128

    def body(q_ref, k_ref, v_ref, do_ref, out_ref, lse_ref, dq_ref, dk_ref, dv_ref, t_ref):
        head_a = lax.broadcasted_iota(jnp.int32, (1, 128), 1) < HEAD_DIM
        x = do_ref[...] * out_ref[...].astype(f32)
        t_ref[...] = jnp.where(head_a, jnp.sum(jnp.where(head_a, x, 0.0), axis=1, keepdims=True),
                               jnp.sum(jnp.where(head_a, 0.0, x), axis=1, keepdims=True))
        dq_ref[...] = jnp.zeros_like(dq_ref)
        dk_ref[...] = jnp.zeros_like(dk_ref)
        dv_ref[...] = jnp.zeros_like(dv_ref)
        for d, nb in BRANCHES:
            def blk(j, carry, d=d, nb=nb):
                cur, prev, mc, mp = _band_rows(j, d, nb)
                q, do = q_ref[cur, :], do_ref[cur, :]
                kc, kp = k_ref[cur, :].astype(bf16), k_ref[prev, :].astype(bf16)
                vc, vp = v_ref[cur, :].astype(bf16), v_ref[prev, :].astype(bf16)
                t, lse_b = t_ref[cur, :], lse_ref[cur, :]
                dq = dkc = dkp = dvc = dvp = jnp.zeros((128, 128), f32)
                for m, off in ((head_a, 0), (jnp.logical_not(head_a), HEAD_DIM)):
                    th, lh = t[:, off:off + 1], lse_b[:, off:off + 1]
                    qm = jnp.where(m, q, 0.0).astype(bf16)
                    dom = jnp.where(m, do, 0.0).astype(bf16)
                    pc = jnp.exp(jnp.where(mc, _nt(qm, kc), NEG) - lh)
                    pp = jnp.exp(jnp.where(mp, _nt(qm, kp), NEG) - lh)
                    dsc = (pc * (_nt(dom, vc) - th)).astype(bf16)
                    dsp = (pp * (_nt(dom, vp) - th)).astype(bf16)
                    dq = dq + jnp.where(m, jnp.dot(dsc, kc, preferred_element_type=f32)
                                        + jnp.dot(dsp, kp, preferred_element_type=f32), 0.0)
                    dvc, dvp = dvc + _tn(pc.astype(bf16), dom), dvp + _tn(pp.astype(bf16), dom)
                    dkc, dkp = dkc + _tn(dsc, qm), dkp + _tn(dsp, qm)
                dq_ref[cur, :] += dq
                dk_ref[cur, :] += dkc
                dv_ref[cur, :] += dvc
                dk_ref[prev, :] += dkp
                dv_ref[prev, :] += dvp
                return carry

            lax.fori_loop(0, nblk, blk, 0)

    pair = pl.BlockSpec((S, 128), lambda h: (0, h))
    return pl.pallas_call(
        body, grid=(N_HEADS_A // 2,),
        in_specs=[pair, pair, pl.BlockSpec((S, 128), lambda h: (0, 2 * ATTN_W // 128 + h)), pair, pair, pair],
        out_specs=[pair] * 3, out_shape=[SDS((S, ATTN_W), f32)] * 3, scratch_shapes=[pltpu.VMEM((S, 128), f32)],
        name=name, compiler_params=_params("parallel"))(qr, kr, proj, dmix_in, out, lse)


def _conv_val(x, w, K, rows):
    acc = x * w[K - 1:K, :]
    for s in range(1, K):
        acc = acc + jnp.where(rows >= s, pltpu.roll(x, s, 0), 0.0) * w[K - 1 - s:K - s, :]
    return acc


def _colconv_fwd(xs, ws, bs, K, fn, nblk, tc, outs, name):
    S = xs[0][0].shape[0]
    n = len(xs)
    has_b = bs is not None

    def body(*refs):
        rows = lax.broadcasted_iota(jnp.int32, (S, tc), 0)
        cs = []
        for k in range(n):
            c = _conv_val(refs[k][...].astype(f32), refs[n + k][...], K, rows)
            if has_b:
                c = c + refs[2 * n + k][...]
            cs.append(c)
        for o_ref, val in zip(refs[(3 if has_b else 2) * n:], fn(*cs)):
            o_ref[...] = val.astype(o_ref.dtype)

    def cspec(rows_, cb0):
        return pl.BlockSpec((rows_, tc), lambda j, cb0=cb0: (0, cb0 + j))

    in_specs = [cspec(S, cb) for _, cb in xs] + [cspec(K, cb) for _, cb in ws]
    args = [a for a, _ in xs] + [a for a, _ in ws]
    if has_b:
        in_specs += [cspec(1, cb) for _, cb in bs]
        args += [a for a, _ in bs]
    return pl.pallas_call(
        body, grid=(nblk,), in_specs=in_specs, out_specs=[cspec(S, 0) for _ in outs],
        out_shape=[SDS((S, nblk * tc), dt) for dt in outs], name=name, compiler_params=_params("parallel"))(*args)


def _colconv_bwd(xs, ws, bs, K, fn, douts, nblk, tc, name, dx_dtype=f32):
    S = xs[0][0].shape[0]
    n, nd = len(xs), len(douts)
    has_b = bs is not None
    nin = (3 if has_b else 2) * n

    def body(*refs):
        rows = lax.broadcasted_iota(jnp.int32, (S, tc), 0)
        x = [refs[k][...].astype(f32) for k in range(n)]
        w = [refs[n + k][...] for k in range(n)]
        cs = []
        for k in range(n):
            c = _conv_val(x[k], w[k], K, rows)
            if has_b:
                c = c + refs[2 * n + k][...]
            cs.append(c)
        _, vjp = jax.vjp(fn, *cs)
        dcs = vjp(tuple(r[...].astype(f32) for r in refs[nin:nin + nd]))
        o = refs[nin + nd:]
        for k in range(n):
            dc = dcs[k]
            dx = dc * w[k][K - 1:K, :]
            o[n + k][K - 1:K, :] = jnp.sum(dc * x[k], axis=0, keepdims=True)
            for s in range(1, K):
                dx = dx + jnp.where(rows < S - s, pltpu.roll(dc, S - s, 0), 0.0) * w[k][K - 1 - s:K - s, :]
                xsh = jnp.where(rows >= s, pltpu.roll(x[k], s, 0), 0.0)
                o[n + k][K - 1 - s:K - s, :] = jnp.sum(dc * xsh, axis=0, keepdims=True)
            o[k][...] = dx.astype(o[k].dtype)
            if has_b:
                o[2 * n + k][...] = jnp.sum(dc, axis=0, keepdims=True)

    def cspec(rows_, cb0):
        return pl.BlockSpec((rows_, tc), lambda j, cb0=cb0: (0, cb0 + j))

    in_specs = [cspec(S, cb) for _, cb in xs] + [cspec(K, cb) for _, cb in ws]
    args = [a for a, _ in xs] + [a for a, _ in ws]
    if has_b:
        in_specs += [cspec(1, cb) for _, cb in bs]
        args += [a for a, _ in bs]
    in_specs += [cspec(S, 0) for _ in douts]
    args += list(douts)
    W = nblk * tc
    out_specs = [cspec(S, 0)] * n + [cspec(K, 0)] * n + ([cspec(1, 0)] * n if has_b else [])
    out_shape = [SDS((S, W), dx_dtype)] * n + [SDS((K, W), f32)] * n + ([SDS((1, W), f32)] * n if has_b else [])
    res = pl.pallas_call(body, grid=(nblk,), in_specs=in_specs, out_specs=out_specs, out_shape=out_shape,
                         name=name, compiler_params=_params("parallel"))(*args)
    return res[:n], res[n:2 * n], res[2 * n:]


def _silu_fn(c):
    return (c * jax.nn.sigmoid(c),)


def _geglu_fn(gate, up):
    gelu = 0.5 * gate * (1.0 + jnp.tanh(0.7978845608028654 * (gate + 0.044715 * gate * gate * gate)))
    return (gelu * up,)


def _softplus(x):
    u = jnp.exp(jnp.minimum(x, 20.0))
    small = u * (1.0 - 0.5 * u)
    return jnp.where(x > 20.0, x, jnp.where(u < 1e-4, small, jnp.log(1.0 + u)))


def _bmm(a, b, precision=None):
    return lax.dot_general(a, b, (((2,), (1,)), ((0,), (0,))), precision=precision, preferred_element_type=f32)


def _bnt(a, b):
    return lax.dot_general(a, b, (((2,), (2,)), ((0,), (0,))), preferred_element_type=f32)


def _btn(a, b):
    return lax.dot_general(a, b, (((1,), (1,)), ((0,), (0,))), preferred_element_type=f32)


def _unit_lower_inverse(A):
    n = A.shape[-1]
    eye = (lax.broadcasted_iota(jnp.int32, (1, n, n), 1) == lax.broadcasted_iota(jnp.int32, (1, n, n), 2)).astype(f32)
    P = -A
    T = eye + P
    for _ in range(5):
        P = _bmm(P, P, HI)
        T = T + _bmm(T, P, HI)
    return T


def _dn_prep_fn(q, k, v, ba, alog, dtb, h):
    G, C = q.shape[0], CHUNK
    lane = lax.broadcasted_iota(jnp.int32, (1, 1, 128), 2)

    def sel(arr, idx):
        return jnp.sum(jnp.where(lane == idx, arr, 0.0), axis=-1, keepdims=True)

    beta = jax.nn.sigmoid(sel(ba, h))
    g = -jnp.exp(sel(alog[None], h)) * _softplus(sel(ba, N_HEADS_D + h) + sel(dtb[None], h))
    qn = q * lax.rsqrt(jnp.sum(q * q, axis=-1, keepdims=True) + EPS) * (DK ** -0.5)
    kn = k * lax.rsqrt(jnp.sum(k * k, axis=-1, keepdims=True) + EPS)
    ii = lax.broadcasted_iota(jnp.int32, (1, C, C), 1)
    jj = lax.broadcasted_iota(jnp.int32, (1, C, C), 2)
    tril, strict = ii >= jj, ii > jj
    gsq = jnp.broadcast_to(g, (G, C, C))
    gcol = _bmm(jnp.broadcast_to(tril.astype(f32), (G, C, C)), gsq, HI)
    grow = _bmm(jnp.ones((G, C, C), f32), jnp.where(ii <= jj, gsq, 0.0), HI)
    decay = jnp.exp(jnp.where(tril, gcol - grow, NEG))
    gc = gcol[:, :, :1]
    glast = gcol[:, C - 1:C, :1]
    kb = kn * beta
    A = jnp.where(strict, _bnt(kb.astype(bf16), kn.astype(bf16)) * decay, 0.0)
    T = _unit_lower_inverse(A).astype(bf16)
    u = _bmm(T, (v * beta).astype(bf16))
    w = _bmm(T, (kb * jnp.exp(gc)).astype(bf16))
    qk = _bnt(qn.astype(bf16), kn.astype(bf16)) * decay
    qd = qn * jnp.exp(gc)
    kd = kn * jnp.exp(glast - gc)
    return u, w, qk, qd, kd, jnp.broadcast_to(jnp.exp(glast), (G, C, DK))


def _dn_scan_fn(u, w, qk, qd, kd, eg, St):
    b = lambda a: a.astype(bf16)
    vnew = u - _bmm(b(w), b(St))
    o = _bmm(b(qd), b(St)) + _bmm(b(qk), b(vnew))
    return o, St * eg[:, :1, :] + _btn(b(kd), b(vnew))


def _dn_post_fn(o, z, nw):
    return (_rms(o, nw) * (z * jax.nn.sigmoid(z)),)


DN_GROUP = 8


def _dn_prep_specs(S, rows):
    def col(first):
        return pl.BlockSpec((rows, DK), lambda i, h, first=first: (i, first // DK + h))

    par = pl.BlockSpec((1, 128), lambda i, h: (0, 0))
    return [col(0), col(N_HEADS_D * DK), col(2 * N_HEADS_D * DK),
            pl.BlockSpec((rows, 128), lambda i, h: (i, 3584 // 128)), par, par]


def _dn_prep(qkv, proj, alog, dtb, name):
    S = qkv.shape[0]
    G = DN_GROUP
    rows = G * CHUNK

    def body(q_ref, k_ref, v_ref, ba_ref, al_ref, dt_ref, u_ref, w_ref, qk_ref, qd_ref, kd_ref, eg_ref):
        h = pl.program_id(1)
        r3 = lambda ref: ref[...].reshape(G, CHUNK, 128)
        u, w, qk, qd, kd, eg = _dn_prep_fn(r3(q_ref), r3(k_ref), r3(v_ref), r3(ba_ref), al_ref[...], dt_ref[...], h)
        for ref, val in ((u_ref, u), (w_ref, w), (qd_ref, qd), (kd_ref, kd), (eg_ref, eg)):
            ref[...] = val.reshape(rows, DK)
        qk_ref[:, :CHUNK] = qk.reshape(rows, CHUNK)
        qk_ref[:, CHUNK:] = jnp.zeros((rows, DK - CHUNK), f32)

    out = pl.BlockSpec((rows, DK), lambda i, h: (i, h))
    return pl.pallas_call(
        body, grid=(S // rows, N_HEADS_D), in_specs=_dn_prep_specs(S, rows), out_specs=[out] * 6,
        out_shape=[SDS((S, N_HEADS_D * DK), f32)] * 6, name=name,
        compiler_params=_params("parallel", "parallel"))(qkv, qkv, qkv, proj, alog, dtb)


def _dn_prep_bwd(qkv, proj, alog, dtb, cts, name):
    S = qkv.shape[0]
    G = DN_GROUP
    rows = G * CHUNK

    def body(q_ref, k_ref, v_ref, ba_ref, al_ref, dt_ref, du_ref, dw_ref, dqk_ref, dqd_ref, dkd_ref, deg_ref,
             dq_ref, dk_ref, dv_ref, dba_ref, dal_ref, ddt_ref):
        i, h = pl.program_id(0), pl.program_id(1)
        r3 = lambda ref: ref[...].reshape(G, CHUNK, 128)
        _, vjp = jax.vjp(lambda q, k, v, ba, al, dt: _dn_prep_fn(q, k, v, ba, al, dt, h),
                         r3(q_ref), r3(k_ref), r3(v_ref), r3(ba_ref), al_ref[...], dt_ref[...])
        dqk = dqk_ref[:, :CHUNK].reshape(G, CHUNK, CHUNK)
        dq, dk, dv, dba, dal, ddt = vjp((r3(du_ref), r3(dw_ref), dqk, r3(dqd_ref), r3(dkd_ref), r3(deg_ref)))
        dq_ref[...] = dq.reshape(rows, DK)
        dk_ref[...] = dk.reshape(rows, DK)
        dv_ref[...] = dv.reshape(rows, DK)

        @pl.when(h == 0)
        def _():
            dba_ref[...] = jnp.zeros_like(dba_ref)

        @pl.when(jnp.logical_and(i == 0, h == 0))
        def _():
            dal_ref[...] = jnp.zeros_like(dal_ref)
            ddt_ref[...] = jnp.zeros_like(ddt_ref)

        dba_ref[...] += dba.reshape(rows, 128)
        dal_ref[...] += dal
        ddt_ref[...] += ddt

    hcol = pl.BlockSpec((rows, DK), lambda i, h: (i, h))
    par = pl.BlockSpec((1, 128), lambda i, h: (0, 0))
    W = N_HEADS_D * DK
    return pl.pallas_call(
        body, grid=(S // rows, N_HEADS_D), in_specs=_dn_prep_specs(S, rows) + [hcol] * 6,
        out_specs=[hcol] * 3 + [pl.BlockSpec((rows, 128), lambda i, h: (i, 0)), par, par],
        out_shape=[SDS((S, W), f32)] * 3 + [SDS((S, 128), f32), SDS((1, 128), f32), SDS((1, 128), f32)], name=name,
        compiler_params=_params("arbitrary", "arbitrary"))(qkv, qkv, qkv, proj, alog, dtb, *cts)


def _heads(x):
    return jnp.stack([x[:, DK * h:DK * (h + 1)] for h in range(N_HEADS_D)])


def _dn_scan(pre, name):
    S = pre[0].shape[0]
    NCH = S // CHUNK

    def body(u_ref, w_ref, qk_ref, qd_ref, kd_ref, eg_ref, o_ref, st_ref, s_ref):
        @pl.when(pl.program_id(0) == 0)
        def _():
            s_ref[...] = jnp.zeros_like(s_ref)

        St = s_ref[...]
        st_ref[0] = St
        o, Sn = _dn_scan_fn(_heads(u_ref[...]), _heads(w_ref[...]), _heads(qk_ref[...])[:, :, :CHUNK], _heads(qd_ref[...]),
                            _heads(kd_ref[...]), _heads(eg_ref[...]), St)
        for h in range(N_HEADS_D):
            o_ref[:, DK * h:DK * (h + 1)] = o[h]
        s_ref[...] = Sn

    blk = pl.BlockSpec((CHUNK, N_HEADS_D * DK), lambda n: (n, 0))
    return pl.pallas_call(
        body, grid=(NCH,), in_specs=[blk] * 6,
        out_specs=[blk, pl.BlockSpec((1, N_HEADS_D, DK, DK), lambda n: (n, 0, 0, 0))],
        out_shape=[SDS((S, N_HEADS_D * DK), f32), SDS((NCH, N_HEADS_D, DK, DK), f32)],
        scratch_shapes=[pltpu.VMEM((N_HEADS_D, DK, DK), f32)], name=name, compiler_params=_params("arbitrary"))(*pre)


def _dn_scan_bwd(pre, states, do, name):
    S = do.shape[0]
    NCH = S // CHUNK

    def body(u_ref, w_ref, qk_ref, qd_ref, kd_ref, eg_ref, st_ref, do_ref,
             du_ref, dw_ref, dqk_ref, dqd_ref, dkd_ref, deg_ref, ds_ref):
        @pl.when(pl.program_id(0) == 0)
        def _():
            ds_ref[...] = jnp.zeros_like(ds_ref)

        _, vjp = jax.vjp(_dn_scan_fn, _heads(u_ref[...]), _heads(w_ref[...]), _heads(qk_ref[...])[:, :, :CHUNK],
                         _heads(qd_ref[...]), _heads(kd_ref[...]), _heads(eg_ref[...]), st_ref[0])
        du, dw, dqk, dqd, dkd, deg, dS = vjp((_heads(do_ref[...]), ds_ref[...]))
        ds_ref[...] = dS
        for h in range(N_HEADS_D):
            c = slice(DK * h, DK * (h + 1))
            for ref, val in ((du_ref, du), (dw_ref, dw), (dqd_ref, dqd), (dkd_ref, dkd), (deg_ref, deg)):
                ref[:, c] = val[h]
            dqk_ref[:, DK * h:DK * h + CHUNK] = dqk[h]
            dqk_ref[:, DK * h + CHUNK:DK * (h + 1)] = jnp.zeros((CHUNK, DK - CHUNK), f32)

    blk = pl.BlockSpec((CHUNK, N_HEADS_D * DK), lambda n: (NCH - 1 - n, 0))
    return pl.pallas_call(
        body, grid=(NCH,),
        in_specs=[blk] * 6 + [pl.BlockSpec((1, N_HEADS_D, DK, DK), lambda n: (NCH - 1 - n, 0, 0, 0)), blk],
        out_specs=[blk] * 6, out_shape=[SDS((S, N_HEADS_D * DK), f32)] * 6,
        scratch_shapes=[pltpu.VMEM((N_HEADS_D, DK, DK), f32)], name=name,
        compiler_params=_params("arbitrary"))(*pre, states, do)


def _loss_head(y, t, name):
    S, D = y.shape
    tm = ROW_TILE

    def body(y_ref, t_ref, dy_ref, l_ref):
        i = pl.program_id(0)
        d = y_ref[...] - t_ref[...]
        dy_ref[...] = d * (1.0 / D)
        part = jnp.sum(jnp.sum(d * d, axis=1, keepdims=True), axis=0, keepdims=True) * (0.5 / D)

        @pl.when(i == 0)
        def _():
            l_ref[...] = jnp.zeros_like(l_ref)

        l_ref[...] += jnp.broadcast_to(part, l_ref.shape)

    spec = pl.BlockSpec((tm, D), lambda i: (i, 0))
    dy, l = pl.pallas_call(body, grid=(S // tm,), in_specs=[spec, spec],
                           out_specs=[spec, pl.BlockSpec((1, 128), lambda i: (0, 0))],
                           out_shape=[SDS((S, D), f32), SDS((1, 128), f32)], name=name,
                           compiler_params=_params("arbitrary"))(y, t)
    return l[0, 0], dy


def _adamw(w, g, m, v, tr, name):
    L, R, C = w.shape
    assert R % tr == 0

    def body(w_ref, g_ref, m_ref, v_ref, d_ref, mo_ref, vo_ref):
        gv = g_ref[...]
        m2 = ADAM_B1 * m_ref[...] + (1.0 - ADAM_B1) * gv
        v2 = ADAM_B2 * v_ref[...] + (1.0 - ADAM_B2) * (gv * gv)
        m_hat = m2 / (1.0 - ADAM_B1 ** ADAM_STEP)
        v_hat = v2 / (1.0 - ADAM_B2 ** ADAM_STEP)
        d_ref[...] = -ADAM_LR * (m_hat / (jnp.sqrt(v_hat) + ADAM_EPS) + ADAM_WD * w_ref[...])
        mo_ref[...] = m2
        vo_ref[...] = v2

    spec = pl.BlockSpec((1, tr, C), lambda l, i: (l, i, 0))
    return pl.pallas_call(body, grid=(L, R // tr), in_specs=[spec] * 4, out_specs=[spec] * 3,
                          out_shape=[SDS((L, R, C), f32)] * 3, name=name,
                          compiler_params=_params("parallel", "parallel"))(w, g, m, v)


def _rope_tables(S):
    inv = 1.0 / (10000.0 ** (jnp.arange(0, HEAD_DIM, 2, dtype=f32) / HEAD_DIM))
    ang = jnp.arange(S, dtype=f32)[:, None] * inv[None, :]
    cos, sin = jnp.cos(ang), jnp.sin(ang)
    return (jnp.tile(jnp.concatenate([cos, cos], axis=1), (1, N_HEADS_A)),
            jnp.tile(jnp.concatenate([-sin, sin], axis=1), (1, N_HEADS_A)))


def _layer_fwd(x, W, cos, sgn_sin, l):
    n = f"l{l}_"
    (h1,) = _rows(_rms_fn, [x], [W["norm_pre_mix"]], [(D_MODEL, bf16)], n + "pre_mix_norm")
    proj = _mm(h1, W["w_in"], "nn", 512, 768, f32, n + "in_proj")
    qr, kr = _rows(_rope_fwd_fn, [(proj, ATTN_W, 0), (proj, ATTN_W, 1), cos, sgn_sin], [],
                   [(ATTN_W, f32), (ATTN_W, f32)], n + "rope")
    attn_out, lse = _attn_fwd(qr, kr, proj, n + "attn_fwd")
    (qkv,) = _colconv_fwd([(proj, 3)], [(W["dn_conv_w"], 0)], None, 4, _silu_fn, 3, 512, [f32], n + "dn_conv")
    dn_pre = _dn_prep(qkv, proj, W["dn_a_log"], W["dn_dt_bias"], n + "dn_prep")
    dn_o, dn_states = _dn_scan(dn_pre, n + "dn_scan")
    (dn_out,) = _rows(_dn_post_fn, [(dn_o, DK, 0), (proj, DK, 3072 // DK)], [W["dn_norm_w"]], [(DK, bf16)], n + "dn_post",
                      ncol=N_HEADS_D)
    mix_in = jnp.concatenate([attn_out, dn_out], axis=1)
    mix = _mm(mix_in, W["w_out"], "nn", 512, 512, f32, n + "out_proj")
    (x1,) = _rows(_res_rms_fn, [mix, x], [W["norm_post_mix"]], [(D_MODEL, f32)], n + "post_mix_norm")
    (h2,) = _rows(_rms_fn, [x1], [W["norm_pre_ffn"]], [(D_MODEL, bf16)], n + "pre_ffn_norm")
    u0 = _mm(h2, W["ffn_w_in"], "nn", 1024, 512, bf16, n + "ffn_in")
    nb_ff = D_FF // 256
    (act,) = _colconv_fwd([(u0, 0), (u0, nb_ff)], [(W["ffn_conv_w"], 0), (W["ffn_conv_w"], nb_ff)],
                          [(W["ffn_conv_b"], 0), (W["ffn_conv_b"], nb_ff)], 3, _geglu_fn, nb_ff, 256, [bf16],
                          n + "ffn_conv_glu")
    f = _mm(act, W["ffn_w_out"], "nn", 512, 512, f32, n + "ffn_out")
    (x2,) = _rows(_res_rms_fn, [f, x1], [W["norm_post_ffn"]], [(D_MODEL, f32)], n + "post_ffn_norm")
    saved = dict(x=x, h1=h1, proj=proj, qr=qr, kr=kr, attn_out=attn_out, lse=lse, qkv=qkv, dn_pre=dn_pre, dn_o=dn_o,
                 dn_states=dn_states, mix_in=mix_in, mix=mix, x1=x1, h2=h2, u0=u0, act=act, f=f)
    return x2, saved


def _layer_bwd(dx2, sv, W, cos, sgn_sin, l):
    n = f"l{l}_"
    S = dx2.shape[0]
    g = {}
    (df,), (g["norm_post_ffn"],) = _rows_vjp(_rms_fn, [sv["f"]], [W["norm_post_ffn"]], [dx2], [0], [0],
                                             n + "post_ffn_norm_bwd", row_dtype=bf16)
    dact = _mm(df, W["ffn_w_out"], "nt", 512, 1408, f32, n + "ffn_out_dx")
    g["ffn_w_out"] = _mm(sv["act"], df, "tn", 256, 1024, f32, n + "ffn_out_dw")
    nb_ff = D_FF // 256
    u0 = sv["u0"]
    dxs, dws, dbs = _colconv_bwd([(u0, 0), (u0, nb_ff)], [(W["ffn_conv_w"], 0), (W["ffn_conv_w"], nb_ff)],
                                 [(W["ffn_conv_b"], 0), (W["ffn_conv_b"], nb_ff)], 3, _geglu_fn, [dact], nb_ff, 256,
                                 n + "ffn_conv_glu_bwd", dx_dtype=bf16)
    du0 = jnp.concatenate(dxs, axis=1)
    g["ffn_conv_w"] = jnp.concatenate(dws, axis=1)
    g["ffn_conv_b"] = jnp.concatenate(dbs, axis=1)
    dh2 = _mm(du0, W["ffn_w_in"], "nt", 512, 512, f32, n + "ffn_in_dx")
    g["ffn_w_in"] = _mm(sv["h2"], du0, "tn", 512, D_FF // 2, f32, n + "ffn_in_dw", column_shards=True)
    (dx1,), (g["norm_pre_ffn"],) = _rows_vjp(_rms_fn, [sv["x1"]], [W["norm_pre_ffn"]], [dh2], [0], [0],
                                             n + "pre_ffn_norm_bwd", adds={0: dx2})
    (dmix,), (g["norm_post_mix"],) = _rows_vjp(_rms_fn, [sv["mix"]], [W["norm_post_mix"]], [dx1], [0], [0],
                                               n + "post_mix_norm_bwd", row_dtype=bf16)
    dmix_in = _mm(dmix, W["w_out"], "nt", 512, 512, f32, n + "out_proj_dx")
    g["w_out"] = _mm(sv["mix_in"], dmix, "tn", 512, 512, f32, n + "out_proj_dw")

    (ddn_o, dz), (g["dn_norm_w"],) = _rows_vjp(
        _dn_post_fn, [(sv["dn_o"], DK, 0), (sv["proj"], DK, 3072 // DK)], [W["dn_norm_w"]], [(dmix_in, DK, ATTN_W // DK)],
        [0, 1], [0], n + "dn_post_bwd", ncol=N_HEADS_D)
    dpre = _dn_scan_bwd(sv["dn_pre"], sv["dn_states"], ddn_o, n + "dn_scan_bwd")
    dq, dk, dv, dba, g["dn_a_log"], g["dn_dt_bias"] = _dn_prep_bwd(
        sv["qkv"], sv["proj"], W["dn_a_log"], W["dn_dt_bias"], dpre, n + "dn_prep_bwd")
    dqkv = jnp.concatenate([dq, dk, dv], axis=1)
    (dqkv0,), (g["dn_conv_w"],), _ = _colconv_bwd([(sv["proj"], 3)], [(W["dn_conv_w"], 0)], None, 4, _silu_fn,
                                                 [dqkv], 3, 512, n + "dn_conv_bwd")

    dqr, dkr, dav = _attn_bwd(sv["qr"], sv["kr"], sv["proj"], dmix_in, sv["attn_out"], sv["lse"], n + "attn_bwd")
    daq, dak = _rows(_rope_bwd_fn, [dqr, dkr, cos, sgn_sin], [], [(ATTN_W, f32)] * 2, n + "rope_bwd")
    dproj = jnp.concatenate([daq, dak, dav, dqkv0, dz, dba, jnp.zeros((S, PROJ_W - 3712), f32)], axis=1).astype(bf16)
    dh1 = _mm(dproj, W["w_in"], "nt", 512, 512, f32, n + "in_proj_dx")
    g["w_in"] = _mm(sv["h1"], dproj, "tn", 512, 768, f32, n + "in_proj_dw")
    (dx,), (g["norm_pre_mix"],) = _rows_vjp(_rms_fn, [sv["x"]], [W["norm_pre_mix"]], [dh1], [0], [0],
                                            n + "pre_mix_norm_bwd", adds={0: dx1})
    return dx, g


def _local_step(x, target, layers):
    cos, sgn_sin = _rope_tables(x.shape[0])
    saved = []
    for l, W in enumerate(layers):
        x, sv = _layer_fwd(x, W, cos, sgn_sin, l)
        saved.append(sv)
    loss, dx = _loss_head(x, target, "loss_head")
    grads = [None] * len(layers)
    for l in reversed(range(len(layers))):
        dx, grads[l] = _layer_bwd(dx, saved[l], layers[l], cos, sgn_sin, l)
    return loss, dx, grads


def _pos():
    x, y, c = lax.axis_index("x"), lax.axis_index("y"), lax.axis_index("c")
    return x, y, c, [(1 - x, y), (x, 1 - y), (1 - x, 1 - y)]


def _rcopy(src, dst, send_sem, recv_sem, dev):
    return pltpu.make_async_remote_copy(src_ref=src, dst_ref=dst, send_sem=send_sem, recv_sem=recv_sem,
                                        device_id=dev, device_id_type=MESH)


def _half_rows(ref, h, which, axis):
    if h is None:
        return ref
    rows = pl.ds(pl.multiple_of(which * h, 16), h)
    return ref.at[:, rows, :] if axis == 1 else ref.at[rows, :]


def _dma_sems(*counts):
    return [pltpu.SemaphoreType.DMA((k,)) for k in counts]


def _all_gather(arrs, halves, name):
    n = len(arrs)

    def body(*refs):
        ins, outs = refs[:n], refs[n:2 * n]
        send1, recv1, send2, recv2 = refs[2 * n:]
        x, y, c, chips = _pos()
        me, sib, s_me = (x, y, c), (x, y, 1 - c), 2 * x + y
        sends = []
        for i in range(n):
            for j, chip in enumerate(chips):
                cp = _rcopy(_half_rows(ins[i], halves[i], c, 1), _half_rows(outs[i].at[s_me], halves[i], c, 1),
                            send1.at[3 * i + j], recv1.at[3 * i + j], (*chip, c))
                cp.start()
                sends.append(cp)
        for i in range(n):
            for j, (px, py) in enumerate(chips):
                k = 3 * i + j
                landed = _half_rows(outs[i].at[2 * px + py], halves[i], c, 1)
                _rcopy(landed, landed, send1.at[k], recv1.at[k], me).wait_recv()
                if halves[i] is not None:
                    cp = _rcopy(landed, landed, send2.at[k], recv2.at[k], sib)
                    cp.start()
                    sends.append(cp)
        for i in range(n):
            if halves[i] is None:
                continue
            for j, (px, py) in enumerate(chips):
                k = 3 * i + j
                other = _half_rows(outs[i].at[2 * px + py], halves[i], 1 - c, 1)
                _rcopy(other, other, send2.at[k], recv2.at[k], me).wait_recv()
        for cp in sends:
            cp.wait_send()

    return pl.pallas_call(
        body, in_specs=[ANY] * n, out_specs=[ANY] * n,
        out_shape=[SDS((4,) + a.shape, a.dtype) for a in arrs],
        scratch_shapes=_dma_sems(3 * n, 3 * n, 3 * n, 3 * n), name=name)(*arrs)


def _exchange_halves(gs, name):
    n = len(gs)

    def body(*refs):
        ins, outs = refs[:n], refs[n:2 * n]
        send, recv = refs[2 * n:]
        x, y, c, _ = _pos()
        sends = []
        for k in range(n):
            cp = _rcopy(_half_rows(ins[k], gs[k].shape[1] // 2, 1 - c, 1), outs[k], send.at[k], recv.at[k], (x, y, 1 - c))
            cp.start()
            sends.append(cp)
        for k in range(n):
            _rcopy(outs[k], outs[k], send.at[k], recv.at[k], (x, y, c)).wait_recv()
        for cp in sends:
            cp.wait_send()

    return pl.pallas_call(
        body, in_specs=[ANY] * n, out_specs=[ANY] * n,
        out_shape=[SDS((4, g.shape[1] // 2, g.shape[2]), g.dtype) for g in gs],
        scratch_shapes=_dma_sems(n, n), name=name)(*gs)


def _scatter_partials(ps, name):
    n = len(ps)

    def body(*refs):
        ins, outs = refs[:n], refs[n:2 * n]
        send, recv = refs[2 * n:]
        x, y, c, chips = _pos()
        sends = []
        for k in range(n):
            for j, (px, py) in enumerate(chips):
                cp = _rcopy(ins[k].at[2 * px + py], outs[k].at[j], send.at[3 * k + j], recv.at[3 * k + j], (px, py, c))
                cp.start()
                sends.append(cp)
        for k in range(n):
            for j in range(3):
                _rcopy(outs[k].at[j], outs[k].at[j], send.at[3 * k + j], recv.at[3 * k + j], (x, y, c)).wait_recv()
        for cp in sends:
            cp.wait_send()

    return pl.pallas_call(
        body, in_specs=[ANY] * n, out_specs=[ANY] * n,
        out_shape=[SDS((3,) + p.shape[1:], p.dtype) for p in ps],
        scratch_shapes=_dma_sems(3 * n, 3 * n), name=name)(*ps)


def _join_halves(rs, name):
    n = len(rs)

    def body(*refs):
        outs = refs[n:2 * n]
        send, recv = refs[2 * n:]
        x, y, c, _ = _pos()
        sends = []
        for k in range(n):
            mine = _half_rows(outs[k], rs[k].shape[0] // 2, c, 0)
            cp = _rcopy(mine, mine, send.at[k], recv.at[k], (x, y, 1 - c))
            cp.start()
            sends.append(cp)
        for k in range(n):
            other = _half_rows(outs[k], rs[k].shape[0] // 2, 1 - c, 0)
            _rcopy(other, other, send.at[k], recv.at[k], (x, y, c)).wait_recv()
        for cp in sends:
            cp.wait_send()

    return pl.pallas_call(
        body, in_specs=[ANY] * n, out_specs=[ANY] * n, out_shape=[SDS(r.shape, r.dtype) for r in rs],
        input_output_aliases={k: k for k in range(n)}, scratch_shapes=_dma_sems(n, n), name=name)(*rs)


def _all_reduce_small(pack, name):
    R = pack.shape[0]

    def body(in_ref, out_ref, buf, send, recv):
        x, y, c, _ = _pos()
        me = 4 * x + 2 * y + c
        buf[me] = in_ref[...]
        sends = []
        for k in range(1, 8):
            peer = me ^ k
            cp = _rcopy(buf.at[me], buf.at[me], send.at[k - 1], recv.at[k - 1], ((peer >> 2) & 1, (peer >> 1) & 1, peer & 1))
            cp.start()
            sends.append(cp)
        for k in range(1, 8):
            _rcopy(buf.at[me ^ k], buf.at[me ^ k], send.at[k - 1], recv.at[k - 1], (x, y, c)).wait_recv()
        for cp in sends:
            cp.wait_send()
        acc = buf[0]
        for d in range(1, 8):
            acc = acc + buf[d]
        out_ref[...] = acc

    return pl.pallas_call(
        body, out_shape=SDS((R, 128), f32),
        in_specs=[pl.BlockSpec(memory_space=pltpu.VMEM)], out_specs=pl.BlockSpec(memory_space=pltpu.VMEM),
        scratch_shapes=[pltpu.VMEM((8, R, 128), f32)] + _dma_sems(7, 7), name=name)(pack)


def _add_sibling(g, recv, c_arr, tr, name):
    _, R, C = g.shape
    h = R // 2
    nrb = h // tr
    assert h % tr == 0

    def body(c_ref, g_ref, r_ref, o_ref):
        o_ref[...] = (g_ref[...] + r_ref[...]).astype(o_ref.dtype)

    spec = pl.BlockSpec((1, tr, C), lambda s, r, c_ref: (s, r, 0))
    grid_spec = pltpu.PrefetchScalarGridSpec(
        num_scalar_prefetch=1, grid=(4, nrb),
        in_specs=[pl.BlockSpec((1, tr, C), lambda s, r, c_ref: (s, c_ref[0] * nrb + r, 0)), spec], out_specs=spec)
    return pl.pallas_call(body, grid_spec=grid_spec, out_shape=SDS((4, h, C), bf16), name=name,
                          compiler_params=_params("parallel", "parallel"))(c_arr, g, recv)


def _add_chips(p, recv, sc_arr, tr, name):
    _, h, C = p.shape
    nrb = h // tr
    assert h % tr == 0

    def body(sc_ref, p_ref, r_ref, o_ref):
        o_ref[...] = (p_ref[0].astype(f32) + r_ref[0].astype(f32)) + (r_ref[1].astype(f32) + r_ref[2].astype(f32))

    grid_spec = pltpu.PrefetchScalarGridSpec(
        num_scalar_prefetch=1, grid=(nrb,),
        in_specs=[pl.BlockSpec((1, tr, C), lambda r, sc_ref: (sc_ref[0], r, 0)),
                  pl.BlockSpec((3, tr, C), lambda r, sc_ref: (0, r, 0))],
        out_specs=pl.BlockSpec((tr, C), lambda r, sc_ref: (sc_ref[1] * nrb + r, 0)))
    return pl.pallas_call(body, grid_spec=grid_spec, out_shape=SDS((2 * h, C), f32), name=name,
                          compiler_params=_params("parallel"))(sc_arr, p, recv)


_BIG = (("w_in", 1024, 256), ("w_out", 256, 128), ("ffn_w_in", 1024, 256), ("ffn_w_out", 704, 352))
_SMALL = ("dn_conv_w", "ffn_conv_w", "ffn_conv_b", "norm_pre_mix", "norm_post_mix", "norm_pre_ffn", "norm_post_ffn",
          "dn_norm_w", "dn_a_log", "dn_dt_bias")
_WEIGHTS = ("w_in", "dn_conv_w", "dn_a_log", "dn_dt_bias", "dn_norm_w", "w_out", "ffn_w_in", "ffn_conv_w", "ffn_conv_b",
            "ffn_w_out", "norm_pre_mix", "norm_post_mix", "norm_pre_ffn", "norm_post_ffn")
_ADAM_ROWS = {"w_in": 256, "w_out": 256, "ffn_w_in": 128, "ffn_w_out": 176}


def _shard_major(name, g):
    if name == "w_in":
        return jnp.stack([g[:, 898 * s:898 * (s + 1)] for s in range(4)])
    if name == "ffn_w_in":
        return g
    return g.reshape(4, g.shape[0] // 4, g.shape[1])


def kernel(x, w_in, dn_conv_w, dn_a_log, dn_dt_bias, dn_norm_w, w_out, ffn_w_in, ffn_conv_w, ffn_conv_b, ffn_w_out, norm_pre_mix, norm_post_mix, norm_pre_ffn, norm_post_ffn, loss_target, m_w_in, m_dn_conv_w, m_dn_a_log, m_dn_dt_bias, m_dn_norm_w, m_w_out, m_ffn_w_in, m_ffn_conv_w, m_ffn_conv_b, m_ffn_w_out, m_norm_pre_mix, m_norm_post_mix, m_norm_pre_ffn, m_norm_post_ffn, v_w_in, v_dn_conv_w, v_dn_a_log, v_dn_dt_bias, v_dn_norm_w, v_w_out, v_ffn_w_in, v_ffn_conv_w, v_ffn_conv_b, v_ffn_w_out, v_norm_pre_mix, v_norm_post_mix, v_norm_pre_ffn, v_norm_post_ffn):
    w = dict(w_in=w_in, dn_conv_w=dn_conv_w, dn_a_log=dn_a_log, dn_dt_bias=dn_dt_bias, dn_norm_w=dn_norm_w, w_out=w_out,
             ffn_w_in=ffn_w_in, ffn_conv_w=ffn_conv_w, ffn_conv_b=ffn_conv_b, ffn_w_out=ffn_w_out, norm_pre_mix=norm_pre_mix,
             norm_post_mix=norm_post_mix, norm_pre_ffn=norm_pre_ffn, norm_post_ffn=norm_post_ffn)
    m = dict(w_in=m_w_in, dn_conv_w=m_dn_conv_w, dn_a_log=m_dn_a_log, dn_dt_bias=m_dn_dt_bias, dn_norm_w=m_dn_norm_w,
             w_out=m_w_out, ffn_w_in=m_ffn_w_in, ffn_conv_w=m_ffn_conv_w, ffn_conv_b=m_ffn_conv_b, ffn_w_out=m_ffn_w_out,
             norm_pre_mix=m_norm_pre_mix, norm_post_mix=m_norm_post_mix, norm_pre_ffn=m_norm_pre_ffn,
             norm_post_ffn=m_norm_post_ffn)
    v = dict(w_in=v_w_in, dn_conv_w=v_dn_conv_w, dn_a_log=v_dn_a_log, dn_dt_bias=v_dn_dt_bias, dn_norm_w=v_dn_norm_w,
             w_out=v_w_out, ffn_w_in=v_ffn_w_in, ffn_conv_w=v_ffn_conv_w, ffn_conv_b=v_ffn_conv_b, ffn_w_out=v_ffn_w_out,
             norm_pre_mix=v_norm_pre_mix, norm_post_mix=v_norm_post_mix, norm_pre_ffn=v_norm_pre_ffn,
             norm_post_ffn=v_norm_post_ffn)
    xi, yi, ci = lax.axis_index("x"), lax.axis_index("y"), lax.axis_index("c")
    s_me = 2 * xi + yi
    c_arr = jnp.reshape(ci, (1,)).astype(jnp.int32)
    sc_arr = jnp.stack([s_me, ci]).astype(jnp.int32)

    own = [w_in.astype(bf16), w_out.astype(bf16), ffn_w_in.astype(bf16), ffn_w_out.astype(bf16), dn_conv_w, ffn_conv_w]
    gathered = _all_gather(own, [512, 128, 512, 352, None, None], "weights_all_gather")
    shards = [[jnp.where(s_me == s, o, a[s]) for s in range(4)] for o, a in zip(own, gathered)]
    cat = lambda k: jnp.concatenate(shards[k], axis=-1)
    rows = lambda k: jnp.concatenate(shards[k], axis=1)
    full = dict(w_in=jnp.pad(cat(0), ((0, 0), (0, 0), (0, PROJ_W - IN_COLS))), w_out=rows(1),
                ffn_w_in=cat(2), ffn_w_out=rows(3), dn_conv_w=cat(4), ffn_conv_w=cat(5))
    lanes = lambda a: jnp.pad(a, ((0, 0), (0, 128 - a.shape[1])))
    full.update(dn_a_log=lanes(dn_a_log), dn_dt_bias=lanes(dn_dt_bias), dn_norm_w=dn_norm_w, ffn_conv_b=ffn_conv_b,
                norm_pre_mix=norm_pre_mix, norm_post_mix=norm_post_mix, norm_pre_ffn=norm_pre_ffn,
                norm_post_ffn=norm_post_ffn)
    layers = [{k: (a[l] if a.ndim == 3 else a[l:l + 1]) for k, a in full.items()} for l in range(2)]

    loss_local, dx, grads = _local_step(x[0], loss_target[0], layers)
    loss = lax.psum(loss_local, ("x", "y", "c"))

    keys = [(name, l) for name, _, _ in _BIG for l in range(2)]
    tiles = {name: tr for name, _, tr in _BIG}
    gs = [_shard_major(name, grads[l][name]) for name, l in keys]
    from_sib = _exchange_halves(gs, "grads_to_sibling")
    chip_part = [_add_sibling(g, r, c_arr, tiles[name], f"add_sibling_{name}{l}") for g, r, (name, l) in zip(gs, from_sib, keys)]
    from_chips = _scatter_partials(chip_part, "grads_to_owner_chip")
    reduced = [_add_chips(p, r, sc_arr, tiles[name], f"add_chips_{name}{l}") for p, r, (name, l) in zip(chip_part, from_chips, keys)]
    joined = _join_halves(reduced, "grads_join_halves")
    g_out = {name: jnp.stack([joined[2 * i], joined[2 * i + 1]]) for i, (name, _, _) in enumerate(_BIG)}

    small = {}
    for name in _SMALL:
        per_layer = [grads[l][name] for l in range(2)]
        if name in ("dn_a_log", "dn_dt_bias"):
            per_layer = [p[:, :N_HEADS_D] for p in per_layer]
        small[name] = jnp.stack(per_layer).reshape((2,) + (w[name].shape[1:] if name not in ("dn_conv_w", "ffn_conv_w")
                                                           else per_layer[0].shape))
    flat = jnp.concatenate([small[name].reshape(-1) for name in _SMALL])
    n_rows = -(-flat.shape[0] // 1024) * 8
    summed = _all_reduce_small(jnp.pad(flat, (0, n_rows * 128 - flat.shape[0])).reshape(n_rows, 128),
                               "small_grads_all_reduce").reshape(-1)
    off = 0
    for name in _SMALL:
        size = small[name].size
        g_out[name] = summed[off:off + size].reshape(small[name].shape)
        off += size
    g_out["dn_conv_w"] = lax.dynamic_slice_in_dim(g_out["dn_conv_w"], s_me * 384, 384, axis=2)
    g_out["ffn_conv_w"] = lax.dynamic_slice_in_dim(g_out["ffn_conv_w"], s_me * 1408, 1408, axis=2)

    deltas, new_m, new_v = {}, {}, {}
    for name in _WEIGHTS:
        shape = w[name].shape
        as3 = (lambda a: a) if len(shape) == 3 else (lambda a: a.reshape(shape[0], 1, shape[1]))
        tr = _ADAM_ROWS.get(name, as3(w[name]).shape[1])
        d_, m_, v_ = _adamw(as3(w[name]), as3(g_out[name]), as3(m[name]), as3(v[name]), tr, f"adamw_{name}")
        deltas[name], new_m[name], new_v[name] = d_.reshape(shape), m_.reshape(shape), v_.reshape(shape)

    return (loss, dx[None], *[g_out[k] for k in _WEIGHTS], *[deltas[k] for k in _WEIGHTS],
            *[new_m[k] for k in _WEIGHTS], *[new_v[k] for k in _WEIGHTS])
```

```python
import jax
import jax.numpy as jnp
from jax import lax
from jax.experimental import pallas as pl
from jax.experimental.pallas import tpu as pltpu

f32, bf16 = jnp.float32, jnp.bfloat16
SDS = jax.ShapeDtypeStruct
HI = lax.Precision.HIGH
MESH = pl.DeviceIdType.MESH
ANY = pl.BlockSpec(memory_space=pl.ANY)

D_MODEL = 1024
N_HEADS_A, HEAD_DIM = 8, 64
ATTN_W = 512
N_HEADS_D, DK = 4, 128
CHUNK = 64
D_FF = 2816
IN_COLS = 3592
PROJ_W = 3840
BRANCHES = ((1, 16), (4, 4), (16, 1))
EPS = 1e-6
NEG = -1e30
ROW_TILE = 256
VMEM_LIMIT = 56 * 1024 * 1024

ADAM_LR, ADAM_B1, ADAM_B2, ADAM_EPS, ADAM_WD, ADAM_STEP = 0.001, 0.9, 0.999, 1e-08, 0.01, 10


def _params(*sem):
    return pltpu.CompilerParams(dimension_semantics=sem, vmem_limit_bytes=VMEM_LIMIT)


def _mm(a, b, mode, tm, tn, out_dtype, name, column_shards=False):
    if mode == "nn":
        (M, K), N = a.shape, b.shape[1]
        dims = (((1,), (0,)), ((), ()))
        a_spec = pl.BlockSpec((tm, K), lambda i, j: (i, 0))
        b_spec = pl.BlockSpec((K, tn), lambda i, j: (0, j))
    elif mode == "nt":
        (M, K), N = a.shape, b.shape[0]
        dims = (((1,), (1,)), ((), ()))
        a_spec = pl.BlockSpec((tm, K), lambda i, j: (i, 0))
        b_spec = pl.BlockSpec((tn, K), lambda i, j: (j, 0))
    else:
        (K, M), N = a.shape, b.shape[1]
        dims = (((0,), (0,)), ((), ()))
        a_spec = pl.BlockSpec((K, tm), lambda i, j: (0, i))
        b_spec = pl.BlockSpec((K, tn), lambda i, j: (0, j))
    assert M % tm == 0 and N % tn == 0, (name, M, N, tm, tn)

    def body(a_ref, b_ref, o_ref):
        o_ref[...] = lax.dot_general(a_ref[...].astype(bf16), b_ref[...].astype(bf16), dims,
                                     preferred_element_type=f32).astype(o_ref.dtype)

    if column_shards:
        out_spec, out_shape = pl.BlockSpec((None, tm, tn), lambda i, j: (j, i, 0)), SDS((N // tn, M, tn), out_dtype)
    else:
        out_spec, out_shape = pl.BlockSpec((tm, tn), lambda i, j: (i, j)), SDS((M, N), out_dtype)
    return pl.pallas_call(body, grid=(M // tm, N // tn), in_specs=[a_spec, b_spec], out_specs=out_spec,
                          out_shape=out_shape, name=name, compiler_params=_params("parallel", "arbitrary"))(a, b)


def _row_spec(r, tm):
    if isinstance(r, tuple):
        arr, width, cb = r
        return arr, pl.BlockSpec((tm, width), lambda i, j, cb=cb: (i, cb + j))
    return r, pl.BlockSpec((tm, r.shape[1]), lambda i, j: (i, j))


def _full_spec(p):
    return pl.BlockSpec(p.shape, lambda i, j: (0,) * p.ndim)


def _rows(fn, rows, params, outs, name, tm=ROW_TILE, ncol=1):
    arrs, specs = zip(*[_row_spec(r, tm) for r in rows])
    S = arrs[0].shape[0]
    nr, npar = len(rows), len(params)

    def body(*refs):
        vals = fn(*[r[...].astype(f32) for r in refs[:nr]], *[p[...] for p in refs[nr:nr + npar]])
        for o_ref, v in zip(refs[nr + npar:], vals):
            o_ref[...] = v.astype(o_ref.dtype)

    return pl.pallas_call(
        body, grid=(S // tm, ncol), in_specs=list(specs) + [_full_spec(p) for p in params],
        out_specs=[pl.BlockSpec((tm, w), lambda i, j: (i, j)) for w, _ in outs],
        out_shape=[SDS((S, w * ncol), dt) for w, dt in outs], name=name,
        compiler_params=_params("parallel", "parallel"))(*arrs, *params)


def _rows_vjp(fn, rows, params, cts, wrt_rows, wrt_params, name, adds=None, tm=ROW_TILE, ncol=1, row_dtype=f32):
    adds = adds or {}
    arrs, specs = zip(*[_row_spec(r, tm) for r in rows])
    carrs, cspecs = zip(*[_row_spec(c, tm) for c in cts])
    add_keys = sorted(adds)
    aarrs = [adds[k] for k in add_keys]
    S = arrs[0].shape[0]
    nr, npar, nc, na = len(rows), len(params), len(cts), len(aarrs)
    widths = [specs[k].block_shape[1] for k in wrt_rows]

    def body(*refs):
        first = jnp.logical_and(pl.program_id(0) == 0, pl.program_id(1) == 0)
        rv = [r[...].astype(f32) for r in refs[:nr]]
        pv = [p[...] for p in refs[nr:nr + npar]]
        cv = tuple(c[...].astype(f32) for c in refs[nr + npar:nr + npar + nc])
        av = dict(zip(add_keys, refs[nr + npar + nc:nr + npar + nc + na]))
        o = refs[nr + npar + nc + na:]
        _, vjp = jax.vjp(fn, *rv, *pv)
        g = vjp(cv)
        for n, k in enumerate(wrt_rows):
            val = g[k]
            if k in av:
                val = val + av[k][...]
            o[n][...] = val.astype(o[n].dtype)
        for n, k in enumerate(wrt_params):
            ref = o[len(wrt_rows) + n]

            @pl.when(first)
            def _(ref=ref):
                ref[...] = jnp.zeros_like(ref)

            ref[...] += g[nr + k]

    res = pl.pallas_call(
        body, grid=(S // tm, ncol),
        in_specs=list(specs) + [_full_spec(p) for p in params] + list(cspecs)
        + [pl.BlockSpec((tm, a.shape[1] // ncol), lambda i, j: (i, j)) for a in aarrs],
        out_specs=[pl.BlockSpec((tm, w), lambda i, j: (i, j)) for w in widths] + [_full_spec(params[k]) for k in wrt_params],
        out_shape=[SDS((S, w * ncol), row_dtype) for w in widths] + [SDS(params[k].shape, f32) for k in wrt_params],
        name=name, compiler_params=_params("arbitrary", "arbitrary"))(*arrs, *params, *carrs, *aarrs)
    return res[:len(wrt_rows)], res[len(wrt_rows):]


def _rms(x, w):
    return x * lax.rsqrt(jnp.mean(x * x, axis=-1, keepdims=True) + EPS) * w


def _rms_fn(x, w):
    return (_rms(x, w),)


def _res_rms_fn(f, res, w):
    return (res + _rms(f, w),)


def _swap_halves(x):
    lane = lax.broadcasted_iota(jnp.int32, x.shape, 1)
    first = (lane % HEAD_DIM) < (HEAD_DIM // 2)
    n = x.shape[1]
    return jnp.where(first, pltpu.roll(x, n - HEAD_DIM // 2, 1), pltpu.roll(x, HEAD_DIM // 2, 1))


def _rope_fwd_fn(q, k, cos, sgn_sin):
    scale = HEAD_DIM ** -0.5
    return ((q * cos + _swap_halves(q) * sgn_sin) * scale, k * cos + _swap_halves(k) * sgn_sin)


def _rope_bwd_fn(dq, dk, cos, sgn_sin):
    dq = dq * (HEAD_DIM ** -0.5)
    return (dq * cos + _swap_halves(dq * sgn_sin), dk * cos + _swap_halves(dk * sgn_sin))


def _nt(a, b):
    return lax.dot_general(a, b, (((1,), (1,)), ((), ())), preferred_element_type=f32)


def _tn(a, b):
    return lax.dot_general(a, b, (((0,), (0,)), ((), ())), preferred_element_type=f32)


def _band_rows(j, d, nb):
    r, i = j // nb, j % nb
    if d == 1:
        cur = pl.ds(pl.multiple_of(i * 128, 128), 128)
        prev = pl.ds(pl.multiple_of(jnp.maximum(i - 1, 0) * 128, 128), 128)
    else:
        cur = pl.ds(i * (128 * d) + r, 128, stride=d)
        prev = pl.ds(jnp.maximum(i - 1, 0) * (128 * d) + r, 128, stride=d)
    a = lax.broadcasted_iota(jnp.int32, (128, 128), 0)
    c = lax.broadcasted_iota(jnp.int32, (128, 128), 1)
    return cur, prev, c <= a, jnp.logical_and(c >= a, i != 0)


def _attn_fwd(qr, kr, proj, name):
    S = qr.shape[0]
    nblk = S // 128

    def body(q_ref, k_ref, v_ref, out_ref, lse_ref, *scr):
        head_a = lax.broadcasted_iota(jnp.int32, (1, 128), 1) < HEAD_DIM
        for b, (d, nb) in enumerate(BRANCHES):
            ob_ref, lb_ref = scr[2 * b], scr[2 * b + 1]

            def blk(j, carry, d=d, nb=nb, ob_ref=ob_ref, lb_ref=lb_ref):
                cur, prev, mc, mp = _band_rows(j, d, nb)
                q = q_ref[cur, :]
                kc, kp = k_ref[cur, :].astype(bf16), k_ref[prev, :].astype(bf16)
                vc, vp = v_ref[cur, :].astype(bf16), v_ref[prev, :].astype(bf16)
                res = []
                for m in (head_a, jnp.logical_not(head_a)):
                    qm = jnp.where(m, q, 0.0).astype(bf16)
                    sc = jnp.where(mc, _nt(qm, kc), NEG)
                    sp = jnp.where(mp, _nt(qm, kp), NEG)
                    mx = jnp.maximum(jnp.max(sc, axis=1, keepdims=True), jnp.max(sp, axis=1, keepdims=True))
                    pc, pp = jnp.exp(sc - mx), jnp.exp(sp - mx)
                    l = jnp.sum(pc, axis=1, keepdims=True) + jnp.sum(pp, axis=1, keepdims=True)
                    o = (jnp.dot(pc.astype(bf16), vc, preferred_element_type=f32)
                         + jnp.dot(pp.astype(bf16), vp, preferred_element_type=f32)) / l
                    res.append((o, mx + jnp.log(l)))
                ob_ref[cur, :] = jnp.where(head_a, res[0][0], res[1][0])
                lb_ref[cur, :] = jnp.where(head_a, res[0][1], res[1][1])
                return carry

            lax.fori_loop(0, nblk, blk, 0)
        l0, l1, l2 = scr[1][...], scr[3][...], scr[5][...]
        mx = jnp.maximum(jnp.maximum(l0, l1), l2)
        e0, e1, e2 = jnp.exp(l0 - mx), jnp.exp(l1 - mx), jnp.exp(l2 - mx)
        den = e0 + e1 + e2
        out_ref[...] = ((e0 * scr[0][...] + e1 * scr[2][...] + e2 * scr[4][...]) / den).astype(out_ref.dtype)
        lse_ref[...] = mx + jnp.log(den)

    pair = pl.BlockSpec((S, 128), lambda h: (0, h))
    return pl.pallas_call(
        body, grid=(N_HEADS_A // 2,),
        in_specs=[pair, pair, pl.BlockSpec((S, 128), lambda h: (0, 2 * ATTN_W // 128 + h))], out_specs=[pair, pair],
        out_shape=[SDS((S, ATTN_W), bf16), SDS((S, ATTN_W), f32)], scratch_shapes=[pltpu.VMEM((S, 128), f32)] * 6,
        name=name, compiler_params=_params("parallel"))(qr, kr, proj)


def _attn_bwd(qr, kr, proj, dmix_in, out, lse, name):
    S = qr.shape[0]
    nblk = S // 128

    def body(q_ref, k_ref, v_ref, do_ref, out_ref, lse_ref, dq_ref, dk_ref, dv_ref, t_ref):
        head_a = lax.broadcasted_iota(jnp.int32, (1, 128), 1) < HEAD_DIM
        x = do_ref[...] * out_ref[...].astype(f32)
        t_ref[...] = jnp.where(head_a, jnp.sum(jnp.where(head_a, x, 0.0), axis=1, keepdims=True),
                               jnp.sum(jnp.where(head_a, 0.0, x), axis=1, keepdims=True))
        dq_ref[...] = jnp.zeros_like(dq_ref)
        dk_ref[...] = jnp.zeros_like(dk_ref)
        dv_ref[...] = jnp.zeros_like(dv_ref)
        for d, nb in BRANCHES:
            def blk(j, carry, d=d, nb=nb):
                cur, prev, mc, mp = _band_rows(j, d, nb)
                q, do = q_ref[cur, :], do_ref[cur, :]
                kc, kp = k_ref[cur, :].astype(bf16), k_ref[prev, :].astype(bf16)
                vc, vp = v_ref[cur, :].astype(bf16), v_ref[prev, :].astype(bf16)
                t, lse_b = t_ref[cur, :], lse_ref[cur, :]
                dq = dkc = dkp = dvc = dvp = jnp.zeros((128, 128), f32)
                for m, off in ((head_a, 0), (jnp.logical_not(head_a), HEAD_DIM)):
                    th, lh = t[:, off:off + 1], lse_b[:, off:off + 1]
                    qm = jnp.where(m, q, 0.0).astype(bf16)
                    dom = jnp.where(m, do, 0.0).astype(bf16)
                    pc = jnp.exp(jnp.where(mc, _nt(qm, kc), NEG) - lh)
                    pp = jnp.exp(jnp.where(mp, _nt(qm, kp), NEG) - lh)
                    dsc = (pc * (_nt(dom, vc) - th)).astype(bf16)
                    dsp = (pp * (_nt(dom, vp) - th)).astype(bf16)
                    dq = dq + jnp.where(m, jnp.dot(dsc, kc, preferred_element_type=f32)
                                        + jnp.dot(dsp, kp, preferred_element_type=f32), 0.0)
                    dvc, dvp = dvc + _tn(pc.astype(bf16), dom), dvp + _tn(pp.astype(bf16), dom)
                    dkc, dkp = dkc + _tn(dsc, qm), dkp + _tn(dsp, qm)
                dq_ref[cur, :] += dq
                dk_ref[cur, :] += dkc
                dv_ref[cur, :] += dvc
                dk_ref[prev, :] += dkp
                dv_ref[prev, :] += dvp
                return carry

            lax.fori_loop(0, nblk, blk, 0)

    pair = pl.BlockSpec((S, 128), lambda h: (0, h))
    return pl.pallas_call(
        body, grid=(N_HEADS_A // 2,),
        in_specs=[pair, pair, pl.BlockSpec((S, 128), lambda h: (0, 2 * ATTN_W // 128 + h)), pair, pair, pair],
        out_specs=[pair] * 3, out_shape=[SDS((S, ATTN_W), f32)] * 3, scratch_shapes=[pltpu.VMEM((S, 128), f32)],
        name=name, compiler_params=_params("parallel"))(qr, kr, proj, dmix_in, out, lse)


def _conv_val(x, w, K, rows):
    acc = x * w[K - 1:K, :]
    for s in range(1, K):
        acc = acc + jnp.where(rows >= s, pltpu.roll(x, s, 0), 0.0) * w[K - 1 - s:K - s, :]
    return acc


def _colconv_fwd(xs, ws, bs, K, fn, nblk, tc, outs, name):
    S = xs[0][0].shape[0]
    n = len(xs)
    has_b = bs is not None

    def body(*refs):
        rows = lax.broadcasted_iota(jnp.int32, (S, tc), 0)
        cs = []
        for k in range(n):
            c = _conv_val(refs[k][...].astype(f32), refs[n + k][...], K, rows)
            if has_b:
                c = c + refs[2 * n + k][...]
            cs.append(c)
        for o_ref, val in zip(refs[(3 if has_b else 2) * n:], fn(*cs)):
            o_ref[...] = val.astype(o_ref.dtype)

    def cspec(rows_, cb0):
        return pl.BlockSpec((rows_, tc), lambda j, cb0=cb0: (0, cb0 + j))

    in_specs = [cspec(S, cb) for _, cb in xs] + [cspec(K, cb) for _, cb in ws]
    args = [a for a, _ in xs] + [a for a, _ in ws]
    if has_b:
        in_specs += [cspec(1, cb) for _, cb in bs]
        args += [a for a, _ in bs]
    return pl.pallas_call(
        body, grid=(nblk,), in_specs=in_specs, out_specs=[cspec(S, 0) for _ in outs],
        out_shape=[SDS((S, nblk * tc), dt) for dt in outs], name=name, compiler_params=_params("parallel"))(*args)


def _colconv_bwd(xs, ws, bs, K, fn, douts, nblk, tc, name, dx_dtype=f32):
    S = xs[0][0].shape[0]
    n, nd = len(xs), len(douts)
    has_b = bs is not None
    nin = (3 if has_b else 2) * n

    def body(*refs):
        rows = lax.broadcasted_iota(jnp.int32, (S, tc), 0)
        x = [refs[k][...].astype(f32) for k in range(n)]
        w = [refs[n + k][...] for k in range(n)]
        cs = []
        for k in range(n):
            c = _conv_val(x[k], w[k], K, rows)
            if has_b:
                c = c + refs[2 * n + k][...]
            cs.append(c)
        _, vjp = jax.vjp(fn, *cs)
        dcs = vjp(tuple(r[...].astype(f32) for r in refs[nin:nin + nd]))
        o = refs[nin + nd:]
        for k in range(n):
            dc = dcs[k]
            dx = dc * w[k][K - 1:K, :]
            o[n + k][K - 1:K, :] = jnp.sum(dc * x[k], axis=0, keepdims=True)
            for s in range(1, K):
                dx = dx + jnp.where(rows < S - s, pltpu.roll(dc, S - s, 0), 0.0) * w[k][K - 1 - s:K - s, :]
                xsh = jnp.where(rows >= s, pltpu.roll(x[k], s, 0), 0.0)
                o[n + k][K - 1 - s:K - s, :] = jnp.sum(dc * xsh, axis=0, keepdims=True)
            o[k][...] = dx.astype(o[k].dtype)
            if has_b:
                o[2 * n + k][...] = jnp.sum(dc, axis=0, keepdims=True)

    def cspec(rows_, cb0):
        return pl.BlockSpec((rows_, tc), lambda j, cb0=cb0: (0, cb0 + j))

    in_specs = [cspec(S, cb) for _, cb in xs] + [cspec(K, cb) for _, cb in ws]
    args = [a for a, _ in xs] + [a for a, _ in ws]
    if has_b:
        in_specs += [cspec(1, cb) for _, cb in bs]
        args += [a for a, _ in bs]
    in_specs += [cspec(S, 0) for _ in douts]
    args += list(douts)
    W = nblk * tc
    out_specs = [cspec(S, 0)] * n + [cspec(K, 0)] * n + ([cspec(1, 0)] * n if has_b else [])
    out_shape = [SDS((S, W), dx_dtype)] * n + [SDS((K, W), f32)] * n + ([SDS((1, W), f32)] * n if has_b else [])
    res = pl.pallas_call(body, grid=(nblk,), in_specs=in_specs, out_specs=out_specs, out_shape=out_shape,
                         name=name, compiler_params=_params("parallel"))(*args)
    return res[:n], res[n:2 * n], res[2 * n:]


def _silu_fn(c):
    return (c * jax.nn.sigmoid(c),)


def _geglu_fn(gate, up):
    gelu = 0.5 * gate * (1.0 + jnp.tanh(0.7978845608028654 * (gate + 0.044715 * gate * gate * gate)))
    return (gelu * up,)


def _softplus(x):
    u = jnp.exp(jnp.minimum(x, 20.0))
    small = u * (1.0 - 0.5 * u)
    return jnp.where(x > 20.0, x, jnp.where(u < 1e-4, small, jnp.log(1.0 + u)))


def _bmm(a, b, precision=None):
    return lax.dot_general(a, b, (((2,), (1,)), ((0,), (0,))), precision=precision, preferred_element_type=f32)


def _bnt(a, b):
    return lax.dot_general(a, b, (((2,), (2,)), ((0,), (0,))), preferred_element_type=f32)


def _btn(a, b):
    return lax.dot_general(a, b, (((1,), (1,)), ((0,), (0,))), preferred_element_type=f32)


def _unit_lower_inverse(A):
    n = A.shape[-1]
    eye = (lax.broadcasted_iota(jnp.int32, (1, n, n), 1) == lax.broadcasted_iota(jnp.int32, (1, n, n), 2)).astype(f32)
    P = -A
    T = eye + P
    for _ in range(5):
        P = _bmm(P, P, HI)
        T = T + _bmm(T, P, HI)
    return T


def _dn_prep_fn(q, k, v, ba, alog, dtb, h):
    G, C = q.shape[0], CHUNK
    lane = lax.broadcasted_iota(jnp.int32, (1, 1, 128), 2)

    def sel(arr, idx):
        return jnp.sum(jnp.where(lane == idx, arr, 0.0), axis=-1, keepdims=True)

    beta = jax.nn.sigmoid(sel(ba, h))
    g = -jnp.exp(sel(alog[None], h)) * _softplus(sel(ba, N_HEADS_D + h) + sel(dtb[None], h))
    qn = q * lax.rsqrt(jnp.sum(q * q, axis=-1, keepdims=True) + EPS) * (DK ** -0.5)
    kn = k * lax.rsqrt(jnp.sum(k * k, axis=-1, keepdims=True) + EPS)
    ii = lax.broadcasted_iota(jnp.int32, (1, C, C), 1)
    jj = lax.broadcasted_iota(jnp.int32, (1, C, C), 2)
    tril, strict = ii >= jj, ii > jj
    gsq = jnp.broadcast_to(g, (G, C, C))
    gcol = _bmm(jnp.broadcast_to(tril.astype(f32), (G, C, C)), gsq, HI)
    grow = _bmm(jnp.ones((G, C, C), f32), jnp.where(ii <= jj, gsq, 0.0), HI)
    decay = jnp.exp(jnp.where(tril, gcol - grow, NEG))
    gc = gcol[:, :, :1]
    glast = gcol[:, C - 1:C, :1]
    kb = kn * beta
    A = jnp.where(strict, _bnt(kb.astype(bf16), kn.astype(bf16)) * decay, 0.0)
    T = _unit_lower_inverse(A).astype(bf16)
    u = _bmm(T, (v * beta).astype(bf16))
    w = _bmm(T, (kb * jnp.exp(gc)).astype(bf16))
    qk = _bnt(qn.astype(bf16), kn.astype(bf16)) * decay
    qd = qn * jnp.exp(gc)
    kd = kn * jnp.exp(glast - gc)
    return u, w, qk, qd, kd, jnp.broadcast_to(jnp.exp(glast), (G, C, DK))


def _dn_scan_fn(u, w, qk, qd, kd, eg, St):
    b = lambda a: a.astype(bf16)
    vnew = u - _bmm(b(w), b(St))
    o = _bmm(b(qd), b(St)) + _bmm(b(qk), b(vnew))
    return o, St * eg[:, :1, :] + _btn(b(kd), b(vnew))


def _dn_post_fn(o, z, nw):
    return (_rms(o, nw) * (z * jax.nn.sigmoid(z)),)


DN_GROUP = 8


def _dn_prep_specs(S, rows):
    def col(first):
        return pl.BlockSpec((rows, DK), lambda i, h, first=first: (i, first // DK + h))

    par = pl.BlockSpec((1, 128), lambda i, h: (0, 0))
    return [col(0), col(N_HEADS_D * DK), col(2 * N_HEADS_D * DK),
            pl.BlockSpec((rows, 128), lambda i, h: (i, 3584 // 128)), par, par]


def _dn_prep(qkv, proj, alog, dtb, name):
    S = qkv.shape[0]
    G = DN_GROUP
    rows = G * CHUNK

    def body(q_ref, k_ref, v_ref, ba_ref, al_ref, dt_ref, u_ref, w_ref, qk_ref, qd_ref, kd_ref, eg_ref):
        h = pl.program_id(1)
        r3 = lambda ref: ref[...].reshape(G, CHUNK, 128)
        u, w, qk, qd, kd, eg = _dn_prep_fn(r3(q_ref), r3(k_ref), r3(v_ref), r3(ba_ref), al_ref[...], dt_ref[...], h)
        for ref, val in ((u_ref, u), (w_ref, w), (qd_ref, qd), (kd_ref, kd), (eg_ref, eg)):
            ref[...] = val.reshape(rows, DK)
        qk_ref[:, :CHUNK] = qk.reshape(rows, CHUNK)
        qk_ref[:, CHUNK:] = jnp.zeros((rows, DK - CHUNK), f32)

    out = pl.BlockSpec((rows, DK), lambda i, h: (i, h))
    return pl.pallas_call(
        body, grid=(S // rows, N_HEADS_D), in_specs=_dn_prep_specs(S, rows), out_specs=[out] * 6,
        out_shape=[SDS((S, N_HEADS_D * DK), f32)] * 6, name=name,
        compiler_params=_params("parallel", "parallel"))(qkv, qkv, qkv, proj, alog, dtb)


def _dn_prep_bwd(qkv, proj, alog, dtb, cts, name):
    S = qkv.shape[0]
    G = DN_GROUP
    rows = G * CHUNK

    def body(q_ref, k_ref, v_ref, ba_ref, al_ref, dt_ref, du_ref, dw_ref, dqk_ref, dqd_ref, dkd_ref, deg_ref,
             dq_ref, dk_ref, dv_ref, dba_ref, dal_ref, ddt_ref):
        i, h = pl.program_id(0), pl.program_id(1)
        r3 = lambda ref: ref[...].reshape(G, CHUNK, 128)
        _, vjp = jax.vjp(lambda q, k, v, ba, al, dt: _dn_prep_fn(q, k, v, ba, al, dt, h),
                         r3(q_ref), r3(k_ref), r3(v_ref), r3(ba_ref), al_ref[...], dt_ref[...])
        dqk = dqk_ref[:, :CHUNK].reshape(G, CHUNK, CHUNK)
        dq, dk, dv, dba, dal, ddt = vjp((r3(du_ref), r3(dw_ref), dqk, r3(dqd_ref), r3(dkd_ref), r3(deg_ref)))
        dq_ref[...] = dq.reshape(rows, DK)
        dk_ref[...] = dk.reshape(rows, DK)
        dv_ref[...] = dv.reshape(rows, DK)

        @pl.when(h == 0)
        def _():
            dba_ref[...] = jnp.zeros_like(dba_ref)

        @pl.when(jnp.logical_and(i == 0, h == 0))
        def _():
            dal_ref[...] = jnp.zeros_like(dal_ref)
            ddt_ref[...] = jnp.zeros_like(ddt_ref)

        dba_ref[...] += dba.reshape(rows, 128)
        dal_ref[...] += dal
        ddt_ref[...] += ddt

    hcol = pl.BlockSpec((rows, DK), lambda i, h: (i, h))
    par = pl.BlockSpec((1, 128), lambda i, h: (0, 0))
    W = N_HEADS_D * DK
    return pl.pallas_call(
        body, grid=(S // rows, N_HEADS_D), in_specs=_dn_prep_specs(S, rows) + [hcol] * 6,
        out_specs=[hcol] * 3 + [pl.BlockSpec((rows, 128), lambda i, h: (i, 0)), par, par],
        out_shape=[SDS((S, W), f32)] * 3 + [SDS((S, 128), f32), SDS((1, 128), f32), SDS((1, 128), f32)], name=name,
        compiler_params=_params("arbitrary", "arbitrary"))(qkv, qkv, qkv, proj, alog, dtb, *cts)


def _heads(x):
    return jnp.stack([x[:, DK * h:DK * (h + 1)] for h in range(N_HEADS_D)])


def _dn_scan(pre, name):
    S = pre[0].shape[0]
    NCH = S // CHUNK

    def body(u_ref, w_ref, qk_ref, qd_ref, kd_ref, eg_ref, o_ref, st_ref, s_ref):
        @pl.when(pl.program_id(0) == 0)
        def _():
            s_ref[...] = jnp.zeros_like(s_ref)

        St = s_ref[...]
        st_ref[0] = St
        o, Sn = _dn_scan_fn(_heads(u_ref[...]), _heads(w_ref[...]), _heads(qk_ref[...])[:, :, :CHUNK], _heads(qd_ref[...]),
                            _heads(kd_ref[...]), _heads(eg_ref[...]), St)
        for h in range(N_HEADS_D):
            o_ref[:, DK * h:DK * (h + 1)] = o[h]
        s_ref[...] = Sn

    blk = pl.BlockSpec((CHUNK, N_HEADS_D * DK), lambda n: (n, 0))
    return pl.pallas_call(
        body, grid=(NCH,), in_specs=[blk] * 6,
        out_specs=[blk, pl.BlockSpec((1, N_HEADS_D, DK, DK), lambda n: (n, 0, 0, 0))],
        out_shape=[SDS((S, N_HEADS_D * DK), f32), SDS((NCH, N_HEADS_D, DK, DK), f32)],
        scratch_shapes=[pltpu.VMEM((N_HEADS_D, DK, DK), f32)], name=name, compiler_params=_params("arbitrary"))(*pre)


def _dn_scan_bwd(pre, states, do, name):
    S = do.shape[0]
    NCH = S // CHUNK

    def body(u_ref, w_ref, qk_ref, qd_ref, kd_ref, eg_ref, st_ref, do_ref,
             du_ref, dw_ref, dqk_ref, dqd_ref, dkd_ref, deg_ref, ds_ref):
        @pl.when(pl.program_id(0) == 0)
        def _():
            ds_ref[...] = jnp.zeros_like(ds_ref)

        _, vjp = jax.vjp(_dn_scan_fn, _heads(u_ref[...]), _heads(w_ref[...]), _heads(qk_ref[...])[:, :, :CHUNK],
                         _heads(qd_ref[...]), _heads(kd_ref[...]), _heads(eg_ref[...]), st_ref[0])
        du, dw, dqk, dqd, dkd, deg, dS = vjp((_heads(do_ref[...]), ds_ref[...]))
        ds_ref[...] = dS
        for h in range(N_HEADS_D):
            c = slice(DK * h, DK * (h + 1))
            for ref, val in ((du_ref, du), (dw_ref, dw), (dqd_ref, dqd), (dkd_ref, dkd), (deg_ref, deg)):
                ref[:, c] = val[h]
            dqk_ref[:, DK * h:DK * h + CHUNK] = dqk[h]
            dqk_ref[:, DK * h + CHUNK:DK * (h + 1)] = jnp.zeros((CHUNK, DK - CHUNK), f32)

    blk = pl.BlockSpec((CHUNK, N_HEADS_D * DK), lambda n: (NCH - 1 - n, 0))
    return pl.pallas_call(
        body, grid=(NCH,),
        in_specs=[blk] * 6 + [pl.BlockSpec((1, N_HEADS_D, DK, DK), lambda n: (NCH - 1 - n, 0, 0, 0)), blk],
        out_specs=[blk] * 6, out_shape=[SDS((S, N_HEADS_D * DK), f32)] * 6,
        scratch_shapes=[pltpu.VMEM((N_HEADS_D, DK, DK), f32)], name=name,
        compiler_params=_params("arbitrary"))(*pre, states, do)


def _loss_head(y, t, name):
    S, D = y.shape
    tm = ROW_TILE

    def body(y_ref, t_ref, dy_ref, l_ref):
        i = pl.program_id(0)
        d = y_ref[...] - t_ref[...]
        dy_ref[...] = d * (1.0 / D)
        part = jnp.sum(jnp.sum(d * d, axis=1, keepdims=True), axis=0, keepdims=True) * (0.5 / D)

        @pl.when(i == 0)
        def _():
            l_ref[...] = jnp.zeros_like(l_ref)

        l_ref[...] += jnp.broadcast_to(part, l_ref.shape)

    spec = pl.BlockSpec((tm, D), lambda i: (i, 0))
    dy, l = pl.pallas_call(body, grid=(S // tm,), in_specs=[spec, spec],
                           out_specs=[spec, pl.BlockSpec((1, 128), lambda i: (0, 0))],
                           out_shape=[SDS((S, D), f32), SDS((1, 128), f32)], name=name,
                           compiler_params=_params("arbitrary"))(y, t)
    return l[0, 0], dy


def _adamw(w, g, m, v, tr, name):
    L, R, C = w.shape
    assert R % tr == 0

    def body(w_ref, g_ref, m_ref, v_ref, d_ref, mo_ref, vo_ref):
        gv = g_ref[...]
        m2 = ADAM_B1 * m_ref[...] + (1.0 - ADAM_B1) * gv
        v2 = ADAM_B2 * v_ref[...] + (1.0 - ADAM_B2) * (gv * gv)
        m_hat = m2 / (1.0 - ADAM_B1 ** ADAM_STEP)
        v_hat = v2 / (1.0 - ADAM_B2 ** ADAM_STEP)
        d_ref[...] = -ADAM_LR * (m_hat / (jnp.sqrt(v_hat) + ADAM_EPS) + ADAM_WD * w_ref[...])
        mo_ref[...] = m2
        vo_ref[...] = v2

    spec = pl.BlockSpec((1, tr, C), lambda l, i: (l, i, 0))
    return pl.pallas_call(body, grid=(L, R // tr), in_specs=[spec] * 4, out_specs=[spec] * 3,
                          out_shape=[SDS((L, R, C), f32)] * 3, name=name,
                          compiler_params=_params("parallel", "parallel"))(w, g, m, v)


def _rope_tables(S):
    inv = 1.0 / (10000.0 ** (jnp.arange(0, HEAD_DIM, 2, dtype=f32) / HEAD_DIM))
    ang = jnp.arange(S, dtype=f32)[:, None] * inv[None, :]
    cos, sin = jnp.cos(ang), jnp.sin(ang)
    return (jnp.tile(jnp.concatenate([cos, cos], axis=1), (1, N_HEADS_A)),
            jnp.tile(jnp.concatenate([-sin, sin], axis=1), (1, N_HEADS_A)))


def _layer_fwd(x, W, cos, sgn_sin, l):
    n = f"l{l}_"
    (h1,) = _rows(_rms_fn, [x], [W["norm_pre_mix"]], [(D_MODEL, bf16)], n + "pre_mix_norm")
    proj = _mm(h1, W["w_in"], "nn", 512, 768, f32, n + "in_proj")
    qr, kr = _rows(_rope_fwd_fn, [(proj, ATTN_W, 0), (proj, ATTN_W, 1), cos, sgn_sin], [],
                   [(ATTN_W, f32), (ATTN_W, f32)], n + "rope")
    attn_out, lse = _attn_fwd(qr, kr, proj, n + "attn_fwd")
    (qkv,) = _colconv_fwd([(proj, 3)], [(W["dn_conv_w"], 0)], None, 4, _silu_fn, 3, 512, [f32], n + "dn_conv")
    dn_pre = _dn_prep(qkv, proj, W["dn_a_log"], W["dn_dt_bias"], n + "dn_prep")
    dn_o, dn_states = _dn_scan(dn_pre, n + "dn_scan")
    (dn_out,) = _rows(_dn_post_fn, [(dn_o, DK, 0), (proj, DK, 3072 // DK)], [W["dn_norm_w"]], [(DK, bf16)], n + "dn_post",
                      ncol=N_HEADS_D)
    mix_in = jnp.concatenate([attn_out, dn_out], axis=1)
    mix = _mm(mix_in, W["w_out"], "nn", 512, 512, f32, n + "out_proj")
    (x1,) = _rows(_res_rms_fn, [mix, x], [W["norm_post_mix"]], [(D_MODEL, f32)], n + "post_mix_norm")
    (h2,) = _rows(_rms_fn, [x1], [W["norm_pre_ffn"]], [(D_MODEL, bf16)], n + "pre_ffn_norm")
    u0 = _mm(h2, W["ffn_w_in"], "nn", 1024, 512, bf16, n + "ffn_in")
    nb_ff = D_FF // 256
    (act,) = _colconv_fwd([(u0, 0), (u0, nb_ff)], [(W["ffn_conv_w"], 0), (W["ffn_conv_w"], nb_ff)],
                          [(W["ffn_conv_b"], 0), (W["ffn_conv_b"], nb_ff)], 3, _geglu_fn, nb_ff, 256, [bf16],
                          n + "ffn_conv_glu")
    f = _mm(act, W["ffn_w_out"], "nn", 512, 512, f32, n + "ffn_out")
    (x2,) = _rows(_res_rms_fn, [f, x1], [W["norm_post_ffn"]], [(D_MODEL, f32)], n + "post_ffn_norm")
    saved = dict(x=x, h1=h1, proj=proj, qr=qr, kr=kr, attn_out=attn_out, lse=lse, qkv=qkv, dn_pre=dn_pre, dn_o=dn_o,
                 dn_states=dn_states, mix_in=mix_in, mix=mix, x1=x1, h2=h2, u0=u0, act=act, f=f)
    return x2, saved


def _layer_bwd(dx2, sv, W, cos, sgn_sin, l):
    n = f"l{l}_"
    S = dx2.shape[0]
    g = {}
    (df,), (g["norm_post_ffn"],) = _rows_vjp(_rms_fn, [sv["f"]], [W["norm_post_ffn"]], [dx2], [0], [0],
                                             n + "post_ffn_norm_bwd", row_dtype=bf16)
    dact = _mm(df, W["ffn_w_out"], "nt", 512, 1408, f32, n + "ffn_out_dx")
    g["ffn_w_out"] = _mm(sv["act"], df, "tn", 256, 1024, f32, n + "ffn_out_dw")
    nb_ff = D_FF // 256
    u0 = sv["u0"]
    dxs, dws, dbs = _colconv_bwd([(u0, 0), (u0, nb_ff)], [(W["ffn_conv_w"], 0), (W["ffn_conv_w"], nb_ff)],
                                 [(W["ffn_conv_b"], 0), (W["ffn_conv_b"], nb_ff)], 3, _geglu_fn, [dact], nb_ff, 256,
                                 n + "ffn_conv_glu_bwd", dx_dtype=bf16)
    du0 = jnp.concatenate(dxs, axis=1)
    g["ffn_conv_w"] = jnp.concatenate(dws, axis=1)
    g["ffn_conv_b"] = jnp.concatenate(dbs, axis=1)
    dh2 = _mm(du0, W["ffn_w_in"], "nt", 512, 512, f32, n + "ffn_in_dx")
    g["ffn_w_in"] = _mm(sv["h2"], du0, "tn", 512, D_FF // 2, f32, n + "ffn_in_dw", column_shards=True)
    (dx1,), (g["norm_pre_ffn"],) = _rows_vjp(_rms_fn, [sv["x1"]], [W["norm_pre_ffn"]], [dh2], [0], [0],
                                             n + "pre_ffn_norm_bwd", adds={0: dx2})
    (dmix,), (g["norm_post_mix"],) = _rows_vjp(_rms_fn, [sv["mix"]], [W["norm_post_mix"]], [dx1], [0], [0],
                                               n + "post_mix_norm_bwd", row_dtype=bf16)
    dmix_in = _mm(dmix, W["w_out"], "nt", 512, 512, f32, n + "out_proj_dx")
    g["w_out"] = _mm(sv["mix_in"], dmix, "tn", 512, 512, f32, n + "out_proj_dw")

    (ddn_o, dz), (g["dn_norm_w"],) = _rows_vjp(
        _dn_post_fn, [(sv["dn_o"], DK, 0), (sv["proj"], DK, 3072 // DK)], [W["dn_norm_w"]], [(dmix_in, DK, ATTN_W // DK)],
        [0, 1], [0], n + "dn_post_bwd", ncol=N_HEADS_D)
    dpre = _dn_scan_bwd(sv["dn_pre"], sv["dn_states"], ddn_o, n + "dn_scan_bwd")
    dq, dk, dv, dba, g["dn_a_log"], g["dn_dt_bias"] = _dn_prep_bwd(
        sv["qkv"], sv["proj"], W["dn_a_log"], W["dn_dt_bias"], dpre, n + "dn_prep_bwd")
    dqkv = jnp.concatenate([dq, dk, dv], axis=1)
    (dqkv0,), (g["dn_conv_w"],), _ = _colconv_bwd([(sv["proj"], 3)], [(W["dn_conv_w"], 0)], None, 4, _silu_fn,
                                                 [dqkv], 3, 512, n + "dn_conv_bwd")

    dqr, dkr, dav = _attn_bwd(sv["qr"], sv["kr"], sv["proj"], dmix_in, sv["attn_out"], sv["lse"], n + "attn_bwd")
    daq, dak = _rows(_rope_bwd_fn, [dqr, dkr, cos, sgn_sin], [], [(ATTN_W, f32)] * 2, n + "rope_bwd")
    dproj = jnp.concatenate([daq, dak, dav, dqkv0, dz, dba, jnp.zeros((S, PROJ_W - 3712), f32)], axis=1).astype(bf16)
    dh1 = _mm(dproj, W["w_in"], "nt", 512, 512, f32, n + "in_proj_dx")
    g["w_in"] = _mm(sv["h1"], dproj, "tn", 512, 768, f32, n + "in_proj_dw")
    (dx,), (g["norm_pre_mix"],) = _rows_vjp(_rms_fn, [sv["x"]], [W["norm_pre_mix"]], [dh1], [0], [0],
                                            n + "pre_mix_norm_bwd", adds={0: dx1})
    return dx, g


def _local_step(x, target, layers):
    cos, sgn_sin = _rope_tables(x.shape[0])
    saved = []
    for l, W in enumerate(layers):
        x, sv = _layer_fwd(x, W, cos, sgn_sin, l)
        saved.append(sv)
    loss, dx = _loss_head(x, target, "loss_head")
    grads = [None] * len(layers)
    for l in reversed(range(len(layers))):
        dx, grads[l] = _layer_bwd(dx, saved[l], layers[l], cos, sgn_sin, l)
    return loss, dx, grads


def _pos():
    x, y, c = lax.axis_index("x"), lax.axis_index("y"), lax.axis_index("c")
    return x, y, c, [(1 - x, y), (x, 1 - y), (1 - x, 1 - y)]


def _rcopy(src, dst, send_sem, recv_sem, dev):
    return pltpu.make_async_remote_copy(src_ref=src, dst_ref=dst, send_sem=send_sem, recv_sem=recv_sem,
                                        device_id=dev, device_id_type=MESH)


def _half_rows(ref, h, which, axis):
    if h is None:
        return ref
    rows = pl.ds(pl.multiple_of(which * h, 16), h)
    return ref.at[:, rows, :] if axis == 1 else ref.at[rows, :]


def _dma_sems(*counts):
    return [pltpu.SemaphoreType.DMA((k,)) for k in counts]


def _all_gather(arrs, halves, name):
    n = len(arrs)

    def body(*refs):
        ins, outs = refs[:n], refs[n:2 * n]
        send1, recv1, send2, recv2 = refs[2 * n:]
        x, y, c, chips = _pos()
        me, sib, s_me = (x, y, c), (x, y, 1 - c), 2 * x + y
        sends = []
        for i in range(n):
            for j, chip in enumerate(chips):
                cp = _rcopy(_half_rows(ins[i], halves[i], c, 1), _half_rows(outs[i].at[s_me], halves[i], c, 1),
                            send1.at[3 * i + j], recv1.at[3 * i + j], (*chip, c))
                cp.start()
                sends.append(cp)
        for i in range(n):
            for j, (px, py) in enumerate(chips):
                k = 3 * i + j
                landed = _half_rows(outs[i].at[2 * px + py], halves[i], c, 1)
                _rcopy(landed, landed, send1.at[k], recv1.at[k], me).wait_recv()
                if halves[i] is not None:
                    cp = _rcopy(landed, landed, send2.at[k], recv2.at[k], sib)
                    cp.start()
                    sends.append(cp)
        for i in range(n):
            if halves[i] is None:
                continue
            for j, (px, py) in enumerate(chips):
                k = 3 * i + j
                other = _half_rows(outs[i].at[2 * px + py], halves[i], 1 - c, 1)
                _rcopy(other, other, send2.at[k], recv2.at[k], me).wait_recv()
        for cp in sends:
            cp.wait_send()

    return pl.pallas_call(
        body, in_specs=[ANY] * n, out_specs=[ANY] * n,
        out_shape=[SDS((4,) + a.shape, a.dtype) for a in arrs],
        scratch_shapes=_dma_sems(3 * n, 3 * n, 3 * n, 3 * n), name=name)(*arrs)


HBM = pl.BlockSpec(memory_space=pltpu.HBM)
SEM = pl.BlockSpec(memory_space=pltpu.SEMAPHORE)
_EFFECT = pltpu.SideEffectType.DATAFLOW_SIDE_EFFECTING


def _in_hbm(a):
    return pltpu.with_memory_space_constraint(a, pltpu.HBM)


def _split_copy(srcs, land_shapes, plan, after, name):
    n = len(srcs)
    k = 3 * n

    def body(*refs):
        ins, lands, token = refs[:n], refs[n:2 * n], refs[-1]
        send, recv = refs[2 * n + 1], refs[2 * n + 2]
        for i, (src, dst, dev, _) in enumerate(plan(ins, lands)):
            _rcopy(src, dst, send.at[i], recv.at[i], dev).start()
        token[...] = jnp.zeros_like(token)

    lands = [_in_hbm(lax.empty(s.shape, s.dtype)) for s in land_shapes]
    return pl.pallas_call(
        body, name=name,
        out_shape=(pltpu.SemaphoreType.DMA((k,)), pltpu.SemaphoreType.DMA((k,)),
                   *[pltpu.HBM(a.shape, a.dtype) for a in srcs], *[pltpu.HBM(s.shape, s.dtype) for s in land_shapes],
                   SDS((8, 128), f32)),
        in_specs=[HBM] * (2 * n) + [ANY], out_specs=(SEM, SEM, *[HBM] * (2 * n), pl.BlockSpec(memory_space=pltpu.VMEM)),
        input_output_aliases={i: 2 + i for i in range(2 * n)},
        compiler_params=pltpu.CompilerParams(has_side_effects=_EFFECT))(*[_in_hbm(a) for a in srcs], *lands, after)


def _split_wait(started, n, plan, after, name):
    send, recv = started[0], started[1]
    thru = started[2:2 + 2 * n]

    def body(*refs):
        ins, lands = refs[:n], refs[n:2 * n]
        send_ref, recv_ref = refs[2 * n], refs[2 * n + 1]
        for i, (src, _, dev, mine) in enumerate(plan(ins, lands)):
            cp = _rcopy(src, mine, send_ref.at[i], recv_ref.at[i], dev)
            cp.wait_send()
            cp.wait_recv()

    res = pl.pallas_call(
        body, name=name, out_shape=tuple(pltpu.HBM(a.shape, a.dtype) for a in thru),
        in_specs=[HBM] * (2 * n) + [SEM, SEM, ANY], out_specs=tuple([HBM] * (2 * n)),
        input_output_aliases={i: i for i in range(2 * n)},
        compiler_params=pltpu.CompilerParams(has_side_effects=_EFFECT))(*thru, send, recv, after)
    return res[:n], res[n:]


def _gather_plan(halves):
    def plan(ins, lands):
        x, y, c, chips = _pos()
        out = []
        for i in range(len(ins)):
            for px, py in chips:
                out.append((_half_rows(ins[i], halves[i], c, 1), _half_rows(lands[i].at[2 * x + y], halves[i], c, 1),
                            (px, py, c), _half_rows(lands[i].at[2 * px + py], halves[i], c, 1)))
        return out
    return plan


def _scatter_plan(ins, lands):
    x, y, c, chips = _pos()
    out = []
    for i in range(len(ins)):
        for j, (px, py) in enumerate(chips):
            out.append((ins[i].at[2 * px + py], lands[i].at[j], (px, py, c), lands[i].at[j]))
    return out


def _pass_to_sibling(lands, halves, name):
    n = len(lands)

    def body(*refs):
        outs = refs[n:2 * n]
        send, recv = refs[2 * n:]
        x, y, c, chips = _pos()
        sends = []
        for i in range(n):
            for j, (px, py) in enumerate(chips):
                landed = _half_rows(outs[i].at[2 * px + py], halves[i], c, 1)
                cp = _rcopy(landed, landed, send.at[3 * i + j], recv.at[3 * i + j], (x, y, 1 - c))
                cp.start()
                sends.append(cp)
        for i in range(n):
            for j, (px, py) in enumerate(chips):
                other = _half_rows(outs[i].at[2 * px + py], halves[i], 1 - c, 1)
                _rcopy(other, other, send.at[3 * i + j], recv.at[3 * i + j], (x, y, c)).wait_recv()
        for cp in sends:
            cp.wait_send()

    return pl.pallas_call(
        body, in_specs=[ANY] * n, out_specs=[ANY] * n, out_shape=[SDS(a.shape, a.dtype) for a in lands],
        input_output_aliases={k: k for k in range(n)}, scratch_shapes=_dma_sems(3 * n, 3 * n), name=name)(*lands)


def _exchange_halves(gs, name):
    n = len(gs)

    def body(*refs):
        ins, outs = refs[:n], refs[n:2 * n]
        send, recv = refs[2 * n:]
        x, y, c, _ = _pos()
        sends = []
        for k in range(n):
            cp = _rcopy(_half_rows(ins[k], gs[k].shape[1] // 2, 1 - c, 1), outs[k], send.at[k], recv.at[k], (x, y, 1 - c))
            cp.start()
            sends.append(cp)
        for k in range(n):
            _rcopy(outs[k], outs[k], send.at[k], recv.at[k], (x, y, c)).wait_recv()
        for cp in sends:
            cp.wait_send()

    return pl.pallas_call(
        body, in_specs=[ANY] * n, out_specs=[ANY] * n,
        out_shape=[SDS((4, g.shape[1] // 2, g.shape[2]), g.dtype) for g in gs],
        scratch_shapes=_dma_sems(n, n), name=name)(*gs)


def _scatter_partials(ps, name):
    n = len(ps)

    def body(*refs):
        ins, outs = refs[:n], refs[n:2 * n]
        send, recv = refs[2 * n:]
        x, y, c, chips = _pos()
        sends = []
        for k in range(n):
            for j, (px, py) in enumerate(chips):
                cp = _rcopy(ins[k].at[2 * px + py], outs[k].at[j], send.at[3 * k + j], recv.at[3 * k + j], (px, py, c))
                cp.start()
                sends.append(cp)
        for k in range(n):
            for j in range(3):
                _rcopy(outs[k].at[j], outs[k].at[j], send.at[3 * k + j], recv.at[3 * k + j], (x, y, c)).wait_recv()
        for cp in sends:
            cp.wait_send()

    return pl.pallas_call(
        body, in_specs=[ANY] * n, out_specs=[ANY] * n,
        out_shape=[SDS((3,) + p.shape[1:], p.dtype) for p in ps],
        scratch_shapes=_dma_sems(3 * n, 3 * n), name=name)(*ps)


def _join_halves(rs, name):
    n = len(rs)

    def body(*refs):
        outs = refs[n:2 * n]
        send, recv = refs[2 * n:]
        x, y, c, _ = _pos()
        sends = []
        for k in range(n):
            mine = _half_rows(outs[k], rs[k].shape[0] // 2, c, 0)
            cp = _rcopy(mine, mine, send.at[k], recv.at[k], (x, y, 1 - c))
            cp.start()
            sends.append(cp)
        for k in range(n):
            other = _half_rows(outs[k], rs[k].shape[0] // 2, 1 - c, 0)
            _rcopy(other, other, send.at[k], recv.at[k], (x, y, c)).wait_recv()
        for cp in sends:
            cp.wait_send()

    return pl.pallas_call(
        body, in_specs=[ANY] * n, out_specs=[ANY] * n, out_shape=[SDS(r.shape, r.dtype) for r in rs],
        input_output_aliases={k: k for k in range(n)}, scratch_shapes=_dma_sems(n, n), name=name)(*rs)


def _all_reduce_small(pack, name):
    R = pack.shape[0]

    def body(in_ref, out_ref, buf, send, recv):
        x, y, c, _ = _pos()
        me = 4 * x + 2 * y + c
        buf[me] = in_ref[...]
        sends = []
        for k in range(1, 8):
            peer = me ^ k
            cp = _rcopy(buf.at[me], buf.at[me], send.at[k - 1], recv.at[k - 1], ((peer >> 2) & 1, (peer >> 1) & 1, peer & 1))
            cp.start()
            sends.append(cp)
        for k in range(1, 8):
            _rcopy(buf.at[me ^ k], buf.at[me ^ k], send.at[k - 1], recv.at[k - 1], (x, y, c)).wait_recv()
        for cp in sends:
            cp.wait_send()
        acc = buf[0]
        for d in range(1, 8):
            acc = acc + buf[d]
        out_ref[...] = acc

    return pl.pallas_call(
        body, out_shape=SDS((R, 128), f32),
        in_specs=[pl.BlockSpec(memory_space=pltpu.VMEM)], out_specs=pl.BlockSpec(memory_space=pltpu.VMEM),
        scratch_shapes=[pltpu.VMEM((8, R, 128), f32)] + _dma_sems(7, 7), name=name)(pack)


def _add_sibling(g, recv, c_arr, tr, name):
    _, R, C = g.shape
    h = R // 2
    nrb = h // tr
    assert h % tr == 0

    def body(c_ref, g_ref, r_ref, o_ref):
        o_ref[...] = (g_ref[...] + r_ref[...]).astype(o_ref.dtype)

    spec = pl.BlockSpec((1, tr, C), lambda s, r, c_ref: (s, r, 0))
    grid_spec = pltpu.PrefetchScalarGridSpec(
        num_scalar_prefetch=1, grid=(4, nrb),
        in_specs=[pl.BlockSpec((1, tr, C), lambda s, r, c_ref: (s, c_ref[0] * nrb + r, 0)), spec], out_specs=spec)
    return pl.pallas_call(body, grid_spec=grid_spec, out_shape=SDS((4, h, C), bf16), name=name,
                          compiler_params=_params("parallel", "parallel"))(c_arr, g, recv)


def _add_chips(p, recv, sc_arr, tr, name):
    _, h, C = p.shape
    nrb = h // tr
    assert h % tr == 0

    def body(sc_ref, p_ref, r_ref, o_ref):
        o_ref[...] = (p_ref[0].astype(f32) + r_ref[0].astype(f32)) + (r_ref[1].astype(f32) + r_ref[2].astype(f32))

    grid_spec = pltpu.PrefetchScalarGridSpec(
        num_scalar_prefetch=1, grid=(nrb,),
        in_specs=[pl.BlockSpec((1, tr, C), lambda r, sc_ref: (sc_ref[0], r, 0)),
                  pl.BlockSpec((3, tr, C), lambda r, sc_ref: (0, r, 0))],
        out_specs=pl.BlockSpec((tr, C), lambda r, sc_ref: (sc_ref[1] * nrb + r, 0)))
    return pl.pallas_call(body, grid_spec=grid_spec, out_shape=SDS((2 * h, C), f32), name=name,
                          compiler_params=_params("parallel"))(sc_arr, p, recv)


_BIG = (("w_in", 1024, 256), ("w_out", 256, 128), ("ffn_w_in", 1024, 256), ("ffn_w_out", 704, 352))
_SMALL = ("dn_conv_w", "ffn_conv_w", "ffn_conv_b", "norm_pre_mix", "norm_post_mix", "norm_pre_ffn", "norm_post_ffn",
          "dn_norm_w", "dn_a_log", "dn_dt_bias")
_WEIGHTS = ("w_in", "dn_conv_w", "dn_a_log", "dn_dt_bias", "dn_norm_w", "w_out", "ffn_w_in", "ffn_conv_w", "ffn_conv_b",
            "ffn_w_out", "norm_pre_mix", "norm_post_mix", "norm_pre_ffn", "norm_post_ffn")
_ADAM_ROWS = {"w_in": 256, "w_out": 256, "ffn_w_in": 128, "ffn_w_out": 176}


def _shard_major(name, g):
    if name == "w_in":
        return jnp.stack([g[:, 898 * s:898 * (s + 1)] for s in range(4)])
    if name == "ffn_w_in":
        return g
    return g.reshape(4, g.shape[0] // 4, g.shape[1])


def kernel(x, w_in, dn_conv_w, dn_a_log, dn_dt_bias, dn_norm_w, w_out, ffn_w_in, ffn_conv_w, ffn_conv_b, ffn_w_out, norm_pre_mix, norm_post_mix, norm_pre_ffn, norm_post_ffn, loss_target, m_w_in, m_dn_conv_w, m_dn_a_log, m_dn_dt_bias, m_dn_norm_w, m_w_out, m_ffn_w_in, m_ffn_conv_w, m_ffn_conv_b, m_ffn_w_out, m_norm_pre_mix, m_norm_post_mix, m_norm_pre_ffn, m_norm_post_ffn, v_w_in, v_dn_conv_w, v_dn_a_log, v_dn_dt_bias, v_dn_norm_w, v_w_out, v_ffn_w_in, v_ffn_conv_w, v_ffn_conv_b, v_ffn_w_out, v_norm_pre_mix, v_norm_post_mix, v_norm_pre_ffn, v_norm_post_ffn):
    w = dict(w_in=w_in, dn_conv_w=dn_conv_w, dn_a_log=dn_a_log, dn_dt_bias=dn_dt_bias, dn_norm_w=dn_norm_w, w_out=w_out,
             ffn_w_in=ffn_w_in, ffn_conv_w=ffn_conv_w, ffn_conv_b=ffn_conv_b, ffn_w_out=ffn_w_out, norm_pre_mix=norm_pre_mix,
             norm_post_mix=norm_post_mix, norm_pre_ffn=norm_pre_ffn, norm_post_ffn=norm_post_ffn)
    m = dict(w_in=m_w_in, dn_conv_w=m_dn_conv_w, dn_a_log=m_dn_a_log, dn_dt_bias=m_dn_dt_bias, dn_norm_w=m_dn_norm_w,
             w_out=m_w_out, ffn_w_in=m_ffn_w_in, ffn_conv_w=m_ffn_conv_w, ffn_conv_b=m_ffn_conv_b, ffn_w_out=m_ffn_w_out,
             norm_pre_mix=m_norm_pre_mix, norm_post_mix=m_norm_post_mix, norm_pre_ffn=m_norm_pre_ffn,
             norm_post_ffn=m_norm_post_ffn)
    v = dict(w_in=v_w_in, dn_conv_w=v_dn_conv_w, dn_a_log=v_dn_a_log, dn_dt_bias=v_dn_dt_bias, dn_norm_w=v_dn_norm_w,
             w_out=v_w_out, ffn_w_in=v_ffn_w_in, ffn_conv_w=v_ffn_conv_w, ffn_conv_b=v_ffn_conv_b, ffn_w_out=v_ffn_w_out,
             norm_pre_mix=v_norm_pre_mix, norm_post_mix=v_norm_post_mix, norm_pre_ffn=v_norm_pre_ffn,
             norm_post_ffn=v_norm_post_ffn)
    xi, yi, ci = lax.axis_index("x"), lax.axis_index("y"), lax.axis_index("c")
    s_me = 2 * xi + yi
    c_arr = jnp.reshape(ci, (1,)).astype(jnp.int32)
    sc_arr = jnp.stack([s_me, ci]).astype(jnp.int32)

    mats = [name for name, _, _ in _BIG]
    halves = [rows // 2 for _, rows, _ in _BIG]
    tiles = {name: tr for name, _, tr in _BIG}

    own = {k: w[k].astype(bf16) for k in mats}
    got0 = _all_gather([own[k][0:1] for k in mats] + [dn_conv_w, ffn_conv_w], halves + [None, None], "weights_gather_l0")
    plan1 = _gather_plan(halves)
    src1 = [own[k][1:2] for k in mats]
    started = _split_copy(src1, [SDS((4,) + a.shape, a.dtype) for a in src1], plan1, got0[0], "weights_gather_l1_start")

    def pick(mine, gathered):
        return [jnp.where(s_me == s, mine, gathered[s]) for s in range(4)]

    conv = {"dn_conv_w": jnp.concatenate(pick(dn_conv_w, got0[4]), axis=-1),
            "ffn_conv_w": jnp.concatenate(pick(ffn_conv_w, got0[5]), axis=-1)}
    lanes = lambda a: jnp.pad(a, ((0, 0), (0, 128 - a.shape[1])))
    vec = dict(dn_a_log=lanes(dn_a_log), dn_dt_bias=lanes(dn_dt_bias), dn_norm_w=dn_norm_w, ffn_conv_b=ffn_conv_b,
               norm_pre_mix=norm_pre_mix, norm_post_mix=norm_post_mix, norm_pre_ffn=norm_pre_ffn, norm_post_ffn=norm_post_ffn)

    def layer_weights(l, gathered):
        sh = {k: pick(own[k][l], a[:, 0]) for k, a in zip(mats, gathered)}
        W = dict(w_in=jnp.pad(jnp.concatenate(sh["w_in"], axis=-1), ((0, 0), (0, PROJ_W - IN_COLS))),
                 w_out=jnp.concatenate(sh["w_out"], axis=0), ffn_w_in=jnp.concatenate(sh["ffn_w_in"], axis=-1),
                 ffn_w_out=jnp.concatenate(sh["ffn_w_out"], axis=0))
        W.update({k: a[l] for k, a in conv.items()})
        W.update({k: a[l:l + 1] for k, a in vec.items()})
        return W

    cos, sgn_sin = _rope_tables(x.shape[1])
    W0 = layer_weights(0, got0[:4])
    W0_first = dict(W0, norm_pre_mix=W0["norm_pre_mix"] + started[-1][0, 0])
    x1, saved0 = _layer_fwd(x[0], W0_first, cos, sgn_sin, 0)
    _, landed = _split_wait(started, len(mats), plan1, x1, "weights_gather_l1_wait")
    W1 = layer_weights(1, _pass_to_sibling(landed, halves, "weights_gather_l1_sibling"))
    x2, saved1 = _layer_fwd(x1, W1, cos, sgn_sin, 1)
    loss_local, dy = _loss_head(x2, loss_target[0], "loss_head")
    loss = lax.psum(loss_local, ("x", "y", "c"))

    def chip_partials(l, grads_l):
        gs = [_shard_major(name, grads_l[name]) for name in mats]
        from_sib = _exchange_halves(gs, f"grads_l{l}_to_sibling")
        return [_add_sibling(g, r, c_arr, tiles[name], f"add_sibling_{name}{l}") for g, r, name in zip(gs, from_sib, mats)]

    def owner_sums(l, parts, recvd):
        return [_add_chips(p, r, sc_arr, tiles[name], f"add_chips_{name}{l}") for p, r, name in zip(parts, recvd, mats)]

    dx1, grads1 = _layer_bwd(dy, saved1, W1, cos, sgn_sin, 1)
    part1 = chip_partials(1, grads1)
    sent = _split_copy(part1, [SDS((3,) + p.shape[1:], p.dtype) for p in part1], _scatter_plan, dx1, "grads_l1_scatter_start")
    W0_last = dict(W0, norm_post_ffn=W0["norm_post_ffn"] + sent[-1][0, 0])
    dx, grads0 = _layer_bwd(dx1, saved0, W0_last, cos, sgn_sin, 0)
    part1, recvd1 = _split_wait(sent, len(mats), _scatter_plan, dx, "grads_l1_scatter_wait")
    part0 = chip_partials(0, grads0)
    recvd0 = _scatter_partials(part0, "grads_l0_scatter")
    joined = _join_halves(owner_sums(0, part0, recvd0) + owner_sums(1, part1, recvd1), "grads_join_halves")
    g_out = {name: jnp.stack([joined[i], joined[len(mats) + i]]) for i, name in enumerate(mats)}
    grads = [grads0, grads1]

    small = {}
    for name in _SMALL:
        per_layer = [grads[l][name] for l in range(2)]
        if name in ("dn_a_log", "dn_dt_bias"):
            per_layer = [p[:, :N_HEADS_D] for p in per_layer]
        small[name] = jnp.stack(per_layer).reshape((2,) + (w[name].shape[1:] if name not in ("dn_conv_w", "ffn_conv_w")
                                                           else per_layer[0].shape))
    flat = jnp.concatenate([small[name].reshape(-1) for name in _SMALL])
    n_rows = -(-flat.shape[0] // 1024) * 8
    summed = _all_reduce_small(jnp.pad(flat, (0, n_rows * 128 - flat.shape[0])).reshape(n_rows, 128),
                               "small_grads_all_reduce").reshape(-1)
    off = 0
    for name in _SMALL:
        size = small[name].size
        g_out[name] = summed[off:off + size].reshape(small[name].shape)
        off += size
    g_out["dn_conv_w"] = lax.dynamic_slice_in_dim(g_out["dn_conv_w"], s_me * 384, 384, axis=2)
    g_out["ffn_conv_w"] = lax.dynamic_slice_in_dim(g_out["ffn_conv_w"], s_me * 1408, 1408, axis=2)

    deltas, new_m, new_v = {}, {}, {}
    for name in _WEIGHTS:
        shape = w[name].shape
        as3 = (lambda a: a) if len(shape) == 3 else (lambda a: a.reshape(shape[0], 1, shape[1]))
        tr = _ADAM_ROWS.get(name, as3(w[name]).shape[1])
        d_, m_, v_ = _adamw(as3(w[name]), as3(g_out[name]), as3(m[name]), as3(v[name]), tr, f"adamw_{name}")
        deltas[name], new_m[name], new_v[name] = d_.reshape(shape), m_.reshape(shape), v_.reshape(shape)

    return (loss, dx[None], *[g_out[k] for k in _WEIGHTS], *[deltas[k] for k in _WEIGHTS],
            *[new_m[k] for k in _WEIGHTS], *[new_v[k] for k in _WEIGHTS])
```

```python
import jax
import jax.numpy as jnp
from jax import lax
from jax.experimental import pallas as pl
from jax.experimental.pallas import tpu as pltpu

f32, bf16 = jnp.float32, jnp.bfloat16
SDS = jax.ShapeDtypeStruct
HI = lax.Precision.HIGH
MESH = pl.DeviceIdType.MESH
ANY = pl.BlockSpec(memory_space=pl.ANY)

D_MODEL = 1024
N_HEADS_A, HEAD_DIM = 8, 64
ATTN_W = 512
N_HEADS_D, DK = 4, 128
CHUNK = 64
D_FF = 2816
IN_COLS = 3592
PROJ_W = 3840
BRANCHES = ((1, 16), (4, 4), (16, 1))
EPS = 1e-6
NEG = -1e30
ROW_TILE = 256
VMEM_LIMIT = 56 * 1024 * 1024

ADAM_LR, ADAM_B1, ADAM_B2, ADAM_EPS, ADAM_WD, ADAM_STEP = 0.001, 0.9, 0.999, 1e-08, 0.01, 10


def _params(*sem):
    return pltpu.CompilerParams(dimension_semantics=sem, vmem_limit_bytes=VMEM_LIMIT)


def _mm(a, b, mode, tm, tn, out_dtype, name, column_shards=False):
    if mode == "nn":
        (M, K), N = a.shape, b.shape[1]
        dims = (((1,), (0,)), ((), ()))
        a_spec = pl.BlockSpec((tm, K), lambda i, j: (i, 0))
        b_spec = pl.BlockSpec((K, tn), lambda i, j: (0, j))
    elif mode == "nt":
        (M, K), N = a.shape, b.shape[0]
        dims = (((1,), (1,)), ((), ()))
        a_spec = pl.BlockSpec((tm, K), lambda i, j: (i, 0))
        b_spec = pl.BlockSpec((tn, K), lambda i, j: (j, 0))
    else:
        (K, M), N = a.shape, b.shape[1]
        dims = (((0,), (0,)), ((), ()))
        a_spec = pl.BlockSpec((K, tm), lambda i, j: (0, i))
        b_spec = pl.BlockSpec((K, tn), lambda i, j: (0, j))
    assert M % tm == 0 and N % tn == 0, (name, M, N, tm, tn)

    def body(a_ref, b_ref, o_ref):
        o_ref[...] = lax.dot_general(a_ref[...].astype(bf16), b_ref[...].astype(bf16), dims,
                                     preferred_element_type=f32).astype(o_ref.dtype)

    if column_shards:
        out_spec, out_shape = pl.BlockSpec((None, tm, tn), lambda i, j: (j, i, 0)), SDS((N // tn, M, tn), out_dtype)
    else:
        out_spec, out_shape = pl.BlockSpec((tm, tn), lambda i, j: (i, j)), SDS((M, N), out_dtype)
    return pl.pallas_call(body, grid=(M // tm, N // tn), in_specs=[a_spec, b_spec], out_specs=out_spec,
                          out_shape=out_shape, name=name, compiler_params=_params("parallel", "arbitrary"))(a, b)


def _row_spec(r, tm):
    if isinstance(r, tuple):
        arr, width, cb = r
        return arr, pl.BlockSpec((tm, width), lambda i, j, cb=cb: (i, cb + j))
    return r, pl.BlockSpec((tm, r.shape[1]), lambda i, j: (i, j))


def _full_spec(p):
    return pl.BlockSpec(p.shape, lambda i, j: (0,) * p.ndim)


def _rows(fn, rows, params, outs, name, tm=ROW_TILE, ncol=1):
    arrs, specs = zip(*[_row_spec(r, tm) for r in rows])
    S = arrs[0].shape[0]
    nr, npar = len(rows), len(params)

    def body(*refs):
        vals = fn(*[r[...].astype(f32) for r in refs[:nr]], *[p[...] for p in refs[nr:nr + npar]])
        for o_ref, v in zip(refs[nr + npar:], vals):
            o_ref[...] = v.astype(o_ref.dtype)

    return pl.pallas_call(
        body, grid=(S // tm, ncol), in_specs=list(specs) + [_full_spec(p) for p in params],
        out_specs=[pl.BlockSpec((tm, w), lambda i, j: (i, j)) for w, _ in outs],
        out_shape=[SDS((S, w * ncol), dt) for w, dt in outs], name=name,
        compiler_params=_params("parallel", "parallel"))(*arrs, *params)


def _rows_vjp(fn, rows, params, cts, wrt_rows, wrt_params, name, adds=None, tm=ROW_TILE, ncol=1, row_dtype=f32):
    adds = adds or {}
    arrs, specs = zip(*[_row_spec(r, tm) for r in rows])
    carrs, cspecs = zip(*[_row_spec(c, tm) for c in cts])
    add_keys = sorted(adds)
    aarrs = [adds[k] for k in add_keys]
    S = arrs[0].shape[0]
    nr, npar, nc, na = len(rows), len(params), len(cts), len(aarrs)
    widths = [specs[k].block_shape[1] for k in wrt_rows]

    def body(*refs):
        first = jnp.logical_and(pl.program_id(0) == 0, pl.program_id(1) == 0)
        rv = [r[...].astype(f32) for r in refs[:nr]]
        pv = [p[...] for p in refs[nr:nr + npar]]
        cv = tuple(c[...].astype(f32) for c in refs[nr + npar:nr + npar + nc])
        av = dict(zip(add_keys, refs[nr + npar + nc:nr + npar + nc + na]))
        o = refs[nr + npar + nc + na:]
        _, vjp = jax.vjp(fn, *rv, *pv)
        g = vjp(cv)
        for n, k in enumerate(wrt_rows):
            val = g[k]
            if k in av:
                val = val + av[k][...]
            o[n][...] = val.astype(o[n].dtype)
        for n, k in enumerate(wrt_params):
            ref = o[len(wrt_rows) + n]

            @pl.when(first)
            def _(ref=ref):
                ref[...] = jnp.zeros_like(ref)

            ref[...] += g[nr + k]

    res = pl.pallas_call(
        body, grid=(S // tm, ncol),
        in_specs=list(specs) + [_full_spec(p) for p in params] + list(cspecs)
        + [pl.BlockSpec((tm, a.shape[1] // ncol), lambda i, j: (i, j)) for a in aarrs],
        out_specs=[pl.BlockSpec((tm, w), lambda i, j: (i, j)) for w in widths] + [_full_spec(params[k]) for k in wrt_params],
        out_shape=[SDS((S, w * ncol), row_dtype) for w in widths] + [SDS(params[k].shape, f32) for k in wrt_params],
        name=name, compiler_params=_params("arbitrary", "arbitrary"))(*arrs, *params, *carrs, *aarrs)
    return res[:len(wrt_rows)], res[len(wrt_rows):]


def _rms(x, w):
    return x * lax.rsqrt(jnp.mean(x * x, axis=-1, keepdims=True) + EPS) * w


def _rms_fn(x, w):
    return (_rms(x, w),)


def _res_rms_fn(f, res, w):
    return (res + _rms(f, w),)


def _swap_halves(x):
    lane = lax.broadcasted_iota(jnp.int32, x.shape, 1)
    first = (lane % HEAD_DIM) < (HEAD_DIM // 2)
    n = x.shape[1]
    return jnp.where(first, pltpu.roll(x, n - HEAD_DIM // 2, 1), pltpu.roll(x, HEAD_DIM // 2, 1))


def _rope_fwd_fn(q, k, cos, sgn_sin):
    scale = HEAD_DIM ** -0.5
    return ((q * cos + _swap_halves(q) * sgn_sin) * scale, k * cos + _swap_halves(k) * sgn_sin)


def _rope_bwd_fn(dq, dk, cos, sgn_sin):
    dq = dq * (HEAD_DIM ** -0.5)
    return (dq * cos + _swap_halves(dq * sgn_sin), dk * cos + _swap_halves(dk * sgn_sin))


def _nt(a, b):
    return lax.dot_general(a, b, (((1,), (1,)), ((), ())), preferred_element_type=f32)


def _tn(a, b):
    return lax.dot_general(a, b, (((0,), (0,)), ((), ())), preferred_element_type=f32)


def _band_rows(j, d, nb):
    r, i = j // nb, j % nb
    if d == 1:
        cur = pl.ds(pl.multiple_of(i * 128, 128), 128)
        prev = pl.ds(pl.multiple_of(jnp.maximum(i - 1, 0) * 128, 128), 128)
    else:
        cur = pl.ds(i * (128 * d) + r, 128, stride=d)
        prev = pl.ds(jnp.maximum(i - 1, 0) * (128 * d) + r, 128, stride=d)
    a = lax.broadcasted_iota(jnp.int32, (128, 128), 0)
    c = lax.broadcasted_iota(jnp.int32, (128, 128), 1)
    return cur, prev, c <= a, jnp.logical_and(c >= a, i != 0)


def _attn_fwd(qr, kr, proj, name):
    S = qr.shape[0]
    nblk = S // 128

    def body(q_ref, k_ref, v_ref, out_ref, lse_ref, *scr):
        head_a = lax.broadcasted_iota(jnp.int32, (1, 128), 1) < HEAD_DIM
        for b, (d, nb) in enumerate(BRANCHES):
            ob_ref, lb_ref = scr[2 * b], scr[2 * b + 1]

            def blk(j, carry, d=d, nb=nb, ob_ref=ob_ref, lb_ref=lb_ref):
                cur, prev, mc, mp = _band_rows(j, d, nb)
                q = q_ref[cur, :]
                kc, kp = k_ref[cur, :].astype(bf16), k_ref[prev, :].astype(bf16)
                vc, vp = v_ref[cur, :].astype(bf16), v_ref[prev, :].astype(bf16)
                res = []
                for m in (head_a, jnp.logical_not(head_a)):
                    qm = jnp.where(m, q, 0.0).astype(bf16)
                    sc = jnp.where(mc, _nt(qm, kc), NEG)
                    sp = jnp.where(mp, _nt(qm, kp), NEG)
                    mx = jnp.maximum(jnp.max(sc, axis=1, keepdims=True), jnp.max(sp, axis=1, keepdims=True))
                    pc, pp = jnp.exp(sc - mx), jnp.exp(sp - mx)
                    l = jnp.sum(pc, axis=1, keepdims=True) + jnp.sum(pp, axis=1, keepdims=True)
                    o = (jnp.dot(pc.astype(bf16), vc, preferred_element_type=f32)
                         + jnp.dot(pp.astype(bf16), vp, preferred_element_type=f32)) / l
                    res.append((o, mx + jnp.log(l)))
                ob_ref[cur, :] = jnp.where(head_a, res[0][0], res[1][0])
                lb_ref[cur, :] = jnp.where(head_a, res[0][1], res[1][1])
                return carry

            lax.fori_loop(0, nblk, blk, 0)
        l0, l1, l2 = scr[1][...], scr[3][...], scr[5][...]
        mx = jnp.maximum(jnp.maximum(l0, l1), l2)
        e0, e1, e2 = jnp.exp(l0 - mx), jnp.exp(l1 - mx), jnp.exp(l2 - mx)
        den = e0 + e1 + e2
        out_ref[...] = ((e0 * scr[0][...] + e1 * scr[2][...] + e2 * scr[4][...]) / den).astype(out_ref.dtype)
        lse_ref[...] = mx + jnp.log(den)

    pair = pl.BlockSpec((S, 128), lambda h: (0, h))
    return pl.pallas_call(
        body, grid=(N_HEADS_A // 2,),
        in_specs=[pair, pair, pl.BlockSpec((S, 128), lambda h: (0, 2 * ATTN_W // 128 + h))], out_specs=[pair, pair],
        out_shape=[SDS((S, ATTN_W), bf16), SDS((S, ATTN_W), f32)], scratch_shapes=[pltpu.VMEM((S, 128), f32)] * 6,
        name=name, compiler_params=_params("parallel"))(qr, kr, proj)


def _attn_bwd(qr, kr, proj, dmix_in, out, lse, name):
    S = qr.shape[0]
    nblk = S // 128

    def body(q_ref, k_ref, v_ref, do_ref, out_ref, lse_ref, dq_ref, dk_ref, dv_ref, t_ref):
        head_a = lax.broadcasted_iota(jnp.int32, (1, 128), 1) < HEAD_DIM
        x = do_ref[...] * out_ref[...].astype(f32)
        t_ref[...] = jnp.where(head_a, jnp.sum(jnp.where(head_a, x, 0.0), axis=1, keepdims=True),
                               jnp.sum(jnp.where(head_a, 0.0, x), axis=1, keepdims=True))
        dq_ref[...] = jnp.zeros_like(dq_ref)
        dk_ref[...] = jnp.zeros_like(dk_ref)
        dv_ref[...] = jnp.zeros_like(dv_ref)
        for d, nb in BRANCHES:
            def blk(j, carry, d=d, nb=nb):
                cur, prev, mc, mp = _band_rows(j, d, nb)
                q, do = q_ref[cur, :], do_ref[cur, :]
                kc, kp = k_ref[cur, :].astype(bf16), k_ref[prev, :].astype(bf16)
                vc, vp = v_ref[cur, :].astype(bf16), v_ref[prev, :].astype(bf16)
                t, lse_b = t_ref[cur, :], lse_ref[cur, :]
                dq = dkc = dkp = dvc = dvp = jnp.zeros((128, 128), f32)
                for m, off in ((head_a, 0), (jnp.logical_not(head_a), HEAD_DIM)):
                    th, lh = t[:, off:off + 1], lse_b[:, off:off + 1]
                    qm = jnp.where(m, q, 0.0).astype(bf16)
                    dom = jnp.where(m, do, 0.0).astype(bf16)
                    pc = jnp.exp(jnp.where(mc, _nt(qm, kc), NEG) - lh)
                    pp = jnp.exp(jnp.where(mp, _nt(qm, kp), NEG) - lh)
                    dsc = (pc * (_nt(dom, vc) - th)).astype(bf16)
                    dsp = (pp * (_nt(dom, vp) - th)).astype(bf16)
                    dq = dq + jnp.where(m, jnp.dot(dsc, kc, preferred_element_type=f32)
                                        + jnp.dot(dsp, kp, preferred_element_type=f32), 0.0)
                    dvc, dvp = dvc + _tn(pc.astype(bf16), dom), dvp + _tn(pp.astype(bf16), dom)
                    dkc, dkp = dkc + _tn(dsc, qm), dkp + _tn(dsp, qm)
                dq_ref[cur, :] += dq
                dk_ref[cur, :] += dkc
                dv_ref[cur, :] += dvc
                dk_ref[prev, :] += dkp
                dv_ref[prev, :] += dvp
                return carry

            lax.fori_loop(0, nblk, blk, 0)

    pair = pl.BlockSpec((S, 128), lambda h: (0, h))
    return pl.pallas_call(
        body, grid=(N_HEADS_A // 2,),
        in_specs=[pair, pair, pl.BlockSpec((S, 128), lambda h: (0, 2 * ATTN_W // 128 + h)), pair, pair, pair],
        out_specs=[pair] * 3, out_shape=[SDS((S, ATTN_W), f32)] * 3, scratch_shapes=[pltpu.VMEM((S, 128), f32)],
        name=name, compiler_params=_params("parallel"))(qr, kr, proj, dmix_in, out, lse)


def _conv_val(x, w, K, rows):
    acc = x * w[K - 1:K, :]
    for s in range(1, K):
        acc = acc + jnp.where(rows >= s, pltpu.roll(x, s, 0), 0.0) * w[K - 1 - s:K - s, :]
    return acc


def _colconv_fwd(xs, ws, bs, K, fn, nblk, tc, outs, name):
    S = xs[0][0].shape[0]
    n = len(xs)
    has_b = bs is not None

    def body(*refs):
        rows = lax.broadcasted_iota(jnp.int32, (S, tc), 0)
        cs = []
        for k in range(n):
            c = _conv_val(refs[k][...].astype(f32), refs[n + k][...], K, rows)
            if has_b:
                c = c + refs[2 * n + k][...]
            cs.append(c)
        for o_ref, val in zip(refs[(3 if has_b else 2) * n:], fn(*cs)):
            o_ref[...] = val.astype(o_ref.dtype)

    def cspec(rows_, cb0):
        return pl.BlockSpec((rows_, tc), lambda j, cb0=cb0: (0, cb0 + j))

    in_specs = [cspec(S, cb) for _, cb in xs] + [cspec(K, cb) for _, cb in ws]
    args = [a for a, _ in xs] + [a for a, _ in ws]
    if has_b:
        in_specs += [cspec(1, cb) for _, cb in bs]
        args += [a for a, _ in bs]
    return pl.pallas_call(
        body, grid=(nblk,), in_specs=in_specs, out_specs=[cspec(S, 0) for _ in outs],
        out_shape=[SDS((S, nblk * tc), dt) for dt in outs], name=name, compiler_params=_params("parallel"))(*args)


def _colconv_bwd(xs, ws, bs, K, fn, douts, nblk, tc, name, dx_dtype=f32):
    S = xs[0][0].shape[0]
    n, nd = len(xs), len(douts)
    has_b = bs is not None
    nin = (3 if has_b else 2) * n

    def body(*refs):
        rows = lax.broadcasted_iota(jnp.int32, (S, tc), 0)
        x = [refs[k][...].astype(f32) for k in range(n)]
        w = [refs[n + k][...] for k in range(n)]
        cs = []
        for k in range(n):
            c = _conv_val(x[k], w[k], K, rows)
            if has_b:
                c = c + refs[2 * n + k][...]
            cs.append(c)
        _, vjp = jax.vjp(fn, *cs)
        dcs = vjp(tuple(r[...].astype(f32) for r in refs[nin:nin + nd]))
        o = refs[nin + nd:]
        for k in range(n):
            dc = dcs[k]
            dx = dc * w[k][K - 1:K, :]
            o[n + k][K - 1:K, :] = jnp.sum(dc * x[k], axis=0, keepdims=True)
            for s in range(1, K):
                dx = dx + jnp.where(rows < S - s, pltpu.roll(dc, S - s, 0), 0.0) * w[k][K - 1 - s:K - s, :]
                xsh = jnp.where(rows >= s, pltpu.roll(x[k], s, 0), 0.0)
                o[n + k][K - 1 - s:K - s, :] = jnp.sum(dc * xsh, axis=0, keepdims=True)
            o[k][...] = dx.astype(o[k].dtype)
            if has_b:
                o[2 * n + k][...] = jnp.sum(dc, axis=0, keepdims=True)

    def cspec(rows_, cb0):
        return pl.BlockSpec((rows_, tc), lambda j, cb0=cb0: (0, cb0 + j))

    in_specs = [cspec(S, cb) for _, cb in xs] + [cspec(K, cb) for _, cb in ws]
    args = [a for a, _ in xs] + [a for a, _ in ws]
    if has_b:
        in_specs += [cspec(1, cb) for _, cb in bs]
        args += [a for a, _ in bs]
    in_specs += [cspec(S, 0) for _ in douts]
    args += list(douts)
    W = nblk * tc
    out_specs = [cspec(S, 0)] * n + [cspec(K, 0)] * n + ([cspec(1, 0)] * n if has_b else [])
    out_shape = [SDS((S, W), dx_dtype)] * n + [SDS((K, W), f32)] * n + ([SDS((1, W), f32)] * n if has_b else [])
    res = pl.pallas_call(body, grid=(nblk,), in_specs=in_specs, out_specs=out_specs, out_shape=out_shape,
                         name=name, compiler_params=_params("parallel"))(*args)
    return res[:n], res[n:2 * n], res[2 * n:]


def _silu_fn(c):
    return (c * jax.nn.sigmoid(c),)


def _geglu_fn(gate, up):
    gelu = 0.5 * gate * (1.0 + jnp.tanh(0.7978845608028654 * (gate + 0.044715 * gate * gate * gate)))
    return (gelu * up,)


def _softplus(x):
    u = jnp.exp(jnp.minimum(x, 20.0))
    small = u * (1.0 - 0.5 * u)
    return jnp.where(x > 20.0, x, jnp.where(u < 1e-4, small, jnp.log(1.0 + u)))


def _bmm(a, b, precision=None):
    return lax.dot_general(a, b, (((2,), (1,)), ((0,), (0,))), precision=precision, preferred_element_type=f32)


def _bnt(a, b):
    return lax.dot_general(a, b, (((2,), (2,)), ((0,), (0,))), preferred_element_type=f32)


def _btn(a, b):
    return lax.dot_general(a, b, (((1,), (1,)), ((0,), (0,))), preferred_element_type=f32)


def _unit_lower_inverse(A):
    n = A.shape[-1]
    eye = (lax.broadcasted_iota(jnp.int32, (1, n, n), 1) == lax.broadcasted_iota(jnp.int32, (1, n, n), 2)).astype(f32)
    P = -A
    T = eye + P
    for _ in range(5):
        P = _bmm(P, P, HI)
        T = T + _bmm(T, P, HI)
    return T


def _dn_prep_fn(q, k, v, ba, alog, dtb, h):
    G, C = q.shape[0], CHUNK
    lane = lax.broadcasted_iota(jnp.int32, (1, 1, 128), 2)

    def sel(arr, idx):
        return jnp.sum(jnp.where(lane == idx, arr, 0.0), axis=-1, keepdims=True)

    beta = jax.nn.sigmoid(sel(ba, h))
    g = -jnp.exp(sel(alog[None], h)) * _softplus(sel(ba, N_HEADS_D + h) + sel(dtb[None], h))
    qn = q * lax.rsqrt(jnp.sum(q * q, axis=-1, keepdims=True) + EPS) * (DK ** -0.5)
    kn = k * lax.rsqrt(jnp.sum(k * k, axis=-1, keepdims=True) + EPS)
    ii = lax.broadcasted_iota(jnp.int32, (1, C, C), 1)
    jj = lax.broadcasted_iota(jnp.int32, (1, C, C), 2)
    tril, strict = ii >= jj, ii > jj
    gsq = jnp.broadcast_to(g, (G, C, C))
    gcol = _bmm(jnp.broadcast_to(tril.astype(f32), (G, C, C)), gsq, HI)
    grow = _bmm(jnp.ones((G, C, C), f32), jnp.where(ii <= jj, gsq, 0.0), HI)
    decay = jnp.exp(jnp.where(tril, gcol - grow, NEG))
    gc = gcol[:, :, :1]
    glast = gcol[:, C - 1:C, :1]
    kb = kn * beta
    A = jnp.where(strict, _bnt(kb.astype(bf16), kn.astype(bf16)) * decay, 0.0)
    T = _unit_lower_inverse(A).astype(bf16)
    u = _bmm(T, (v * beta).astype(bf16))
    w = _bmm(T, (kb * jnp.exp(gc)).astype(bf16))
    qk = _bnt(qn.astype(bf16), kn.astype(bf16)) * decay
    qd = qn * jnp.exp(gc)
    kd = kn * jnp.exp(glast - gc)
    return u, w, qk, qd, kd, jnp.broadcast_to(jnp.exp(glast), (G, C, DK))


def _dn_scan_fn(u, w, qk, qd, kd, eg, St):
    b = lambda a: a.astype(bf16)
    vnew = u - _bmm(b(w), b(St))
    o = _bmm(b(qd), b(St)) + _bmm(b(qk), b(vnew))
    return o, St * eg[:, :1, :] + _btn(b(kd), b(vnew))


def _dn_post_fn(o, z, nw):
    return (_rms(o, nw) * (z * jax.nn.sigmoid(z)),)


DN_GROUP = 8


def _dn_prep_specs(S, rows):
    def col(first):
        return pl.BlockSpec((rows, DK), lambda i, h, first=first: (i, first // DK + h))

    par = pl.BlockSpec((1, 128), lambda i, h: (0, 0))
    return [col(0), col(N_HEADS_D * DK), col(2 * N_HEADS_D * DK),
            pl.BlockSpec((rows, 128), lambda i, h: (i, 3584 // 128)), par, par]


def _dn_prep(qkv, proj, alog, dtb, name):
    S = qkv.shape[0]
    G = DN_GROUP
    rows = G * CHUNK

    def body(q_ref, k_ref, v_ref, ba_ref, al_ref, dt_ref, u_ref, w_ref, qk_ref, qd_ref, kd_ref, eg_ref):
        h = pl.program_id(1)
        r3 = lambda ref: ref[...].reshape(G, CHUNK, 128)
        u, w, qk, qd, kd, eg = _dn_prep_fn(r3(q_ref), r3(k_ref), r3(v_ref), r3(ba_ref), al_ref[...], dt_ref[...], h)
        for ref, val in ((u_ref, u), (w_ref, w), (qd_ref, qd), (kd_ref, kd), (eg_ref, eg)):
            ref[...] = val.reshape(rows, DK)
        qk_ref[:, :CHUNK] = qk.reshape(rows, CHUNK)
        qk_ref[:, CHUNK:] = jnp.zeros((rows, DK - CHUNK), f32)

    out = pl.BlockSpec((rows, DK), lambda i, h: (i, h))
    return pl.pallas_call(
        body, grid=(S // rows, N_HEADS_D), in_specs=_dn_prep_specs(S, rows), out_specs=[out] * 6,
        out_shape=[SDS((S, N_HEADS_D * DK), f32)] * 6, name=name,
        compiler_params=_params("parallel", "parallel"))(qkv, qkv, qkv, proj, alog, dtb)


def _dn_prep_bwd(qkv, proj, alog, dtb, cts, name):
    S = qkv.shape[0]
    G = DN_GROUP
    rows = G * CHUNK

    def body(q_ref, k_ref, v_ref, ba_ref, al_ref, dt_ref, du_ref, dw_ref, dqk_ref, dqd_ref, dkd_ref, deg_ref,
             dq_ref, dk_ref, dv_ref, dba_ref, dal_ref, ddt_ref):
        i, h = pl.program_id(0), pl.program_id(1)
        r3 = lambda ref: ref[...].reshape(G, CHUNK, 128)
        _, vjp = jax.vjp(lambda q, k, v, ba, al, dt: _dn_prep_fn(q, k, v, ba, al, dt, h),
                         r3(q_ref), r3(k_ref), r3(v_ref), r3(ba_ref), al_ref[...], dt_ref[...])
        dqk = dqk_ref[:, :CHUNK].reshape(G, CHUNK, CHUNK)
        dq, dk, dv, dba, dal, ddt = vjp((r3(du_ref), r3(dw_ref), dqk, r3(dqd_ref), r3(dkd_ref), r3(deg_ref)))
        dq_ref[...] = dq.reshape(rows, DK)
        dk_ref[...] = dk.reshape(rows, DK)
        dv_ref[...] = dv.reshape(rows, DK)

        @pl.when(h == 0)
        def _():
            dba_ref[...] = jnp.zeros_like(dba_ref)

        @pl.when(jnp.logical_and(i == 0, h == 0))
        def _():
            dal_ref[...] = jnp.zeros_like(dal_ref)
            ddt_ref[...] = jnp.zeros_like(ddt_ref)

        dba_ref[...] += dba.reshape(rows, 128)
        dal_ref[...] += dal
        ddt_ref[...] += ddt

    hcol = pl.BlockSpec((rows, DK), lambda i, h: (i, h))
    par = pl.BlockSpec((1, 128), lambda i, h: (0, 0))
    W = N_HEADS_D * DK
    return pl.pallas_call(
        body, grid=(S // rows, N_HEADS_D), in_specs=_dn_prep_specs(S, rows) + [hcol] * 6,
        out_specs=[hcol] * 3 + [pl.BlockSpec((rows, 128), lambda i, h: (i, 0)), par, par],
        out_shape=[SDS((S, W), f32)] * 3 + [SDS((S, 128), f32), SDS((1, 128), f32), SDS((1, 128), f32)], name=name,
        compiler_params=_params("arbitrary", "arbitrary"))(qkv, qkv, qkv, proj, alog, dtb, *cts)


def _heads(x):
    return jnp.stack([x[:, DK * h:DK * (h + 1)] for h in range(N_HEADS_D)])


def _dn_scan(pre, name):
    S = pre[0].shape[0]
    NCH = S // CHUNK

    def body(u_ref, w_ref, qk_ref, qd_ref, kd_ref, eg_ref, o_ref, st_ref, s_ref):
        @pl.when(pl.program_id(0) == 0)
        def _():
            s_ref[...] = jnp.zeros_like(s_ref)

        St = s_ref[...]
        st_ref[0] = St
        o, Sn = _dn_scan_fn(_heads(u_ref[...]), _heads(w_ref[...]), _heads(qk_ref[...])[:, :, :CHUNK], _heads(qd_ref[...]),
                            _heads(kd_ref[...]), _heads(eg_ref[...]), St)
        for h in range(N_HEADS_D):
            o_ref[:, DK * h:DK * (h + 1)] = o[h]
        s_ref[...] = Sn

    blk = pl.BlockSpec((CHUNK, N_HEADS_D * DK), lambda n: (n, 0))
    return pl.pallas_call(
        body, grid=(NCH,), in_specs=[blk] * 6,
        out_specs=[blk, pl.BlockSpec((1, N_HEADS_D, DK, DK), lambda n: (n, 0, 0, 0))],
        out_shape=[SDS((S, N_HEADS_D * DK), f32), SDS((NCH, N_HEADS_D, DK, DK), f32)],
        scratch_shapes=[pltpu.VMEM((N_HEADS_D, DK, DK), f32)], name=name, compiler_params=_params("arbitrary"))(*pre)


def _dn_scan_bwd(pre, states, do, name):
    S = do.shape[0]
    NCH = S // CHUNK

    def body(u_ref, w_ref, qk_ref, qd_ref, kd_ref, eg_ref, st_ref, do_ref,
             du_ref, dw_ref, dqk_ref, dqd_ref, dkd_ref, deg_ref, ds_ref):
        @pl.when(pl.program_id(0) == 0)
        def _():
            ds_ref[...] = jnp.zeros_like(ds_ref)

        _, vjp = jax.vjp(_dn_scan_fn, _heads(u_ref[...]), _heads(w_ref[...]), _heads(qk_ref[...])[:, :, :CHUNK],
                         _heads(qd_ref[...]), _heads(kd_ref[...]), _heads(eg_ref[...]), st_ref[0])
        du, dw, dqk, dqd, dkd, deg, dS = vjp((_heads(do_ref[...]), ds_ref[...]))
        ds_ref[...] = dS
        for h in range(N_HEADS_D):
            c = slice(DK * h, DK * (h + 1))
            for ref, val in ((du_ref, du), (dw_ref, dw), (dqd_ref, dqd), (dkd_ref, dkd), (deg_ref, deg)):
                ref[:, c] = val[h]
            dqk_ref[:, DK * h:DK * h + CHUNK] = dqk[h]
            dqk_ref[:, DK * h + CHUNK:DK * (h + 1)] = jnp.zeros((CHUNK, DK - CHUNK), f32)

    blk = pl.BlockSpec((CHUNK, N_HEADS_D * DK), lambda n: (NCH - 1 - n, 0))
    return pl.pallas_call(
        body, grid=(NCH,),
        in_specs=[blk] * 6 + [pl.BlockSpec((1, N_HEADS_D, DK, DK), lambda n: (NCH - 1 - n, 0, 0, 0)), blk],
        out_specs=[blk] * 6, out_shape=[SDS((S, N_HEADS_D * DK), f32)] * 6,
        scratch_shapes=[pltpu.VMEM((N_HEADS_D, DK, DK), f32)], name=name,
        compiler_params=_params("arbitrary"))(*pre, states, do)


def _loss_head(y, t, name):
    S, D = y.shape
    tm = ROW_TILE

    def body(y_ref, t_ref, dy_ref, l_ref):
        i = pl.program_id(0)
        d = y_ref[...] - t_ref[...]
        dy_ref[...] = d * (1.0 / D)
        part = jnp.sum(jnp.sum(d * d, axis=1, keepdims=True), axis=0, keepdims=True) * (0.5 / D)

        @pl.when(i == 0)
        def _():
            l_ref[...] = jnp.zeros_like(l_ref)

        l_ref[...] += jnp.broadcast_to(part, l_ref.shape)

    spec = pl.BlockSpec((tm, D), lambda i: (i, 0))
    dy, l = pl.pallas_call(body, grid=(S // tm,), in_specs=[spec, spec],
                           out_specs=[spec, pl.BlockSpec((1, 128), lambda i: (0, 0))],
                           out_shape=[SDS((S, D), f32), SDS((1, 128), f32)], name=name,
                           compiler_params=_params("arbitrary"))(y, t)
    return l[0, 0], dy


def _adamw(w, g, m, v, tr, name):
    L, R, C = w.shape
    assert R % tr == 0

    def body(w_ref, g_ref, m_ref, v_ref, d_ref, mo_ref, vo_ref):
        gv = g_ref[...]
        m2 = ADAM_B1 * m_ref[...] + (1.0 - ADAM_B1) * gv
        v2 = ADAM_B2 * v_ref[...] + (1.0 - ADAM_B2) * (gv * gv)
        m_hat = m2 / (1.0 - ADAM_B1 ** ADAM_STEP)
        v_hat = v2 / (1.0 - ADAM_B2 ** ADAM_STEP)
        d_ref[...] = -ADAM_LR * (m_hat / (jnp.sqrt(v_hat) + ADAM_EPS) + ADAM_WD * w_ref[...])
        mo_ref[...] = m2
        vo_ref[...] = v2

    spec = pl.BlockSpec((1, tr, C), lambda l, i: (l, i, 0))
    return pl.pallas_call(body, grid=(L, R // tr), in_specs=[spec] * 4, out_specs=[spec] * 3,
                          out_shape=[SDS((L, R, C), f32)] * 3, name=name,
                          compiler_params=_params("parallel", "parallel"))(w, g, m, v)


def _rope_tables(S):
    inv = 1.0 / (10000.0 ** (jnp.arange(0, HEAD_DIM, 2, dtype=f32) / HEAD_DIM))
    ang = jnp.arange(S, dtype=f32)[:, None] * inv[None, :]
    cos, sin = jnp.cos(ang), jnp.sin(ang)
    return (jnp.tile(jnp.concatenate([cos, cos], axis=1), (1, N_HEADS_A)),
            jnp.tile(jnp.concatenate([-sin, sin], axis=1), (1, N_HEADS_A)))


def _layer_fwd(x, W, cos, sgn_sin, l, late_weights=None):
    n = f"l{l}_"
    (h1,) = _rows(_rms_fn, [x], [W["norm_pre_mix"]], [(D_MODEL, bf16)], n + "pre_mix_norm")
    proj = _mm(h1, W["w_in"], "nn", 512, 768, f32, n + "in_proj")
    qr, kr = _rows(_rope_fwd_fn, [(proj, ATTN_W, 0), (proj, ATTN_W, 1), cos, sgn_sin], [],
                   [(ATTN_W, f32), (ATTN_W, f32)], n + "rope")
    attn_out, lse = _attn_fwd(qr, kr, proj, n + "attn_fwd")
    (qkv,) = _colconv_fwd([(proj, 3)], [(W["dn_conv_w"], 0)], None, 4, _silu_fn, 3, 512, [f32], n + "dn_conv")
    dn_pre = _dn_prep(qkv, proj, W["dn_a_log"], W["dn_dt_bias"], n + "dn_prep")
    dn_o, dn_states = _dn_scan(dn_pre, n + "dn_scan")
    (dn_out,) = _rows(_dn_post_fn, [(dn_o, DK, 0), (proj, DK, 3072 // DK)], [W["dn_norm_w"]], [(DK, bf16)], n + "dn_post",
                      ncol=N_HEADS_D)
    mix_in = jnp.concatenate([attn_out, dn_out], axis=1)
    late = late_weights(mix_in) if late_weights is not None else {}
    W = {**W, **late}
    mix = _mm(mix_in, W["w_out"], "nn", 512, 512, f32, n + "out_proj")
    (x1,) = _rows(_res_rms_fn, [mix, x], [W["norm_post_mix"]], [(D_MODEL, f32)], n + "post_mix_norm")
    (h2,) = _rows(_rms_fn, [x1], [W["norm_pre_ffn"]], [(D_MODEL, bf16)], n + "pre_ffn_norm")
    u0 = _mm(h2, W["ffn_w_in"], "nn", 1024, 512, bf16, n + "ffn_in")
    nb_ff = D_FF // 256
    (act,) = _colconv_fwd([(u0, 0), (u0, nb_ff)], [(W["ffn_conv_w"], 0), (W["ffn_conv_w"], nb_ff)],
                          [(W["ffn_conv_b"], 0), (W["ffn_conv_b"], nb_ff)], 3, _geglu_fn, nb_ff, 256, [bf16],
                          n + "ffn_conv_glu")
    f = _mm(act, W["ffn_w_out"], "nn", 512, 512, f32, n + "ffn_out")
    (x2,) = _rows(_res_rms_fn, [f, x1], [W["norm_post_ffn"]], [(D_MODEL, f32)], n + "post_ffn_norm")
    saved = dict(x=x, h1=h1, proj=proj, qr=qr, kr=kr, attn_out=attn_out, lse=lse, qkv=qkv, dn_pre=dn_pre, dn_o=dn_o,
                 dn_states=dn_states, mix_in=mix_in, mix=mix, x1=x1, h2=h2, u0=u0, act=act, f=f, late=late)
    return x2, saved


def _layer_bwd(dx2, sv, W, cos, sgn_sin, l, after_ffn=None):
    n = f"l{l}_"
    S = dx2.shape[0]
    g = {}
    (df,), (g["norm_post_ffn"],) = _rows_vjp(_rms_fn, [sv["f"]], [W["norm_post_ffn"]], [dx2], [0], [0],
                                             n + "post_ffn_norm_bwd", row_dtype=bf16)
    dact = _mm(df, W["ffn_w_out"], "nt", 512, 1408, f32, n + "ffn_out_dx")
    g["ffn_w_out"] = _mm(sv["act"], df, "tn", 256, 1024, f32, n + "ffn_out_dw")
    nb_ff = D_FF // 256
    u0 = sv["u0"]
    dxs, dws, dbs = _colconv_bwd([(u0, 0), (u0, nb_ff)], [(W["ffn_conv_w"], 0), (W["ffn_conv_w"], nb_ff)],
                                 [(W["ffn_conv_b"], 0), (W["ffn_conv_b"], nb_ff)], 3, _geglu_fn, [dact], nb_ff, 256,
                                 n + "ffn_conv_glu_bwd", dx_dtype=bf16)
    du0 = jnp.concatenate(dxs, axis=1)
    g["ffn_conv_w"] = jnp.concatenate(dws, axis=1)
    g["ffn_conv_b"] = jnp.concatenate(dbs, axis=1)
    dh2 = _mm(du0, W["ffn_w_in"], "nt", 512, 512, f32, n + "ffn_in_dx")
    g["ffn_w_in"] = _mm(sv["h2"], du0, "tn", 512, D_FF // 2, f32, n + "ffn_in_dw", column_shards=True)
    (dx1,), (g["norm_pre_ffn"],) = _rows_vjp(_rms_fn, [sv["x1"]], [W["norm_pre_ffn"]], [dh2], [0], [0],
                                             n + "pre_ffn_norm_bwd", adds={0: dx2})
    if after_ffn is not None:
        W = dict(W, norm_post_mix=W["norm_post_mix"] + after_ffn(g, dx1))
    (dmix,), (g["norm_post_mix"],) = _rows_vjp(_rms_fn, [sv["mix"]], [W["norm_post_mix"]], [dx1], [0], [0],
                                               n + "post_mix_norm_bwd", row_dtype=bf16)
    dmix_in = _mm(dmix, W["w_out"], "nt", 512, 512, f32, n + "out_proj_dx")
    g["w_out"] = _mm(sv["mix_in"], dmix, "tn", 512, 512, f32, n + "out_proj_dw")

    (ddn_o, dz), (g["dn_norm_w"],) = _rows_vjp(
        _dn_post_fn, [(sv["dn_o"], DK, 0), (sv["proj"], DK, 3072 // DK)], [W["dn_norm_w"]], [(dmix_in, DK, ATTN_W // DK)],
        [0, 1], [0], n + "dn_post_bwd", ncol=N_HEADS_D)
    dpre = _dn_scan_bwd(sv["dn_pre"], sv["dn_states"], ddn_o, n + "dn_scan_bwd")
    dq, dk, dv, dba, g["dn_a_log"], g["dn_dt_bias"] = _dn_prep_bwd(
        sv["qkv"], sv["proj"], W["dn_a_log"], W["dn_dt_bias"], dpre, n + "dn_prep_bwd")
    dqkv = jnp.concatenate([dq, dk, dv], axis=1)
    (dqkv0,), (g["dn_conv_w"],), _ = _colconv_bwd([(sv["proj"], 3)], [(W["dn_conv_w"], 0)], None, 4, _silu_fn,
                                                 [dqkv], 3, 512, n + "dn_conv_bwd")

    dqr, dkr, dav = _attn_bwd(sv["qr"], sv["kr"], sv["proj"], dmix_in, sv["attn_out"], sv["lse"], n + "attn_bwd")
    daq, dak = _rows(_rope_bwd_fn, [dqr, dkr, cos, sgn_sin], [], [(ATTN_W, f32)] * 2, n + "rope_bwd")
    dproj = jnp.concatenate([daq, dak, dav, dqkv0, dz, dba, jnp.zeros((S, PROJ_W - 3712), f32)], axis=1).astype(bf16)
    dh1 = _mm(dproj, W["w_in"], "nt", 512, 512, f32, n + "in_proj_dx")
    g["w_in"] = _mm(sv["h1"], dproj, "tn", 512, 768, f32, n + "in_proj_dw")
    (dx,), (g["norm_pre_mix"],) = _rows_vjp(_rms_fn, [sv["x"]], [W["norm_pre_mix"]], [dh1], [0], [0],
                                            n + "pre_mix_norm_bwd", adds={0: dx1})
    return dx, g


def _local_step(x, target, layers):
    cos, sgn_sin = _rope_tables(x.shape[0])
    saved = []
    for l, W in enumerate(layers):
        x, sv = _layer_fwd(x, W, cos, sgn_sin, l)
        saved.append(sv)
    loss, dx = _loss_head(x, target, "loss_head")
    grads = [None] * len(layers)
    for l in reversed(range(len(layers))):
        dx, grads[l] = _layer_bwd(dx, saved[l], layers[l], cos, sgn_sin, l)
    return loss, dx, grads


def _pos():
    x, y, c = lax.axis_index("x"), lax.axis_index("y"), lax.axis_index("c")
    return x, y, c, [(1 - x, y), (x, 1 - y), (1 - x, 1 - y)]


def _rcopy(src, dst, send_sem, recv_sem, dev):
    return pltpu.make_async_remote_copy(src_ref=src, dst_ref=dst, send_sem=send_sem, recv_sem=recv_sem,
                                        device_id=dev, device_id_type=MESH)


def _half_rows(ref, h, which, axis):
    if h is None:
        return ref
    rows = pl.ds(pl.multiple_of(which * h, 16), h)
    return ref.at[:, rows, :] if axis == 1 else ref.at[rows, :]


def _dma_sems(*counts):
    return [pltpu.SemaphoreType.DMA((k,)) for k in counts]


def _all_gather(arrs, halves, name):
    n = len(arrs)

    def body(*refs):
        ins, outs = refs[:n], refs[n:2 * n]
        send1, recv1, send2, recv2 = refs[2 * n:]
        x, y, c, chips = _pos()
        me, sib, s_me = (x, y, c), (x, y, 1 - c), 2 * x + y
        sends = []
        for i in range(n):
            for j, chip in enumerate(chips):
                cp = _rcopy(_half_rows(ins[i], halves[i], c, 1), _half_rows(outs[i].at[s_me], halves[i], c, 1),
                            send1.at[3 * i + j], recv1.at[3 * i + j], (*chip, c))
                cp.start()
                sends.append(cp)
        for i in range(n):
            for j, (px, py) in enumerate(chips):
                k = 3 * i + j
                landed = _half_rows(outs[i].at[2 * px + py], halves[i], c, 1)
                _rcopy(landed, landed, send1.at[k], recv1.at[k], me).wait_recv()
                if halves[i] is not None:
                    cp = _rcopy(landed, landed, send2.at[k], recv2.at[k], sib)
                    cp.start()
                    sends.append(cp)
        for i in range(n):
            if halves[i] is None:
                continue
            for j, (px, py) in enumerate(chips):
                k = 3 * i + j
                other = _half_rows(outs[i].at[2 * px + py], halves[i], 1 - c, 1)
                _rcopy(other, other, send2.at[k], recv2.at[k], me).wait_recv()
        for cp in sends:
            cp.wait_send()

    return pl.pallas_call(
        body, in_specs=[ANY] * n, out_specs=[ANY] * n,
        out_shape=[SDS((4,) + a.shape, a.dtype) for a in arrs],
        scratch_shapes=_dma_sems(3 * n, 3 * n, 3 * n, 3 * n), name=name)(*arrs)


HBM = pl.BlockSpec(memory_space=pltpu.HBM)
SEM = pl.BlockSpec(memory_space=pltpu.SEMAPHORE)
_EFFECT = pltpu.SideEffectType.DATAFLOW_SIDE_EFFECTING


def _in_hbm(a):
    return pltpu.with_memory_space_constraint(a, pltpu.HBM)


def _split_copy(srcs, land_shapes, plan, per, after, name):
    n = len(srcs)
    k = per * n

    def body(*refs):
        ins, lands, token = refs[:n], refs[n:2 * n], refs[-1]
        send, recv = refs[2 * n + 1], refs[2 * n + 2]
        for i, (src, dst, dev, _) in enumerate(plan(ins, lands)):
            _rcopy(src, dst, send.at[i], recv.at[i], dev).start()
        token[...] = jnp.zeros_like(token)

    lands = [_in_hbm(lax.empty(s.shape, s.dtype)) for s in land_shapes]
    return pl.pallas_call(
        body, name=name,
        out_shape=(pltpu.SemaphoreType.DMA((k,)), pltpu.SemaphoreType.DMA((k,)),
                   *[pltpu.HBM(a.shape, a.dtype) for a in srcs], *[pltpu.HBM(s.shape, s.dtype) for s in land_shapes],
                   SDS((8, 128), f32)),
        in_specs=[HBM] * (2 * n) + [ANY], out_specs=(SEM, SEM, *[HBM] * (2 * n), pl.BlockSpec(memory_space=pltpu.VMEM)),
        input_output_aliases={i: 2 + i for i in range(2 * n)},
        compiler_params=pltpu.CompilerParams(has_side_effects=_EFFECT))(*[_in_hbm(a) for a in srcs], *lands, after)


def _split_wait(started, n, plan, after, name):
    send, recv = started[0], started[1]
    thru = started[2:2 + 2 * n]

    def body(*refs):
        ins, lands = refs[:n], refs[n:2 * n]
        send_ref, recv_ref = refs[2 * n], refs[2 * n + 1]
        for i, (src, _, dev, mine) in enumerate(plan(ins, lands)):
            cp = _rcopy(src, mine, send_ref.at[i], recv_ref.at[i], dev)
            cp.wait_send()
            cp.wait_recv()

    res = pl.pallas_call(
        body, name=name, out_shape=tuple(pltpu.HBM(a.shape, a.dtype) for a in thru),
        in_specs=[HBM] * (2 * n) + [SEM, SEM, ANY], out_specs=tuple([HBM] * (2 * n)),
        input_output_aliases={i: i for i in range(2 * n)},
        compiler_params=pltpu.CompilerParams(has_side_effects=_EFFECT))(*thru, send, recv, after)
    return res[:n], res[n:]


def _gather_plan(halves):
    def plan(ins, lands):
        x, y, c, chips = _pos()
        out = []
        for i in range(len(ins)):
            for px, py in chips:
                out.append((_half_rows(ins[i], halves[i], c, 1), _half_rows(lands[i].at[2 * x + y], halves[i], c, 1),
                            (px, py, c), _half_rows(lands[i].at[2 * px + py], halves[i], c, 1)))
        return out
    return plan


def _scatter_plan(ins, lands):
    x, y, c, chips = _pos()
    out = []
    for i in range(len(ins)):
        for j, (px, py) in enumerate(chips):
            out.append((ins[i].at[2 * px + py], lands[i].at[j], (px, py, c), lands[i].at[j]))
    return out


def _exchange_plan(ins, lands):
    x, y, c, _ = _pos()
    return [(_half_rows(g, g.shape[1] // 2, 1 - c, 1), land, (x, y, 1 - c), land) for g, land in zip(ins, lands)]


def _pass_to_sibling(lands, halves, name):
    n = len(lands)

    def body(*refs):
        outs = refs[n:2 * n]
        send, recv = refs[2 * n:]
        x, y, c, chips = _pos()
        sends = []
        for i in range(n):
            for j, (px, py) in enumerate(chips):
                landed = _half_rows(outs[i].at[2 * px + py], halves[i], c, 1)
                cp = _rcopy(landed, landed, send.at[3 * i + j], recv.at[3 * i + j], (x, y, 1 - c))
                cp.start()
                sends.append(cp)
        for i in range(n):
            for j, (px, py) in enumerate(chips):
                other = _half_rows(outs[i].at[2 * px + py], halves[i], 1 - c, 1)
                _rcopy(other, other, send.at[3 * i + j], recv.at[3 * i + j], (x, y, c)).wait_recv()
        for cp in sends:
            cp.wait_send()

    return pl.pallas_call(
        body, in_specs=[ANY] * n, out_specs=[ANY] * n, out_shape=[SDS(a.shape, a.dtype) for a in lands],
        input_output_aliases={k: k for k in range(n)}, scratch_shapes=_dma_sems(3 * n, 3 * n), name=name)(*lands)


def _exchange_halves(gs, name):
    n = len(gs)

    def body(*refs):
        ins, outs = refs[:n], refs[n:2 * n]
        send, recv = refs[2 * n:]
        x, y, c, _ = _pos()
        sends = []
        for k in range(n):
            cp = _rcopy(_half_rows(ins[k], gs[k].shape[1] // 2, 1 - c, 1), outs[k], send.at[k], recv.at[k], (x, y, 1 - c))
            cp.start()
            sends.append(cp)
        for k in range(n):
            _rcopy(outs[k], outs[k], send.at[k], recv.at[k], (x, y, c)).wait_recv()
        for cp in sends:
            cp.wait_send()

    return pl.pallas_call(
        body, in_specs=[ANY] * n, out_specs=[ANY] * n,
        out_shape=[SDS((4, g.shape[1] // 2, g.shape[2]), g.dtype) for g in gs],
        scratch_shapes=_dma_sems(n, n), name=name)(*gs)


def _scatter_partials(ps, name):
    n = len(ps)

    def body(*refs):
        ins, outs = refs[:n], refs[n:2 * n]
        send, recv = refs[2 * n:]
        x, y, c, chips = _pos()
        sends = []
        for k in range(n):
            for j, (px, py) in enumerate(chips):
                cp = _rcopy(ins[k].at[2 * px + py], outs[k].at[j], send.at[3 * k + j], recv.at[3 * k + j], (px, py, c))
                cp.start()
                sends.append(cp)
        for k in range(n):
            for j in range(3):
                _rcopy(outs[k].at[j], outs[k].at[j], send.at[3 * k + j], recv.at[3 * k + j], (x, y, c)).wait_recv()
        for cp in sends:
            cp.wait_send()

    return pl.pallas_call(
        body, in_specs=[ANY] * n, out_specs=[ANY] * n,
        out_shape=[SDS((3,) + p.shape[1:], p.dtype) for p in ps],
        scratch_shapes=_dma_sems(3 * n, 3 * n), name=name)(*ps)


def _join_halves(rs, name):
    n = len(rs)

    def body(*refs):
        outs = refs[n:2 * n]
        send, recv = refs[2 * n:]
        x, y, c, _ = _pos()
        sends = []
        for k in range(n):
            mine = _half_rows(outs[k], rs[k].shape[0] // 2, c, 0)
            cp = _rcopy(mine, mine, send.at[k], recv.at[k], (x, y, 1 - c))
            cp.start()
            sends.append(cp)
        for k in range(n):
            other = _half_rows(outs[k], rs[k].shape[0] // 2, 1 - c, 0)
            _rcopy(other, other, send.at[k], recv.at[k], (x, y, c)).wait_recv()
        for cp in sends:
            cp.wait_send()

    return pl.pallas_call(
        body, in_specs=[ANY] * n, out_specs=[ANY] * n, out_shape=[SDS(r.shape, r.dtype) for r in rs],
        input_output_aliases={k: k for k in range(n)}, scratch_shapes=_dma_sems(n, n), name=name)(*rs)


def _all_reduce_small(pack, name):
    R = pack.shape[0]

    def body(in_ref, out_ref, buf, send, recv):
        x, y, c, _ = _pos()
        me = 4 * x + 2 * y + c
        buf[me] = in_ref[...]
        sends = []
        for k in range(1, 8):
            peer = me ^ k
            cp = _rcopy(buf.at[me], buf.at[me], send.at[k - 1], recv.at[k - 1], ((peer >> 2) & 1, (peer >> 1) & 1, peer & 1))
            cp.start()
            sends.append(cp)
        for k in range(1, 8):
            _rcopy(buf.at[me ^ k], buf.at[me ^ k], send.at[k - 1], recv.at[k - 1], (x, y, c)).wait_recv()
        for cp in sends:
            cp.wait_send()
        acc = buf[0]
        for d in range(1, 8):
            acc = acc + buf[d]
        out_ref[...] = acc

    return pl.pallas_call(
        body, out_shape=SDS((R, 128), f32),
        in_specs=[pl.BlockSpec(memory_space=pltpu.VMEM)], out_specs=pl.BlockSpec(memory_space=pltpu.VMEM),
        scratch_shapes=[pltpu.VMEM((8, R, 128), f32)] + _dma_sems(7, 7), name=name)(pack)


def _add_sibling(g, recv, c_arr, tr, name):
    _, R, C = g.shape
    h = R // 2
    nrb = h // tr
    assert h % tr == 0

    def body(c_ref, g_ref, r_ref, o_ref):
        o_ref[...] = (g_ref[...] + r_ref[...]).astype(o_ref.dtype)

    spec = pl.BlockSpec((1, tr, C), lambda s, r, c_ref: (s, r, 0))
    grid_spec = pltpu.PrefetchScalarGridSpec(
        num_scalar_prefetch=1, grid=(4, nrb),
        in_specs=[pl.BlockSpec((1, tr, C), lambda s, r, c_ref: (s, c_ref[0] * nrb + r, 0)), spec], out_specs=spec)
    return pl.pallas_call(body, grid_spec=grid_spec, out_shape=SDS((4, h, C), bf16), name=name,
                          compiler_params=_params("parallel", "parallel"))(c_arr, g, recv)


def _add_chips(p, recv, sc_arr, tr, name):
    _, h, C = p.shape
    nrb = h // tr
    assert h % tr == 0

    def body(sc_ref, p_ref, r_ref, o_ref):
        o_ref[...] = (p_ref[0].astype(f32) + r_ref[0].astype(f32)) + (r_ref[1].astype(f32) + r_ref[2].astype(f32))

    grid_spec = pltpu.PrefetchScalarGridSpec(
        num_scalar_prefetch=1, grid=(nrb,),
        in_specs=[pl.BlockSpec((1, tr, C), lambda r, sc_ref: (sc_ref[0], r, 0)),
                  pl.BlockSpec((3, tr, C), lambda r, sc_ref: (0, r, 0))],
        out_specs=pl.BlockSpec((tr, C), lambda r, sc_ref: (sc_ref[1] * nrb + r, 0)))
    return pl.pallas_call(body, grid_spec=grid_spec, out_shape=SDS((2 * h, C), f32), name=name,
                          compiler_params=_params("parallel"))(sc_arr, p, recv)


_BIG = (("w_in", 1024, 256), ("w_out", 256, 128), ("ffn_w_in", 1024, 256), ("ffn_w_out", 704, 352))
_SMALL = ("dn_conv_w", "ffn_conv_w", "ffn_conv_b", "norm_pre_mix", "norm_post_mix", "norm_pre_ffn", "norm_post_ffn",
          "dn_norm_w", "dn_a_log", "dn_dt_bias")
_WEIGHTS = ("w_in", "dn_conv_w", "dn_a_log", "dn_dt_bias", "dn_norm_w", "w_out", "ffn_w_in", "ffn_conv_w", "ffn_conv_b",
            "ffn_w_out", "norm_pre_mix", "norm_post_mix", "norm_pre_ffn", "norm_post_ffn")
_ADAM_ROWS = {"w_in": 256, "w_out": 256, "ffn_w_in": 128, "ffn_w_out": 176}


def _shard_major(name, g):
    if name == "w_in":
        return jnp.stack([g[:, 898 * s:898 * (s + 1)] for s in range(4)])
    if name == "ffn_w_in":
        return g
    return g.reshape(4, g.shape[0] // 4, g.shape[1])


def kernel(x, w_in, dn_conv_w, dn_a_log, dn_dt_bias, dn_norm_w, w_out, ffn_w_in, ffn_conv_w, ffn_conv_b, ffn_w_out, norm_pre_mix, norm_post_mix, norm_pre_ffn, norm_post_ffn, loss_target, m_w_in, m_dn_conv_w, m_dn_a_log, m_dn_dt_bias, m_dn_norm_w, m_w_out, m_ffn_w_in, m_ffn_conv_w, m_ffn_conv_b, m_ffn_w_out, m_norm_pre_mix, m_norm_post_mix, m_norm_pre_ffn, m_norm_post_ffn, v_w_in, v_dn_conv_w, v_dn_a_log, v_dn_dt_bias, v_dn_norm_w, v_w_out, v_ffn_w_in, v_ffn_conv_w, v_ffn_conv_b, v_ffn_w_out, v_norm_pre_mix, v_norm_post_mix, v_norm_pre_ffn, v_norm_post_ffn):
    w = dict(w_in=w_in, dn_conv_w=dn_conv_w, dn_a_log=dn_a_log, dn_dt_bias=dn_dt_bias, dn_norm_w=dn_norm_w, w_out=w_out,
             ffn_w_in=ffn_w_in, ffn_conv_w=ffn_conv_w, ffn_conv_b=ffn_conv_b, ffn_w_out=ffn_w_out, norm_pre_mix=norm_pre_mix,
             norm_post_mix=norm_post_mix, norm_pre_ffn=norm_pre_ffn, norm_post_ffn=norm_post_ffn)
    m = dict(w_in=m_w_in, dn_conv_w=m_dn_conv_w, dn_a_log=m_dn_a_log, dn_dt_bias=m_dn_dt_bias, dn_norm_w=m_dn_norm_w,
             w_out=m_w_out, ffn_w_in=m_ffn_w_in, ffn_conv_w=m_ffn_conv_w, ffn_conv_b=m_ffn_conv_b, ffn_w_out=m_ffn_w_out,
             norm_pre_mix=m_norm_pre_mix, norm_post_mix=m_norm_post_mix, norm_pre_ffn=m_norm_pre_ffn,
             norm_post_ffn=m_norm_post_ffn)
    v = dict(w_in=v_w_in, dn_conv_w=v_dn_conv_w, dn_a_log=v_dn_a_log, dn_dt_bias=v_dn_dt_bias, dn_norm_w=v_dn_norm_w,
             w_out=v_w_out, ffn_w_in=v_ffn_w_in, ffn_conv_w=v_ffn_conv_w, ffn_conv_b=v_ffn_conv_b, ffn_w_out=v_ffn_w_out,
             norm_pre_mix=v_norm_pre_mix, norm_post_mix=v_norm_post_mix, norm_pre_ffn=v_norm_pre_ffn,
             norm_post_ffn=v_norm_post_ffn)
    xi, yi, ci = lax.axis_index("x"), lax.axis_index("y"), lax.axis_index("c")
    s_me = 2 * xi + yi
    c_arr = jnp.reshape(ci, (1,)).astype(jnp.int32)
    sc_arr = jnp.stack([s_me, ci]).astype(jnp.int32)

    mats = [name for name, _, _ in _BIG]
    rest = mats[1:]
    half_of = {name: rows // 2 for name, rows, _ in _BIG}
    tiles = {name: tr for name, _, tr in _BIG}
    gathered_shape = lambda a: SDS((4,) + a.shape, a.dtype)

    own = {k: w[k].astype(bf16) for k in mats}
    got_in = _all_gather([own["w_in"][0:1], dn_conv_w, ffn_conv_w], [half_of["w_in"], None, None], "weights_gather_w_in0")
    plan0 = _gather_plan([half_of[k] for k in rest])
    src0 = [own[k][0:1] for k in rest]
    started0 = _split_copy(src0, [gathered_shape(a) for a in src0], plan0, 3, got_in[0], "weights_gather_l0_start")
    plan1 = _gather_plan([half_of[k] for k in mats])
    src1 = [own[k][1:2] for k in mats]
    started1 = _split_copy(src1, [gathered_shape(a) for a in src1], plan1, 3, started0[-1], "weights_gather_l1_start")

    def pick(mine, gathered):
        return [jnp.where(s_me == s, mine, gathered[s]) for s in range(4)]

    conv = {"dn_conv_w": jnp.concatenate(pick(dn_conv_w, got_in[1]), axis=-1),
            "ffn_conv_w": jnp.concatenate(pick(ffn_conv_w, got_in[2]), axis=-1)}
    lanes = lambda a: jnp.pad(a, ((0, 0), (0, 128 - a.shape[1])))
    vec = dict(dn_a_log=lanes(dn_a_log), dn_dt_bias=lanes(dn_dt_bias), dn_norm_w=dn_norm_w, ffn_conv_b=ffn_conv_b,
               norm_pre_mix=norm_pre_mix, norm_post_mix=norm_post_mix, norm_pre_ffn=norm_pre_ffn, norm_post_ffn=norm_post_ffn)

    def matrices(l, names, gathered):
        sh = {k: pick(own[k][l], a[:, 0]) for k, a in zip(names, gathered)}
        W = {}
        for k in names:
            if k == "w_in":
                W[k] = jnp.pad(jnp.concatenate(sh[k], axis=-1), ((0, 0), (0, PROJ_W - IN_COLS)))
            else:
                W[k] = jnp.concatenate(sh[k], axis=-1 if k == "ffn_w_in" else 0)
        return W

    def small_weights(l):
        return {**{k: a[l] for k, a in conv.items()}, **{k: a[l:l + 1] for k, a in vec.items()}}

    def late_l0(mix_in):
        _, landed = _split_wait(started0, len(rest), plan0, mix_in, "weights_gather_l0_wait")
        return matrices(0, rest, _pass_to_sibling(landed, [half_of[k] for k in rest], "weights_gather_l0_sibling"))

    cos, sgn_sin = _rope_tables(x.shape[1])
    W0 = {**small_weights(0), **matrices(0, ["w_in"], got_in[:1])}
    W0_first = dict(W0, norm_pre_mix=W0["norm_pre_mix"] + started1[-1][0, 0])
    x1, saved0 = _layer_fwd(x[0], W0_first, cos, sgn_sin, 0, late_weights=late_l0)
    _, landed1 = _split_wait(started1, len(mats), plan1, x1, "weights_gather_l1_wait")
    W1 = {**small_weights(1),
          **matrices(1, mats, _pass_to_sibling(landed1, [half_of[k] for k in mats], "weights_gather_l1_sibling"))}
    x2, saved1 = _layer_fwd(x1, W1, cos, sgn_sin, 1)
    loss_local, dy = _loss_head(x2, loss_target[0], "loss_head")
    loss = lax.psum(loss_local, ("x", "y", "c"))

    def shard_major(names, grads_l):
        return [_shard_major(name, grads_l[name]) for name in names]

    def add_siblings(l, names, gs, from_sib):
        return [_add_sibling(g, r, c_arr, tiles[name], f"add_sibling_{name}{l}") for g, r, name in zip(gs, from_sib, names)]

    def scatter_start(l, names, parts, after, tag):
        return _split_copy(parts, [SDS((3,) + p.shape[1:], p.dtype) for p in parts], _scatter_plan, 3, after,
                           f"grads_l{l}{tag}_scatter_start")

    def owner_sums(l, names, sent, after, tag):
        parts, recvd = _split_wait(sent, len(names), _scatter_plan, after, f"grads_l{l}{tag}_scatter_wait")
        return [_add_chips(p, r, sc_arr, tiles[name], f"add_chips_{name}{l}") for p, r, name in zip(parts, recvd, names)]

    dx1, grads1 = _layer_bwd(dy, saved1, W1, cos, sgn_sin, 1)
    gs1 = shard_major(mats, grads1)
    swap1 = _split_copy(gs1, [SDS((4, g.shape[1] // 2, g.shape[2]), g.dtype) for g in gs1], _exchange_plan, 1, dx1,
                        "grads_l1_sibling_start")
    ffn = ["ffn_w_in", "ffn_w_out"]
    launched = {}

    def after_ffn_l0(g_ffn, dx_mid):
        gs1_, from_sib1 = _split_wait(swap1, len(mats), _exchange_plan, dx_mid, "grads_l1_sibling_wait")
        launched["l1"] = scatter_start(1, mats, add_siblings(1, mats, gs1_, from_sib1), dx_mid, "")
        gs0 = shard_major(ffn, g_ffn)
        from_sib0 = _exchange_halves(gs0, "grads_l0_ffn_to_sibling")
        launched["l0_ffn"] = scatter_start(0, ffn, add_siblings(0, ffn, gs0, from_sib0), launched["l1"][-1], "_ffn")
        return launched["l0_ffn"][-1][0, 0]

    W0_last = dict(W0, **saved0["late"], norm_post_ffn=W0["norm_post_ffn"] + swap1[-1][0, 0])
    dx, grads0 = _layer_bwd(dx1, saved0, W0_last, cos, sgn_sin, 0, after_ffn=after_ffn_l0)
    mix = ["w_in", "w_out"]
    gs0 = shard_major(mix, grads0)
    part0 = add_siblings(0, mix, gs0, _exchange_halves(gs0, "grads_l0_mix_to_sibling"))
    recvd0 = _scatter_partials(part0, "grads_l0_mix_scatter")
    red = dict(zip([(0, k) for k in mix],
                   [_add_chips(p, r, sc_arr, tiles[k], f"add_chips_{k}0") for p, r, k in zip(part0, recvd0, mix)]))
    red.update(zip([(0, k) for k in ffn], owner_sums(0, ffn, launched["l0_ffn"], dx, "_ffn")))
    red.update(zip([(1, k) for k in mats], owner_sums(1, mats, launched["l1"], dx, "")))
    order = [(l, k) for l in range(2) for k in mats]
    joined = dict(zip(order, _join_halves([red[key] for key in order], "grads_join_halves")))
    g_out = {k: jnp.stack([joined[(0, k)], joined[(1, k)]]) for k in mats}
    grads = [grads0, grads1]

    small = {}
    for name in _SMALL:
        per_layer = [grads[l][name] for l in range(2)]
        if name in ("dn_a_log", "dn_dt_bias"):
            per_layer = [p[:, :N_HEADS_D] for p in per_layer]
        small[name] = jnp.stack(per_layer).reshape((2,) + (w[name].shape[1:] if name not in ("dn_conv_w", "ffn_conv_w")
                                                           else per_layer[0].shape))
    flat = jnp.concatenate([small[name].reshape(-1) for name in _SMALL])
    n_rows = -(-flat.shape[0] // 1024) * 8
    summed = _all_reduce_small(jnp.pad(flat, (0, n_rows * 128 - flat.shape[0])).reshape(n_rows, 128),
                               "small_grads_all_reduce").reshape(-1)
    off = 0
    for name in _SMALL:
        size = small[name].size
        g_out[name] = summed[off:off + size].reshape(small[name].shape)
        off += size
    g_out["dn_conv_w"] = lax.dynamic_slice_in_dim(g_out["dn_conv_w"], s_me * 384, 384, axis=2)
    g_out["ffn_conv_w"] = lax.dynamic_slice_in_dim(g_out["ffn_conv_w"], s_me * 1408, 1408, axis=2)

    deltas, new_m, new_v = {}, {}, {}
    for name in _WEIGHTS:
        shape = w[name].shape
        as3 = (lambda a: a) if len(shape) == 3 else (lambda a: a.reshape(shape[0], 1, shape[1]))
        tr = _ADAM_ROWS.get(name, as3(w[name]).shape[1])
        d_, m_, v_ = _adamw(as3(w[name]), as3(g_out[name]), as3(m[name]), as3(v[name]), tr, f"adamw_{name}")
        deltas[name], new_m[name], new_v[name] = d_.reshape(shape), m_.reshape(shape), v_.reshape(shape)

    return (loss, dx[None], *[g_out[k] for k in _WEIGHTS], *[deltas[k] for k in _WEIGHTS],
            *[new_m[k] for k in _WEIGHTS], *[new_v[k] for k in _WEIGHTS])
```

```python
import jax
import jax.numpy as jnp
from jax import lax
from jax.experimental import pallas as pl
from jax.experimental.pallas import tpu as pltpu

f32, bf16 = jnp.float32, jnp.bfloat16
SDS = jax.ShapeDtypeStruct
HI = lax.Precision.HIGH
MESH = pl.DeviceIdType.MESH
ANY = pl.BlockSpec(memory_space=pl.ANY)

D_MODEL = 1024
N_HEADS_A, HEAD_DIM = 8, 64
ATTN_W = 512
N_HEADS_D, DK = 4, 128
CHUNK = 64
D_FF = 2816
IN_COLS = 3592
PROJ_W = 3840
BRANCHES = ((1, 16), (4, 4), (16, 1))
EPS = 1e-6
NEG = -1e30
ROW_TILE = 256
VMEM_LIMIT = 56 * 1024 * 1024

ADAM_LR, ADAM_B1, ADAM_B2, ADAM_EPS, ADAM_WD, ADAM_STEP = 0.001, 0.9, 0.999, 1e-08, 0.01, 10


def _params(*sem):
    return pltpu.CompilerParams(dimension_semantics=sem, vmem_limit_bytes=VMEM_LIMIT)


def _mm(a, b, mode, tm, tn, out_dtype, name, column_shards=False):
    if mode == "nn":
        (M, K), N = a.shape, b.shape[1]
        dims = (((1,), (0,)), ((), ()))
        a_spec = pl.BlockSpec((tm, K), lambda i, j: (i, 0))
        b_spec = pl.BlockSpec((K, tn), lambda i, j: (0, j))
    elif mode == "nt":
        (M, K), N = a.shape, b.shape[0]
        dims = (((1,), (1,)), ((), ()))
        a_spec = pl.BlockSpec((tm, K), lambda i, j: (i, 0))
        b_spec = pl.BlockSpec((tn, K), lambda i, j: (j, 0))
    else:
        (K, M), N = a.shape, b.shape[1]
        dims = (((0,), (0,)), ((), ()))
        a_spec = pl.BlockSpec((K, tm), lambda i, j: (0, i))
        b_spec = pl.BlockSpec((K, tn), lambda i, j: (0, j))
    assert M % tm == 0 and N % tn == 0, (name, M, N, tm, tn)

    def body(a_ref, b_ref, o_ref):
        o_ref[...] = lax.dot_general(a_ref[...].astype(bf16), b_ref[...].astype(bf16), dims,
                                     preferred_element_type=f32).astype(o_ref.dtype)

    if column_shards:
        out_spec, out_shape = pl.BlockSpec((None, tm, tn), lambda i, j: (j, i, 0)), SDS((N // tn, M, tn), out_dtype)
    else:
        out_spec, out_shape = pl.BlockSpec((tm, tn), lambda i, j: (i, j)), SDS((M, N), out_dtype)
    return pl.pallas_call(body, grid=(M // tm, N // tn), in_specs=[a_spec, b_spec], out_specs=out_spec,
                          out_shape=out_shape, name=name, compiler_params=_params("parallel", "arbitrary"))(a, b)


def _row_spec(r, tm):
    if isinstance(r, tuple):
        arr, width, cb = r
        return arr, pl.BlockSpec((tm, width), lambda i, j, cb=cb: (i, cb + j))
    return r, pl.BlockSpec((tm, r.shape[1]), lambda i, j: (i, j))


def _full_spec(p):
    return pl.BlockSpec(p.shape, lambda i, j: (0,) * p.ndim)


def _rows(fn, rows, params, outs, name, tm=ROW_TILE, ncol=1):
    arrs, specs = zip(*[_row_spec(r, tm) for r in rows])
    S = arrs[0].shape[0]
    nr, npar = len(rows), len(params)

    def body(*refs):
        vals = fn(*[r[...].astype(f32) for r in refs[:nr]], *[p[...] for p in refs[nr:nr + npar]])
        for o_ref, v in zip(refs[nr + npar:], vals):
            o_ref[...] = v.astype(o_ref.dtype)

    return pl.pallas_call(
        body, grid=(S // tm, ncol), in_specs=list(specs) + [_full_spec(p) for p in params],
        out_specs=[pl.BlockSpec((tm, w), lambda i, j: (i, j)) for w, _ in outs],
        out_shape=[SDS((S, w * ncol), dt) for w, dt in outs], name=name,
        compiler_params=_params("parallel", "parallel"))(*arrs, *params)


def _rows_vjp(fn, rows, params, cts, wrt_rows, wrt_params, name, adds=None, tm=ROW_TILE, ncol=1, row_dtype=f32):
    adds = adds or {}
    arrs, specs = zip(*[_row_spec(r, tm) for r in rows])
    carrs, cspecs = zip(*[_row_spec(c, tm) for c in cts])
    add_keys = sorted(adds)
    aarrs = [adds[k] for k in add_keys]
    S = arrs[0].shape[0]
    nr, npar, nc, na = len(rows), len(params), len(cts), len(aarrs)
    widths = [specs[k].block_shape[1] for k in wrt_rows]

    def body(*refs):
        first = jnp.logical_and(pl.program_id(0) == 0, pl.program_id(1) == 0)
        rv = [r[...].astype(f32) for r in refs[:nr]]
        pv = [p[...] for p in refs[nr:nr + npar]]
        cv = tuple(c[...].astype(f32) for c in refs[nr + npar:nr + npar + nc])
        av = dict(zip(add_keys, refs[nr + npar + nc:nr + npar + nc + na]))
        o = refs[nr + npar + nc + na:]
        _, vjp = jax.vjp(fn, *rv, *pv)
        g = vjp(cv)
        for n, k in enumerate(wrt_rows):
            val = g[k]
            if k in av:
                val = val + av[k][...]
            o[n][...] = val.astype(o[n].dtype)
        for n, k in enumerate(wrt_params):
            ref = o[len(wrt_rows) + n]

            @pl.when(first)
            def _(ref=ref):
                ref[...] = jnp.zeros_like(ref)

            ref[...] += g[nr + k]

    res = pl.pallas_call(
        body, grid=(S // tm, ncol),
        in_specs=list(specs) + [_full_spec(p) for p in params] + list(cspecs)
        + [pl.BlockSpec((tm, a.shape[1] // ncol), lambda i, j: (i, j)) for a in aarrs],
        out_specs=[pl.BlockSpec((tm, w), lambda i, j: (i, j)) for w in widths] + [_full_spec(params[k]) for k in wrt_params],
        out_shape=[SDS((S, w * ncol), row_dtype) for w in widths] + [SDS(params[k].shape, f32) for k in wrt_params],
        name=name, compiler_params=_params("arbitrary", "arbitrary"))(*arrs, *params, *carrs, *aarrs)
    return res[:len(wrt_rows)], res[len(wrt_rows):]


def _rms(x, w):
    return x * lax.rsqrt(jnp.mean(x * x, axis=-1, keepdims=True) + EPS) * w


def _rms_fn(x, w):
    return (_rms(x, w),)


def _res_rms_fn(f, res, w):
    return (res + _rms(f, w),)


def _swap_halves(x):
    lane = lax.broadcasted_iota(jnp.int32, x.shape, 1)
    first = (lane % HEAD_DIM) < (HEAD_DIM // 2)
    n = x.shape[1]
    return jnp.where(first, pltpu.roll(x, n - HEAD_DIM // 2, 1), pltpu.roll(x, HEAD_DIM // 2, 1))


def _rope_fwd_fn(q, k, cos, sgn_sin):
    scale = HEAD_DIM ** -0.5
    return ((q * cos + _swap_halves(q) * sgn_sin) * scale, k * cos + _swap_halves(k) * sgn_sin)


def _rope_bwd_fn(dq, dk, cos, sgn_sin):
    dq = dq * (HEAD_DIM ** -0.5)
    return (dq * cos + _swap_halves(dq * sgn_sin), dk * cos + _swap_halves(dk * sgn_sin))


def _nt(a, b):
    return lax.dot_general(a, b, (((1,), (1,)), ((), ())), preferred_element_type=f32)


def _tn(a, b):
    return lax.dot_general(a, b, (((0,), (0,)), ((), ())), preferred_element_type=f32)


def _band_rows(j, d, nb):
    r, i = j // nb, j % nb
    if d == 1:
        cur = pl.ds(pl.multiple_of(i * 128, 128), 128)
        prev = pl.ds(pl.multiple_of(jnp.maximum(i - 1, 0) * 128, 128), 128)
    else:
        cur = pl.ds(i * (128 * d) + r, 128, stride=d)
        prev = pl.ds(jnp.maximum(i - 1, 0) * (128 * d) + r, 128, stride=d)
    a = lax.broadcasted_iota(jnp.int32, (128, 128), 0)
    c = lax.broadcasted_iota(jnp.int32, (128, 128), 1)
    return cur, prev, c <= a, jnp.logical_and(c >= a, i != 0)


def _attn_fwd(qr, kr, proj, name):
    S = qr.shape[0]
    nblk = S // 128

    def body(q_ref, k_ref, v_ref, out_ref, lse_ref, *scr):
        head_a = lax.broadcasted_iota(jnp.int32, (1, 128), 1) < HEAD_DIM
        for b, (d, nb) in enumerate(BRANCHES):
            ob_ref, lb_ref = scr[2 * b], scr[2 * b + 1]

            def blk(j, carry, d=d, nb=nb, ob_ref=ob_ref, lb_ref=lb_ref):
                cur, prev, mc, mp = _band_rows(j, d, nb)
                q = q_ref[cur, :]
                kc, kp = k_ref[cur, :].astype(bf16), k_ref[prev, :].astype(bf16)
                vc, vp = v_ref[cur, :].astype(bf16), v_ref[prev, :].astype(bf16)
                res = []
                for m in (head_a, jnp.logical_not(head_a)):
                    qm = jnp.where(m, q, 0.0).astype(bf16)
                    sc = jnp.where(mc, _nt(qm, kc), NEG)
                    sp = jnp.where(mp, _nt(qm, kp), NEG)
                    mx = jnp.maximum(jnp.max(sc, axis=1, keepdims=True), jnp.max(sp, axis=1, keepdims=True))
                    pc, pp = jnp.exp(sc - mx), jnp.exp(sp - mx)
                    l = jnp.sum(pc, axis=1, keepdims=True) + jnp.sum(pp, axis=1, keepdims=True)
                    o = (jnp.dot(pc.astype(bf16), vc, preferred_element_type=f32)
                         + jnp.dot(pp.astype(bf16), vp, preferred_element_type=f32)) / l
                    res.append((o, mx + jnp.log(l)))
                ob_ref[cur, :] = jnp.where(head_a, res[0][0], res[1][0])
                lb_ref[cur, :] = jnp.where(head_a, res[0][1], res[1][1])
                return carry

            lax.fori_loop(0, nblk, blk, 0, unroll=4)
        l0, l1, l2 = scr[1][...], scr[3][...], scr[5][...]
        mx = jnp.maximum(jnp.maximum(l0, l1), l2)
        e0, e1, e2 = jnp.exp(l0 - mx), jnp.exp(l1 - mx), jnp.exp(l2 - mx)
        den = e0 + e1 + e2
        out_ref[...] = ((e0 * scr[0][...] + e1 * scr[2][...] + e2 * scr[4][...]) / den).astype(out_ref.dtype)
        lse_ref[...] = mx + jnp.log(den)

    pair = pl.BlockSpec((S, 128), lambda h: (0, h))
    return pl.pallas_call(
        body, grid=(N_HEADS_A // 2,),
        in_specs=[pair, pair, pl.BlockSpec((S, 128), lambda h: (0, 2 * ATTN_W // 128 + h))], out_specs=[pair, pair],
        out_shape=[SDS((S, ATTN_W), bf16), SDS((S, ATTN_W), f32)], scratch_shapes=[pltpu.VMEM((S, 128), f32)] * 6,
        name=name, compiler_params=_params("parallel"))(qr, kr, proj)


def _attn_bwd(qr, kr, proj, dmix_in, out, lse, name):
    S = qr.shape[0]
    nblk = S // 128

    def body(q_ref, k_ref, v_ref, do_ref, out_ref, lse_ref, dq_ref, dk_ref, dv_ref, t_ref):
        head_a = lax.broadcasted_iota(jnp.int32, (1, 128), 1) < HEAD_DIM
        x = do_ref[...] * out_ref[...].astype(f32)
        t_ref[...] = jnp.where(head_a, jnp.sum(jnp.where(head_a, x, 0.0), axis=1, keepdims=True),
                               jnp.sum(jnp.where(head_a, 0.0, x), axis=1, keepdims=True))
        dq_ref[...] = jnp.zeros_like(dq_ref)
        dk_ref[...] = jnp.zeros_like(dk_ref)
        dv_ref[...] = jnp.zeros_like(dv_ref)
        for d, nb in BRANCHES:
            def blk(j, carry, d=d, nb=nb):
                cur, prev, mc, mp = _band_rows(j, d, nb)
                q, do = q_ref[cur, :], do_ref[cur, :]
                kc, kp = k_ref[cur, :].astype(bf16), k_ref[prev, :].astype(bf16)
                vc, vp = v_ref[cur, :].astype(bf16), v_ref[prev, :].astype(bf16)
                t, lse_b = t_ref[cur, :], lse_ref[cur, :]
                dq = dkc = dkp = dvc = dvp = jnp.zeros((128, 128), f32)
                for m, off in ((head_a, 0), (jnp.logical_not(head_a), HEAD_DIM)):
                    th, lh = t[:, off:off + 1], lse_b[:, off:off + 1]
                    qm = jnp.where(m, q, 0.0).astype(bf16)
                    dom = jnp.where(m, do, 0.0).astype(bf16)
                    pc = jnp.exp(jnp.where(mc, _nt(qm, kc), NEG) - lh)
                    pp = jnp.exp(jnp.where(mp, _nt(qm, kp), NEG) - lh)
                    dsc = (pc * (_nt(dom, vc) - th)).astype(bf16)
                    dsp = (pp * (_nt(dom, vp) - th)).astype(bf16)
                    dq = dq + jnp.where(m, jnp.dot(dsc, kc, preferred_element_type=f32)
                                        + jnp.dot(dsp, kp, preferred_element_type=f32), 0.0)
                    dvc, dvp = dvc + _tn(pc.astype(bf16), dom), dvp + _tn(pp.astype(bf16), dom)
                    dkc, dkp = dkc + _tn(dsc, qm), dkp + _tn(dsp, qm)
                dq_ref[cur, :] += dq
                dk_ref[cur, :] += dkc
                dv_ref[cur, :] += dvc
                dk_ref[prev, :] += dkp
                dv_ref[prev, :] += dvp
                return carry

            lax.fori_loop(0, nblk, blk, 0, unroll=4)

    pair = pl.BlockSpec((S, 128), lambda h: (0, h))
    return pl.pallas_call(
        body, grid=(N_HEADS_A // 2,),
        in_specs=[pair, pair, pl.BlockSpec((S, 128), lambda h: (0, 2 * ATTN_W // 128 + h)), pair, pair, pair],
        out_specs=[pair] * 3, out_shape=[SDS((S, ATTN_W), f32)] * 3, scratch_shapes=[pltpu.VMEM((S, 128), f32)],
        name=name, compiler_params=_params("parallel"))(qr, kr, proj, dmix_in, out, lse)


def _conv_val(x, w, K, rows):
    acc = x * w[K - 1:K, :]
    for s in range(1, K):
        acc = acc + jnp.where(rows >= s, pltpu.roll(x, s, 0), 0.0) * w[K - 1 - s:K - s, :]
    return acc


def _colconv_fwd(xs, ws, bs, K, fn, nblk, tc, outs, name):
    S = xs[0][0].shape[0]
    n = len(xs)
    has_b = bs is not None

    def body(*refs):
        rows = lax.broadcasted_iota(jnp.int32, (S, tc), 0)
        cs = []
        for k in range(n):
            c = _conv_val(refs[k][...].astype(f32), refs[n + k][...], K, rows)
            if has_b:
                c = c + refs[2 * n + k][...]
            cs.append(c)
        for o_ref, val in zip(refs[(3 if has_b else 2) * n:], fn(*cs)):
            o_ref[...] = val.astype(o_ref.dtype)

    def cspec(rows_, cb0):
        return pl.BlockSpec((rows_, tc), lambda j, cb0=cb0: (0, cb0 + j))

    in_specs = [cspec(S, cb) for _, cb in xs] + [cspec(K, cb) for _, cb in ws]
    args = [a for a, _ in xs] + [a for a, _ in ws]
    if has_b:
        in_specs += [cspec(1, cb) for _, cb in bs]
        args += [a for a, _ in bs]
    return pl.pallas_call(
        body, grid=(nblk,), in_specs=in_specs, out_specs=[cspec(S, 0) for _ in outs],
        out_shape=[SDS((S, nblk * tc), dt) for dt in outs], name=name, compiler_params=_params("parallel"))(*args)


def _colconv_bwd(xs, ws, bs, K, fn, douts, nblk, tc, name, dx_dtype=f32):
    S = xs[0][0].shape[0]
    n, nd = len(xs), len(douts)
    has_b = bs is not None
    nin = (3 if has_b else 2) * n

    def body(*refs):
        rows = lax.broadcasted_iota(jnp.int32, (S, tc), 0)
        x = [refs[k][...].astype(f32) for k in range(n)]
        w = [refs[n + k][...] for k in range(n)]
        cs = []
        for k in range(n):
            c = _conv_val(x[k], w[k], K, rows)
            if has_b:
                c = c + refs[2 * n + k][...]
            cs.append(c)
        _, vjp = jax.vjp(fn, *cs)
        dcs = vjp(tuple(r[...].astype(f32) for r in refs[nin:nin + nd]))
        o = refs[nin + nd:]
        for k in range(n):
            dc = dcs[k]
            dx = dc * w[k][K - 1:K, :]
            o[n + k][K - 1:K, :] = jnp.sum(dc * x[k], axis=0, keepdims=True)
            for s in range(1, K):
                dx = dx + jnp.where(rows < S - s, pltpu.roll(dc, S - s, 0), 0.0) * w[k][K - 1 - s:K - s, :]
                xsh = jnp.where(rows >= s, pltpu.roll(x[k], s, 0), 0.0)
                o[n + k][K - 1 - s:K - s, :] = jnp.sum(dc * xsh, axis=0, keepdims=True)
            o[k][...] = dx.astype(o[k].dtype)
            if has_b:
                o[2 * n + k][...] = jnp.sum(dc, axis=0, keepdims=True)

    def cspec(rows_, cb0):
        return pl.BlockSpec((rows_, tc), lambda j, cb0=cb0: (0, cb0 + j))

    in_specs = [cspec(S, cb) for _, cb in xs] + [cspec(K, cb) for _, cb in ws]
    args = [a for a, _ in xs] + [a for a, _ in ws]
    if has_b:
        in_specs += [cspec(1, cb) for _, cb in bs]
        args += [a for a, _ in bs]
    in_specs += [cspec(S, 0) for _ in douts]
    args += list(douts)
    W = nblk * tc
    out_specs = [cspec(S, 0)] * n + [cspec(K, 0)] * n + ([cspec(1, 0)] * n if has_b else [])
    out_shape = [SDS((S, W), dx_dtype)] * n + [SDS((K, W), f32)] * n + ([SDS((1, W), f32)] * n if has_b else [])
    res = pl.pallas_call(body, grid=(nblk,), in_specs=in_specs, out_specs=out_specs, out_shape=out_shape,
                         name=name, compiler_params=_params("parallel"))(*args)
    return res[:n], res[n:2 * n], res[2 * n:]


def _silu_fn(c):
    return (c * jax.nn.sigmoid(c),)


def _geglu_fn(gate, up):
    gelu = 0.5 * gate * (1.0 + jnp.tanh(0.7978845608028654 * (gate + 0.044715 * gate * gate * gate)))
    return (gelu * up,)


def _softplus(x):
    u = jnp.exp(jnp.minimum(x, 20.0))
    small = u * (1.0 - 0.5 * u)
    return jnp.where(x > 20.0, x, jnp.where(u < 1e-4, small, jnp.log(1.0 + u)))


def _bmm(a, b, precision=None):
    return lax.dot_general(a, b, (((2,), (1,)), ((0,), (0,))), precision=precision, preferred_element_type=f32)


def _bnt(a, b):
    return lax.dot_general(a, b, (((2,), (2,)), ((0,), (0,))), preferred_element_type=f32)


def _btn(a, b):
    return lax.dot_general(a, b, (((1,), (1,)), ((0,), (0,))), preferred_element_type=f32)


def _unit_lower_inverse(A):
    n = A.shape[-1]
    eye = (lax.broadcasted_iota(jnp.int32, (1, n, n), 1) == lax.broadcasted_iota(jnp.int32, (1, n, n), 2)).astype(f32)
    P = -A
    T = eye + P
    for _ in range(5):
        P = _bmm(P, P, HI)
        T = T + _bmm(T, P, HI)
    return T


def _dn_prep_fn(q, k, v, ba, alog, dtb, h):
    G, C = q.shape[0], CHUNK
    lane = lax.broadcasted_iota(jnp.int32, (1, 1, 128), 2)

    def sel(arr, idx):
        return jnp.sum(jnp.where(lane == idx, arr, 0.0), axis=-1, keepdims=True)

    beta = jax.nn.sigmoid(sel(ba, h))
    g = -jnp.exp(sel(alog[None], h)) * _softplus(sel(ba, N_HEADS_D + h) + sel(dtb[None], h))
    qn = q * lax.rsqrt(jnp.sum(q * q, axis=-1, keepdims=True) + EPS) * (DK ** -0.5)
    kn = k * lax.rsqrt(jnp.sum(k * k, axis=-1, keepdims=True) + EPS)
    ii = lax.broadcasted_iota(jnp.int32, (1, C, C), 1)
    jj = lax.broadcasted_iota(jnp.int32, (1, C, C), 2)
    tril, strict = ii >= jj, ii > jj
    gsq = jnp.broadcast_to(g, (G, C, C))
    gcol = _bmm(jnp.broadcast_to(tril.astype(f32), (G, C, C)), gsq, HI)
    grow = _bmm(jnp.ones((G, C, C), f32), jnp.where(ii <= jj, gsq, 0.0), HI)
    decay = jnp.exp(jnp.where(tril, gcol - grow, NEG))
    gc = gcol[:, :, :1]
    glast = gcol[:, C - 1:C, :1]
    kb = kn * beta
    A = jnp.where(strict, _bnt(kb.astype(bf16), kn.astype(bf16)) * decay, 0.0)
    T = _unit_lower_inverse(A).astype(bf16)
    u = _bmm(T, (v * beta).astype(bf16))
    w = _bmm(T, (kb * jnp.exp(gc)).astype(bf16))
    qk = _bnt(qn.astype(bf16), kn.astype(bf16)) * decay
    qd = qn * jnp.exp(gc)
    kd = kn * jnp.exp(glast - gc)
    return u, w, qk, qd, kd, jnp.broadcast_to(jnp.exp(glast), (G, C, DK))


def _dn_scan_fn(u, w, qk, qd, kd, eg, St):
    b = lambda a: a.astype(bf16)
    vnew = u - _bmm(b(w), b(St))
    o = _bmm(b(qd), b(St)) + _bmm(b(qk), b(vnew))
    return o, St * eg[:, :1, :] + _btn(b(kd), b(vnew))


def _dn_post_fn(o, z, nw):
    return (_rms(o, nw) * (z * jax.nn.sigmoid(z)),)


DN_GROUP = 8


def _dn_prep_specs(S, rows):
    def col(first):
        return pl.BlockSpec((rows, DK), lambda i, h, first=first: (i, first // DK + h))

    par = pl.BlockSpec((1, 128), lambda i, h: (0, 0))
    return [col(0), col(N_HEADS_D * DK), col(2 * N_HEADS_D * DK),
            pl.BlockSpec((rows, 128), lambda i, h: (i, 3584 // 128)), par, par]


def _dn_prep(qkv, proj, alog, dtb, name):
    S = qkv.shape[0]
    G = DN_GROUP
    rows = G * CHUNK

    def body(q_ref, k_ref, v_ref, ba_ref, al_ref, dt_ref, u_ref, w_ref, qk_ref, qd_ref, kd_ref, eg_ref):
        h = pl.program_id(1)
        r3 = lambda ref: ref[...].reshape(G, CHUNK, 128)
        u, w, qk, qd, kd, eg = _dn_prep_fn(r3(q_ref), r3(k_ref), r3(v_ref), r3(ba_ref), al_ref[...], dt_ref[...], h)
        for ref, val in ((u_ref, u), (w_ref, w), (qd_ref, qd), (kd_ref, kd), (eg_ref, eg)):
            ref[...] = val.reshape(rows, DK)
        qk_ref[:, :CHUNK] = qk.reshape(rows, CHUNK)
        qk_ref[:, CHUNK:] = jnp.zeros((rows, DK - CHUNK), f32)

    out = pl.BlockSpec((rows, DK), lambda i, h: (i, h))
    return pl.pallas_call(
        body, grid=(S // rows, N_HEADS_D), in_specs=_dn_prep_specs(S, rows), out_specs=[out] * 6,
        out_shape=[SDS((S, N_HEADS_D * DK), f32)] * 6, name=name,
        compiler_params=_params("parallel", "parallel"))(qkv, qkv, qkv, proj, alog, dtb)


def _dn_prep_bwd(qkv, proj, alog, dtb, cts, name):
    S = qkv.shape[0]
    G = DN_GROUP
    rows = G * CHUNK

    def body(q_ref, k_ref, v_ref, ba_ref, al_ref, dt_ref, du_ref, dw_ref, dqk_ref, dqd_ref, dkd_ref, deg_ref,
             dq_ref, dk_ref, dv_ref, dba_ref, dal_ref, ddt_ref):
        i, h = pl.program_id(0), pl.program_id(1)
        r3 = lambda ref: ref[...].reshape(G, CHUNK, 128)
        _, vjp = jax.vjp(lambda q, k, v, ba, al, dt: _dn_prep_fn(q, k, v, ba, al, dt, h),
                         r3(q_ref), r3(k_ref), r3(v_ref), r3(ba_ref), al_ref[...], dt_ref[...])
        dqk = dqk_ref[:, :CHUNK].reshape(G, CHUNK, CHUNK)
        dq, dk, dv, dba, dal, ddt = vjp((r3(du_ref), r3(dw_ref), dqk, r3(dqd_ref), r3(dkd_ref), r3(deg_ref)))
        dq_ref[...] = dq.reshape(rows, DK)
        dk_ref[...] = dk.reshape(rows, DK)
        dv_ref[...] = dv.reshape(rows, DK)

        @pl.when(h == 0)
        def _():
            dba_ref[...] = jnp.zeros_like(dba_ref)

        @pl.when(jnp.logical_and(i == 0, h == 0))
        def _():
            dal_ref[...] = jnp.zeros_like(dal_ref)
            ddt_ref[...] = jnp.zeros_like(ddt_ref)

        dba_ref[...] += dba.reshape(rows, 128)
        dal_ref[...] += dal
        ddt_ref[...] += ddt

    hcol = pl.BlockSpec((rows, DK), lambda i, h: (i, h))
    par = pl.BlockSpec((1, 128), lambda i, h: (0, 0))
    W = N_HEADS_D * DK
    return pl.pallas_call(
        body, grid=(S // rows, N_HEADS_D), in_specs=_dn_prep_specs(S, rows) + [hcol] * 6,
        out_specs=[hcol] * 3 + [pl.BlockSpec((rows, 128), lambda i, h: (i, 0)), par, par],
        out_shape=[SDS((S, W), f32)] * 3 + [SDS((S, 128), f32), SDS((1, 128), f32), SDS((1, 128), f32)], name=name,
        compiler_params=_params("arbitrary", "arbitrary"))(qkv, qkv, qkv, proj, alog, dtb, *cts)


def _heads(x):
    return jnp.stack([x[:, DK * h:DK * (h + 1)] for h in range(N_HEADS_D)])


SCAN_CHUNKS = 4


def _dn_scan(pre, name):
    S = pre[0].shape[0]
    NCH = S // CHUNK
    rows = SCAN_CHUNKS * CHUNK

    def body(u_ref, w_ref, qk_ref, qd_ref, kd_ref, eg_ref, o_ref, st_ref, s_ref):
        @pl.when(pl.program_id(0) == 0)
        def _():
            s_ref[...] = jnp.zeros_like(s_ref)

        St = s_ref[...]
        for k in range(SCAN_CHUNKS):
            r = slice(k * CHUNK, (k + 1) * CHUNK)
            st_ref[k] = St
            o, St = _dn_scan_fn(_heads(u_ref[r, :]), _heads(w_ref[r, :]), _heads(qk_ref[r, :])[:, :, :CHUNK],
                                _heads(qd_ref[r, :]), _heads(kd_ref[r, :]), _heads(eg_ref[r, :]), St)
            for h in range(N_HEADS_D):
                o_ref[r, DK * h:DK * (h + 1)] = o[h]
        s_ref[...] = St

    blk = pl.BlockSpec((rows, N_HEADS_D * DK), lambda n: (n, 0))
    return pl.pallas_call(
        body, grid=(S // rows,), in_specs=[blk] * 6,
        out_specs=[blk, pl.BlockSpec((SCAN_CHUNKS, N_HEADS_D, DK, DK), lambda n: (n, 0, 0, 0))],
        out_shape=[SDS((S, N_HEADS_D * DK), f32), SDS((NCH, N_HEADS_D, DK, DK), f32)],
        scratch_shapes=[pltpu.VMEM((N_HEADS_D, DK, DK), f32)], name=name, compiler_params=_params("arbitrary"))(*pre)


def _dn_scan_bwd(pre, states, do, name):
    S = do.shape[0]
    rows = SCAN_CHUNKS * CHUNK
    steps = S // rows

    def body(u_ref, w_ref, qk_ref, qd_ref, kd_ref, eg_ref, st_ref, do_ref,
             du_ref, dw_ref, dqk_ref, dqd_ref, dkd_ref, deg_ref, ds_ref):
        @pl.when(pl.program_id(0) == 0)
        def _():
            ds_ref[...] = jnp.zeros_like(ds_ref)

        dS = ds_ref[...]
        for k in reversed(range(SCAN_CHUNKS)):
            r = slice(k * CHUNK, (k + 1) * CHUNK)
            _, vjp = jax.vjp(_dn_scan_fn, _heads(u_ref[r, :]), _heads(w_ref[r, :]), _heads(qk_ref[r, :])[:, :, :CHUNK],
                             _heads(qd_ref[r, :]), _heads(kd_ref[r, :]), _heads(eg_ref[r, :]), st_ref[k])
            du, dw, dqk, dqd, dkd, deg, dS = vjp((_heads(do_ref[r, :]), dS))
            for h in range(N_HEADS_D):
                c = slice(DK * h, DK * (h + 1))
                for ref, val in ((du_ref, du), (dw_ref, dw), (dqd_ref, dqd), (dkd_ref, dkd), (deg_ref, deg)):
                    ref[r, c] = val[h]
                dqk_ref[r, DK * h:DK * h + CHUNK] = dqk[h]
                dqk_ref[r, DK * h + CHUNK:DK * (h + 1)] = jnp.zeros((CHUNK, DK - CHUNK), f32)
        ds_ref[...] = dS

    blk = pl.BlockSpec((rows, N_HEADS_D * DK), lambda n: (steps - 1 - n, 0))
    return pl.pallas_call(
        body, grid=(steps,),
        in_specs=[blk] * 6 + [pl.BlockSpec((SCAN_CHUNKS, N_HEADS_D, DK, DK), lambda n: (steps - 1 - n, 0, 0, 0)), blk],
        out_specs=[blk] * 6, out_shape=[SDS((S, N_HEADS_D * DK), f32)] * 6,
        scratch_shapes=[pltpu.VMEM((N_HEADS_D, DK, DK), f32)], name=name,
        compiler_params=_params("arbitrary"))(*pre, states, do)


def _loss_head(y, t, name):
    S, D = y.shape
    tm = ROW_TILE

    def body(y_ref, t_ref, dy_ref, l_ref):
        i = pl.program_id(0)
        d = y_ref[...] - t_ref[...]
        dy_ref[...] = d * (1.0 / D)
        part = jnp.sum(jnp.sum(d * d, axis=1, keepdims=True), axis=0, keepdims=True) * (0.5 / D)

        @pl.when(i == 0)
        def _():
            l_ref[...] = jnp.zeros_like(l_ref)

        l_ref[...] += jnp.broadcast_to(part, l_ref.shape)

    spec = pl.BlockSpec((tm, D), lambda i: (i, 0))
    dy, l = pl.pallas_call(body, grid=(S // tm,), in_specs=[spec, spec],
                           out_specs=[spec, pl.BlockSpec((1, 128), lambda i: (0, 0))],
                           out_shape=[SDS((S, D), f32), SDS((1, 128), f32)], name=name,
                           compiler_params=_params("arbitrary"))(y, t)
    return l[0, 0], dy


def _adamw(w, g, m, v, tr, name):
    L, R, C = w.shape
    assert R % tr == 0

    def body(w_ref, g_ref, m_ref, v_ref, d_ref, mo_ref, vo_ref):
        gv = g_ref[...]
        m2 = ADAM_B1 * m_ref[...] + (1.0 - ADAM_B1) * gv
        v2 = ADAM_B2 * v_ref[...] + (1.0 - ADAM_B2) * (gv * gv)
        m_hat = m2 / (1.0 - ADAM_B1 ** ADAM_STEP)
        v_hat = v2 / (1.0 - ADAM_B2 ** ADAM_STEP)
        d_ref[...] = -ADAM_LR * (m_hat / (jnp.sqrt(v_hat) + ADAM_EPS) + ADAM_WD * w_ref[...])
        mo_ref[...] = m2
        vo_ref[...] = v2

    spec = pl.BlockSpec((1, tr, C), lambda l, i: (l, i, 0))
    return pl.pallas_call(body, grid=(L, R // tr), in_specs=[spec] * 4, out_specs=[spec] * 3,
                          out_shape=[SDS((L, R, C), f32)] * 3, name=name,
                          compiler_params=_params("parallel", "parallel"))(w, g, m, v)


def _rope_tables(S):
    inv = 1.0 / (10000.0 ** (jnp.arange(0, HEAD_DIM, 2, dtype=f32) / HEAD_DIM))
    ang = jnp.arange(S, dtype=f32)[:, None] * inv[None, :]
    cos, sin = jnp.cos(ang), jnp.sin(ang)
    return (jnp.tile(jnp.concatenate([cos, cos], axis=1), (1, N_HEADS_A)),
            jnp.tile(jnp.concatenate([-sin, sin], axis=1), (1, N_HEADS_A)))


def _layer_fwd(x, W, cos, sgn_sin, l, late_weights=None):
    n = f"l{l}_"
    (h1,) = _rows(_rms_fn, [x], [W["norm_pre_mix"]], [(D_MODEL, bf16)], n + "pre_mix_norm")
    proj = _mm(h1, W["w_in"], "nn", 512, 768, f32, n + "in_proj")
    qr, kr = _rows(_rope_fwd_fn, [(proj, ATTN_W, 0), (proj, ATTN_W, 1), cos, sgn_sin], [],
                   [(ATTN_W, f32), (ATTN_W, f32)], n + "rope")
    attn_out, lse = _attn_fwd(qr, kr, proj, n + "attn_fwd")
    (qkv,) = _colconv_fwd([(proj, 3)], [(W["dn_conv_w"], 0)], None, 4, _silu_fn, 3, 512, [f32], n + "dn_conv")
    dn_pre = _dn_prep(qkv, proj, W["dn_a_log"], W["dn_dt_bias"], n + "dn_prep")
    dn_o, dn_states = _dn_scan(dn_pre, n + "dn_scan")
    (dn_out,) = _rows(_dn_post_fn, [(dn_o, DK, 0), (proj, DK, 3072 // DK)], [W["dn_norm_w"]], [(DK, bf16)], n + "dn_post",
                      ncol=N_HEADS_D)
    mix_in = jnp.concatenate([attn_out, dn_out], axis=1)
    late = late_weights(mix_in) if late_weights is not None else {}
    W = {**W, **late}
    mix = _mm(mix_in, W["w_out"], "nn", 512, 512, f32, n + "out_proj")
    (x1,) = _rows(_res_rms_fn, [mix, x], [W["norm_post_mix"]], [(D_MODEL, f32)], n + "post_mix_norm")
    (h2,) = _rows(_rms_fn, [x1], [W["norm_pre_ffn"]], [(D_MODEL, bf16)], n + "pre_ffn_norm")
    u0 = _mm(h2, W["ffn_w_in"], "nn", 1024, 512, bf16, n + "ffn_in")
    nb_ff = D_FF // 256
    (act,) = _colconv_fwd([(u0, 0), (u0, nb_ff)], [(W["ffn_conv_w"], 0), (W["ffn_conv_w"], nb_ff)],
                          [(W["ffn_conv_b"], 0), (W["ffn_conv_b"], nb_ff)], 3, _geglu_fn, nb_ff, 256, [bf16],
                          n + "ffn_conv_glu")
    f = _mm(act, W["ffn_w_out"], "nn", 512, 512, f32, n + "ffn_out")
    (x2,) = _rows(_res_rms_fn, [f, x1], [W["norm_post_ffn"]], [(D_MODEL, f32)], n + "post_ffn_norm")
    saved = dict(x=x, h1=h1, proj=proj, qr=qr, kr=kr, attn_out=attn_out, lse=lse, qkv=qkv, dn_pre=dn_pre, dn_o=dn_o,
                 dn_states=dn_states, mix_in=mix_in, mix=mix, x1=x1, h2=h2, u0=u0, act=act, f=f, late=late)
    return x2, saved


def _layer_bwd(dx2, sv, W, cos, sgn_sin, l, after_ffn=None):
    n = f"l{l}_"
    S = dx2.shape[0]
    g = {}
    (df,), (g["norm_post_ffn"],) = _rows_vjp(_rms_fn, [sv["f"]], [W["norm_post_ffn"]], [dx2], [0], [0],
                                             n + "post_ffn_norm_bwd", row_dtype=bf16)
    dact = _mm(df, W["ffn_w_out"], "nt", 512, 1408, f32, n + "ffn_out_dx")
    g["ffn_w_out"] = _mm(sv["act"], df, "tn", 256, 1024, f32, n + "ffn_out_dw")
    nb_ff = D_FF // 256
    u0 = sv["u0"]
    dxs, dws, dbs = _colconv_bwd([(u0, 0), (u0, nb_ff)], [(W["ffn_conv_w"], 0), (W["ffn_conv_w"], nb_ff)],
                                 [(W["ffn_conv_b"], 0), (W["ffn_conv_b"], nb_ff)], 3, _geglu_fn, [dact], nb_ff, 256,
                                 n + "ffn_conv_glu_bwd", dx_dtype=bf16)
    du0 = jnp.concatenate(dxs, axis=1)
    g["ffn_conv_w"] = jnp.concatenate(dws, axis=1)
    g["ffn_conv_b"] = jnp.concatenate(dbs, axis=1)
    dh2 = _mm(du0, W["ffn_w_in"], "nt", 512, 512, f32, n + "ffn_in_dx")
    g["ffn_w_in"] = _mm(sv["h2"], du0, "tn", 512, D_FF // 2, f32, n + "ffn_in_dw", column_shards=True)
    (dx1,), (g["norm_pre_ffn"],) = _rows_vjp(_rms_fn, [sv["x1"]], [W["norm_pre_ffn"]], [dh2], [0], [0],
                                             n + "pre_ffn_norm_bwd", adds={0: dx2})
    if after_ffn is not None:
        W = dict(W, norm_post_mix=W["norm_post_mix"] + after_ffn(g, dx1))
    (dmix,), (g["norm_post_mix"],) = _rows_vjp(_rms_fn, [sv["mix"]], [W["norm_post_mix"]], [dx1], [0], [0],
                                               n + "post_mix_norm_bwd", row_dtype=bf16)
    dmix_in = _mm(dmix, W["w_out"], "nt", 512, 512, f32, n + "out_proj_dx")
    g["w_out"] = _mm(sv["mix_in"], dmix, "tn", 512, 512, f32, n + "out_proj_dw")

    (ddn_o, dz), (g["dn_norm_w"],) = _rows_vjp(
        _dn_post_fn, [(sv["dn_o"], DK, 0), (sv["proj"], DK, 3072 // DK)], [W["dn_norm_w"]], [(dmix_in, DK, ATTN_W // DK)],
        [0, 1], [0], n + "dn_post_bwd", ncol=N_HEADS_D)
    dpre = _dn_scan_bwd(sv["dn_pre"], sv["dn_states"], ddn_o, n + "dn_scan_bwd")
    dq, dk, dv, dba, g["dn_a_log"], g["dn_dt_bias"] = _dn_prep_bwd(
        sv["qkv"], sv["proj"], W["dn_a_log"], W["dn_dt_bias"], dpre, n + "dn_prep_bwd")
    dqkv = jnp.concatenate([dq, dk, dv], axis=1)
    (dqkv0,), (g["dn_conv_w"],), _ = _colconv_bwd([(sv["proj"], 3)], [(W["dn_conv_w"], 0)], None, 4, _silu_fn,
                                                 [dqkv], 3, 512, n + "dn_conv_bwd")

    dqr, dkr, dav = _attn_bwd(sv["qr"], sv["kr"], sv["proj"], dmix_in, sv["attn_out"], sv["lse"], n + "attn_bwd")
    daq, dak = _rows(_rope_bwd_fn, [dqr, dkr, cos, sgn_sin], [], [(ATTN_W, f32)] * 2, n + "rope_bwd")
    dproj = jnp.concatenate([daq, dak, dav, dqkv0, dz, dba, jnp.zeros((S, PROJ_W - 3712), f32)], axis=1).astype(bf16)
    dh1 = _mm(dproj, W["w_in"], "nt", 512, 512, f32, n + "in_proj_dx")
    g["w_in"] = _mm(sv["h1"], dproj, "tn", 512, 768, f32, n + "in_proj_dw")
    (dx,), (g["norm_pre_mix"],) = _rows_vjp(_rms_fn, [sv["x"]], [W["norm_pre_mix"]], [dh1], [0], [0],
                                            n + "pre_mix_norm_bwd", adds={0: dx1})
    return dx, g


def _local_step(x, target, layers):
    cos, sgn_sin = _rope_tables(x.shape[0])
    saved = []
    for l, W in enumerate(layers):
        x, sv = _layer_fwd(x, W, cos, sgn_sin, l)
        saved.append(sv)
    loss, dx = _loss_head(x, target, "loss_head")
    grads = [None] * len(layers)
    for l in reversed(range(len(layers))):
        dx, grads[l] = _layer_bwd(dx, saved[l], layers[l], cos, sgn_sin, l)
    return loss, dx, grads


def _pos():
    x, y, c = lax.axis_index("x"), lax.axis_index("y"), lax.axis_index("c")
    return x, y, c, [(1 - x, y), (x, 1 - y), (1 - x, 1 - y)]


def _rcopy(src, dst, send_sem, recv_sem, dev):
    return pltpu.make_async_remote_copy(src_ref=src, dst_ref=dst, send_sem=send_sem, recv_sem=recv_sem,
                                        device_id=dev, device_id_type=MESH)


def _half_rows(ref, h, which, axis):
    if h is None:
        return ref
    rows = pl.ds(pl.multiple_of(which * h, 16), h)
    return ref.at[:, rows, :] if axis == 1 else ref.at[rows, :]


def _dma_sems(*counts):
    return [pltpu.SemaphoreType.DMA((k,)) for k in counts]


def _all_gather(arrs, halves, name):
    n = len(arrs)

    def body(*refs):
        ins, outs = refs[:n], refs[n:2 * n]
        send1, recv1, send2, recv2 = refs[2 * n:]
        x, y, c, chips = _pos()
        me, sib, s_me = (x, y, c), (x, y, 1 - c), 2 * x + y
        sends = []
        for i in range(n):
            for j, chip in enumerate(chips):
                cp = _rcopy(_half_rows(ins[i], halves[i], c, 1), _half_rows(outs[i].at[s_me], halves[i], c, 1),
                            send1.at[3 * i + j], recv1.at[3 * i + j], (*chip, c))
                cp.start()
                sends.append(cp)
        for i in range(n):
            for j, (px, py) in enumerate(chips):
                k = 3 * i + j
                landed = _half_rows(outs[i].at[2 * px + py], halves[i], c, 1)
                _rcopy(landed, landed, send1.at[k], recv1.at[k], me).wait_recv()
                if halves[i] is not None:
                    cp = _rcopy(landed, landed, send2.at[k], recv2.at[k], sib)
                    cp.start()
                    sends.append(cp)
        for i in range(n):
            if halves[i] is None:
                continue
            for j, (px, py) in enumerate(chips):
                k = 3 * i + j
                other = _half_rows(outs[i].at[2 * px + py], halves[i], 1 - c, 1)
                _rcopy(other, other, send2.at[k], recv2.at[k], me).wait_recv()
        for cp in sends:
            cp.wait_send()

    return pl.pallas_call(
        body, in_specs=[ANY] * n, out_specs=[ANY] * n,
        out_shape=[SDS((4,) + a.shape, a.dtype) for a in arrs],
        scratch_shapes=_dma_sems(3 * n, 3 * n, 3 * n, 3 * n), name=name)(*arrs)


HBM = pl.BlockSpec(memory_space=pltpu.HBM)
SEM = pl.BlockSpec(memory_space=pltpu.SEMAPHORE)
_EFFECT = pltpu.SideEffectType.DATAFLOW_SIDE_EFFECTING


def _in_hbm(a):
    return pltpu.with_memory_space_constraint(a, pltpu.HBM)


def _split_copy(srcs, land_shapes, plan, per, after, name):
    n = len(srcs)
    k = per * n

    def body(*refs):
        ins, lands, token = refs[:n], refs[n:2 * n], refs[-1]
        send, recv = refs[2 * n + 1], refs[2 * n + 2]
        for i, (src, dst, dev, _) in enumerate(plan(ins, lands)):
            _rcopy(src, dst, send.at[i], recv.at[i], dev).start()
        token[...] = jnp.zeros_like(token)

    lands = [_in_hbm(lax.empty(s.shape, s.dtype)) for s in land_shapes]
    return pl.pallas_call(
        body, name=name,
        out_shape=(pltpu.SemaphoreType.DMA((k,)), pltpu.SemaphoreType.DMA((k,)),
                   *[pltpu.HBM(a.shape, a.dtype) for a in srcs], *[pltpu.HBM(s.shape, s.dtype) for s in land_shapes],
                   SDS((8, 128), f32)),
        in_specs=[HBM] * (2 * n) + [ANY], out_specs=(SEM, SEM, *[HBM] * (2 * n), pl.BlockSpec(memory_space=pltpu.VMEM)),
        input_output_aliases={i: 2 + i for i in range(2 * n)},
        compiler_params=pltpu.CompilerParams(has_side_effects=_EFFECT))(*[_in_hbm(a) for a in srcs], *lands, after)


def _split_wait(started, n, plan, after, name):
    send, recv = started[0], started[1]
    thru = started[2:2 + 2 * n]

    def body(*refs):
        ins, lands = refs[:n], refs[n:2 * n]
        send_ref, recv_ref = refs[2 * n], refs[2 * n + 1]
        for i, (src, _, dev, mine) in enumerate(plan(ins, lands)):
            cp = _rcopy(src, mine, send_ref.at[i], recv_ref.at[i], dev)
            cp.wait_send()
            cp.wait_recv()

    res = pl.pallas_call(
        body, name=name, out_shape=tuple(pltpu.HBM(a.shape, a.dtype) for a in thru),
        in_specs=[HBM] * (2 * n) + [SEM, SEM, ANY], out_specs=tuple([HBM] * (2 * n)),
        input_output_aliases={i: i for i in range(2 * n)},
        compiler_params=pltpu.CompilerParams(has_side_effects=_EFFECT))(*thru, send, recv, after)
    return res[:n], res[n:]


def _gather_plan(halves):
    def plan(ins, lands):
        x, y, c, chips = _pos()
        out = []
        for i in range(len(ins)):
            for px, py in chips:
                out.append((_half_rows(ins[i], halves[i], c, 1), _half_rows(lands[i].at[2 * x + y], halves[i], c, 1),
                            (px, py, c), _half_rows(lands[i].at[2 * px + py], halves[i], c, 1)))
        return out
    return plan


def _scatter_plan(ins, lands):
    x, y, c, chips = _pos()
    out = []
    for i in range(len(ins)):
        for j, (px, py) in enumerate(chips):
            out.append((ins[i].at[2 * px + py], lands[i].at[j], (px, py, c), lands[i].at[j]))
    return out


def _exchange_plan(ins, lands):
    x, y, c, _ = _pos()
    return [(_half_rows(g, g.shape[1] // 2, 1 - c, 1), land, (x, y, 1 - c), land) for g, land in zip(ins, lands)]


def _pass_to_sibling(lands, halves, name):
    n = len(lands)

    def body(*refs):
        outs = refs[n:2 * n]
        send, recv = refs[2 * n:]
        x, y, c, chips = _pos()
        sends = []
        for i in range(n):
            for j, (px, py) in enumerate(chips):
                landed = _half_rows(outs[i].at[2 * px + py], halves[i], c, 1)
                cp = _rcopy(landed, landed, send.at[3 * i + j], recv.at[3 * i + j], (x, y, 1 - c))
                cp.start()
                sends.append(cp)
        for i in range(n):
            for j, (px, py) in enumerate(chips):
                other = _half_rows(outs[i].at[2 * px + py], halves[i], 1 - c, 1)
                _rcopy(other, other, send.at[3 * i + j], recv.at[3 * i + j], (x, y, c)).wait_recv()
        for cp in sends:
            cp.wait_send()

    return pl.pallas_call(
        body, in_specs=[ANY] * n, out_specs=[ANY] * n, out_shape=[SDS(a.shape, a.dtype) for a in lands],
        input_output_aliases={k: k for k in range(n)}, scratch_shapes=_dma_sems(3 * n, 3 * n), name=name)(*lands)


def _exchange_halves(gs, name):
    n = len(gs)

    def body(*refs):
        ins, outs = refs[:n], refs[n:2 * n]
        send, recv = refs[2 * n:]
        x, y, c, _ = _pos()
        sends = []
        for k in range(n):
            cp = _rcopy(_half_rows(ins[k], gs[k].shape[1] // 2, 1 - c, 1), outs[k], send.at[k], recv.at[k], (x, y, 1 - c))
            cp.start()
            sends.append(cp)
        for k in range(n):
            _rcopy(outs[k], outs[k], send.at[k], recv.at[k], (x, y, c)).wait_recv()
        for cp in sends:
            cp.wait_send()

    return pl.pallas_call(
        body, in_specs=[ANY] * n, out_specs=[ANY] * n,
        out_shape=[SDS((4, g.shape[1] // 2, g.shape[2]), g.dtype) for g in gs],
        scratch_shapes=_dma_sems(n, n), name=name)(*gs)


def _scatter_partials(ps, name):
    n = len(ps)

    def body(*refs):
        ins, outs = refs[:n], refs[n:2 * n]
        send, recv = refs[2 * n:]
        x, y, c, chips = _pos()
        sends = []
        for k in range(n):
            for j, (px, py) in enumerate(chips):
                cp = _rcopy(ins[k].at[2 * px + py], outs[k].at[j], send.at[3 * k + j], recv.at[3 * k + j], (px, py, c))
                cp.start()
                sends.append(cp)
        for k in range(n):
            for j in range(3):
                _rcopy(outs[k].at[j], outs[k].at[j], send.at[3 * k + j], recv.at[3 * k + j], (x, y, c)).wait_recv()
        for cp in sends:
            cp.wait_send()

    return pl.pallas_call(
        body, in_specs=[ANY] * n, out_specs=[ANY] * n,
        out_shape=[SDS((3,) + p.shape[1:], p.dtype) for p in ps],
        scratch_shapes=_dma_sems(3 * n, 3 * n), name=name)(*ps)


def _join_halves(rs, name):
    n = len(rs)

    def body(*refs):
        outs = refs[n:2 * n]
        send, recv = refs[2 * n:]
        x, y, c, _ = _pos()
        sends = []
        for k in range(n):
            mine = _half_rows(outs[k], rs[k].shape[0] // 2, c, 0)
            cp = _rcopy(mine, mine, send.at[k], recv.at[k], (x, y, 1 - c))
            cp.start()
            sends.append(cp)
        for k in range(n):
            other = _half_rows(outs[k], rs[k].shape[0] // 2, 1 - c, 0)
            _rcopy(other, other, send.at[k], recv.at[k], (x, y, c)).wait_recv()
        for cp in sends:
            cp.wait_send()

    return pl.pallas_call(
        body, in_specs=[ANY] * n, out_specs=[ANY] * n, out_shape=[SDS(r.shape, r.dtype) for r in rs],
        input_output_aliases={k: k for k in range(n)}, scratch_shapes=_dma_sems(n, n), name=name)(*rs)


def _all_reduce_small(pack, name):
    R = pack.shape[0]

    def body(in_ref, out_ref, buf, send, recv):
        x, y, c, _ = _pos()
        me = 4 * x + 2 * y + c
        buf[me] = in_ref[...]
        sends = []
        for k in range(1, 8):
            peer = me ^ k
            cp = _rcopy(buf.at[me], buf.at[me], send.at[k - 1], recv.at[k - 1], ((peer >> 2) & 1, (peer >> 1) & 1, peer & 1))
            cp.start()
            sends.append(cp)
        for k in range(1, 8):
            _rcopy(buf.at[me ^ k], buf.at[me ^ k], send.at[k - 1], recv.at[k - 1], (x, y, c)).wait_recv()
        for cp in sends:
            cp.wait_send()
        acc = buf[0]
        for d in range(1, 8):
            acc = acc + buf[d]
        out_ref[...] = acc

    return pl.pallas_call(
        body, out_shape=SDS((R, 128), f32),
        in_specs=[pl.BlockSpec(memory_space=pltpu.VMEM)], out_specs=pl.BlockSpec(memory_space=pltpu.VMEM),
        scratch_shapes=[pltpu.VMEM((8, R, 128), f32)] + _dma_sems(7, 7), name=name)(pack)


def _add_sibling(g, recv, c_arr, tr, name):
    _, R, C = g.shape
    h = R // 2
    nrb = h // tr
    assert h % tr == 0

    def body(c_ref, g_ref, r_ref, o_ref):
        o_ref[...] = (g_ref[...] + r_ref[...]).astype(o_ref.dtype)

    spec = pl.BlockSpec((1, tr, C), lambda s, r, c_ref: (s, r, 0))
    grid_spec = pltpu.PrefetchScalarGridSpec(
        num_scalar_prefetch=1, grid=(4, nrb),
        in_specs=[pl.BlockSpec((1, tr, C), lambda s, r, c_ref: (s, c_ref[0] * nrb + r, 0)), spec], out_specs=spec)
    return pl.pallas_call(body, grid_spec=grid_spec, out_shape=SDS((4, h, C), bf16), name=name,
                          compiler_params=_params("parallel", "parallel"))(c_arr, g, recv)


def _add_chips(p, recv, sc_arr, tr, name):
    _, h, C = p.shape
    nrb = h // tr
    assert h % tr == 0

    def body(sc_ref, p_ref, r_ref, o_ref):
        o_ref[...] = (p_ref[0].astype(f32) + r_ref[0].astype(f32)) + (r_ref[1].astype(f32) + r_ref[2].astype(f32))

    grid_spec = pltpu.PrefetchScalarGridSpec(
        num_scalar_prefetch=1, grid=(nrb,),
        in_specs=[pl.BlockSpec((1, tr, C), lambda r, sc_ref: (sc_ref[0], r, 0)),
                  pl.BlockSpec((3, tr, C), lambda r, sc_ref: (0, r, 0))],
        out_specs=pl.BlockSpec((tr, C), lambda r, sc_ref: (sc_ref[1] * nrb + r, 0)))
    return pl.pallas_call(body, grid_spec=grid_spec, out_shape=SDS((2 * h, C), f32), name=name,
                          compiler_params=_params("parallel"))(sc_arr, p, recv)


_BIG = (("w_in", 1024, 256), ("w_out", 256, 128), ("ffn_w_in", 1024, 256), ("ffn_w_out", 704, 352))
_SMALL = ("dn_conv_w", "ffn_conv_w", "ffn_conv_b", "norm_pre_mix", "norm_post_mix", "norm_pre_ffn", "norm_post_ffn",
          "dn_norm_w", "dn_a_log", "dn_dt_bias")
_WEIGHTS = ("w_in", "dn_conv_w", "dn_a_log", "dn_dt_bias", "dn_norm_w", "w_out", "ffn_w_in", "ffn_conv_w", "ffn_conv_b",
            "ffn_w_out", "norm_pre_mix", "norm_post_mix", "norm_pre_ffn", "norm_post_ffn")
_ADAM_ROWS = {"w_in": 256, "w_out": 256, "ffn_w_in": 128, "ffn_w_out": 176}


def _shard_major(name, g):
    if name == "w_in":
        return jnp.stack([g[:, 898 * s:898 * (s + 1)] for s in range(4)])
    if name == "ffn_w_in":
        return g
    return g.reshape(4, g.shape[0] // 4, g.shape[1])


def kernel(x, w_in, dn_conv_w, dn_a_log, dn_dt_bias, dn_norm_w, w_out, ffn_w_in, ffn_conv_w, ffn_conv_b, ffn_w_out, norm_pre_mix, norm_post_mix, norm_pre_ffn, norm_post_ffn, loss_target, m_w_in, m_dn_conv_w, m_dn_a_log, m_dn_dt_bias, m_dn_norm_w, m_w_out, m_ffn_w_in, m_ffn_conv_w, m_ffn_conv_b, m_ffn_w_out, m_norm_pre_mix, m_norm_post_mix, m_norm_pre_ffn, m_norm_post_ffn, v_w_in, v_dn_conv_w, v_dn_a_log, v_dn_dt_bias, v_dn_norm_w, v_w_out, v_ffn_w_in, v_ffn_conv_w, v_ffn_conv_b, v_ffn_w_out, v_norm_pre_mix, v_norm_post_mix, v_norm_pre_ffn, v_norm_post_ffn):
    w = dict(w_in=w_in, dn_conv_w=dn_conv_w, dn_a_log=dn_a_log, dn_dt_bias=dn_dt_bias, dn_norm_w=dn_norm_w, w_out=w_out,
             ffn_w_in=ffn_w_in, ffn_conv_w=ffn_conv_w, ffn_conv_b=ffn_conv_b, ffn_w_out=ffn_w_out, norm_pre_mix=norm_pre_mix,
             norm_post_mix=norm_post_mix, norm_pre_ffn=norm_pre_ffn, norm_post_ffn=norm_post_ffn)
    m = dict(w_in=m_w_in, dn_conv_w=m_dn_conv_w, dn_a_log=m_dn_a_log, dn_dt_bias=m_dn_dt_bias, dn_norm_w=m_dn_norm_w,
             w_out=m_w_out, ffn_w_in=m_ffn_w_in, ffn_conv_w=m_ffn_conv_w, ffn_conv_b=m_ffn_conv_b, ffn_w_out=m_ffn_w_out,
             norm_pre_mix=m_norm_pre_mix, norm_post_mix=m_norm_post_mix, norm_pre_ffn=m_norm_pre_ffn,
             norm_post_ffn=m_norm_post_ffn)
    v = dict(w_in=v_w_in, dn_conv_w=v_dn_conv_w, dn_a_log=v_dn_a_log, dn_dt_bias=v_dn_dt_bias, dn_norm_w=v_dn_norm_w,
             w_out=v_w_out, ffn_w_in=v_ffn_w_in, ffn_conv_w=v_ffn_conv_w, ffn_conv_b=v_ffn_conv_b, ffn_w_out=v_ffn_w_out,
             norm_pre_mix=v_norm_pre_mix, norm_post_mix=v_norm_post_mix, norm_pre_ffn=v_norm_pre_ffn,
             norm_post_ffn=v_norm_post_ffn)
    xi, yi, ci = lax.axis_index("x"), lax.axis_index("y"), lax.axis_index("c")
    s_me = 2 * xi + yi
    c_arr = jnp.reshape(ci, (1,)).astype(jnp.int32)
    sc_arr = jnp.stack([s_me, ci]).astype(jnp.int32)

    mats = [name for name, _, _ in _BIG]
    rest = mats[1:]
    half_of = {name: rows // 2 for name, rows, _ in _BIG}
    tiles = {name: tr for name, _, tr in _BIG}
    gathered_shape = lambda a: SDS((4,) + a.shape, a.dtype)

    own = {k: w[k].astype(bf16) for k in mats}
    got_in = _all_gather([own["w_in"][0:1], dn_conv_w, ffn_conv_w], [half_of["w_in"], None, None], "weights_gather_w_in0")
    plan0 = _gather_plan([half_of[k] for k in rest])
    src0 = [own[k][0:1] for k in rest]
    started0 = _split_copy(src0, [gathered_shape(a) for a in src0], plan0, 3, got_in[0], "weights_gather_l0_start")
    plan1 = _gather_plan([half_of[k] for k in mats])
    src1 = [own[k][1:2] for k in mats]
    started1 = _split_copy(src1, [gathered_shape(a) for a in src1], plan1, 3, started0[-1], "weights_gather_l1_start")

    def pick(mine, gathered):
        return [jnp.where(s_me == s, mine, gathered[s]) for s in range(4)]

    conv = {"dn_conv_w": jnp.concatenate(pick(dn_conv_w, got_in[1]), axis=-1),
            "ffn_conv_w": jnp.concatenate(pick(ffn_conv_w, got_in[2]), axis=-1)}
    lanes = lambda a: jnp.pad(a, ((0, 0), (0, 128 - a.shape[1])))
    vec = dict(dn_a_log=lanes(dn_a_log), dn_dt_bias=lanes(dn_dt_bias), dn_norm_w=dn_norm_w, ffn_conv_b=ffn_conv_b,
               norm_pre_mix=norm_pre_mix, norm_post_mix=norm_post_mix, norm_pre_ffn=norm_pre_ffn, norm_post_ffn=norm_post_ffn)

    def matrices(l, names, gathered):
        W = {}
        for k, a in zip(names, gathered):
            if k in ("w_out", "ffn_w_out"):
                rows_, cols = own[k].shape[1:]
                W[k] = lax.dynamic_update_slice(a[:, 0], own[k][l][None], (s_me, 0, 0)).reshape(4 * rows_, cols)
            else:
                cat = jnp.concatenate(pick(own[k][l], a[:, 0]), axis=-1)
                W[k] = jnp.pad(cat, ((0, 0), (0, PROJ_W - IN_COLS))) if k == "w_in" else cat
        return W

    def small_weights(l):
        return {**{k: a[l] for k, a in conv.items()}, **{k: a[l:l + 1] for k, a in vec.items()}}

    def late_l0(mix_in):
        _, landed = _split_wait(started0, len(rest), plan0, mix_in, "weights_gather_l0_wait")
        return matrices(0, rest, _pass_to_sibling(landed, [half_of[k] for k in rest], "weights_gather_l0_sibling"))

    cos, sgn_sin = _rope_tables(x.shape[1])
    W0 = {**small_weights(0), **matrices(0, ["w_in"], got_in[:1])}
    W0_first = dict(W0, norm_pre_mix=W0["norm_pre_mix"] + started1[-1][0, 0])
    x1, saved0 = _layer_fwd(x[0], W0_first, cos, sgn_sin, 0, late_weights=late_l0)
    _, landed1 = _split_wait(started1, len(mats), plan1, x1, "weights_gather_l1_wait")
    W1 = {**small_weights(1),
          **matrices(1, mats, _pass_to_sibling(landed1, [half_of[k] for k in mats], "weights_gather_l1_sibling"))}
    x2, saved1 = _layer_fwd(x1, W1, cos, sgn_sin, 1)
    loss_local, dy = _loss_head(x2, loss_target[0], "loss_head")
    loss = lax.psum(loss_local, ("x", "y", "c"))

    def shard_major(names, grads_l):
        return [_shard_major(name, grads_l[name]) for name in names]

    def add_siblings(l, names, gs, from_sib):
        return [_add_sibling(g, r, c_arr, tiles[name], f"add_sibling_{name}{l}") for g, r, name in zip(gs, from_sib, names)]

    def scatter_start(l, names, parts, after, tag):
        return _split_copy(parts, [SDS((3,) + p.shape[1:], p.dtype) for p in parts], _scatter_plan, 3, after,
                           f"grads_l{l}{tag}_scatter_start")

    def owner_sums(l, names, sent, after, tag):
        parts, recvd = _split_wait(sent, len(names), _scatter_plan, after, f"grads_l{l}{tag}_scatter_wait")
        return [_add_chips(p, r, sc_arr, tiles[name], f"add_chips_{name}{l}") for p, r, name in zip(parts, recvd, names)]

    dx1, grads1 = _layer_bwd(dy, saved1, W1, cos, sgn_sin, 1)
    gs1 = shard_major(mats, grads1)
    swap1 = _split_copy(gs1, [SDS((4, g.shape[1] // 2, g.shape[2]), g.dtype) for g in gs1], _exchange_plan, 1, dx1,
                        "grads_l1_sibling_start")
    ffn = ["ffn_w_in", "ffn_w_out"]
    launched = {}

    def after_ffn_l0(g_ffn, dx_mid):
        gs1_, from_sib1 = _split_wait(swap1, len(mats), _exchange_plan, dx_mid, "grads_l1_sibling_wait")
        launched["l1"] = scatter_start(1, mats, add_siblings(1, mats, gs1_, from_sib1), dx_mid, "")
        gs0 = shard_major(ffn, g_ffn)
        from_sib0 = _exchange_halves(gs0, "grads_l0_ffn_to_sibling")
        launched["l0_ffn"] = scatter_start(0, ffn, add_siblings(0, ffn, gs0, from_sib0), launched["l1"][-1], "_ffn")
        return launched["l0_ffn"][-1][0, 0]

    W0_last = dict(W0, **saved0["late"], norm_post_ffn=W0["norm_post_ffn"] + swap1[-1][0, 0])
    dx, grads0 = _layer_bwd(dx1, saved0, W0_last, cos, sgn_sin, 0, after_ffn=after_ffn_l0)
    mix = ["w_in", "w_out"]
    gs0 = shard_major(mix, grads0)
    part0 = add_siblings(0, mix, gs0, _exchange_halves(gs0, "grads_l0_mix_to_sibling"))
    recvd0 = _scatter_partials(part0, "grads_l0_mix_scatter")
    red = dict(zip([(0, k) for k in mix],
                   [_add_chips(p, r, sc_arr, tiles[k], f"add_chips_{k}0") for p, r, k in zip(part0, recvd0, mix)]))
    red.update(zip([(0, k) for k in ffn], owner_sums(0, ffn, launched["l0_ffn"], dx, "_ffn")))
    red.update(zip([(1, k) for k in mats], owner_sums(1, mats, launched["l1"], dx, "")))
    order = [(l, k) for l in range(2) for k in mats]
    joined = dict(zip(order, _join_halves([red[key] for key in order], "grads_join_halves")))
    g_out = {k: jnp.stack([joined[(0, k)], joined[(1, k)]]) for k in mats}
    grads = [grads0, grads1]

    small = {}
    for name in _SMALL:
        per_layer = [grads[l][name] for l in range(2)]
        if name in ("dn_a_log", "dn_dt_bias"):
            per_layer = [p[:, :N_HEADS_D] for p in per_layer]
        small[name] = jnp.stack(per_layer).reshape((2,) + (w[name].shape[1:] if name not in ("dn_conv_w", "ffn_conv_w")
                                                           else per_layer[0].shape))
    flat = jnp.concatenate([small[name].reshape(-1) for name in _SMALL])
    n_rows = -(-flat.shape[0] // 1024) * 8
    summed = _all_reduce_small(jnp.pad(flat, (0, n_rows * 128 - flat.shape[0])).reshape(n_rows, 128),
                               "small_grads_all_reduce").reshape(-1)
    off = 0
    for name in _SMALL:
        size = small[name].size
        g_out[name] = summed[off:off + size].reshape(small[name].shape)
        off += size
    g_out["dn_conv_w"] = lax.dynamic_slice_in_dim(g_out["dn_conv_w"], s_me * 384, 384, axis=2)
    g_out["ffn_conv_w"] = lax.dynamic_slice_in_dim(g_out["ffn_conv_w"], s_me * 1408, 1408, axis=2)

    deltas, new_m, new_v = {}, {}, {}
    for name in _WEIGHTS:
        shape = w[name].shape
        as3 = (lambda a: a) if len(shape) == 3 else (lambda a: a.reshape(shape[0], 1, shape[1]))
        tr = _ADAM_ROWS.get(name, as3(w[name]).shape[1])
        d_, m_, v_ = _adamw(as3(w[name]), as3(g_out[name]), as3(m[name]), as3(v[name]), tr, f"adamw_{name}")
        deltas[name], new_m[name], new_v[name] = d_.reshape(shape), m_.reshape(shape), v_.reshape(shape)

    return (loss, dx[None], *[g_out[k] for k in _WEIGHTS], *[deltas[k] for k in _WEIGHTS],
            *[new_m[k] for k in _WEIGHTS], *[new_v[k] for k in _WEIGHTS])
```

```python
import jax
import jax.numpy as jnp
from jax import lax
from jax.experimental import pallas as pl
from jax.experimental.pallas import tpu as pltpu

f32, bf16 = jnp.float32, jnp.bfloat16
SDS = jax.ShapeDtypeStruct
HI = lax.Precision.HIGH
MESH = pl.DeviceIdType.MESH
ANY = pl.BlockSpec(memory_space=pl.ANY)

D_MODEL = 1024
N_HEADS_A, HEAD_DIM = 8, 64
ATTN_W = 512
N_HEADS_D, DK = 4, 128
CHUNK = 64
D_FF = 2816
IN_COLS = 3592
PROJ_W = 3840
BRANCHES = ((1, 16), (4, 4), (16, 1))
EPS = 1e-6
NEG = -1e30
ROW_TILE = 256
VMEM_LIMIT = 56 * 1024 * 1024

ADAM_LR, ADAM_B1, ADAM_B2, ADAM_EPS, ADAM_WD, ADAM_STEP = 0.001, 0.9, 0.999, 1e-08, 0.01, 10


def _params(*sem):
    return pltpu.CompilerParams(dimension_semantics=sem, vmem_limit_bytes=VMEM_LIMIT)


def _mm(a, b, mode, tm, tn, out_dtype, name, column_shards=False):
    if mode == "nn":
        (M, K), N = a.shape, b.shape[1]
        dims = (((1,), (0,)), ((), ()))
        a_spec = pl.BlockSpec((tm, K), lambda i, j: (i, 0))
        b_spec = pl.BlockSpec((K, tn), lambda i, j: (0, j))
    elif mode == "nt":
        (M, K), N = a.shape, b.shape[0]
        dims = (((1,), (1,)), ((), ()))
        a_spec = pl.BlockSpec((tm, K), lambda i, j: (i, 0))
        b_spec = pl.BlockSpec((tn, K), lambda i, j: (j, 0))
    else:
        (K, M), N = a.shape, b.shape[1]
        dims = (((0,), (0,)), ((), ()))
        a_spec = pl.BlockSpec((K, tm), lambda i, j: (0, i))
        b_spec = pl.BlockSpec((K, tn), lambda i, j: (0, j))
    assert M % tm == 0 and N % tn == 0, (name, M, N, tm, tn)

    def body(a_ref, b_ref, o_ref):
        o_ref[...] = lax.dot_general(a_ref[...].astype(bf16), b_ref[...].astype(bf16), dims,
                                     preferred_element_type=f32).astype(o_ref.dtype)

    if column_shards:
        out_spec, out_shape = pl.BlockSpec((None, tm, tn), lambda i, j: (j, i, 0)), SDS((N // tn, M, tn), out_dtype)
    else:
        out_spec, out_shape = pl.BlockSpec((tm, tn), lambda i, j: (i, j)), SDS((M, N), out_dtype)
    return pl.pallas_call(body, grid=(M // tm, N // tn), in_specs=[a_spec, b_spec], out_specs=out_spec,
                          out_shape=out_shape, name=name, compiler_params=_params("parallel", "arbitrary"))(a, b)


def _row_spec(r, tm):
    if isinstance(r, tuple):
        arr, width, cb = r
        return arr, pl.BlockSpec((tm, width), lambda i, j, cb=cb: (i, cb + j))
    return r, pl.BlockSpec((tm, r.shape[1]), lambda i, j: (i, j))


def _full_spec(p):
    return pl.BlockSpec(p.shape, lambda i, j: (0,) * p.ndim)


def _rows(fn, rows, params, outs, name, tm=ROW_TILE, ncol=1):
    arrs, specs = zip(*[_row_spec(r, tm) for r in rows])
    S = arrs[0].shape[0]
    nr, npar = len(rows), len(params)

    def body(*refs):
        vals = fn(*[r[...].astype(f32) for r in refs[:nr]], *[p[...] for p in refs[nr:nr + npar]])
        for o_ref, v in zip(refs[nr + npar:], vals):
            o_ref[...] = v.astype(o_ref.dtype)

    return pl.pallas_call(
        body, grid=(S // tm, ncol), in_specs=list(specs) + [_full_spec(p) for p in params],
        out_specs=[pl.BlockSpec((tm, w), lambda i, j: (i, j)) for w, _ in outs],
        out_shape=[SDS((S, w * ncol), dt) for w, dt in outs], name=name,
        compiler_params=_params("parallel", "parallel"))(*arrs, *params)


def _rows_vjp(fn, rows, params, cts, wrt_rows, wrt_params, name, adds=None, tm=ROW_TILE, ncol=1, row_dtype=f32):
    adds = adds or {}
    arrs, specs = zip(*[_row_spec(r, tm) for r in rows])
    carrs, cspecs = zip(*[_row_spec(c, tm) for c in cts])
    add_keys = sorted(adds)
    aarrs = [adds[k] for k in add_keys]
    S = arrs[0].shape[0]
    nr, npar, nc, na = len(rows), len(params), len(cts), len(aarrs)
    widths = [specs[k].block_shape[1] for k in wrt_rows]

    def body(*refs):
        first = jnp.logical_and(pl.program_id(0) == 0, pl.program_id(1) == 0)
        rv = [r[...].astype(f32) for r in refs[:nr]]
        pv = [p[...] for p in refs[nr:nr + npar]]
        cv = tuple(c[...].astype(f32) for c in refs[nr + npar:nr + npar + nc])
        av = dict(zip(add_keys, refs[nr + npar + nc:nr + npar + nc + na]))
        o = refs[nr + npar + nc + na:]
        _, vjp = jax.vjp(fn, *rv, *pv)
        g = vjp(cv)
        for n, k in enumerate(wrt_rows):
            val = g[k]
            if k in av:
                val = val + av[k][...]
            o[n][...] = val.astype(o[n].dtype)
        for n, k in enumerate(wrt_params):
            ref = o[len(wrt_rows) + n]

            @pl.when(first)
            def _(ref=ref):
                ref[...] = jnp.zeros_like(ref)

            ref[...] += g[nr + k]

    res = pl.pallas_call(
        body, grid=(S // tm, ncol),
        in_specs=list(specs) + [_full_spec(p) for p in params] + list(cspecs)
        + [pl.BlockSpec((tm, a.shape[1] // ncol), lambda i, j: (i, j)) for a in aarrs],
        out_specs=[pl.BlockSpec((tm, w), lambda i, j: (i, j)) for w in widths] + [_full_spec(params[k]) for k in wrt_params],
        out_shape=[SDS((S, w * ncol), row_dtype) for w in widths] + [SDS(params[k].shape, f32) for k in wrt_params],
        name=name, compiler_params=_params("arbitrary", "arbitrary"))(*arrs, *params, *carrs, *aarrs)
    return res[:len(wrt_rows)], res[len(wrt_rows):]


def _rms(x, w):
    return x * lax.rsqrt(jnp.mean(x * x, axis=-1, keepdims=True) + EPS) * w


def _rms_fn(x, w):
    return (_rms(x, w),)


def _res_rms_fn(f, res, w):
    return (res + _rms(f, w),)


def _swap_halves(x):
    lane = lax.broadcasted_iota(jnp.int32, x.shape, 1)
    first = (lane % HEAD_DIM) < (HEAD_DIM // 2)
    n = x.shape[1]
    return jnp.where(first, pltpu.roll(x, n - HEAD_DIM // 2, 1), pltpu.roll(x, HEAD_DIM // 2, 1))


def _rope_fwd_fn(q, k, cos, sgn_sin):
    scale = HEAD_DIM ** -0.5
    return ((q * cos + _swap_halves(q) * sgn_sin) * scale, k * cos + _swap_halves(k) * sgn_sin)


def _rope_bwd_fn(dq, dk, cos, sgn_sin):
    dq = dq * (HEAD_DIM ** -0.5)
    return (dq * cos + _swap_halves(dq * sgn_sin), dk * cos + _swap_halves(dk * sgn_sin))


def _nt(a, b):
    return lax.dot_general(a, b, (((1,), (1,)), ((), ())), preferred_element_type=f32)


def _tn(a, b):
    return lax.dot_general(a, b, (((0,), (0,)), ((), ())), preferred_element_type=f32)


def _band_rows(j, d, nb):
    r, i = j // nb, j % nb
    if d == 1:
        cur = pl.ds(pl.multiple_of(i * 128, 128), 128)
        prev = pl.ds(pl.multiple_of(jnp.maximum(i - 1, 0) * 128, 128), 128)
    else:
        cur = pl.ds(i * (128 * d) + r, 128, stride=d)
        prev = pl.ds(jnp.maximum(i - 1, 0) * (128 * d) + r, 128, stride=d)
    return cur, prev, (i == 0).astype(jnp.int32)


def _band_bias(bias_ref):
    a = lax.broadcasted_iota(jnp.int32, (256, 256), 0) % 128
    c = lax.broadcasted_iota(jnp.int32, (256, 256), 1)
    own = jnp.logical_and(c < 128, c <= a)
    before = jnp.logical_and(c >= 128, c - 128 >= a)
    bias_ref[0] = jnp.where(jnp.logical_or(own, before), 0.0, NEG)
    bias_ref[1] = jnp.where(own, 0.0, NEG)


def _stack_heads(x, head_a):
    return jnp.concatenate([jnp.where(head_a, x, 0.0), jnp.where(head_a, 0.0, x)], axis=0)


def _unstack_heads(x2, head_a):
    return jnp.where(head_a, x2[:128], x2[128:])


def _attn_fwd(qr, kr, proj, name):
    S = qr.shape[0]
    nblk = S // 128

    def body(q_ref, k_ref, v_ref, out_ref, lse_ref, bias_ref, *scr):
        head_a = lax.broadcasted_iota(jnp.int32, (1, 128), 1) < HEAD_DIM
        _band_bias(bias_ref)
        for b, (d, nb) in enumerate(BRANCHES):
            ob_ref, lb_ref = scr[2 * b], scr[2 * b + 1]

            def blk(j, carry, d=d, nb=nb, ob_ref=ob_ref, lb_ref=lb_ref):
                cur, prev, first = _band_rows(j, d, nb)
                q2 = _stack_heads(q_ref[cur, :], head_a).astype(bf16)
                if nb == 1:
                    k2, v2, bias = k_ref[cur, :].astype(bf16), v_ref[cur, :].astype(bf16), bias_ref[1][:, :128]
                else:
                    k2 = jnp.concatenate([k_ref[cur, :], k_ref[prev, :]], axis=0).astype(bf16)
                    v2 = jnp.concatenate([v_ref[cur, :], v_ref[prev, :]], axis=0).astype(bf16)
                    bias = bias_ref[first]
                s = _nt(q2, k2) + bias
                mx = jnp.max(s, axis=1, keepdims=True)
                p = jnp.exp(s - mx)
                l = jnp.sum(p, axis=1, keepdims=True)
                o = jnp.dot(p.astype(bf16), v2, preferred_element_type=f32) / l
                ob_ref[cur, :] = _unstack_heads(o, head_a)
                lb_ref[cur, :] = _unstack_heads(jnp.broadcast_to(mx + jnp.log(l), (256, 128)), head_a)
                return carry

            lax.fori_loop(0, nblk, blk, 0, unroll=2)
        l0, l1, l2 = scr[1][...], scr[3][...], scr[5][...]
        mx = jnp.maximum(jnp.maximum(l0, l1), l2)
        e0, e1, e2 = jnp.exp(l0 - mx), jnp.exp(l1 - mx), jnp.exp(l2 - mx)
        den = e0 + e1 + e2
        out_ref[...] = ((e0 * scr[0][...] + e1 * scr[2][...] + e2 * scr[4][...]) / den).astype(out_ref.dtype)
        lse_ref[...] = mx + jnp.log(den)

    pair = pl.BlockSpec((S, 128), lambda h: (0, h))
    return pl.pallas_call(
        body, grid=(N_HEADS_A // 2,),
        in_specs=[pair, pair, pl.BlockSpec((S, 128), lambda h: (0, 2 * ATTN_W // 128 + h))], out_specs=[pair, pair],
        out_shape=[SDS((S, ATTN_W), bf16), SDS((S, ATTN_W), f32)],
        scratch_shapes=[pltpu.VMEM((2, 256, 256), f32)] + [pltpu.VMEM((S, 128), f32)] * 6,
        name=name, compiler_params=_params("parallel"))(qr, kr, proj)


def _attn_bwd(qr, kr, proj, dmix_in, out, lse, name):
    S = qr.shape[0]
    nblk = S // 128

    def body(q_ref, k_ref, v_ref, do_ref, out_ref, lse_ref, dq_ref, dk_ref, dv_ref, bias_ref, t_ref):
        head_a = lax.broadcasted_iota(jnp.int32, (1, 128), 1) < HEAD_DIM
        _band_bias(bias_ref)
        x = do_ref[...] * out_ref[...].astype(f32)
        t_ref[...] = jnp.where(head_a, jnp.sum(jnp.where(head_a, x, 0.0), axis=1, keepdims=True),
                               jnp.sum(jnp.where(head_a, 0.0, x), axis=1, keepdims=True))
        dq_ref[...] = jnp.zeros_like(dq_ref)
        dk_ref[...] = jnp.zeros_like(dk_ref)
        dv_ref[...] = jnp.zeros_like(dv_ref)
        for d, nb in BRANCHES:
            def blk(j, carry, d=d, nb=nb):
                cur, prev, first = _band_rows(j, d, nb)
                q2 = _stack_heads(q_ref[cur, :], head_a).astype(bf16)
                do2 = _stack_heads(do_ref[cur, :], head_a).astype(bf16)
                t, lse_b = t_ref[cur, :], lse_ref[cur, :]
                t2 = jnp.concatenate([t[:, :1], t[:, HEAD_DIM:HEAD_DIM + 1]], axis=0)
                lse2 = jnp.concatenate([lse_b[:, :1], lse_b[:, HEAD_DIM:HEAD_DIM + 1]], axis=0)
                if nb == 1:
                    k2, v2, bias = k_ref[cur, :].astype(bf16), v_ref[cur, :].astype(bf16), bias_ref[1][:, :128]
                else:
                    k2 = jnp.concatenate([k_ref[cur, :], k_ref[prev, :]], axis=0).astype(bf16)
                    v2 = jnp.concatenate([v_ref[cur, :], v_ref[prev, :]], axis=0).astype(bf16)
                    bias = bias_ref[first]
                p = jnp.exp(_nt(q2, k2) + bias - lse2)
                ds = (p * (_nt(do2, v2) - t2)).astype(bf16)
                dq_ref[cur, :] += _unstack_heads(jnp.dot(ds, k2, preferred_element_type=f32), head_a)
                dk2, dv2 = _tn(ds, q2), _tn(p.astype(bf16), do2)
                dk_ref[cur, :] += dk2[:128]
                dv_ref[cur, :] += dv2[:128]
                if nb != 1:
                    dk_ref[prev, :] += dk2[128:]
                    dv_ref[prev, :] += dv2[128:]
                return carry

            lax.fori_loop(0, nblk, blk, 0, unroll=2)

    pair = pl.BlockSpec((S, 128), lambda h: (0, h))
    return pl.pallas_call(
        body, grid=(N_HEADS_A // 2,),
        in_specs=[pair, pair, pl.BlockSpec((S, 128), lambda h: (0, 2 * ATTN_W // 128 + h)), pair, pair, pair],
        out_specs=[pair] * 3, out_shape=[SDS((S, ATTN_W), f32)] * 3,
        scratch_shapes=[pltpu.VMEM((2, 256, 256), f32), pltpu.VMEM((S, 128), f32)],
        name=name, compiler_params=_params("parallel"))(qr, kr, proj, dmix_in, out, lse)


def _conv_val(x, w, K, rows):
    acc = x * w[K - 1:K, :]
    for s in range(1, K):
        acc = acc + jnp.where(rows >= s, pltpu.roll(x, s, 0), 0.0) * w[K - 1 - s:K - s, :]
    return acc


def _colconv_fwd(xs, ws, bs, K, fn, nblk, tc, outs, name):
    S = xs[0][0].shape[0]
    n = len(xs)
    has_b = bs is not None

    def body(*refs):
        rows = lax.broadcasted_iota(jnp.int32, (S, tc), 0)
        cs = []
        for k in range(n):
            c = _conv_val(refs[k][...].astype(f32), refs[n + k][...], K, rows)
            if has_b:
                c = c + refs[2 * n + k][...]
            cs.append(c)
        for o_ref, val in zip(refs[(3 if has_b else 2) * n:], fn(*cs)):
            o_ref[...] = val.astype(o_ref.dtype)

    def cspec(rows_, cb0):
        return pl.BlockSpec((rows_, tc), lambda j, cb0=cb0: (0, cb0 + j))

    in_specs = [cspec(S, cb) for _, cb in xs] + [cspec(K, cb) for _, cb in ws]
    args = [a for a, _ in xs] + [a for a, _ in ws]
    if has_b:
        in_specs += [cspec(1, cb) for _, cb in bs]
        args += [a for a, _ in bs]
    return pl.pallas_call(
        body, grid=(nblk,), in_specs=in_specs, out_specs=[cspec(S, 0) for _ in outs],
        out_shape=[SDS((S, nblk * tc), dt) for dt in outs], name=name, compiler_params=_params("parallel"))(*args)


def _colconv_bwd(xs, ws, bs, K, fn, douts, nblk, tc, name, dx_dtype=f32):
    S = xs[0][0].shape[0]
    n, nd = len(xs), len(douts)
    has_b = bs is not None
    nin = (3 if has_b else 2) * n

    def body(*refs):
        rows = lax.broadcasted_iota(jnp.int32, (S, tc), 0)
        x = [refs[k][...].astype(f32) for k in range(n)]
        w = [refs[n + k][...] for k in range(n)]
        cs = []
        for k in range(n):
            c = _conv_val(x[k], w[k], K, rows)
            if has_b:
                c = c + refs[2 * n + k][...]
            cs.append(c)
        _, vjp = jax.vjp(fn, *cs)
        dcs = vjp(tuple(r[...].astype(f32) for r in refs[nin:nin + nd]))
        o = refs[nin + nd:]
        for k in range(n):
            dc = dcs[k]
            dx = dc * w[k][K - 1:K, :]
            o[n + k][K - 1:K, :] = jnp.sum(dc * x[k], axis=0, keepdims=True)
            for s in range(1, K):
                dx = dx + jnp.where(rows < S - s, pltpu.roll(dc, S - s, 0), 0.0) * w[k][K - 1 - s:K - s, :]
                xsh = jnp.where(rows >= s, pltpu.roll(x[k], s, 0), 0.0)
                o[n + k][K - 1 - s:K - s, :] = jnp.sum(dc * xsh, axis=0, keepdims=True)
            o[k][...] = dx.astype(o[k].dtype)
            if has_b:
                o[2 * n + k][...] = jnp.sum(dc, axis=0, keepdims=True)

    def cspec(rows_, cb0):
        return pl.BlockSpec((rows_, tc), lambda j, cb0=cb0: (0, cb0 + j))

    in_specs = [cspec(S, cb) for _, cb in xs] + [cspec(K, cb) for _, cb in ws]
    args = [a for a, _ in xs] + [a for a, _ in ws]
    if has_b:
        in_specs += [cspec(1, cb) for _, cb in bs]
        args += [a for a, _ in bs]
    in_specs += [cspec(S, 0) for _ in douts]
    args += list(douts)
    W = nblk * tc
    out_specs = [cspec(S, 0)] * n + [cspec(K, 0)] * n + ([cspec(1, 0)] * n if has_b else [])
    out_shape = [SDS((S, W), dx_dtype)] * n + [SDS((K, W), f32)] * n + ([SDS((1, W), f32)] * n if has_b else [])
    res = pl.pallas_call(body, grid=(nblk,), in_specs=in_specs, out_specs=out_specs, out_shape=out_shape,
                         name=name, compiler_params=_params("parallel"))(*args)
    return res[:n], res[n:2 * n], res[2 * n:]


def _silu_fn(c):
    return (c * jax.nn.sigmoid(c),)


_GELU_C, _GELU_A = 0.7978845608028654, 0.044715


@jax.custom_vjp
def _geglu(gate, up):
    return 0.5 * gate * (1.0 + jnp.tanh(_GELU_C * (gate + _GELU_A * gate * gate * gate))) * up


def _geglu_vjp_fwd(gate, up):
    return _geglu(gate, up), (gate, up)


def _geglu_vjp_bwd(res, d):
    gate, up = res
    g2 = gate * gate
    t = jnp.tanh(_GELU_C * gate * (1.0 + _GELU_A * g2))
    h = 0.5 * (1.0 + t)
    dgelu = h + (0.5 * _GELU_C) * gate * (1.0 - t * t) * (1.0 + (3.0 * _GELU_A) * g2)
    return d * up * dgelu, d * (gate * h)


_geglu.defvjp(_geglu_vjp_fwd, _geglu_vjp_bwd)


def _geglu_fn(gate, up):
    return (_geglu(gate, up),)


def _softplus(x):
    u = jnp.exp(jnp.minimum(x, 20.0))
    small = u * (1.0 - 0.5 * u)
    return jnp.where(x > 20.0, x, jnp.where(u < 1e-4, small, jnp.log(1.0 + u)))


def _bmm(a, b, precision=None):
    return lax.dot_general(a, b, (((2,), (1,)), ((0,), (0,))), precision=precision, preferred_element_type=f32)


def _bnt(a, b):
    return lax.dot_general(a, b, (((2,), (2,)), ((0,), (0,))), preferred_element_type=f32)


def _btn(a, b):
    return lax.dot_general(a, b, (((1,), (1,)), ((0,), (0,))), preferred_element_type=f32)


def _unit_lower_inverse(A):
    n = A.shape[-1]
    eye = (lax.broadcasted_iota(jnp.int32, (1, n, n), 1) == lax.broadcasted_iota(jnp.int32, (1, n, n), 2)).astype(f32)
    P = -A
    T = eye + P
    for _ in range(5):
        P = _bmm(P, P, HI)
        T = T + _bmm(T, P, HI)
    return T


def _dn_prep_fn(q, k, v, ba, alog, dtb, h):
    G, C = q.shape[0], CHUNK
    lane = lax.broadcasted_iota(jnp.int32, (1, 1, 128), 2)

    def sel(arr, idx):
        return jnp.sum(jnp.where(lane == idx, arr, 0.0), axis=-1, keepdims=True)

    beta = jax.nn.sigmoid(sel(ba, h))
    g = -jnp.exp(sel(alog[None], h)) * _softplus(sel(ba, N_HEADS_D + h) + sel(dtb[None], h))
    qn = q * lax.rsqrt(jnp.sum(q * q, axis=-1, keepdims=True) + EPS) * (DK ** -0.5)
    kn = k * lax.rsqrt(jnp.sum(k * k, axis=-1, keepdims=True) + EPS)
    ii = lax.broadcasted_iota(jnp.int32, (1, C, C), 1)
    jj = lax.broadcasted_iota(jnp.int32, (1, C, C), 2)
    tril, strict = ii >= jj, ii > jj
    gsq = jnp.broadcast_to(g, (G, C, C))
    gcol = _bmm(jnp.broadcast_to(tril.astype(f32), (G, C, C)), gsq, HI)
    grow = _bmm(jnp.ones((G, C, C), f32), jnp.where(ii <= jj, gsq, 0.0), HI)
    decay = jnp.exp(jnp.where(tril, gcol - grow, NEG))
    gc = gcol[:, :, :1]
    glast = gcol[:, C - 1:C, :1]
    kb = kn * beta
    A = jnp.where(strict, _bnt(kb.astype(bf16), kn.astype(bf16)) * decay, 0.0)
    T = _unit_lower_inverse(A).astype(bf16)
    u = _bmm(T, (v * beta).astype(bf16))
    w = _bmm(T, (kb * jnp.exp(gc)).astype(bf16))
    qk = _bnt(qn.astype(bf16), kn.astype(bf16)) * decay
    qd = qn * jnp.exp(gc)
    kd = kn * jnp.exp(glast - gc)
    return u, w, qk, qd, kd, jnp.broadcast_to(jnp.exp(glast), (G, C, DK))


def _dn_scan_fn(u, w, qk, qd, kd, eg, St):
    b = lambda a: a.astype(bf16)
    vnew = u - _bmm(b(w), b(St))
    o = _bmm(b(qd), b(St)) + _bmm(b(qk), b(vnew))
    return o, St * eg[:, :1, :] + _btn(b(kd), b(vnew))


def _dn_post_fn(o, z, nw):
    return (_rms(o, nw) * (z * jax.nn.sigmoid(z)),)


DN_GROUP = 8


def _dn_prep_specs(S, rows):
    def col(first):
        return pl.BlockSpec((rows, DK), lambda i, h, first=first: (i, first // DK + h))

    par = pl.BlockSpec((1, 128), lambda i, h: (0, 0))
    return [col(0), col(N_HEADS_D * DK), col(2 * N_HEADS_D * DK),
            pl.BlockSpec((rows, 128), lambda i, h: (i, 3584 // 128)), par, par]


def _dn_prep(qkv, proj, alog, dtb, name):
    S = qkv.shape[0]
    G = DN_GROUP
    rows = G * CHUNK

    def body(q_ref, k_ref, v_ref, ba_ref, al_ref, dt_ref, u_ref, w_ref, qk_ref, qd_ref, kd_ref, eg_ref):
        h = pl.program_id(1)
        r3 = lambda ref: ref[...].reshape(G, CHUNK, 128)
        u, w, qk, qd, kd, eg = _dn_prep_fn(r3(q_ref), r3(k_ref), r3(v_ref), r3(ba_ref), al_ref[...], dt_ref[...], h)
        for ref, val in ((u_ref, u), (w_ref, w), (qd_ref, qd), (kd_ref, kd), (eg_ref, eg)):
            ref[...] = val.reshape(rows, DK)
        qk_ref[:, :CHUNK] = qk.reshape(rows, CHUNK)
        qk_ref[:, CHUNK:] = jnp.zeros((rows, DK - CHUNK), f32)

    out = pl.BlockSpec((rows, DK), lambda i, h: (i, h))
    return pl.pallas_call(
        body, grid=(S // rows, N_HEADS_D), in_specs=_dn_prep_specs(S, rows), out_specs=[out] * 6,
        out_shape=[SDS((S, N_HEADS_D * DK), f32)] * 6, name=name,
        compiler_params=_params("parallel", "parallel"))(qkv, qkv, qkv, proj, alog, dtb)


def _dn_prep_bwd(qkv, proj, alog, dtb, cts, name):
    S = qkv.shape[0]
    G = DN_GROUP
    rows = G * CHUNK

    def body(q_ref, k_ref, v_ref, ba_ref, al_ref, dt_ref, du_ref, dw_ref, dqk_ref, dqd_ref, dkd_ref, deg_ref,
             dq_ref, dk_ref, dv_ref, dba_ref, dal_ref, ddt_ref):
        i, h = pl.program_id(0), pl.program_id(1)
        r3 = lambda ref: ref[...].reshape(G, CHUNK, 128)
        _, vjp = jax.vjp(lambda q, k, v, ba, al, dt: _dn_prep_fn(q, k, v, ba, al, dt, h),
                         r3(q_ref), r3(k_ref), r3(v_ref), r3(ba_ref), al_ref[...], dt_ref[...])
        dqk = dqk_ref[:, :CHUNK].reshape(G, CHUNK, CHUNK)
        dq, dk, dv, dba, dal, ddt = vjp((r3(du_ref), r3(dw_ref), dqk, r3(dqd_ref), r3(dkd_ref), r3(deg_ref)))
        dq_ref[...] = dq.reshape(rows, DK)
        dk_ref[...] = dk.reshape(rows, DK)
        dv_ref[...] = dv.reshape(rows, DK)

        @pl.when(h == 0)
        def _():
            dba_ref[...] = jnp.zeros_like(dba_ref)

        @pl.when(jnp.logical_and(i == 0, h == 0))
        def _():
            dal_ref[...] = jnp.zeros_like(dal_ref)
            ddt_ref[...] = jnp.zeros_like(ddt_ref)

        dba_ref[...] += dba.reshape(rows, 128)
        dal_ref[...] += dal
        ddt_ref[...] += ddt

    hcol = pl.BlockSpec((rows, DK), lambda i, h: (i, h))
    par = pl.BlockSpec((1, 128), lambda i, h: (0, 0))
    W = N_HEADS_D * DK
    return pl.pallas_call(
        body, grid=(S // rows, N_HEADS_D), in_specs=_dn_prep_specs(S, rows) + [hcol] * 6,
        out_specs=[hcol] * 3 + [pl.BlockSpec((rows, 128), lambda i, h: (i, 0)), par, par],
        out_shape=[SDS((S, W), f32)] * 3 + [SDS((S, 128), f32), SDS((1, 128), f32), SDS((1, 128), f32)], name=name,
        compiler_params=_params("arbitrary", "arbitrary"))(qkv, qkv, qkv, proj, alog, dtb, *cts)


def _heads(x):
    return jnp.stack([x[:, DK * h:DK * (h + 1)] for h in range(N_HEADS_D)])


SCAN_CHUNKS = 4


def _dn_scan(pre, name):
    S = pre[0].shape[0]
    NCH = S // CHUNK
    rows = SCAN_CHUNKS * CHUNK

    def body(u_ref, w_ref, qk_ref, qd_ref, kd_ref, eg_ref, o_ref, st_ref, s_ref):
        @pl.when(pl.program_id(0) == 0)
        def _():
            s_ref[...] = jnp.zeros_like(s_ref)

        St = s_ref[...]
        for k in range(SCAN_CHUNKS):
            r = slice(k * CHUNK, (k + 1) * CHUNK)
            st_ref[k] = St
            o, St = _dn_scan_fn(_heads(u_ref[r, :]), _heads(w_ref[r, :]), _heads(qk_ref[r, :])[:, :, :CHUNK],
                                _heads(qd_ref[r, :]), _heads(kd_ref[r, :]), _heads(eg_ref[r, :]), St)
            for h in range(N_HEADS_D):
                o_ref[r, DK * h:DK * (h + 1)] = o[h]
        s_ref[...] = St

    blk = pl.BlockSpec((rows, N_HEADS_D * DK), lambda n: (n, 0))
    return pl.pallas_call(
        body, grid=(S // rows,), in_specs=[blk] * 6,
        out_specs=[blk, pl.BlockSpec((SCAN_CHUNKS, N_HEADS_D, DK, DK), lambda n: (n, 0, 0, 0))],
        out_shape=[SDS((S, N_HEADS_D * DK), f32), SDS((NCH, N_HEADS_D, DK, DK), f32)],
        scratch_shapes=[pltpu.VMEM((N_HEADS_D, DK, DK), f32)], name=name, compiler_params=_params("arbitrary"))(*pre)


def _dn_scan_bwd(pre, states, do, name):
    S = do.shape[0]
    rows = SCAN_CHUNKS * CHUNK
    steps = S // rows

    def body(u_ref, w_ref, qk_ref, qd_ref, kd_ref, eg_ref, st_ref, do_ref,
             du_ref, dw_ref, dqk_ref, dqd_ref, dkd_ref, deg_ref, ds_ref):
        @pl.when(pl.program_id(0) == 0)
        def _():
            ds_ref[...] = jnp.zeros_like(ds_ref)

        dS = ds_ref[...]
        for k in reversed(range(SCAN_CHUNKS)):
            r = slice(k * CHUNK, (k + 1) * CHUNK)
            _, vjp = jax.vjp(_dn_scan_fn, _heads(u_ref[r, :]), _heads(w_ref[r, :]), _heads(qk_ref[r, :])[:, :, :CHUNK],
                             _heads(qd_ref[r, :]), _heads(kd_ref[r, :]), _heads(eg_ref[r, :]), st_ref[k])
            du, dw, dqk, dqd, dkd, deg, dS = vjp((_heads(do_ref[r, :]), dS))
            for h in range(N_HEADS_D):
                c = slice(DK * h, DK * (h + 1))
                for ref, val in ((du_ref, du), (dw_ref, dw), (dqd_ref, dqd), (dkd_ref, dkd), (deg_ref, deg)):
                    ref[r, c] = val[h]
                dqk_ref[r, DK * h:DK * h + CHUNK] = dqk[h]
                dqk_ref[r, DK * h + CHUNK:DK * (h + 1)] = jnp.zeros((CHUNK, DK - CHUNK), f32)
        ds_ref[...] = dS

    blk = pl.BlockSpec((rows, N_HEADS_D * DK), lambda n: (steps - 1 - n, 0))
    return pl.pallas_call(
        body, grid=(steps,),
        in_specs=[blk] * 6 + [pl.BlockSpec((SCAN_CHUNKS, N_HEADS_D, DK, DK), lambda n: (steps - 1 - n, 0, 0, 0)), blk],
        out_specs=[blk] * 6, out_shape=[SDS((S, N_HEADS_D * DK), f32)] * 6,
        scratch_shapes=[pltpu.VMEM((N_HEADS_D, DK, DK), f32)], name=name,
        compiler_params=_params("arbitrary"))(*pre, states, do)


def _loss_head(y, t, name):
    S, D = y.shape
    tm = ROW_TILE

    def body(y_ref, t_ref, dy_ref, l_ref):
        i = pl.program_id(0)
        d = y_ref[...] - t_ref[...]
        dy_ref[...] = d * (1.0 / D)
        part = jnp.sum(jnp.sum(d * d, axis=1, keepdims=True), axis=0, keepdims=True) * (0.5 / D)

        @pl.when(i == 0)
        def _():
            l_ref[...] = jnp.zeros_like(l_ref)

        l_ref[...] += jnp.broadcast_to(part, l_ref.shape)

    spec = pl.BlockSpec((tm, D), lambda i: (i, 0))
    dy, l = pl.pallas_call(body, grid=(S // tm,), in_specs=[spec, spec],
                           out_specs=[spec, pl.BlockSpec((1, 128), lambda i: (0, 0))],
                           out_shape=[SDS((S, D), f32), SDS((1, 128), f32)], name=name,
                           compiler_params=_params("arbitrary"))(y, t)
    return l[0, 0], dy


def _adamw(w, g, m, v, tr, name):
    L, R, C = w.shape
    assert R % tr == 0

    def body(w_ref, g_ref, m_ref, v_ref, d_ref, mo_ref, vo_ref):
        gv = g_ref[...]
        m2 = ADAM_B1 * m_ref[...] + (1.0 - ADAM_B1) * gv
        v2 = ADAM_B2 * v_ref[...] + (1.0 - ADAM_B2) * (gv * gv)
        m_hat = m2 / (1.0 - ADAM_B1 ** ADAM_STEP)
        v_hat = v2 / (1.0 - ADAM_B2 ** ADAM_STEP)
        d_ref[...] = -ADAM_LR * (m_hat / (jnp.sqrt(v_hat) + ADAM_EPS) + ADAM_WD * w_ref[...])
        mo_ref[...] = m2
        vo_ref[...] = v2

    spec = pl.BlockSpec((1, tr, C), lambda l, i: (l, i, 0))
    return pl.pallas_call(body, grid=(L, R // tr), in_specs=[spec] * 4, out_specs=[spec] * 3,
                          out_shape=[SDS((L, R, C), f32)] * 3, name=name,
                          compiler_params=_params("parallel", "parallel"))(w, g, m, v)


def _rope_tables(S):
    inv = 1.0 / (10000.0 ** (jnp.arange(0, HEAD_DIM, 2, dtype=f32) / HEAD_DIM))
    ang = jnp.arange(S, dtype=f32)[:, None] * inv[None, :]
    cos, sin = jnp.cos(ang), jnp.sin(ang)
    return (jnp.tile(jnp.concatenate([cos, cos], axis=1), (1, N_HEADS_A)),
            jnp.tile(jnp.concatenate([-sin, sin], axis=1), (1, N_HEADS_A)))


def _layer_fwd(x, W, cos, sgn_sin, l, late_weights=None):
    n = f"l{l}_"
    (h1,) = _rows(_rms_fn, [x], [W["norm_pre_mix"]], [(D_MODEL, bf16)], n + "pre_mix_norm")
    proj = _mm(h1, W["w_in"], "nn", 512, 768, f32, n + "in_proj")
    qr, kr = _rows(_rope_fwd_fn, [(proj, ATTN_W, 0), (proj, ATTN_W, 1), cos, sgn_sin], [],
                   [(ATTN_W, f32), (ATTN_W, f32)], n + "rope")
    attn_out, lse = _attn_fwd(qr, kr, proj, n + "attn_fwd")
    (qkv,) = _colconv_fwd([(proj, 3)], [(W["dn_conv_w"], 0)], None, 4, _silu_fn, 3, 512, [f32], n + "dn_conv")
    dn_pre = _dn_prep(qkv, proj, W["dn_a_log"], W["dn_dt_bias"], n + "dn_prep")
    dn_o, dn_states = _dn_scan(dn_pre, n + "dn_scan")
    (dn_out,) = _rows(_dn_post_fn, [(dn_o, DK, 0), (proj, DK, 3072 // DK)], [W["dn_norm_w"]], [(DK, bf16)], n + "dn_post",
                      ncol=N_HEADS_D)
    mix_in = jnp.concatenate([attn_out, dn_out], axis=1)
    late = late_weights(mix_in) if late_weights is not None else {}
    W = {**W, **late}
    mix = _mm(mix_in, W["w_out"], "nn", 512, 512, f32, n + "out_proj")
    (x1,) = _rows(_res_rms_fn, [mix, x], [W["norm_post_mix"]], [(D_MODEL, f32)], n + "post_mix_norm")
    (h2,) = _rows(_rms_fn, [x1], [W["norm_pre_ffn"]], [(D_MODEL, bf16)], n + "pre_ffn_norm")
    u0 = _mm(h2, W["ffn_w_in"], "nn", 1024, 512, bf16, n + "ffn_in")
    nb_ff = D_FF // 256
    (act,) = _colconv_fwd([(u0, 0), (u0, nb_ff)], [(W["ffn_conv_w"], 0), (W["ffn_conv_w"], nb_ff)],
                          [(W["ffn_conv_b"], 0), (W["ffn_conv_b"], nb_ff)], 3, _geglu_fn, nb_ff, 256, [bf16],
                          n + "ffn_conv_glu")
    f = _mm(act, W["ffn_w_out"], "nn", 512, 512, f32, n + "ffn_out")
    (x2,) = _rows(_res_rms_fn, [f, x1], [W["norm_post_ffn"]], [(D_MODEL, f32)], n + "post_ffn_norm")
    saved = dict(x=x, h1=h1, proj=proj, qr=qr, kr=kr, attn_out=attn_out, lse=lse, qkv=qkv, dn_pre=dn_pre, dn_o=dn_o,
                 dn_states=dn_states, mix_in=mix_in, mix=mix, x1=x1, h2=h2, u0=u0, act=act, f=f, late=late)
    return x2, saved


def _layer_bwd(dx2, sv, W, cos, sgn_sin, l, after_ffn=None):
    n = f"l{l}_"
    S = dx2.shape[0]
    g = {}
    (df,), (g["norm_post_ffn"],) = _rows_vjp(_rms_fn, [sv["f"]], [W["norm_post_ffn"]], [dx2], [0], [0],
                                             n + "post_ffn_norm_bwd", row_dtype=bf16)
    dact = _mm(df, W["ffn_w_out"], "nt", 512, 1408, f32, n + "ffn_out_dx")
    g["ffn_w_out"] = _mm(sv["act"], df, "tn", 256, 1024, f32, n + "ffn_out_dw")
    nb_ff = D_FF // 256
    u0 = sv["u0"]
    dxs, dws, dbs = _colconv_bwd([(u0, 0), (u0, nb_ff)], [(W["ffn_conv_w"], 0), (W["ffn_conv_w"], nb_ff)],
                                 [(W["ffn_conv_b"], 0), (W["ffn_conv_b"], nb_ff)], 3, _geglu_fn, [dact], nb_ff, 256,
                                 n + "ffn_conv_glu_bwd", dx_dtype=bf16)
    du0 = jnp.concatenate(dxs, axis=1)
    g["ffn_conv_w"] = jnp.concatenate(dws, axis=1)
    g["ffn_conv_b"] = jnp.concatenate(dbs, axis=1)
    dh2 = _mm(du0, W["ffn_w_in"], "nt", 512, 512, f32, n + "ffn_in_dx")
    g["ffn_w_in"] = _mm(sv["h2"], du0, "tn", 512, D_FF // 2, f32, n + "ffn_in_dw", column_shards=True)
    (dx1,), (g["norm_pre_ffn"],) = _rows_vjp(_rms_fn, [sv["x1"]], [W["norm_pre_ffn"]], [dh2], [0], [0],
                                             n + "pre_ffn_norm_bwd", adds={0: dx2})
    if after_ffn is not None:
        W = dict(W, norm_post_mix=W["norm_post_mix"] + after_ffn(g, dx1))
    (dmix,), (g["norm_post_mix"],) = _rows_vjp(_rms_fn, [sv["mix"]], [W["norm_post_mix"]], [dx1], [0], [0],
                                               n + "post_mix_norm_bwd", row_dtype=bf16)
    dmix_in = _mm(dmix, W["w_out"], "nt", 512, 512, f32, n + "out_proj_dx")
    g["w_out"] = _mm(sv["mix_in"], dmix, "tn", 512, 512, f32, n + "out_proj_dw")

    (ddn_o, dz), (g["dn_norm_w"],) = _rows_vjp(
        _dn_post_fn, [(sv["dn_o"], DK, 0), (sv["proj"], DK, 3072 // DK)], [W["dn_norm_w"]], [(dmix_in, DK, ATTN_W // DK)],
        [0, 1], [0], n + "dn_post_bwd", ncol=N_HEADS_D)
    dpre = _dn_scan_bwd(sv["dn_pre"], sv["dn_states"], ddn_o, n + "dn_scan_bwd")
    dq, dk, dv, dba, g["dn_a_log"], g["dn_dt_bias"] = _dn_prep_bwd(
        sv["qkv"], sv["proj"], W["dn_a_log"], W["dn_dt_bias"], dpre, n + "dn_prep_bwd")
    dqkv = jnp.concatenate([dq, dk, dv], axis=1)
    (dqkv0,), (g["dn_conv_w"],), _ = _colconv_bwd([(sv["proj"], 3)], [(W["dn_conv_w"], 0)], None, 4, _silu_fn,
                                                 [dqkv], 3, 512, n + "dn_conv_bwd")

    dqr, dkr, dav = _attn_bwd(sv["qr"], sv["kr"], sv["proj"], dmix_in, sv["attn_out"], sv["lse"], n + "attn_bwd")
    daq, dak = _rows(_rope_bwd_fn, [dqr, dkr, cos, sgn_sin], [], [(ATTN_W, f32)] * 2, n + "rope_bwd")
    dproj = jnp.concatenate([daq, dak, dav, dqkv0, dz, dba, jnp.zeros((S, PROJ_W - 3712), f32)], axis=1).astype(bf16)
    dh1 = _mm(dproj, W["w_in"], "nt", 512, 512, f32, n + "in_proj_dx")
    g["w_in"] = _mm(sv["h1"], dproj, "tn", 512, 768, f32, n + "in_proj_dw")
    (dx,), (g["norm_pre_mix"],) = _rows_vjp(_rms_fn, [sv["x"]], [W["norm_pre_mix"]], [dh1], [0], [0],
                                            n + "pre_mix_norm_bwd", adds={0: dx1})
    return dx, g


def _local_step(x, target, layers):
    cos, sgn_sin = _rope_tables(x.shape[0])
    saved = []
    for l, W in enumerate(layers):
        x, sv = _layer_fwd(x, W, cos, sgn_sin, l)
        saved.append(sv)
    loss, dx = _loss_head(x, target, "loss_head")
    grads = [None] * len(layers)
    for l in reversed(range(len(layers))):
        dx, grads[l] = _layer_bwd(dx, saved[l], layers[l], cos, sgn_sin, l)
    return loss, dx, grads


def _pos():
    x, y, c = lax.axis_index("x"), lax.axis_index("y"), lax.axis_index("c")
    return x, y, c, [(1 - x, y), (x, 1 - y), (1 - x, 1 - y)]


def _rcopy(src, dst, send_sem, recv_sem, dev):
    return pltpu.make_async_remote_copy(src_ref=src, dst_ref=dst, send_sem=send_sem, recv_sem=recv_sem,
                                        device_id=dev, device_id_type=MESH)


def _half_rows(ref, h, which, axis):
    if h is None:
        return ref
    rows = pl.ds(pl.multiple_of(which * h, 16), h)
    return ref.at[:, rows, :] if axis == 1 else ref.at[rows, :]


def _dma_sems(*counts):
    return [pltpu.SemaphoreType.DMA((k,)) for k in counts]


def _all_gather(arrs, halves, name):
    n = len(arrs)

    def body(*refs):
        ins, outs = refs[:n], refs[n:2 * n]
        send1, recv1, send2, recv2 = refs[2 * n:]
        x, y, c, chips = _pos()
        me, sib, s_me = (x, y, c), (x, y, 1 - c), 2 * x + y
        sends = []
        for i in range(n):
            for j, chip in enumerate(chips):
                cp = _rcopy(_half_rows(ins[i], halves[i], c, 1), _half_rows(outs[i].at[s_me], halves[i], c, 1),
                            send1.at[3 * i + j], recv1.at[3 * i + j], (*chip, c))
                cp.start()
                sends.append(cp)
        for i in range(n):
            for j, (px, py) in enumerate(chips):
                k = 3 * i + j
                landed = _half_rows(outs[i].at[2 * px + py], halves[i], c, 1)
                _rcopy(landed, landed, send1.at[k], recv1.at[k], me).wait_recv()
                if halves[i] is not None:
                    cp = _rcopy(landed, landed, send2.at[k], recv2.at[k], sib)
                    cp.start()
                    sends.append(cp)
        for i in range(n):
            if halves[i] is None:
                continue
            for j, (px, py) in enumerate(chips):
                k = 3 * i + j
                other = _half_rows(outs[i].at[2 * px + py], halves[i], 1 - c, 1)
                _rcopy(other, other, send2.at[k], recv2.at[k], me).wait_recv()
        for cp in sends:
            cp.wait_send()

    return pl.pallas_call(
        body, in_specs=[ANY] * n, out_specs=[ANY] * n,
        out_shape=[SDS((4,) + a.shape, a.dtype) for a in arrs],
        scratch_shapes=_dma_sems(3 * n, 3 * n, 3 * n, 3 * n), name=name)(*arrs)


HBM = pl.BlockSpec(memory_space=pltpu.HBM)
SEM = pl.BlockSpec(memory_space=pltpu.SEMAPHORE)
_EFFECT = pltpu.SideEffectType.DATAFLOW_SIDE_EFFECTING


def _in_hbm(a):
    return pltpu.with_memory_space_constraint(a, pltpu.HBM)


def _split_copy(srcs, land_shapes, plan, per, after, name):
    n = len(srcs)
    k = per * n

    def body(*refs):
        ins, lands, token = refs[:n], refs[n:2 * n], refs[-1]
        send, recv = refs[2 * n + 1], refs[2 * n + 2]
        for i, (src, dst, dev, _) in enumerate(plan(ins, lands)):
            _rcopy(src, dst, send.at[i], recv.at[i], dev).start()
        token[...] = jnp.zeros_like(token)

    lands = [_in_hbm(lax.empty(s.shape, s.dtype)) for s in land_shapes]
    return pl.pallas_call(
        body, name=name,
        out_shape=(pltpu.SemaphoreType.DMA((k,)), pltpu.SemaphoreType.DMA((k,)),
                   *[pltpu.HBM(a.shape, a.dtype) for a in srcs], *[pltpu.HBM(s.shape, s.dtype) for s in land_shapes],
                   SDS((8, 128), f32)),
        in_specs=[HBM] * (2 * n) + [ANY], out_specs=(SEM, SEM, *[HBM] * (2 * n), pl.BlockSpec(memory_space=pltpu.VMEM)),
        input_output_aliases={i: 2 + i for i in range(2 * n)},
        compiler_params=pltpu.CompilerParams(has_side_effects=_EFFECT))(*[_in_hbm(a) for a in srcs], *lands, after)


def _split_wait(started, n, plan, after, name):
    send, recv = started[0], started[1]
    thru = started[2:2 + 2 * n]

    def body(*refs):
        ins, lands = refs[:n], refs[n:2 * n]
        send_ref, recv_ref = refs[2 * n], refs[2 * n + 1]
        for i, (src, _, dev, mine) in enumerate(plan(ins, lands)):
            cp = _rcopy(src, mine, send_ref.at[i], recv_ref.at[i], dev)
            cp.wait_send()
            cp.wait_recv()

    res = pl.pallas_call(
        body, name=name, out_shape=tuple(pltpu.HBM(a.shape, a.dtype) for a in thru),
        in_specs=[HBM] * (2 * n) + [SEM, SEM, ANY], out_specs=tuple([HBM] * (2 * n)),
        input_output_aliases={i: i for i in range(2 * n)},
        compiler_params=pltpu.CompilerParams(has_side_effects=_EFFECT))(*thru, send, recv, after)
    return res[:n], res[n:]


def _gather_plan(halves):
    def plan(ins, lands):
        x, y, c, chips = _pos()
        out = []
        for i in range(len(ins)):
            for px, py in chips:
                out.append((_half_rows(ins[i], halves[i], c, 1), _half_rows(lands[i].at[2 * x + y], halves[i], c, 1),
                            (px, py, c), _half_rows(lands[i].at[2 * px + py], halves[i], c, 1)))
        return out
    return plan


def _scatter_plan(ins, lands):
    x, y, c, chips = _pos()
    out = []
    for i in range(len(ins)):
        for j, (px, py) in enumerate(chips):
            out.append((ins[i].at[2 * px + py], lands[i].at[j], (px, py, c), lands[i].at[j]))
    return out


def _exchange_plan(ins, lands):
    x, y, c, _ = _pos()
    return [(_half_rows(g, g.shape[1] // 2, 1 - c, 1), land, (x, y, 1 - c), land) for g, land in zip(ins, lands)]


def _pass_to_sibling(lands, halves, name):
    n = len(lands)

    def body(*refs):
        outs = refs[n:2 * n]
        send, recv = refs[2 * n:]
        x, y, c, chips = _pos()
        sends = []
        for i in range(n):
            for j, (px, py) in enumerate(chips):
                landed = _half_rows(outs[i].at[2 * px + py], halves[i], c, 1)
                cp = _rcopy(landed, landed, send.at[3 * i + j], recv.at[3 * i + j], (x, y, 1 - c))
                cp.start()
                sends.append(cp)
        for i in range(n):
            for j, (px, py) in enumerate(chips):
                other = _half_rows(outs[i].at[2 * px + py], halves[i], 1 - c, 1)
                _rcopy(other, other, send.at[3 * i + j], recv.at[3 * i + j], (x, y, c)).wait_recv()
        for cp in sends:
            cp.wait_send()

    return pl.pallas_call(
        body, in_specs=[ANY] * n, out_specs=[ANY] * n, out_shape=[SDS(a.shape, a.dtype) for a in lands],
        input_output_aliases={k: k for k in range(n)}, scratch_shapes=_dma_sems(3 * n, 3 * n), name=name)(*lands)


def _exchange_halves(gs, name):
    n = len(gs)

    def body(*refs):
        ins, outs = refs[:n], refs[n:2 * n]
        send, recv = refs[2 * n:]
        x, y, c, _ = _pos()
        sends = []
        for k in range(n):
            cp = _rcopy(_half_rows(ins[k], gs[k].shape[1] // 2, 1 - c, 1), outs[k], send.at[k], recv.at[k], (x, y, 1 - c))
            cp.start()
            sends.append(cp)
        for k in range(n):
            _rcopy(outs[k], outs[k], send.at[k], recv.at[k], (x, y, c)).wait_recv()
        for cp in sends:
            cp.wait_send()

    return pl.pallas_call(
        body, in_specs=[ANY] * n, out_specs=[ANY] * n,
        out_shape=[SDS((4, g.shape[1] // 2, g.shape[2]), g.dtype) for g in gs],
        scratch_shapes=_dma_sems(n, n), name=name)(*gs)


def _scatter_partials(ps, name):
    n = len(ps)

    def body(*refs):
        ins, outs = refs[:n], refs[n:2 * n]
        send, recv = refs[2 * n:]
        x, y, c, chips = _pos()
        sends = []
        for k in range(n):
            for j, (px, py) in enumerate(chips):
                cp = _rcopy(ins[k].at[2 * px + py], outs[k].at[j], send.at[3 * k + j], recv.at[3 * k + j], (px, py, c))
                cp.start()
                sends.append(cp)
        for k in range(n):
            for j in range(3):
                _rcopy(outs[k].at[j], outs[k].at[j], send.at[3 * k + j], recv.at[3 * k + j], (x, y, c)).wait_recv()
        for cp in sends:
            cp.wait_send()

    return pl.pallas_call(
        body, in_specs=[ANY] * n, out_specs=[ANY] * n,
        out_shape=[SDS((3,) + p.shape[1:], p.dtype) for p in ps],
        scratch_shapes=_dma_sems(3 * n, 3 * n), name=name)(*ps)


def _join_halves(rs, name):
    n = len(rs)

    def body(*refs):
        outs = refs[n:2 * n]
        send, recv = refs[2 * n:]
        x, y, c, _ = _pos()
        sends = []
        for k in range(n):
            mine = _half_rows(outs[k], rs[k].shape[0] // 2, c, 0)
            cp = _rcopy(mine, mine, send.at[k], recv.at[k], (x, y, 1 - c))
            cp.start()
            sends.append(cp)
        for k in range(n):
            other = _half_rows(outs[k], rs[k].shape[0] // 2, 1 - c, 0)
            _rcopy(other, other, send.at[k], recv.at[k], (x, y, c)).wait_recv()
        for cp in sends:
            cp.wait_send()

    return pl.pallas_call(
        body, in_specs=[ANY] * n, out_specs=[ANY] * n, out_shape=[SDS(r.shape, r.dtype) for r in rs],
        input_output_aliases={k: k for k in range(n)}, scratch_shapes=_dma_sems(n, n), name=name)(*rs)


def _all_reduce_small(pack, name):
    R = pack.shape[0]

    def body(in_ref, out_ref, buf, send, recv):
        x, y, c, _ = _pos()
        me = 4 * x + 2 * y + c
        buf[me] = in_ref[...]
        sends = []
        for k in range(1, 8):
            peer = me ^ k
            cp = _rcopy(buf.at[me], buf.at[me], send.at[k - 1], recv.at[k - 1], ((peer >> 2) & 1, (peer >> 1) & 1, peer & 1))
            cp.start()
            sends.append(cp)
        for k in range(1, 8):
            _rcopy(buf.at[me ^ k], buf.at[me ^ k], send.at[k - 1], recv.at[k - 1], (x, y, c)).wait_recv()
        for cp in sends:
            cp.wait_send()
        acc = buf[0]
        for d in range(1, 8):
            acc = acc + buf[d]
        out_ref[...] = acc

    return pl.pallas_call(
        body, out_shape=SDS((R, 128), f32),
        in_specs=[pl.BlockSpec(memory_space=pltpu.VMEM)], out_specs=pl.BlockSpec(memory_space=pltpu.VMEM),
        scratch_shapes=[pltpu.VMEM((8, R, 128), f32)] + _dma_sems(7, 7), name=name)(pack)


def _add_sibling(g, recv, c_arr, tr, name):
    _, R, C = g.shape
    h = R // 2
    nrb = h // tr
    assert h % tr == 0

    def body(c_ref, g_ref, r_ref, o_ref):
        o_ref[...] = (g_ref[...] + r_ref[...]).astype(o_ref.dtype)

    spec = pl.BlockSpec((1, tr, C), lambda s, r, c_ref: (s, r, 0))
    grid_spec = pltpu.PrefetchScalarGridSpec(
        num_scalar_prefetch=1, grid=(4, nrb),
        in_specs=[pl.BlockSpec((1, tr, C), lambda s, r, c_ref: (s, c_ref[0] * nrb + r, 0)), spec], out_specs=spec)
    return pl.pallas_call(body, grid_spec=grid_spec, out_shape=SDS((4, h, C), bf16), name=name,
                          compiler_params=_params("parallel", "parallel"))(c_arr, g, recv)


def _add_chips(p, recv, sc_arr, tr, name):
    _, h, C = p.shape
    nrb = h // tr
    assert h % tr == 0

    def body(sc_ref, p_ref, r_ref, o_ref):
        o_ref[...] = (p_ref[0].astype(f32) + r_ref[0].astype(f32)) + (r_ref[1].astype(f32) + r_ref[2].astype(f32))

    grid_spec = pltpu.PrefetchScalarGridSpec(
        num_scalar_prefetch=1, grid=(nrb,),
        in_specs=[pl.BlockSpec((1, tr, C), lambda r, sc_ref: (sc_ref[0], r, 0)),
                  pl.BlockSpec((3, tr, C), lambda r, sc_ref: (0, r, 0))],
        out_specs=pl.BlockSpec((tr, C), lambda r, sc_ref: (sc_ref[1] * nrb + r, 0)))
    return pl.pallas_call(body, grid_spec=grid_spec, out_shape=SDS((2 * h, C), f32), name=name,
                          compiler_params=_params("parallel"))(sc_arr, p, recv)


_BIG = (("w_in", 1024, 256), ("w_out", 256, 128), ("ffn_w_in", 1024, 256), ("ffn_w_out", 704, 352))
_SMALL = ("dn_conv_w", "ffn_conv_w", "ffn_conv_b", "norm_pre_mix", "norm_post_mix", "norm_pre_ffn", "norm_post_ffn",
          "dn_norm_w", "dn_a_log", "dn_dt_bias")
_WEIGHTS = ("w_in", "dn_conv_w", "dn_a_log", "dn_dt_bias", "dn_norm_w", "w_out", "ffn_w_in", "ffn_conv_w", "ffn_conv_b",
            "ffn_w_out", "norm_pre_mix", "norm_post_mix", "norm_pre_ffn", "norm_post_ffn")
_ADAM_ROWS = {"w_in": 256, "w_out": 256, "ffn_w_in": 128, "ffn_w_out": 176}


def _shard_major(name, g):
    if name == "w_in":
        return jnp.stack([g[:, 898 * s:898 * (s + 1)] for s in range(4)])
    if name == "ffn_w_in":
        return g
    return g.reshape(4, g.shape[0] // 4, g.shape[1])


def kernel(x, w_in, dn_conv_w, dn_a_log, dn_dt_bias, dn_norm_w, w_out, ffn_w_in, ffn_conv_w, ffn_conv_b, ffn_w_out, norm_pre_mix, norm_post_mix, norm_pre_ffn, norm_post_ffn, loss_target, m_w_in, m_dn_conv_w, m_dn_a_log, m_dn_dt_bias, m_dn_norm_w, m_w_out, m_ffn_w_in, m_ffn_conv_w, m_ffn_conv_b, m_ffn_w_out, m_norm_pre_mix, m_norm_post_mix, m_norm_pre_ffn, m_norm_post_ffn, v_w_in, v_dn_conv_w, v_dn_a_log, v_dn_dt_bias, v_dn_norm_w, v_w_out, v_ffn_w_in, v_ffn_conv_w, v_ffn_conv_b, v_ffn_w_out, v_norm_pre_mix, v_norm_post_mix, v_norm_pre_ffn, v_norm_post_ffn):
    w = dict(w_in=w_in, dn_conv_w=dn_conv_w, dn_a_log=dn_a_log, dn_dt_bias=dn_dt_bias, dn_norm_w=dn_norm_w, w_out=w_out,
             ffn_w_in=ffn_w_in, ffn_conv_w=ffn_conv_w, ffn_conv_b=ffn_conv_b, ffn_w_out=ffn_w_out, norm_pre_mix=norm_pre_mix,
             norm_post_mix=norm_post_mix, norm_pre_ffn=norm_pre_ffn, norm_post_ffn=norm_post_ffn)
    m = dict(w_in=m_w_in, dn_conv_w=m_dn_conv_w, dn_a_log=m_dn_a_log, dn_dt_bias=m_dn_dt_bias, dn_norm_w=m_dn_norm_w,
             w_out=m_w_out, ffn_w_in=m_ffn_w_in, ffn_conv_w=m_ffn_conv_w, ffn_conv_b=m_ffn_conv_b, ffn_w_out=m_ffn_w_out,
             norm_pre_mix=m_norm_pre_mix, norm_post_mix=m_norm_post_mix, norm_pre_ffn=m_norm_pre_ffn,
             norm_post_ffn=m_norm_post_ffn)
    v = dict(w_in=v_w_in, dn_conv_w=v_dn_conv_w, dn_a_log=v_dn_a_log, dn_dt_bias=v_dn_dt_bias, dn_norm_w=v_dn_norm_w,
             w_out=v_w_out, ffn_w_in=v_ffn_w_in, ffn_conv_w=v_ffn_conv_w, ffn_conv_b=v_ffn_conv_b, ffn_w_out=v_ffn_w_out,
             norm_pre_mix=v_norm_pre_mix, norm_post_mix=v_norm_post_mix, norm_pre_ffn=v_norm_pre_ffn,
             norm_post_ffn=v_norm_post_ffn)
    xi, yi, ci = lax.axis_index("x"), lax.axis_index("y"), lax.axis_index("c")
    s_me = 2 * xi + yi
    c_arr = jnp.reshape(ci, (1,)).astype(jnp.int32)
    sc_arr = jnp.stack([s_me, ci]).astype(jnp.int32)

    mats = [name for name, _, _ in _BIG]
    rest = mats[1:]
    half_of = {name: rows // 2 for name, rows, _ in _BIG}
    tiles = {name: tr for name, _, tr in _BIG}
    gathered_shape = lambda a: SDS((4,) + a.shape, a.dtype)

    own = {k: w[k].astype(bf16) for k in mats}
    got_in = _all_gather([own["w_in"][0:1], dn_conv_w, ffn_conv_w], [half_of["w_in"], None, None], "weights_gather_w_in0")
    plan0 = _gather_plan([half_of[k] for k in rest])
    src0 = [own[k][0:1] for k in rest]
    started0 = _split_copy(src0, [gathered_shape(a) for a in src0], plan0, 3, got_in[0], "weights_gather_l0_start")
    plan1 = _gather_plan([half_of[k] for k in mats])
    src1 = [own[k][1:2] for k in mats]
    started1 = _split_copy(src1, [gathered_shape(a) for a in src1], plan1, 3, started0[-1], "weights_gather_l1_start")

    def pick(mine, gathered):
        return [jnp.where(s_me == s, mine, gathered[s]) for s in range(4)]

    conv = {"dn_conv_w": jnp.concatenate(pick(dn_conv_w, got_in[1]), axis=-1),
            "ffn_conv_w": jnp.concatenate(pick(ffn_conv_w, got_in[2]), axis=-1)}
    lanes = lambda a: jnp.pad(a, ((0, 0), (0, 128 - a.shape[1])))
    vec = dict(dn_a_log=lanes(dn_a_log), dn_dt_bias=lanes(dn_dt_bias), dn_norm_w=dn_norm_w, ffn_conv_b=ffn_conv_b,
               norm_pre_mix=norm_pre_mix, norm_post_mix=norm_post_mix, norm_pre_ffn=norm_pre_ffn, norm_post_ffn=norm_post_ffn)

    def matrices(l, names, gathered):
        W = {}
        for k, a in zip(names, gathered):
            if k in ("w_out", "ffn_w_out"):
                rows_, cols = own[k].shape[1:]
                W[k] = lax.dynamic_update_slice(a[:, 0], own[k][l][None], (s_me, 0, 0)).reshape(4 * rows_, cols)
            else:
                cat = jnp.concatenate(pick(own[k][l], a[:, 0]), axis=-1)
                W[k] = jnp.pad(cat, ((0, 0), (0, PROJ_W - IN_COLS))) if k == "w_in" else cat
        return W

    def small_weights(l):
        return {**{k: a[l] for k, a in conv.items()}, **{k: a[l:l + 1] for k, a in vec.items()}}

    def late_l0(mix_in):
        _, landed = _split_wait(started0, len(rest), plan0, mix_in, "weights_gather_l0_wait")
        return matrices(0, rest, _pass_to_sibling(landed, [half_of[k] for k in rest], "weights_gather_l0_sibling"))

    cos, sgn_sin = _rope_tables(x.shape[1])
    W0 = {**small_weights(0), **matrices(0, ["w_in"], got_in[:1])}
    W0_first = dict(W0, norm_pre_mix=W0["norm_pre_mix"] + started1[-1][0, 0])
    x1, saved0 = _layer_fwd(x[0], W0_first, cos, sgn_sin, 0, late_weights=late_l0)
    _, landed1 = _split_wait(started1, len(mats), plan1, x1, "weights_gather_l1_wait")
    W1 = {**small_weights(1),
          **matrices(1, mats, _pass_to_sibling(landed1, [half_of[k] for k in mats], "weights_gather_l1_sibling"))}
    x2, saved1 = _layer_fwd(x1, W1, cos, sgn_sin, 1)
    loss_local, dy = _loss_head(x2, loss_target[0], "loss_head")
    loss = lax.psum(loss_local, ("x", "y", "c"))

    def shard_major(names, grads_l):
        return [_shard_major(name, grads_l[name]) for name in names]

    def add_siblings(l, names, gs, from_sib):
        return [_add_sibling(g, r, c_arr, tiles[name], f"add_sibling_{name}{l}") for g, r, name in zip(gs, from_sib, names)]

    def scatter_start(l, names, parts, after, tag):
        return _split_copy(parts, [SDS((3,) + p.shape[1:], p.dtype) for p in parts], _scatter_plan, 3, after,
                           f"grads_l{l}{tag}_scatter_start")

    def owner_sums(l, names, sent, after, tag):
        parts, recvd = _split_wait(sent, len(names), _scatter_plan, after, f"grads_l{l}{tag}_scatter_wait")
        return [_add_chips(p, r, sc_arr, tiles[name], f"add_chips_{name}{l}") for p, r, name in zip(parts, recvd, names)]

    dx1, grads1 = _layer_bwd(dy, saved1, W1, cos, sgn_sin, 1)
    gs1 = shard_major(mats, grads1)
    swap1 = _split_copy(gs1, [SDS((4, g.shape[1] // 2, g.shape[2]), g.dtype) for g in gs1], _exchange_plan, 1, dx1,
                        "grads_l1_sibling_start")
    ffn = ["ffn_w_in", "ffn_w_out"]
    launched = {}

    def after_ffn_l0(g_ffn, dx_mid):
        gs1_, from_sib1 = _split_wait(swap1, len(mats), _exchange_plan, dx_mid, "grads_l1_sibling_wait")
        launched["l1"] = scatter_start(1, mats, add_siblings(1, mats, gs1_, from_sib1), dx_mid, "")
        gs0 = shard_major(ffn, g_ffn)
        from_sib0 = _exchange_halves(gs0, "grads_l0_ffn_to_sibling")
        launched["l0_ffn"] = scatter_start(0, ffn, add_siblings(0, ffn, gs0, from_sib0), launched["l1"][-1], "_ffn")
        return launched["l0_ffn"][-1][0, 0]

    W0_last = dict(W0, **saved0["late"], norm_post_ffn=W0["norm_post_ffn"] + swap1[-1][0, 0])
    dx, grads0 = _layer_bwd(dx1, saved0, W0_last, cos, sgn_sin, 0, after_ffn=after_ffn_l0)
    mix = ["w_in", "w_out"]
    gs0 = shard_major(mix, grads0)
    part0 = add_siblings(0, mix, gs0, _exchange_halves(gs0, "grads_l0_mix_to_sibling"))
    recvd0 = _scatter_partials(part0, "grads_l0_mix_scatter")
    red = dict(zip([(0, k) for k in mix],
                   [_add_chips(p, r, sc_arr, tiles[k], f"add_chips_{k}0") for p, r, k in zip(part0, recvd0, mix)]))
    red.update(zip([(0, k) for k in ffn], owner_sums(0, ffn, launched["l0_ffn"], dx, "_ffn")))
    red.update(zip([(1, k) for k in mats], owner_sums(1, mats, launched["l1"], dx, "")))
    order = [(l, k) for l in range(2) for k in mats]
    joined = dict(zip(order, _join_halves([red[key] for key in order], "grads_join_halves")))
    g_out = {k: jnp.stack([joined[(0, k)], joined[(1, k)]]) for k in mats}
    grads = [grads0, grads1]

    small = {}
    for name in _SMALL:
        per_layer = [grads[l][name] for l in range(2)]
        if name in ("dn_a_log", "dn_dt_bias"):
            per_layer = [p[:, :N_HEADS_D] for p in per_layer]
        small[name] = jnp.stack(per_layer).reshape((2,) + (w[name].shape[1:] if name not in ("dn_conv_w", "ffn_conv_w")
                                                           else per_layer[0].shape))
    flat = jnp.concatenate([small[name].reshape(-1) for name in _SMALL])
    n_rows = -(-flat.shape[0] // 1024) * 8
    summed = _all_reduce_small(jnp.pad(flat, (0, n_rows * 128 - flat.shape[0])).reshape(n_rows, 128),
                               "small_grads_all_reduce").reshape(-1)
    off = 0
    for name in _SMALL:
        size = small[name].size
        g_out[name] = summed[off:off + size].reshape(small[name].shape)
        off += size
    g_out["dn_conv_w"] = lax.dynamic_slice_in_dim(g_out["dn_conv_w"], s_me * 384, 384, axis=2)
    g_out["ffn_conv_w"] = lax.dynamic_slice_in_dim(g_out["ffn_conv_w"], s_me * 1408, 1408, axis=2)

    deltas, new_m, new_v = {}, {}, {}
    for name in _WEIGHTS:
        shape = w[name].shape
        as3 = (lambda a: a) if len(shape) == 3 else (lambda a: a.reshape(shape[0], 1, shape[1]))
        tr = _ADAM_ROWS.get(name, as3(w[name]).shape[1])
        d_, m_, v_ = _adamw(as3(w[name]), as3(g_out[name]), as3(m[name]), as3(v[name]), tr, f"adamw_{name}")
        deltas[name], new_m[name], new_v[name] = d_.reshape(shape), m_.reshape(shape), v_.reshape(shape)

    return (loss, dx[None], *[g_out[k] for k in _WEIGHTS], *[deltas[k] for k in _WEIGHTS],
            *[new_m[k] for k in _WEIGHTS], *[new_v[k] for k in _WEIGHTS])
```

```python
import jax
import jax.numpy as jnp
from jax import lax
from jax.experimental import pallas as pl
from jax.experimental.pallas import tpu as pltpu

f32, bf16 = jnp.float32, jnp.bfloat16
SDS = jax.ShapeDtypeStruct
HI = lax.Precision.HIGH
MESH = pl.DeviceIdType.MESH
ANY = pl.BlockSpec(memory_space=pl.ANY)

D_MODEL = 1024
N_HEADS_A, HEAD_DIM = 8, 64
ATTN_W = 512
N_HEADS_D, DK = 4, 128
CHUNK = 64
D_FF = 2816
IN_COLS = 3592
PROJ_W = 3840
BRANCHES = ((1, 16), (4, 4), (16, 1))
EPS = 1e-6
NEG = -1e30
ROW_TILE = 256
VMEM_LIMIT = 56 * 1024 * 1024

ADAM_LR, ADAM_B1, ADAM_B2, ADAM_EPS, ADAM_WD, ADAM_STEP = 0.001, 0.9, 0.999, 1e-08, 0.01, 10


def _params(*sem):
    return pltpu.CompilerParams(dimension_semantics=sem, vmem_limit_bytes=VMEM_LIMIT)


def _mm(a, b, mode, tm, tn, out_dtype, name, column_shards=False):
    if mode == "nn":
        (M, K), N = a.shape, b.shape[1]
        dims = (((1,), (0,)), ((), ()))
        a_spec = pl.BlockSpec((tm, K), lambda i, j: (i, 0))
        b_spec = pl.BlockSpec((K, tn), lambda i, j: (0, j))
    elif mode == "nt":
        (M, K), N = a.shape, b.shape[0]
        dims = (((1,), (1,)), ((), ()))
        a_spec = pl.BlockSpec((tm, K), lambda i, j: (i, 0))
        b_spec = pl.BlockSpec((tn, K), lambda i, j: (j, 0))
    else:
        (K, M), N = a.shape, b.shape[1]
        dims = (((0,), (0,)), ((), ()))
        a_spec = pl.BlockSpec((K, tm), lambda i, j: (0, i))
        b_spec = pl.BlockSpec((K, tn), lambda i, j: (0, j))
    assert M % tm == 0 and N % tn == 0, (name, M, N, tm, tn)

    def body(a_ref, b_ref, o_ref):
        o_ref[...] = lax.dot_general(a_ref[...].astype(bf16), b_ref[...].astype(bf16), dims,
                                     preferred_element_type=f32).astype(o_ref.dtype)

    if column_shards:
        out_spec, out_shape = pl.BlockSpec((None, tm, tn), lambda i, j: (j, i, 0)), SDS((N // tn, M, tn), out_dtype)
    else:
        out_spec, out_shape = pl.BlockSpec((tm, tn), lambda i, j: (i, j)), SDS((M, N), out_dtype)
    return pl.pallas_call(body, grid=(M // tm, N // tn), in_specs=[a_spec, b_spec], out_specs=out_spec,
                          out_shape=out_shape, name=name, compiler_params=_params("parallel", "arbitrary"))(a, b)


def _row_spec(r, tm):
    if isinstance(r, tuple):
        arr, width, cb = r
        return arr, pl.BlockSpec((tm, width), lambda i, j, cb=cb: (i, cb + j))
    return r, pl.BlockSpec((tm, r.shape[1]), lambda i, j: (i, j))


def _full_spec(p):
    return pl.BlockSpec(p.shape, lambda i, j: (0,) * p.ndim)


def _rows(fn, rows, params, outs, name, tm=ROW_TILE, ncol=1):
    arrs, specs = zip(*[_row_spec(r, tm) for r in rows])
    S = arrs[0].shape[0]
    nr, npar = len(rows), len(params)

    def body(*refs):
        vals = fn(*[r[...].astype(f32) for r in refs[:nr]], *[p[...] for p in refs[nr:nr + npar]])
        for o_ref, v in zip(refs[nr + npar:], vals):
            o_ref[...] = v.astype(o_ref.dtype)

    return pl.pallas_call(
        body, grid=(S // tm, ncol), in_specs=list(specs) + [_full_spec(p) for p in params],
        out_specs=[pl.BlockSpec((tm, w), lambda i, j: (i, j)) for w, _ in outs],
        out_shape=[SDS((S, w * ncol), dt) for w, dt in outs], name=name,
        compiler_params=_params("parallel", "parallel"))(*arrs, *params)


def _rows_vjp(fn, rows, params, cts, wrt_rows, wrt_params, name, adds=None, tm=ROW_TILE, ncol=1, row_dtype=f32):
    adds = adds or {}
    arrs, specs = zip(*[_row_spec(r, tm) for r in rows])
    carrs, cspecs = zip(*[_row_spec(c, tm) for c in cts])
    add_keys = sorted(adds)
    aarrs = [adds[k] for k in add_keys]
    S = arrs[0].shape[0]
    nr, npar, nc, na = len(rows), len(params), len(cts), len(aarrs)
    widths = [specs[k].block_shape[1] for k in wrt_rows]

    def body(*refs):
        first = jnp.logical_and(pl.program_id(0) == 0, pl.program_id(1) == 0)
        rv = [r[...].astype(f32) for r in refs[:nr]]
        pv = [p[...] for p in refs[nr:nr + npar]]
        cv = tuple(c[...].astype(f32) for c in refs[nr + npar:nr + npar + nc])
        av = dict(zip(add_keys, refs[nr + npar + nc:nr + npar + nc + na]))
        o = refs[nr + npar + nc + na:]
        _, vjp = jax.vjp(fn, *rv, *pv)
        g = vjp(cv)
        for n, k in enumerate(wrt_rows):
            val = g[k]
            if k in av:
                val = val + av[k][...]
            o[n][...] = val.astype(o[n].dtype)
        for n, k in enumerate(wrt_params):
            ref = o[len(wrt_rows) + n]

            @pl.when(first)
            def _(ref=ref):
                ref[...] = jnp.zeros_like(ref)

            ref[...] += g[nr + k]

    res = pl.pallas_call(
        body, grid=(S // tm, ncol),
        in_specs=list(specs) + [_full_spec(p) for p in params] + list(cspecs)
        + [pl.BlockSpec((tm, a.shape[1] // ncol), lambda i, j: (i, j)) for a in aarrs],
        out_specs=[pl.BlockSpec((tm, w), lambda i, j: (i, j)) for w in widths] + [_full_spec(params[k]) for k in wrt_params],
        out_shape=[SDS((S, w * ncol), row_dtype) for w in widths] + [SDS(params[k].shape, f32) for k in wrt_params],
        name=name, compiler_params=_params("arbitrary", "arbitrary"))(*arrs, *params, *carrs, *aarrs)
    return res[:len(wrt_rows)], res[len(wrt_rows):]


def _rms(x, w):
    return x * lax.rsqrt(jnp.mean(x * x, axis=-1, keepdims=True) + EPS) * w


def _rms_fn(x, w):
    return (_rms(x, w),)


def _res_rms_fn(f, res, w):
    return (res + _rms(f, w),)


def _swap_halves(x):
    lane = lax.broadcasted_iota(jnp.int32, x.shape, 1)
    first = (lane % HEAD_DIM) < (HEAD_DIM // 2)
    n = x.shape[1]
    return jnp.where(first, pltpu.roll(x, n - HEAD_DIM // 2, 1), pltpu.roll(x, HEAD_DIM // 2, 1))


def _rope_fwd_fn(q, k, cos, sgn_sin):
    scale = HEAD_DIM ** -0.5
    return ((q * cos + _swap_halves(q) * sgn_sin) * scale, k * cos + _swap_halves(k) * sgn_sin)


def _rope_bwd_fn(dq, dk, cos, sgn_sin):
    dq = dq * (HEAD_DIM ** -0.5)
    return (dq * cos + _swap_halves(dq * sgn_sin), dk * cos + _swap_halves(dk * sgn_sin))


def _nt(a, b):
    return lax.dot_general(a, b, (((1,), (1,)), ((), ())), preferred_element_type=f32)


def _tn(a, b):
    return lax.dot_general(a, b, (((0,), (0,)), ((), ())), preferred_element_type=f32)


def _band_rows(j, d, nb):
    r, i = j // nb, j % nb
    if d == 1:
        cur = pl.ds(pl.multiple_of(i * 128, 128), 128)
        prev = pl.ds(pl.multiple_of(jnp.maximum(i - 1, 0) * 128, 128), 128)
    else:
        cur = pl.ds(i * (128 * d) + r, 128, stride=d)
        prev = pl.ds(jnp.maximum(i - 1, 0) * (128 * d) + r, 128, stride=d)
    return cur, prev, (i == 0).astype(jnp.int32)


def _band_bias(bias_ref):
    a = lax.broadcasted_iota(jnp.int32, (256, 256), 0) % 128
    c = lax.broadcasted_iota(jnp.int32, (256, 256), 1)
    own = jnp.logical_and(c < 128, c <= a)
    before = jnp.logical_and(c >= 128, c - 128 >= a)
    bias_ref[0] = jnp.where(jnp.logical_or(own, before), 0.0, NEG)
    bias_ref[1] = jnp.where(own, 0.0, NEG)


def _stack_heads(x, head_a):
    return jnp.concatenate([jnp.where(head_a, x, 0.0), jnp.where(head_a, 0.0, x)], axis=0)


def _unstack_heads(x2, head_a):
    return jnp.where(head_a, x2[:128], x2[128:])


def _attn_fwd(qr, kr, proj, name):
    S = qr.shape[0]
    nblk = S // 128

    def body(q_ref, k_ref, v_ref, out_ref, lse_ref, bias_ref, *scr):
        head_a = lax.broadcasted_iota(jnp.int32, (1, 128), 1) < HEAD_DIM
        _band_bias(bias_ref)
        for b, (d, nb) in enumerate(BRANCHES):
            ob_ref, lb_ref = scr[2 * b], scr[2 * b + 1]

            def blk(j, carry, d=d, nb=nb, ob_ref=ob_ref, lb_ref=lb_ref):
                cur, prev, first = _band_rows(j, d, nb)
                q2 = _stack_heads(q_ref[cur, :], head_a).astype(bf16)
                if nb == 1:
                    k2, v2, bias = k_ref[cur, :].astype(bf16), v_ref[cur, :].astype(bf16), bias_ref[1][:, :128]
                else:
                    k2 = jnp.concatenate([k_ref[cur, :], k_ref[prev, :]], axis=0).astype(bf16)
                    v2 = jnp.concatenate([v_ref[cur, :], v_ref[prev, :]], axis=0).astype(bf16)
                    bias = bias_ref[first]
                s = _nt(q2, k2) + bias
                mx = jnp.max(s, axis=1, keepdims=True)
                p = jnp.exp(s - mx)
                l = jnp.sum(p, axis=1, keepdims=True)
                o = jnp.dot(p.astype(bf16), v2, preferred_element_type=f32) / l
                ob_ref[cur, :] = _unstack_heads(o, head_a)
                lb_ref[cur, :] = _unstack_heads(jnp.broadcast_to(mx + jnp.log(l), (256, 128)), head_a)
                return carry

            lax.fori_loop(0, nblk, blk, 0, unroll=16)
        l0, l1, l2 = scr[1][...], scr[3][...], scr[5][...]
        mx = jnp.maximum(jnp.maximum(l0, l1), l2)
        e0, e1, e2 = jnp.exp(l0 - mx), jnp.exp(l1 - mx), jnp.exp(l2 - mx)
        den = e0 + e1 + e2
        out_ref[...] = ((e0 * scr[0][...] + e1 * scr[2][...] + e2 * scr[4][...]) / den).astype(out_ref.dtype)
        lse_ref[...] = mx + jnp.log(den)

    pair = pl.BlockSpec((S, 128), lambda h: (0, h))
    return pl.pallas_call(
        body, grid=(N_HEADS_A // 2,),
        in_specs=[pair, pair, pl.BlockSpec((S, 128), lambda h: (0, 2 * ATTN_W // 128 + h))], out_specs=[pair, pair],
        out_shape=[SDS((S, ATTN_W), bf16), SDS((S, ATTN_W), f32)],
        scratch_shapes=[pltpu.VMEM((2, 256, 256), f32)] + [pltpu.VMEM((S, 128), f32)] * 6,
        name=name, compiler_params=_params("parallel"))(qr, kr, proj)


def _attn_bwd(qr, kr, proj, dmix_in, out, lse, name):
    S = qr.shape[0]
    nblk = S // 128

    def body(q_ref, k_ref, v_ref, do_ref, out_ref, lse_ref, dq_ref, dk_ref, dv_ref, bias_ref, t_ref):
        head_a = lax.broadcasted_iota(jnp.int32, (1, 128), 1) < HEAD_DIM
        _band_bias(bias_ref)
        x = do_ref[...] * out_ref[...].astype(f32)
        t_ref[...] = jnp.where(head_a, jnp.sum(jnp.where(head_a, x, 0.0), axis=1, keepdims=True),
                               jnp.sum(jnp.where(head_a, 0.0, x), axis=1, keepdims=True))
        dq_ref[...] = jnp.zeros_like(dq_ref)
        dk_ref[...] = jnp.zeros_like(dk_ref)
        dv_ref[...] = jnp.zeros_like(dv_ref)
        for d, nb in BRANCHES:
            def blk(j, carry, d=d, nb=nb):
                cur, prev, first = _band_rows(j, d, nb)
                q2 = _stack_heads(q_ref[cur, :], head_a).astype(bf16)
                do2 = _stack_heads(do_ref[cur, :], head_a).astype(bf16)
                t, lse_b = t_ref[cur, :], lse_ref[cur, :]
                t2 = jnp.concatenate([t[:, :1], t[:, HEAD_DIM:HEAD_DIM + 1]], axis=0)
                lse2 = jnp.concatenate([lse_b[:, :1], lse_b[:, HEAD_DIM:HEAD_DIM + 1]], axis=0)
                if nb == 1:
                    k2, v2, bias = k_ref[cur, :].astype(bf16), v_ref[cur, :].astype(bf16), bias_ref[1][:, :128]
                else:
                    k2 = jnp.concatenate([k_ref[cur, :], k_ref[prev, :]], axis=0).astype(bf16)
                    v2 = jnp.concatenate([v_ref[cur, :], v_ref[prev, :]], axis=0).astype(bf16)
                    bias = bias_ref[first]
                p = jnp.exp(_nt(q2, k2) + bias - lse2)
                ds = (p * (_nt(do2, v2) - t2)).astype(bf16)
                dq_ref[cur, :] += _unstack_heads(jnp.dot(ds, k2, preferred_element_type=f32), head_a)
                dk2, dv2 = _tn(ds, q2), _tn(p.astype(bf16), do2)
                dk_ref[cur, :] += dk2[:128]
                dv_ref[cur, :] += dv2[:128]
                if nb != 1:
                    dk_ref[prev, :] += dk2[128:]
                    dv_ref[prev, :] += dv2[128:]
                return carry

            lax.fori_loop(0, nblk, blk, 0, unroll=8)

    pair = pl.BlockSpec((S, 128), lambda h: (0, h))
    return pl.pallas_call(
        body, grid=(N_HEADS_A // 2,),
        in_specs=[pair, pair, pl.BlockSpec((S, 128), lambda h: (0, 2 * ATTN_W // 128 + h)), pair, pair, pair],
        out_specs=[pair] * 3, out_shape=[SDS((S, ATTN_W), f32)] * 3,
        scratch_shapes=[pltpu.VMEM((2, 256, 256), f32), pltpu.VMEM((S, 128), f32)],
        name=name, compiler_params=_params("parallel"))(qr, kr, proj, dmix_in, out, lse)


def _conv_val(x, w, K, rows):
    acc = x * w[K - 1:K, :]
    for s in range(1, K):
        acc = acc + jnp.where(rows >= s, pltpu.roll(x, s, 0), 0.0) * w[K - 1 - s:K - s, :]
    return acc


def _colconv_fwd(xs, ws, bs, K, fn, nblk, tc, outs, name):
    S = xs[0][0].shape[0]
    n = len(xs)
    has_b = bs is not None

    def body(*refs):
        rows = lax.broadcasted_iota(jnp.int32, (S, tc), 0)
        cs = []
        for k in range(n):
            c = _conv_val(refs[k][...].astype(f32), refs[n + k][...], K, rows)
            if has_b:
                c = c + refs[2 * n + k][...]
            cs.append(c)
        for o_ref, val in zip(refs[(3 if has_b else 2) * n:], fn(*cs)):
            o_ref[...] = val.astype(o_ref.dtype)

    def cspec(rows_, cb0):
        return pl.BlockSpec((rows_, tc), lambda j, cb0=cb0: (0, cb0 + j))

    in_specs = [cspec(S, cb) for _, cb in xs] + [cspec(K, cb) for _, cb in ws]
    args = [a for a, _ in xs] + [a for a, _ in ws]
    if has_b:
        in_specs += [cspec(1, cb) for _, cb in bs]
        args += [a for a, _ in bs]
    return pl.pallas_call(
        body, grid=(nblk,), in_specs=in_specs, out_specs=[cspec(S, 0) for _ in outs],
        out_shape=[SDS((S, nblk * tc), dt) for dt in outs], name=name, compiler_params=_params("parallel"))(*args)


def _colconv_bwd(xs, ws, bs, K, fn, douts, nblk, tc, name, dx_dtype=f32):
    S = xs[0][0].shape[0]
    n, nd = len(xs), len(douts)
    has_b = bs is not None
    nin = (3 if has_b else 2) * n

    def body(*refs):
        rows = lax.broadcasted_iota(jnp.int32, (S, tc), 0)
        x = [refs[k][...].astype(f32) for k in range(n)]
        w = [refs[n + k][...] for k in range(n)]
        cs = []
        for k in range(n):
            c = _conv_val(x[k], w[k], K, rows)
            if has_b:
                c = c + refs[2 * n + k][...]
            cs.append(c)
        _, vjp = jax.vjp(fn, *cs)
        dcs = vjp(tuple(r[...].astype(f32) for r in refs[nin:nin + nd]))
        o = refs[nin + nd:]
        for k in range(n):
            dc = dcs[k]
            dx = dc * w[k][K - 1:K, :]
            o[n + k][K - 1:K, :] = jnp.sum(dc * x[k], axis=0, keepdims=True)
            for s in range(1, K):
                dx = dx + jnp.where(rows < S - s, pltpu.roll(dc, S - s, 0), 0.0) * w[k][K - 1 - s:K - s, :]
                xsh = jnp.where(rows >= s, pltpu.roll(x[k], s, 0), 0.0)
                o[n + k][K - 1 - s:K - s, :] = jnp.sum(dc * xsh, axis=0, keepdims=True)
            o[k][...] = dx.astype(o[k].dtype)
            if has_b:
                o[2 * n + k][...] = jnp.sum(dc, axis=0, keepdims=True)

    def cspec(rows_, cb0):
        return pl.BlockSpec((rows_, tc), lambda j, cb0=cb0: (0, cb0 + j))

    in_specs = [cspec(S, cb) for _, cb in xs] + [cspec(K, cb) for _, cb in ws]
    args = [a for a, _ in xs] + [a for a, _ in ws]
    if has_b:
        in_specs += [cspec(1, cb) for _, cb in bs]
        args += [a for a, _ in bs]
    in_specs += [cspec(S, 0) for _ in douts]
    args += list(douts)
    W = nblk * tc
    out_specs = [cspec(S, 0)] * n + [cspec(K, 0)] * n + ([cspec(1, 0)] * n if has_b else [])
    out_shape = [SDS((S, W), dx_dtype)] * n + [SDS((K, W), f32)] * n + ([SDS((1, W), f32)] * n if has_b else [])
    res = pl.pallas_call(body, grid=(nblk,), in_specs=in_specs, out_specs=out_specs, out_shape=out_shape,
                         name=name, compiler_params=_params("parallel"))(*args)
    return res[:n], res[n:2 * n], res[2 * n:]


def _silu_fn(c):
    return (c * jax.nn.sigmoid(c),)


_GELU_C, _GELU_A = 0.7978845608028654, 0.044715


@jax.custom_vjp
def _geglu(gate, up):
    return 0.5 * gate * (1.0 + jnp.tanh(_GELU_C * (gate + _GELU_A * gate * gate * gate))) * up


def _geglu_vjp_fwd(gate, up):
    return _geglu(gate, up), (gate, up)


def _geglu_vjp_bwd(res, d):
    gate, up = res
    g2 = gate * gate
    t = jnp.tanh(_GELU_C * gate * (1.0 + _GELU_A * g2))
    h = 0.5 * (1.0 + t)
    dgelu = h + (0.5 * _GELU_C) * gate * (1.0 - t * t) * (1.0 + (3.0 * _GELU_A) * g2)
    return d * up * dgelu, d * (gate * h)


_geglu.defvjp(_geglu_vjp_fwd, _geglu_vjp_bwd)


def _geglu_fn(gate, up):
    return (_geglu(gate, up),)


def _softplus(x):
    u = jnp.exp(jnp.minimum(x, 20.0))
    small = u * (1.0 - 0.5 * u)
    return jnp.where(x > 20.0, x, jnp.where(u < 1e-4, small, jnp.log(1.0 + u)))


def _bmm(a, b, precision=None):
    return lax.dot_general(a, b, (((2,), (1,)), ((0,), (0,))), precision=precision, preferred_element_type=f32)


def _bnt(a, b, precision=None):
    return lax.dot_general(a, b, (((2,), (2,)), ((0,), (0,))), precision=precision, preferred_element_type=f32)


def _btn(a, b, precision=None):
    return lax.dot_general(a, b, (((1,), (1,)), ((0,), (0,))), precision=precision, preferred_element_type=f32)


@jax.custom_vjp
def _unit_lower_inverse(A):
    n = A.shape[-1]
    eye = (lax.broadcasted_iota(jnp.int32, (1, n, n), 1) == lax.broadcasted_iota(jnp.int32, (1, n, n), 2)).astype(f32)
    P = -A
    T = eye + P
    for _ in range(5):
        P = _bmm(P, P, HI)
        T = T + _bmm(T, P, HI)
    return T


def _unit_lower_inverse_fwd(A):
    T = _unit_lower_inverse(A)
    return T, T


def _unit_lower_inverse_bwd(T, dT):
    return (-_btn(T, _bnt(dT, T, HI), HI),)


_unit_lower_inverse.defvjp(_unit_lower_inverse_fwd, _unit_lower_inverse_bwd)


def _dn_prep_fn(q, k, v, ba, alog, dtb, h):
    G, C = q.shape[0], CHUNK
    lane = lax.broadcasted_iota(jnp.int32, (1, 1, 128), 2)

    def sel(arr, idx):
        return jnp.sum(jnp.where(lane == idx, arr, 0.0), axis=-1, keepdims=True)

    beta = jax.nn.sigmoid(sel(ba, h))
    g = -jnp.exp(sel(alog[None], h)) * _softplus(sel(ba, N_HEADS_D + h) + sel(dtb[None], h))
    qn = q * lax.rsqrt(jnp.sum(q * q, axis=-1, keepdims=True) + EPS) * (DK ** -0.5)
    kn = k * lax.rsqrt(jnp.sum(k * k, axis=-1, keepdims=True) + EPS)
    ii = lax.broadcasted_iota(jnp.int32, (1, C, C), 1)
    jj = lax.broadcasted_iota(jnp.int32, (1, C, C), 2)
    tril, strict = ii >= jj, ii > jj
    gsq = jnp.broadcast_to(g, (G, C, C))
    gcol = _bmm(jnp.broadcast_to(tril.astype(f32), (G, C, C)), gsq, HI)
    grow = _bmm(jnp.ones((G, C, C), f32), jnp.where(ii <= jj, gsq, 0.0), HI)
    decay = jnp.exp(jnp.where(tril, gcol - grow, NEG))
    gc = gcol[:, :, :1]
    glast = gcol[:, C - 1:C, :1]
    kb = kn * beta
    A = jnp.where(strict, _bnt(kb.astype(bf16), kn.astype(bf16)) * decay, 0.0)
    T = _unit_lower_inverse(A).astype(bf16)
    u = _bmm(T, (v * beta).astype(bf16))
    w = _bmm(T, (kb * jnp.exp(gc)).astype(bf16))
    qk = _bnt(qn.astype(bf16), kn.astype(bf16)) * decay
    qd = qn * jnp.exp(gc)
    kd = kn * jnp.exp(glast - gc)
    return u, w, qk, qd, kd, jnp.broadcast_to(jnp.exp(glast), (G, C, DK))


def _dn_scan_fn(u, w, qk, qd, kd, eg, St):
    b = lambda a: a.astype(bf16)
    vnew = u - _bmm(b(w), b(St))
    o = _bmm(b(qd), b(St)) + _bmm(b(qk), b(vnew))
    return o, St * eg[:, :1, :] + _btn(b(kd), b(vnew))


def _dn_post_fn(o, z, nw):
    return (_rms(o, nw) * (z * jax.nn.sigmoid(z)),)


DN_GROUP = 8


def _dn_prep_specs(S, rows):
    def col(first):
        return pl.BlockSpec((rows, DK), lambda i, h, first=first: (i, first // DK + h))

    par = pl.BlockSpec((1, 128), lambda i, h: (0, 0))
    return [col(0), col(N_HEADS_D * DK), col(2 * N_HEADS_D * DK),
            pl.BlockSpec((rows, 128), lambda i, h: (i, 3584 // 128)), par, par]


def _dn_prep(qkv, proj, alog, dtb, name):
    S = qkv.shape[0]
    G = DN_GROUP
    rows = G * CHUNK

    def body(q_ref, k_ref, v_ref, ba_ref, al_ref, dt_ref, u_ref, w_ref, qk_ref, qd_ref, kd_ref, eg_ref):
        h = pl.program_id(1)
        r3 = lambda ref: ref[...].reshape(G, CHUNK, 128)
        u, w, qk, qd, kd, eg = _dn_prep_fn(r3(q_ref), r3(k_ref), r3(v_ref), r3(ba_ref), al_ref[...], dt_ref[...], h)
        for ref, val in ((u_ref, u), (w_ref, w), (qd_ref, qd), (kd_ref, kd), (eg_ref, eg)):
            ref[...] = val.reshape(rows, DK)
        qk_ref[:, :CHUNK] = qk.reshape(rows, CHUNK)
        qk_ref[:, CHUNK:] = jnp.zeros((rows, DK - CHUNK), f32)

    out = pl.BlockSpec((rows, DK), lambda i, h: (i, h))
    return pl.pallas_call(
        body, grid=(S // rows, N_HEADS_D), in_specs=_dn_prep_specs(S, rows), out_specs=[out] * 6,
        out_shape=[SDS((S, N_HEADS_D * DK), f32)] * 6, name=name,
        compiler_params=_params("parallel", "parallel"))(qkv, qkv, qkv, proj, alog, dtb)


def _dn_prep_bwd(qkv, proj, alog, dtb, cts, name):
    S = qkv.shape[0]
    G = DN_GROUP
    rows = G * CHUNK

    def body(q_ref, k_ref, v_ref, ba_ref, al_ref, dt_ref, du_ref, dw_ref, dqk_ref, dqd_ref, dkd_ref, deg_ref,
             dq_ref, dk_ref, dv_ref, dba_ref, dal_ref, ddt_ref):
        i, h = pl.program_id(0), pl.program_id(1)
        r3 = lambda ref: ref[...].reshape(G, CHUNK, 128)
        _, vjp = jax.vjp(lambda q, k, v, ba, al, dt: _dn_prep_fn(q, k, v, ba, al, dt, h),
                         r3(q_ref), r3(k_ref), r3(v_ref), r3(ba_ref), al_ref[...], dt_ref[...])
        dqk = dqk_ref[:, :CHUNK].reshape(G, CHUNK, CHUNK)
        dq, dk, dv, dba, dal, ddt = vjp((r3(du_ref), r3(dw_ref), dqk, r3(dqd_ref), r3(dkd_ref), r3(deg_ref)))
        dq_ref[...] = dq.reshape(rows, DK)
        dk_ref[...] = dk.reshape(rows, DK)
        dv_ref[...] = dv.reshape(rows, DK)

        @pl.when(h == 0)
        def _():
            dba_ref[...] = jnp.zeros_like(dba_ref)

        @pl.when(jnp.logical_and(i == 0, h == 0))
        def _():
            dal_ref[...] = jnp.zeros_like(dal_ref)
            ddt_ref[...] = jnp.zeros_like(ddt_ref)

        dba_ref[...] += dba.reshape(rows, 128)
        dal_ref[...] += dal
        ddt_ref[...] += ddt

    hcol = pl.BlockSpec((rows, DK), lambda i, h: (i, h))
    par = pl.BlockSpec((1, 128), lambda i, h: (0, 0))
    W = N_HEADS_D * DK
    return pl.pallas_call(
        body, grid=(S // rows, N_HEADS_D), in_specs=_dn_prep_specs(S, rows) + [hcol] * 6,
        out_specs=[hcol] * 3 + [pl.BlockSpec((rows, 128), lambda i, h: (i, 0)), par, par],
        out_shape=[SDS((S, W), f32)] * 3 + [SDS((S, 128), f32), SDS((1, 128), f32), SDS((1, 128), f32)], name=name,
        compiler_params=_params("arbitrary", "arbitrary"))(qkv, qkv, qkv, proj, alog, dtb, *cts)


def _heads(x):
    return jnp.stack([x[:, DK * h:DK * (h + 1)] for h in range(N_HEADS_D)])


SCAN_CHUNKS = 4


def _dn_scan(pre, name):
    S = pre[0].shape[0]
    NCH = S // CHUNK
    rows = SCAN_CHUNKS * CHUNK

    def body(u_ref, w_ref, qk_ref, qd_ref, kd_ref, eg_ref, o_ref, st_ref, s_ref):
        @pl.when(pl.program_id(0) == 0)
        def _():
            s_ref[...] = jnp.zeros_like(s_ref)

        St = s_ref[...]
        for k in range(SCAN_CHUNKS):
            r = slice(k * CHUNK, (k + 1) * CHUNK)
            st_ref[k] = St
            o, St = _dn_scan_fn(_heads(u_ref[r, :]), _heads(w_ref[r, :]), _heads(qk_ref[r, :])[:, :, :CHUNK],
                                _heads(qd_ref[r, :]), _heads(kd_ref[r, :]), _heads(eg_ref[r, :]), St)
            for h in range(N_HEADS_D):
                o_ref[r, DK * h:DK * (h + 1)] = o[h]
        s_ref[...] = St

    blk = pl.BlockSpec((rows, N_HEADS_D * DK), lambda n: (n, 0))
    return pl.pallas_call(
        body, grid=(S // rows,), in_specs=[blk] * 6,
        out_specs=[blk, pl.BlockSpec((SCAN_CHUNKS, N_HEADS_D, DK, DK), lambda n: (n, 0, 0, 0))],
        out_shape=[SDS((S, N_HEADS_D * DK), f32), SDS((NCH, N_HEADS_D, DK, DK), f32)],
        scratch_shapes=[pltpu.VMEM((N_HEADS_D, DK, DK), f32)], name=name, compiler_params=_params("arbitrary"))(*pre)


def _dn_scan_bwd(pre, states, do, name):
    S = do.shape[0]
    rows = SCAN_CHUNKS * CHUNK
    steps = S // rows

    def body(u_ref, w_ref, qk_ref, qd_ref, kd_ref, eg_ref, st_ref, do_ref,
             du_ref, dw_ref, dqk_ref, dqd_ref, dkd_ref, deg_ref, ds_ref):
        @pl.when(pl.program_id(0) == 0)
        def _():
            ds_ref[...] = jnp.zeros_like(ds_ref)

        dS = ds_ref[...]
        for k in reversed(range(SCAN_CHUNKS)):
            r = slice(k * CHUNK, (k + 1) * CHUNK)
            _, vjp = jax.vjp(_dn_scan_fn, _heads(u_ref[r, :]), _heads(w_ref[r, :]), _heads(qk_ref[r, :])[:, :, :CHUNK],
                             _heads(qd_ref[r, :]), _heads(kd_ref[r, :]), _heads(eg_ref[r, :]), st_ref[k])
            du, dw, dqk, dqd, dkd, deg, dS = vjp((_heads(do_ref[r, :]), dS))
            for h in range(N_HEADS_D):
                c = slice(DK * h, DK * (h + 1))
                for ref, val in ((du_ref, du), (dw_ref, dw), (dqd_ref, dqd), (dkd_ref, dkd), (deg_ref, deg)):
                    ref[r, c] = val[h]
                dqk_ref[r, DK * h:DK * h + CHUNK] = dqk[h]
                dqk_ref[r, DK * h + CHUNK:DK * (h + 1)] = jnp.zeros((CHUNK, DK - CHUNK), f32)
        ds_ref[...] = dS

    blk = pl.BlockSpec((rows, N_HEADS_D * DK), lambda n: (steps - 1 - n, 0))
    return pl.pallas_call(
        body, grid=(steps,),
        in_specs=[blk] * 6 + [pl.BlockSpec((SCAN_CHUNKS, N_HEADS_D, DK, DK), lambda n: (steps - 1 - n, 0, 0, 0)), blk],
        out_specs=[blk] * 6, out_shape=[SDS((S, N_HEADS_D * DK), f32)] * 6,
        scratch_shapes=[pltpu.VMEM((N_HEADS_D, DK, DK), f32)], name=name,
        compiler_params=_params("arbitrary"))(*pre, states, do)


def _loss_head(y, t, name):
    S, D = y.shape
    tm = ROW_TILE

    def body(y_ref, t_ref, dy_ref, l_ref):
        i = pl.program_id(0)
        d = y_ref[...] - t_ref[...]
        dy_ref[...] = d * (1.0 / D)
        part = jnp.sum(jnp.sum(d * d, axis=1, keepdims=True), axis=0, keepdims=True) * (0.5 / D)

        @pl.when(i == 0)
        def _():
            l_ref[...] = jnp.zeros_like(l_ref)

        l_ref[...] += jnp.broadcast_to(part, l_ref.shape)

    spec = pl.BlockSpec((tm, D), lambda i: (i, 0))
    dy, l = pl.pallas_call(body, grid=(S // tm,), in_specs=[spec, spec],
                           out_specs=[spec, pl.BlockSpec((1, 128), lambda i: (0, 0))],
                           out_shape=[SDS((S, D), f32), SDS((1, 128), f32)], name=name,
                           compiler_params=_params("arbitrary"))(y, t)
    return l[0, 0], dy


def _adamw(w, g, m, v, tr, name):
    L, R, C = w.shape
    assert R % tr == 0

    def body(w_ref, g_ref, m_ref, v_ref, d_ref, mo_ref, vo_ref):
        gv = g_ref[...]
        m2 = ADAM_B1 * m_ref[...] + (1.0 - ADAM_B1) * gv
        v2 = ADAM_B2 * v_ref[...] + (1.0 - ADAM_B2) * (gv * gv)
        m_hat = m2 / (1.0 - ADAM_B1 ** ADAM_STEP)
        v_hat = v2 / (1.0 - ADAM_B2 ** ADAM_STEP)
        d_ref[...] = -ADAM_LR * (m_hat / (jnp.sqrt(v_hat) + ADAM_EPS) + ADAM_WD * w_ref[...])
        mo_ref[...] = m2
        vo_ref[...] = v2

    spec = pl.BlockSpec((1, tr, C), lambda l, i: (l, i, 0))
    return pl.pallas_call(body, grid=(L, R // tr), in_specs=[spec] * 4, out_specs=[spec] * 3,
                          out_shape=[SDS((L, R, C), f32)] * 3, name=name,
                          compiler_params=_params("parallel", "parallel"))(w, g, m, v)


def _rope_tables(S):
    inv = 1.0 / (10000.0 ** (jnp.arange(0, HEAD_DIM, 2, dtype=f32) / HEAD_DIM))
    ang = jnp.arange(S, dtype=f32)[:, None] * inv[None, :]
    cos, sin = jnp.cos(ang), jnp.sin(ang)
    return (jnp.tile(jnp.concatenate([cos, cos], axis=1), (1, N_HEADS_A)),
            jnp.tile(jnp.concatenate([-sin, sin], axis=1), (1, N_HEADS_A)))


def _layer_fwd(x, W, cos, sgn_sin, l, late_weights=None):
    n = f"l{l}_"
    (h1,) = _rows(_rms_fn, [x], [W["norm_pre_mix"]], [(D_MODEL, bf16)], n + "pre_mix_norm")
    proj = _mm(h1, W["w_in"], "nn", 512, 768, f32, n + "in_proj")
    qr, kr = _rows(_rope_fwd_fn, [(proj, ATTN_W, 0), (proj, ATTN_W, 1), cos, sgn_sin], [],
                   [(ATTN_W, f32), (ATTN_W, f32)], n + "rope")
    attn_out, lse = _attn_fwd(qr, kr, proj, n + "attn_fwd")
    (qkv,) = _colconv_fwd([(proj, 3)], [(W["dn_conv_w"], 0)], None, 4, _silu_fn, 3, 512, [f32], n + "dn_conv")
    dn_pre = _dn_prep(qkv, proj, W["dn_a_log"], W["dn_dt_bias"], n + "dn_prep")
    dn_o, dn_states = _dn_scan(dn_pre, n + "dn_scan")
    (dn_out,) = _rows(_dn_post_fn, [(dn_o, DK, 0), (proj, DK, 3072 // DK)], [W["dn_norm_w"]], [(DK, bf16)], n + "dn_post",
                      ncol=N_HEADS_D)
    mix_in = jnp.concatenate([attn_out, dn_out], axis=1)
    late = late_weights(mix_in) if late_weights is not None else {}
    W = {**W, **late}
    mix = _mm(mix_in, W["w_out"], "nn", 512, 512, f32, n + "out_proj")
    (x1,) = _rows(_res_rms_fn, [mix, x], [W["norm_post_mix"]], [(D_MODEL, f32)], n + "post_mix_norm")
    (h2,) = _rows(_rms_fn, [x1], [W["norm_pre_ffn"]], [(D_MODEL, bf16)], n + "pre_ffn_norm")
    u0 = _mm(h2, W["ffn_w_in"], "nn", 1024, 512, bf16, n + "ffn_in")
    nb_ff = D_FF // 256
    (act,) = _colconv_fwd([(u0, 0), (u0, nb_ff)], [(W["ffn_conv_w"], 0), (W["ffn_conv_w"], nb_ff)],
                          [(W["ffn_conv_b"], 0), (W["ffn_conv_b"], nb_ff)], 3, _geglu_fn, nb_ff, 256, [bf16],
                          n + "ffn_conv_glu")
    f = _mm(act, W["ffn_w_out"], "nn", 512, 512, f32, n + "ffn_out")
    (x2,) = _rows(_res_rms_fn, [f, x1], [W["norm_post_ffn"]], [(D_MODEL, f32)], n + "post_ffn_norm")
    saved = dict(x=x, h1=h1, proj=proj, qr=qr, kr=kr, attn_out=attn_out, lse=lse, qkv=qkv, dn_pre=dn_pre, dn_o=dn_o,
                 dn_states=dn_states, mix_in=mix_in, mix=mix, x1=x1, h2=h2, u0=u0, act=act, f=f, late=late)
    return x2, saved


def _layer_bwd(dx2, sv, W, cos, sgn_sin, l, after_ffn=None):
    n = f"l{l}_"
    S = dx2.shape[0]
    g = {}
    (df,), (g["norm_post_ffn"],) = _rows_vjp(_rms_fn, [sv["f"]], [W["norm_post_ffn"]], [dx2], [0], [0],
                                             n + "post_ffn_norm_bwd", row_dtype=bf16)
    dact = _mm(df, W["ffn_w_out"], "nt", 512, 1408, f32, n + "ffn_out_dx")
    g["ffn_w_out"] = _mm(sv["act"], df, "tn", 256, 1024, f32, n + "ffn_out_dw")
    nb_ff = D_FF // 256
    u0 = sv["u0"]
    dxs, dws, dbs = _colconv_bwd([(u0, 0), (u0, nb_ff)], [(W["ffn_conv_w"], 0), (W["ffn_conv_w"], nb_ff)],
                                 [(W["ffn_conv_b"], 0), (W["ffn_conv_b"], nb_ff)], 3, _geglu_fn, [dact], nb_ff, 256,
                                 n + "ffn_conv_glu_bwd", dx_dtype=bf16)
    du0 = jnp.concatenate(dxs, axis=1)
    g["ffn_conv_w"] = jnp.concatenate(dws, axis=1)
    g["ffn_conv_b"] = jnp.concatenate(dbs, axis=1)
    dh2 = _mm(du0, W["ffn_w_in"], "nt", 512, 512, f32, n + "ffn_in_dx")
    g["ffn_w_in"] = _mm(sv["h2"], du0, "tn", 512, D_FF // 2, f32, n + "ffn_in_dw", column_shards=True)
    (dx1,), (g["norm_pre_ffn"],) = _rows_vjp(_rms_fn, [sv["x1"]], [W["norm_pre_ffn"]], [dh2], [0], [0],
                                             n + "pre_ffn_norm_bwd", adds={0: dx2})
    if after_ffn is not None:
        W = dict(W, norm_post_mix=W["norm_post_mix"] + after_ffn(g, dx1))
    (dmix,), (g["norm_post_mix"],) = _rows_vjp(_rms_fn, [sv["mix"]], [W["norm_post_mix"]], [dx1], [0], [0],
                                               n + "post_mix_norm_bwd", row_dtype=bf16)
    dmix_in = _mm(dmix, W["w_out"], "nt", 512, 512, f32, n + "out_proj_dx")
    g["w_out"] = _mm(sv["mix_in"], dmix, "tn", 512, 512, f32, n + "out_proj_dw")

    (ddn_o, dz), (g["dn_norm_w"],) = _rows_vjp(
        _dn_post_fn, [(sv["dn_o"], DK, 0), (sv["proj"], DK, 3072 // DK)], [W["dn_norm_w"]], [(dmix_in, DK, ATTN_W // DK)],
        [0, 1], [0], n + "dn_post_bwd", ncol=N_HEADS_D)
    dpre = _dn_scan_bwd(sv["dn_pre"], sv["dn_states"], ddn_o, n + "dn_scan_bwd")
    dq, dk, dv, dba, g["dn_a_log"], g["dn_dt_bias"] = _dn_prep_bwd(
        sv["qkv"], sv["proj"], W["dn_a_log"], W["dn_dt_bias"], dpre, n + "dn_prep_bwd")
    dqkv = jnp.concatenate([dq, dk, dv], axis=1)
    (dqkv0,), (g["dn_conv_w"],), _ = _colconv_bwd([(sv["proj"], 3)], [(W["dn_conv_w"], 0)], None, 4, _silu_fn,
                                                 [dqkv], 3, 512, n + "dn_conv_bwd")

    dqr, dkr, dav = _attn_bwd(sv["qr"], sv["kr"], sv["proj"], dmix_in, sv["attn_out"], sv["lse"], n + "attn_bwd")
    daq, dak = _rows(_rope_bwd_fn, [dqr, dkr, cos, sgn_sin], [], [(ATTN_W, f32)] * 2, n + "rope_bwd")
    dproj = jnp.concatenate([daq, dak, dav, dqkv0, dz, dba, jnp.zeros((S, PROJ_W - 3712), f32)], axis=1).astype(bf16)
    dh1 = _mm(dproj, W["w_in"], "nt", 512, 512, f32, n + "in_proj_dx")
    g["w_in"] = _mm(sv["h1"], dproj, "tn", 512, 768, f32, n + "in_proj_dw")
    (dx,), (g["norm_pre_mix"],) = _rows_vjp(_rms_fn, [sv["x"]], [W["norm_pre_mix"]], [dh1], [0], [0],
                                            n + "pre_mix_norm_bwd", adds={0: dx1})
    return dx, g


def _local_step(x, target, layers):
    cos, sgn_sin = _rope_tables(x.shape[0])
    saved = []
    for l, W in enumerate(layers):
        x, sv = _layer_fwd(x, W, cos, sgn_sin, l)
        saved.append(sv)
    loss, dx = _loss_head(x, target, "loss_head")
    grads = [None] * len(layers)
    for l in reversed(range(len(layers))):
        dx, grads[l] = _layer_bwd(dx, saved[l], layers[l], cos, sgn_sin, l)
    return loss, dx, grads


def _pos():
    x, y, c = lax.axis_index("x"), lax.axis_index("y"), lax.axis_index("c")
    return x, y, c, [(1 - x, y), (x, 1 - y), (1 - x, 1 - y)]


def _rcopy(src, dst, send_sem, recv_sem, dev):
    return pltpu.make_async_remote_copy(src_ref=src, dst_ref=dst, send_sem=send_sem, recv_sem=recv_sem,
                                        device_id=dev, device_id_type=MESH)


def _half_rows(ref, h, which, axis):
    if h is None:
        return ref
    rows = pl.ds(pl.multiple_of(which * h, 16), h)
    return ref.at[:, rows, :] if axis == 1 else ref.at[rows, :]


def _dma_sems(*counts):
    return [pltpu.SemaphoreType.DMA((k,)) for k in counts]


def _all_gather(arrs, halves, name):
    n = len(arrs)

    def body(*refs):
        ins, outs = refs[:n], refs[n:2 * n]
        send1, recv1, send2, recv2 = refs[2 * n:]
        x, y, c, chips = _pos()
        me, sib, s_me = (x, y, c), (x, y, 1 - c), 2 * x + y
        sends = []
        for i in range(n):
            for j, chip in enumerate(chips):
                cp = _rcopy(_half_rows(ins[i], halves[i], c, 1), _half_rows(outs[i].at[s_me], halves[i], c, 1),
                            send1.at[3 * i + j], recv1.at[3 * i + j], (*chip, c))
                cp.start()
                sends.append(cp)
        for i in range(n):
            for j, (px, py) in enumerate(chips):
                k = 3 * i + j
                landed = _half_rows(outs[i].at[2 * px + py], halves[i], c, 1)
                _rcopy(landed, landed, send1.at[k], recv1.at[k], me).wait_recv()
                if halves[i] is not None:
                    cp = _rcopy(landed, landed, send2.at[k], recv2.at[k], sib)
                    cp.start()
                    sends.append(cp)
        for i in range(n):
            if halves[i] is None:
                continue
            for j, (px, py) in enumerate(chips):
                k = 3 * i + j
                other = _half_rows(outs[i].at[2 * px + py], halves[i], 1 - c, 1)
                _rcopy(other, other, send2.at[k], recv2.at[k], me).wait_recv()
        for cp in sends:
            cp.wait_send()

    return pl.pallas_call(
        body, in_specs=[ANY] * n, out_specs=[ANY] * n,
        out_shape=[SDS((4,) + a.shape, a.dtype) for a in arrs],
        scratch_shapes=_dma_sems(3 * n, 3 * n, 3 * n, 3 * n), name=name)(*arrs)


HBM = pl.BlockSpec(memory_space=pltpu.HBM)
SEM = pl.BlockSpec(memory_space=pltpu.SEMAPHORE)
_EFFECT = pltpu.SideEffectType.DATAFLOW_SIDE_EFFECTING


def _in_hbm(a):
    return pltpu.with_memory_space_constraint(a, pltpu.HBM)


def _split_copy(srcs, land_shapes, plan, per, after, name):
    n = len(srcs)
    k = per * n

    def body(*refs):
        ins, lands, token = refs[:n], refs[n:2 * n], refs[-1]
        send, recv = refs[2 * n + 1], refs[2 * n + 2]
        for i, (src, dst, dev, _) in enumerate(plan(ins, lands)):
            _rcopy(src, dst, send.at[i], recv.at[i], dev).start()
        token[...] = jnp.zeros_like(token)

    lands = [_in_hbm(lax.empty(s.shape, s.dtype)) for s in land_shapes]
    return pl.pallas_call(
        body, name=name,
        out_shape=(pltpu.SemaphoreType.DMA((k,)), pltpu.SemaphoreType.DMA((k,)),
                   *[pltpu.HBM(a.shape, a.dtype) for a in srcs], *[pltpu.HBM(s.shape, s.dtype) for s in land_shapes],
                   SDS((8, 128), f32)),
        in_specs=[HBM] * (2 * n) + [ANY], out_specs=(SEM, SEM, *[HBM] * (2 * n), pl.BlockSpec(memory_space=pltpu.VMEM)),
        input_output_aliases={i: 2 + i for i in range(2 * n)},
        compiler_params=pltpu.CompilerParams(has_side_effects=_EFFECT))(*[_in_hbm(a) for a in srcs], *lands, after)


def _split_wait(started, n, plan, after, name):
    send, recv = started[0], started[1]
    thru = started[2:2 + 2 * n]

    def body(*refs):
        ins, lands = refs[:n], refs[n:2 * n]
        send_ref, recv_ref = refs[2 * n], refs[2 * n + 1]
        for i, (src, _, dev, mine) in enumerate(plan(ins, lands)):
            cp = _rcopy(src, mine, send_ref.at[i], recv_ref.at[i], dev)
            cp.wait_send()
            cp.wait_recv()

    res = pl.pallas_call(
        body, name=name, out_shape=tuple(pltpu.HBM(a.shape, a.dtype) for a in thru),
        in_specs=[HBM] * (2 * n) + [SEM, SEM, ANY], out_specs=tuple([HBM] * (2 * n)),
        input_output_aliases={i: i for i in range(2 * n)},
        compiler_params=pltpu.CompilerParams(has_side_effects=_EFFECT))(*thru, send, recv, after)
    return res[:n], res[n:]


def _gather_plan(halves):
    def plan(ins, lands):
        x, y, c, chips = _pos()
        out = []
        for i in range(len(ins)):
            for px, py in chips:
                out.append((_half_rows(ins[i], halves[i], c, 1), _half_rows(lands[i].at[2 * x + y], halves[i], c, 1),
                            (px, py, c), _half_rows(lands[i].at[2 * px + py], halves[i], c, 1)))
        return out
    return plan


def _scatter_plan(ins, lands):
    x, y, c, chips = _pos()
    out = []
    for i in range(len(ins)):
        for j, (px, py) in enumerate(chips):
            out.append((ins[i].at[2 * px + py], lands[i].at[j], (px, py, c), lands[i].at[j]))
    return out


def _exchange_plan(ins, lands):
    x, y, c, _ = _pos()
    return [(_half_rows(g, g.shape[1] // 2, 1 - c, 1), land, (x, y, 1 - c), land) for g, land in zip(ins, lands)]


def _pass_to_sibling(lands, halves, name):
    n = len(lands)

    def body(*refs):
        outs = refs[n:2 * n]
        send, recv = refs[2 * n:]
        x, y, c, chips = _pos()
        sends = []
        for i in range(n):
            for j, (px, py) in enumerate(chips):
                landed = _half_rows(outs[i].at[2 * px + py], halves[i], c, 1)
                cp = _rcopy(landed, landed, send.at[3 * i + j], recv.at[3 * i + j], (x, y, 1 - c))
                cp.start()
                sends.append(cp)
        for i in range(n):
            for j, (px, py) in enumerate(chips):
                other = _half_rows(outs[i].at[2 * px + py], halves[i], 1 - c, 1)
                _rcopy(other, other, send.at[3 * i + j], recv.at[3 * i + j], (x, y, c)).wait_recv()
        for cp in sends:
            cp.wait_send()

    return pl.pallas_call(
        body, in_specs=[ANY] * n, out_specs=[ANY] * n, out_shape=[SDS(a.shape, a.dtype) for a in lands],
        input_output_aliases={k: k for k in range(n)}, scratch_shapes=_dma_sems(3 * n, 3 * n), name=name)(*lands)


def _exchange_halves(gs, name):
    n = len(gs)

    def body(*refs):
        ins, outs = refs[:n], refs[n:2 * n]
        send, recv = refs[2 * n:]
        x, y, c, _ = _pos()
        sends = []
        for k in range(n):
            cp = _rcopy(_half_rows(ins[k], gs[k].shape[1] // 2, 1 - c, 1), outs[k], send.at[k], recv.at[k], (x, y, 1 - c))
            cp.start()
            sends.append(cp)
        for k in range(n):
            _rcopy(outs[k], outs[k], send.at[k], recv.at[k], (x, y, c)).wait_recv()
        for cp in sends:
            cp.wait_send()

    return pl.pallas_call(
        body, in_specs=[ANY] * n, out_specs=[ANY] * n,
        out_shape=[SDS((4, g.shape[1] // 2, g.shape[2]), g.dtype) for g in gs],
        scratch_shapes=_dma_sems(n, n), name=name)(*gs)


def _scatter_partials(ps, name):
    n = len(ps)

    def body(*refs):
        ins, outs = refs[:n], refs[n:2 * n]
        send, recv = refs[2 * n:]
        x, y, c, chips = _pos()
        sends = []
        for k in range(n):
            for j, (px, py) in enumerate(chips):
                cp = _rcopy(ins[k].at[2 * px + py], outs[k].at[j], send.at[3 * k + j], recv.at[3 * k + j], (px, py, c))
                cp.start()
                sends.append(cp)
        for k in range(n):
            for j in range(3):
                _rcopy(outs[k].at[j], outs[k].at[j], send.at[3 * k + j], recv.at[3 * k + j], (x, y, c)).wait_recv()
        for cp in sends:
            cp.wait_send()

    return pl.pallas_call(
        body, in_specs=[ANY] * n, out_specs=[ANY] * n,
        out_shape=[SDS((3,) + p.shape[1:], p.dtype) for p in ps],
        scratch_shapes=_dma_sems(3 * n, 3 * n), name=name)(*ps)


def _join_halves(rs, name):
    n = len(rs)

    def body(*refs):
        outs = refs[n:2 * n]
        send, recv = refs[2 * n:]
        x, y, c, _ = _pos()
        sends = []
        for k in range(n):
            mine = _half_rows(outs[k], rs[k].shape[0] // 2, c, 0)
            cp = _rcopy(mine, mine, send.at[k], recv.at[k], (x, y, 1 - c))
            cp.start()
            sends.append(cp)
        for k in range(n):
            other = _half_rows(outs[k], rs[k].shape[0] // 2, 1 - c, 0)
            _rcopy(other, other, send.at[k], recv.at[k], (x, y, c)).wait_recv()
        for cp in sends:
            cp.wait_send()

    return pl.pallas_call(
        body, in_specs=[ANY] * n, out_specs=[ANY] * n, out_shape=[SDS(r.shape, r.dtype) for r in rs],
        input_output_aliases={k: k for k in range(n)}, scratch_shapes=_dma_sems(n, n), name=name)(*rs)


def _all_reduce_small(pack, name):
    R = pack.shape[0]

    def body(in_ref, out_ref, buf, send, recv):
        x, y, c, _ = _pos()
        me = 4 * x + 2 * y + c
        buf[me] = in_ref[...]
        sends = []
        for k in range(1, 8):
            peer = me ^ k
            cp = _rcopy(buf.at[me], buf.at[me], send.at[k - 1], recv.at[k - 1], ((peer >> 2) & 1, (peer >> 1) & 1, peer & 1))
            cp.start()
            sends.append(cp)
        for k in range(1, 8):
            _rcopy(buf.at[me ^ k], buf.at[me ^ k], send.at[k - 1], recv.at[k - 1], (x, y, c)).wait_recv()
        for cp in sends:
            cp.wait_send()
        acc = buf[0]
        for d in range(1, 8):
            acc = acc + buf[d]
        out_ref[...] = acc

    return pl.pallas_call(
        body, out_shape=SDS((R, 128), f32),
        in_specs=[pl.BlockSpec(memory_space=pltpu.VMEM)], out_specs=pl.BlockSpec(memory_space=pltpu.VMEM),
        scratch_shapes=[pltpu.VMEM((8, R, 128), f32)] + _dma_sems(7, 7), name=name)(pack)


def _add_sibling(g, recv, c_arr, tr, name):
    _, R, C = g.shape
    h = R // 2
    nrb = h // tr
    assert h % tr == 0

    def body(c_ref, g_ref, r_ref, o_ref):
        o_ref[...] = (g_ref[...] + r_ref[...]).astype(o_ref.dtype)

    spec = pl.BlockSpec((1, tr, C), lambda s, r, c_ref: (s, r, 0))
    grid_spec = pltpu.PrefetchScalarGridSpec(
        num_scalar_prefetch=1, grid=(4, nrb),
        in_specs=[pl.BlockSpec((1, tr, C), lambda s, r, c_ref: (s, c_ref[0] * nrb + r, 0)), spec], out_specs=spec)
    return pl.pallas_call(body, grid_spec=grid_spec, out_shape=SDS((4, h, C), bf16), name=name,
                          compiler_params=_params("parallel", "parallel"))(c_arr, g, recv)


def _add_chips(p, recv, sc_arr, tr, name):
    _, h, C = p.shape
    nrb = h // tr
    assert h % tr == 0

    def body(sc_ref, p_ref, r_ref, o_ref):
        o_ref[...] = (p_ref[0].astype(f32) + r_ref[0].astype(f32)) + (r_ref[1].astype(f32) + r_ref[2].astype(f32))

    grid_spec = pltpu.PrefetchScalarGridSpec(
        num_scalar_prefetch=1, grid=(nrb,),
        in_specs=[pl.BlockSpec((1, tr, C), lambda r, sc_ref: (sc_ref[0], r, 0)),
                  pl.BlockSpec((3, tr, C), lambda r, sc_ref: (0, r, 0))],
        out_specs=pl.BlockSpec((tr, C), lambda r, sc_ref: (sc_ref[1] * nrb + r, 0)))
    return pl.pallas_call(body, grid_spec=grid_spec, out_shape=SDS((2 * h, C), f32), name=name,
                          compiler_params=_params("parallel"))(sc_arr, p, recv)


_BIG = (("w_in", 1024, 256), ("w_out", 256, 128), ("ffn_w_in", 1024, 256), ("ffn_w_out", 704, 352))
_SMALL = ("dn_conv_w", "ffn_conv_w", "ffn_conv_b", "norm_pre_mix", "norm_post_mix", "norm_pre_ffn", "norm_post_ffn",
          "dn_norm_w", "dn_a_log", "dn_dt_bias")
_WEIGHTS = ("w_in", "dn_conv_w", "dn_a_log", "dn_dt_bias", "dn_norm_w", "w_out", "ffn_w_in", "ffn_conv_w", "ffn_conv_b",
            "ffn_w_out", "norm_pre_mix", "norm_post_mix", "norm_pre_ffn", "norm_post_ffn")
_ADAM_ROWS = {"w_in": 256, "w_out": 256, "ffn_w_in": 128, "ffn_w_out": 176}


def _shard_major(name, g):
    if name == "w_in":
        return jnp.stack([g[:, 898 * s:898 * (s + 1)] for s in range(4)])
    if name == "ffn_w_in":
        return g
    return g.reshape(4, g.shape[0] // 4, g.shape[1])


def kernel(x, w_in, dn_conv_w, dn_a_log, dn_dt_bias, dn_norm_w, w_out, ffn_w_in, ffn_conv_w, ffn_conv_b, ffn_w_out, norm_pre_mix, norm_post_mix, norm_pre_ffn, norm_post_ffn, loss_target, m_w_in, m_dn_conv_w, m_dn_a_log, m_dn_dt_bias, m_dn_norm_w, m_w_out, m_ffn_w_in, m_ffn_conv_w, m_ffn_conv_b, m_ffn_w_out, m_norm_pre_mix, m_norm_post_mix, m_norm_pre_ffn, m_norm_post_ffn, v_w_in, v_dn_conv_w, v_dn_a_log, v_dn_dt_bias, v_dn_norm_w, v_w_out, v_ffn_w_in, v_ffn_conv_w, v_ffn_conv_b, v_ffn_w_out, v_norm_pre_mix, v_norm_post_mix, v_norm_pre_ffn, v_norm_post_ffn):
    w = dict(w_in=w_in, dn_conv_w=dn_conv_w, dn_a_log=dn_a_log, dn_dt_bias=dn_dt_bias, dn_norm_w=dn_norm_w, w_out=w_out,
             ffn_w_in=ffn_w_in, ffn_conv_w=ffn_conv_w, ffn_conv_b=ffn_conv_b, ffn_w_out=ffn_w_out, norm_pre_mix=norm_pre_mix,
             norm_post_mix=norm_post_mix, norm_pre_ffn=norm_pre_ffn, norm_post_ffn=norm_post_ffn)
    m = dict(w_in=m_w_in, dn_conv_w=m_dn_conv_w, dn_a_log=m_dn_a_log, dn_dt_bias=m_dn_dt_bias, dn_norm_w=m_dn_norm_w,
             w_out=m_w_out, ffn_w_in=m_ffn_w_in, ffn_conv_w=m_ffn_conv_w, ffn_conv_b=m_ffn_conv_b, ffn_w_out=m_ffn_w_out,
             norm_pre_mix=m_norm_pre_mix, norm_post_mix=m_norm_post_mix, norm_pre_ffn=m_norm_pre_ffn,
             norm_post_ffn=m_norm_post_ffn)
    v = dict(w_in=v_w_in, dn_conv_w=v_dn_conv_w, dn_a_log=v_dn_a_log, dn_dt_bias=v_dn_dt_bias, dn_norm_w=v_dn_norm_w,
             w_out=v_w_out, ffn_w_in=v_ffn_w_in, ffn_conv_w=v_ffn_conv_w, ffn_conv_b=v_ffn_conv_b, ffn_w_out=v_ffn_w_out,
             norm_pre_mix=v_norm_pre_mix, norm_post_mix=v_norm_post_mix, norm_pre_ffn=v_norm_pre_ffn,
             norm_post_ffn=v_norm_post_ffn)
    xi, yi, ci = lax.axis_index("x"), lax.axis_index("y"), lax.axis_index("c")
    s_me = 2 * xi + yi
    c_arr = jnp.reshape(ci, (1,)).astype(jnp.int32)
    sc_arr = jnp.stack([s_me, ci]).astype(jnp.int32)

    mats = [name for name, _, _ in _BIG]
    rest = mats[1:]
    half_of = {name: rows // 2 for name, rows, _ in _BIG}
    tiles = {name: tr for name, _, tr in _BIG}
    gathered_shape = lambda a: SDS((4,) + a.shape, a.dtype)

    own = {k: w[k].astype(bf16) for k in mats}
    got_in = _all_gather([own["w_in"][0:1], dn_conv_w, ffn_conv_w], [half_of["w_in"], None, None], "weights_gather_w_in0")
    plan0 = _gather_plan([half_of[k] for k in rest])
    src0 = [own[k][0:1] for k in rest]
    started0 = _split_copy(src0, [gathered_shape(a) for a in src0], plan0, 3, got_in[0], "weights_gather_l0_start")
    plan1 = _gather_plan([half_of[k] for k in mats])
    src1 = [own[k][1:2] for k in mats]
    started1 = _split_copy(src1, [gathered_shape(a) for a in src1], plan1, 3, started0[-1], "weights_gather_l1_start")

    def pick(mine, gathered):
        return [jnp.where(s_me == s, mine, gathered[s]) for s in range(4)]

    conv = {"dn_conv_w": jnp.concatenate(pick(dn_conv_w, got_in[1]), axis=-1),
            "ffn_conv_w": jnp.concatenate(pick(ffn_conv_w, got_in[2]), axis=-1)}
    lanes = lambda a: jnp.pad(a, ((0, 0), (0, 128 - a.shape[1])))
    vec = dict(dn_a_log=lanes(dn_a_log), dn_dt_bias=lanes(dn_dt_bias), dn_norm_w=dn_norm_w, ffn_conv_b=ffn_conv_b,
               norm_pre_mix=norm_pre_mix, norm_post_mix=norm_post_mix, norm_pre_ffn=norm_pre_ffn, norm_post_ffn=norm_post_ffn)

    def matrices(l, names, gathered):
        W = {}
        for k, a in zip(names, gathered):
            if k in ("w_out", "ffn_w_out"):
                rows_, cols = own[k].shape[1:]
                W[k] = lax.dynamic_update_slice(a[:, 0], own[k][l][None], (s_me, 0, 0)).reshape(4 * rows_, cols)
            else:
                cat = jnp.concatenate(pick(own[k][l], a[:, 0]), axis=-1)
                W[k] = jnp.pad(cat, ((0, 0), (0, PROJ_W - IN_COLS))) if k == "w_in" else cat
        return W

    def small_weights(l):
        return {**{k: a[l] for k, a in conv.items()}, **{k: a[l:l + 1] for k, a in vec.items()}}

    def late_l0(mix_in):
        _, landed = _split_wait(started0, len(rest), plan0, mix_in, "weights_gather_l0_wait")
        return matrices(0, rest, _pass_to_sibling(landed, [half_of[k] for k in rest], "weights_gather_l0_sibling"))

    cos, sgn_sin = _rope_tables(x.shape[1])
    W0 = {**small_weights(0), **matrices(0, ["w_in"], got_in[:1])}
    W0_first = dict(W0, norm_pre_mix=W0["norm_pre_mix"] + started1[-1][0, 0])
    x1, saved0 = _layer_fwd(x[0], W0_first, cos, sgn_sin, 0, late_weights=late_l0)
    _, landed1 = _split_wait(started1, len(mats), plan1, x1, "weights_gather_l1_wait")
    W1 = {**small_weights(1),
          **matrices(1, mats, _pass_to_sibling(landed1, [half_of[k] for k in mats], "weights_gather_l1_sibling"))}
    x2, saved1 = _layer_fwd(x1, W1, cos, sgn_sin, 1)
    loss_local, dy = _loss_head(x2, loss_target[0], "loss_head")
    loss = lax.psum(loss_local, ("x", "y", "c"))

    def shard_major(names, grads_l):
        return [_shard_major(name, grads_l[name]) for name in names]

    def add_siblings(l, names, gs, from_sib):
        return [_add_sibling(g, r, c_arr, tiles[name], f"add_sibling_{name}{l}") for g, r, name in zip(gs, from_sib, names)]

    def scatter_start(l, names, parts, after, tag):
        return _split_copy(parts, [SDS((3,) + p.shape[1:], p.dtype) for p in parts], _scatter_plan, 3, after,
                           f"grads_l{l}{tag}_scatter_start")

    def owner_sums(l, names, sent, after, tag):
        parts, recvd = _split_wait(sent, len(names), _scatter_plan, after, f"grads_l{l}{tag}_scatter_wait")
        return [_add_chips(p, r, sc_arr, tiles[name], f"add_chips_{name}{l}") for p, r, name in zip(parts, recvd, names)]

    dx1, grads1 = _layer_bwd(dy, saved1, W1, cos, sgn_sin, 1)
    gs1 = shard_major(mats, grads1)
    swap1 = _split_copy(gs1, [SDS((4, g.shape[1] // 2, g.shape[2]), g.dtype) for g in gs1], _exchange_plan, 1, dx1,
                        "grads_l1_sibling_start")
    ffn = ["ffn_w_in", "ffn_w_out"]
    launched = {}

    def after_ffn_l0(g_ffn, dx_mid):
        gs1_, from_sib1 = _split_wait(swap1, len(mats), _exchange_plan, dx_mid, "grads_l1_sibling_wait")
        launched["l1"] = scatter_start(1, mats, add_siblings(1, mats, gs1_, from_sib1), dx_mid, "")
        gs0 = shard_major(ffn, g_ffn)
        from_sib0 = _exchange_halves(gs0, "grads_l0_ffn_to_sibling")
        launched["l0_ffn"] = scatter_start(0, ffn, add_siblings(0, ffn, gs0, from_sib0), launched["l1"][-1], "_ffn")
        return launched["l0_ffn"][-1][0, 0]

    W0_last = dict(W0, **saved0["late"], norm_post_ffn=W0["norm_post_ffn"] + swap1[-1][0, 0])
    dx, grads0 = _layer_bwd(dx1, saved0, W0_last, cos, sgn_sin, 0, after_ffn=after_ffn_l0)
    mix = ["w_in", "w_out"]
    gs0 = shard_major(mix, grads0)
    part0 = add_siblings(0, mix, gs0, _exchange_halves(gs0, "grads_l0_mix_to_sibling"))
    recvd0 = _scatter_partials(part0, "grads_l0_mix_scatter")
    red = dict(zip([(0, k) for k in mix],
                   [_add_chips(p, r, sc_arr, tiles[k], f"add_chips_{k}0") for p, r, k in zip(part0, recvd0, mix)]))
    red.update(zip([(0, k) for k in ffn], owner_sums(0, ffn, launched["l0_ffn"], dx, "_ffn")))
    red.update(zip([(1, k) for k in mats], owner_sums(1, mats, launched["l1"], dx, "")))
    order = [(l, k) for l in range(2) for k in mats]
    joined = dict(zip(order, _join_halves([red[key] for key in order], "grads_join_halves")))
    g_out = {k: jnp.stack([joined[(0, k)], joined[(1, k)]]) for k in mats}
    grads = [grads0, grads1]

    small = {}
    for name in _SMALL:
        per_layer = [grads[l][name] for l in range(2)]
        if name in ("dn_a_log", "dn_dt_bias"):
            per_layer = [p[:, :N_HEADS_D] for p in per_layer]
        small[name] = jnp.stack(per_layer).reshape((2,) + (w[name].shape[1:] if name not in ("dn_conv_w", "ffn_conv_w")
                                                           else per_layer[0].shape))
    flat = jnp.concatenate([small[name].reshape(-1) for name in _SMALL])
    n_rows = -(-flat.shape[0] // 1024) * 8
    summed = _all_reduce_small(jnp.pad(flat, (0, n_rows * 128 - flat.shape[0])).reshape(n_rows, 128),
                               "small_grads_all_reduce").reshape(-1)
    off = 0
    for name in _SMALL:
        size = small[name].size
        g_out[name] = summed[off:off + size].reshape(small[name].shape)
        off += size
    g_out["dn_conv_w"] = lax.dynamic_slice_in_dim(g_out["dn_conv_w"], s_me * 384, 384, axis=2)
    g_out["ffn_conv_w"] = lax.dynamic_slice_in_dim(g_out["ffn_conv_w"], s_me * 1408, 1408, axis=2)

    deltas, new_m, new_v = {}, {}, {}
    for name in _WEIGHTS:
        shape = w[name].shape
        as3 = (lambda a: a) if len(shape) == 3 else (lambda a: a.reshape(shape[0], 1, shape[1]))
        tr = _ADAM_ROWS.get(name, as3(w[name]).shape[1])
        d_, m_, v_ = _adamw(as3(w[name]), as3(g_out[name]), as3(m[name]), as3(v[name]), tr, f"adamw_{name}")
        deltas[name], new_m[name], new_v[name] = d_.reshape(shape), m_.reshape(shape), v_.reshape(shape)

    return (loss, dx[None], *[g_out[k] for k in _WEIGHTS], *[deltas[k] for k in _WEIGHTS],
            *[new_m[k] for k in _WEIGHTS], *[new_v[k] for k in _WEIGHTS])
```

```python
import jax
import jax.numpy as jnp
from jax import lax
from jax.experimental import pallas as pl
from jax.experimental.pallas import tpu as pltpu

f32, bf16 = jnp.float32, jnp.bfloat16
SDS = jax.ShapeDtypeStruct
HI = lax.Precision.HIGH
MESH = pl.DeviceIdType.MESH
ANY = pl.BlockSpec(memory_space=pl.ANY)

D_MODEL = 1024
N_HEADS_A, HEAD_DIM = 8, 64
ATTN_W = 512
N_HEADS_D, DK = 4, 128
CHUNK = 64
D_FF = 2816
IN_COLS = 3592
PROJ_W = 3840
BRANCHES = ((1, 16), (4, 4), (16, 1))
EPS = 1e-6
NEG = -1e30
ROW_TILE = 256
VMEM_LIMIT = 56 * 1024 * 1024

ADAM_LR, ADAM_B1, ADAM_B2, ADAM_EPS, ADAM_WD, ADAM_STEP = 0.001, 0.9, 0.999, 1e-08, 0.01, 10


def _params(*sem):
    return pltpu.CompilerParams(dimension_semantics=sem, vmem_limit_bytes=VMEM_LIMIT)


def _mm(a, b, mode, tm, tn, out_dtype, name, column_shards=False):
    if mode == "nn":
        (M, K), N = a.shape, b.shape[1]
        dims = (((1,), (0,)), ((), ()))
        a_spec = pl.BlockSpec((tm, K), lambda i, j: (i, 0))
        b_spec = pl.BlockSpec((K, tn), lambda i, j: (0, j))
    elif mode == "nt":
        (M, K), N = a.shape, b.shape[0]
        dims = (((1,), (1,)), ((), ()))
        a_spec = pl.BlockSpec((tm, K), lambda i, j: (i, 0))
        b_spec = pl.BlockSpec((tn, K), lambda i, j: (j, 0))
    else:
        (K, M), N = a.shape, b.shape[1]
        dims = (((0,), (0,)), ((), ()))
        a_spec = pl.BlockSpec((K, tm), lambda i, j: (0, i))
        b_spec = pl.BlockSpec((K, tn), lambda i, j: (0, j))
    assert M % tm == 0 and N % tn == 0, (name, M, N, tm, tn)

    def body(a_ref, b_ref, o_ref):
        o_ref[...] = lax.dot_general(a_ref[...].astype(bf16), b_ref[...].astype(bf16), dims,
                                     preferred_element_type=f32).astype(o_ref.dtype)

    if column_shards:
        out_spec, out_shape = pl.BlockSpec((None, tm, tn), lambda i, j: (j, i, 0)), SDS((N // tn, M, tn), out_dtype)
    else:
        out_spec, out_shape = pl.BlockSpec((tm, tn), lambda i, j: (i, j)), SDS((M, N), out_dtype)
    return pl.pallas_call(body, grid=(M // tm, N // tn), in_specs=[a_spec, b_spec], out_specs=out_spec,
                          out_shape=out_shape, name=name, compiler_params=_params("parallel", "arbitrary"))(a, b)


def _row_spec(r, tm):
    if isinstance(r, tuple):
        arr, width, cb = r
        return arr, pl.BlockSpec((tm, width), lambda i, j, cb=cb: (i, cb + j))
    return r, pl.BlockSpec((tm, r.shape[1]), lambda i, j: (i, j))


def _full_spec(p):
    return pl.BlockSpec(p.shape, lambda i, j: (0,) * p.ndim)


def _rows(fn, rows, params, outs, name, tm=ROW_TILE, ncol=1):
    arrs, specs = zip(*[_row_spec(r, tm) for r in rows])
    S = arrs[0].shape[0]
    nr, npar = len(rows), len(params)

    def body(*refs):
        vals = fn(*[r[...].astype(f32) for r in refs[:nr]], *[p[...] for p in refs[nr:nr + npar]])
        for o_ref, v in zip(refs[nr + npar:], vals):
            o_ref[...] = v.astype(o_ref.dtype)

    return pl.pallas_call(
        body, grid=(S // tm, ncol), in_specs=list(specs) + [_full_spec(p) for p in params],
        out_specs=[pl.BlockSpec((tm, w), lambda i, j: (i, j)) for w, _ in outs],
        out_shape=[SDS((S, w * ncol), dt) for w, dt in outs], name=name,
        compiler_params=_params("parallel", "parallel"))(*arrs, *params)


def _rows_vjp(fn, rows, params, cts, wrt_rows, wrt_params, name, adds=None, tm=ROW_TILE, ncol=1, row_dtype=f32):
    adds = adds or {}
    arrs, specs = zip(*[_row_spec(r, tm) for r in rows])
    carrs, cspecs = zip(*[_row_spec(c, tm) for c in cts])
    add_keys = sorted(adds)
    aarrs = [adds[k] for k in add_keys]
    S = arrs[0].shape[0]
    nr, npar, nc, na = len(rows), len(params), len(cts), len(aarrs)
    widths = [specs[k].block_shape[1] for k in wrt_rows]

    def body(*refs):
        first = jnp.logical_and(pl.program_id(0) == 0, pl.program_id(1) == 0)
        rv = [r[...].astype(f32) for r in refs[:nr]]
        pv = [p[...] for p in refs[nr:nr + npar]]
        cv = tuple(c[...].astype(f32) for c in refs[nr + npar:nr + npar + nc])
        av = dict(zip(add_keys, refs[nr + npar + nc:nr + npar + nc + na]))
        o = refs[nr + npar + nc + na:]
        _, vjp = jax.vjp(fn, *rv, *pv)
        g = vjp(cv)
        for n, k in enumerate(wrt_rows):
            val = g[k]
            if k in av:
                val = val + av[k][...]
            o[n][...] = val.astype(o[n].dtype)
        for n, k in enumerate(wrt_params):
            ref = o[len(wrt_rows) + n]

            @pl.when(first)
            def _(ref=ref):
                ref[...] = jnp.zeros_like(ref)

            ref[...] += g[nr + k]

    res = pl.pallas_call(
        body, grid=(S // tm, ncol),
        in_specs=list(specs) + [_full_spec(p) for p in params] + list(cspecs)
        + [pl.BlockSpec((tm, a.shape[1] // ncol), lambda i, j: (i, j)) for a in aarrs],
        out_specs=[pl.BlockSpec((tm, w), lambda i, j: (i, j)) for w in widths] + [_full_spec(params[k]) for k in wrt_params],
        out_shape=[SDS((S, w * ncol), row_dtype) for w in widths] + [SDS(params[k].shape, f32) for k in wrt_params],
        name=name, compiler_params=_params("arbitrary", "arbitrary"))(*arrs, *params, *carrs, *aarrs)
    return res[:len(wrt_rows)], res[len(wrt_rows):]


def _rms(x, w):
    return x * lax.rsqrt(jnp.mean(x * x, axis=-1, keepdims=True) + EPS) * w


def _rms_fn(x, w):
    return (_rms(x, w),)


def _res_rms_fn(f, res, w):
    return (res + _rms(f, w),)


def _swap_halves(x):
    lane = lax.broadcasted_iota(jnp.int32, x.shape, 1)
    first = (lane % HEAD_DIM) < (HEAD_DIM // 2)
    n = x.shape[1]
    return jnp.where(first, pltpu.roll(x, n - HEAD_DIM // 2, 1), pltpu.roll(x, HEAD_DIM // 2, 1))


def _rope_fwd_fn(q, k, cos, sgn_sin):
    scale = HEAD_DIM ** -0.5
    return ((q * cos + _swap_halves(q) * sgn_sin) * scale, k * cos + _swap_halves(k) * sgn_sin)


def _rope_bwd_fn(dq, dk, cos, sgn_sin):
    dq = dq * (HEAD_DIM ** -0.5)
    return (dq * cos + _swap_halves(dq * sgn_sin), dk * cos + _swap_halves(dk * sgn_sin))


def _nt(a, b):
    return lax.dot_general(a, b, (((1,), (1,)), ((), ())), preferred_element_type=f32)


def _tn(a, b):
    return lax.dot_general(a, b, (((0,), (0,)), ((), ())), preferred_element_type=f32)


def _band_rows(j, d, nb):
    r, i = j // nb, j % nb
    if d == 1:
        cur = pl.ds(pl.multiple_of(i * 128, 128), 128)
        prev = pl.ds(pl.multiple_of(jnp.maximum(i - 1, 0) * 128, 128), 128)
    else:
        cur = pl.ds(i * (128 * d) + r, 128, stride=d)
        prev = pl.ds(jnp.maximum(i - 1, 0) * (128 * d) + r, 128, stride=d)
    return cur, prev, (i == 0).astype(jnp.int32)


def _band_bias(bias_ref):
    a = lax.broadcasted_iota(jnp.int32, (256, 256), 0) % 128
    c = lax.broadcasted_iota(jnp.int32, (256, 256), 1)
    own = jnp.logical_and(c < 128, c <= a)
    before = jnp.logical_and(c >= 128, c - 128 >= a)
    bias_ref[0] = jnp.where(jnp.logical_or(own, before), 0.0, NEG)
    bias_ref[1] = jnp.where(own, 0.0, NEG)


def _stack_heads(x, head_a):
    return jnp.concatenate([jnp.where(head_a, x, 0.0), jnp.where(head_a, 0.0, x)], axis=0)


def _unstack_heads(x2, head_a):
    return jnp.where(head_a, x2[:128], x2[128:])


def _attn_fwd(qr, kr, proj, name):
    S = qr.shape[0]
    nblk = S // 128

    def body(q_ref, k_ref, v_ref, out_ref, lse_ref, bias_ref, *scr):
        head_a = lax.broadcasted_iota(jnp.int32, (1, 128), 1) < HEAD_DIM
        _band_bias(bias_ref)
        for b, (d, nb) in enumerate(BRANCHES):
            ob_ref, lb_ref = scr[2 * b], scr[2 * b + 1]

            def blk(j, carry, d=d, nb=nb, ob_ref=ob_ref, lb_ref=lb_ref):
                cur, prev, first = _band_rows(j, d, nb)
                q2 = _stack_heads(q_ref[cur, :], head_a).astype(bf16)
                if nb == 1:
                    k2, v2, bias = k_ref[cur, :].astype(bf16), v_ref[cur, :].astype(bf16), bias_ref[1][:, :128]
                else:
                    k2 = jnp.concatenate([k_ref[cur, :], k_ref[prev, :]], axis=0).astype(bf16)
                    v2 = jnp.concatenate([v_ref[cur, :], v_ref[prev, :]], axis=0).astype(bf16)
                    bias = bias_ref[first]
                s = _nt(q2, k2) + bias
                mx = jnp.max(s, axis=1, keepdims=True)
                p = jnp.exp(s - mx)
                l = jnp.sum(p, axis=1, keepdims=True)
                o = jnp.dot(p.astype(bf16), v2, preferred_element_type=f32) / l
                ob_ref[cur, :] = _unstack_heads(o, head_a)
                lb_ref[cur, :] = _unstack_heads(jnp.broadcast_to(mx + jnp.log(l), (256, 128)), head_a)
                return carry

            lax.fori_loop(0, nblk, blk, 0, unroll=16)
        l0, l1, l2 = scr[1][...], scr[3][...], scr[5][...]
        mx = jnp.maximum(jnp.maximum(l0, l1), l2)
        e0, e1, e2 = jnp.exp(l0 - mx), jnp.exp(l1 - mx), jnp.exp(l2 - mx)
        den = e0 + e1 + e2
        out_ref[...] = ((e0 * scr[0][...] + e1 * scr[2][...] + e2 * scr[4][...]) / den).astype(out_ref.dtype)
        lse_ref[...] = mx + jnp.log(den)

    pair = pl.BlockSpec((S, 128), lambda h: (0, h))
    return pl.pallas_call(
        body, grid=(N_HEADS_A // 2,),
        in_specs=[pair, pair, pl.BlockSpec((S, 128), lambda h: (0, 2 * ATTN_W // 128 + h))], out_specs=[pair, pair],
        out_shape=[SDS((S, ATTN_W), bf16), SDS((S, ATTN_W), f32)],
        scratch_shapes=[pltpu.VMEM((2, 256, 256), f32)] + [pltpu.VMEM((S, 128), f32)] * 6,
        name=name, compiler_params=_params("parallel"))(qr, kr, proj)


def _attn_bwd(qr, kr, proj, dmix_in, out, lse, name):
    S = qr.shape[0]
    nblk = S // 128

    def body(q_ref, k_ref, v_ref, do_ref, out_ref, lse_ref, dq_ref, dk_ref, dv_ref, bias_ref, t_ref):
        head_a = lax.broadcasted_iota(jnp.int32, (1, 128), 1) < HEAD_DIM
        _band_bias(bias_ref)
        x = do_ref[...] * out_ref[...].astype(f32)
        t_ref[...] = jnp.where(head_a, jnp.sum(jnp.where(head_a, x, 0.0), axis=1, keepdims=True),
                               jnp.sum(jnp.where(head_a, 0.0, x), axis=1, keepdims=True))
        dq_ref[...] = jnp.zeros_like(dq_ref)
        dk_ref[...] = jnp.zeros_like(dk_ref)
        dv_ref[...] = jnp.zeros_like(dv_ref)
        for d, nb in BRANCHES:
            def blk(j, carry, d=d, nb=nb):
                cur, prev, first = _band_rows(j, d, nb)
                q2 = _stack_heads(q_ref[cur, :], head_a).astype(bf16)
                do2 = _stack_heads(do_ref[cur, :], head_a).astype(bf16)
                t, lse_b = t_ref[cur, :], lse_ref[cur, :]
                t2 = jnp.concatenate([t[:, :1], t[:, HEAD_DIM:HEAD_DIM + 1]], axis=0)
                lse2 = jnp.concatenate([lse_b[:, :1], lse_b[:, HEAD_DIM:HEAD_DIM + 1]], axis=0)
                if nb == 1:
                    k2, v2, bias = k_ref[cur, :].astype(bf16), v_ref[cur, :].astype(bf16), bias_ref[1][:, :128]
                else:
                    k2 = jnp.concatenate([k_ref[cur, :], k_ref[prev, :]], axis=0).astype(bf16)
                    v2 = jnp.concatenate([v_ref[cur, :], v_ref[prev, :]], axis=0).astype(bf16)
                    bias = bias_ref[first]
                p = jnp.exp(_nt(q2, k2) + bias - lse2)
                ds = (p * (_nt(do2, v2) - t2)).astype(bf16)
                dq_ref[cur, :] += _unstack_heads(jnp.dot(ds, k2, preferred_element_type=f32), head_a)
                dk2, dv2 = _tn(ds, q2), _tn(p.astype(bf16), do2)
                dk_ref[cur, :] += dk2[:128]
                dv_ref[cur, :] += dv2[:128]
                if nb != 1:
                    dk_ref[prev, :] += dk2[128:]
                    dv_ref[prev, :] += dv2[128:]
                return carry

            lax.fori_loop(0, nblk, blk, 0, unroll=8)

    pair = pl.BlockSpec((S, 128), lambda h: (0, h))
    return pl.pallas_call(
        body, grid=(N_HEADS_A // 2,),
        in_specs=[pair, pair, pl.BlockSpec((S, 128), lambda h: (0, 2 * ATTN_W // 128 + h)), pair, pair, pair],
        out_specs=[pair] * 3, out_shape=[SDS((S, ATTN_W), f32)] * 3,
        scratch_shapes=[pltpu.VMEM((2, 256, 256), f32), pltpu.VMEM((S, 128), f32)],
        name=name, compiler_params=_params("parallel"))(qr, kr, proj, dmix_in, out, lse)


def _conv_val(x, w, K, rows):
    acc = x * w[K - 1:K, :]
    for s in range(1, K):
        acc = acc + jnp.where(rows >= s, pltpu.roll(x, s, 0), 0.0) * w[K - 1 - s:K - s, :]
    return acc


def _colconv_fwd(xs, ws, bs, K, fn, nblk, tc, outs, name):
    S = xs[0][0].shape[0]
    n = len(xs)
    has_b = bs is not None

    def body(*refs):
        rows = lax.broadcasted_iota(jnp.int32, (S, tc), 0)
        cs = []
        for k in range(n):
            c = _conv_val(refs[k][...].astype(f32), refs[n + k][...], K, rows)
            if has_b:
                c = c + refs[2 * n + k][...]
            cs.append(c)
        for o_ref, val in zip(refs[(3 if has_b else 2) * n:], fn(*cs)):
            o_ref[...] = val.astype(o_ref.dtype)

    def cspec(rows_, cb0):
        return pl.BlockSpec((rows_, tc), lambda j, cb0=cb0: (0, cb0 + j))

    in_specs = [cspec(S, cb) for _, cb in xs] + [cspec(K, cb) for _, cb in ws]
    args = [a for a, _ in xs] + [a for a, _ in ws]
    if has_b:
        in_specs += [cspec(1, cb) for _, cb in bs]
        args += [a for a, _ in bs]
    return pl.pallas_call(
        body, grid=(nblk,), in_specs=in_specs, out_specs=[cspec(S, 0) for _ in outs],
        out_shape=[SDS((S, nblk * tc), dt) for dt in outs], name=name, compiler_params=_params("parallel"))(*args)


def _colconv_bwd(xs, ws, bs, K, fn, douts, nblk, tc, name, dx_dtype=f32):
    S = xs[0][0].shape[0]
    n, nd = len(xs), len(douts)
    has_b = bs is not None
    nin = (3 if has_b else 2) * n

    def body(*refs):
        rows = lax.broadcasted_iota(jnp.int32, (S, tc), 0)
        x = [refs[k][...].astype(f32) for k in range(n)]
        w = [refs[n + k][...] for k in range(n)]
        cs = []
        for k in range(n):
            c = _conv_val(x[k], w[k], K, rows)
            if has_b:
                c = c + refs[2 * n + k][...]
            cs.append(c)
        _, vjp = jax.vjp(fn, *cs)
        dcs = vjp(tuple(r[...].astype(f32) for r in refs[nin:nin + nd]))
        o = refs[nin + nd:]
        for k in range(n):
            dc = dcs[k]
            dx = dc * w[k][K - 1:K, :]
            o[n + k][K - 1:K, :] = jnp.sum(dc * x[k], axis=0, keepdims=True)
            for s in range(1, K):
                dx = dx + jnp.where(rows < S - s, pltpu.roll(dc, S - s, 0), 0.0) * w[k][K - 1 - s:K - s, :]
                xsh = jnp.where(rows >= s, pltpu.roll(x[k], s, 0), 0.0)
                o[n + k][K - 1 - s:K - s, :] = jnp.sum(dc * xsh, axis=0, keepdims=True)
            o[k][...] = dx.astype(o[k].dtype)
            if has_b:
                o[2 * n + k][...] = jnp.sum(dc, axis=0, keepdims=True)

    def cspec(rows_, cb0):
        return pl.BlockSpec((rows_, tc), lambda j, cb0=cb0: (0, cb0 + j))

    in_specs = [cspec(S, cb) for _, cb in xs] + [cspec(K, cb) for _, cb in ws]
    args = [a for a, _ in xs] + [a for a, _ in ws]
    if has_b:
        in_specs += [cspec(1, cb) for _, cb in bs]
        args += [a for a, _ in bs]
    in_specs += [cspec(S, 0) for _ in douts]
    args += list(douts)
    W = nblk * tc
    out_specs = [cspec(S, 0)] * n + [cspec(K, 0)] * n + ([cspec(1, 0)] * n if has_b else [])
    out_shape = [SDS((S, W), dx_dtype)] * n + [SDS((K, W), f32)] * n + ([SDS((1, W), f32)] * n if has_b else [])
    res = pl.pallas_call(body, grid=(nblk,), in_specs=in_specs, out_specs=out_specs, out_shape=out_shape,
                         name=name, compiler_params=_params("parallel"))(*args)
    return res[:n], res[n:2 * n], res[2 * n:]


def _silu_fn(c):
    return (c * jax.nn.sigmoid(c),)


_GELU_C, _GELU_A = 0.7978845608028654, 0.044715


@jax.custom_vjp
def _geglu(gate, up):
    return 0.5 * gate * (1.0 + jnp.tanh(_GELU_C * (gate + _GELU_A * gate * gate * gate))) * up


def _geglu_vjp_fwd(gate, up):
    return _geglu(gate, up), (gate, up)


def _geglu_vjp_bwd(res, d):
    gate, up = res
    g2 = gate * gate
    t = jnp.tanh(_GELU_C * gate * (1.0 + _GELU_A * g2))
    h = 0.5 * (1.0 + t)
    dgelu = h + (0.5 * _GELU_C) * gate * (1.0 - t * t) * (1.0 + (3.0 * _GELU_A) * g2)
    return d * up * dgelu, d * (gate * h)


_geglu.defvjp(_geglu_vjp_fwd, _geglu_vjp_bwd)


def _geglu_fn(gate, up):
    return (_geglu(gate, up),)


def _softplus(x):
    u = jnp.exp(jnp.minimum(x, 20.0))
    small = u * (1.0 - 0.5 * u)
    return jnp.where(x > 20.0, x, jnp.where(u < 1e-4, small, jnp.log(1.0 + u)))


def _bmm(a, b, precision=None):
    return lax.dot_general(a, b, (((2,), (1,)), ((0,), (0,))), precision=precision, preferred_element_type=f32)


def _bnt(a, b, precision=None):
    return lax.dot_general(a, b, (((2,), (2,)), ((0,), (0,))), precision=precision, preferred_element_type=f32)


def _btn(a, b, precision=None):
    return lax.dot_general(a, b, (((1,), (1,)), ((0,), (0,))), precision=precision, preferred_element_type=f32)


@jax.custom_vjp
def _unit_lower_inverse(A):
    n = A.shape[-1]
    eye = (lax.broadcasted_iota(jnp.int32, (1, n, n), 1) == lax.broadcasted_iota(jnp.int32, (1, n, n), 2)).astype(f32)
    P = -A
    T = eye + P
    for _ in range(5):
        P = _bmm(P, P, HI)
        T = T + _bmm(T, P, HI)
    return T


def _unit_lower_inverse_fwd(A):
    T = _unit_lower_inverse(A)
    return T, T


def _unit_lower_inverse_bwd(T, dT):
    return (-_btn(T, _bnt(dT, T, HI), HI),)


_unit_lower_inverse.defvjp(_unit_lower_inverse_fwd, _unit_lower_inverse_bwd)


def _dn_prep_fn(q, k, v, ba, alog, dtb, h):
    G, C = q.shape[0], CHUNK
    lane = lax.broadcasted_iota(jnp.int32, (1, 1, 128), 2)

    def sel(arr, idx):
        return jnp.sum(jnp.where(lane == idx, arr, 0.0), axis=-1, keepdims=True)

    beta = jax.nn.sigmoid(sel(ba, h))
    g = -jnp.exp(sel(alog[None], h)) * _softplus(sel(ba, N_HEADS_D + h) + sel(dtb[None], h))
    qn = q * lax.rsqrt(jnp.sum(q * q, axis=-1, keepdims=True) + EPS) * (DK ** -0.5)
    kn = k * lax.rsqrt(jnp.sum(k * k, axis=-1, keepdims=True) + EPS)
    ii = lax.broadcasted_iota(jnp.int32, (1, C, C), 1)
    jj = lax.broadcasted_iota(jnp.int32, (1, C, C), 2)
    tril, strict = ii >= jj, ii > jj
    gsq = jnp.broadcast_to(g, (G, C, C))
    gcol = _bmm(jnp.broadcast_to(tril.astype(f32), (G, C, C)), gsq, HI)
    grow = _bmm(jnp.ones((G, C, C), f32), jnp.where(ii <= jj, gsq, 0.0), HI)
    decay = jnp.exp(jnp.where(tril, gcol - grow, NEG))
    gc = gcol[:, :, :1]
    glast = gcol[:, C - 1:C, :1]
    kb = kn * beta
    A = jnp.where(strict, _bnt(kb.astype(bf16), kn.astype(bf16)) * decay, 0.0)
    T = _unit_lower_inverse(A).astype(bf16)
    u = _bmm(T, (v * beta).astype(bf16))
    w = _bmm(T, (kb * jnp.exp(gc)).astype(bf16))
    qk = _bnt(qn.astype(bf16), kn.astype(bf16)) * decay
    qd = qn * jnp.exp(gc)
    kd = kn * jnp.exp(glast - gc)
    return u, w, qk, qd, kd, jnp.broadcast_to(jnp.exp(glast), (G, C, DK))


def _dn_scan_fn(u, w, qk, qd, kd, eg, St):
    b = lambda a: a.astype(bf16)
    vnew = u - _bmm(b(w), b(St))
    o = _bmm(b(qd), b(St)) + _bmm(b(qk), b(vnew))
    return o, St * eg[:, :1, :] + _btn(b(kd), b(vnew))


def _dn_post_fn(o, z, nw):
    return (_rms(o, nw) * (z * jax.nn.sigmoid(z)),)


DN_GROUP = 8


def _dn_prep_specs(S, rows):
    def col(first):
        return pl.BlockSpec((rows, DK), lambda i, h, first=first: (i, first // DK + h))

    par = pl.BlockSpec((1, 128), lambda i, h: (0, 0))
    return [col(0), col(N_HEADS_D * DK), col(2 * N_HEADS_D * DK),
            pl.BlockSpec((rows, 128), lambda i, h: (i, 3584 // 128)), par, par]


def _dn_prep(qkv, proj, alog, dtb, name):
    S = qkv.shape[0]
    G = DN_GROUP
    rows = G * CHUNK

    def body(q_ref, k_ref, v_ref, ba_ref, al_ref, dt_ref, u_ref, w_ref, qk_ref, qd_ref, kd_ref, eg_ref):
        h = pl.program_id(1)
        r3 = lambda ref: ref[...].reshape(G, CHUNK, 128)
        u, w, qk, qd, kd, eg = _dn_prep_fn(r3(q_ref), r3(k_ref), r3(v_ref), r3(ba_ref), al_ref[...], dt_ref[...], h)
        for ref, val in ((u_ref, u), (w_ref, w), (qd_ref, qd), (kd_ref, kd), (eg_ref, eg)):
            ref[...] = val.reshape(rows, DK)
        qk_ref[:, :CHUNK] = qk.reshape(rows, CHUNK)
        qk_ref[:, CHUNK:] = jnp.zeros((rows, DK - CHUNK), f32)

    out = pl.BlockSpec((rows, DK), lambda i, h: (i, h))
    return pl.pallas_call(
        body, grid=(S // rows, N_HEADS_D), in_specs=_dn_prep_specs(S, rows), out_specs=[out] * 6,
        out_shape=[SDS((S, N_HEADS_D * DK), f32)] * 6, name=name,
        compiler_params=_params("parallel", "parallel"))(qkv, qkv, qkv, proj, alog, dtb)


def _dn_prep_bwd(qkv, proj, alog, dtb, cts, name):
    S = qkv.shape[0]
    G = DN_GROUP
    rows = G * CHUNK

    def body(q_ref, k_ref, v_ref, ba_ref, al_ref, dt_ref, du_ref, dw_ref, dqk_ref, dqd_ref, dkd_ref, deg_ref,
             dq_ref, dk_ref, dv_ref, dba_ref, dal_ref, ddt_ref):
        i, h = pl.program_id(0), pl.program_id(1)
        r3 = lambda ref: ref[...].reshape(G, CHUNK, 128)
        _, vjp = jax.vjp(lambda q, k, v, ba, al, dt: _dn_prep_fn(q, k, v, ba, al, dt, h),
                         r3(q_ref), r3(k_ref), r3(v_ref), r3(ba_ref), al_ref[...], dt_ref[...])
        dqk = dqk_ref[:, :CHUNK].reshape(G, CHUNK, CHUNK)
        dq, dk, dv, dba, dal, ddt = vjp((r3(du_ref), r3(dw_ref), dqk, r3(dqd_ref), r3(dkd_ref), r3(deg_ref)))
        dq_ref[...] = dq.reshape(rows, DK)
        dk_ref[...] = dk.reshape(rows, DK)
        dv_ref[...] = dv.reshape(rows, DK)

        @pl.when(h == 0)
        def _():
            dba_ref[...] = jnp.zeros_like(dba_ref)

        @pl.when(jnp.logical_and(i == 0, h == 0))
        def _():
            dal_ref[...] = jnp.zeros_like(dal_ref)
            ddt_ref[...] = jnp.zeros_like(ddt_ref)

        dba_ref[...] += dba.reshape(rows, 128)
        dal_ref[...] += dal
        ddt_ref[...] += ddt

    hcol = pl.BlockSpec((rows, DK), lambda i, h: (i, h))
    par = pl.BlockSpec((1, 128), lambda i, h: (0, 0))
    W = N_HEADS_D * DK
    return pl.pallas_call(
        body, grid=(S // rows, N_HEADS_D), in_specs=_dn_prep_specs(S, rows) + [hcol] * 6,
        out_specs=[hcol] * 3 + [pl.BlockSpec((rows, 128), lambda i, h: (i, 0)), par, par],
        out_shape=[SDS((S, W), f32)] * 3 + [SDS((S, 128), f32), SDS((1, 128), f32), SDS((1, 128), f32)], name=name,
        compiler_params=_params("arbitrary", "arbitrary"))(qkv, qkv, qkv, proj, alog, dtb, *cts)


def _heads(x):
    return jnp.stack([x[:, DK * h:DK * (h + 1)] for h in range(N_HEADS_D)])


SCAN_CHUNKS = 4


def _dn_scan(pre, name):
    S = pre[0].shape[0]
    NCH = S // CHUNK
    rows = SCAN_CHUNKS * CHUNK

    def body(u_ref, w_ref, qk_ref, qd_ref, kd_ref, eg_ref, o_ref, st_ref, s_ref):
        @pl.when(pl.program_id(0) == 0)
        def _():
            s_ref[...] = jnp.zeros_like(s_ref)

        St = s_ref[...]
        for k in range(SCAN_CHUNKS):
            r = slice(k * CHUNK, (k + 1) * CHUNK)
            st_ref[k] = St
            o, St = _dn_scan_fn(_heads(u_ref[r, :]), _heads(w_ref[r, :]), _heads(qk_ref[r, :])[:, :, :CHUNK],
                                _heads(qd_ref[r, :]), _heads(kd_ref[r, :]), _heads(eg_ref[r, :]), St)
            for h in range(N_HEADS_D):
                o_ref[r, DK * h:DK * (h + 1)] = o[h]
        s_ref[...] = St

    blk = pl.BlockSpec((rows, N_HEADS_D * DK), lambda n: (n, 0))
    return pl.pallas_call(
        body, grid=(S // rows,), in_specs=[blk] * 6,
        out_specs=[blk, pl.BlockSpec((SCAN_CHUNKS, N_HEADS_D, DK, DK), lambda n: (n, 0, 0, 0))],
        out_shape=[SDS((S, N_HEADS_D * DK), f32), SDS((NCH, N_HEADS_D, DK, DK), f32)],
        scratch_shapes=[pltpu.VMEM((N_HEADS_D, DK, DK), f32)], name=name, compiler_params=_params("arbitrary"))(*pre)


def _dn_scan_bwd(pre, states, do, name):
    S = do.shape[0]
    rows = SCAN_CHUNKS * CHUNK
    steps = S // rows

    def body(u_ref, w_ref, qk_ref, qd_ref, kd_ref, eg_ref, st_ref, do_ref,
             du_ref, dw_ref, dqk_ref, dqd_ref, dkd_ref, deg_ref, ds_ref):
        @pl.when(pl.program_id(0) == 0)
        def _():
            ds_ref[...] = jnp.zeros_like(ds_ref)

        dS = ds_ref[...]
        for k in reversed(range(SCAN_CHUNKS)):
            r = slice(k * CHUNK, (k + 1) * CHUNK)
            _, vjp = jax.vjp(_dn_scan_fn, _heads(u_ref[r, :]), _heads(w_ref[r, :]), _heads(qk_ref[r, :])[:, :, :CHUNK],
                             _heads(qd_ref[r, :]), _heads(kd_ref[r, :]), _heads(eg_ref[r, :]), st_ref[k])
            du, dw, dqk, dqd, dkd, deg, dS = vjp((_heads(do_ref[r, :]), dS))
            for h in range(N_HEADS_D):
                c = slice(DK * h, DK * (h + 1))
                for ref, val in ((du_ref, du), (dw_ref, dw), (dqd_ref, dqd), (dkd_ref, dkd), (deg_ref, deg)):
                    ref[r, c] = val[h]
                dqk_ref[r, DK * h:DK * h + CHUNK] = dqk[h]
                dqk_ref[r, DK * h + CHUNK:DK * (h + 1)] = jnp.zeros((CHUNK, DK - CHUNK), f32)
        ds_ref[...] = dS

    blk = pl.BlockSpec((rows, N_HEADS_D * DK), lambda n: (steps - 1 - n, 0))
    return pl.pallas_call(
        body, grid=(steps,),
        in_specs=[blk] * 6 + [pl.BlockSpec((SCAN_CHUNKS, N_HEADS_D, DK, DK), lambda n: (steps - 1 - n, 0, 0, 0)), blk],
        out_specs=[blk] * 6, out_shape=[SDS((S, N_HEADS_D * DK), f32)] * 6,
        scratch_shapes=[pltpu.VMEM((N_HEADS_D, DK, DK), f32)], name=name,
        compiler_params=_params("arbitrary"))(*pre, states, do)


def _loss_head(y, t, name):
    S, D = y.shape
    tm = ROW_TILE

    def body(y_ref, t_ref, dy_ref, l_ref):
        i = pl.program_id(0)
        d = y_ref[...] - t_ref[...]
        dy_ref[...] = d * (1.0 / D)
        part = jnp.sum(jnp.sum(d * d, axis=1, keepdims=True), axis=0, keepdims=True) * (0.5 / D)

        @pl.when(i == 0)
        def _():
            l_ref[...] = jnp.zeros_like(l_ref)

        l_ref[...] += jnp.broadcast_to(part, l_ref.shape)

    spec = pl.BlockSpec((tm, D), lambda i: (i, 0))
    dy, l = pl.pallas_call(body, grid=(S // tm,), in_specs=[spec, spec],
                           out_specs=[spec, pl.BlockSpec((1, 128), lambda i: (0, 0))],
                           out_shape=[SDS((S, D), f32), SDS((1, 128), f32)], name=name,
                           compiler_params=_params("arbitrary"))(y, t)
    return l[0, 0], dy


def _adamw(w, g, m, v, tr, name):
    L, R, C = w.shape
    assert R % tr == 0

    def body(w_ref, g_ref, m_ref, v_ref, d_ref, mo_ref, vo_ref):
        gv = g_ref[...]
        m2 = ADAM_B1 * m_ref[...] + (1.0 - ADAM_B1) * gv
        v2 = ADAM_B2 * v_ref[...] + (1.0 - ADAM_B2) * (gv * gv)
        m_hat = m2 / (1.0 - ADAM_B1 ** ADAM_STEP)
        v_hat = v2 / (1.0 - ADAM_B2 ** ADAM_STEP)
        d_ref[...] = -ADAM_LR * (m_hat / (jnp.sqrt(v_hat) + ADAM_EPS) + ADAM_WD * w_ref[...])
        mo_ref[...] = m2
        vo_ref[...] = v2

    spec = pl.BlockSpec((1, tr, C), lambda l, i: (l, i, 0))
    return pl.pallas_call(body, grid=(L, R // tr), in_specs=[spec] * 4, out_specs=[spec] * 3,
                          out_shape=[SDS((L, R, C), f32)] * 3, name=name,
                          compiler_params=_params("parallel", "parallel"))(w, g, m, v)


def _rope_tables(S):
    inv = 1.0 / (10000.0 ** (jnp.arange(0, HEAD_DIM, 2, dtype=f32) / HEAD_DIM))
    ang = jnp.arange(S, dtype=f32)[:, None] * inv[None, :]
    cos, sin = jnp.cos(ang), jnp.sin(ang)
    return (jnp.tile(jnp.concatenate([cos, cos], axis=1), (1, N_HEADS_A)),
            jnp.tile(jnp.concatenate([-sin, sin], axis=1), (1, N_HEADS_A)))


def _layer_fwd(x, W, cos, sgn_sin, l, late_weights=None):
    n = f"l{l}_"
    (h1,) = _rows(_rms_fn, [x], [W["norm_pre_mix"]], [(D_MODEL, bf16)], n + "pre_mix_norm")
    proj = _mm(h1, W["w_in"], "nn", 512, 768, f32, n + "in_proj")
    qr, kr = _rows(_rope_fwd_fn, [(proj, ATTN_W, 0), (proj, ATTN_W, 1), cos, sgn_sin], [],
                   [(ATTN_W, f32), (ATTN_W, f32)], n + "rope")
    attn_out, lse = _attn_fwd(qr, kr, proj, n + "attn_fwd")
    (qkv,) = _colconv_fwd([(proj, 3)], [(W["dn_conv_w"], 0)], None, 4, _silu_fn, 3, 512, [f32], n + "dn_conv")
    dn_pre = _dn_prep(qkv, proj, W["dn_a_log"], W["dn_dt_bias"], n + "dn_prep")
    dn_o, dn_states = _dn_scan(dn_pre, n + "dn_scan")
    (dn_out,) = _rows(_dn_post_fn, [(dn_o, DK, 0), (proj, DK, 3072 // DK)], [W["dn_norm_w"]], [(DK, bf16)], n + "dn_post",
                      ncol=N_HEADS_D, tm=4 * ROW_TILE)
    mix_in = jnp.concatenate([attn_out, dn_out], axis=1)
    late = late_weights(mix_in) if late_weights is not None else {}
    W = {**W, **late}
    mix = _mm(mix_in, W["w_out"], "nn", 512, 512, f32, n + "out_proj")
    (x1,) = _rows(_res_rms_fn, [mix, x], [W["norm_post_mix"]], [(D_MODEL, f32)], n + "post_mix_norm")
    (h2,) = _rows(_rms_fn, [x1], [W["norm_pre_ffn"]], [(D_MODEL, bf16)], n + "pre_ffn_norm")
    u0 = _mm(h2, W["ffn_w_in"], "nn", 1024, 512, bf16, n + "ffn_in")
    nb_ff = D_FF // 256
    (act,) = _colconv_fwd([(u0, 0), (u0, nb_ff)], [(W["ffn_conv_w"], 0), (W["ffn_conv_w"], nb_ff)],
                          [(W["ffn_conv_b"], 0), (W["ffn_conv_b"], nb_ff)], 3, _geglu_fn, nb_ff, 256, [bf16],
                          n + "ffn_conv_glu")
    f = _mm(act, W["ffn_w_out"], "nn", 512, 512, f32, n + "ffn_out")
    (x2,) = _rows(_res_rms_fn, [f, x1], [W["norm_post_ffn"]], [(D_MODEL, f32)], n + "post_ffn_norm")
    saved = dict(x=x, h1=h1, proj=proj, qr=qr, kr=kr, attn_out=attn_out, lse=lse, qkv=qkv, dn_pre=dn_pre, dn_o=dn_o,
                 dn_states=dn_states, mix_in=mix_in, mix=mix, x1=x1, h2=h2, u0=u0, act=act, f=f, late=late)
    return x2, saved


def _layer_bwd(dx2, sv, W, cos, sgn_sin, l, after_ffn=None):
    n = f"l{l}_"
    S = dx2.shape[0]
    g = {}
    (df,), (g["norm_post_ffn"],) = _rows_vjp(_rms_fn, [sv["f"]], [W["norm_post_ffn"]], [dx2], [0], [0],
                                             n + "post_ffn_norm_bwd", row_dtype=bf16)
    dact = _mm(df, W["ffn_w_out"], "nt", 512, 1408, f32, n + "ffn_out_dx")
    g["ffn_w_out"] = _mm(sv["act"], df, "tn", 256, 1024, f32, n + "ffn_out_dw")
    nb_ff = D_FF // 256
    u0 = sv["u0"]
    dxs, dws, dbs = _colconv_bwd([(u0, 0), (u0, nb_ff)], [(W["ffn_conv_w"], 0), (W["ffn_conv_w"], nb_ff)],
                                 [(W["ffn_conv_b"], 0), (W["ffn_conv_b"], nb_ff)], 3, _geglu_fn, [dact], nb_ff, 256,
                                 n + "ffn_conv_glu_bwd", dx_dtype=bf16)
    du0 = jnp.concatenate(dxs, axis=1)
    g["ffn_conv_w"] = jnp.concatenate(dws, axis=1)
    g["ffn_conv_b"] = jnp.concatenate(dbs, axis=1)
    dh2 = _mm(du0, W["ffn_w_in"], "nt", 512, 512, f32, n + "ffn_in_dx")
    g["ffn_w_in"] = _mm(sv["h2"], du0, "tn", 512, D_FF // 2, f32, n + "ffn_in_dw", column_shards=True)
    (dx1,), (g["norm_pre_ffn"],) = _rows_vjp(_rms_fn, [sv["x1"]], [W["norm_pre_ffn"]], [dh2], [0], [0],
                                             n + "pre_ffn_norm_bwd", adds={0: dx2})
    if after_ffn is not None:
        W = dict(W, norm_post_mix=W["norm_post_mix"] + after_ffn(g, dx1))
    (dmix,), (g["norm_post_mix"],) = _rows_vjp(_rms_fn, [sv["mix"]], [W["norm_post_mix"]], [dx1], [0], [0],
                                               n + "post_mix_norm_bwd", row_dtype=bf16)
    dmix_in = _mm(dmix, W["w_out"], "nt", 512, 512, f32, n + "out_proj_dx")
    g["w_out"] = _mm(sv["mix_in"], dmix, "tn", 512, 512, f32, n + "out_proj_dw")

    (ddn_o, dz), (g["dn_norm_w"],) = _rows_vjp(
        _dn_post_fn, [(sv["dn_o"], DK, 0), (sv["proj"], DK, 3072 // DK)], [W["dn_norm_w"]], [(dmix_in, DK, ATTN_W // DK)],
        [0, 1], [0], n + "dn_post_bwd", ncol=N_HEADS_D, tm=4 * ROW_TILE)
    dpre = _dn_scan_bwd(sv["dn_pre"], sv["dn_states"], ddn_o, n + "dn_scan_bwd")
    dq, dk, dv, dba, g["dn_a_log"], g["dn_dt_bias"] = _dn_prep_bwd(
        sv["qkv"], sv["proj"], W["dn_a_log"], W["dn_dt_bias"], dpre, n + "dn_prep_bwd")
    dqkv = jnp.concatenate([dq, dk, dv], axis=1)
    (dqkv0,), (g["dn_conv_w"],), _ = _colconv_bwd([(sv["proj"], 3)], [(W["dn_conv_w"], 0)], None, 4, _silu_fn,
                                                 [dqkv], 3, 512, n + "dn_conv_bwd")

    dqr, dkr, dav = _attn_bwd(sv["qr"], sv["kr"], sv["proj"], dmix_in, sv["attn_out"], sv["lse"], n + "attn_bwd")
    daq, dak = _rows(_rope_bwd_fn, [dqr, dkr, cos, sgn_sin], [], [(ATTN_W, f32)] * 2, n + "rope_bwd")
    dproj = jnp.concatenate([daq, dak, dav, dqkv0, dz, dba, jnp.zeros((S, PROJ_W - 3712), f32)], axis=1).astype(bf16)
    dh1 = _mm(dproj, W["w_in"], "nt", 512, 512, f32, n + "in_proj_dx")
    g["w_in"] = _mm(sv["h1"], dproj, "tn", 512, 768, f32, n + "in_proj_dw")
    (dx,), (g["norm_pre_mix"],) = _rows_vjp(_rms_fn, [sv["x"]], [W["norm_pre_mix"]], [dh1], [0], [0],
                                            n + "pre_mix_norm_bwd", adds={0: dx1})
    return dx, g


def _local_step(x, target, layers):
    cos, sgn_sin = _rope_tables(x.shape[0])
    saved = []
    for l, W in enumerate(layers):
        x, sv = _layer_fwd(x, W, cos, sgn_sin, l)
        saved.append(sv)
    loss, dx = _loss_head(x, target, "loss_head")
    grads = [None] * len(layers)
    for l in reversed(range(len(layers))):
        dx, grads[l] = _layer_bwd(dx, saved[l], layers[l], cos, sgn_sin, l)
    return loss, dx, grads


def _pos():
    x, y, c = lax.axis_index("x"), lax.axis_index("y"), lax.axis_index("c")
    return x, y, c, [(1 - x, y), (x, 1 - y), (1 - x, 1 - y)]


def _rcopy(src, dst, send_sem, recv_sem, dev):
    return pltpu.make_async_remote_copy(src_ref=src, dst_ref=dst, send_sem=send_sem, recv_sem=recv_sem,
                                        device_id=dev, device_id_type=MESH)


def _half_rows(ref, h, which, axis):
    if h is None:
        return ref
    rows = pl.ds(pl.multiple_of(which * h, 16), h)
    return ref.at[:, rows, :] if axis == 1 else ref.at[rows, :]


def _dma_sems(*counts):
    return [pltpu.SemaphoreType.DMA((k,)) for k in counts]


def _all_gather(arrs, halves, name):
    n = len(arrs)

    def body(*refs):
        ins, outs = refs[:n], refs[n:2 * n]
        send1, recv1, send2, recv2 = refs[2 * n:]
        x, y, c, chips = _pos()
        me, sib, s_me = (x, y, c), (x, y, 1 - c), 2 * x + y
        sends = []
        for i in range(n):
            for j, chip in enumerate(chips):
                cp = _rcopy(_half_rows(ins[i], halves[i], c, 1), _half_rows(outs[i].at[s_me], halves[i], c, 1),
                            send1.at[3 * i + j], recv1.at[3 * i + j], (*chip, c))
                cp.start()
                sends.append(cp)
        for i in range(n):
            for j, (px, py) in enumerate(chips):
                k = 3 * i + j
                landed = _half_rows(outs[i].at[2 * px + py], halves[i], c, 1)
                _rcopy(landed, landed, send1.at[k], recv1.at[k], me).wait_recv()
                if halves[i] is not None:
                    cp = _rcopy(landed, landed, send2.at[k], recv2.at[k], sib)
                    cp.start()
                    sends.append(cp)
        for i in range(n):
            if halves[i] is None:
                continue
            for j, (px, py) in enumerate(chips):
                k = 3 * i + j
                other = _half_rows(outs[i].at[2 * px + py], halves[i], 1 - c, 1)
                _rcopy(other, other, send2.at[k], recv2.at[k], me).wait_recv()
        for cp in sends:
            cp.wait_send()

    return pl.pallas_call(
        body, in_specs=[ANY] * n, out_specs=[ANY] * n,
        out_shape=[SDS((4,) + a.shape, a.dtype) for a in arrs],
        scratch_shapes=_dma_sems(3 * n, 3 * n, 3 * n, 3 * n), name=name)(*arrs)


HBM = pl.BlockSpec(memory_space=pltpu.HBM)
SEM = pl.BlockSpec(memory_space=pltpu.SEMAPHORE)
_EFFECT = pltpu.SideEffectType.DATAFLOW_SIDE_EFFECTING


def _in_hbm(a):
    return pltpu.with_memory_space_constraint(a, pltpu.HBM)


def _split_copy(srcs, land_shapes, plan, per, after, name):
    n = len(srcs)
    k = per * n

    def body(*refs):
        ins, lands, token = refs[:n], refs[n:2 * n], refs[-1]
        send, recv = refs[2 * n + 1], refs[2 * n + 2]
        for i, (src, dst, dev, _) in enumerate(plan(ins, lands)):
            _rcopy(src, dst, send.at[i], recv.at[i], dev).start()
        token[...] = jnp.zeros_like(token)

    lands = [_in_hbm(lax.empty(s.shape, s.dtype)) for s in land_shapes]
    return pl.pallas_call(
        body, name=name,
        out_shape=(pltpu.SemaphoreType.DMA((k,)), pltpu.SemaphoreType.DMA((k,)),
                   *[pltpu.HBM(a.shape, a.dtype) for a in srcs], *[pltpu.HBM(s.shape, s.dtype) for s in land_shapes],
                   SDS((8, 128), f32)),
        in_specs=[HBM] * (2 * n) + [ANY], out_specs=(SEM, SEM, *[HBM] * (2 * n), pl.BlockSpec(memory_space=pltpu.VMEM)),
        input_output_aliases={i: 2 + i for i in range(2 * n)},
        compiler_params=pltpu.CompilerParams(has_side_effects=_EFFECT))(*[_in_hbm(a) for a in srcs], *lands, after)


def _split_wait(started, n, plan, after, name):
    send, recv = started[0], started[1]
    thru = started[2:2 + 2 * n]

    def body(*refs):
        ins, lands = refs[:n], refs[n:2 * n]
        send_ref, recv_ref = refs[2 * n], refs[2 * n + 1]
        for i, (src, _, dev, mine) in enumerate(plan(ins, lands)):
            cp = _rcopy(src, mine, send_ref.at[i], recv_ref.at[i], dev)
            cp.wait_send()
            cp.wait_recv()

    res = pl.pallas_call(
        body, name=name, out_shape=tuple(pltpu.HBM(a.shape, a.dtype) for a in thru),
        in_specs=[HBM] * (2 * n) + [SEM, SEM, ANY], out_specs=tuple([HBM] * (2 * n)),
        input_output_aliases={i: i for i in range(2 * n)},
        compiler_params=pltpu.CompilerParams(has_side_effects=_EFFECT))(*thru, send, recv, after)
    return res[:n], res[n:]


def _gather_plan(halves):
    def plan(ins, lands):
        x, y, c, chips = _pos()
        out = []
        for i in range(len(ins)):
            for px, py in chips:
                out.append((_half_rows(ins[i], halves[i], c, 1), _half_rows(lands[i].at[2 * x + y], halves[i], c, 1),
                            (px, py, c), _half_rows(lands[i].at[2 * px + py], halves[i], c, 1)))
        return out
    return plan


def _scatter_plan(ins, lands):
    x, y, c, chips = _pos()
    out = []
    for i in range(len(ins)):
        for j, (px, py) in enumerate(chips):
            out.append((ins[i].at[2 * px + py], lands[i].at[j], (px, py, c), lands[i].at[j]))
    return out


def _exchange_plan(ins, lands):
    x, y, c, _ = _pos()
    return [(_half_rows(g, g.shape[1] // 2, 1 - c, 1), land, (x, y, 1 - c), land) for g, land in zip(ins, lands)]


def _pass_to_sibling(lands, halves, name):
    n = len(lands)

    def body(*refs):
        outs = refs[n:2 * n]
        send, recv = refs[2 * n:]
        x, y, c, chips = _pos()
        sends = []
        for i in range(n):
            for j, (px, py) in enumerate(chips):
                landed = _half_rows(outs[i].at[2 * px + py], halves[i], c, 1)
                cp = _rcopy(landed, landed, send.at[3 * i + j], recv.at[3 * i + j], (x, y, 1 - c))
                cp.start()
                sends.append(cp)
        for i in range(n):
            for j, (px, py) in enumerate(chips):
                other = _half_rows(outs[i].at[2 * px + py], halves[i], 1 - c, 1)
                _rcopy(other, other, send.at[3 * i + j], recv.at[3 * i + j], (x, y, c)).wait_recv()
        for cp in sends:
            cp.wait_send()

    return pl.pallas_call(
        body, in_specs=[ANY] * n, out_specs=[ANY] * n, out_shape=[SDS(a.shape, a.dtype) for a in lands],
        input_output_aliases={k: k for k in range(n)}, scratch_shapes=_dma_sems(3 * n, 3 * n), name=name)(*lands)


def _exchange_halves(gs, name):
    n = len(gs)

    def body(*refs):
        ins, outs = refs[:n], refs[n:2 * n]
        send, recv = refs[2 * n:]
        x, y, c, _ = _pos()
        sends = []
        for k in range(n):
            cp = _rcopy(_half_rows(ins[k], gs[k].shape[1] // 2, 1 - c, 1), outs[k], send.at[k], recv.at[k], (x, y, 1 - c))
            cp.start()
            sends.append(cp)
        for k in range(n):
            _rcopy(outs[k], outs[k], send.at[k], recv.at[k], (x, y, c)).wait_recv()
        for cp in sends:
            cp.wait_send()

    return pl.pallas_call(
        body, in_specs=[ANY] * n, out_specs=[ANY] * n,
        out_shape=[SDS((4, g.shape[1] // 2, g.shape[2]), g.dtype) for g in gs],
        scratch_shapes=_dma_sems(n, n), name=name)(*gs)


def _scatter_partials(ps, name):
    n = len(ps)

    def body(*refs):
        ins, outs = refs[:n], refs[n:2 * n]
        send, recv = refs[2 * n:]
        x, y, c, chips = _pos()
        sends = []
        for k in range(n):
            for j, (px, py) in enumerate(chips):
                cp = _rcopy(ins[k].at[2 * px + py], outs[k].at[j], send.at[3 * k + j], recv.at[3 * k + j], (px, py, c))
                cp.start()
                sends.append(cp)
        for k in range(n):
            for j in range(3):
                _rcopy(outs[k].at[j], outs[k].at[j], send.at[3 * k + j], recv.at[3 * k + j], (x, y, c)).wait_recv()
        for cp in sends:
            cp.wait_send()

    return pl.pallas_call(
        body, in_specs=[ANY] * n, out_specs=[ANY] * n,
        out_shape=[SDS((3,) + p.shape[1:], p.dtype) for p in ps],
        scratch_shapes=_dma_sems(3 * n, 3 * n), name=name)(*ps)


def _join_halves(rs, name):
    n = len(rs)

    def body(*refs):
        outs = refs[n:2 * n]
        send, recv = refs[2 * n:]
        x, y, c, _ = _pos()
        sends = []
        for k in range(n):
            mine = _half_rows(outs[k], rs[k].shape[0] // 2, c, 0)
            cp = _rcopy(mine, mine, send.at[k], recv.at[k], (x, y, 1 - c))
            cp.start()
            sends.append(cp)
        for k in range(n):
            other = _half_rows(outs[k], rs[k].shape[0] // 2, 1 - c, 0)
            _rcopy(other, other, send.at[k], recv.at[k], (x, y, c)).wait_recv()
        for cp in sends:
            cp.wait_send()

    return pl.pallas_call(
        body, in_specs=[ANY] * n, out_specs=[ANY] * n, out_shape=[SDS(r.shape, r.dtype) for r in rs],
        input_output_aliases={k: k for k in range(n)}, scratch_shapes=_dma_sems(n, n), name=name)(*rs)


def _all_reduce_small(pack, name):
    R = pack.shape[0]

    def body(in_ref, out_ref, buf, send, recv):
        x, y, c, _ = _pos()
        me = 4 * x + 2 * y + c
        buf[me] = in_ref[...]
        sends = []
        for k in range(1, 8):
            peer = me ^ k
            cp = _rcopy(buf.at[me], buf.at[me], send.at[k - 1], recv.at[k - 1], ((peer >> 2) & 1, (peer >> 1) & 1, peer & 1))
            cp.start()
            sends.append(cp)
        for k in range(1, 8):
            _rcopy(buf.at[me ^ k], buf.at[me ^ k], send.at[k - 1], recv.at[k - 1], (x, y, c)).wait_recv()
        for cp in sends:
            cp.wait_send()
        acc = buf[0]
        for d in range(1, 8):
            acc = acc + buf[d]
        out_ref[...] = acc

    return pl.pallas_call(
        body, out_shape=SDS((R, 128), f32),
        in_specs=[pl.BlockSpec(memory_space=pltpu.VMEM)], out_specs=pl.BlockSpec(memory_space=pltpu.VMEM),
        scratch_shapes=[pltpu.VMEM((8, R, 128), f32)] + _dma_sems(7, 7), name=name)(pack)


def _add_sibling(g, recv, c_arr, tr, name):
    _, R, C = g.shape
    h = R // 2
    nrb = h // tr
    assert h % tr == 0

    def body(c_ref, g_ref, r_ref, o_ref):
        o_ref[...] = (g_ref[...] + r_ref[...]).astype(o_ref.dtype)

    spec = pl.BlockSpec((1, tr, C), lambda s, r, c_ref: (s, r, 0))
    grid_spec = pltpu.PrefetchScalarGridSpec(
        num_scalar_prefetch=1, grid=(4, nrb),
        in_specs=[pl.BlockSpec((1, tr, C), lambda s, r, c_ref: (s, c_ref[0] * nrb + r, 0)), spec], out_specs=spec)
    return pl.pallas_call(body, grid_spec=grid_spec, out_shape=SDS((4, h, C), bf16), name=name,
                          compiler_params=_params("parallel", "parallel"))(c_arr, g, recv)


def _add_chips(p, recv, sc_arr, tr, name):
    _, h, C = p.shape
    nrb = h // tr
    assert h % tr == 0

    def body(sc_ref, p_ref, r_ref, o_ref):
        o_ref[...] = (p_ref[0].astype(f32) + r_ref[0].astype(f32)) + (r_ref[1].astype(f32) + r_ref[2].astype(f32))

    grid_spec = pltpu.PrefetchScalarGridSpec(
        num_scalar_prefetch=1, grid=(nrb,),
        in_specs=[pl.BlockSpec((1, tr, C), lambda r, sc_ref: (sc_ref[0], r, 0)),
                  pl.BlockSpec((3, tr, C), lambda r, sc_ref: (0, r, 0))],
        out_specs=pl.BlockSpec((tr, C), lambda r, sc_ref: (sc_ref[1] * nrb + r, 0)))
    return pl.pallas_call(body, grid_spec=grid_spec, out_shape=SDS((2 * h, C), f32), name=name,
                          compiler_params=_params("parallel"))(sc_arr, p, recv)


_BIG = (("w_in", 1024, 256), ("w_out", 256, 128), ("ffn_w_in", 1024, 256), ("ffn_w_out", 704, 352))
_SMALL = ("dn_conv_w", "ffn_conv_w", "ffn_conv_b", "norm_pre_mix", "norm_post_mix", "norm_pre_ffn", "norm_post_ffn",
          "dn_norm_w", "dn_a_log", "dn_dt_bias")
_WEIGHTS = ("w_in", "dn_conv_w", "dn_a_log", "dn_dt_bias", "dn_norm_w", "w_out", "ffn_w_in", "ffn_conv_w", "ffn_conv_b",
            "ffn_w_out", "norm_pre_mix", "norm_post_mix", "norm_pre_ffn", "norm_post_ffn")
_ADAM_ROWS = {"w_in": 256, "w_out": 256, "ffn_w_in": 128, "ffn_w_out": 176}


def _shard_major(name, g):
    if name == "w_in":
        return jnp.stack([g[:, 898 * s:898 * (s + 1)] for s in range(4)])
    if name == "ffn_w_in":
        return g
    return g.reshape(4, g.shape[0] // 4, g.shape[1])


def kernel(x, w_in, dn_conv_w, dn_a_log, dn_dt_bias, dn_norm_w, w_out, ffn_w_in, ffn_conv_w, ffn_conv_b, ffn_w_out, norm_pre_mix, norm_post_mix, norm_pre_ffn, norm_post_ffn, loss_target, m_w_in, m_dn_conv_w, m_dn_a_log, m_dn_dt_bias, m_dn_norm_w, m_w_out, m_ffn_w_in, m_ffn_conv_w, m_ffn_conv_b, m_ffn_w_out, m_norm_pre_mix, m_norm_post_mix, m_norm_pre_ffn, m_norm_post_ffn, v_w_in, v_dn_conv_w, v_dn_a_log, v_dn_dt_bias, v_dn_norm_w, v_w_out, v_ffn_w_in, v_ffn_conv_w, v_ffn_conv_b, v_ffn_w_out, v_norm_pre_mix, v_norm_post_mix, v_norm_pre_ffn, v_norm_post_ffn):
    w = dict(w_in=w_in, dn_conv_w=dn_conv_w, dn_a_log=dn_a_log, dn_dt_bias=dn_dt_bias, dn_norm_w=dn_norm_w, w_out=w_out,
             ffn_w_in=ffn_w_in, ffn_conv_w=ffn_conv_w, ffn_conv_b=ffn_conv_b, ffn_w_out=ffn_w_out, norm_pre_mix=norm_pre_mix,
             norm_post_mix=norm_post_mix, norm_pre_ffn=norm_pre_ffn, norm_post_ffn=norm_post_ffn)
    m = dict(w_in=m_w_in, dn_conv_w=m_dn_conv_w, dn_a_log=m_dn_a_log, dn_dt_bias=m_dn_dt_bias, dn_norm_w=m_dn_norm_w,
             w_out=m_w_out, ffn_w_in=m_ffn_w_in, ffn_conv_w=m_ffn_conv_w, ffn_conv_b=m_ffn_conv_b, ffn_w_out=m_ffn_w_out,
             norm_pre_mix=m_norm_pre_mix, norm_post_mix=m_norm_post_mix, norm_pre_ffn=m_norm_pre_ffn,
             norm_post_ffn=m_norm_post_ffn)
    v = dict(w_in=v_w_in, dn_conv_w=v_dn_conv_w, dn_a_log=v_dn_a_log, dn_dt_bias=v_dn_dt_bias, dn_norm_w=v_dn_norm_w,
             w_out=v_w_out, ffn_w_in=v_ffn_w_in, ffn_conv_w=v_ffn_conv_w, ffn_conv_b=v_ffn_conv_b, ffn_w_out=v_ffn_w_out,
             norm_pre_mix=v_norm_pre_mix, norm_post_mix=v_norm_post_mix, norm_pre_ffn=v_norm_pre_ffn,
             norm_post_ffn=v_norm_post_ffn)
    xi, yi, ci = lax.axis_index("x"), lax.axis_index("y"), lax.axis_index("c")
    s_me = 2 * xi + yi
    c_arr = jnp.reshape(ci, (1,)).astype(jnp.int32)
    sc_arr = jnp.stack([s_me, ci]).astype(jnp.int32)

    mats = [name for name, _, _ in _BIG]
    rest = mats[1:]
    half_of = {name: rows // 2 for name, rows, _ in _BIG}
    tiles = {name: tr for name, _, tr in _BIG}
    gathered_shape = lambda a: SDS((4,) + a.shape, a.dtype)

    own = {k: w[k].astype(bf16) for k in mats}
    got_in = _all_gather([own["w_in"][0:1], dn_conv_w, ffn_conv_w], [half_of["w_in"], None, None], "weights_gather_w_in0")
    plan0 = _gather_plan([half_of[k] for k in rest])
    src0 = [own[k][0:1] for k in rest]
    started0 = _split_copy(src0, [gathered_shape(a) for a in src0], plan0, 3, got_in[0], "weights_gather_l0_start")
    plan1 = _gather_plan([half_of[k] for k in mats])
    src1 = [own[k][1:2] for k in mats]
    started1 = _split_copy(src1, [gathered_shape(a) for a in src1], plan1, 3, started0[-1], "weights_gather_l1_start")

    def pick(mine, gathered):
        return [jnp.where(s_me == s, mine, gathered[s]) for s in range(4)]

    conv = {"dn_conv_w": jnp.concatenate(pick(dn_conv_w, got_in[1]), axis=-1),
            "ffn_conv_w": jnp.concatenate(pick(ffn_conv_w, got_in[2]), axis=-1)}
    lanes = lambda a: jnp.pad(a, ((0, 0), (0, 128 - a.shape[1])))
    vec = dict(dn_a_log=lanes(dn_a_log), dn_dt_bias=lanes(dn_dt_bias), dn_norm_w=dn_norm_w, ffn_conv_b=ffn_conv_b,
               norm_pre_mix=norm_pre_mix, norm_post_mix=norm_post_mix, norm_pre_ffn=norm_pre_ffn, norm_post_ffn=norm_post_ffn)

    def matrices(l, names, gathered):
        W = {}
        for k, a in zip(names, gathered):
            if k in ("w_out", "ffn_w_out"):
                rows_, cols = own[k].shape[1:]
                W[k] = lax.dynamic_update_slice(a[:, 0], own[k][l][None], (s_me, 0, 0)).reshape(4 * rows_, cols)
            else:
                cat = jnp.concatenate(pick(own[k][l], a[:, 0]), axis=-1)
                W[k] = jnp.pad(cat, ((0, 0), (0, PROJ_W - IN_COLS))) if k == "w_in" else cat
        return W

    def small_weights(l):
        return {**{k: a[l] for k, a in conv.items()}, **{k: a[l:l + 1] for k, a in vec.items()}}

    def late_l0(mix_in):
        _, landed = _split_wait(started0, len(rest), plan0, mix_in, "weights_gather_l0_wait")
        return matrices(0, rest, _pass_to_sibling(landed, [half_of[k] for k in rest], "weights_gather_l0_sibling"))

    cos, sgn_sin = _rope_tables(x.shape[1])
    W0 = {**small_weights(0), **matrices(0, ["w_in"], got_in[:1])}
    W0_first = dict(W0, norm_pre_mix=W0["norm_pre_mix"] + started1[-1][0, 0])
    x1, saved0 = _layer_fwd(x[0], W0_first, cos, sgn_sin, 0, late_weights=late_l0)
    _, landed1 = _split_wait(started1, len(mats), plan1, x1, "weights_gather_l1_wait")
    W1 = {**small_weights(1),
          **matrices(1, mats, _pass_to_sibling(landed1, [half_of[k] for k in mats], "weights_gather_l1_sibling"))}
    x2, saved1 = _layer_fwd(x1, W1, cos, sgn_sin, 1)
    loss_local, dy = _loss_head(x2, loss_target[0], "loss_head")
    loss = lax.psum(loss_local, ("x", "y", "c"))

    def shard_major(names, grads_l):
        return [_shard_major(name, grads_l[name]) for name in names]

    def add_siblings(l, names, gs, from_sib):
        return [_add_sibling(g, r, c_arr, tiles[name], f"add_sibling_{name}{l}") for g, r, name in zip(gs, from_sib, names)]

    def scatter_start(l, names, parts, after, tag):
        return _split_copy(parts, [SDS((3,) + p.shape[1:], p.dtype) for p in parts], _scatter_plan, 3, after,
                           f"grads_l{l}{tag}_scatter_start")

    def owner_sums(l, names, sent, after, tag):
        parts, recvd = _split_wait(sent, len(names), _scatter_plan, after, f"grads_l{l}{tag}_scatter_wait")
        return [_add_chips(p, r, sc_arr, tiles[name], f"add_chips_{name}{l}") for p, r, name in zip(parts, recvd, names)]

    dx1, grads1 = _layer_bwd(dy, saved1, W1, cos, sgn_sin, 1)
    gs1 = shard_major(mats, grads1)
    swap1 = _split_copy(gs1, [SDS((4, g.shape[1] // 2, g.shape[2]), g.dtype) for g in gs1], _exchange_plan, 1, dx1,
                        "grads_l1_sibling_start")
    ffn = ["ffn_w_in", "ffn_w_out"]
    launched = {}

    def after_ffn_l0(g_ffn, dx_mid):
        gs1_, from_sib1 = _split_wait(swap1, len(mats), _exchange_plan, dx_mid, "grads_l1_sibling_wait")
        launched["l1"] = scatter_start(1, mats, add_siblings(1, mats, gs1_, from_sib1), dx_mid, "")
        gs0 = shard_major(ffn, g_ffn)
        from_sib0 = _exchange_halves(gs0, "grads_l0_ffn_to_sibling")
        launched["l0_ffn"] = scatter_start(0, ffn, add_siblings(0, ffn, gs0, from_sib0), launched["l1"][-1], "_ffn")
        return launched["l0_ffn"][-1][0, 0]

    W0_last = dict(W0, **saved0["late"], norm_post_ffn=W0["norm_post_ffn"] + swap1[-1][0, 0])
    dx, grads0 = _layer_bwd(dx1, saved0, W0_last, cos, sgn_sin, 0, after_ffn=after_ffn_l0)
    mix = ["w_in", "w_out"]
    gs0 = shard_major(mix, grads0)
    part0 = add_siblings(0, mix, gs0, _exchange_halves(gs0, "grads_l0_mix_to_sibling"))
    sent0 = scatter_start(0, mix, part0, dx, "_mix")
    red = dict(zip([(0, k) for k in ffn], owner_sums(0, ffn, launched["l0_ffn"], sent0[-1], "_ffn")))
    red.update(zip([(1, k) for k in mats], owner_sums(1, mats, launched["l1"], sent0[-1], "")))
    early = [(0, k) for k in ffn] + [(1, k) for k in mats]
    joined = dict(zip(early, _join_halves([red[key] for key in early], "grads_join_early")))
    grads = [grads0, grads1]

    small = {}
    for name in _SMALL:
        per_layer = [grads[l][name] for l in range(2)]
        if name in ("dn_a_log", "dn_dt_bias"):
            per_layer = [p[:, :N_HEADS_D] for p in per_layer]
        small[name] = jnp.stack(per_layer).reshape((2,) + (w[name].shape[1:] if name not in ("dn_conv_w", "ffn_conv_w")
                                                           else per_layer[0].shape))
    flat = jnp.concatenate([small[name].reshape(-1) for name in _SMALL])
    n_rows = -(-flat.shape[0] // 1024) * 8
    summed = _all_reduce_small(jnp.pad(flat, (0, n_rows * 128 - flat.shape[0])).reshape(n_rows, 128),
                               "small_grads_all_reduce").reshape(-1)
    off = 0
    g_out = {}
    for name in _SMALL:
        size = small[name].size
        g_out[name] = summed[off:off + size].reshape(small[name].shape)
        off += size
    g_out["dn_conv_w"] = lax.dynamic_slice_in_dim(g_out["dn_conv_w"], s_me * 384, 384, axis=2)
    g_out["ffn_conv_w"] = lax.dynamic_slice_in_dim(g_out["ffn_conv_w"], s_me * 1408, 1408, axis=2)
    for k in ffn:
        g_out[k] = jnp.stack([joined[(0, k)], joined[(1, k)]])

    deltas, new_m, new_v = {}, {}, {}

    def step(name):
        shape = w[name].shape
        as3 = (lambda a: a) if len(shape) == 3 else (lambda a: a.reshape(shape[0], 1, shape[1]))
        tr = _ADAM_ROWS.get(name, as3(w[name]).shape[1])
        d_, m_, v_ = _adamw(as3(w[name]), as3(g_out[name]), as3(m[name]), as3(v[name]), tr, f"adamw_{name}")
        deltas[name], new_m[name], new_v[name] = d_.reshape(shape), m_.reshape(shape), v_.reshape(shape)

    for name in _WEIGHTS:
        if name not in mix:
            step(name)
    done = jnp.reshape(deltas["ffn_w_in"][0, 0, 0] + deltas["ffn_w_out"][0, 0, 0] + deltas["norm_post_ffn"][0, 0], (1,))
    late = _join_halves(owner_sums(0, mix, sent0, done, "_mix"), "grads_join_late")
    for k, a in zip(mix, late):
        g_out[k] = jnp.stack([a, joined[(1, k)]])
        step(k)

    return (loss, dx[None], *[g_out[k] for k in _WEIGHTS], *[deltas[k] for k in _WEIGHTS],
            *[new_m[k] for k in _WEIGHTS], *[new_v[k] for k in _WEIGHTS])
```

```python
import jax
import jax.numpy as jnp
from jax import lax
from jax.experimental import pallas as pl
from jax.experimental.pallas import tpu as pltpu

f32, bf16 = jnp.float32, jnp.bfloat16
SDS = jax.ShapeDtypeStruct
HI = lax.Precision.HIGH
MESH = pl.DeviceIdType.MESH
ANY = pl.BlockSpec(memory_space=pl.ANY)

D_MODEL = 1024
N_HEADS_A, HEAD_DIM = 8, 64
ATTN_W = 512
N_HEADS_D, DK = 4, 128
CHUNK = 64
D_FF = 2816
IN_COLS = 3592
PROJ_W = 3840
BRANCHES = ((1, 16), (4, 4), (16, 1))
EPS = 1e-6
NEG = -1e30
ROW_TILE = 256
VMEM_LIMIT = 56 * 1024 * 1024

ADAM_LR, ADAM_B1, ADAM_B2, ADAM_EPS, ADAM_WD, ADAM_STEP = 0.001, 0.9, 0.999, 1e-08, 0.01, 10


def _params(*sem):
    return pltpu.CompilerParams(dimension_semantics=sem, vmem_limit_bytes=VMEM_LIMIT)


def _mm(a, b, mode, tm, tn, out_dtype, name, column_shards=False):
    if mode == "nn":
        (M, K), N = a.shape, b.shape[1]
        dims = (((1,), (0,)), ((), ()))
        a_spec = pl.BlockSpec((tm, K), lambda i, j: (i, 0))
        b_spec = pl.BlockSpec((K, tn), lambda i, j: (0, j))
    elif mode == "nt":
        (M, K), N = a.shape, b.shape[0]
        dims = (((1,), (1,)), ((), ()))
        a_spec = pl.BlockSpec((tm, K), lambda i, j: (i, 0))
        b_spec = pl.BlockSpec((tn, K), lambda i, j: (j, 0))
    else:
        (K, M), N = a.shape, b.shape[1]
        dims = (((0,), (0,)), ((), ()))
        a_spec = pl.BlockSpec((K, tm), lambda i, j: (0, i))
        b_spec = pl.BlockSpec((K, tn), lambda i, j: (0, j))
    assert M % tm == 0 and N % tn == 0, (name, M, N, tm, tn)

    def body(a_ref, b_ref, o_ref):
        o_ref[...] = lax.dot_general(a_ref[...].astype(bf16), b_ref[...].astype(bf16), dims,
                                     preferred_element_type=f32).astype(o_ref.dtype)

    if column_shards:
        out_spec, out_shape = pl.BlockSpec((None, tm, tn), lambda i, j: (j, i, 0)), SDS((N // tn, M, tn), out_dtype)
    else:
        out_spec, out_shape = pl.BlockSpec((tm, tn), lambda i, j: (i, j)), SDS((M, N), out_dtype)
    return pl.pallas_call(body, grid=(M // tm, N // tn), in_specs=[a_spec, b_spec], out_specs=out_spec,
                          out_shape=out_shape, name=name, compiler_params=_params("parallel", "arbitrary"))(a, b)


def _row_spec(r, tm):
    if isinstance(r, tuple):
        arr, width, cb = r
        return arr, pl.BlockSpec((tm, width), lambda i, j, cb=cb: (i, cb + j))
    return r, pl.BlockSpec((tm, r.shape[1]), lambda i, j: (i, j))


def _full_spec(p):
    return pl.BlockSpec(p.shape, lambda i, j: (0,) * p.ndim)


def _rows(fn, rows, params, outs, name, tm=ROW_TILE, ncol=1):
    arrs, specs = zip(*[_row_spec(r, tm) for r in rows])
    S = arrs[0].shape[0]
    nr, npar = len(rows), len(params)

    def body(*refs):
        vals = fn(*[r[...].astype(f32) for r in refs[:nr]], *[p[...] for p in refs[nr:nr + npar]])
        for o_ref, v in zip(refs[nr + npar:], vals):
            o_ref[...] = v.astype(o_ref.dtype)

    return pl.pallas_call(
        body, grid=(S // tm, ncol), in_specs=list(specs) + [_full_spec(p) for p in params],
        out_specs=[pl.BlockSpec((tm, w), lambda i, j: (i, j)) for w, _ in outs],
        out_shape=[SDS((S, w * ncol), dt) for w, dt in outs], name=name,
        compiler_params=_params("parallel", "parallel"))(*arrs, *params)


def _rows_vjp(fn, rows, params, cts, wrt_rows, wrt_params, name, adds=None, tm=ROW_TILE, ncol=1, row_dtype=f32):
    adds = adds or {}
    arrs, specs = zip(*[_row_spec(r, tm) for r in rows])
    carrs, cspecs = zip(*[_row_spec(c, tm) for c in cts])
    add_keys = sorted(adds)
    aarrs = [adds[k] for k in add_keys]
    S = arrs[0].shape[0]
    nr, npar, nc, na = len(rows), len(params), len(cts), len(aarrs)
    widths = [specs[k].block_shape[1] for k in wrt_rows]
    row_dtypes = row_dtype if isinstance(row_dtype, (list, tuple)) else [row_dtype] * len(wrt_rows)

    def body(*refs):
        first = jnp.logical_and(pl.program_id(0) == 0, pl.program_id(1) == 0)
        rv = [r[...].astype(f32) for r in refs[:nr]]
        pv = [p[...] for p in refs[nr:nr + npar]]
        cv = tuple(c[...].astype(f32) for c in refs[nr + npar:nr + npar + nc])
        av = dict(zip(add_keys, refs[nr + npar + nc:nr + npar + nc + na]))
        o = refs[nr + npar + nc + na:]
        _, vjp = jax.vjp(fn, *rv, *pv)
        g = vjp(cv)
        for n, k in enumerate(wrt_rows):
            val = g[k]
            if k in av:
                val = val + av[k][...]
            o[n][...] = val.astype(o[n].dtype)
        for n, k in enumerate(wrt_params):
            ref = o[len(wrt_rows) + n]

            @pl.when(first)
            def _(ref=ref):
                ref[...] = jnp.zeros_like(ref)

            ref[...] += g[nr + k]

    res = pl.pallas_call(
        body, grid=(S // tm, ncol),
        in_specs=list(specs) + [_full_spec(p) for p in params] + list(cspecs)
        + [pl.BlockSpec((tm, a.shape[1] // ncol), lambda i, j: (i, j)) for a in aarrs],
        out_specs=[pl.BlockSpec((tm, w), lambda i, j: (i, j)) for w in widths] + [_full_spec(params[k]) for k in wrt_params],
        out_shape=[SDS((S, w * ncol), dt) for w, dt in zip(widths, row_dtypes)]
        + [SDS(params[k].shape, f32) for k in wrt_params],
        name=name, compiler_params=_params("arbitrary", "arbitrary"))(*arrs, *params, *carrs, *aarrs)
    return res[:len(wrt_rows)], res[len(wrt_rows):]


def _rms(x, w):
    return x * lax.rsqrt(jnp.mean(x * x, axis=-1, keepdims=True) + EPS) * w


def _rms_fn(x, w):
    return (_rms(x, w),)


def _res_rms_fn(f, res, w):
    return (res + _rms(f, w),)


def _res_rms_pre_fn(f, res, w_post, w_pre):
    x1 = res + _rms(f, w_post)
    return x1, _rms(x1, w_pre)


def _swap_halves(x):
    lane = lax.broadcasted_iota(jnp.int32, x.shape, 1)
    first = (lane % HEAD_DIM) < (HEAD_DIM // 2)
    n = x.shape[1]
    return jnp.where(first, pltpu.roll(x, n - HEAD_DIM // 2, 1), pltpu.roll(x, HEAD_DIM // 2, 1))


def _rope_fwd_fn(q, k, cos, sgn_sin):
    scale = HEAD_DIM ** -0.5
    return ((q * cos + _swap_halves(q) * sgn_sin) * scale, k * cos + _swap_halves(k) * sgn_sin)


def _rope_bwd_fn(dq, dk, cos, sgn_sin):
    dq = dq * (HEAD_DIM ** -0.5)
    return (dq * cos + _swap_halves(dq * sgn_sin), dk * cos + _swap_halves(dk * sgn_sin))


def _nt(a, b):
    return lax.dot_general(a, b, (((1,), (1,)), ((), ())), preferred_element_type=f32)


def _tn(a, b):
    return lax.dot_general(a, b, (((0,), (0,)), ((), ())), preferred_element_type=f32)


def _band_rows(j, d, nb):
    r, i = j // nb, j % nb
    if d == 1:
        cur = pl.ds(pl.multiple_of(i * 128, 128), 128)
        prev = pl.ds(pl.multiple_of(jnp.maximum(i - 1, 0) * 128, 128), 128)
    else:
        cur = pl.ds(i * (128 * d) + r, 128, stride=d)
        prev = pl.ds(jnp.maximum(i - 1, 0) * (128 * d) + r, 128, stride=d)
    return cur, prev, (i == 0).astype(jnp.int32)


def _band_bias(bias_ref):
    a = lax.broadcasted_iota(jnp.int32, (256, 256), 0) % 128
    c = lax.broadcasted_iota(jnp.int32, (256, 256), 1)
    own = jnp.logical_and(c < 128, c <= a)
    before = jnp.logical_and(c >= 128, c - 128 >= a)
    bias_ref[0] = jnp.where(jnp.logical_or(own, before), 0.0, NEG)
    bias_ref[1] = jnp.where(own, 0.0, NEG)


def _stack_heads(x, head_a):
    return jnp.concatenate([jnp.where(head_a, x, 0.0), jnp.where(head_a, 0.0, x)], axis=0)


def _unstack_heads(x2, head_a):
    return jnp.where(head_a, x2[:128], x2[128:])


def _attn_fwd(qr, kr, proj, name):
    S = qr.shape[0]
    nblk = S // 128

    def body(q_ref, k_ref, v_ref, out_ref, lse_ref, bias_ref, *scr):
        head_a = lax.broadcasted_iota(jnp.int32, (1, 128), 1) < HEAD_DIM
        _band_bias(bias_ref)
        for b, (d, nb) in enumerate(BRANCHES):
            ob_ref, lb_ref = scr[2 * b], scr[2 * b + 1]

            def blk(j, carry, d=d, nb=nb, ob_ref=ob_ref, lb_ref=lb_ref):
                cur, prev, first = _band_rows(j, d, nb)
                q2 = _stack_heads(q_ref[cur, :], head_a).astype(bf16)
                if nb == 1:
                    k2, v2, bias = k_ref[cur, :].astype(bf16), v_ref[cur, :].astype(bf16), bias_ref[1][:, :128]
                else:
                    k2 = jnp.concatenate([k_ref[cur, :], k_ref[prev, :]], axis=0).astype(bf16)
                    v2 = jnp.concatenate([v_ref[cur, :], v_ref[prev, :]], axis=0).astype(bf16)
                    bias = bias_ref[first]
                s = _nt(q2, k2) + bias
                mx = jnp.max(s, axis=1, keepdims=True)
                p = jnp.exp(s - mx)
                l = jnp.sum(p, axis=1, keepdims=True)
                o = jnp.dot(p.astype(bf16), v2, preferred_element_type=f32) / l
                ob_ref[cur, :] = _unstack_heads(o, head_a)
                lb_ref[cur, :] = _unstack_heads(jnp.broadcast_to(mx + jnp.log(l), (256, 128)), head_a)
                return carry

            lax.fori_loop(0, nblk, blk, 0, unroll=16)
        l0, l1, l2 = scr[1][...], scr[3][...], scr[5][...]
        mx = jnp.maximum(jnp.maximum(l0, l1), l2)
        e0, e1, e2 = jnp.exp(l0 - mx), jnp.exp(l1 - mx), jnp.exp(l2 - mx)
        den = e0 + e1 + e2
        out_ref[...] = ((e0 * scr[0][...] + e1 * scr[2][...] + e2 * scr[4][...]) / den).astype(out_ref.dtype)
        lse_ref[...] = mx + jnp.log(den)

    pair = pl.BlockSpec((S, 128), lambda h: (0, h))
    return pl.pallas_call(
        body, grid=(N_HEADS_A // 2,),
        in_specs=[pair, pair, pl.BlockSpec((S, 128), lambda h: (0, 2 * ATTN_W // 128 + h))], out_specs=[pair, pair],
        out_shape=[SDS((S, ATTN_W), bf16), SDS((S, ATTN_W), f32)],
        scratch_shapes=[pltpu.VMEM((2, 256, 256), f32)] + [pltpu.VMEM((S, 128), f32)] * 6,
        name=name, compiler_params=_params("parallel"))(qr, kr, proj)


def _attn_bwd(qr, kr, proj, dmix_in, out, lse, name):
    S = qr.shape[0]
    nblk = S // 128

    def body(q_ref, k_ref, v_ref, do_ref, out_ref, lse_ref, dq_ref, dk_ref, dv_ref, bias_ref, t_ref):
        head_a = lax.broadcasted_iota(jnp.int32, (1, 128), 1) < HEAD_DIM
        _band_bias(bias_ref)
        x = do_ref[...] * out_ref[...].astype(f32)
        t_ref[...] = jnp.where(head_a, jnp.sum(jnp.where(head_a, x, 0.0), axis=1, keepdims=True),
                               jnp.sum(jnp.where(head_a, 0.0, x), axis=1, keepdims=True))
        dq_ref[...] = jnp.zeros_like(dq_ref)
        dk_ref[...] = jnp.zeros_like(dk_ref)
        dv_ref[...] = jnp.zeros_like(dv_ref)
        for d, nb in BRANCHES:
            def blk(j, carry, d=d, nb=nb):
                cur, prev, first = _band_rows(j, d, nb)
                q2 = _stack_heads(q_ref[cur, :], head_a).astype(bf16)
                do2 = _stack_heads(do_ref[cur, :], head_a).astype(bf16)
                t, lse_b = t_ref[cur, :], lse_ref[cur, :]
                t2 = jnp.concatenate([t[:, :1], t[:, HEAD_DIM:HEAD_DIM + 1]], axis=0)
                lse2 = jnp.concatenate([lse_b[:, :1], lse_b[:, HEAD_DIM:HEAD_DIM + 1]], axis=0)
                if nb == 1:
                    k2, v2, bias = k_ref[cur, :].astype(bf16), v_ref[cur, :].astype(bf16), bias_ref[1][:, :128]
                else:
                    k2 = jnp.concatenate([k_ref[cur, :], k_ref[prev, :]], axis=0).astype(bf16)
                    v2 = jnp.concatenate([v_ref[cur, :], v_ref[prev, :]], axis=0).astype(bf16)
                    bias = bias_ref[first]
                p = jnp.exp(_nt(q2, k2) + bias - lse2)
                ds = (p * (_nt(do2, v2) - t2)).astype(bf16)
                dq_ref[cur, :] += _unstack_heads(jnp.dot(ds, k2, preferred_element_type=f32), head_a)
                dk2, dv2 = _tn(ds, q2), _tn(p.astype(bf16), do2)
                dk_ref[cur, :] += dk2[:128]
                dv_ref[cur, :] += dv2[:128]
                if nb != 1:
                    dk_ref[prev, :] += dk2[128:]
                    dv_ref[prev, :] += dv2[128:]
                return carry

            lax.fori_loop(0, nblk, blk, 0, unroll=8)

    pair = pl.BlockSpec((S, 128), lambda h: (0, h))
    return pl.pallas_call(
        body, grid=(N_HEADS_A // 2,),
        in_specs=[pair, pair, pl.BlockSpec((S, 128), lambda h: (0, 2 * ATTN_W // 128 + h)), pair, pair, pair],
        out_specs=[pair] * 3, out_shape=[SDS((S, ATTN_W), f32)] * 3,
        scratch_shapes=[pltpu.VMEM((2, 256, 256), f32), pltpu.VMEM((S, 128), f32)],
        name=name, compiler_params=_params("parallel"))(qr, kr, proj, dmix_in, out, lse)


def _conv_val(x, w, K, rows):
    acc = x * w[K - 1:K, :]
    for s in range(1, K):
        acc = acc + jnp.where(rows >= s, pltpu.roll(x, s, 0), 0.0) * w[K - 1 - s:K - s, :]
    return acc


def _colconv_fwd(xs, ws, bs, K, fn, nblk, tc, outs, name):
    S = xs[0][0].shape[0]
    n = len(xs)
    has_b = bs is not None

    def body(*refs):
        rows = lax.broadcasted_iota(jnp.int32, (S, tc), 0)
        cs = []
        for k in range(n):
            c = _conv_val(refs[k][...].astype(f32), refs[n + k][...], K, rows)
            if has_b:
                c = c + refs[2 * n + k][...]
            cs.append(c)
        for o_ref, val in zip(refs[(3 if has_b else 2) * n:], fn(*cs)):
            o_ref[...] = val.astype(o_ref.dtype)

    def cspec(rows_, cb0):
        return pl.BlockSpec((rows_, tc), lambda j, cb0=cb0: (0, cb0 + j))

    in_specs = [cspec(S, cb) for _, cb in xs] + [cspec(K, cb) for _, cb in ws]
    args = [a for a, _ in xs] + [a for a, _ in ws]
    if has_b:
        in_specs += [cspec(1, cb) for _, cb in bs]
        args += [a for a, _ in bs]
    return pl.pallas_call(
        body, grid=(nblk,), in_specs=in_specs, out_specs=[cspec(S, 0) for _ in outs],
        out_shape=[SDS((S, nblk * tc), dt) for dt in outs], name=name, compiler_params=_params("parallel"))(*args)


def _colconv_bwd(xs, ws, bs, K, fn, douts, nblk, tc, name, dx_dtype=f32):
    S = xs[0][0].shape[0]
    n, nd = len(xs), len(douts)
    has_b = bs is not None
    nin = (3 if has_b else 2) * n

    def body(*refs):
        rows = lax.broadcasted_iota(jnp.int32, (S, tc), 0)
        x = [refs[k][...].astype(f32) for k in range(n)]
        w = [refs[n + k][...] for k in range(n)]
        cs = []
        for k in range(n):
            c = _conv_val(x[k], w[k], K, rows)
            if has_b:
                c = c + refs[2 * n + k][...]
            cs.append(c)
        _, vjp = jax.vjp(fn, *cs)
        dcs = vjp(tuple(r[...].astype(f32) for r in refs[nin:nin + nd]))
        o = refs[nin + nd:]
        for k in range(n):
            dc = dcs[k]
            dx = dc * w[k][K - 1:K, :]
            o[n + k][K - 1:K, :] = jnp.sum(dc * x[k], axis=0, keepdims=True)
            for s in range(1, K):
                dx = dx + jnp.where(rows < S - s, pltpu.roll(dc, S - s, 0), 0.0) * w[k][K - 1 - s:K - s, :]
                xsh = jnp.where(rows >= s, pltpu.roll(x[k], s, 0), 0.0)
                o[n + k][K - 1 - s:K - s, :] = jnp.sum(dc * xsh, axis=0, keepdims=True)
            o[k][...] = dx.astype(o[k].dtype)
            if has_b:
                o[2 * n + k][...] = jnp.sum(dc, axis=0, keepdims=True)

    def cspec(rows_, cb0):
        return pl.BlockSpec((rows_, tc), lambda j, cb0=cb0: (0, cb0 + j))

    in_specs = [cspec(S, cb) for _, cb in xs] + [cspec(K, cb) for _, cb in ws]
    args = [a for a, _ in xs] + [a for a, _ in ws]
    if has_b:
        in_specs += [cspec(1, cb) for _, cb in bs]
        args += [a for a, _ in bs]
    in_specs += [cspec(S, 0) for _ in douts]
    args += list(douts)
    W = nblk * tc
    out_specs = [cspec(S, 0)] * n + [cspec(K, 0)] * n + ([cspec(1, 0)] * n if has_b else [])
    out_shape = [SDS((S, W), dx_dtype)] * n + [SDS((K, W), f32)] * n + ([SDS((1, W), f32)] * n if has_b else [])
    res = pl.pallas_call(body, grid=(nblk,), in_specs=in_specs, out_specs=out_specs, out_shape=out_shape,
                         name=name, compiler_params=_params("parallel"))(*args)
    return res[:n], res[n:2 * n], res[2 * n:]


def _silu_fn(c):
    return (c * jax.nn.sigmoid(c),)


_GELU_C, _GELU_A = 0.7978845608028654, 0.044715


@jax.custom_vjp
def _geglu(gate, up):
    return 0.5 * gate * (1.0 + jnp.tanh(_GELU_C * (gate + _GELU_A * gate * gate * gate))) * up


def _geglu_vjp_fwd(gate, up):
    return _geglu(gate, up), (gate, up)


def _geglu_vjp_bwd(res, d):
    gate, up = res
    g2 = gate * gate
    t = jnp.tanh(_GELU_C * gate * (1.0 + _GELU_A * g2))
    h = 0.5 * (1.0 + t)
    dgelu = h + (0.5 * _GELU_C) * gate * (1.0 - t * t) * (1.0 + (3.0 * _GELU_A) * g2)
    return d * up * dgelu, d * (gate * h)


_geglu.defvjp(_geglu_vjp_fwd, _geglu_vjp_bwd)


def _geglu_fn(gate, up):
    return (_geglu(gate, up),)


def _softplus(x):
    u = jnp.exp(jnp.minimum(x, 20.0))
    small = u * (1.0 - 0.5 * u)
    return jnp.where(x > 20.0, x, jnp.where(u < 1e-4, small, jnp.log(1.0 + u)))


def _bmm(a, b, precision=None):
    return lax.dot_general(a, b, (((2,), (1,)), ((0,), (0,))), precision=precision, preferred_element_type=f32)


def _bnt(a, b, precision=None):
    return lax.dot_general(a, b, (((2,), (2,)), ((0,), (0,))), precision=precision, preferred_element_type=f32)


def _btn(a, b, precision=None):
    return lax.dot_general(a, b, (((1,), (1,)), ((0,), (0,))), precision=precision, preferred_element_type=f32)


@jax.custom_vjp
def _unit_lower_inverse(A):
    n = A.shape[-1]
    eye = (lax.broadcasted_iota(jnp.int32, (1, n, n), 1) == lax.broadcasted_iota(jnp.int32, (1, n, n), 2)).astype(f32)
    P = -A
    T = eye + P
    for _ in range(5):
        P = _bmm(P, P, HI)
        T = T + _bmm(T, P, HI)
    return T


def _unit_lower_inverse_fwd(A):
    T = _unit_lower_inverse(A)
    return T, T


def _unit_lower_inverse_bwd(T, dT):
    return (-_btn(T, _bnt(dT, T, HI), HI),)


_unit_lower_inverse.defvjp(_unit_lower_inverse_fwd, _unit_lower_inverse_bwd)


def _dn_prep_fn(q, k, v, ba, alog, dtb, h):
    G, C = q.shape[0], CHUNK
    lane = lax.broadcasted_iota(jnp.int32, (1, 1, 128), 2)

    def sel(arr, idx):
        return jnp.sum(jnp.where(lane == idx, arr, 0.0), axis=-1, keepdims=True)

    beta = jax.nn.sigmoid(sel(ba, h))
    g = -jnp.exp(sel(alog[None], h)) * _softplus(sel(ba, N_HEADS_D + h) + sel(dtb[None], h))
    qn = q * lax.rsqrt(jnp.sum(q * q, axis=-1, keepdims=True) + EPS) * (DK ** -0.5)
    kn = k * lax.rsqrt(jnp.sum(k * k, axis=-1, keepdims=True) + EPS)
    ii = lax.broadcasted_iota(jnp.int32, (1, C, C), 1)
    jj = lax.broadcasted_iota(jnp.int32, (1, C, C), 2)
    tril, strict = ii >= jj, ii > jj
    gsq = jnp.broadcast_to(g, (G, C, C))
    gcol = _bmm(jnp.broadcast_to(tril.astype(f32), (G, C, C)), gsq, HI)
    grow = _bmm(jnp.ones((G, C, C), f32), jnp.where(ii <= jj, gsq, 0.0), HI)
    decay = jnp.exp(jnp.where(tril, gcol - grow, NEG))
    gc = gcol[:, :, :1]
    glast = gcol[:, C - 1:C, :1]
    kb = kn * beta
    A = jnp.where(strict, _bnt(kb.astype(bf16), kn.astype(bf16)) * decay, 0.0)
    T = _unit_lower_inverse(A).astype(bf16)
    u = _bmm(T, (v * beta).astype(bf16))
    w = _bmm(T, (kb * jnp.exp(gc)).astype(bf16))
    qk = _bnt(qn.astype(bf16), kn.astype(bf16)) * decay
    qd = qn * jnp.exp(gc)
    kd = kn * jnp.exp(glast - gc)
    return u, w, qk, qd, kd, jnp.broadcast_to(jnp.exp(glast), (G, C, DK))


def _dn_scan_fn(u, w, qk, qd, kd, eg, St):
    b = lambda a: a.astype(bf16)
    vnew = u - _bmm(b(w), b(St))
    o = _bmm(b(qd), b(St)) + _bmm(b(qk), b(vnew))
    return o, St * eg[:, :1, :] + _btn(b(kd), b(vnew))


def _dn_post_fn(o, z, nw):
    return (_rms(o, nw) * (z * jax.nn.sigmoid(z)),)


DN_GROUP = 8


def _dn_prep_specs(S, rows):
    def col(first):
        return pl.BlockSpec((rows, DK), lambda i, h, first=first: (i, first // DK + h))

    par = pl.BlockSpec((1, 128), lambda i, h: (0, 0))
    return [col(0), col(N_HEADS_D * DK), col(2 * N_HEADS_D * DK),
            pl.BlockSpec((rows, 128), lambda i, h: (i, 3584 // 128)), par, par]


def _dn_prep(qkv, proj, alog, dtb, name):
    S = qkv.shape[0]
    G = DN_GROUP
    rows = G * CHUNK

    def body(q_ref, k_ref, v_ref, ba_ref, al_ref, dt_ref, u_ref, w_ref, qk_ref, qd_ref, kd_ref, eg_ref):
        h = pl.program_id(1)
        r3 = lambda ref: ref[...].reshape(G, CHUNK, 128)
        u, w, qk, qd, kd, eg = _dn_prep_fn(r3(q_ref), r3(k_ref), r3(v_ref), r3(ba_ref), al_ref[...], dt_ref[...], h)
        for ref, val in ((u_ref, u), (w_ref, w), (qd_ref, qd), (kd_ref, kd), (eg_ref, eg)):
            ref[...] = val.reshape(rows, DK)
        qk_ref[:, :CHUNK] = qk.reshape(rows, CHUNK)
        qk_ref[:, CHUNK:] = jnp.zeros((rows, DK - CHUNK), f32)

    out = pl.BlockSpec((rows, DK), lambda i, h: (i, h))
    return pl.pallas_call(
        body, grid=(S // rows, N_HEADS_D), in_specs=_dn_prep_specs(S, rows), out_specs=[out] * 6,
        out_shape=[SDS((S, N_HEADS_D * DK), f32)] * 6, name=name,
        compiler_params=_params("parallel", "parallel"))(qkv, qkv, qkv, proj, alog, dtb)


def _dn_prep_bwd(qkv, proj, alog, dtb, cts, name):
    S = qkv.shape[0]
    G = DN_GROUP
    rows = G * CHUNK

    def body(q_ref, k_ref, v_ref, ba_ref, al_ref, dt_ref, du_ref, dw_ref, dqk_ref, dqd_ref, dkd_ref, deg_ref,
             dq_ref, dk_ref, dv_ref, dba_ref, dal_ref, ddt_ref):
        i, h = pl.program_id(0), pl.program_id(1)
        r3 = lambda ref: ref[...].reshape(G, CHUNK, 128)
        _, vjp = jax.vjp(lambda q, k, v, ba, al, dt: _dn_prep_fn(q, k, v, ba, al, dt, h),
                         r3(q_ref), r3(k_ref), r3(v_ref), r3(ba_ref), al_ref[...], dt_ref[...])
        dqk = dqk_ref[:, :CHUNK].reshape(G, CHUNK, CHUNK)
        dq, dk, dv, dba, dal, ddt = vjp((r3(du_ref), r3(dw_ref), dqk, r3(dqd_ref), r3(dkd_ref), r3(deg_ref)))
        dq_ref[...] = dq.reshape(rows, DK)
        dk_ref[...] = dk.reshape(rows, DK)
        dv_ref[...] = dv.reshape(rows, DK)

        @pl.when(h == 0)
        def _():
            dba_ref[...] = jnp.zeros_like(dba_ref)

        @pl.when(jnp.logical_and(i == 0, h == 0))
        def _():
            dal_ref[...] = jnp.zeros_like(dal_ref)
            ddt_ref[...] = jnp.zeros_like(ddt_ref)

        dba_ref[...] += dba.reshape(rows, 128)
        dal_ref[...] += dal
        ddt_ref[...] += ddt

    hcol = pl.BlockSpec((rows, DK), lambda i, h: (i, h))
    par = pl.BlockSpec((1, 128), lambda i, h: (0, 0))
    W = N_HEADS_D * DK
    return pl.pallas_call(
        body, grid=(S // rows, N_HEADS_D), in_specs=_dn_prep_specs(S, rows) + [hcol] * 6,
        out_specs=[hcol] * 3 + [pl.BlockSpec((rows, 128), lambda i, h: (i, 0)), par, par],
        out_shape=[SDS((S, W), f32)] * 3 + [SDS((S, 128), f32), SDS((1, 128), f32), SDS((1, 128), f32)], name=name,
        compiler_params=_params("arbitrary", "arbitrary"))(qkv, qkv, qkv, proj, alog, dtb, *cts)


def _heads(x):
    return jnp.stack([x[:, DK * h:DK * (h + 1)] for h in range(N_HEADS_D)])


SCAN_CHUNKS = 4


def _dn_scan(pre, name):
    S = pre[0].shape[0]
    NCH = S // CHUNK
    rows = SCAN_CHUNKS * CHUNK

    def body(u_ref, w_ref, qk_ref, qd_ref, kd_ref, eg_ref, o_ref, st_ref, s_ref):
        @pl.when(pl.program_id(0) == 0)
        def _():
            s_ref[...] = jnp.zeros_like(s_ref)

        St = s_ref[...]
        for k in range(SCAN_CHUNKS):
            r = slice(k * CHUNK, (k + 1) * CHUNK)
            st_ref[k] = St
            o, St = _dn_scan_fn(_heads(u_ref[r, :]), _heads(w_ref[r, :]), _heads(qk_ref[r, :])[:, :, :CHUNK],
                                _heads(qd_ref[r, :]), _heads(kd_ref[r, :]), _heads(eg_ref[r, :]), St)
            for h in range(N_HEADS_D):
                o_ref[r, DK * h:DK * (h + 1)] = o[h]
        s_ref[...] = St

    blk = pl.BlockSpec((rows, N_HEADS_D * DK), lambda n: (n, 0))
    return pl.pallas_call(
        body, grid=(S // rows,), in_specs=[blk] * 6,
        out_specs=[blk, pl.BlockSpec((SCAN_CHUNKS, N_HEADS_D, DK, DK), lambda n: (n, 0, 0, 0))],
        out_shape=[SDS((S, N_HEADS_D * DK), f32), SDS((NCH, N_HEADS_D, DK, DK), f32)],
        scratch_shapes=[pltpu.VMEM((N_HEADS_D, DK, DK), f32)], name=name, compiler_params=_params("arbitrary"))(*pre)


def _dn_scan_bwd(pre, states, do, name):
    S = do.shape[0]
    rows = SCAN_CHUNKS * CHUNK
    steps = S // rows

    def body(u_ref, w_ref, qk_ref, qd_ref, kd_ref, eg_ref, st_ref, do_ref,
             du_ref, dw_ref, dqk_ref, dqd_ref, dkd_ref, deg_ref, ds_ref):
        @pl.when(pl.program_id(0) == 0)
        def _():
            ds_ref[...] = jnp.zeros_like(ds_ref)

        dS = ds_ref[...]
        for k in reversed(range(SCAN_CHUNKS)):
            r = slice(k * CHUNK, (k + 1) * CHUNK)
            _, vjp = jax.vjp(_dn_scan_fn, _heads(u_ref[r, :]), _heads(w_ref[r, :]), _heads(qk_ref[r, :])[:, :, :CHUNK],
                             _heads(qd_ref[r, :]), _heads(kd_ref[r, :]), _heads(eg_ref[r, :]), st_ref[k])
            du, dw, dqk, dqd, dkd, deg, dS = vjp((_heads(do_ref[r, :]), dS))
            for h in range(N_HEADS_D):
                c = slice(DK * h, DK * (h + 1))
                for ref, val in ((du_ref, du), (dw_ref, dw), (dqd_ref, dqd), (dkd_ref, dkd), (deg_ref, deg)):
                    ref[r, c] = val[h]
                dqk_ref[r, DK * h:DK * h + CHUNK] = dqk[h]
                dqk_ref[r, DK * h + CHUNK:DK * (h + 1)] = jnp.zeros((CHUNK, DK - CHUNK), f32)
        ds_ref[...] = dS

    blk = pl.BlockSpec((rows, N_HEADS_D * DK), lambda n: (steps - 1 - n, 0))
    return pl.pallas_call(
        body, grid=(steps,),
        in_specs=[blk] * 6 + [pl.BlockSpec((SCAN_CHUNKS, N_HEADS_D, DK, DK), lambda n: (steps - 1 - n, 0, 0, 0)), blk],
        out_specs=[blk] * 6, out_shape=[SDS((S, N_HEADS_D * DK), f32)] * 6,
        scratch_shapes=[pltpu.VMEM((N_HEADS_D, DK, DK), f32)], name=name,
        compiler_params=_params("arbitrary"))(*pre, states, do)


def _loss_head(y, t, name):
    S, D = y.shape
    tm = ROW_TILE

    def body(y_ref, t_ref, dy_ref, l_ref):
        i = pl.program_id(0)
        d = y_ref[...] - t_ref[...]
        dy_ref[...] = d * (1.0 / D)
        part = jnp.sum(jnp.sum(d * d, axis=1, keepdims=True), axis=0, keepdims=True) * (0.5 / D)

        @pl.when(i == 0)
        def _():
            l_ref[...] = jnp.zeros_like(l_ref)

        l_ref[...] += jnp.broadcast_to(part, l_ref.shape)

    spec = pl.BlockSpec((tm, D), lambda i: (i, 0))
    dy, l = pl.pallas_call(body, grid=(S // tm,), in_specs=[spec, spec],
                           out_specs=[spec, pl.BlockSpec((1, 128), lambda i: (0, 0))],
                           out_shape=[SDS((S, D), f32), SDS((1, 128), f32)], name=name,
                           compiler_params=_params("arbitrary"))(y, t)
    return l[0, 0], dy


def _adamw(w, g, m, v, tr, name):
    L, R, C = w.shape
    assert R % tr == 0

    def body(w_ref, g_ref, m_ref, v_ref, d_ref, mo_ref, vo_ref):
        gv = g_ref[...]
        m2 = ADAM_B1 * m_ref[...] + (1.0 - ADAM_B1) * gv
        v2 = ADAM_B2 * v_ref[...] + (1.0 - ADAM_B2) * (gv * gv)
        m_hat = m2 / (1.0 - ADAM_B1 ** ADAM_STEP)
        v_hat = v2 / (1.0 - ADAM_B2 ** ADAM_STEP)
        d_ref[...] = -ADAM_LR * (m_hat / (jnp.sqrt(v_hat) + ADAM_EPS) + ADAM_WD * w_ref[...])
        mo_ref[...] = m2
        vo_ref[...] = v2

    spec = pl.BlockSpec((1, tr, C), lambda l, i: (l, i, 0))
    return pl.pallas_call(body, grid=(L, R // tr), in_specs=[spec] * 4, out_specs=[spec] * 3,
                          out_shape=[SDS((L, R, C), f32)] * 3, name=name,
                          compiler_params=_params("parallel", "parallel"))(w, g, m, v)


def _rope_tables(S):
    inv = 1.0 / (10000.0 ** (jnp.arange(0, HEAD_DIM, 2, dtype=f32) / HEAD_DIM))
    ang = jnp.arange(S, dtype=f32)[:, None] * inv[None, :]
    cos, sin = jnp.cos(ang), jnp.sin(ang)
    return (jnp.tile(jnp.concatenate([cos, cos], axis=1), (1, N_HEADS_A)),
            jnp.tile(jnp.concatenate([-sin, sin], axis=1), (1, N_HEADS_A)))


def _layer_fwd(x, W, cos, sgn_sin, l, late_weights=None, h1=None, next_pre_mix=None):
    n = f"l{l}_"
    if h1 is None:
        (h1,) = _rows(_rms_fn, [x], [W["norm_pre_mix"]], [(D_MODEL, bf16)], n + "pre_mix_norm")
    proj = _mm(h1, W["w_in"], "nn", 512, 768, f32, n + "in_proj")
    qr, kr = _rows(_rope_fwd_fn, [(proj, ATTN_W, 0), (proj, ATTN_W, 1), cos, sgn_sin], [],
                   [(ATTN_W, f32), (ATTN_W, f32)], n + "rope")
    attn_out, lse = _attn_fwd(qr, kr, proj, n + "attn_fwd")
    (qkv,) = _colconv_fwd([(proj, 3)], [(W["dn_conv_w"], 0)], None, 4, _silu_fn, 3, 512, [f32], n + "dn_conv")
    dn_pre = _dn_prep(qkv, proj, W["dn_a_log"], W["dn_dt_bias"], n + "dn_prep")
    dn_o, dn_states = _dn_scan(dn_pre, n + "dn_scan")
    (dn_out,) = _rows(_dn_post_fn, [(dn_o, DK, 0), (proj, DK, 3072 // DK)], [W["dn_norm_w"]], [(DK, bf16)], n + "dn_post",
                      ncol=N_HEADS_D, tm=4 * ROW_TILE)
    mix_in = jnp.concatenate([attn_out, dn_out], axis=1)
    late = late_weights(mix_in) if late_weights is not None else {}
    W = {**W, **late}
    mix = _mm(mix_in, W["w_out"], "nn", 512, 512, f32, n + "out_proj")
    x1, h2 = _rows(_res_rms_pre_fn, [mix, x], [W["norm_post_mix"], W["norm_pre_ffn"]],
                   [(D_MODEL, f32), (D_MODEL, bf16)], n + "post_mix_pre_ffn_norm")
    u0 = _mm(h2, W["ffn_w_in"], "nn", 1024, 512, bf16, n + "ffn_in")
    nb_ff = D_FF // 256
    (act,) = _colconv_fwd([(u0, 0), (u0, nb_ff)], [(W["ffn_conv_w"], 0), (W["ffn_conv_w"], nb_ff)],
                          [(W["ffn_conv_b"], 0), (W["ffn_conv_b"], nb_ff)], 3, _geglu_fn, nb_ff, 256, [bf16],
                          n + "ffn_conv_glu")
    f = _mm(act, W["ffn_w_out"], "nn", 512, 512, f32, n + "ffn_out")
    if next_pre_mix is None:
        (x2,), h1_next = _rows(_res_rms_fn, [f, x1], [W["norm_post_ffn"]], [(D_MODEL, f32)], n + "post_ffn_norm"), None
    else:
        x2, h1_next = _rows(_res_rms_pre_fn, [f, x1], [W["norm_post_ffn"], next_pre_mix],
                            [(D_MODEL, f32), (D_MODEL, bf16)], n + "post_ffn_next_pre_mix_norm")
    saved = dict(x=x, h1=h1, proj=proj, qr=qr, kr=kr, attn_out=attn_out, lse=lse, qkv=qkv, dn_pre=dn_pre, dn_o=dn_o,
                 dn_states=dn_states, mix_in=mix_in, mix=mix, x1=x1, h2=h2, u0=u0, act=act, f=f, late=late)
    return x2, h1_next, saved


def _layer_bwd(dx2, sv, W, cos, sgn_sin, l, after_ffn=None, next_layer=None, first_layer=True):
    n = f"l{l}_"
    S = dx2.shape[0]
    g = {}
    g_next_pre = None
    if next_layer is None:
        (df,), (g["norm_post_ffn"],) = _rows_vjp(_rms_fn, [sv["f"]], [W["norm_post_ffn"]], [dx2], [0], [0],
                                                 n + "post_ffn_norm_bwd", row_dtype=bf16)
    else:
        (df, dx2), (g["norm_post_ffn"], g_next_pre) = _rows_vjp(
            _res_rms_pre_fn, [sv["f"], sv["x1"]], [W["norm_post_ffn"], next_layer[1]], [dx2, next_layer[0]], [0, 1], [0, 1],
            n + "post_ffn_next_pre_mix_norm_bwd", row_dtype=[bf16, f32])
    dact = _mm(df, W["ffn_w_out"], "nt", 512, 1408, f32, n + "ffn_out_dx")
    g["ffn_w_out"] = _mm(sv["act"], df, "tn", 256, 1024, f32, n + "ffn_out_dw")
    nb_ff = D_FF // 256
    u0 = sv["u0"]
    dxs, dws, dbs = _colconv_bwd([(u0, 0), (u0, nb_ff)], [(W["ffn_conv_w"], 0), (W["ffn_conv_w"], nb_ff)],
                                 [(W["ffn_conv_b"], 0), (W["ffn_conv_b"], nb_ff)], 3, _geglu_fn, [dact], nb_ff, 256,
                                 n + "ffn_conv_glu_bwd", dx_dtype=bf16)
    du0 = jnp.concatenate(dxs, axis=1)
    g["ffn_conv_w"] = jnp.concatenate(dws, axis=1)
    g["ffn_conv_b"] = jnp.concatenate(dbs, axis=1)
    dh2 = _mm(du0, W["ffn_w_in"], "nt", 512, 512, f32, n + "ffn_in_dx")
    g["ffn_w_in"] = _mm(sv["h2"], du0, "tn", 512, D_FF // 2, f32, n + "ffn_in_dw", column_shards=True)
    if after_ffn is not None:
        W = dict(W, norm_post_mix=W["norm_post_mix"] + after_ffn(g, dh2))
    (dmix, dx1), (g["norm_post_mix"], g["norm_pre_ffn"]) = _rows_vjp(
        _res_rms_pre_fn, [sv["mix"], sv["x"]], [W["norm_post_mix"], W["norm_pre_ffn"]], [dx2, dh2], [0, 1], [0, 1],
        n + "post_mix_pre_ffn_norm_bwd", row_dtype=[bf16, f32])
    dmix_in = _mm(dmix, W["w_out"], "nt", 512, 512, f32, n + "out_proj_dx")
    g["w_out"] = _mm(sv["mix_in"], dmix, "tn", 512, 512, f32, n + "out_proj_dw")

    (ddn_o, dz), (g["dn_norm_w"],) = _rows_vjp(
        _dn_post_fn, [(sv["dn_o"], DK, 0), (sv["proj"], DK, 3072 // DK)], [W["dn_norm_w"]], [(dmix_in, DK, ATTN_W // DK)],
        [0, 1], [0], n + "dn_post_bwd", ncol=N_HEADS_D, tm=4 * ROW_TILE)
    dpre = _dn_scan_bwd(sv["dn_pre"], sv["dn_states"], ddn_o, n + "dn_scan_bwd")
    dq, dk, dv, dba, g["dn_a_log"], g["dn_dt_bias"] = _dn_prep_bwd(
        sv["qkv"], sv["proj"], W["dn_a_log"], W["dn_dt_bias"], dpre, n + "dn_prep_bwd")
    dqkv = jnp.concatenate([dq, dk, dv], axis=1)
    (dqkv0,), (g["dn_conv_w"],), _ = _colconv_bwd([(sv["proj"], 3)], [(W["dn_conv_w"], 0)], None, 4, _silu_fn,
                                                 [dqkv], 3, 512, n + "dn_conv_bwd")

    dqr, dkr, dav = _attn_bwd(sv["qr"], sv["kr"], sv["proj"], dmix_in, sv["attn_out"], sv["lse"], n + "attn_bwd")
    daq, dak = _rows(_rope_bwd_fn, [dqr, dkr, cos, sgn_sin], [], [(ATTN_W, f32)] * 2, n + "rope_bwd")
    dproj = jnp.concatenate([daq, dak, dav, dqkv0, dz, dba, jnp.zeros((S, PROJ_W - 3712), f32)], axis=1).astype(bf16)
    dh1 = _mm(dproj, W["w_in"], "nt", 512, 512, f32, n + "in_proj_dx")
    g["w_in"] = _mm(sv["h1"], dproj, "tn", 512, 768, f32, n + "in_proj_dw")
    if not first_layer:
        return (dx1, dh1), g, g_next_pre
    (dx,), (g["norm_pre_mix"],) = _rows_vjp(_rms_fn, [sv["x"]], [W["norm_pre_mix"]], [dh1], [0], [0],
                                            n + "pre_mix_norm_bwd", adds={0: dx1})
    return dx, g, g_next_pre


def _local_step(x, target, layers):
    cos, sgn_sin = _rope_tables(x.shape[0])
    saved, h1 = [], None
    for l, W in enumerate(layers):
        nxt = layers[l + 1]["norm_pre_mix"] if l + 1 < len(layers) else None
        x, h1, sv = _layer_fwd(x, W, cos, sgn_sin, l, h1=h1, next_pre_mix=nxt)
        saved.append(sv)
    loss, dx = _loss_head(x, target, "loss_head")
    grads = [None] * len(layers)
    nxt = None
    for l in reversed(range(len(layers))):
        dx, grads[l], g_pre = _layer_bwd(dx, saved[l], layers[l], cos, sgn_sin, l, next_layer=nxt, first_layer=(l == 0))
        if g_pre is not None:
            grads[l + 1]["norm_pre_mix"] = g_pre
        if l > 0:
            dx, dh1 = dx
            nxt = (dh1, layers[l]["norm_pre_mix"])
    return loss, dx, grads


def _pos():
    x, y, c = lax.axis_index("x"), lax.axis_index("y"), lax.axis_index("c")
    return x, y, c, [(1 - x, y), (x, 1 - y), (1 - x, 1 - y)]


def _rcopy(src, dst, send_sem, recv_sem, dev):
    return pltpu.make_async_remote_copy(src_ref=src, dst_ref=dst, send_sem=send_sem, recv_sem=recv_sem,
                                        device_id=dev, device_id_type=MESH)


def _half_rows(ref, h, which, axis):
    if h is None:
        return ref
    rows = pl.ds(pl.multiple_of(which * h, 16), h)
    return ref.at[:, rows, :] if axis == 1 else ref.at[rows, :]


def _dma_sems(*counts):
    return [pltpu.SemaphoreType.DMA((k,)) for k in counts]


def _all_gather(arrs, halves, name):
    n = len(arrs)

    def body(*refs):
        ins, outs = refs[:n], refs[n:2 * n]
        send1, recv1, send2, recv2 = refs[2 * n:]
        x, y, c, chips = _pos()
        me, sib, s_me = (x, y, c), (x, y, 1 - c), 2 * x + y
        sends = []
        for i in range(n):
            for j, chip in enumerate(chips):
                cp = _rcopy(_half_rows(ins[i], halves[i], c, 1), _half_rows(outs[i].at[s_me], halves[i], c, 1),
                            send1.at[3 * i + j], recv1.at[3 * i + j], (*chip, c))
                cp.start()
                sends.append(cp)
        for i in range(n):
            for j, (px, py) in enumerate(chips):
                k = 3 * i + j
                landed = _half_rows(outs[i].at[2 * px + py], halves[i], c, 1)
                _rcopy(landed, landed, send1.at[k], recv1.at[k], me).wait_recv()
                if halves[i] is not None:
                    cp = _rcopy(landed, landed, send2.at[k], recv2.at[k], sib)
                    cp.start()
                    sends.append(cp)
        for i in range(n):
            if halves[i] is None:
                continue
            for j, (px, py) in enumerate(chips):
                k = 3 * i + j
                other = _half_rows(outs[i].at[2 * px + py], halves[i], 1 - c, 1)
                _rcopy(other, other, send2.at[k], recv2.at[k], me).wait_recv()
        for cp in sends:
            cp.wait_send()

    return pl.pallas_call(
        body, in_specs=[ANY] * n, out_specs=[ANY] * n,
        out_shape=[SDS((4,) + a.shape, a.dtype) for a in arrs],
        scratch_shapes=_dma_sems(3 * n, 3 * n, 3 * n, 3 * n), name=name)(*arrs)


HBM = pl.BlockSpec(memory_space=pltpu.HBM)
SEM = pl.BlockSpec(memory_space=pltpu.SEMAPHORE)
_EFFECT = pltpu.SideEffectType.DATAFLOW_SIDE_EFFECTING


def _in_hbm(a):
    return pltpu.with_memory_space_constraint(a, pltpu.HBM)


def _split_copy(srcs, land_shapes, plan, per, after, name):
    n = len(srcs)
    k = per * n

    def body(*refs):
        ins, lands, token = refs[:n], refs[n:2 * n], refs[-1]
        send, recv = refs[2 * n + 1], refs[2 * n + 2]
        for i, (src, dst, dev, _) in enumerate(plan(ins, lands)):
            _rcopy(src, dst, send.at[i], recv.at[i], dev).start()
        token[...] = jnp.zeros_like(token)

    lands = [_in_hbm(lax.empty(s.shape, s.dtype)) for s in land_shapes]
    return pl.pallas_call(
        body, name=name,
        out_shape=(pltpu.SemaphoreType.DMA((k,)), pltpu.SemaphoreType.DMA((k,)),
                   *[pltpu.HBM(a.shape, a.dtype) for a in srcs], *[pltpu.HBM(s.shape, s.dtype) for s in land_shapes],
                   SDS((8, 128), f32)),
        in_specs=[HBM] * (2 * n) + [ANY], out_specs=(SEM, SEM, *[HBM] * (2 * n), pl.BlockSpec(memory_space=pltpu.VMEM)),
        input_output_aliases={i: 2 + i for i in range(2 * n)},
        compiler_params=pltpu.CompilerParams(has_side_effects=_EFFECT))(*[_in_hbm(a) for a in srcs], *lands, after)


def _split_wait(started, n, plan, after, name):
    send, recv = started[0], started[1]
    thru = started[2:2 + 2 * n]

    def body(*refs):
        ins, lands = refs[:n], refs[n:2 * n]
        send_ref, recv_ref = refs[2 * n], refs[2 * n + 1]
        for i, (src, _, dev, mine) in enumerate(plan(ins, lands)):
            cp = _rcopy(src, mine, send_ref.at[i], recv_ref.at[i], dev)
            cp.wait_send()
            cp.wait_recv()

    res = pl.pallas_call(
        body, name=name, out_shape=tuple(pltpu.HBM(a.shape, a.dtype) for a in thru),
        in_specs=[HBM] * (2 * n) + [SEM, SEM, ANY], out_specs=tuple([HBM] * (2 * n)),
        input_output_aliases={i: i for i in range(2 * n)},
        compiler_params=pltpu.CompilerParams(has_side_effects=_EFFECT))(*thru, send, recv, after)
    return res[:n], res[n:]


def _gather_plan(halves):
    def plan(ins, lands):
        x, y, c, chips = _pos()
        out = []
        for i in range(len(ins)):
            for px, py in chips:
                out.append((_half_rows(ins[i], halves[i], c, 1), _half_rows(lands[i].at[2 * x + y], halves[i], c, 1),
                            (px, py, c), _half_rows(lands[i].at[2 * px + py], halves[i], c, 1)))
        return out
    return plan


def _scatter_plan(ins, lands):
    x, y, c, chips = _pos()
    out = []
    for i in range(len(ins)):
        for j, (px, py) in enumerate(chips):
            out.append((ins[i].at[2 * px + py], lands[i].at[j], (px, py, c), lands[i].at[j]))
    return out


def _exchange_plan(ins, lands):
    x, y, c, _ = _pos()
    return [(_half_rows(g, g.shape[1] // 2, 1 - c, 1), land, (x, y, 1 - c), land) for g, land in zip(ins, lands)]


def _pass_to_sibling(lands, halves, name):
    n = len(lands)

    def body(*refs):
        outs = refs[n:2 * n]
        send, recv = refs[2 * n:]
        x, y, c, chips = _pos()
        sends = []
        for i in range(n):
            for j, (px, py) in enumerate(chips):
                landed = _half_rows(outs[i].at[2 * px + py], halves[i], c, 1)
                cp = _rcopy(landed, landed, send.at[3 * i + j], recv.at[3 * i + j], (x, y, 1 - c))
                cp.start()
                sends.append(cp)
        for i in range(n):
            for j, (px, py) in enumerate(chips):
                other = _half_rows(outs[i].at[2 * px + py], halves[i], 1 - c, 1)
                _rcopy(other, other, send.at[3 * i + j], recv.at[3 * i + j], (x, y, c)).wait_recv()
        for cp in sends:
            cp.wait_send()

    return pl.pallas_call(
        body, in_specs=[ANY] * n, out_specs=[ANY] * n, out_shape=[SDS(a.shape, a.dtype) for a in lands],
        input_output_aliases={k: k for k in range(n)}, scratch_shapes=_dma_sems(3 * n, 3 * n), name=name)(*lands)


def _exchange_halves(gs, name):
    n = len(gs)

    def body(*refs):
        ins, outs = refs[:n], refs[n:2 * n]
        send, recv = refs[2 * n:]
        x, y, c, _ = _pos()
        sends = []
        for k in range(n):
            cp = _rcopy(_half_rows(ins[k], gs[k].shape[1] // 2, 1 - c, 1), outs[k], send.at[k], recv.at[k], (x, y, 1 - c))
            cp.start()
            sends.append(cp)
        for k in range(n):
            _rcopy(outs[k], outs[k], send.at[k], recv.at[k], (x, y, c)).wait_recv()
        for cp in sends:
            cp.wait_send()

    return pl.pallas_call(
        body, in_specs=[ANY] * n, out_specs=[ANY] * n,
        out_shape=[SDS((4, g.shape[1] // 2, g.shape[2]), g.dtype) for g in gs],
        scratch_shapes=_dma_sems(n, n), name=name)(*gs)


def _scatter_partials(ps, name):
    n = len(ps)

    def body(*refs):
        ins, outs = refs[:n], refs[n:2 * n]
        send, recv = refs[2 * n:]
        x, y, c, chips = _pos()
        sends = []
        for k in range(n):
            for j, (px, py) in enumerate(chips):
                cp = _rcopy(ins[k].at[2 * px + py], outs[k].at[j], send.at[3 * k + j], recv.at[3 * k + j], (px, py, c))
                cp.start()
                sends.append(cp)
        for k in range(n):
            for j in range(3):
                _rcopy(outs[k].at[j], outs[k].at[j], send.at[3 * k + j], recv.at[3 * k + j], (x, y, c)).wait_recv()
        for cp in sends:
            cp.wait_send()

    return pl.pallas_call(
        body, in_specs=[ANY] * n, out_specs=[ANY] * n,
        out_shape=[SDS((3,) + p.shape[1:], p.dtype) for p in ps],
        scratch_shapes=_dma_sems(3 * n, 3 * n), name=name)(*ps)


def _join_halves(rs, name):
    n = len(rs)

    def body(*refs):
        outs = refs[n:2 * n]
        send, recv = refs[2 * n:]
        x, y, c, _ = _pos()
        sends = []
        for k in range(n):
            mine = _half_rows(outs[k], rs[k].shape[0] // 2, c, 0)
            cp = _rcopy(mine, mine, send.at[k], recv.at[k], (x, y, 1 - c))
            cp.start()
            sends.append(cp)
        for k in range(n):
            other = _half_rows(outs[k], rs[k].shape[0] // 2, 1 - c, 0)
            _rcopy(other, other, send.at[k], recv.at[k], (x, y, c)).wait_recv()
        for cp in sends:
            cp.wait_send()

    return pl.pallas_call(
        body, in_specs=[ANY] * n, out_specs=[ANY] * n, out_shape=[SDS(r.shape, r.dtype) for r in rs],
        input_output_aliases={k: k for k in range(n)}, scratch_shapes=_dma_sems(n, n), name=name)(*rs)


def _all_reduce_small(pack, name):
    R = pack.shape[0]

    def body(in_ref, out_ref, buf, send, recv):
        x, y, c, _ = _pos()
        me = 4 * x + 2 * y + c
        buf[me] = in_ref[...]
        sends = []
        for k in range(1, 8):
            peer = me ^ k
            cp = _rcopy(buf.at[me], buf.at[me], send.at[k - 1], recv.at[k - 1], ((peer >> 2) & 1, (peer >> 1) & 1, peer & 1))
            cp.start()
            sends.append(cp)
        for k in range(1, 8):
            _rcopy(buf.at[me ^ k], buf.at[me ^ k], send.at[k - 1], recv.at[k - 1], (x, y, c)).wait_recv()
        for cp in sends:
            cp.wait_send()
        acc = buf[0]
        for d in range(1, 8):
            acc = acc + buf[d]
        out_ref[...] = acc

    return pl.pallas_call(
        body, out_shape=SDS((R, 128), f32),
        in_specs=[pl.BlockSpec(memory_space=pltpu.VMEM)], out_specs=pl.BlockSpec(memory_space=pltpu.VMEM),
        scratch_shapes=[pltpu.VMEM((8, R, 128), f32)] + _dma_sems(7, 7), name=name)(pack)


def _add_sibling(g, recv, c_arr, tr, name):
    _, R, C = g.shape
    h = R // 2
    nrb = h // tr
    assert h % tr == 0

    def body(c_ref, g_ref, r_ref, o_ref):
        o_ref[...] = (g_ref[...] + r_ref[...]).astype(o_ref.dtype)

    spec = pl.BlockSpec((1, tr, C), lambda s, r, c_ref: (s, r, 0))
    grid_spec = pltpu.PrefetchScalarGridSpec(
        num_scalar_prefetch=1, grid=(4, nrb),
        in_specs=[pl.BlockSpec((1, tr, C), lambda s, r, c_ref: (s, c_ref[0] * nrb + r, 0)), spec], out_specs=spec)
    return pl.pallas_call(body, grid_spec=grid_spec, out_shape=SDS((4, h, C), bf16), name=name,
                          compiler_params=_params("parallel", "parallel"))(c_arr, g, recv)


def _add_chips(p, recv, sc_arr, tr, name):
    _, h, C = p.shape
    nrb = h // tr
    assert h % tr == 0

    def body(sc_ref, p_ref, r_ref, o_ref):
        o_ref[...] = (p_ref[0].astype(f32) + r_ref[0].astype(f32)) + (r_ref[1].astype(f32) + r_ref[2].astype(f32))

    grid_spec = pltpu.PrefetchScalarGridSpec(
        num_scalar_prefetch=1, grid=(nrb,),
        in_specs=[pl.BlockSpec((1, tr, C), lambda r, sc_ref: (sc_ref[0], r, 0)),
                  pl.BlockSpec((3, tr, C), lambda r, sc_ref: (0, r, 0))],
        out_specs=pl.BlockSpec((tr, C), lambda r, sc_ref: (sc_ref[1] * nrb + r, 0)))
    return pl.pallas_call(body, grid_spec=grid_spec, out_shape=SDS((2 * h, C), f32), name=name,
                          compiler_params=_params("parallel"))(sc_arr, p, recv)


_BIG = (("w_in", 1024, 256), ("w_out", 256, 128), ("ffn_w_in", 1024, 256), ("ffn_w_out", 704, 352))
_SMALL = ("dn_conv_w", "ffn_conv_w", "ffn_conv_b", "norm_pre_mix", "norm_post_mix", "norm_pre_ffn", "norm_post_ffn",
          "dn_norm_w", "dn_a_log", "dn_dt_bias")
_WEIGHTS = ("w_in", "dn_conv_w", "dn_a_log", "dn_dt_bias", "dn_norm_w", "w_out", "ffn_w_in", "ffn_conv_w", "ffn_conv_b",
            "ffn_w_out", "norm_pre_mix", "norm_post_mix", "norm_pre_ffn", "norm_post_ffn")
_ADAM_ROWS = {"w_in": 256, "w_out": 256, "ffn_w_in": 128, "ffn_w_out": 176}


def _shard_major(name, g):
    if name == "w_in":
        return jnp.stack([g[:, 898 * s:898 * (s + 1)] for s in range(4)])
    if name == "ffn_w_in":
        return g
    return g.reshape(4, g.shape[0] // 4, g.shape[1])


def kernel(x, w_in, dn_conv_w, dn_a_log, dn_dt_bias, dn_norm_w, w_out, ffn_w_in, ffn_conv_w, ffn_conv_b, ffn_w_out, norm_pre_mix, norm_post_mix, norm_pre_ffn, norm_post_ffn, loss_target, m_w_in, m_dn_conv_w, m_dn_a_log, m_dn_dt_bias, m_dn_norm_w, m_w_out, m_ffn_w_in, m_ffn_conv_w, m_ffn_conv_b, m_ffn_w_out, m_norm_pre_mix, m_norm_post_mix, m_norm_pre_ffn, m_norm_post_ffn, v_w_in, v_dn_conv_w, v_dn_a_log, v_dn_dt_bias, v_dn_norm_w, v_w_out, v_ffn_w_in, v_ffn_conv_w, v_ffn_conv_b, v_ffn_w_out, v_norm_pre_mix, v_norm_post_mix, v_norm_pre_ffn, v_norm_post_ffn):
    w = dict(w_in=w_in, dn_conv_w=dn_conv_w, dn_a_log=dn_a_log, dn_dt_bias=dn_dt_bias, dn_norm_w=dn_norm_w, w_out=w_out,
             ffn_w_in=ffn_w_in, ffn_conv_w=ffn_conv_w, ffn_conv_b=ffn_conv_b, ffn_w_out=ffn_w_out, norm_pre_mix=norm_pre_mix,
             norm_post_mix=norm_post_mix, norm_pre_ffn=norm_pre_ffn, norm_post_ffn=norm_post_ffn)
    m = dict(w_in=m_w_in, dn_conv_w=m_dn_conv_w, dn_a_log=m_dn_a_log, dn_dt_bias=m_dn_dt_bias, dn_norm_w=m_dn_norm_w,
             w_out=m_w_out, ffn_w_in=m_ffn_w_in, ffn_conv_w=m_ffn_conv_w, ffn_conv_b=m_ffn_conv_b, ffn_w_out=m_ffn_w_out,
             norm_pre_mix=m_norm_pre_mix, norm_post_mix=m_norm_post_mix, norm_pre_ffn=m_norm_pre_ffn,
             norm_post_ffn=m_norm_post_ffn)
    v = dict(w_in=v_w_in, dn_conv_w=v_dn_conv_w, dn_a_log=v_dn_a_log, dn_dt_bias=v_dn_dt_bias, dn_norm_w=v_dn_norm_w,
             w_out=v_w_out, ffn_w_in=v_ffn_w_in, ffn_conv_w=v_ffn_conv_w, ffn_conv_b=v_ffn_conv_b, ffn_w_out=v_ffn_w_out,
             norm_pre_mix=v_norm_pre_mix, norm_post_mix=v_norm_post_mix, norm_pre_ffn=v_norm_pre_ffn,
             norm_post_ffn=v_norm_post_ffn)
    xi, yi, ci = lax.axis_index("x"), lax.axis_index("y"), lax.axis_index("c")
    s_me = 2 * xi + yi
    c_arr = jnp.reshape(ci, (1,)).astype(jnp.int32)
    sc_arr = jnp.stack([s_me, ci]).astype(jnp.int32)

    mats = [name for name, _, _ in _BIG]
    rest = mats[1:]
    half_of = {name: rows // 2 for name, rows, _ in _BIG}
    tiles = {name: tr for name, _, tr in _BIG}
    gathered_shape = lambda a: SDS((4,) + a.shape, a.dtype)

    own = {k: w[k].astype(bf16) for k in mats}
    got_in = _all_gather([own["w_in"][0:1], dn_conv_w, ffn_conv_w], [half_of["w_in"], None, None], "weights_gather_w_in0")
    plan0 = _gather_plan([half_of[k] for k in rest])
    src0 = [own[k][0:1] for k in rest]
    started0 = _split_copy(src0, [gathered_shape(a) for a in src0], plan0, 3, got_in[0], "weights_gather_l0_start")
    plan1 = _gather_plan([half_of[k] for k in mats])
    src1 = [own[k][1:2] for k in mats]
    started1 = _split_copy(src1, [gathered_shape(a) for a in src1], plan1, 3, started0[-1], "weights_gather_l1_start")

    def pick(mine, gathered):
        return [jnp.where(s_me == s, mine, gathered[s]) for s in range(4)]

    conv = {"dn_conv_w": jnp.concatenate(pick(dn_conv_w, got_in[1]), axis=-1),
            "ffn_conv_w": jnp.concatenate(pick(ffn_conv_w, got_in[2]), axis=-1)}
    lanes = lambda a: jnp.pad(a, ((0, 0), (0, 128 - a.shape[1])))
    vec = dict(dn_a_log=lanes(dn_a_log), dn_dt_bias=lanes(dn_dt_bias), dn_norm_w=dn_norm_w, ffn_conv_b=ffn_conv_b,
               norm_pre_mix=norm_pre_mix, norm_post_mix=norm_post_mix, norm_pre_ffn=norm_pre_ffn, norm_post_ffn=norm_post_ffn)

    def matrices(l, names, gathered):
        W = {}
        for k, a in zip(names, gathered):
            if k in ("w_out", "ffn_w_out"):
                rows_, cols = own[k].shape[1:]
                W[k] = lax.dynamic_update_slice(a[:, 0], own[k][l][None], (s_me, 0, 0)).reshape(4 * rows_, cols)
            else:
                cat = jnp.concatenate(pick(own[k][l], a[:, 0]), axis=-1)
                W[k] = jnp.pad(cat, ((0, 0), (0, PROJ_W - IN_COLS))) if k == "w_in" else cat
        return W

    def small_weights(l):
        return {**{k: a[l] for k, a in conv.items()}, **{k: a[l:l + 1] for k, a in vec.items()}}

    def late_l0(mix_in):
        _, landed = _split_wait(started0, len(rest), plan0, mix_in, "weights_gather_l0_wait")
        return matrices(0, rest, _pass_to_sibling(landed, [half_of[k] for k in rest], "weights_gather_l0_sibling"))

    cos, sgn_sin = _rope_tables(x.shape[1])
    W0 = {**small_weights(0), **matrices(0, ["w_in"], got_in[:1])}
    W0_first = dict(W0, norm_pre_mix=W0["norm_pre_mix"] + started1[-1][0, 0])
    x1, h1_l1, saved0 = _layer_fwd(x[0], W0_first, cos, sgn_sin, 0, late_weights=late_l0,
                                   next_pre_mix=norm_pre_mix[1:2])
    _, landed1 = _split_wait(started1, len(mats), plan1, x1, "weights_gather_l1_wait")
    W1 = {**small_weights(1),
          **matrices(1, mats, _pass_to_sibling(landed1, [half_of[k] for k in mats], "weights_gather_l1_sibling"))}
    x2, _, saved1 = _layer_fwd(x1, W1, cos, sgn_sin, 1, h1=h1_l1)
    loss_local, dy = _loss_head(x2, loss_target[0], "loss_head")
    loss = lax.psum(loss_local, ("x", "y", "c"))

    def shard_major(names, grads_l):
        return [_shard_major(name, grads_l[name]) for name in names]

    def add_siblings(l, names, gs, from_sib):
        return [_add_sibling(g, r, c_arr, tiles[name], f"add_sibling_{name}{l}") for g, r, name in zip(gs, from_sib, names)]

    def scatter_start(l, names, parts, after, tag):
        return _split_copy(parts, [SDS((3,) + p.shape[1:], p.dtype) for p in parts], _scatter_plan, 3, after,
                           f"grads_l{l}{tag}_scatter_start")

    def owner_sums(l, names, sent, after, tag):
        parts, recvd = _split_wait(sent, len(names), _scatter_plan, after, f"grads_l{l}{tag}_scatter_wait")
        return [_add_chips(p, r, sc_arr, tiles[name], f"add_chips_{name}{l}") for p, r, name in zip(parts, recvd, names)]

    (dx1, dh1_l1), grads1, _ = _layer_bwd(dy, saved1, W1, cos, sgn_sin, 1, first_layer=False)
    gs1 = shard_major(mats, grads1)
    swap1 = _split_copy(gs1, [SDS((4, g.shape[1] // 2, g.shape[2]), g.dtype) for g in gs1], _exchange_plan, 1, dx1,
                        "grads_l1_sibling_start")
    ffn = ["ffn_w_in", "ffn_w_out"]
    launched = {}

    def after_ffn_l0(g_ffn, dx_mid):
        gs1_, from_sib1 = _split_wait(swap1, len(mats), _exchange_plan, dx_mid, "grads_l1_sibling_wait")
        launched["l1"] = scatter_start(1, mats, add_siblings(1, mats, gs1_, from_sib1), dx_mid, "")
        gs0 = shard_major(ffn, g_ffn)
        from_sib0 = _exchange_halves(gs0, "grads_l0_ffn_to_sibling")
        launched["l0_ffn"] = scatter_start(0, ffn, add_siblings(0, ffn, gs0, from_sib0), launched["l1"][-1], "_ffn")
        return launched["l0_ffn"][-1][0, 0]

    W0_last = dict(W0, **saved0["late"], norm_post_ffn=W0["norm_post_ffn"] + swap1[-1][0, 0])
    dx, grads0, grads1["norm_pre_mix"] = _layer_bwd(dx1, saved0, W0_last, cos, sgn_sin, 0, after_ffn=after_ffn_l0,
                                                    next_layer=(dh1_l1, norm_pre_mix[1:2]))
    mix = ["w_in", "w_out"]
    gs0 = shard_major(mix, grads0)
    part0 = add_siblings(0, mix, gs0, _exchange_halves(gs0, "grads_l0_mix_to_sibling"))
    sent0 = scatter_start(0, mix, part0, dx, "_mix")
    red = dict(zip([(0, k) for k in ffn], owner_sums(0, ffn, launched["l0_ffn"], sent0[-1], "_ffn")))
    red.update(zip([(1, k) for k in mats], owner_sums(1, mats, launched["l1"], sent0[-1], "")))
    early = [(0, k) for k in ffn] + [(1, k) for k in mats]
    joined = dict(zip(early, _join_halves([red[key] for key in early], "grads_join_early")))
    grads = [grads0, grads1]

    small = {}
    for name in _SMALL:
        per_layer = [grads[l][name] for l in range(2)]
        if name in ("dn_a_log", "dn_dt_bias"):
            per_layer = [p[:, :N_HEADS_D] for p in per_layer]
        small[name] = jnp.stack(per_layer).reshape((2,) + (w[name].shape[1:] if name not in ("dn_conv_w", "ffn_conv_w")
                                                           else per_layer[0].shape))
    flat = jnp.concatenate([small[name].reshape(-1) for name in _SMALL])
    n_rows = -(-flat.shape[0] // 1024) * 8
    summed = _all_reduce_small(jnp.pad(flat, (0, n_rows * 128 - flat.shape[0])).reshape(n_rows, 128),
                               "small_grads_all_reduce").reshape(-1)
    off = 0
    g_out = {}
    for name in _SMALL:
        size = small[name].size
        g_out[name] = summed[off:off + size].reshape(small[name].shape)
        off += size
    g_out["dn_conv_w"] = lax.dynamic_slice_in_dim(g_out["dn_conv_w"], s_me * 384, 384, axis=2)
    g_out["ffn_conv_w"] = lax.dynamic_slice_in_dim(g_out["ffn_conv_w"], s_me * 1408, 1408, axis=2)
    for k in ffn:
        g_out[k] = jnp.stack([joined[(0, k)], joined[(1, k)]])

    deltas, new_m, new_v = {}, {}, {}

    def step(name):
        shape = w[name].shape
        as3 = (lambda a: a) if len(shape) == 3 else (lambda a: a.reshape(shape[0], 1, shape[1]))
        tr = _ADAM_ROWS.get(name, as3(w[name]).shape[1])
        d_, m_, v_ = _adamw(as3(w[name]), as3(g_out[name]), as3(m[name]), as3(v[name]), tr, f"adamw_{name}")
        deltas[name], new_m[name], new_v[name] = d_.reshape(shape), m_.reshape(shape), v_.reshape(shape)

    for name in _WEIGHTS:
        if name not in mix:
            step(name)
    done = jnp.reshape(deltas["ffn_w_in"][0, 0, 0] + deltas["ffn_w_out"][0, 0, 0] + deltas["norm_post_ffn"][0, 0], (1,))
    late = _join_halves(owner_sums(0, mix, sent0, done, "_mix"), "grads_join_late")
    for k, a in zip(mix, late):
        g_out[k] = jnp.stack([a, joined[(1, k)]])
        step(k)

    return (loss, dx[None], *[g_out[k] for k in _WEIGHTS], *[deltas[k] for k in _WEIGHTS],
            *[new_m[k] for k in _WEIGHTS], *[new_v[k] for k in _WEIGHTS])
```

```python
import jax
import jax.numpy as jnp
from jax import lax
from jax.experimental import pallas as pl
from jax.experimental.pallas import tpu as pltpu

f32, bf16 = jnp.float32, jnp.bfloat16
SDS = jax.ShapeDtypeStruct
HI = lax.Precision.HIGH
MESH = pl.DeviceIdType.MESH
ANY = pl.BlockSpec(memory_space=pl.ANY)

D_MODEL = 1024
N_HEADS_A, HEAD_DIM = 8, 64
ATTN_W = 512
N_HEADS_D, DK = 4, 128
CHUNK = 64
D_FF = 2816
IN_COLS = 3592
PROJ_W = 3840
BRANCHES = ((1, 16), (4, 4), (16, 1))
EPS = 1e-6
NEG = -1e30
ROW_TILE = 256
VMEM_LIMIT = 56 * 1024 * 1024

ADAM_LR, ADAM_B1, ADAM_B2, ADAM_EPS, ADAM_WD, ADAM_STEP = 0.001, 0.9, 0.999, 1e-08, 0.01, 10


def _params(*sem):
    return pltpu.CompilerParams(dimension_semantics=sem, vmem_limit_bytes=VMEM_LIMIT)


def _mm(a, b, mode, tm, tn, out_dtype, name, column_shards=False):
    if mode == "nn":
        (M, K), N = a.shape, b.shape[1]
        dims = (((1,), (0,)), ((), ()))
        a_spec = pl.BlockSpec((tm, K), lambda i, j: (i, 0))
        b_spec = pl.BlockSpec((K, tn), lambda i, j: (0, j))
    elif mode == "nt":
        (M, K), N = a.shape, b.shape[0]
        dims = (((1,), (1,)), ((), ()))
        a_spec = pl.BlockSpec((tm, K), lambda i, j: (i, 0))
        b_spec = pl.BlockSpec((tn, K), lambda i, j: (j, 0))
    else:
        (K, M), N = a.shape, b.shape[1]
        dims = (((0,), (0,)), ((), ()))
        a_spec = pl.BlockSpec((K, tm), lambda i, j: (0, i))
        b_spec = pl.BlockSpec((K, tn), lambda i, j: (0, j))
    assert M % tm == 0 and N % tn == 0, (name, M, N, tm, tn)

    def body(a_ref, b_ref, o_ref):
        o_ref[...] = lax.dot_general(a_ref[...].astype(bf16), b_ref[...].astype(bf16), dims,
                                     preferred_element_type=f32).astype(o_ref.dtype)

    if column_shards:
        out_spec, out_shape = pl.BlockSpec((None, tm, tn), lambda i, j: (j, i, 0)), SDS((N // tn, M, tn), out_dtype)
    else:
        out_spec, out_shape = pl.BlockSpec((tm, tn), lambda i, j: (i, j)), SDS((M, N), out_dtype)
    return pl.pallas_call(body, grid=(M // tm, N // tn), in_specs=[a_spec, b_spec], out_specs=out_spec,
                          out_shape=out_shape, name=name, compiler_params=_params("parallel", "arbitrary"))(a, b)


def _row_spec(r, tm):
    if isinstance(r, tuple):
        arr, width, cb = r
        return arr, pl.BlockSpec((tm, width), lambda i, j, cb=cb: (i, cb + j))
    return r, pl.BlockSpec((tm, r.shape[1]), lambda i, j: (i, j))


def _full_spec(p):
    return pl.BlockSpec(p.shape, lambda i, j: (0,) * p.ndim)


def _rows(fn, rows, params, outs, name, tm=ROW_TILE, ncol=1):
    arrs, specs = zip(*[_row_spec(r, tm) for r in rows])
    S = arrs[0].shape[0]
    nr, npar = len(rows), len(params)

    def body(*refs):
        vals = fn(*[r[...].astype(f32) for r in refs[:nr]], *[p[...] for p in refs[nr:nr + npar]])
        for o_ref, v in zip(refs[nr + npar:], vals):
            o_ref[...] = v.astype(o_ref.dtype)

    return pl.pallas_call(
        body, grid=(S // tm, ncol), in_specs=list(specs) + [_full_spec(p) for p in params],
        out_specs=[pl.BlockSpec((tm, w), lambda i, j: (i, j)) for w, _ in outs],
        out_shape=[SDS((S, w * ncol), dt) for w, dt in outs], name=name,
        compiler_params=_params("parallel", "parallel"))(*arrs, *params)


def _rows_vjp(fn, rows, params, cts, wrt_rows, wrt_params, name, adds=None, tm=ROW_TILE, ncol=1, row_dtype=f32):
    adds = adds or {}
    arrs, specs = zip(*[_row_spec(r, tm) for r in rows])
    carrs, cspecs = zip(*[_row_spec(c, tm) for c in cts])
    add_keys = sorted(adds)
    aarrs = [adds[k] for k in add_keys]
    S = arrs[0].shape[0]
    nr, npar, nc, na = len(rows), len(params), len(cts), len(aarrs)
    widths = [specs[k].block_shape[1] for k in wrt_rows]
    row_dtypes = row_dtype if isinstance(row_dtype, (list, tuple)) else [row_dtype] * len(wrt_rows)

    def body(*refs):
        first = jnp.logical_and(pl.program_id(0) == 0, pl.program_id(1) == 0)
        rv = [r[...].astype(f32) for r in refs[:nr]]
        pv = [p[...] for p in refs[nr:nr + npar]]
        cv = tuple(c[...].astype(f32) for c in refs[nr + npar:nr + npar + nc])
        av = dict(zip(add_keys, refs[nr + npar + nc:nr + npar + nc + na]))
        o = refs[nr + npar + nc + na:]
        _, vjp = jax.vjp(fn, *rv, *pv)
        g = vjp(cv)
        for n, k in enumerate(wrt_rows):
            val = g[k]
            if k in av:
                val = val + av[k][...]
            o[n][...] = val.astype(o[n].dtype)
        for n, k in enumerate(wrt_params):
            ref = o[len(wrt_rows) + n]

            @pl.when(first)
            def _(ref=ref):
                ref[...] = jnp.zeros_like(ref)

            ref[...] += g[nr + k]

    res = pl.pallas_call(
        body, grid=(S // tm, ncol),
        in_specs=list(specs) + [_full_spec(p) for p in params] + list(cspecs)
        + [pl.BlockSpec((tm, a.shape[1] // ncol), lambda i, j: (i, j)) for a in aarrs],
        out_specs=[pl.BlockSpec((tm, w), lambda i, j: (i, j)) for w in widths] + [_full_spec(params[k]) for k in wrt_params],
        out_shape=[SDS((S, w * ncol), dt) for w, dt in zip(widths, row_dtypes)]
        + [SDS(params[k].shape, f32) for k in wrt_params],
        name=name, compiler_params=_params("arbitrary", "arbitrary"))(*arrs, *params, *carrs, *aarrs)
    return res[:len(wrt_rows)], res[len(wrt_rows):]


def _rms(x, w):
    return x * lax.rsqrt(jnp.mean(x * x, axis=-1, keepdims=True) + EPS) * w


def _rms_fn(x, w):
    return (_rms(x, w),)


def _res_rms_fn(f, res, w):
    return (res + _rms(f, w),)


def _res_rms_pre_fn(f, res, w_post, w_pre):
    x1 = res + _rms(f, w_post)
    return x1, _rms(x1, w_pre)


def _swap_halves(x):
    lane = lax.broadcasted_iota(jnp.int32, x.shape, 1)
    first = (lane % HEAD_DIM) < (HEAD_DIM // 2)
    n = x.shape[1]
    return jnp.where(first, pltpu.roll(x, n - HEAD_DIM // 2, 1), pltpu.roll(x, HEAD_DIM // 2, 1))


def _rope_fwd_fn(q, k, cos, sgn_sin):
    scale = HEAD_DIM ** -0.5
    return ((q * cos + _swap_halves(q) * sgn_sin) * scale, k * cos + _swap_halves(k) * sgn_sin)


def _rope_bwd_fn(dq, dk, cos, sgn_sin):
    dq = dq * (HEAD_DIM ** -0.5)
    return (dq * cos + _swap_halves(dq * sgn_sin), dk * cos + _swap_halves(dk * sgn_sin))


def _nt(a, b):
    return lax.dot_general(a, b, (((1,), (1,)), ((), ())), preferred_element_type=f32)


def _tn(a, b):
    return lax.dot_general(a, b, (((0,), (0,)), ((), ())), preferred_element_type=f32)


def _band_rows(j, d, nb):
    r, i = j // nb, j % nb
    if d == 1:
        cur = pl.ds(pl.multiple_of(i * 128, 128), 128)
        prev = pl.ds(pl.multiple_of(jnp.maximum(i - 1, 0) * 128, 128), 128)
    else:
        cur = pl.ds(i * (128 * d) + r, 128, stride=d)
        prev = pl.ds(jnp.maximum(i - 1, 0) * (128 * d) + r, 128, stride=d)
    return cur, prev, (i == 0).astype(jnp.int32)


def _band_bias(bias_ref):
    a = lax.broadcasted_iota(jnp.int32, (256, 256), 0) % 128
    c = lax.broadcasted_iota(jnp.int32, (256, 256), 1)
    own = jnp.logical_and(c < 128, c <= a)
    before = jnp.logical_and(c >= 128, c - 128 >= a)
    bias_ref[0] = jnp.where(jnp.logical_or(own, before), 0.0, NEG)
    bias_ref[1] = jnp.where(own, 0.0, NEG)


def _stack_heads(x, head_a):
    return jnp.concatenate([jnp.where(head_a, x, 0.0), jnp.where(head_a, 0.0, x)], axis=0)


def _unstack_heads(x2, head_a):
    return jnp.where(head_a, x2[:128], x2[128:])


def _attn_fwd(qr, kr, proj, name):
    S = qr.shape[0]
    nblk = S // 128

    def body(q_ref, k_ref, v_ref, out_ref, lse_ref, bias_ref, *scr):
        head_a = lax.broadcasted_iota(jnp.int32, (1, 128), 1) < HEAD_DIM
        _band_bias(bias_ref)
        for b, (d, nb) in enumerate(BRANCHES):
            ob_ref, lb_ref = scr[2 * b], scr[2 * b + 1]

            def blk(j, carry, d=d, nb=nb, ob_ref=ob_ref, lb_ref=lb_ref):
                cur, prev, first = _band_rows(j, d, nb)
                q2 = _stack_heads(q_ref[cur, :], head_a).astype(bf16)
                if nb == 1:
                    k2, v2, bias = k_ref[cur, :].astype(bf16), v_ref[cur, :].astype(bf16), bias_ref[1][:, :128]
                else:
                    k2 = jnp.concatenate([k_ref[cur, :], k_ref[prev, :]], axis=0).astype(bf16)
                    v2 = jnp.concatenate([v_ref[cur, :], v_ref[prev, :]], axis=0).astype(bf16)
                    bias = bias_ref[first]
                s = _nt(q2, k2) + bias
                mx = jnp.max(s, axis=1, keepdims=True)
                p = jnp.exp(s - mx)
                l = jnp.sum(p, axis=1, keepdims=True)
                o = jnp.dot(p.astype(bf16), v2, preferred_element_type=f32) / l
                ob_ref[cur, :] = _unstack_heads(o, head_a)
                lb_ref[cur, :] = _unstack_heads(jnp.broadcast_to(mx + jnp.log(l), (256, 128)), head_a)
                return carry

            lax.fori_loop(0, nblk, blk, 0, unroll=16)
        l0, l1, l2 = scr[1][...], scr[3][...], scr[5][...]
        mx = jnp.maximum(jnp.maximum(l0, l1), l2)
        e0, e1, e2 = jnp.exp(l0 - mx), jnp.exp(l1 - mx), jnp.exp(l2 - mx)
        den = e0 + e1 + e2
        out_ref[...] = ((e0 * scr[0][...] + e1 * scr[2][...] + e2 * scr[4][...]) / den).astype(out_ref.dtype)
        lse_ref[...] = mx + jnp.log(den)

    pair = pl.BlockSpec((S, 128), lambda h: (0, h))
    return pl.pallas_call(
        body, grid=(N_HEADS_A // 2,),
        in_specs=[pair, pair, pl.BlockSpec((S, 128), lambda h: (0, 2 * ATTN_W // 128 + h))], out_specs=[pair, pair],
        out_shape=[SDS((S, ATTN_W), bf16), SDS((S, ATTN_W), f32)],
        scratch_shapes=[pltpu.VMEM((2, 256, 256), f32)] + [pltpu.VMEM((S, 128), f32)] * 6,
        name=name, compiler_params=_params("parallel"))(qr, kr, proj)


def _attn_bwd(qr, kr, proj, dmix_in, out, lse, name):
    S = qr.shape[0]
    nblk = S // 128

    def body(q_ref, k_ref, v_ref, do_ref, out_ref, lse_ref, dq_ref, dk_ref, dv_ref, bias_ref, t_ref):
        head_a = lax.broadcasted_iota(jnp.int32, (1, 128), 1) < HEAD_DIM
        _band_bias(bias_ref)
        x = do_ref[...] * out_ref[...].astype(f32)
        t_ref[...] = jnp.where(head_a, jnp.sum(jnp.where(head_a, x, 0.0), axis=1, keepdims=True),
                               jnp.sum(jnp.where(head_a, 0.0, x), axis=1, keepdims=True))
        dq_ref[...] = jnp.zeros_like(dq_ref)
        dk_ref[...] = jnp.zeros_like(dk_ref)
        dv_ref[...] = jnp.zeros_like(dv_ref)
        for d, nb in BRANCHES:
            def blk(j, carry, d=d, nb=nb):
                cur, prev, first = _band_rows(j, d, nb)
                q2 = _stack_heads(q_ref[cur, :], head_a).astype(bf16)
                do2 = _stack_heads(do_ref[cur, :], head_a).astype(bf16)
                t, lse_b = t_ref[cur, :], lse_ref[cur, :]
                t2 = jnp.concatenate([t[:, :1], t[:, HEAD_DIM:HEAD_DIM + 1]], axis=0)
                lse2 = jnp.concatenate([lse_b[:, :1], lse_b[:, HEAD_DIM:HEAD_DIM + 1]], axis=0)
                if nb == 1:
                    k2, v2, bias = k_ref[cur, :].astype(bf16), v_ref[cur, :].astype(bf16), bias_ref[1][:, :128]
                else:
                    k2 = jnp.concatenate([k_ref[cur, :], k_ref[prev, :]], axis=0).astype(bf16)
                    v2 = jnp.concatenate([v_ref[cur, :], v_ref[prev, :]], axis=0).astype(bf16)
                    bias = bias_ref[first]
                p = jnp.exp(_nt(q2, k2) + bias - lse2)
                ds = (p * (_nt(do2, v2) - t2)).astype(bf16)
                dq_ref[cur, :] += _unstack_heads(jnp.dot(ds, k2, preferred_element_type=f32), head_a)
                dk2, dv2 = _tn(ds, q2), _tn(p.astype(bf16), do2)
                dk_ref[cur, :] += dk2[:128]
                dv_ref[cur, :] += dv2[:128]
                if nb != 1:
                    dk_ref[prev, :] += dk2[128:]
                    dv_ref[prev, :] += dv2[128:]
                return carry

            lax.fori_loop(0, nblk, blk, 0, unroll=16)

    pair = pl.BlockSpec((S, 128), lambda h: (0, h))
    return pl.pallas_call(
        body, grid=(N_HEADS_A // 2,),
        in_specs=[pair, pair, pl.BlockSpec((S, 128), lambda h: (0, 2 * ATTN_W // 128 + h)), pair, pair, pair],
        out_specs=[pair] * 3, out_shape=[SDS((S, ATTN_W), f32)] * 3,
        scratch_shapes=[pltpu.VMEM((2, 256, 256), f32), pltpu.VMEM((S, 128), f32)],
        name=name, compiler_params=_params("parallel"))(qr, kr, proj, dmix_in, out, lse)


def _conv_val(x, w, K, rows):
    acc = x * w[K - 1:K, :]
    for s in range(1, K):
        acc = acc + jnp.where(rows >= s, pltpu.roll(x, s, 0), 0.0) * w[K - 1 - s:K - s, :]
    return acc


def _colconv_fwd(xs, ws, bs, K, fn, nblk, tc, outs, name):
    S = xs[0][0].shape[0]
    n = len(xs)
    has_b = bs is not None

    def body(*refs):
        rows = lax.broadcasted_iota(jnp.int32, (S, tc), 0)
        cs = []
        for k in range(n):
            c = _conv_val(refs[k][...].astype(f32), refs[n + k][...], K, rows)
            if has_b:
                c = c + refs[2 * n + k][...]
            cs.append(c)
        for o_ref, val in zip(refs[(3 if has_b else 2) * n:], fn(*cs)):
            o_ref[...] = val.astype(o_ref.dtype)

    def cspec(rows_, cb0):
        return pl.BlockSpec((rows_, tc), lambda j, cb0=cb0: (0, cb0 + j))

    in_specs = [cspec(S, cb) for _, cb in xs] + [cspec(K, cb) for _, cb in ws]
    args = [a for a, _ in xs] + [a for a, _ in ws]
    if has_b:
        in_specs += [cspec(1, cb) for _, cb in bs]
        args += [a for a, _ in bs]
    return pl.pallas_call(
        body, grid=(nblk,), in_specs=in_specs, out_specs=[cspec(S, 0) for _ in outs],
        out_shape=[SDS((S, nblk * tc), dt) for dt in outs], name=name, compiler_params=_params("parallel"))(*args)


def _colconv_bwd(xs, ws, bs, K, fn, douts, nblk, tc, name, dx_dtype=f32):
    S = xs[0][0].shape[0]
    n, nd = len(xs), len(douts)
    has_b = bs is not None
    nin = (3 if has_b else 2) * n

    def body(*refs):
        rows = lax.broadcasted_iota(jnp.int32, (S, tc), 0)
        x = [refs[k][...].astype(f32) for k in range(n)]
        w = [refs[n + k][...] for k in range(n)]
        cs = []
        for k in range(n):
            c = _conv_val(x[k], w[k], K, rows)
            if has_b:
                c = c + refs[2 * n + k][...]
            cs.append(c)
        _, vjp = jax.vjp(fn, *cs)
        dcs = vjp(tuple(r[...].astype(f32) for r in refs[nin:nin + nd]))
        o = refs[nin + nd:]
        for k in range(n):
            dc = dcs[k]
            dx = dc * w[k][K - 1:K, :]
            o[n + k][K - 1:K, :] = jnp.sum(dc * x[k], axis=0, keepdims=True)
            for s in range(1, K):
                dx = dx + jnp.where(rows < S - s, pltpu.roll(dc, S - s, 0), 0.0) * w[k][K - 1 - s:K - s, :]
                xsh = jnp.where(rows >= s, pltpu.roll(x[k], s, 0), 0.0)
                o[n + k][K - 1 - s:K - s, :] = jnp.sum(dc * xsh, axis=0, keepdims=True)
            o[k][...] = dx.astype(o[k].dtype)
            if has_b:
                o[2 * n + k][...] = jnp.sum(dc, axis=0, keepdims=True)

    def cspec(rows_, cb0):
        return pl.BlockSpec((rows_, tc), lambda j, cb0=cb0: (0, cb0 + j))

    in_specs = [cspec(S, cb) for _, cb in xs] + [cspec(K, cb) for _, cb in ws]
    args = [a for a, _ in xs] + [a for a, _ in ws]
    if has_b:
        in_specs += [cspec(1, cb) for _, cb in bs]
        args += [a for a, _ in bs]
    in_specs += [cspec(S, 0) for _ in douts]
    args += list(douts)
    W = nblk * tc
    out_specs = [cspec(S, 0)] * n + [cspec(K, 0)] * n + ([cspec(1, 0)] * n if has_b else [])
    out_shape = [SDS((S, W), dx_dtype)] * n + [SDS((K, W), f32)] * n + ([SDS((1, W), f32)] * n if has_b else [])
    res = pl.pallas_call(body, grid=(nblk,), in_specs=in_specs, out_specs=out_specs, out_shape=out_shape,
                         name=name, compiler_params=_params("parallel"))(*args)
    return res[:n], res[n:2 * n], res[2 * n:]


def _silu_fn(c):
    return (c * jax.nn.sigmoid(c),)


_GELU_C, _GELU_A = 0.7978845608028654, 0.044715


@jax.custom_vjp
def _geglu(gate, up):
    return 0.5 * gate * (1.0 + jnp.tanh(_GELU_C * (gate + _GELU_A * gate * gate * gate))) * up


def _geglu_vjp_fwd(gate, up):
    return _geglu(gate, up), (gate, up)


def _geglu_vjp_bwd(res, d):
    gate, up = res
    g2 = gate * gate
    t = jnp.tanh(_GELU_C * gate * (1.0 + _GELU_A * g2))
    h = 0.5 * (1.0 + t)
    dgelu = h + (0.5 * _GELU_C) * gate * (1.0 - t * t) * (1.0 + (3.0 * _GELU_A) * g2)
    return d * up * dgelu, d * (gate * h)


_geglu.defvjp(_geglu_vjp_fwd, _geglu_vjp_bwd)


def _geglu_fn(gate, up):
    return (_geglu(gate, up),)


def _softplus(x):
    u = jnp.exp(jnp.minimum(x, 20.0))
    small = u * (1.0 - 0.5 * u)
    return jnp.where(x > 20.0, x, jnp.where(u < 1e-4, small, jnp.log(1.0 + u)))


def _bmm(a, b, precision=None):
    return lax.dot_general(a, b, (((2,), (1,)), ((0,), (0,))), precision=precision, preferred_element_type=f32)


def _bnt(a, b, precision=None):
    return lax.dot_general(a, b, (((2,), (2,)), ((0,), (0,))), precision=precision, preferred_element_type=f32)


def _btn(a, b, precision=None):
    return lax.dot_general(a, b, (((1,), (1,)), ((0,), (0,))), precision=precision, preferred_element_type=f32)


@jax.custom_vjp
def _unit_lower_inverse(A):
    n = A.shape[-1]
    eye = (lax.broadcasted_iota(jnp.int32, (1, n, n), 1) == lax.broadcasted_iota(jnp.int32, (1, n, n), 2)).astype(f32)
    P = -A
    T = eye + P
    for _ in range(5):
        P = _bmm(P, P, HI)
        T = T + _bmm(T, P, HI)
    return T


def _unit_lower_inverse_fwd(A):
    T = _unit_lower_inverse(A)
    return T, T


def _unit_lower_inverse_bwd(T, dT):
    return (-_btn(T, _bnt(dT, T, HI), HI),)


_unit_lower_inverse.defvjp(_unit_lower_inverse_fwd, _unit_lower_inverse_bwd)


def _dn_prep_fn(q, k, v, ba, alog, dtb, h):
    G, C = q.shape[0], CHUNK
    lane = lax.broadcasted_iota(jnp.int32, (1, 1, 128), 2)

    def sel(arr, idx):
        return jnp.sum(jnp.where(lane == idx, arr, 0.0), axis=-1, keepdims=True)

    beta = jax.nn.sigmoid(sel(ba, h))
    g = -jnp.exp(sel(alog[None], h)) * _softplus(sel(ba, N_HEADS_D + h) + sel(dtb[None], h))
    qn = q * lax.rsqrt(jnp.sum(q * q, axis=-1, keepdims=True) + EPS) * (DK ** -0.5)
    kn = k * lax.rsqrt(jnp.sum(k * k, axis=-1, keepdims=True) + EPS)
    ii = lax.broadcasted_iota(jnp.int32, (1, C, C), 1)
    jj = lax.broadcasted_iota(jnp.int32, (1, C, C), 2)
    tril, strict = ii >= jj, ii > jj
    gsq = jnp.broadcast_to(g, (G, C, C))
    gcol = _bmm(jnp.broadcast_to(tril.astype(f32), (G, C, C)), gsq, HI)
    grow = _bmm(jnp.ones((G, C, C), f32), jnp.where(ii <= jj, gsq, 0.0), HI)
    decay = jnp.exp(jnp.where(tril, gcol - grow, NEG))
    gc = gcol[:, :, :1]
    glast = gcol[:, C - 1:C, :1]
    kb = kn * beta
    A = jnp.where(strict, _bnt(kb.astype(bf16), kn.astype(bf16)) * decay, 0.0)
    T = _unit_lower_inverse(A).astype(bf16)
    u = _bmm(T, (v * beta).astype(bf16))
    w = _bmm(T, (kb * jnp.exp(gc)).astype(bf16))
    qk = _bnt(qn.astype(bf16), kn.astype(bf16)) * decay
    qd = qn * jnp.exp(gc)
    kd = kn * jnp.exp(glast - gc)
    return u, w, qk, qd, kd, jnp.broadcast_to(jnp.exp(glast), (G, C, DK))


def _dn_scan_fn(u, w, qk, qd, kd, eg, St):
    b = lambda a: a.astype(bf16)
    vnew = u - _bmm(b(w), b(St))
    o = _bmm(b(qd), b(St)) + _bmm(b(qk), b(vnew))
    return o, St * eg[:, :1, :] + _btn(b(kd), b(vnew))


def _dn_post_fn(o, z, nw):
    return (_rms(o, nw) * (z * jax.nn.sigmoid(z)),)


DN_GROUP = 8


def _dn_prep_specs(S, rows):
    def col(first):
        return pl.BlockSpec((rows, DK), lambda i, h, first=first: (i, first // DK + h))

    par = pl.BlockSpec((1, 128), lambda i, h: (0, 0))
    return [col(0), col(N_HEADS_D * DK), col(2 * N_HEADS_D * DK),
            pl.BlockSpec((rows, 128), lambda i, h: (i, 3584 // 128)), par, par]


def _dn_prep(qkv, proj, alog, dtb, name):
    S = qkv.shape[0]
    G = DN_GROUP
    rows = G * CHUNK

    def body(q_ref, k_ref, v_ref, ba_ref, al_ref, dt_ref, u_ref, w_ref, qk_ref, qd_ref, kd_ref, eg_ref):
        h = pl.program_id(1)
        r3 = lambda ref: ref[...].reshape(G, CHUNK, 128)
        u, w, qk, qd, kd, eg = _dn_prep_fn(r3(q_ref), r3(k_ref), r3(v_ref), r3(ba_ref), al_ref[...], dt_ref[...], h)
        for ref, val in ((u_ref, u), (w_ref, w), (qd_ref, qd), (kd_ref, kd), (eg_ref, eg)):
            ref[...] = val.reshape(rows, DK)
        qk_ref[:, :CHUNK] = qk.reshape(rows, CHUNK)
        qk_ref[:, CHUNK:] = jnp.zeros((rows, DK - CHUNK), f32)

    out = pl.BlockSpec((rows, DK), lambda i, h: (i, h))
    return pl.pallas_call(
        body, grid=(S // rows, N_HEADS_D), in_specs=_dn_prep_specs(S, rows), out_specs=[out] * 6,
        out_shape=[SDS((S, N_HEADS_D * DK), f32)] * 6, name=name,
        compiler_params=_params("parallel", "parallel"))(qkv, qkv, qkv, proj, alog, dtb)


def _dn_prep_bwd(qkv, proj, alog, dtb, cts, name):
    S = qkv.shape[0]
    G = DN_GROUP
    rows = G * CHUNK

    def body(q_ref, k_ref, v_ref, ba_ref, al_ref, dt_ref, du_ref, dw_ref, dqk_ref, dqd_ref, dkd_ref, deg_ref,
             dq_ref, dk_ref, dv_ref, dba_ref, dal_ref, ddt_ref):
        i, h = pl.program_id(0), pl.program_id(1)
        r3 = lambda ref: ref[...].reshape(G, CHUNK, 128)
        _, vjp = jax.vjp(lambda q, k, v, ba, al, dt: _dn_prep_fn(q, k, v, ba, al, dt, h),
                         r3(q_ref), r3(k_ref), r3(v_ref), r3(ba_ref), al_ref[...], dt_ref[...])
        dqk = dqk_ref[:, :CHUNK].reshape(G, CHUNK, CHUNK)
        dq, dk, dv, dba, dal, ddt = vjp((r3(du_ref), r3(dw_ref), dqk, r3(dqd_ref), r3(dkd_ref), r3(deg_ref)))
        dq_ref[...] = dq.reshape(rows, DK)
        dk_ref[...] = dk.reshape(rows, DK)
        dv_ref[...] = dv.reshape(rows, DK)

        @pl.when(h == 0)
        def _():
            dba_ref[...] = jnp.zeros_like(dba_ref)

        @pl.when(jnp.logical_and(i == 0, h == 0))
        def _():
            dal_ref[...] = jnp.zeros_like(dal_ref)
            ddt_ref[...] = jnp.zeros_like(ddt_ref)

        dba_ref[...] += dba.reshape(rows, 128)
        dal_ref[...] += dal
        ddt_ref[...] += ddt

    hcol = pl.BlockSpec((rows, DK), lambda i, h: (i, h))
    par = pl.BlockSpec((1, 128), lambda i, h: (0, 0))
    W = N_HEADS_D * DK
    return pl.pallas_call(
        body, grid=(S // rows, N_HEADS_D), in_specs=_dn_prep_specs(S, rows) + [hcol] * 6,
        out_specs=[hcol] * 3 + [pl.BlockSpec((rows, 128), lambda i, h: (i, 0)), par, par],
        out_shape=[SDS((S, W), f32)] * 3 + [SDS((S, 128), f32), SDS((1, 128), f32), SDS((1, 128), f32)], name=name,
        compiler_params=_params("arbitrary", "arbitrary"))(qkv, qkv, qkv, proj, alog, dtb, *cts)


def _heads(x):
    return jnp.stack([x[:, DK * h:DK * (h + 1)] for h in range(N_HEADS_D)])


SCAN_CHUNKS = 4


def _dn_scan(pre, name):
    S = pre[0].shape[0]
    NCH = S // CHUNK
    rows = SCAN_CHUNKS * CHUNK

    def body(u_ref, w_ref, qk_ref, qd_ref, kd_ref, eg_ref, o_ref, st_ref, s_ref):
        @pl.when(pl.program_id(0) == 0)
        def _():
            s_ref[...] = jnp.zeros_like(s_ref)

        St = s_ref[...]
        for k in range(SCAN_CHUNKS):
            r = slice(k * CHUNK, (k + 1) * CHUNK)
            st_ref[k] = St
            o, St = _dn_scan_fn(_heads(u_ref[r, :]), _heads(w_ref[r, :]), _heads(qk_ref[r, :])[:, :, :CHUNK],
                                _heads(qd_ref[r, :]), _heads(kd_ref[r, :]), _heads(eg_ref[r, :]), St)
            for h in range(N_HEADS_D):
                o_ref[r, DK * h:DK * (h + 1)] = o[h]
        s_ref[...] = St

    blk = pl.BlockSpec((rows, N_HEADS_D * DK), lambda n: (n, 0))
    return pl.pallas_call(
        body, grid=(S // rows,), in_specs=[blk] * 6,
        out_specs=[blk, pl.BlockSpec((SCAN_CHUNKS, N_HEADS_D, DK, DK), lambda n: (n, 0, 0, 0))],
        out_shape=[SDS((S, N_HEADS_D * DK), f32), SDS((NCH, N_HEADS_D, DK, DK), f32)],
        scratch_shapes=[pltpu.VMEM((N_HEADS_D, DK, DK), f32)], name=name, compiler_params=_params("arbitrary"))(*pre)


def _dn_scan_bwd(pre, states, do, name):
    S = do.shape[0]
    rows = SCAN_CHUNKS * CHUNK
    steps = S // rows

    def body(u_ref, w_ref, qk_ref, qd_ref, kd_ref, eg_ref, st_ref, do_ref,
             du_ref, dw_ref, dqk_ref, dqd_ref, dkd_ref, deg_ref, ds_ref):
        @pl.when(pl.program_id(0) == 0)
        def _():
            ds_ref[...] = jnp.zeros_like(ds_ref)

        dS = ds_ref[...]
        for k in reversed(range(SCAN_CHUNKS)):
            r = slice(k * CHUNK, (k + 1) * CHUNK)
            _, vjp = jax.vjp(_dn_scan_fn, _heads(u_ref[r, :]), _heads(w_ref[r, :]), _heads(qk_ref[r, :])[:, :, :CHUNK],
                             _heads(qd_ref[r, :]), _heads(kd_ref[r, :]), _heads(eg_ref[r, :]), st_ref[k])
            du, dw, dqk, dqd, dkd, deg, dS = vjp((_heads(do_ref[r, :]), dS))
            for h in range(N_HEADS_D):
                c = slice(DK * h, DK * (h + 1))
                for ref, val in ((du_ref, du), (dw_ref, dw), (dqd_ref, dqd), (dkd_ref, dkd), (deg_ref, deg)):
                    ref[r, c] = val[h]
                dqk_ref[r, DK * h:DK * h + CHUNK] = dqk[h]
                dqk_ref[r, DK * h + CHUNK:DK * (h + 1)] = jnp.zeros((CHUNK, DK - CHUNK), f32)
        ds_ref[...] = dS

    blk = pl.BlockSpec((rows, N_HEADS_D * DK), lambda n: (steps - 1 - n, 0))
    return pl.pallas_call(
        body, grid=(steps,),
        in_specs=[blk] * 6 + [pl.BlockSpec((SCAN_CHUNKS, N_HEADS_D, DK, DK), lambda n: (steps - 1 - n, 0, 0, 0)), blk],
        out_specs=[blk] * 6, out_shape=[SDS((S, N_HEADS_D * DK), f32)] * 6,
        scratch_shapes=[pltpu.VMEM((N_HEADS_D, DK, DK), f32)], name=name,
        compiler_params=_params("arbitrary"))(*pre, states, do)


def _loss_head(y, t, name):
    S, D = y.shape
    tm = ROW_TILE

    def body(y_ref, t_ref, dy_ref, l_ref):
        i = pl.program_id(0)
        d = y_ref[...] - t_ref[...]
        dy_ref[...] = d * (1.0 / D)
        part = jnp.sum(jnp.sum(d * d, axis=1, keepdims=True), axis=0, keepdims=True) * (0.5 / D)

        @pl.when(i == 0)
        def _():
            l_ref[...] = jnp.zeros_like(l_ref)

        l_ref[...] += jnp.broadcast_to(part, l_ref.shape)

    spec = pl.BlockSpec((tm, D), lambda i: (i, 0))
    dy, l = pl.pallas_call(body, grid=(S // tm,), in_specs=[spec, spec],
                           out_specs=[spec, pl.BlockSpec((1, 128), lambda i: (0, 0))],
                           out_shape=[SDS((S, D), f32), SDS((1, 128), f32)], name=name,
                           compiler_params=_params("arbitrary"))(y, t)
    return l[0, 0], dy


def _adamw_refs(w_ref, g_ref, m_ref, v_ref, d_ref, mo_ref, vo_ref):
    gv = g_ref[...]
    m2 = ADAM_B1 * m_ref[...] + (1.0 - ADAM_B1) * gv
    v2 = ADAM_B2 * v_ref[...] + (1.0 - ADAM_B2) * (gv * gv)
    m_hat = m2 / (1.0 - ADAM_B1 ** ADAM_STEP)
    v_hat = v2 / (1.0 - ADAM_B2 ** ADAM_STEP)
    d_ref[...] = -ADAM_LR * (m_hat / (jnp.sqrt(v_hat) + ADAM_EPS) + ADAM_WD * w_ref[...])
    mo_ref[...] = m2
    vo_ref[...] = v2


def _adamw_small(ws, gs, ms, vs, name):
    n = len(ws)

    def body(*refs):
        for i in range(n):
            _adamw_refs(*[refs[k * n + i] for k in range(7)])

    res = pl.pallas_call(body, out_shape=[SDS(a.shape, f32) for a in ws] * 3, name=name)(*ws, *gs, *ms, *vs)
    return res[:n], res[n:2 * n], res[2 * n:]


def _adamw(w, g, m, v, tr, name):
    L, R, C = w.shape
    assert R % tr == 0

    def body(*refs):
        _adamw_refs(*refs)

    spec = pl.BlockSpec((1, tr, C), lambda l, i: (l, i, 0))
    return pl.pallas_call(body, grid=(L, R // tr), in_specs=[spec] * 4, out_specs=[spec] * 3,
                          out_shape=[SDS((L, R, C), f32)] * 3, name=name,
                          compiler_params=_params("parallel", "parallel"))(w, g, m, v)


def _rope_tables(S):
    inv = 1.0 / (10000.0 ** (jnp.arange(0, HEAD_DIM, 2, dtype=f32) / HEAD_DIM))
    ang = jnp.arange(S, dtype=f32)[:, None] * inv[None, :]
    cos, sin = jnp.cos(ang), jnp.sin(ang)
    return (jnp.tile(jnp.concatenate([cos, cos], axis=1), (1, N_HEADS_A)),
            jnp.tile(jnp.concatenate([-sin, sin], axis=1), (1, N_HEADS_A)))


def _layer_fwd(x, W, cos, sgn_sin, l, late_weights=None, h1=None, next_pre_mix=None):
    n = f"l{l}_"
    if h1 is None:
        (h1,) = _rows(_rms_fn, [x], [W["norm_pre_mix"]], [(D_MODEL, bf16)], n + "pre_mix_norm")
    proj = _mm(h1, W["w_in"], "nn", 1024, 768, f32, n + "in_proj")
    qr, kr = _rows(_rope_fwd_fn, [(proj, ATTN_W, 0), (proj, ATTN_W, 1), cos, sgn_sin], [],
                   [(ATTN_W, f32), (ATTN_W, f32)], n + "rope")
    attn_out, lse = _attn_fwd(qr, kr, proj, n + "attn_fwd")
    (qkv,) = _colconv_fwd([(proj, 3)], [(W["dn_conv_w"], 0)], None, 4, _silu_fn, 3, 512, [f32], n + "dn_conv")
    dn_pre = _dn_prep(qkv, proj, W["dn_a_log"], W["dn_dt_bias"], n + "dn_prep")
    dn_o, dn_states = _dn_scan(dn_pre, n + "dn_scan")
    (dn_out,) = _rows(_dn_post_fn, [(dn_o, DK, 0), (proj, DK, 3072 // DK)], [W["dn_norm_w"]], [(DK, bf16)], n + "dn_post",
                      ncol=N_HEADS_D, tm=4 * ROW_TILE)
    mix_in = jnp.concatenate([attn_out, dn_out], axis=1)
    late = late_weights(mix_in) if late_weights is not None else {}
    W = {**W, **late}
    mix = _mm(mix_in, W["w_out"], "nn", 512, 512, f32, n + "out_proj")
    x1, h2 = _rows(_res_rms_pre_fn, [mix, x], [W["norm_post_mix"], W["norm_pre_ffn"]],
                   [(D_MODEL, f32), (D_MODEL, bf16)], n + "post_mix_pre_ffn_norm")
    u0 = _mm(h2, W["ffn_w_in"], "nn", 1024, 512, bf16, n + "ffn_in")
    nb_ff = D_FF // 256
    (act,) = _colconv_fwd([(u0, 0), (u0, nb_ff)], [(W["ffn_conv_w"], 0), (W["ffn_conv_w"], nb_ff)],
                          [(W["ffn_conv_b"], 0), (W["ffn_conv_b"], nb_ff)], 3, _geglu_fn, nb_ff, 256, [bf16],
                          n + "ffn_conv_glu")
    f = _mm(act, W["ffn_w_out"], "nn", 512, 512, f32, n + "ffn_out")
    if next_pre_mix is None:
        (x2,), h1_next = _rows(_res_rms_fn, [f, x1], [W["norm_post_ffn"]], [(D_MODEL, f32)], n + "post_ffn_norm"), None
    else:
        x2, h1_next = _rows(_res_rms_pre_fn, [f, x1], [W["norm_post_ffn"], next_pre_mix],
                            [(D_MODEL, f32), (D_MODEL, bf16)], n + "post_ffn_next_pre_mix_norm")
    saved = dict(x=x, h1=h1, proj=proj, qr=qr, kr=kr, attn_out=attn_out, lse=lse, qkv=qkv, dn_pre=dn_pre, dn_o=dn_o,
                 dn_states=dn_states, mix_in=mix_in, mix=mix, x1=x1, h2=h2, u0=u0, act=act, f=f, late=late)
    return x2, h1_next, saved


def _layer_bwd(dx2, sv, W, cos, sgn_sin, l, after_ffn=None, next_layer=None, first_layer=True):
    n = f"l{l}_"
    S = dx2.shape[0]
    g = {}
    g_next_pre = None
    if next_layer is None:
        (df,), (g["norm_post_ffn"],) = _rows_vjp(_rms_fn, [sv["f"]], [W["norm_post_ffn"]], [dx2], [0], [0],
                                                 n + "post_ffn_norm_bwd", row_dtype=bf16)
    else:
        (df, dx2), (g["norm_post_ffn"], g_next_pre) = _rows_vjp(
            _res_rms_pre_fn, [sv["f"], sv["x1"]], [W["norm_post_ffn"], next_layer[1]], [dx2, next_layer[0]], [0, 1], [0, 1],
            n + "post_ffn_next_pre_mix_norm_bwd", row_dtype=[bf16, f32])
    dact = _mm(df, W["ffn_w_out"], "nt", 512, 1408, f32, n + "ffn_out_dx")
    g["ffn_w_out"] = _mm(sv["act"], df, "tn", 256, 1024, f32, n + "ffn_out_dw")
    nb_ff = D_FF // 256
    u0 = sv["u0"]
    dxs, dws, dbs = _colconv_bwd([(u0, 0), (u0, nb_ff)], [(W["ffn_conv_w"], 0), (W["ffn_conv_w"], nb_ff)],
                                 [(W["ffn_conv_b"], 0), (W["ffn_conv_b"], nb_ff)], 3, _geglu_fn, [dact], nb_ff, 256,
                                 n + "ffn_conv_glu_bwd", dx_dtype=bf16)
    du0 = jnp.concatenate(dxs, axis=1)
    g["ffn_conv_w"] = jnp.concatenate(dws, axis=1)
    g["ffn_conv_b"] = jnp.concatenate(dbs, axis=1)
    dh2 = _mm(du0, W["ffn_w_in"], "nt", 512, 512, f32, n + "ffn_in_dx")
    g["ffn_w_in"] = _mm(sv["h2"], du0, "tn", 512, D_FF // 2, f32, n + "ffn_in_dw", column_shards=True)
    if after_ffn is not None:
        W = dict(W, norm_post_mix=W["norm_post_mix"] + after_ffn(g, dh2))
    (dmix, dx1), (g["norm_post_mix"], g["norm_pre_ffn"]) = _rows_vjp(
        _res_rms_pre_fn, [sv["mix"], sv["x"]], [W["norm_post_mix"], W["norm_pre_ffn"]], [dx2, dh2], [0, 1], [0, 1],
        n + "post_mix_pre_ffn_norm_bwd", row_dtype=[bf16, f32])
    dmix_in = _mm(dmix, W["w_out"], "nt", 512, 512, f32, n + "out_proj_dx")
    g["w_out"] = _mm(sv["mix_in"], dmix, "tn", 512, 512, f32, n + "out_proj_dw")

    (ddn_o, dz), (g["dn_norm_w"],) = _rows_vjp(
        _dn_post_fn, [(sv["dn_o"], DK, 0), (sv["proj"], DK, 3072 // DK)], [W["dn_norm_w"]], [(dmix_in, DK, ATTN_W // DK)],
        [0, 1], [0], n + "dn_post_bwd", ncol=N_HEADS_D, tm=4 * ROW_TILE)
    dpre = _dn_scan_bwd(sv["dn_pre"], sv["dn_states"], ddn_o, n + "dn_scan_bwd")
    dq, dk, dv, dba, g["dn_a_log"], g["dn_dt_bias"] = _dn_prep_bwd(
        sv["qkv"], sv["proj"], W["dn_a_log"], W["dn_dt_bias"], dpre, n + "dn_prep_bwd")
    dqkv = jnp.concatenate([dq, dk, dv], axis=1)
    (dqkv0,), (g["dn_conv_w"],), _ = _colconv_bwd([(sv["proj"], 3)], [(W["dn_conv_w"], 0)], None, 4, _silu_fn,
                                                 [dqkv], 3, 512, n + "dn_conv_bwd")

    dqr, dkr, dav = _attn_bwd(sv["qr"], sv["kr"], sv["proj"], dmix_in, sv["attn_out"], sv["lse"], n + "attn_bwd")
    daq, dak = _rows(_rope_bwd_fn, [dqr, dkr, cos, sgn_sin], [], [(ATTN_W, f32)] * 2, n + "rope_bwd")
    dproj = jnp.concatenate([daq, dak, dav, dqkv0, dz, dba, jnp.zeros((S, PROJ_W - 3712), f32)], axis=1).astype(bf16)
    dh1 = _mm(dproj, W["w_in"], "nt", 512, 512, f32, n + "in_proj_dx")
    g["w_in"] = _mm(sv["h1"], dproj, "tn", 512, 768, f32, n + "in_proj_dw")
    if not first_layer:
        return (dx1, dh1), g, g_next_pre
    (dx,), (g["norm_pre_mix"],) = _rows_vjp(_rms_fn, [sv["x"]], [W["norm_pre_mix"]], [dh1], [0], [0],
                                            n + "pre_mix_norm_bwd", adds={0: dx1})
    return dx, g, g_next_pre


def _local_step(x, target, layers):
    cos, sgn_sin = _rope_tables(x.shape[0])
    saved, h1 = [], None
    for l, W in enumerate(layers):
        nxt = layers[l + 1]["norm_pre_mix"] if l + 1 < len(layers) else None
        x, h1, sv = _layer_fwd(x, W, cos, sgn_sin, l, h1=h1, next_pre_mix=nxt)
        saved.append(sv)
    loss, dx = _loss_head(x, target, "loss_head")
    grads = [None] * len(layers)
    nxt = None
    for l in reversed(range(len(layers))):
        dx, grads[l], g_pre = _layer_bwd(dx, saved[l], layers[l], cos, sgn_sin, l, next_layer=nxt, first_layer=(l == 0))
        if g_pre is not None:
            grads[l + 1]["norm_pre_mix"] = g_pre
        if l > 0:
            dx, dh1 = dx
            nxt = (dh1, layers[l]["norm_pre_mix"])
    return loss, dx, grads


def _pos():
    x, y, c = lax.axis_index("x"), lax.axis_index("y"), lax.axis_index("c")
    return x, y, c, [(1 - x, y), (x, 1 - y), (1 - x, 1 - y)]


def _rcopy(src, dst, send_sem, recv_sem, dev):
    return pltpu.make_async_remote_copy(src_ref=src, dst_ref=dst, send_sem=send_sem, recv_sem=recv_sem,
                                        device_id=dev, device_id_type=MESH)


def _half_rows(ref, h, which, axis):
    if h is None:
        return ref
    rows = pl.ds(pl.multiple_of(which * h, 16), h)
    return ref.at[:, rows, :] if axis == 1 else ref.at[rows, :]


def _dma_sems(*counts):
    return [pltpu.SemaphoreType.DMA((k,)) for k in counts]


def _all_gather(arrs, halves, name):
    n = len(arrs)

    def body(*refs):
        ins, outs = refs[:n], refs[n:2 * n]
        send1, recv1, send2, recv2 = refs[2 * n:]
        x, y, c, chips = _pos()
        me, sib, s_me = (x, y, c), (x, y, 1 - c), 2 * x + y
        sends = []
        for i in range(n):
            for j, chip in enumerate(chips):
                cp = _rcopy(_half_rows(ins[i], halves[i], c, 1), _half_rows(outs[i].at[s_me], halves[i], c, 1),
                            send1.at[3 * i + j], recv1.at[3 * i + j], (*chip, c))
                cp.start()
                sends.append(cp)
        for i in range(n):
            for j, (px, py) in enumerate(chips):
                k = 3 * i + j
                landed = _half_rows(outs[i].at[2 * px + py], halves[i], c, 1)
                _rcopy(landed, landed, send1.at[k], recv1.at[k], me).wait_recv()
                if halves[i] is not None:
                    cp = _rcopy(landed, landed, send2.at[k], recv2.at[k], sib)
                    cp.start()
                    sends.append(cp)
        for i in range(n):
            if halves[i] is None:
                continue
            for j, (px, py) in enumerate(chips):
                k = 3 * i + j
                other = _half_rows(outs[i].at[2 * px + py], halves[i], 1 - c, 1)
                _rcopy(other, other, send2.at[k], recv2.at[k], me).wait_recv()
        for cp in sends:
            cp.wait_send()

    return pl.pallas_call(
        body, in_specs=[ANY] * n, out_specs=[ANY] * n,
        out_shape=[SDS((4,) + a.shape, a.dtype) for a in arrs],
        scratch_shapes=_dma_sems(3 * n, 3 * n, 3 * n, 3 * n), name=name)(*arrs)


HBM = pl.BlockSpec(memory_space=pltpu.HBM)
SEM = pl.BlockSpec(memory_space=pltpu.SEMAPHORE)
_EFFECT = pltpu.SideEffectType.DATAFLOW_SIDE_EFFECTING


def _in_hbm(a):
    return pltpu.with_memory_space_constraint(a, pltpu.HBM)


def _split_copy(srcs, land_shapes, plan, per, after, name):
    n = len(srcs)
    k = per * n

    def body(*refs):
        ins, lands, token = refs[:n], refs[n:2 * n], refs[-1]
        send, recv = refs[2 * n + 1], refs[2 * n + 2]
        for i, (src, dst, dev, _) in enumerate(plan(ins, lands)):
            _rcopy(src, dst, send.at[i], recv.at[i], dev).start()
        token[...] = jnp.zeros_like(token)

    lands = [_in_hbm(lax.empty(s.shape, s.dtype)) for s in land_shapes]
    return pl.pallas_call(
        body, name=name,
        out_shape=(pltpu.SemaphoreType.DMA((k,)), pltpu.SemaphoreType.DMA((k,)),
                   *[pltpu.HBM(a.shape, a.dtype) for a in srcs], *[pltpu.HBM(s.shape, s.dtype) for s in land_shapes],
                   SDS((8, 128), f32)),
        in_specs=[HBM] * (2 * n) + [ANY], out_specs=(SEM, SEM, *[HBM] * (2 * n), pl.BlockSpec(memory_space=pltpu.VMEM)),
        input_output_aliases={i: 2 + i for i in range(2 * n)},
        compiler_params=pltpu.CompilerParams(has_side_effects=_EFFECT))(*[_in_hbm(a) for a in srcs], *lands, after)


def _split_wait(started, n, plan, after, name):
    send, recv = started[0], started[1]
    thru = started[2:2 + 2 * n]

    def body(*refs):
        ins, lands = refs[:n], refs[n:2 * n]
        send_ref, recv_ref = refs[2 * n], refs[2 * n + 1]
        for i, (src, _, dev, mine) in enumerate(plan(ins, lands)):
            cp = _rcopy(src, mine, send_ref.at[i], recv_ref.at[i], dev)
            cp.wait_send()
            cp.wait_recv()

    res = pl.pallas_call(
        body, name=name, out_shape=tuple(pltpu.HBM(a.shape, a.dtype) for a in thru),
        in_specs=[HBM] * (2 * n) + [SEM, SEM, ANY], out_specs=tuple([HBM] * (2 * n)),
        input_output_aliases={i: i for i in range(2 * n)},
        compiler_params=pltpu.CompilerParams(has_side_effects=_EFFECT))(*thru, send, recv, after)
    return res[:n], res[n:]


def _gather_plan(halves):
    def plan(ins, lands):
        x, y, c, chips = _pos()
        out = []
        for i in range(len(ins)):
            for px, py in chips:
                out.append((_half_rows(ins[i], halves[i], c, 1), _half_rows(lands[i].at[2 * x + y], halves[i], c, 1),
                            (px, py, c), _half_rows(lands[i].at[2 * px + py], halves[i], c, 1)))
        return out
    return plan


def _scatter_plan(ins, lands):
    x, y, c, chips = _pos()
    out = []
    for i in range(len(ins)):
        for j, (px, py) in enumerate(chips):
            out.append((ins[i].at[2 * px + py], lands[i].at[j], (px, py, c), lands[i].at[j]))
    return out


def _exchange_plan(ins, lands):
    x, y, c, _ = _pos()
    return [(_half_rows(g, g.shape[1] // 2, 1 - c, 1), land, (x, y, 1 - c), land) for g, land in zip(ins, lands)]


def _pass_to_sibling(lands, halves, name):
    n = len(lands)

    def body(*refs):
        outs = refs[n:2 * n]
        send, recv = refs[2 * n:]
        x, y, c, chips = _pos()
        sends = []
        for i in range(n):
            for j, (px, py) in enumerate(chips):
                landed = _half_rows(outs[i].at[2 * px + py], halves[i], c, 1)
                cp = _rcopy(landed, landed, send.at[3 * i + j], recv.at[3 * i + j], (x, y, 1 - c))
                cp.start()
                sends.append(cp)
        for i in range(n):
            for j, (px, py) in enumerate(chips):
                other = _half_rows(outs[i].at[2 * px + py], halves[i], 1 - c, 1)
                _rcopy(other, other, send.at[3 * i + j], recv.at[3 * i + j], (x, y, c)).wait_recv()
        for cp in sends:
            cp.wait_send()

    return pl.pallas_call(
        body, in_specs=[ANY] * n, out_specs=[ANY] * n, out_shape=[SDS(a.shape, a.dtype) for a in lands],
        input_output_aliases={k: k for k in range(n)}, scratch_shapes=_dma_sems(3 * n, 3 * n), name=name)(*lands)


def _exchange_halves(gs, name):
    n = len(gs)

    def body(*refs):
        ins, outs = refs[:n], refs[n:2 * n]
        send, recv = refs[2 * n:]
        x, y, c, _ = _pos()
        sends = []
        for k in range(n):
            cp = _rcopy(_half_rows(ins[k], gs[k].shape[1] // 2, 1 - c, 1), outs[k], send.at[k], recv.at[k], (x, y, 1 - c))
            cp.start()
            sends.append(cp)
        for k in range(n):
            _rcopy(outs[k], outs[k], send.at[k], recv.at[k], (x, y, c)).wait_recv()
        for cp in sends:
            cp.wait_send()

    return pl.pallas_call(
        body, in_specs=[ANY] * n, out_specs=[ANY] * n,
        out_shape=[SDS((4, g.shape[1] // 2, g.shape[2]), g.dtype) for g in gs],
        scratch_shapes=_dma_sems(n, n), name=name)(*gs)


def _scatter_partials(ps, name):
    n = len(ps)

    def body(*refs):
        ins, outs = refs[:n], refs[n:2 * n]
        send, recv = refs[2 * n:]
        x, y, c, chips = _pos()
        sends = []
        for k in range(n):
            for j, (px, py) in enumerate(chips):
                cp = _rcopy(ins[k].at[2 * px + py], outs[k].at[j], send.at[3 * k + j], recv.at[3 * k + j], (px, py, c))
                cp.start()
                sends.append(cp)
        for k in range(n):
            for j in range(3):
                _rcopy(outs[k].at[j], outs[k].at[j], send.at[3 * k + j], recv.at[3 * k + j], (x, y, c)).wait_recv()
        for cp in sends:
            cp.wait_send()

    return pl.pallas_call(
        body, in_specs=[ANY] * n, out_specs=[ANY] * n,
        out_shape=[SDS((3,) + p.shape[1:], p.dtype) for p in ps],
        scratch_shapes=_dma_sems(3 * n, 3 * n), name=name)(*ps)


def _join_halves(rs, name):
    n = len(rs)

    def body(*refs):
        outs = refs[n:2 * n]
        send, recv = refs[2 * n:]
        x, y, c, _ = _pos()
        sends = []
        for k in range(n):
            mine = _half_rows(outs[k], rs[k].shape[0] // 2, c, 0)
            cp = _rcopy(mine, mine, send.at[k], recv.at[k], (x, y, 1 - c))
            cp.start()
            sends.append(cp)
        for k in range(n):
            other = _half_rows(outs[k], rs[k].shape[0] // 2, 1 - c, 0)
            _rcopy(other, other, send.at[k], recv.at[k], (x, y, c)).wait_recv()
        for cp in sends:
            cp.wait_send()

    return pl.pallas_call(
        body, in_specs=[ANY] * n, out_specs=[ANY] * n, out_shape=[SDS(r.shape, r.dtype) for r in rs],
        input_output_aliases={k: k for k in range(n)}, scratch_shapes=_dma_sems(n, n), name=name)(*rs)


def _all_reduce_small(pack, name):
    R = pack.shape[0]

    def body(in_ref, out_ref, buf, send, recv):
        x, y, c, _ = _pos()
        me = 4 * x + 2 * y + c
        buf[me] = in_ref[...]
        sends = []
        for k in range(1, 8):
            peer = me ^ k
            cp = _rcopy(buf.at[me], buf.at[me], send.at[k - 1], recv.at[k - 1], ((peer >> 2) & 1, (peer >> 1) & 1, peer & 1))
            cp.start()
            sends.append(cp)
        for k in range(1, 8):
            _rcopy(buf.at[me ^ k], buf.at[me ^ k], send.at[k - 1], recv.at[k - 1], (x, y, c)).wait_recv()
        for cp in sends:
            cp.wait_send()
        acc = buf[0]
        for d in range(1, 8):
            acc = acc + buf[d]
        out_ref[...] = acc

    return pl.pallas_call(
        body, out_shape=SDS((R, 128), f32),
        in_specs=[pl.BlockSpec(memory_space=pltpu.VMEM)], out_specs=pl.BlockSpec(memory_space=pltpu.VMEM),
        scratch_shapes=[pltpu.VMEM((8, R, 128), f32)] + _dma_sems(7, 7), name=name)(pack)


def _add_sibling(g, recv, c_arr, tr, name):
    _, R, C = g.shape
    h = R // 2
    nrb = h // tr
    assert h % tr == 0

    def body(c_ref, g_ref, r_ref, o_ref):
        o_ref[...] = (g_ref[...] + r_ref[...]).astype(o_ref.dtype)

    spec = pl.BlockSpec((1, tr, C), lambda s, r, c_ref: (s, r, 0))
    grid_spec = pltpu.PrefetchScalarGridSpec(
        num_scalar_prefetch=1, grid=(4, nrb),
        in_specs=[pl.BlockSpec((1, tr, C), lambda s, r, c_ref: (s, c_ref[0] * nrb + r, 0)), spec], out_specs=spec)
    return pl.pallas_call(body, grid_spec=grid_spec, out_shape=SDS((4, h, C), bf16), name=name,
                          compiler_params=_params("parallel", "parallel"))(c_arr, g, recv)


def _add_chips(p, recv, sc_arr, tr, name):
    _, h, C = p.shape
    nrb = h // tr
    assert h % tr == 0

    def body(sc_ref, p_ref, r_ref, o_ref):
        o_ref[...] = (p_ref[0].astype(f32) + r_ref[0].astype(f32)) + (r_ref[1].astype(f32) + r_ref[2].astype(f32))

    grid_spec = pltpu.PrefetchScalarGridSpec(
        num_scalar_prefetch=1, grid=(nrb,),
        in_specs=[pl.BlockSpec((1, tr, C), lambda r, sc_ref: (sc_ref[0], r, 0)),
                  pl.BlockSpec((3, tr, C), lambda r, sc_ref: (0, r, 0))],
        out_specs=pl.BlockSpec((tr, C), lambda r, sc_ref: (sc_ref[1] * nrb + r, 0)))
    return pl.pallas_call(body, grid_spec=grid_spec, out_shape=SDS((2 * h, C), f32), name=name,
                          compiler_params=_params("parallel"))(sc_arr, p, recv)


_BIG = (("w_in", 1024, 256), ("w_out", 256, 128), ("ffn_w_in", 1024, 256), ("ffn_w_out", 704, 352))
_SMALL = ("dn_conv_w", "ffn_conv_w", "ffn_conv_b", "norm_pre_mix", "norm_post_mix", "norm_pre_ffn", "norm_post_ffn",
          "dn_norm_w", "dn_a_log", "dn_dt_bias")
_WEIGHTS = ("w_in", "dn_conv_w", "dn_a_log", "dn_dt_bias", "dn_norm_w", "w_out", "ffn_w_in", "ffn_conv_w", "ffn_conv_b",
            "ffn_w_out", "norm_pre_mix", "norm_post_mix", "norm_pre_ffn", "norm_post_ffn")
_ADAM_ROWS = {"w_in": 256, "w_out": 256, "ffn_w_in": 128, "ffn_w_out": 176}


def _shard_major(name, g):
    if name == "w_in":
        return jnp.stack([g[:, 898 * s:898 * (s + 1)] for s in range(4)])
    if name == "ffn_w_in":
        return g
    return g.reshape(4, g.shape[0] // 4, g.shape[1])


def kernel(x, w_in, dn_conv_w, dn_a_log, dn_dt_bias, dn_norm_w, w_out, ffn_w_in, ffn_conv_w, ffn_conv_b, ffn_w_out, norm_pre_mix, norm_post_mix, norm_pre_ffn, norm_post_ffn, loss_target, m_w_in, m_dn_conv_w, m_dn_a_log, m_dn_dt_bias, m_dn_norm_w, m_w_out, m_ffn_w_in, m_ffn_conv_w, m_ffn_conv_b, m_ffn_w_out, m_norm_pre_mix, m_norm_post_mix, m_norm_pre_ffn, m_norm_post_ffn, v_w_in, v_dn_conv_w, v_dn_a_log, v_dn_dt_bias, v_dn_norm_w, v_w_out, v_ffn_w_in, v_ffn_conv_w, v_ffn_conv_b, v_ffn_w_out, v_norm_pre_mix, v_norm_post_mix, v_norm_pre_ffn, v_norm_post_ffn):
    w = dict(w_in=w_in, dn_conv_w=dn_conv_w, dn_a_log=dn_a_log, dn_dt_bias=dn_dt_bias, dn_norm_w=dn_norm_w, w_out=w_out,
             ffn_w_in=ffn_w_in, ffn_conv_w=ffn_conv_w, ffn_conv_b=ffn_conv_b, ffn_w_out=ffn_w_out, norm_pre_mix=norm_pre_mix,
             norm_post_mix=norm_post_mix, norm_pre_ffn=norm_pre_ffn, norm_post_ffn=norm_post_ffn)
    m = dict(w_in=m_w_in, dn_conv_w=m_dn_conv_w, dn_a_log=m_dn_a_log, dn_dt_bias=m_dn_dt_bias, dn_norm_w=m_dn_norm_w,
             w_out=m_w_out, ffn_w_in=m_ffn_w_in, ffn_conv_w=m_ffn_conv_w, ffn_conv_b=m_ffn_conv_b, ffn_w_out=m_ffn_w_out,
             norm_pre_mix=m_norm_pre_mix, norm_post_mix=m_norm_post_mix, norm_pre_ffn=m_norm_pre_ffn,
             norm_post_ffn=m_norm_post_ffn)
    v = dict(w_in=v_w_in, dn_conv_w=v_dn_conv_w, dn_a_log=v_dn_a_log, dn_dt_bias=v_dn_dt_bias, dn_norm_w=v_dn_norm_w,
             w_out=v_w_out, ffn_w_in=v_ffn_w_in, ffn_conv_w=v_ffn_conv_w, ffn_conv_b=v_ffn_conv_b, ffn_w_out=v_ffn_w_out,
             norm_pre_mix=v_norm_pre_mix, norm_post_mix=v_norm_post_mix, norm_pre_ffn=v_norm_pre_ffn,
             norm_post_ffn=v_norm_post_ffn)
    xi, yi, ci = lax.axis_index("x"), lax.axis_index("y"), lax.axis_index("c")
    s_me = 2 * xi + yi
    c_arr = jnp.reshape(ci, (1,)).astype(jnp.int32)
    sc_arr = jnp.stack([s_me, ci]).astype(jnp.int32)

    mats = [name for name, _, _ in _BIG]
    rest = mats[1:]
    half_of = {name: rows // 2 for name, rows, _ in _BIG}
    tiles = {name: tr for name, _, tr in _BIG}
    gathered_shape = lambda a: SDS((4,) + a.shape, a.dtype)

    own = {k: w[k].astype(bf16) for k in mats}
    got_in = _all_gather([own["w_in"][0:1], dn_conv_w, ffn_conv_w], [half_of["w_in"], None, None], "weights_gather_w_in0")
    plan0 = _gather_plan([half_of[k] for k in rest])
    src0 = [own[k][0:1] for k in rest]
    started0 = _split_copy(src0, [gathered_shape(a) for a in src0], plan0, 3, got_in[0], "weights_gather_l0_start")
    plan1 = _gather_plan([half_of[k] for k in mats])
    src1 = [own[k][1:2] for k in mats]
    started1 = _split_copy(src1, [gathered_shape(a) for a in src1], plan1, 3, started0[-1], "weights_gather_l1_start")

    def pick(mine, gathered):
        return [jnp.where(s_me == s, mine, gathered[s]) for s in range(4)]

    conv = {"dn_conv_w": jnp.concatenate(pick(dn_conv_w, got_in[1]), axis=-1),
            "ffn_conv_w": jnp.concatenate(pick(ffn_conv_w, got_in[2]), axis=-1)}
    lanes = lambda a: jnp.pad(a, ((0, 0), (0, 128 - a.shape[1])))
    vec = dict(dn_a_log=lanes(dn_a_log), dn_dt_bias=lanes(dn_dt_bias), dn_norm_w=dn_norm_w, ffn_conv_b=ffn_conv_b,
               norm_pre_mix=norm_pre_mix, norm_post_mix=norm_post_mix, norm_pre_ffn=norm_pre_ffn, norm_post_ffn=norm_post_ffn)

    def matrices(l, names, gathered):
        W = {}
        for k, a in zip(names, gathered):
            if k in ("w_out", "ffn_w_out"):
                rows_, cols = own[k].shape[1:]
                W[k] = lax.dynamic_update_slice(a[:, 0], own[k][l][None], (s_me, 0, 0)).reshape(4 * rows_, cols)
            else:
                cat = jnp.concatenate(pick(own[k][l], a[:, 0]), axis=-1)
                W[k] = jnp.pad(cat, ((0, 0), (0, PROJ_W - IN_COLS))) if k == "w_in" else cat
        return W

    def small_weights(l):
        return {**{k: a[l] for k, a in conv.items()}, **{k: a[l:l + 1] for k, a in vec.items()}}

    def late_l0(mix_in):
        _, landed = _split_wait(started0, len(rest), plan0, mix_in, "weights_gather_l0_wait")
        return matrices(0, rest, _pass_to_sibling(landed, [half_of[k] for k in rest], "weights_gather_l0_sibling"))

    cos, sgn_sin = _rope_tables(x.shape[1])
    W0 = {**small_weights(0), **matrices(0, ["w_in"], got_in[:1])}
    W0_first = dict(W0, norm_pre_mix=W0["norm_pre_mix"] + started1[-1][0, 0])
    x1, h1_l1, saved0 = _layer_fwd(x[0], W0_first, cos, sgn_sin, 0, late_weights=late_l0,
                                   next_pre_mix=norm_pre_mix[1:2])
    _, landed1 = _split_wait(started1, len(mats), plan1, x1, "weights_gather_l1_wait")
    W1 = {**small_weights(1),
          **matrices(1, mats, _pass_to_sibling(landed1, [half_of[k] for k in mats], "weights_gather_l1_sibling"))}
    x2, _, saved1 = _layer_fwd(x1, W1, cos, sgn_sin, 1, h1=h1_l1)
    loss_local, dy = _loss_head(x2, loss_target[0], "loss_head")
    loss = lax.psum(loss_local, ("x", "y", "c"))

    def shard_major(names, grads_l):
        return [_shard_major(name, grads_l[name]) for name in names]

    def add_siblings(l, names, gs, from_sib):
        return [_add_sibling(g, r, c_arr, tiles[name], f"add_sibling_{name}{l}") for g, r, name in zip(gs, from_sib, names)]

    def scatter_start(l, names, parts, after, tag):
        return _split_copy(parts, [SDS((3,) + p.shape[1:], p.dtype) for p in parts], _scatter_plan, 3, after,
                           f"grads_l{l}{tag}_scatter_start")

    def owner_sums(l, names, sent, after, tag):
        parts, recvd = _split_wait(sent, len(names), _scatter_plan, after, f"grads_l{l}{tag}_scatter_wait")
        return [_add_chips(p, r, sc_arr, tiles[name], f"add_chips_{name}{l}") for p, r, name in zip(parts, recvd, names)]

    (dx1, dh1_l1), grads1, _ = _layer_bwd(dy, saved1, W1, cos, sgn_sin, 1, first_layer=False)
    gs1 = shard_major(mats, grads1)
    swap1 = _split_copy(gs1, [SDS((4, g.shape[1] // 2, g.shape[2]), g.dtype) for g in gs1], _exchange_plan, 1, dx1,
                        "grads_l1_sibling_start")
    ffn = ["ffn_w_in", "ffn_w_out"]
    launched = {}

    def after_ffn_l0(g_ffn, dx_mid):
        gs1_, from_sib1 = _split_wait(swap1, len(mats), _exchange_plan, dx_mid, "grads_l1_sibling_wait")
        launched["l1"] = scatter_start(1, mats, add_siblings(1, mats, gs1_, from_sib1), dx_mid, "")
        gs0 = shard_major(ffn, g_ffn)
        from_sib0 = _exchange_halves(gs0, "grads_l0_ffn_to_sibling")
        launched["l0_ffn"] = scatter_start(0, ffn, add_siblings(0, ffn, gs0, from_sib0), launched["l1"][-1], "_ffn")
        return launched["l0_ffn"][-1][0, 0]

    W0_last = dict(W0, **saved0["late"], norm_post_ffn=W0["norm_post_ffn"] + swap1[-1][0, 0])
    dx, grads0, grads1["norm_pre_mix"] = _layer_bwd(dx1, saved0, W0_last, cos, sgn_sin, 0, after_ffn=after_ffn_l0,
                                                    next_layer=(dh1_l1, norm_pre_mix[1:2]))
    mix = ["w_in", "w_out"]
    gs0 = shard_major(mix, grads0)
    part0 = add_siblings(0, mix, gs0, _exchange_halves(gs0, "grads_l0_mix_to_sibling"))
    sent0 = scatter_start(0, mix, part0, dx, "_mix")
    red = dict(zip([(0, k) for k in ffn], owner_sums(0, ffn, launched["l0_ffn"], sent0[-1], "_ffn")))
    red.update(zip([(1, k) for k in mats], owner_sums(1, mats, launched["l1"], sent0[-1], "")))
    early = [(0, k) for k in ffn] + [(1, k) for k in mats]
    joined = dict(zip(early, _join_halves([red[key] for key in early], "grads_join_early")))
    grads = [grads0, grads1]

    small = {}
    for name in _SMALL:
        per_layer = [grads[l][name] for l in range(2)]
        if name in ("dn_a_log", "dn_dt_bias"):
            per_layer = [p[:, :N_HEADS_D] for p in per_layer]
        small[name] = jnp.stack(per_layer).reshape((2,) + (w[name].shape[1:] if name not in ("dn_conv_w", "ffn_conv_w")
                                                           else per_layer[0].shape))
    flat = jnp.concatenate([small[name].reshape(-1) for name in _SMALL])
    n_rows = -(-flat.shape[0] // 1024) * 8
    summed = _all_reduce_small(jnp.pad(flat, (0, n_rows * 128 - flat.shape[0])).reshape(n_rows, 128),
                               "small_grads_all_reduce").reshape(-1)
    off = 0
    g_out = {}
    for name in _SMALL:
        size = small[name].size
        g_out[name] = summed[off:off + size].reshape(small[name].shape)
        off += size
    g_out["dn_conv_w"] = lax.dynamic_slice_in_dim(g_out["dn_conv_w"], s_me * 384, 384, axis=2)
    g_out["ffn_conv_w"] = lax.dynamic_slice_in_dim(g_out["ffn_conv_w"], s_me * 1408, 1408, axis=2)
    for k in ffn:
        g_out[k] = jnp.stack([joined[(0, k)], joined[(1, k)]])

    deltas, new_m, new_v = {}, {}, {}

    def step(name):
        shape = w[name].shape
        as3 = (lambda a: a) if len(shape) == 3 else (lambda a: a.reshape(shape[0], 1, shape[1]))
        tr = _ADAM_ROWS.get(name, as3(w[name]).shape[1])
        d_, m_, v_ = _adamw(as3(w[name]), as3(g_out[name]), as3(m[name]), as3(v[name]), tr, f"adamw_{name}")
        deltas[name], new_m[name], new_v[name] = d_.reshape(shape), m_.reshape(shape), v_.reshape(shape)

    for name in ffn:
        step(name)
    tiny = [name for name in _WEIGHTS if name not in mats]
    stepped = _adamw_small(*[[d[name] for name in tiny] for d in (w, g_out, m, v)], "adamw_small")
    for out, vals in zip((deltas, new_m, new_v), stepped):
        out.update(zip(tiny, vals))
    done = jnp.reshape(deltas["ffn_w_in"][0, 0, 0] + deltas["ffn_w_out"][0, 0, 0] + deltas["norm_post_ffn"][0, 0], (1,))
    late = _join_halves(owner_sums(0, mix, sent0, done, "_mix"), "grads_join_late")
    for k, a in zip(mix, late):
        g_out[k] = jnp.stack([a, joined[(1, k)]])
        step(k)

    return (loss, dx[None], *[g_out[k] for k in _WEIGHTS], *[deltas[k] for k in _WEIGHTS],
            *[new_m[k] for k in _WEIGHTS], *[new_v[k] for k in _WEIGHTS])
```

```python
import jax
import jax.numpy as jnp
from jax import lax
from jax.experimental import pallas as pl
from jax.experimental.pallas import tpu as pltpu

f32, bf16 = jnp.float32, jnp.bfloat16
SDS = jax.ShapeDtypeStruct
HI = lax.Precision.HIGH
MESH = pl.DeviceIdType.MESH
ANY = pl.BlockSpec(memory_space=pl.ANY)

D_MODEL = 1024
N_HEADS_A, HEAD_DIM = 8, 64
ATTN_W = 512
N_HEADS_D, DK = 4, 128
CHUNK = 64
D_FF = 2816
IN_COLS = 3592
PROJ_W = 3840
BRANCHES = ((1, 16), (4, 4), (16, 1))
EPS = 1e-6
NEG = -1e30
ROW_TILE = 256
VMEM_LIMIT = 56 * 1024 * 1024

ADAM_LR, ADAM_B1, ADAM_B2, ADAM_EPS, ADAM_WD, ADAM_STEP = 0.001, 0.9, 0.999, 1e-08, 0.01, 10


def _params(*sem):
    return pltpu.CompilerParams(dimension_semantics=sem, vmem_limit_bytes=VMEM_LIMIT)


def _mm(a, b, mode, tm, tn, out_dtype, name, column_shards=False):
    if mode == "nn":
        (M, K), N = a.shape, b.shape[1]
        dims = (((1,), (0,)), ((), ()))
        a_spec = pl.BlockSpec((tm, K), lambda i, j: (i, 0))
        b_spec = pl.BlockSpec((K, tn), lambda i, j: (0, j))
    elif mode == "nt":
        (M, K), N = a.shape, b.shape[0]
        dims = (((1,), (1,)), ((), ()))
        a_spec = pl.BlockSpec((tm, K), lambda i, j: (i, 0))
        b_spec = pl.BlockSpec((tn, K), lambda i, j: (j, 0))
    else:
        (K, M), N = a.shape, b.shape[1]
        dims = (((0,), (0,)), ((), ()))
        a_spec = pl.BlockSpec((K, tm), lambda i, j: (0, i))
        b_spec = pl.BlockSpec((K, tn), lambda i, j: (0, j))
    assert M % tm == 0 and N % tn == 0, (name, M, N, tm, tn)

    def body(a_ref, b_ref, o_ref):
        o_ref[...] = lax.dot_general(a_ref[...].astype(bf16), b_ref[...].astype(bf16), dims,
                                     preferred_element_type=f32).astype(o_ref.dtype)

    if column_shards:
        out_spec, out_shape = pl.BlockSpec((None, tm, tn), lambda i, j: (j, i, 0)), SDS((N // tn, M, tn), out_dtype)
    else:
        out_spec, out_shape = pl.BlockSpec((tm, tn), lambda i, j: (i, j)), SDS((M, N), out_dtype)
    return pl.pallas_call(body, grid=(M // tm, N // tn), in_specs=[a_spec, b_spec], out_specs=out_spec,
                          out_shape=out_shape, name=name, compiler_params=_params("parallel", "arbitrary"))(a, b)


def _row_spec(r, tm):
    if isinstance(r, tuple):
        arr, width, cb = r
        return arr, pl.BlockSpec((tm, width), lambda i, j, cb=cb: (i, cb + j))
    return r, pl.BlockSpec((tm, r.shape[1]), lambda i, j: (i, j))


def _full_spec(p):
    return pl.BlockSpec(p.shape, lambda i, j: (0,) * p.ndim)


def _rows(fn, rows, params, outs, name, tm=ROW_TILE, ncol=1):
    arrs, specs = zip(*[_row_spec(r, tm) for r in rows])
    S = arrs[0].shape[0]
    nr, npar = len(rows), len(params)

    def body(*refs):
        vals = fn(*[r[...].astype(f32) for r in refs[:nr]], *[p[...] for p in refs[nr:nr + npar]])
        for o_ref, v in zip(refs[nr + npar:], vals):
            o_ref[...] = v.astype(o_ref.dtype)

    return pl.pallas_call(
        body, grid=(S // tm, ncol), in_specs=list(specs) + [_full_spec(p) for p in params],
        out_specs=[pl.BlockSpec((tm, w), lambda i, j: (i, j)) for w, _ in outs],
        out_shape=[SDS((S, w * ncol), dt) for w, dt in outs], name=name,
        compiler_params=_params("parallel", "parallel"))(*arrs, *params)


def _rows_vjp(fn, rows, params, cts, wrt_rows, wrt_params, name, adds=None, tm=ROW_TILE, ncol=1, row_dtype=f32):
    adds = adds or {}
    arrs, specs = zip(*[_row_spec(r, tm) for r in rows])
    carrs, cspecs = zip(*[_row_spec(c, tm) for c in cts])
    add_keys = sorted(adds)
    aarrs = [adds[k] for k in add_keys]
    S = arrs[0].shape[0]
    nr, npar, nc, na = len(rows), len(params), len(cts), len(aarrs)
    widths = [specs[k].block_shape[1] for k in wrt_rows]
    row_dtypes = row_dtype if isinstance(row_dtype, (list, tuple)) else [row_dtype] * len(wrt_rows)

    def body(*refs):
        first = jnp.logical_and(pl.program_id(0) == 0, pl.program_id(1) == 0)
        rv = [r[...].astype(f32) for r in refs[:nr]]
        pv = [p[...] for p in refs[nr:nr + npar]]
        cv = tuple(c[...].astype(f32) for c in refs[nr + npar:nr + npar + nc])
        av = dict(zip(add_keys, refs[nr + npar + nc:nr + npar + nc + na]))
        o = refs[nr + npar + nc + na:]
        _, vjp = jax.vjp(fn, *rv, *pv)
        g = vjp(cv)
        for n, k in enumerate(wrt_rows):
            val = g[k]
            if k in av:
                val = val + av[k][...]
            o[n][...] = val.astype(o[n].dtype)
        for n, k in enumerate(wrt_params):
            ref = o[len(wrt_rows) + n]

            @pl.when(first)
            def _(ref=ref):
                ref[...] = jnp.zeros_like(ref)

            ref[...] += g[nr + k]

    res = pl.pallas_call(
        body, grid=(S // tm, ncol),
        in_specs=list(specs) + [_full_spec(p) for p in params] + list(cspecs)
        + [pl.BlockSpec((tm, a.shape[1] // ncol), lambda i, j: (i, j)) for a in aarrs],
        out_specs=[pl.BlockSpec((tm, w), lambda i, j: (i, j)) for w in widths] + [_full_spec(params[k]) for k in wrt_params],
        out_shape=[SDS((S, w * ncol), dt) for w, dt in zip(widths, row_dtypes)]
        + [SDS(params[k].shape, f32) for k in wrt_params],
        name=name, compiler_params=_params("arbitrary", "arbitrary"))(*arrs, *params, *carrs, *aarrs)
    return res[:len(wrt_rows)], res[len(wrt_rows):]


def _rms(x, w):
    return x * lax.rsqrt(jnp.mean(x * x, axis=-1, keepdims=True) + EPS) * w


def _rms_fn(x, w):
    return (_rms(x, w),)


def _res_rms_fn(f, res, w):
    return (res + _rms(f, w),)


def _res_rms_pre_fn(f, res, w_post, w_pre):
    x1 = res + _rms(f, w_post)
    return x1, _rms(x1, w_pre)


def _swap_halves(x):
    lane = lax.broadcasted_iota(jnp.int32, x.shape, 1)
    first = (lane % HEAD_DIM) < (HEAD_DIM // 2)
    n = x.shape[1]
    return jnp.where(first, pltpu.roll(x, n - HEAD_DIM // 2, 1), pltpu.roll(x, HEAD_DIM // 2, 1))


def _rope_fwd_fn(q, k, cos, sgn_sin):
    scale = HEAD_DIM ** -0.5
    return ((q * cos + _swap_halves(q) * sgn_sin) * scale, k * cos + _swap_halves(k) * sgn_sin)


def _rope_bwd_fn(dq, dk, cos, sgn_sin):
    dq = dq * (HEAD_DIM ** -0.5)
    return (dq * cos + _swap_halves(dq * sgn_sin), dk * cos + _swap_halves(dk * sgn_sin))


def _nt(a, b):
    return lax.dot_general(a, b, (((1,), (1,)), ((), ())), preferred_element_type=f32)


def _tn(a, b):
    return lax.dot_general(a, b, (((0,), (0,)), ((), ())), preferred_element_type=f32)


def _band_rows(j, d, nb):
    r, i = j // nb, j % nb
    if d == 1:
        cur = pl.ds(pl.multiple_of(i * 128, 128), 128)
        prev = pl.ds(pl.multiple_of(jnp.maximum(i - 1, 0) * 128, 128), 128)
    else:
        cur = pl.ds(i * (128 * d) + r, 128, stride=d)
        prev = pl.ds(jnp.maximum(i - 1, 0) * (128 * d) + r, 128, stride=d)
    return cur, prev, (i == 0).astype(jnp.int32)


def _band_bias(bias_ref):
    a = lax.broadcasted_iota(jnp.int32, (256, 256), 0) % 128
    c = lax.broadcasted_iota(jnp.int32, (256, 256), 1)
    own = jnp.logical_and(c < 128, c <= a)
    before = jnp.logical_and(c >= 128, c - 128 >= a)
    bias_ref[0] = jnp.where(jnp.logical_or(own, before), 0.0, NEG)
    bias_ref[1] = jnp.where(own, 0.0, NEG)


def _stack_heads(x, head_a):
    return jnp.concatenate([jnp.where(head_a, x, 0.0), jnp.where(head_a, 0.0, x)], axis=0)


def _unstack_heads(x2, head_a):
    return jnp.where(head_a, x2[:128], x2[128:])


def _pair_at(S, first_col):
    return pl.BlockSpec((S, 128), lambda h: (0, first_col // 128 + h))


def _attn_fwd(proj, cos, sgn_sin, name):
    S = proj.shape[0]
    nblk = S // 128

    def body(qp_ref, kp_ref, v_ref, cos_ref, sin_ref, out_ref, lse_ref, bias_ref, q_ref, k_ref, *scr):
        head_a = lax.broadcasted_iota(jnp.int32, (1, 128), 1) < HEAD_DIM
        _band_bias(bias_ref)
        q_ref[...], k_ref[...] = _rope_fwd_fn(qp_ref[...], kp_ref[...], cos_ref[...], sin_ref[...])
        for b, (d, nb) in enumerate(BRANCHES):
            ob_ref, lb_ref = scr[2 * b], scr[2 * b + 1]

            def blk(j, carry, d=d, nb=nb, ob_ref=ob_ref, lb_ref=lb_ref):
                cur, prev, first = _band_rows(j, d, nb)
                q2 = _stack_heads(q_ref[cur, :], head_a).astype(bf16)
                if nb == 1:
                    k2, v2, bias = k_ref[cur, :].astype(bf16), v_ref[cur, :].astype(bf16), bias_ref[1][:, :128]
                else:
                    k2 = jnp.concatenate([k_ref[cur, :], k_ref[prev, :]], axis=0).astype(bf16)
                    v2 = jnp.concatenate([v_ref[cur, :], v_ref[prev, :]], axis=0).astype(bf16)
                    bias = bias_ref[first]
                s = _nt(q2, k2) + bias
                mx = jnp.max(s, axis=1, keepdims=True)
                p = jnp.exp(s - mx)
                l = jnp.sum(p, axis=1, keepdims=True)
                o = jnp.dot(p.astype(bf16), v2, preferred_element_type=f32) / l
                ob_ref[cur, :] = _unstack_heads(o, head_a)
                lb_ref[cur, :] = _unstack_heads(jnp.broadcast_to(mx + jnp.log(l), (256, 128)), head_a)
                return carry

            lax.fori_loop(0, nblk, blk, 0, unroll=16)
        l0, l1, l2 = scr[1][...], scr[3][...], scr[5][...]
        mx = jnp.maximum(jnp.maximum(l0, l1), l2)
        e0, e1, e2 = jnp.exp(l0 - mx), jnp.exp(l1 - mx), jnp.exp(l2 - mx)
        den = e0 + e1 + e2
        out_ref[...] = ((e0 * scr[0][...] + e1 * scr[2][...] + e2 * scr[4][...]) / den).astype(out_ref.dtype)
        lse_ref[...] = mx + jnp.log(den)

    pair = pl.BlockSpec((S, 128), lambda h: (0, h))
    return pl.pallas_call(
        body, grid=(N_HEADS_A // 2,),
        in_specs=[pair, _pair_at(S, ATTN_W), _pair_at(S, 2 * ATTN_W), pair, pair], out_specs=[pair, pair],
        out_shape=[SDS((S, ATTN_W), bf16), SDS((S, ATTN_W), f32)],
        scratch_shapes=[pltpu.VMEM((2, 256, 256), f32)] + [pltpu.VMEM((S, 128), f32)] * 8,
        name=name, compiler_params=_params("parallel"))(proj, proj, proj, cos, sgn_sin)


def _attn_bwd(proj, cos, sgn_sin, dmix_in, out, lse, name):
    S = proj.shape[0]
    nblk = S // 128

    def body(qp_ref, kp_ref, v_ref, cos_ref, sin_ref, do_ref, out_ref, lse_ref, dq_ref, dk_ref, dv_ref,
             bias_ref, t_ref, q_ref, k_ref):
        head_a = lax.broadcasted_iota(jnp.int32, (1, 128), 1) < HEAD_DIM
        _band_bias(bias_ref)
        q_ref[...], k_ref[...] = _rope_fwd_fn(qp_ref[...], kp_ref[...], cos_ref[...], sin_ref[...])
        x = do_ref[...] * out_ref[...].astype(f32)
        t_ref[...] = jnp.where(head_a, jnp.sum(jnp.where(head_a, x, 0.0), axis=1, keepdims=True),
                               jnp.sum(jnp.where(head_a, 0.0, x), axis=1, keepdims=True))
        dq_ref[...] = jnp.zeros_like(dq_ref)
        dk_ref[...] = jnp.zeros_like(dk_ref)
        dv_ref[...] = jnp.zeros_like(dv_ref)
        for d, nb in BRANCHES:
            def blk(j, carry, d=d, nb=nb):
                cur, prev, first = _band_rows(j, d, nb)
                q2 = _stack_heads(q_ref[cur, :], head_a).astype(bf16)
                do2 = _stack_heads(do_ref[cur, :], head_a).astype(bf16)
                t, lse_b = t_ref[cur, :], lse_ref[cur, :]
                t2 = jnp.concatenate([t[:, :1], t[:, HEAD_DIM:HEAD_DIM + 1]], axis=0)
                lse2 = jnp.concatenate([lse_b[:, :1], lse_b[:, HEAD_DIM:HEAD_DIM + 1]], axis=0)
                if nb == 1:
                    k2, v2, bias = k_ref[cur, :].astype(bf16), v_ref[cur, :].astype(bf16), bias_ref[1][:, :128]
                else:
                    k2 = jnp.concatenate([k_ref[cur, :], k_ref[prev, :]], axis=0).astype(bf16)
                    v2 = jnp.concatenate([v_ref[cur, :], v_ref[prev, :]], axis=0).astype(bf16)
                    bias = bias_ref[first]
                p = jnp.exp(_nt(q2, k2) + bias - lse2)
                ds = (p * (_nt(do2, v2) - t2)).astype(bf16)
                dq_ref[cur, :] += _unstack_heads(jnp.dot(ds, k2, preferred_element_type=f32), head_a)
                dk2, dv2 = _tn(ds, q2), _tn(p.astype(bf16), do2)
                dk_ref[cur, :] += dk2[:128]
                dv_ref[cur, :] += dv2[:128]
                if nb != 1:
                    dk_ref[prev, :] += dk2[128:]
                    dv_ref[prev, :] += dv2[128:]
                return carry

            lax.fori_loop(0, nblk, blk, 0, unroll=16)
        dq_ref[...], dk_ref[...] = _rope_bwd_fn(dq_ref[...], dk_ref[...], cos_ref[...], sin_ref[...])

    pair = pl.BlockSpec((S, 128), lambda h: (0, h))
    return pl.pallas_call(
        body, grid=(N_HEADS_A // 2,),
        in_specs=[pair, _pair_at(S, ATTN_W), _pair_at(S, 2 * ATTN_W), pair, pair, pair, pair, pair],
        out_specs=[pair] * 3, out_shape=[SDS((S, ATTN_W), f32)] * 3,
        scratch_shapes=[pltpu.VMEM((2, 256, 256), f32)] + [pltpu.VMEM((S, 128), f32)] * 3,
        name=name, compiler_params=_params("parallel"))(proj, proj, proj, cos, sgn_sin, dmix_in, out, lse)


def _conv_val(x, w, K, rows):
    acc = x * w[K - 1:K, :]
    for s in range(1, K):
        acc = acc + jnp.where(rows >= s, pltpu.roll(x, s, 0), 0.0) * w[K - 1 - s:K - s, :]
    return acc


def _colconv_fwd(xs, ws, bs, K, fn, nblk, tc, outs, name):
    S = xs[0][0].shape[0]
    n = len(xs)
    has_b = bs is not None

    def body(*refs):
        rows = lax.broadcasted_iota(jnp.int32, (S, tc), 0)
        cs = []
        for k in range(n):
            c = _conv_val(refs[k][...].astype(f32), refs[n + k][...], K, rows)
            if has_b:
                c = c + refs[2 * n + k][...]
            cs.append(c)
        for o_ref, val in zip(refs[(3 if has_b else 2) * n:], fn(*cs)):
            o_ref[...] = val.astype(o_ref.dtype)

    def cspec(rows_, cb0):
        return pl.BlockSpec((rows_, tc), lambda j, cb0=cb0: (0, cb0 + j))

    in_specs = [cspec(S, cb) for _, cb in xs] + [cspec(K, cb) for _, cb in ws]
    args = [a for a, _ in xs] + [a for a, _ in ws]
    if has_b:
        in_specs += [cspec(1, cb) for _, cb in bs]
        args += [a for a, _ in bs]
    return pl.pallas_call(
        body, grid=(nblk,), in_specs=in_specs, out_specs=[cspec(S, 0) for _ in outs],
        out_shape=[SDS((S, nblk * tc), dt) for dt in outs], name=name, compiler_params=_params("parallel"))(*args)


def _colconv_bwd(xs, ws, bs, K, fn, douts, nblk, tc, name, dx_dtype=f32):
    S = xs[0][0].shape[0]
    n, nd = len(xs), len(douts)
    has_b = bs is not None
    nin = (3 if has_b else 2) * n

    def body(*refs):
        rows = lax.broadcasted_iota(jnp.int32, (S, tc), 0)
        x = [refs[k][...].astype(f32) for k in range(n)]
        w = [refs[n + k][...] for k in range(n)]
        cs = []
        for k in range(n):
            c = _conv_val(x[k], w[k], K, rows)
            if has_b:
                c = c + refs[2 * n + k][...]
            cs.append(c)
        _, vjp = jax.vjp(fn, *cs)
        dcs = vjp(tuple(r[...].astype(f32) for r in refs[nin:nin + nd]))
        o = refs[nin + nd:]
        for k in range(n):
            dc = dcs[k]
            dx = dc * w[k][K - 1:K, :]
            o[n + k][K - 1:K, :] = jnp.sum(dc * x[k], axis=0, keepdims=True)
            for s in range(1, K):
                dx = dx + jnp.where(rows < S - s, pltpu.roll(dc, S - s, 0), 0.0) * w[k][K - 1 - s:K - s, :]
                xsh = jnp.where(rows >= s, pltpu.roll(x[k], s, 0), 0.0)
                o[n + k][K - 1 - s:K - s, :] = jnp.sum(dc * xsh, axis=0, keepdims=True)
            o[k][...] = dx.astype(o[k].dtype)
            if has_b:
                o[2 * n + k][...] = jnp.sum(dc, axis=0, keepdims=True)

    def cspec(rows_, cb0):
        return pl.BlockSpec((rows_, tc), lambda j, cb0=cb0: (0, cb0 + j))

    in_specs = [cspec(S, cb) for _, cb in xs] + [cspec(K, cb) for _, cb in ws]
    args = [a for a, _ in xs] + [a for a, _ in ws]
    if has_b:
        in_specs += [cspec(1, cb) for _, cb in bs]
        args += [a for a, _ in bs]
    in_specs += [cspec(S, 0) for _ in douts]
    args += list(douts)
    W = nblk * tc
    out_specs = [cspec(S, 0)] * n + [cspec(K, 0)] * n + ([cspec(1, 0)] * n if has_b else [])
    out_shape = [SDS((S, W), dx_dtype)] * n + [SDS((K, W), f32)] * n + ([SDS((1, W), f32)] * n if has_b else [])
    res = pl.pallas_call(body, grid=(nblk,), in_specs=in_specs, out_specs=out_specs, out_shape=out_shape,
                         name=name, compiler_params=_params("parallel"))(*args)
    return res[:n], res[n:2 * n], res[2 * n:]


def _silu_fn(c):
    return (c * jax.nn.sigmoid(c),)


_GELU_C, _GELU_A = 0.7978845608028654, 0.044715


@jax.custom_vjp
def _geglu(gate, up):
    return 0.5 * gate * (1.0 + jnp.tanh(_GELU_C * (gate + _GELU_A * gate * gate * gate))) * up


def _geglu_vjp_fwd(gate, up):
    return _geglu(gate, up), (gate, up)


def _geglu_vjp_bwd(res, d):
    gate, up = res
    g2 = gate * gate
    t = jnp.tanh(_GELU_C * gate * (1.0 + _GELU_A * g2))
    h = 0.5 * (1.0 + t)
    dgelu = h + (0.5 * _GELU_C) * gate * (1.0 - t * t) * (1.0 + (3.0 * _GELU_A) * g2)
    return d * up * dgelu, d * (gate * h)


_geglu.defvjp(_geglu_vjp_fwd, _geglu_vjp_bwd)


def _geglu_fn(gate, up):
    return (_geglu(gate, up),)


def _softplus(x):
    u = jnp.exp(jnp.minimum(x, 20.0))
    small = u * (1.0 - 0.5 * u)
    return jnp.where(x > 20.0, x, jnp.where(u < 1e-4, small, jnp.log(1.0 + u)))


def _bmm(a, b, precision=None):
    return lax.dot_general(a, b, (((2,), (1,)), ((0,), (0,))), precision=precision, preferred_element_type=f32)


def _bnt(a, b, precision=None):
    return lax.dot_general(a, b, (((2,), (2,)), ((0,), (0,))), precision=precision, preferred_element_type=f32)


def _btn(a, b, precision=None):
    return lax.dot_general(a, b, (((1,), (1,)), ((0,), (0,))), precision=precision, preferred_element_type=f32)


@jax.custom_vjp
def _unit_lower_inverse(A):
    n = A.shape[-1]
    eye = (lax.broadcasted_iota(jnp.int32, (1, n, n), 1) == lax.broadcasted_iota(jnp.int32, (1, n, n), 2)).astype(f32)
    P = -A
    T = eye + P
    for _ in range(5):
        P = _bmm(P, P, HI)
        T = T + _bmm(T, P, HI)
    return T


def _unit_lower_inverse_fwd(A):
    T = _unit_lower_inverse(A)
    return T, T


def _unit_lower_inverse_bwd(T, dT):
    return (-_btn(T, _bnt(dT, T, HI), HI),)


_unit_lower_inverse.defvjp(_unit_lower_inverse_fwd, _unit_lower_inverse_bwd)


def _dn_prep_fn(q, k, v, ba, alog, dtb, h):
    G, C = q.shape[0], CHUNK
    lane = lax.broadcasted_iota(jnp.int32, (1, 1, 128), 2)

    def sel(arr, idx):
        return jnp.sum(jnp.where(lane == idx, arr, 0.0), axis=-1, keepdims=True)

    beta = jax.nn.sigmoid(sel(ba, h))
    g = -jnp.exp(sel(alog[None], h)) * _softplus(sel(ba, N_HEADS_D + h) + sel(dtb[None], h))
    qn = q * lax.rsqrt(jnp.sum(q * q, axis=-1, keepdims=True) + EPS) * (DK ** -0.5)
    kn = k * lax.rsqrt(jnp.sum(k * k, axis=-1, keepdims=True) + EPS)
    ii = lax.broadcasted_iota(jnp.int32, (1, C, C), 1)
    jj = lax.broadcasted_iota(jnp.int32, (1, C, C), 2)
    tril, strict = ii >= jj, ii > jj
    gsq = jnp.broadcast_to(g, (G, C, C))
    gcol = _bmm(jnp.broadcast_to(tril.astype(f32), (G, C, C)), gsq, HI)
    grow = _bmm(jnp.ones((G, C, C), f32), jnp.where(ii <= jj, gsq, 0.0), HI)
    decay = jnp.exp(jnp.where(tril, gcol - grow, NEG))
    gc = gcol[:, :, :1]
    glast = gcol[:, C - 1:C, :1]
    kb = kn * beta
    A = jnp.where(strict, _bnt(kb.astype(bf16), kn.astype(bf16)) * decay, 0.0)
    T = _unit_lower_inverse(A).astype(bf16)
    u = _bmm(T, (v * beta).astype(bf16))
    w = _bmm(T, (kb * jnp.exp(gc)).astype(bf16))
    qk = _bnt(qn.astype(bf16), kn.astype(bf16)) * decay
    qd = qn * jnp.exp(gc)
    kd = kn * jnp.exp(glast - gc)
    return u, w, qk, qd, kd, jnp.broadcast_to(jnp.exp(glast), (G, C, DK))


def _dn_scan_fn(u, w, qk, qd, kd, eg, St):
    b = lambda a: a.astype(bf16)
    vnew = u - _bmm(b(w), b(St))
    o = _bmm(b(qd), b(St)) + _bmm(b(qk), b(vnew))
    return o, St * eg[:, :1, :] + _btn(b(kd), b(vnew))


def _dn_post_fn(o, z, nw):
    return (_rms(o, nw) * (z * jax.nn.sigmoid(z)),)


DN_GROUP = 8


def _dn_prep_specs(S, rows):
    def col(first):
        return pl.BlockSpec((rows, DK), lambda i, h, first=first: (i, first // DK + h))

    par = pl.BlockSpec((1, 128), lambda i, h: (0, 0))
    return [col(0), col(N_HEADS_D * DK), col(2 * N_HEADS_D * DK),
            pl.BlockSpec((rows, 128), lambda i, h: (i, 3584 // 128)), par, par]


def _dn_prep(qkv, proj, alog, dtb, name):
    S = qkv.shape[0]
    G = DN_GROUP
    rows = G * CHUNK

    def body(q_ref, k_ref, v_ref, ba_ref, al_ref, dt_ref, u_ref, w_ref, qk_ref, qd_ref, kd_ref, eg_ref):
        h = pl.program_id(1)
        r3 = lambda ref: ref[...].reshape(G, CHUNK, 128)
        u, w, qk, qd, kd, eg = _dn_prep_fn(r3(q_ref), r3(k_ref), r3(v_ref), r3(ba_ref), al_ref[...], dt_ref[...], h)
        for ref, val in ((u_ref, u), (w_ref, w), (qd_ref, qd), (kd_ref, kd), (eg_ref, eg)):
            ref[...] = val.reshape(rows, DK)
        qk_ref[:, :CHUNK] = qk.reshape(rows, CHUNK)
        qk_ref[:, CHUNK:] = jnp.zeros((rows, DK - CHUNK), f32)

    out = pl.BlockSpec((rows, DK), lambda i, h: (i, h))
    return pl.pallas_call(
        body, grid=(S // rows, N_HEADS_D), in_specs=_dn_prep_specs(S, rows), out_specs=[out] * 6,
        out_shape=[SDS((S, N_HEADS_D * DK), f32)] * 6, name=name,
        compiler_params=_params("parallel", "parallel"))(qkv, qkv, qkv, proj, alog, dtb)


def _dn_prep_bwd(qkv, proj, alog, dtb, cts, name):
    S = qkv.shape[0]
    G = DN_GROUP
    rows = G * CHUNK

    def body(q_ref, k_ref, v_ref, ba_ref, al_ref, dt_ref, du_ref, dw_ref, dqk_ref, dqd_ref, dkd_ref, deg_ref,
             dq_ref, dk_ref, dv_ref, dba_ref, dal_ref, ddt_ref):
        i, h = pl.program_id(0), pl.program_id(1)
        r3 = lambda ref: ref[...].reshape(G, CHUNK, 128)
        _, vjp = jax.vjp(lambda q, k, v, ba, al, dt: _dn_prep_fn(q, k, v, ba, al, dt, h),
                         r3(q_ref), r3(k_ref), r3(v_ref), r3(ba_ref), al_ref[...], dt_ref[...])
        dqk = dqk_ref[:, :CHUNK].reshape(G, CHUNK, CHUNK)
        dq, dk, dv, dba, dal, ddt = vjp((r3(du_ref), r3(dw_ref), dqk, r3(dqd_ref), r3(dkd_ref), r3(deg_ref)))
        dq_ref[...] = dq.reshape(rows, DK)
        dk_ref[...] = dk.reshape(rows, DK)
        dv_ref[...] = dv.reshape(rows, DK)

        @pl.when(h == 0)
        def _():
            dba_ref[...] = jnp.zeros_like(dba_ref)

        @pl.when(jnp.logical_and(i == 0, h == 0))
        def _():
            dal_ref[...] = jnp.zeros_like(dal_ref)
            ddt_ref[...] = jnp.zeros_like(ddt_ref)

        dba_ref[...] += dba.reshape(rows, 128)
        dal_ref[...] += dal
        ddt_ref[...] += ddt

    hcol = pl.BlockSpec((rows, DK), lambda i, h: (i, h))
    par = pl.BlockSpec((1, 128), lambda i, h: (0, 0))
    W = N_HEADS_D * DK
    return pl.pallas_call(
        body, grid=(S // rows, N_HEADS_D), in_specs=_dn_prep_specs(S, rows) + [hcol] * 6,
        out_specs=[hcol] * 3 + [pl.BlockSpec((rows, 128), lambda i, h: (i, 0)), par, par],
        out_shape=[SDS((S, W), f32)] * 3 + [SDS((S, 128), f32), SDS((1, 128), f32), SDS((1, 128), f32)], name=name,
        compiler_params=_params("arbitrary", "arbitrary"))(qkv, qkv, qkv, proj, alog, dtb, *cts)


def _heads(x):
    return jnp.stack([x[:, DK * h:DK * (h + 1)] for h in range(N_HEADS_D)])


SCAN_CHUNKS = 8


def _dn_scan(pre, name):
    S = pre[0].shape[0]
    NCH = S // CHUNK
    rows = SCAN_CHUNKS * CHUNK

    def body(u_ref, w_ref, qk_ref, qd_ref, kd_ref, eg_ref, o_ref, st_ref, s_ref):
        @pl.when(pl.program_id(0) == 0)
        def _():
            s_ref[...] = jnp.zeros_like(s_ref)

        St = s_ref[...]
        for k in range(SCAN_CHUNKS):
            r = slice(k * CHUNK, (k + 1) * CHUNK)
            st_ref[k] = St
            o, St = _dn_scan_fn(_heads(u_ref[r, :]), _heads(w_ref[r, :]), _heads(qk_ref[r, :])[:, :, :CHUNK],
                                _heads(qd_ref[r, :]), _heads(kd_ref[r, :]), _heads(eg_ref[r, :]), St)
            for h in range(N_HEADS_D):
                o_ref[r, DK * h:DK * (h + 1)] = o[h]
        s_ref[...] = St

    blk = pl.BlockSpec((rows, N_HEADS_D * DK), lambda n: (n, 0))
    return pl.pallas_call(
        body, grid=(S // rows,), in_specs=[blk] * 6,
        out_specs=[blk, pl.BlockSpec((SCAN_CHUNKS, N_HEADS_D, DK, DK), lambda n: (n, 0, 0, 0))],
        out_shape=[SDS((S, N_HEADS_D * DK), f32), SDS((NCH, N_HEADS_D, DK, DK), f32)],
        scratch_shapes=[pltpu.VMEM((N_HEADS_D, DK, DK), f32)], name=name, compiler_params=_params("arbitrary"))(*pre)


def _dn_scan_bwd(pre, states, do, name):
    S = do.shape[0]
    rows = SCAN_CHUNKS * CHUNK
    steps = S // rows

    def body(u_ref, w_ref, qk_ref, qd_ref, kd_ref, eg_ref, st_ref, do_ref,
             du_ref, dw_ref, dqk_ref, dqd_ref, dkd_ref, deg_ref, ds_ref):
        @pl.when(pl.program_id(0) == 0)
        def _():
            ds_ref[...] = jnp.zeros_like(ds_ref)

        dS = ds_ref[...]
        for k in reversed(range(SCAN_CHUNKS)):
            r = slice(k * CHUNK, (k + 1) * CHUNK)
            _, vjp = jax.vjp(_dn_scan_fn, _heads(u_ref[r, :]), _heads(w_ref[r, :]), _heads(qk_ref[r, :])[:, :, :CHUNK],
                             _heads(qd_ref[r, :]), _heads(kd_ref[r, :]), _heads(eg_ref[r, :]), st_ref[k])
            du, dw, dqk, dqd, dkd, deg, dS = vjp((_heads(do_ref[r, :]), dS))
            for h in range(N_HEADS_D):
                c = slice(DK * h, DK * (h + 1))
                for ref, val in ((du_ref, du), (dw_ref, dw), (dqd_ref, dqd), (dkd_ref, dkd), (deg_ref, deg)):
                    ref[r, c] = val[h]
                dqk_ref[r, DK * h:DK * h + CHUNK] = dqk[h]
                dqk_ref[r, DK * h + CHUNK:DK * (h + 1)] = jnp.zeros((CHUNK, DK - CHUNK), f32)
        ds_ref[...] = dS

    blk = pl.BlockSpec((rows, N_HEADS_D * DK), lambda n: (steps - 1 - n, 0))
    return pl.pallas_call(
        body, grid=(steps,),
        in_specs=[blk] * 6 + [pl.BlockSpec((SCAN_CHUNKS, N_HEADS_D, DK, DK), lambda n: (steps - 1 - n, 0, 0, 0)), blk],
        out_specs=[blk] * 6, out_shape=[SDS((S, N_HEADS_D * DK), f32)] * 6,
        scratch_shapes=[pltpu.VMEM((N_HEADS_D, DK, DK), f32)], name=name,
        compiler_params=_params("arbitrary"))(*pre, states, do)


def _loss_head(y, t, name):
    S, D = y.shape
    tm = ROW_TILE

    def body(y_ref, t_ref, dy_ref, l_ref):
        i = pl.program_id(0)
        d = y_ref[...] - t_ref[...]
        dy_ref[...] = d * (1.0 / D)
        part = jnp.sum(jnp.sum(d * d, axis=1, keepdims=True), axis=0, keepdims=True) * (0.5 / D)

        @pl.when(i == 0)
        def _():
            l_ref[...] = jnp.zeros_like(l_ref)

        l_ref[...] += jnp.broadcast_to(part, l_ref.shape)

    spec = pl.BlockSpec((tm, D), lambda i: (i, 0))
    dy, l = pl.pallas_call(body, grid=(S // tm,), in_specs=[spec, spec],
                           out_specs=[spec, pl.BlockSpec((1, 128), lambda i: (0, 0))],
                           out_shape=[SDS((S, D), f32), SDS((1, 128), f32)], name=name,
                           compiler_params=_params("arbitrary"))(y, t)
    return l[0, 0], dy


def _adamw_refs(w_ref, g_ref, m_ref, v_ref, d_ref, mo_ref, vo_ref):
    gv = g_ref[...]
    m2 = ADAM_B1 * m_ref[...] + (1.0 - ADAM_B1) * gv
    v2 = ADAM_B2 * v_ref[...] + (1.0 - ADAM_B2) * (gv * gv)
    m_hat = m2 / (1.0 - ADAM_B1 ** ADAM_STEP)
    v_hat = v2 / (1.0 - ADAM_B2 ** ADAM_STEP)
    d_ref[...] = -ADAM_LR * (m_hat / (jnp.sqrt(v_hat) + ADAM_EPS) + ADAM_WD * w_ref[...])
    mo_ref[...] = m2
    vo_ref[...] = v2


def _adamw_small(ws, gs, ms, vs, name):
    n = len(ws)

    def body(*refs):
        for i in range(n):
            _adamw_refs(*[refs[k * n + i] for k in range(7)])

    res = pl.pallas_call(body, out_shape=[SDS(a.shape, f32) for a in ws] * 3, name=name)(*ws, *gs, *ms, *vs)
    return res[:n], res[n:2 * n], res[2 * n:]


def _adamw(w, g, m, v, tr, name):
    L, R, C = w.shape
    assert R % tr == 0

    def body(*refs):
        _adamw_refs(*refs)

    spec = pl.BlockSpec((1, tr, C), lambda l, i: (l, i, 0))
    return pl.pallas_call(body, grid=(L, R // tr), in_specs=[spec] * 4, out_specs=[spec] * 3,
                          out_shape=[SDS((L, R, C), f32)] * 3, name=name,
                          compiler_params=_params("parallel", "parallel"))(w, g, m, v)


def _rope_tables(S):
    inv = 1.0 / (10000.0 ** (jnp.arange(0, HEAD_DIM, 2, dtype=f32) / HEAD_DIM))
    ang = jnp.arange(S, dtype=f32)[:, None] * inv[None, :]
    cos, sin = jnp.cos(ang), jnp.sin(ang)
    return (jnp.tile(jnp.concatenate([cos, cos], axis=1), (1, N_HEADS_A)),
            jnp.tile(jnp.concatenate([-sin, sin], axis=1), (1, N_HEADS_A)))


def _layer_fwd(x, W, cos, sgn_sin, l, late_weights=None, h1=None, next_pre_mix=None):
    n = f"l{l}_"
    if h1 is None:
        (h1,) = _rows(_rms_fn, [x], [W["norm_pre_mix"]], [(D_MODEL, bf16)], n + "pre_mix_norm")
    proj = _mm(h1, W["w_in"], "nn", 1024, 768, f32, n + "in_proj")
    attn_out, lse = _attn_fwd(proj, cos, sgn_sin, n + "attn_fwd")
    (qkv,) = _colconv_fwd([(proj, 3)], [(W["dn_conv_w"], 0)], None, 4, _silu_fn, 3, 512, [f32], n + "dn_conv")
    dn_pre = _dn_prep(qkv, proj, W["dn_a_log"], W["dn_dt_bias"], n + "dn_prep")
    dn_o, dn_states = _dn_scan(dn_pre, n + "dn_scan")
    (dn_out,) = _rows(_dn_post_fn, [(dn_o, DK, 0), (proj, DK, 3072 // DK)], [W["dn_norm_w"]], [(DK, bf16)], n + "dn_post",
                      ncol=N_HEADS_D, tm=4 * ROW_TILE)
    mix_in = jnp.concatenate([attn_out, dn_out], axis=1)
    late = late_weights(mix_in) if late_weights is not None else {}
    W = {**W, **late}
    mix = _mm(mix_in, W["w_out"], "nn", 512, 512, f32, n + "out_proj")
    x1, h2 = _rows(_res_rms_pre_fn, [mix, x], [W["norm_post_mix"], W["norm_pre_ffn"]],
                   [(D_MODEL, f32), (D_MODEL, bf16)], n + "post_mix_pre_ffn_norm")
    u0 = _mm(h2, W["ffn_w_in"], "nn", 1024, D_FF // 2, bf16, n + "ffn_in")
    nb_ff = D_FF // 256
    (act,) = _colconv_fwd([(u0, 0), (u0, nb_ff)], [(W["ffn_conv_w"], 0), (W["ffn_conv_w"], nb_ff)],
                          [(W["ffn_conv_b"], 0), (W["ffn_conv_b"], nb_ff)], 3, _geglu_fn, nb_ff, 256, [bf16],
                          n + "ffn_conv_glu")
    f = _mm(act, W["ffn_w_out"], "nn", 512, 512, f32, n + "ffn_out")
    if next_pre_mix is None:
        (x2,), h1_next = _rows(_res_rms_fn, [f, x1], [W["norm_post_ffn"]], [(D_MODEL, f32)], n + "post_ffn_norm"), None
    else:
        x2, h1_next = _rows(_res_rms_pre_fn, [f, x1], [W["norm_post_ffn"], next_pre_mix],
                            [(D_MODEL, f32), (D_MODEL, bf16)], n + "post_ffn_next_pre_mix_norm")
    saved = dict(x=x, h1=h1, proj=proj, attn_out=attn_out, lse=lse, qkv=qkv, dn_pre=dn_pre, dn_o=dn_o,
                 dn_states=dn_states, mix_in=mix_in, mix=mix, x1=x1, h2=h2, u0=u0, act=act, f=f, late=late)
    return x2, h1_next, saved


def _layer_bwd(dx2, sv, W, cos, sgn_sin, l, after_ffn=None, next_layer=None, first_layer=True):
    n = f"l{l}_"
    S = dx2.shape[0]
    g = {}
    g_next_pre = None
    if next_layer is None:
        (df,), (g["norm_post_ffn"],) = _rows_vjp(_rms_fn, [sv["f"]], [W["norm_post_ffn"]], [dx2], [0], [0],
                                                 n + "post_ffn_norm_bwd", row_dtype=bf16)
    else:
        (df, dx2), (g["norm_post_ffn"], g_next_pre) = _rows_vjp(
            _res_rms_pre_fn, [sv["f"], sv["x1"]], [W["norm_post_ffn"], next_layer[1]], [dx2, next_layer[0]], [0, 1], [0, 1],
            n + "post_ffn_next_pre_mix_norm_bwd", row_dtype=[bf16, f32])
    dact = _mm(df, W["ffn_w_out"], "nt", 512, 1408, f32, n + "ffn_out_dx")
    g["ffn_w_out"] = _mm(sv["act"], df, "tn", 256, 1024, f32, n + "ffn_out_dw")
    nb_ff = D_FF // 256
    u0 = sv["u0"]
    dxs, dws, dbs = _colconv_bwd([(u0, 0), (u0, nb_ff)], [(W["ffn_conv_w"], 0), (W["ffn_conv_w"], nb_ff)],
                                 [(W["ffn_conv_b"], 0), (W["ffn_conv_b"], nb_ff)], 3, _geglu_fn, [dact], nb_ff, 256,
                                 n + "ffn_conv_glu_bwd", dx_dtype=bf16)
    du0 = jnp.concatenate(dxs, axis=1)
    g["ffn_conv_w"] = jnp.concatenate(dws, axis=1)
    g["ffn_conv_b"] = jnp.concatenate(dbs, axis=1)
    dh2 = _mm(du0, W["ffn_w_in"], "nt", 512, 512, f32, n + "ffn_in_dx")
    g["ffn_w_in"] = _mm(sv["h2"], du0, "tn", 512, D_FF // 2, f32, n + "ffn_in_dw", column_shards=True)
    if after_ffn is not None:
        W = dict(W, norm_post_mix=W["norm_post_mix"] + after_ffn(g, dh2))
    (dmix, dx1), (g["norm_post_mix"], g["norm_pre_ffn"]) = _rows_vjp(
        _res_rms_pre_fn, [sv["mix"], sv["x"]], [W["norm_post_mix"], W["norm_pre_ffn"]], [dx2, dh2], [0, 1], [0, 1],
        n + "post_mix_pre_ffn_norm_bwd", row_dtype=[bf16, f32])
    dmix_in = _mm(dmix, W["w_out"], "nt", 512, 512, f32, n + "out_proj_dx")
    g["w_out"] = _mm(sv["mix_in"], dmix, "tn", 512, 512, f32, n + "out_proj_dw")

    (ddn_o, dz), (g["dn_norm_w"],) = _rows_vjp(
        _dn_post_fn, [(sv["dn_o"], DK, 0), (sv["proj"], DK, 3072 // DK)], [W["dn_norm_w"]], [(dmix_in, DK, ATTN_W // DK)],
        [0, 1], [0], n + "dn_post_bwd", ncol=N_HEADS_D, tm=4 * ROW_TILE)
    dpre = _dn_scan_bwd(sv["dn_pre"], sv["dn_states"], ddn_o, n + "dn_scan_bwd")
    dq, dk, dv, dba, g["dn_a_log"], g["dn_dt_bias"] = _dn_prep_bwd(
        sv["qkv"], sv["proj"], W["dn_a_log"], W["dn_dt_bias"], dpre, n + "dn_prep_bwd")
    dqkv = jnp.concatenate([dq, dk, dv], axis=1)
    (dqkv0,), (g["dn_conv_w"],), _ = _colconv_bwd([(sv["proj"], 3)], [(W["dn_conv_w"], 0)], None, 4, _silu_fn,
                                                 [dqkv], 3, 512, n + "dn_conv_bwd")

    daq, dak, dav = _attn_bwd(sv["proj"], cos, sgn_sin, dmix_in, sv["attn_out"], sv["lse"], n + "attn_bwd")
    dproj = jnp.concatenate([daq, dak, dav, dqkv0, dz, dba, jnp.zeros((S, PROJ_W - 3712), f32)], axis=1).astype(bf16)
    dh1 = _mm(dproj, W["w_in"], "nt", 512, 512, f32, n + "in_proj_dx")
    g["w_in"] = _mm(sv["h1"], dproj, "tn", 512, 768, f32, n + "in_proj_dw")
    if not first_layer:
        return (dx1, dh1), g, g_next_pre
    (dx,), (g["norm_pre_mix"],) = _rows_vjp(_rms_fn, [sv["x"]], [W["norm_pre_mix"]], [dh1], [0], [0],
                                            n + "pre_mix_norm_bwd", adds={0: dx1})
    return dx, g, g_next_pre


def _local_step(x, target, layers):
    cos, sgn_sin = _rope_tables(x.shape[0])
    saved, h1 = [], None
    for l, W in enumerate(layers):
        nxt = layers[l + 1]["norm_pre_mix"] if l + 1 < len(layers) else None
        x, h1, sv = _layer_fwd(x, W, cos, sgn_sin, l, h1=h1, next_pre_mix=nxt)
        saved.append(sv)
    loss, dx = _loss_head(x, target, "loss_head")
    grads = [None] * len(layers)
    nxt = None
    for l in reversed(range(len(layers))):
        dx, grads[l], g_pre = _layer_bwd(dx, saved[l], layers[l], cos, sgn_sin, l, next_layer=nxt, first_layer=(l == 0))
        if g_pre is not None:
            grads[l + 1]["norm_pre_mix"] = g_pre
        if l > 0:
            dx, dh1 = dx
            nxt = (dh1, layers[l]["norm_pre_mix"])
    return loss, dx, grads


def _pos():
    x, y, c = lax.axis_index("x"), lax.axis_index("y"), lax.axis_index("c")
    return x, y, c, [(1 - x, y), (x, 1 - y), (1 - x, 1 - y)]


def _rcopy(src, dst, send_sem, recv_sem, dev):
    return pltpu.make_async_remote_copy(src_ref=src, dst_ref=dst, send_sem=send_sem, recv_sem=recv_sem,
                                        device_id=dev, device_id_type=MESH)


def _half_rows(ref, h, which, axis):
    if h is None:
        return ref
    rows = pl.ds(pl.multiple_of(which * h, 16), h)
    return ref.at[:, rows, :] if axis == 1 else ref.at[rows, :]


def _dma_sems(*counts):
    return [pltpu.SemaphoreType.DMA((k,)) for k in counts]


def _all_gather(arrs, halves, name):
    n = len(arrs)

    def body(*refs):
        ins, outs = refs[:n], refs[n:2 * n]
        send1, recv1, send2, recv2 = refs[2 * n:]
        x, y, c, chips = _pos()
        me, sib, s_me = (x, y, c), (x, y, 1 - c), 2 * x + y
        sends = []
        for i in range(n):
            for j, chip in enumerate(chips):
                cp = _rcopy(_half_rows(ins[i], halves[i], c, 1), _half_rows(outs[i].at[s_me], halves[i], c, 1),
                            send1.at[3 * i + j], recv1.at[3 * i + j], (*chip, c))
                cp.start()
                sends.append(cp)
        for i in range(n):
            for j, (px, py) in enumerate(chips):
                k = 3 * i + j
                landed = _half_rows(outs[i].at[2 * px + py], halves[i], c, 1)
                _rcopy(landed, landed, send1.at[k], recv1.at[k], me).wait_recv()
                if halves[i] is not None:
                    cp = _rcopy(landed, landed, send2.at[k], recv2.at[k], sib)
                    cp.start()
                    sends.append(cp)
        for i in range(n):
            if halves[i] is None:
                continue
            for j, (px, py) in enumerate(chips):
                k = 3 * i + j
                other = _half_rows(outs[i].at[2 * px + py], halves[i], 1 - c, 1)
                _rcopy(other, other, send2.at[k], recv2.at[k], me).wait_recv()
        for cp in sends:
            cp.wait_send()

    return pl.pallas_call(
        body, in_specs=[ANY] * n, out_specs=[ANY] * n,
        out_shape=[SDS((4,) + a.shape, a.dtype) for a in arrs],
        scratch_shapes=_dma_sems(3 * n, 3 * n, 3 * n, 3 * n), name=name)(*arrs)


HBM = pl.BlockSpec(memory_space=pltpu.HBM)
SEM = pl.BlockSpec(memory_space=pltpu.SEMAPHORE)
_EFFECT = pltpu.SideEffectType.DATAFLOW_SIDE_EFFECTING


def _in_hbm(a):
    return pltpu.with_memory_space_constraint(a, pltpu.HBM)


def _split_copy(srcs, land_shapes, plan, per, after, name):
    n = len(srcs)
    k = per * n

    def body(*refs):
        ins, lands, token = refs[:n], refs[n:2 * n], refs[-1]
        send, recv = refs[2 * n + 1], refs[2 * n + 2]
        for i, (src, dst, dev, _) in enumerate(plan(ins, lands)):
            _rcopy(src, dst, send.at[i], recv.at[i], dev).start()
        token[...] = jnp.zeros_like(token)

    lands = [_in_hbm(lax.empty(s.shape, s.dtype)) for s in land_shapes]
    return pl.pallas_call(
        body, name=name,
        out_shape=(pltpu.SemaphoreType.DMA((k,)), pltpu.SemaphoreType.DMA((k,)),
                   *[pltpu.HBM(a.shape, a.dtype) for a in srcs], *[pltpu.HBM(s.shape, s.dtype) for s in land_shapes],
                   SDS((8, 128), f32)),
        in_specs=[HBM] * (2 * n) + [ANY], out_specs=(SEM, SEM, *[HBM] * (2 * n), pl.BlockSpec(memory_space=pltpu.VMEM)),
        input_output_aliases={i: 2 + i for i in range(2 * n)},
        compiler_params=pltpu.CompilerParams(has_side_effects=_EFFECT))(*[_in_hbm(a) for a in srcs], *lands, after)


def _split_wait(started, n, plan, after, name):
    send, recv = started[0], started[1]
    thru = started[2:2 + 2 * n]

    def body(*refs):
        ins, lands = refs[:n], refs[n:2 * n]
        send_ref, recv_ref = refs[2 * n], refs[2 * n + 1]
        for i, (src, _, dev, mine) in enumerate(plan(ins, lands)):
            cp = _rcopy(src, mine, send_ref.at[i], recv_ref.at[i], dev)
            cp.wait_send()
            cp.wait_recv()

    res = pl.pallas_call(
        body, name=name, out_shape=tuple(pltpu.HBM(a.shape, a.dtype) for a in thru),
        in_specs=[HBM] * (2 * n) + [SEM, SEM, ANY], out_specs=tuple([HBM] * (2 * n)),
        input_output_aliases={i: i for i in range(2 * n)},
        compiler_params=pltpu.CompilerParams(has_side_effects=_EFFECT))(*thru, send, recv, after)
    return res[:n], res[n:]


def _gather_plan(halves):
    def plan(ins, lands):
        x, y, c, chips = _pos()
        out = []
        for i in range(len(ins)):
            for px, py in chips:
                out.append((_half_rows(ins[i], halves[i], c, 1), _half_rows(lands[i].at[2 * x + y], halves[i], c, 1),
                            (px, py, c), _half_rows(lands[i].at[2 * px + py], halves[i], c, 1)))
        return out
    return plan


def _scatter_plan(ins, lands):
    x, y, c, chips = _pos()
    out = []
    for i in range(len(ins)):
        for j, (px, py) in enumerate(chips):
            out.append((ins[i].at[2 * px + py], lands[i].at[j], (px, py, c), lands[i].at[j]))
    return out


def _exchange_plan(ins, lands):
    x, y, c, _ = _pos()
    return [(_half_rows(g, g.shape[1] // 2, 1 - c, 1), land, (x, y, 1 - c), land) for g, land in zip(ins, lands)]


def _pass_to_sibling(lands, halves, name):
    n = len(lands)

    def body(*refs):
        outs = refs[n:2 * n]
        send, recv = refs[2 * n:]
        x, y, c, chips = _pos()
        sends = []
        for i in range(n):
            for j, (px, py) in enumerate(chips):
                landed = _half_rows(outs[i].at[2 * px + py], halves[i], c, 1)
                cp = _rcopy(landed, landed, send.at[3 * i + j], recv.at[3 * i + j], (x, y, 1 - c))
                cp.start()
                sends.append(cp)
        for i in range(n):
            for j, (px, py) in enumerate(chips):
                other = _half_rows(outs[i].at[2 * px + py], halves[i], 1 - c, 1)
                _rcopy(other, other, send.at[3 * i + j], recv.at[3 * i + j], (x, y, c)).wait_recv()
        for cp in sends:
            cp.wait_send()

    return pl.pallas_call(
        body, in_specs=[ANY] * n, out_specs=[ANY] * n, out_shape=[SDS(a.shape, a.dtype) for a in lands],
        input_output_aliases={k: k for k in range(n)}, scratch_shapes=_dma_sems(3 * n, 3 * n), name=name)(*lands)


def _exchange_halves(gs, name):
    n = len(gs)

    def body(*refs):
        ins, outs = refs[:n], refs[n:2 * n]
        send, recv = refs[2 * n:]
        x, y, c, _ = _pos()
        sends = []
        for k in range(n):
            cp = _rcopy(_half_rows(ins[k], gs[k].shape[1] // 2, 1 - c, 1), outs[k], send.at[k], recv.at[k], (x, y, 1 - c))
            cp.start()
            sends.append(cp)
        for k in range(n):
            _rcopy(outs[k], outs[k], send.at[k], recv.at[k], (x, y, c)).wait_recv()
        for cp in sends:
            cp.wait_send()

    return pl.pallas_call(
        body, in_specs=[ANY] * n, out_specs=[ANY] * n,
        out_shape=[SDS((4, g.shape[1] // 2, g.shape[2]), g.dtype) for g in gs],
        scratch_shapes=_dma_sems(n, n), name=name)(*gs)


def _scatter_partials(ps, name):
    n = len(ps)

    def body(*refs):
        ins, outs = refs[:n], refs[n:2 * n]
        send, recv = refs[2 * n:]
        x, y, c, chips = _pos()
        sends = []
        for k in range(n):
            for j, (px, py) in enumerate(chips):
                cp = _rcopy(ins[k].at[2 * px + py], outs[k].at[j], send.at[3 * k + j], recv.at[3 * k + j], (px, py, c))
                cp.start()
                sends.append(cp)
        for k in range(n):
            for j in range(3):
                _rcopy(outs[k].at[j], outs[k].at[j], send.at[3 * k + j], recv.at[3 * k + j], (x, y, c)).wait_recv()
        for cp in sends:
            cp.wait_send()

    return pl.pallas_call(
        body, in_specs=[ANY] * n, out_specs=[ANY] * n,
        out_shape=[SDS((3,) + p.shape[1:], p.dtype) for p in ps],
        scratch_shapes=_dma_sems(3 * n, 3 * n), name=name)(*ps)


def _join_halves(rs, name):
    n = len(rs)

    def body(*refs):
        outs = refs[n:2 * n]
        send, recv = refs[2 * n:]
        x, y, c, _ = _pos()
        sends = []
        for k in range(n):
            mine = _half_rows(outs[k], rs[k].shape[0] // 2, c, 0)
            cp = _rcopy(mine, mine, send.at[k], recv.at[k], (x, y, 1 - c))
            cp.start()
            sends.append(cp)
        for k in range(n):
            other = _half_rows(outs[k], rs[k].shape[0] // 2, 1 - c, 0)
            _rcopy(other, other, send.at[k], recv.at[k], (x, y, c)).wait_recv()
        for cp in sends:
            cp.wait_send()

    return pl.pallas_call(
        body, in_specs=[ANY] * n, out_specs=[ANY] * n, out_shape=[SDS(r.shape, r.dtype) for r in rs],
        input_output_aliases={k: k for k in range(n)}, scratch_shapes=_dma_sems(n, n), name=name)(*rs)


def _all_reduce_small(pack, name):
    R = pack.shape[0]

    def body(in_ref, out_ref, buf, send, recv):
        x, y, c, _ = _pos()
        me = 4 * x + 2 * y + c
        buf[me] = in_ref[...]
        sends = []
        for k in range(1, 8):
            peer = me ^ k
            cp = _rcopy(buf.at[me], buf.at[me], send.at[k - 1], recv.at[k - 1], ((peer >> 2) & 1, (peer >> 1) & 1, peer & 1))
            cp.start()
            sends.append(cp)
        for k in range(1, 8):
            _rcopy(buf.at[me ^ k], buf.at[me ^ k], send.at[k - 1], recv.at[k - 1], (x, y, c)).wait_recv()
        for cp in sends:
            cp.wait_send()
        acc = buf[0]
        for d in range(1, 8):
            acc = acc + buf[d]
        out_ref[...] = acc

    return pl.pallas_call(
        body, out_shape=SDS((R, 128), f32),
        in_specs=[pl.BlockSpec(memory_space=pltpu.VMEM)], out_specs=pl.BlockSpec(memory_space=pltpu.VMEM),
        scratch_shapes=[pltpu.VMEM((8, R, 128), f32)] + _dma_sems(7, 7), name=name)(pack)


def _add_sibling(g, recv, c_arr, tr, name):
    _, R, C = g.shape
    h = R // 2
    nrb = h // tr
    assert h % tr == 0

    def body(c_ref, g_ref, r_ref, o_ref):
        o_ref[...] = (g_ref[...] + r_ref[...]).astype(o_ref.dtype)

    spec = pl.BlockSpec((1, tr, C), lambda s, r, c_ref: (s, r, 0))
    grid_spec = pltpu.PrefetchScalarGridSpec(
        num_scalar_prefetch=1, grid=(4, nrb),
        in_specs=[pl.BlockSpec((1, tr, C), lambda s, r, c_ref: (s, c_ref[0] * nrb + r, 0)), spec], out_specs=spec)
    return pl.pallas_call(body, grid_spec=grid_spec, out_shape=SDS((4, h, C), bf16), name=name,
                          compiler_params=_params("parallel", "parallel"))(c_arr, g, recv)


def _add_chips(p, recv, sc_arr, tr, name):
    _, h, C = p.shape
    nrb = h // tr
    assert h % tr == 0

    def body(sc_ref, p_ref, r_ref, o_ref):
        o_ref[...] = (p_ref[0].astype(f32) + r_ref[0].astype(f32)) + (r_ref[1].astype(f32) + r_ref[2].astype(f32))

    grid_spec = pltpu.PrefetchScalarGridSpec(
        num_scalar_prefetch=1, grid=(nrb,),
        in_specs=[pl.BlockSpec((1, tr, C), lambda r, sc_ref: (sc_ref[0], r, 0)),
                  pl.BlockSpec((3, tr, C), lambda r, sc_ref: (0, r, 0))],
        out_specs=pl.BlockSpec((tr, C), lambda r, sc_ref: (sc_ref[1] * nrb + r, 0)))
    return pl.pallas_call(body, grid_spec=grid_spec, out_shape=SDS((2 * h, C), f32), name=name,
                          compiler_params=_params("parallel"))(sc_arr, p, recv)


_BIG = (("w_in", 1024, 256), ("w_out", 256, 128), ("ffn_w_in", 1024, 256), ("ffn_w_out", 704, 352))
_SMALL = ("dn_conv_w", "ffn_conv_w", "ffn_conv_b", "norm_pre_mix", "norm_post_mix", "norm_pre_ffn", "norm_post_ffn",
          "dn_norm_w", "dn_a_log", "dn_dt_bias")
_WEIGHTS = ("w_in", "dn_conv_w", "dn_a_log", "dn_dt_bias", "dn_norm_w", "w_out", "ffn_w_in", "ffn_conv_w", "ffn_conv_b",
            "ffn_w_out", "norm_pre_mix", "norm_post_mix", "norm_pre_ffn", "norm_post_ffn")
_ADAM_ROWS = {"w_in": 256, "w_out": 256, "ffn_w_in": 128, "ffn_w_out": 176}


def _shard_major(name, g):
    if name == "w_in":
        return jnp.stack([g[:, 898 * s:898 * (s + 1)] for s in range(4)])
    if name == "ffn_w_in":
        return g
    return g.reshape(4, g.shape[0] // 4, g.shape[1])


def kernel(x, w_in, dn_conv_w, dn_a_log, dn_dt_bias, dn_norm_w, w_out, ffn_w_in, ffn_conv_w, ffn_conv_b, ffn_w_out, norm_pre_mix, norm_post_mix, norm_pre_ffn, norm_post_ffn, loss_target, m_w_in, m_dn_conv_w, m_dn_a_log, m_dn_dt_bias, m_dn_norm_w, m_w_out, m_ffn_w_in, m_ffn_conv_w, m_ffn_conv_b, m_ffn_w_out, m_norm_pre_mix, m_norm_post_mix, m_norm_pre_ffn, m_norm_post_ffn, v_w_in, v_dn_conv_w, v_dn_a_log, v_dn_dt_bias, v_dn_norm_w, v_w_out, v_ffn_w_in, v_ffn_conv_w, v_ffn_conv_b, v_ffn_w_out, v_norm_pre_mix, v_norm_post_mix, v_norm_pre_ffn, v_norm_post_ffn):
    w = dict(w_in=w_in, dn_conv_w=dn_conv_w, dn_a_log=dn_a_log, dn_dt_bias=dn_dt_bias, dn_norm_w=dn_norm_w, w_out=w_out,
             ffn_w_in=ffn_w_in, ffn_conv_w=ffn_conv_w, ffn_conv_b=ffn_conv_b, ffn_w_out=ffn_w_out, norm_pre_mix=norm_pre_mix,
             norm_post_mix=norm_post_mix, norm_pre_ffn=norm_pre_ffn, norm_post_ffn=norm_post_ffn)
    m = dict(w_in=m_w_in, dn_conv_w=m_dn_conv_w, dn_a_log=m_dn_a_log, dn_dt_bias=m_dn_dt_bias, dn_norm_w=m_dn_norm_w,
             w_out=m_w_out, ffn_w_in=m_ffn_w_in, ffn_conv_w=m_ffn_conv_w, ffn_conv_b=m_ffn_conv_b, ffn_w_out=m_ffn_w_out,
             norm_pre_mix=m_norm_pre_mix, norm_post_mix=m_norm_post_mix, norm_pre_ffn=m_norm_pre_ffn,
             norm_post_ffn=m_norm_post_ffn)
    v = dict(w_in=v_w_in, dn_conv_w=v_dn_conv_w, dn_a_log=v_dn_a_log, dn_dt_bias=v_dn_dt_bias, dn_norm_w=v_dn_norm_w,
             w_out=v_w_out, ffn_w_in=v_ffn_w_in, ffn_conv_w=v_ffn_conv_w, ffn_conv_b=v_ffn_conv_b, ffn_w_out=v_ffn_w_out,
             norm_pre_mix=v_norm_pre_mix, norm_post_mix=v_norm_post_mix, norm_pre_ffn=v_norm_pre_ffn,
             norm_post_ffn=v_norm_post_ffn)
    xi, yi, ci = lax.axis_index("x"), lax.axis_index("y"), lax.axis_index("c")
    s_me = 2 * xi + yi
    c_arr = jnp.reshape(ci, (1,)).astype(jnp.int32)
    sc_arr = jnp.stack([s_me, ci]).astype(jnp.int32)

    mats = [name for name, _, _ in _BIG]
    rest = mats[1:]
    half_of = {name: rows // 2 for name, rows, _ in _BIG}
    tiles = {name: tr for name, _, tr in _BIG}
    gathered_shape = lambda a: SDS((4,) + a.shape, a.dtype)

    own = {k: w[k].astype(bf16) for k in mats}
    got_in = _all_gather([own["w_in"][0:1], dn_conv_w, ffn_conv_w], [half_of["w_in"], None, None], "weights_gather_w_in0")
    plan0 = _gather_plan([half_of[k] for k in rest])
    src0 = [own[k][0:1] for k in rest]
    started0 = _split_copy(src0, [gathered_shape(a) for a in src0], plan0, 3, got_in[0], "weights_gather_l0_start")
    plan1 = _gather_plan([half_of[k] for k in mats])
    src1 = [own[k][1:2] for k in mats]
    started1 = _split_copy(src1, [gathered_shape(a) for a in src1], plan1, 3, started0[-1], "weights_gather_l1_start")

    def pick(mine, gathered):
        return [jnp.where(s_me == s, mine, gathered[s]) for s in range(4)]

    conv = {"dn_conv_w": jnp.concatenate(pick(dn_conv_w, got_in[1]), axis=-1),
            "ffn_conv_w": jnp.concatenate(pick(ffn_conv_w, got_in[2]), axis=-1)}
    lanes = lambda a: jnp.pad(a, ((0, 0), (0, 128 - a.shape[1])))
    vec = dict(dn_a_log=lanes(dn_a_log), dn_dt_bias=lanes(dn_dt_bias), dn_norm_w=dn_norm_w, ffn_conv_b=ffn_conv_b,
               norm_pre_mix=norm_pre_mix, norm_post_mix=norm_post_mix, norm_pre_ffn=norm_pre_ffn, norm_post_ffn=norm_post_ffn)

    def matrices(l, names, gathered):
        W = {}
        for k, a in zip(names, gathered):
            if k in ("w_out", "ffn_w_out"):
                rows_, cols = own[k].shape[1:]
                W[k] = lax.dynamic_update_slice(a[:, 0], own[k][l][None], (s_me, 0, 0)).reshape(4 * rows_, cols)
            else:
                cat = jnp.concatenate(pick(own[k][l], a[:, 0]), axis=-1)
                W[k] = jnp.pad(cat, ((0, 0), (0, PROJ_W - IN_COLS))) if k == "w_in" else cat
        return W

    def small_weights(l):
        return {**{k: a[l] for k, a in conv.items()}, **{k: a[l:l + 1] for k, a in vec.items()}}

    def late_l0(mix_in):
        _, landed = _split_wait(started0, len(rest), plan0, mix_in, "weights_gather_l0_wait")
        return matrices(0, rest, _pass_to_sibling(landed, [half_of[k] for k in rest], "weights_gather_l0_sibling"))

    cos, sgn_sin = _rope_tables(x.shape[1])
    W0 = {**small_weights(0), **matrices(0, ["w_in"], got_in[:1])}
    W0_first = dict(W0, norm_pre_mix=W0["norm_pre_mix"] + started1[-1][0, 0])
    x1, h1_l1, saved0 = _layer_fwd(x[0], W0_first, cos, sgn_sin, 0, late_weights=late_l0,
                                   next_pre_mix=norm_pre_mix[1:2])
    _, landed1 = _split_wait(started1, len(mats), plan1, x1, "weights_gather_l1_wait")
    W1 = {**small_weights(1),
          **matrices(1, mats, _pass_to_sibling(landed1, [half_of[k] for k in mats], "weights_gather_l1_sibling"))}
    x2, _, saved1 = _layer_fwd(x1, W1, cos, sgn_sin, 1, h1=h1_l1)
    loss_local, dy = _loss_head(x2, loss_target[0], "loss_head")
    loss = lax.psum(loss_local, ("x", "y", "c"))

    def shard_major(names, grads_l):
        return [_shard_major(name, grads_l[name]) for name in names]

    def add_siblings(l, names, gs, from_sib):
        return [_add_sibling(g, r, c_arr, tiles[name], f"add_sibling_{name}{l}") for g, r, name in zip(gs, from_sib, names)]

    def scatter_start(l, names, parts, after, tag):
        return _split_copy(parts, [SDS((3,) + p.shape[1:], p.dtype) for p in parts], _scatter_plan, 3, after,
                           f"grads_l{l}{tag}_scatter_start")

    def owner_sums(l, names, sent, after, tag):
        parts, recvd = _split_wait(sent, len(names), _scatter_plan, after, f"grads_l{l}{tag}_scatter_wait")
        return [_add_chips(p, r, sc_arr, tiles[name], f"add_chips_{name}{l}") for p, r, name in zip(parts, recvd, names)]

    (dx1, dh1_l1), grads1, _ = _layer_bwd(dy, saved1, W1, cos, sgn_sin, 1, first_layer=False)
    gs1 = shard_major(mats, grads1)
    swap1 = _split_copy(gs1, [SDS((4, g.shape[1] // 2, g.shape[2]), g.dtype) for g in gs1], _exchange_plan, 1, dx1,
                        "grads_l1_sibling_start")
    ffn = ["ffn_w_in", "ffn_w_out"]
    launched = {}

    def after_ffn_l0(g_ffn, dx_mid):
        gs1_, from_sib1 = _split_wait(swap1, len(mats), _exchange_plan, dx_mid, "grads_l1_sibling_wait")
        launched["l1"] = scatter_start(1, mats, add_siblings(1, mats, gs1_, from_sib1), dx_mid, "")
        gs0 = shard_major(ffn, g_ffn)
        from_sib0 = _exchange_halves(gs0, "grads_l0_ffn_to_sibling")
        launched["l0_ffn"] = scatter_start(0, ffn, add_siblings(0, ffn, gs0, from_sib0), launched["l1"][-1], "_ffn")
        return launched["l0_ffn"][-1][0, 0]

    W0_last = dict(W0, **saved0["late"], norm_post_ffn=W0["norm_post_ffn"] + swap1[-1][0, 0])
    dx, grads0, grads1["norm_pre_mix"] = _layer_bwd(dx1, saved0, W0_last, cos, sgn_sin, 0, after_ffn=after_ffn_l0,
                                                    next_layer=(dh1_l1, norm_pre_mix[1:2]))
    mix = ["w_in", "w_out"]
    gs0 = shard_major(mix, grads0)
    part0 = add_siblings(0, mix, gs0, _exchange_halves(gs0, "grads_l0_mix_to_sibling"))
    sent0 = scatter_start(0, mix, part0, dx, "_mix")
    red = dict(zip([(0, k) for k in ffn], owner_sums(0, ffn, launched["l0_ffn"], sent0[-1], "_ffn")))
    red.update(zip([(1, k) for k in mats], owner_sums(1, mats, launched["l1"], sent0[-1], "")))
    early = [(0, k) for k in ffn] + [(1, k) for k in mats]
    joined = dict(zip(early, _join_halves([red[key] for key in early], "grads_join_early")))
    grads = [grads0, grads1]

    small = {}
    for name in _SMALL:
        per_layer = [grads[l][name] for l in range(2)]
        if name in ("dn_a_log", "dn_dt_bias"):
            per_layer = [p[:, :N_HEADS_D] for p in per_layer]
        small[name] = jnp.stack(per_layer).reshape((2,) + (w[name].shape[1:] if name not in ("dn_conv_w", "ffn_conv_w")
                                                           else per_layer[0].shape))
    flat = jnp.concatenate([small[name].reshape(-1) for name in _SMALL])
    n_rows = -(-flat.shape[0] // 1024) * 8
    summed = _all_reduce_small(jnp.pad(flat, (0, n_rows * 128 - flat.shape[0])).reshape(n_rows, 128),
                               "small_grads_all_reduce").reshape(-1)
    off = 0
    g_out = {}
    for name in _SMALL:
        size = small[name].size
        g_out[name] = summed[off:off + size].reshape(small[name].shape)
        off += size
    g_out["dn_conv_w"] = lax.dynamic_slice_in_dim(g_out["dn_conv_w"], s_me * 384, 384, axis=2)
    g_out["ffn_conv_w"] = lax.dynamic_slice_in_dim(g_out["ffn_conv_w"], s_me * 1408, 1408, axis=2)
    for k in ffn:
        g_out[k] = jnp.stack([joined[(0, k)], joined[(1, k)]])

    deltas, new_m, new_v = {}, {}, {}

    def step(name):
        shape = w[name].shape
        as3 = (lambda a: a) if len(shape) == 3 else (lambda a: a.reshape(shape[0], 1, shape[1]))
        tr = _ADAM_ROWS.get(name, as3(w[name]).shape[1])
        d_, m_, v_ = _adamw(as3(w[name]), as3(g_out[name]), as3(m[name]), as3(v[name]), tr, f"adamw_{name}")
        deltas[name], new_m[name], new_v[name] = d_.reshape(shape), m_.reshape(shape), v_.reshape(shape)

    for name in ffn:
        step(name)
    tiny = [name for name in _WEIGHTS if name not in mats]
    stepped = _adamw_small(*[[d[name] for name in tiny] for d in (w, g_out, m, v)], "adamw_small")
    for out, vals in zip((deltas, new_m, new_v), stepped):
        out.update(zip(tiny, vals))
    done = jnp.reshape(deltas["ffn_w_in"][0, 0, 0] + deltas["ffn_w_out"][0, 0, 0] + deltas["norm_post_ffn"][0, 0], (1,))
    late = _join_halves(owner_sums(0, mix, sent0, done, "_mix"), "grads_join_late")
    for k, a in zip(mix, late):
        g_out[k] = jnp.stack([a, joined[(1, k)]])
        step(k)

    return (loss, dx[None], *[g_out[k] for k in _WEIGHTS], *[deltas[k] for k in _WEIGHTS],
            *[new_m[k] for k in _WEIGHTS], *[new_v[k] for k in _WEIGHTS])
```

```python
import jax
import jax.numpy as jnp
from jax import lax
from jax.experimental import pallas as pl
from jax.experimental.pallas import tpu as pltpu

f32, bf16 = jnp.float32, jnp.bfloat16
SDS = jax.ShapeDtypeStruct
HI = lax.Precision.HIGH
MESH = pl.DeviceIdType.MESH
ANY = pl.BlockSpec(memory_space=pl.ANY)

D_MODEL = 1024
N_HEADS_A, HEAD_DIM = 8, 64
ATTN_W = 512
N_HEADS_D, DK = 4, 128
CHUNK = 64
D_FF = 2816
IN_COLS = 3592
PROJ_W = 3840
BRANCHES = ((1, 16), (4, 4), (16, 1))
EPS = 1e-6
NEG = -1e30
ROW_TILE = 256
VMEM_LIMIT = 56 * 1024 * 1024

ADAM_LR, ADAM_B1, ADAM_B2, ADAM_EPS, ADAM_WD, ADAM_STEP = 0.001, 0.9, 0.999, 1e-08, 0.01, 10


def _params(*sem):
    return pltpu.CompilerParams(dimension_semantics=sem, vmem_limit_bytes=VMEM_LIMIT)


def _mm(a, b, mode, tm, tn, out_dtype, name, column_shards=False):
    if mode == "nn":
        (M, K), N = a.shape, b.shape[1]
        dims = (((1,), (0,)), ((), ()))
        a_spec = pl.BlockSpec((tm, K), lambda i, j: (i, 0))
        b_spec = pl.BlockSpec((K, tn), lambda i, j: (0, j))
    elif mode == "nt":
        (M, K), N = a.shape, b.shape[0]
        dims = (((1,), (1,)), ((), ()))
        a_spec = pl.BlockSpec((tm, K), lambda i, j: (i, 0))
        b_spec = pl.BlockSpec((tn, K), lambda i, j: (j, 0))
    else:
        (K, M), N = a.shape, b.shape[1]
        dims = (((0,), (0,)), ((), ()))
        a_spec = pl.BlockSpec((K, tm), lambda i, j: (0, i))
        b_spec = pl.BlockSpec((K, tn), lambda i, j: (0, j))
    assert M % tm == 0 and N % tn == 0, (name, M, N, tm, tn)

    def body(a_ref, b_ref, o_ref):
        o_ref[...] = lax.dot_general(a_ref[...].astype(bf16), b_ref[...].astype(bf16), dims,
                                     preferred_element_type=f32).astype(o_ref.dtype)

    if column_shards:
        out_spec, out_shape = pl.BlockSpec((None, tm, tn), lambda i, j: (j, i, 0)), SDS((N // tn, M, tn), out_dtype)
    else:
        out_spec, out_shape = pl.BlockSpec((tm, tn), lambda i, j: (i, j)), SDS((M, N), out_dtype)
    return pl.pallas_call(body, grid=(M // tm, N // tn), in_specs=[a_spec, b_spec], out_specs=out_spec,
                          out_shape=out_shape, name=name, compiler_params=_params("parallel", "arbitrary"))(a, b)


def _row_spec(r, tm):
    if isinstance(r, tuple):
        arr, width, cb = r
        return arr, pl.BlockSpec((tm, width), lambda i, j, cb=cb: (i, cb + j))
    return r, pl.BlockSpec((tm, r.shape[1]), lambda i, j: (i, j))


def _full_spec(p):
    return pl.BlockSpec(p.shape, lambda i, j: (0,) * p.ndim)


def _rows(fn, rows, params, outs, name, tm=ROW_TILE, ncol=1):
    arrs, specs = zip(*[_row_spec(r, tm) for r in rows])
    S = arrs[0].shape[0]
    nr, npar = len(rows), len(params)

    def body(*refs):
        vals = fn(*[r[...].astype(f32) for r in refs[:nr]], *[p[...] for p in refs[nr:nr + npar]])
        for o_ref, v in zip(refs[nr + npar:], vals):
            o_ref[...] = v.astype(o_ref.dtype)

    return pl.pallas_call(
        body, grid=(S // tm, ncol), in_specs=list(specs) + [_full_spec(p) for p in params],
        out_specs=[pl.BlockSpec((tm, w), lambda i, j: (i, j)) for w, _ in outs],
        out_shape=[SDS((S, w * ncol), dt) for w, dt in outs], name=name,
        compiler_params=_params("parallel", "parallel"))(*arrs, *params)


def _rows_vjp(fn, rows, params, cts, wrt_rows, wrt_params, name, adds=None, tm=ROW_TILE, ncol=1, row_dtype=f32):
    adds = adds or {}
    arrs, specs = zip(*[_row_spec(r, tm) for r in rows])
    carrs, cspecs = zip(*[_row_spec(c, tm) for c in cts])
    add_keys = sorted(adds)
    aarrs = [adds[k] for k in add_keys]
    S = arrs[0].shape[0]
    nr, npar, nc, na = len(rows), len(params), len(cts), len(aarrs)
    widths = [specs[k].block_shape[1] for k in wrt_rows]
    row_dtypes = row_dtype if isinstance(row_dtype, (list, tuple)) else [row_dtype] * len(wrt_rows)

    def body(*refs):
        first = jnp.logical_and(pl.program_id(0) == 0, pl.program_id(1) == 0)
        rv = [r[...].astype(f32) for r in refs[:nr]]
        pv = [p[...] for p in refs[nr:nr + npar]]
        cv = tuple(c[...].astype(f32) for c in refs[nr + npar:nr + npar + nc])
        av = dict(zip(add_keys, refs[nr + npar + nc:nr + npar + nc + na]))
        o = refs[nr + npar + nc + na:]
        _, vjp = jax.vjp(fn, *rv, *pv)
        g = vjp(cv)
        for n, k in enumerate(wrt_rows):
            val = g[k]
            if k in av:
                val = val + av[k][...]
            o[n][...] = val.astype(o[n].dtype)
        for n, k in enumerate(wrt_params):
            ref = o[len(wrt_rows) + n]

            @pl.when(first)
            def _(ref=ref):
                ref[...] = jnp.zeros_like(ref)

            ref[...] += g[nr + k]

    res = pl.pallas_call(
        body, grid=(S // tm, ncol),
        in_specs=list(specs) + [_full_spec(p) for p in params] + list(cspecs)
        + [pl.BlockSpec((tm, a.shape[1] // ncol), lambda i, j: (i, j)) for a in aarrs],
        out_specs=[pl.BlockSpec((tm, w), lambda i, j: (i, j)) for w in widths] + [_full_spec(params[k]) for k in wrt_params],
        out_shape=[SDS((S, w * ncol), dt) for w, dt in zip(widths, row_dtypes)]
        + [SDS(params[k].shape, f32) for k in wrt_params],
        name=name, compiler_params=_params("arbitrary", "arbitrary"))(*arrs, *params, *carrs, *aarrs)
    return res[:len(wrt_rows)], res[len(wrt_rows):]


def _rms(x, w):
    return x * lax.rsqrt(jnp.mean(x * x, axis=-1, keepdims=True) + EPS) * w


def _rms_fn(x, w):
    return (_rms(x, w),)


def _res_rms_fn(f, res, w):
    return (res + _rms(f, w),)


def _res_rms_pre_fn(f, res, w_post, w_pre):
    x1 = res + _rms(f, w_post)
    return x1, _rms(x1, w_pre)


def _swap_halves(x):
    lane = lax.broadcasted_iota(jnp.int32, x.shape, 1)
    first = (lane % HEAD_DIM) < (HEAD_DIM // 2)
    n = x.shape[1]
    return jnp.where(first, pltpu.roll(x, n - HEAD_DIM // 2, 1), pltpu.roll(x, HEAD_DIM // 2, 1))


def _rope_fwd_fn(q, k, cos, sgn_sin):
    scale = HEAD_DIM ** -0.5
    return ((q * cos + _swap_halves(q) * sgn_sin) * scale, k * cos + _swap_halves(k) * sgn_sin)


def _rope_bwd_fn(dq, dk, cos, sgn_sin):
    dq = dq * (HEAD_DIM ** -0.5)
    return (dq * cos + _swap_halves(dq * sgn_sin), dk * cos + _swap_halves(dk * sgn_sin))


def _nt(a, b):
    return lax.dot_general(a, b, (((1,), (1,)), ((), ())), preferred_element_type=f32)


def _tn(a, b):
    return lax.dot_general(a, b, (((0,), (0,)), ((), ())), preferred_element_type=f32)


def _band_rows(j, d, nb):
    r, i = j // nb, j % nb
    if d == 1:
        cur = pl.ds(pl.multiple_of(i * 128, 128), 128)
        prev = pl.ds(pl.multiple_of(jnp.maximum(i - 1, 0) * 128, 128), 128)
    else:
        cur = pl.ds(i * (128 * d) + r, 128, stride=d)
        prev = pl.ds(jnp.maximum(i - 1, 0) * (128 * d) + r, 128, stride=d)
    return cur, prev, (i == 0).astype(jnp.int32)


def _band_bias(bias_ref):
    a = lax.broadcasted_iota(jnp.int32, (256, 256), 0) % 128
    c = lax.broadcasted_iota(jnp.int32, (256, 256), 1)
    own = jnp.logical_and(c < 128, c <= a)
    before = jnp.logical_and(c >= 128, c - 128 >= a)
    bias_ref[0] = jnp.where(jnp.logical_or(own, before), 0.0, NEG)
    bias_ref[1] = jnp.where(own, 0.0, NEG)


def _stack_heads(x, head_a):
    return jnp.concatenate([jnp.where(head_a, x, 0.0), jnp.where(head_a, 0.0, x)], axis=0)


def _unstack_heads(x2, head_a):
    return jnp.where(head_a, x2[:128], x2[128:])


def _pair_at(S, first_col):
    return pl.BlockSpec((S, 128), lambda h: (0, first_col // 128 + h))


def _attn_fwd(proj, cos, sgn_sin, name):
    S = proj.shape[0]
    nblk = S // 128

    def body(qp_ref, kp_ref, v_ref, cos_ref, sin_ref, out_ref, lse_ref, bias_ref, q_ref, k_ref, *scr):
        head_a = lax.broadcasted_iota(jnp.int32, (1, 128), 1) < HEAD_DIM
        _band_bias(bias_ref)
        q_ref[...], k_ref[...] = _rope_fwd_fn(qp_ref[...], kp_ref[...], cos_ref[...], sin_ref[...])
        for b, (d, nb) in enumerate(BRANCHES):
            ob_ref, lb_ref = scr[2 * b], scr[2 * b + 1]

            def blk(j, carry, d=d, nb=nb, ob_ref=ob_ref, lb_ref=lb_ref):
                cur, prev, first = _band_rows(j, d, nb)
                q2 = _stack_heads(q_ref[cur, :], head_a).astype(bf16)
                if nb == 1:
                    k2, v2, bias = k_ref[cur, :].astype(bf16), v_ref[cur, :].astype(bf16), bias_ref[1][:, :128]
                else:
                    k2 = jnp.concatenate([k_ref[cur, :], k_ref[prev, :]], axis=0).astype(bf16)
                    v2 = jnp.concatenate([v_ref[cur, :], v_ref[prev, :]], axis=0).astype(bf16)
                    bias = bias_ref[first]
                s = _nt(q2, k2) + bias
                mx = jnp.max(s, axis=1, keepdims=True)
                p = jnp.exp(s - mx)
                l = jnp.sum(p, axis=1, keepdims=True)
                o = jnp.dot(p.astype(bf16), v2, preferred_element_type=f32) / l
                ob_ref[cur, :] = _unstack_heads(o, head_a)
                lb_ref[cur, :] = _unstack_heads(jnp.broadcast_to(mx + jnp.log(l), (256, 128)), head_a)
                return carry

            lax.fori_loop(0, nblk, blk, 0, unroll=16)
        l0, l1, l2 = scr[1][...], scr[3][...], scr[5][...]
        mx = jnp.maximum(jnp.maximum(l0, l1), l2)
        e0, e1, e2 = jnp.exp(l0 - mx), jnp.exp(l1 - mx), jnp.exp(l2 - mx)
        den = e0 + e1 + e2
        out_ref[...] = ((e0 * scr[0][...] + e1 * scr[2][...] + e2 * scr[4][...]) / den).astype(out_ref.dtype)
        lse_ref[...] = mx + jnp.log(den)

    pair = pl.BlockSpec((S, 128), lambda h: (0, h))
    return pl.pallas_call(
        body, grid=(N_HEADS_A // 2,),
        in_specs=[pair, _pair_at(S, ATTN_W), _pair_at(S, 2 * ATTN_W), pair, pair], out_specs=[pair, pair],
        out_shape=[SDS((S, ATTN_W), bf16), SDS((S, ATTN_W), f32)],
        scratch_shapes=[pltpu.VMEM((2, 256, 256), f32)] + [pltpu.VMEM((S, 128), f32)] * 8,
        name=name, compiler_params=_params("parallel"))(proj, proj, proj, cos, sgn_sin)


def _attn_bwd(proj, cos, sgn_sin, dmix_in, out, lse, name):
    S = proj.shape[0]
    nblk = S // 128

    def body(qp_ref, kp_ref, v_ref, cos_ref, sin_ref, do_ref, out_ref, lse_ref, dq_ref, dk_ref, dv_ref,
             bias_ref, t_ref, q_ref, k_ref):
        head_a = lax.broadcasted_iota(jnp.int32, (1, 128), 1) < HEAD_DIM
        _band_bias(bias_ref)
        q_ref[...], k_ref[...] = _rope_fwd_fn(qp_ref[...], kp_ref[...], cos_ref[...], sin_ref[...])
        x = do_ref[...] * out_ref[...].astype(f32)
        t_ref[...] = jnp.where(head_a, jnp.sum(jnp.where(head_a, x, 0.0), axis=1, keepdims=True),
                               jnp.sum(jnp.where(head_a, 0.0, x), axis=1, keepdims=True))
        dq_ref[...] = jnp.zeros_like(dq_ref)
        dk_ref[...] = jnp.zeros_like(dk_ref)
        dv_ref[...] = jnp.zeros_like(dv_ref)
        for d, nb in BRANCHES:
            def blk(j, carry, d=d, nb=nb):
                cur, prev, first = _band_rows(j, d, nb)
                q2 = _stack_heads(q_ref[cur, :], head_a).astype(bf16)
                do2 = _stack_heads(do_ref[cur, :], head_a).astype(bf16)
                t, lse_b = t_ref[cur, :], lse_ref[cur, :]
                t2 = jnp.concatenate([t[:, :1], t[:, HEAD_DIM:HEAD_DIM + 1]], axis=0)
                lse2 = jnp.concatenate([lse_b[:, :1], lse_b[:, HEAD_DIM:HEAD_DIM + 1]], axis=0)
                if nb == 1:
                    k2, v2, bias = k_ref[cur, :].astype(bf16), v_ref[cur, :].astype(bf16), bias_ref[1][:, :128]
                else:
                    k2 = jnp.concatenate([k_ref[cur, :], k_ref[prev, :]], axis=0).astype(bf16)
                    v2 = jnp.concatenate([v_ref[cur, :], v_ref[prev, :]], axis=0).astype(bf16)
                    bias = bias_ref[first]
                p = jnp.exp(_nt(q2, k2) + bias - lse2)
                ds = (p * (_nt(do2, v2) - t2)).astype(bf16)
                dq_ref[cur, :] += _unstack_heads(jnp.dot(ds, k2, preferred_element_type=f32), head_a)
                dk2, dv2 = _tn(ds, q2), _tn(p.astype(bf16), do2)
                dk_ref[cur, :] += dk2[:128]
                dv_ref[cur, :] += dv2[:128]
                if nb != 1:
                    dk_ref[prev, :] += dk2[128:]
                    dv_ref[prev, :] += dv2[128:]
                return carry

            lax.fori_loop(0, nblk, blk, 0, unroll=16)
        dq_ref[...], dk_ref[...] = _rope_bwd_fn(dq_ref[...], dk_ref[...], cos_ref[...], sin_ref[...])

    pair = pl.BlockSpec((S, 128), lambda h: (0, h))
    return pl.pallas_call(
        body, grid=(N_HEADS_A // 2,),
        in_specs=[pair, _pair_at(S, ATTN_W), _pair_at(S, 2 * ATTN_W), pair, pair, pair, pair, pair],
        out_specs=[pair] * 3, out_shape=[SDS((S, ATTN_W), f32)] * 3,
        scratch_shapes=[pltpu.VMEM((2, 256, 256), f32)] + [pltpu.VMEM((S, 128), f32)] * 3,
        name=name, compiler_params=_params("parallel"))(proj, proj, proj, cos, sgn_sin, dmix_in, out, lse)


def _conv_val(x, w, K, rows):
    acc = x * w[K - 1:K, :]
    for s in range(1, K):
        acc = acc + jnp.where(rows >= s, pltpu.roll(x, s, 0), 0.0) * w[K - 1 - s:K - s, :]
    return acc


def _colconv_fwd(xs, ws, bs, K, fn, nblk, tc, outs, name):
    S = xs[0][0].shape[0]
    n = len(xs)
    has_b = bs is not None

    def body(*refs):
        rows = lax.broadcasted_iota(jnp.int32, (S, tc), 0)
        cs = []
        for k in range(n):
            c = _conv_val(refs[k][...].astype(f32), refs[n + k][...], K, rows)
            if has_b:
                c = c + refs[2 * n + k][...]
            cs.append(c)
        for o_ref, val in zip(refs[(3 if has_b else 2) * n:], fn(*cs)):
            o_ref[...] = val.astype(o_ref.dtype)

    def cspec(rows_, cb0):
        return pl.BlockSpec((rows_, tc), lambda j, cb0=cb0: (0, cb0 + j))

    in_specs = [cspec(S, cb) for _, cb in xs] + [cspec(K, cb) for _, cb in ws]
    args = [a for a, _ in xs] + [a for a, _ in ws]
    if has_b:
        in_specs += [cspec(1, cb) for _, cb in bs]
        args += [a for a, _ in bs]
    return pl.pallas_call(
        body, grid=(nblk,), in_specs=in_specs, out_specs=[cspec(S, 0) for _ in outs],
        out_shape=[SDS((S, nblk * tc), dt) for dt in outs], name=name, compiler_params=_params("parallel"))(*args)


def _colconv_bwd(xs, ws, bs, K, fn, douts, nblk, tc, name, dx_dtype=f32):
    S = xs[0][0].shape[0]
    n, nd = len(xs), len(douts)
    has_b = bs is not None
    nin = (3 if has_b else 2) * n

    def body(*refs):
        rows = lax.broadcasted_iota(jnp.int32, (S, tc), 0)
        x = [refs[k][...].astype(f32) for k in range(n)]
        w = [refs[n + k][...] for k in range(n)]
        cs = []
        for k in range(n):
            c = _conv_val(x[k], w[k], K, rows)
            if has_b:
                c = c + refs[2 * n + k][...]
            cs.append(c)
        _, vjp = jax.vjp(fn, *cs)
        dcs = vjp(tuple(r[...].astype(f32) for r in refs[nin:nin + nd]))
        o = refs[nin + nd:]
        for k in range(n):
            dc = dcs[k]
            dx = dc * w[k][K - 1:K, :]
            o[n + k][K - 1:K, :] = jnp.sum(dc * x[k], axis=0, keepdims=True)
            for s in range(1, K):
                dx = dx + jnp.where(rows < S - s, pltpu.roll(dc, S - s, 0), 0.0) * w[k][K - 1 - s:K - s, :]
                xsh = jnp.where(rows >= s, pltpu.roll(x[k], s, 0), 0.0)
                o[n + k][K - 1 - s:K - s, :] = jnp.sum(dc * xsh, axis=0, keepdims=True)
            o[k][...] = dx.astype(o[k].dtype)
            if has_b:
                o[2 * n + k][...] = jnp.sum(dc, axis=0, keepdims=True)

    def cspec(rows_, cb0):
        return pl.BlockSpec((rows_, tc), lambda j, cb0=cb0: (0, cb0 + j))

    in_specs = [cspec(S, cb) for _, cb in xs] + [cspec(K, cb) for _, cb in ws]
    args = [a for a, _ in xs] + [a for a, _ in ws]
    if has_b:
        in_specs += [cspec(1, cb) for _, cb in bs]
        args += [a for a, _ in bs]
    in_specs += [cspec(S, 0) for _ in douts]
    args += list(douts)
    W = nblk * tc
    out_specs = [cspec(S, 0)] * n + [cspec(K, 0)] * n + ([cspec(1, 0)] * n if has_b else [])
    out_shape = [SDS((S, W), dx_dtype)] * n + [SDS((K, W), f32)] * n + ([SDS((1, W), f32)] * n if has_b else [])
    res = pl.pallas_call(body, grid=(nblk,), in_specs=in_specs, out_specs=out_specs, out_shape=out_shape,
                         name=name, compiler_params=_params("parallel"))(*args)
    return res[:n], res[n:2 * n], res[2 * n:]


def _silu_fn(c):
    return (c * jax.nn.sigmoid(c),)


_GELU_C, _GELU_A = 0.7978845608028654, 0.044715


@jax.custom_vjp
def _geglu(gate, up):
    return 0.5 * gate * (1.0 + jnp.tanh(_GELU_C * (gate + _GELU_A * gate * gate * gate))) * up


def _geglu_vjp_fwd(gate, up):
    return _geglu(gate, up), (gate, up)


def _geglu_vjp_bwd(res, d):
    gate, up = res
    g2 = gate * gate
    t = jnp.tanh(_GELU_C * gate * (1.0 + _GELU_A * g2))
    h = 0.5 * (1.0 + t)
    dgelu = h + (0.5 * _GELU_C) * gate * (1.0 - t * t) * (1.0 + (3.0 * _GELU_A) * g2)
    return d * up * dgelu, d * (gate * h)


_geglu.defvjp(_geglu_vjp_fwd, _geglu_vjp_bwd)


def _geglu_fn(gate, up):
    return (_geglu(gate, up),)


def _softplus(x):
    u = jnp.exp(jnp.minimum(x, 20.0))
    small = u * (1.0 - 0.5 * u)
    return jnp.where(x > 20.0, x, jnp.where(u < 1e-4, small, jnp.log(1.0 + u)))


def _bmm(a, b, precision=None):
    return lax.dot_general(a, b, (((2,), (1,)), ((0,), (0,))), precision=precision, preferred_element_type=f32)


def _bnt(a, b, precision=None):
    return lax.dot_general(a, b, (((2,), (2,)), ((0,), (0,))), precision=precision, preferred_element_type=f32)


def _btn(a, b, precision=None):
    return lax.dot_general(a, b, (((1,), (1,)), ((0,), (0,))), precision=precision, preferred_element_type=f32)


@jax.custom_vjp
def _unit_lower_inverse(A):
    n = A.shape[-1]
    eye = (lax.broadcasted_iota(jnp.int32, (1, n, n), 1) == lax.broadcasted_iota(jnp.int32, (1, n, n), 2)).astype(f32)
    P = -A
    T = eye + P
    for _ in range(5):
        P = _bmm(P, P, HI)
        T = T + _bmm(T, P, HI)
    return T


def _unit_lower_inverse_fwd(A):
    T = _unit_lower_inverse(A)
    return T, T


def _unit_lower_inverse_bwd(T, dT):
    return (-_btn(T, _bnt(dT, T, HI), HI),)


_unit_lower_inverse.defvjp(_unit_lower_inverse_fwd, _unit_lower_inverse_bwd)


def _dn_prep_fn(q, k, v, ba, alog, dtb, h):
    G, C = q.shape[0], CHUNK
    lane = lax.broadcasted_iota(jnp.int32, (1, 1, 128), 2)

    def sel(arr, idx):
        return jnp.sum(jnp.where(lane == idx, arr, 0.0), axis=-1, keepdims=True)

    beta = jax.nn.sigmoid(sel(ba, h))
    g = -jnp.exp(sel(alog[None], h)) * _softplus(sel(ba, N_HEADS_D + h) + sel(dtb[None], h))
    qn = q * lax.rsqrt(jnp.sum(q * q, axis=-1, keepdims=True) + EPS) * (DK ** -0.5)
    kn = k * lax.rsqrt(jnp.sum(k * k, axis=-1, keepdims=True) + EPS)
    ii = lax.broadcasted_iota(jnp.int32, (1, C, C), 1)
    jj = lax.broadcasted_iota(jnp.int32, (1, C, C), 2)
    tril, strict = ii >= jj, ii > jj
    gsq = jnp.broadcast_to(g, (G, C, C))
    gcol = _bmm(jnp.broadcast_to(tril.astype(f32), (G, C, C)), gsq, HI)
    grow = _bmm(jnp.ones((G, C, C), f32), jnp.where(ii <= jj, gsq, 0.0), HI)
    decay = jnp.exp(jnp.where(tril, gcol - grow, NEG))
    gc = gcol[:, :, :1]
    glast = gcol[:, C - 1:C, :1]
    kb = kn * beta
    A = jnp.where(strict, _bnt(kb.astype(bf16), kn.astype(bf16)) * decay, 0.0)
    T = _unit_lower_inverse(A).astype(bf16)
    u = _bmm(T, (v * beta).astype(bf16))
    w = _bmm(T, (kb * jnp.exp(gc)).astype(bf16))
    qk = _bnt(qn.astype(bf16), kn.astype(bf16)) * decay
    qd = qn * jnp.exp(gc)
    kd = kn * jnp.exp(glast - gc)
    return u, w, qk, qd, kd, jnp.broadcast_to(jnp.exp(glast), (G, C, DK))


def _dn_scan_fn(u, w, qk, qd, kd, eg, St):
    b = lambda a: a.astype(bf16)
    vnew = u - _bmm(b(w), b(St))
    o = _bmm(b(qd), b(St)) + _bmm(b(qk), b(vnew))
    return o, St * eg[:, :1, :] + _btn(b(kd), b(vnew))


def _dn_post_fn(o, z, nw):
    return (_rms(o, nw) * (z * jax.nn.sigmoid(z)),)


DN_GROUP = 8


def _dn_prep_specs(S, rows):
    def col(first):
        return pl.BlockSpec((rows, DK), lambda i, h, first=first: (i, first // DK + h))

    par = pl.BlockSpec((1, 128), lambda i, h: (0, 0))
    return [col(0), col(N_HEADS_D * DK), col(2 * N_HEADS_D * DK),
            pl.BlockSpec((rows, 128), lambda i, h: (i, 3584 // 128)), par, par]


def _dn_prep(qkv, proj, alog, dtb, name):
    S = qkv.shape[0]
    G = DN_GROUP
    rows = G * CHUNK

    def body(q_ref, k_ref, v_ref, ba_ref, al_ref, dt_ref, u_ref, w_ref, qk_ref, qd_ref, kd_ref, eg_ref):
        h = pl.program_id(1)
        r3 = lambda ref: ref[...].reshape(G, CHUNK, 128)
        u, w, qk, qd, kd, eg = _dn_prep_fn(r3(q_ref), r3(k_ref), r3(v_ref), r3(ba_ref), al_ref[...], dt_ref[...], h)
        for ref, val in ((u_ref, u), (w_ref, w), (qd_ref, qd), (kd_ref, kd), (eg_ref, eg)):
            ref[...] = val.reshape(rows, DK)
        qk_ref[:, :CHUNK] = qk.reshape(rows, CHUNK)
        qk_ref[:, CHUNK:] = jnp.zeros((rows, DK - CHUNK), f32)

    out = pl.BlockSpec((rows, DK), lambda i, h: (i, h))
    return pl.pallas_call(
        body, grid=(S // rows, N_HEADS_D), in_specs=_dn_prep_specs(S, rows), out_specs=[out] * 6,
        out_shape=[SDS((S, N_HEADS_D * DK), f32)] * 6, name=name,
        compiler_params=_params("parallel", "parallel"))(qkv, qkv, qkv, proj, alog, dtb)


def _dn_prep_bwd(qkv, proj, alog, dtb, cts, name):
    S = qkv.shape[0]
    G = DN_GROUP
    rows = G * CHUNK

    def body(q_ref, k_ref, v_ref, ba_ref, al_ref, dt_ref, du_ref, dw_ref, dqk_ref, dqd_ref, dkd_ref, deg_ref,
             dq_ref, dk_ref, dv_ref, dba_ref, dal_ref, ddt_ref):
        i, h = pl.program_id(0), pl.program_id(1)
        r3 = lambda ref: ref[...].reshape(G, CHUNK, 128)
        _, vjp = jax.vjp(lambda q, k, v, ba, al, dt: _dn_prep_fn(q, k, v, ba, al, dt, h),
                         r3(q_ref), r3(k_ref), r3(v_ref), r3(ba_ref), al_ref[...], dt_ref[...])
        dqk = dqk_ref[:, :CHUNK].reshape(G, CHUNK, CHUNK)
        dq, dk, dv, dba, dal, ddt = vjp((r3(du_ref), r3(dw_ref), dqk, r3(dqd_ref), r3(dkd_ref), r3(deg_ref)))
        dq_ref[...] = dq.reshape(rows, DK)
        dk_ref[...] = dk.reshape(rows, DK)
        dv_ref[...] = dv.reshape(rows, DK)

        @pl.when(h == 0)
        def _():
            dba_ref[...] = jnp.zeros_like(dba_ref)

        @pl.when(jnp.logical_and(i == 0, h == 0))
        def _():
            dal_ref[...] = jnp.zeros_like(dal_ref)
            ddt_ref[...] = jnp.zeros_like(ddt_ref)

        dba_ref[...] += dba.reshape(rows, 128)
        dal_ref[...] += dal
        ddt_ref[...] += ddt

    hcol = pl.BlockSpec((rows, DK), lambda i, h: (i, h))
    par = pl.BlockSpec((1, 128), lambda i, h: (0, 0))
    W = N_HEADS_D * DK
    return pl.pallas_call(
        body, grid=(S // rows, N_HEADS_D), in_specs=_dn_prep_specs(S, rows) + [hcol] * 6,
        out_specs=[hcol] * 3 + [pl.BlockSpec((rows, 128), lambda i, h: (i, 0)), par, par],
        out_shape=[SDS((S, W), f32)] * 3 + [SDS((S, 128), f32), SDS((1, 128), f32), SDS((1, 128), f32)], name=name,
        compiler_params=_params("arbitrary", "arbitrary"))(qkv, qkv, qkv, proj, alog, dtb, *cts)


def _heads(x):
    return jnp.stack([x[:, DK * h:DK * (h + 1)] for h in range(N_HEADS_D)])


SCAN_CHUNKS = 8


def _dn_scan(pre, name):
    S = pre[0].shape[0]
    NCH = S // CHUNK
    rows = SCAN_CHUNKS * CHUNK

    def body(u_ref, w_ref, qk_ref, qd_ref, kd_ref, eg_ref, o_ref, st_ref, s_ref):
        @pl.when(pl.program_id(0) == 0)
        def _():
            s_ref[...] = jnp.zeros_like(s_ref)

        St = s_ref[...]
        for k in range(SCAN_CHUNKS):
            r = slice(k * CHUNK, (k + 1) * CHUNK)
            st_ref[k] = St
            o, St = _dn_scan_fn(_heads(u_ref[r, :]), _heads(w_ref[r, :]), _heads(qk_ref[r, :])[:, :, :CHUNK],
                                _heads(qd_ref[r, :]), _heads(kd_ref[r, :]), _heads(eg_ref[r, :]), St)
            for h in range(N_HEADS_D):
                o_ref[r, DK * h:DK * (h + 1)] = o[h]
        s_ref[...] = St

    blk = pl.BlockSpec((rows, N_HEADS_D * DK), lambda n: (n, 0))
    return pl.pallas_call(
        body, grid=(S // rows,), in_specs=[blk] * 6,
        out_specs=[blk, pl.BlockSpec((SCAN_CHUNKS, N_HEADS_D, DK, DK), lambda n: (n, 0, 0, 0))],
        out_shape=[SDS((S, N_HEADS_D * DK), f32), SDS((NCH, N_HEADS_D, DK, DK), f32)],
        scratch_shapes=[pltpu.VMEM((N_HEADS_D, DK, DK), f32)], name=name, compiler_params=_params("arbitrary"))(*pre)


def _dn_scan_bwd(pre, states, do, name):
    S = do.shape[0]
    rows = SCAN_CHUNKS * CHUNK
    steps = S // rows

    def body(u_ref, w_ref, qk_ref, qd_ref, kd_ref, eg_ref, st_ref, do_ref,
             du_ref, dw_ref, dqk_ref, dqd_ref, dkd_ref, deg_ref, ds_ref):
        @pl.when(pl.program_id(0) == 0)
        def _():
            ds_ref[...] = jnp.zeros_like(ds_ref)

        dS = ds_ref[...]
        for k in reversed(range(SCAN_CHUNKS)):
            r = slice(k * CHUNK, (k + 1) * CHUNK)
            _, vjp = jax.vjp(_dn_scan_fn, _heads(u_ref[r, :]), _heads(w_ref[r, :]), _heads(qk_ref[r, :])[:, :, :CHUNK],
                             _heads(qd_ref[r, :]), _heads(kd_ref[r, :]), _heads(eg_ref[r, :]), st_ref[k])
            du, dw, dqk, dqd, dkd, deg, dS = vjp((_heads(do_ref[r, :]), dS))
            for h in range(N_HEADS_D):
                c = slice(DK * h, DK * (h + 1))
                for ref, val in ((du_ref, du), (dw_ref, dw), (dqd_ref, dqd), (dkd_ref, dkd), (deg_ref, deg)):
                    ref[r, c] = val[h]
                dqk_ref[r, DK * h:DK * h + CHUNK] = dqk[h]
                dqk_ref[r, DK * h + CHUNK:DK * (h + 1)] = jnp.zeros((CHUNK, DK - CHUNK), f32)
        ds_ref[...] = dS

    blk = pl.BlockSpec((rows, N_HEADS_D * DK), lambda n: (steps - 1 - n, 0))
    return pl.pallas_call(
        body, grid=(steps,),
        in_specs=[blk] * 6 + [pl.BlockSpec((SCAN_CHUNKS, N_HEADS_D, DK, DK), lambda n: (steps - 1 - n, 0, 0, 0)), blk],
        out_specs=[blk] * 6, out_shape=[SDS((S, N_HEADS_D * DK), f32)] * 6,
        scratch_shapes=[pltpu.VMEM((N_HEADS_D, DK, DK), f32)], name=name,
        compiler_params=_params("arbitrary"))(*pre, states, do)


def _loss_head(y, t, name):
    S, D = y.shape
    tm = ROW_TILE

    def body(y_ref, t_ref, dy_ref, l_ref):
        i = pl.program_id(0)
        d = y_ref[...] - t_ref[...]
        dy_ref[...] = d * (1.0 / D)
        part = jnp.sum(jnp.sum(d * d, axis=1, keepdims=True), axis=0, keepdims=True) * (0.5 / D)

        @pl.when(i == 0)
        def _():
            l_ref[...] = jnp.zeros_like(l_ref)

        l_ref[...] += jnp.broadcast_to(part, l_ref.shape)

    spec = pl.BlockSpec((tm, D), lambda i: (i, 0))
    dy, l = pl.pallas_call(body, grid=(S // tm,), in_specs=[spec, spec],
                           out_specs=[spec, pl.BlockSpec((1, 128), lambda i: (0, 0))],
                           out_shape=[SDS((S, D), f32), SDS((1, 128), f32)], name=name,
                           compiler_params=_params("arbitrary"))(y, t)
    return l[0, 0], dy


def _adamw_refs(w_ref, g_ref, m_ref, v_ref, d_ref, mo_ref, vo_ref):
    gv = g_ref[...]
    m2 = ADAM_B1 * m_ref[...] + (1.0 - ADAM_B1) * gv
    v2 = ADAM_B2 * v_ref[...] + (1.0 - ADAM_B2) * (gv * gv)
    m_hat = m2 / (1.0 - ADAM_B1 ** ADAM_STEP)
    v_hat = v2 / (1.0 - ADAM_B2 ** ADAM_STEP)
    d_ref[...] = -ADAM_LR * (m_hat / (jnp.sqrt(v_hat) + ADAM_EPS) + ADAM_WD * w_ref[...])
    mo_ref[...] = m2
    vo_ref[...] = v2


def _adamw_small(ws, gs, ms, vs, name):
    n = len(ws)

    def body(*refs):
        for i in range(n):
            _adamw_refs(*[refs[k * n + i] for k in range(7)])

    res = pl.pallas_call(body, out_shape=[SDS(a.shape, f32) for a in ws] * 3, name=name)(*ws, *gs, *ms, *vs)
    return res[:n], res[n:2 * n], res[2 * n:]


def _adamw(w, g, m, v, tr, name):
    L, R, C = w.shape
    assert R % tr == 0

    def body(*refs):
        _adamw_refs(*refs)

    spec = pl.BlockSpec((1, tr, C), lambda l, i: (l, i, 0))
    return pl.pallas_call(body, grid=(L, R // tr), in_specs=[spec] * 4, out_specs=[spec] * 3,
                          out_shape=[SDS((L, R, C), f32)] * 3, name=name,
                          compiler_params=_params("parallel", "parallel"))(w, g, m, v)


def _rope_tables(S):
    inv = 1.0 / (10000.0 ** (jnp.arange(0, HEAD_DIM, 2, dtype=f32) / HEAD_DIM))
    ang = jnp.arange(S, dtype=f32)[:, None] * inv[None, :]
    cos, sin = jnp.cos(ang), jnp.sin(ang)
    return (jnp.tile(jnp.concatenate([cos, cos], axis=1), (1, N_HEADS_A)),
            jnp.tile(jnp.concatenate([-sin, sin], axis=1), (1, N_HEADS_A)))


def _layer_fwd(x, W, cos, sgn_sin, l, late_weights=None, h1=None, next_pre_mix=None):
    n = f"l{l}_"
    if h1 is None:
        (h1,) = _rows(_rms_fn, [x], [W["norm_pre_mix"]], [(D_MODEL, bf16)], n + "pre_mix_norm")
    proj = _mm(h1, W["w_in"], "nn", 1024, 768, f32, n + "in_proj")
    attn_out, lse = _attn_fwd(proj, cos, sgn_sin, n + "attn_fwd")
    (qkv,) = _colconv_fwd([(proj, 3)], [(W["dn_conv_w"], 0)], None, 4, _silu_fn, 3, 512, [f32], n + "dn_conv")
    dn_pre = _dn_prep(qkv, proj, W["dn_a_log"], W["dn_dt_bias"], n + "dn_prep")
    dn_o, dn_states = _dn_scan(dn_pre, n + "dn_scan")
    (dn_out,) = _rows(_dn_post_fn, [(dn_o, DK, 0), (proj, DK, 3072 // DK)], [W["dn_norm_w"]], [(DK, bf16)], n + "dn_post",
                      ncol=N_HEADS_D, tm=4 * ROW_TILE)
    mix_in = jnp.concatenate([attn_out, dn_out], axis=1)
    late = late_weights(mix_in) if late_weights is not None else {}
    W = {**W, **late}
    mix = _mm(mix_in, W["w_out"], "nn", 512, 512, f32, n + "out_proj")
    x1, h2 = _rows(_res_rms_pre_fn, [mix, x], [W["norm_post_mix"], W["norm_pre_ffn"]],
                   [(D_MODEL, f32), (D_MODEL, bf16)], n + "post_mix_pre_ffn_norm")
    u0 = _mm(h2, W["ffn_w_in"], "nn", 1024, D_FF // 2, bf16, n + "ffn_in")
    nb_ff = D_FF // 256
    (act,) = _colconv_fwd([(u0, 0), (u0, nb_ff)], [(W["ffn_conv_w"], 0), (W["ffn_conv_w"], nb_ff)],
                          [(W["ffn_conv_b"], 0), (W["ffn_conv_b"], nb_ff)], 3, _geglu_fn, nb_ff, 256, [bf16],
                          n + "ffn_conv_glu")
    f = _mm(act, W["ffn_w_out"], "nn", 512, 512, f32, n + "ffn_out")
    if next_pre_mix is None:
        (x2,), h1_next = _rows(_res_rms_fn, [f, x1], [W["norm_post_ffn"]], [(D_MODEL, f32)], n + "post_ffn_norm"), None
    else:
        x2, h1_next = _rows(_res_rms_pre_fn, [f, x1], [W["norm_post_ffn"], next_pre_mix],
                            [(D_MODEL, f32), (D_MODEL, bf16)], n + "post_ffn_next_pre_mix_norm")
    saved = dict(x=x, h1=h1, proj=proj, attn_out=attn_out, lse=lse, qkv=qkv, dn_pre=dn_pre, dn_o=dn_o,
                 dn_states=dn_states, mix_in=mix_in, mix=mix, x1=x1, h2=h2, u0=u0, act=act, f=f, late=late)
    return x2, h1_next, saved


def _layer_bwd(dx2, sv, W, cos, sgn_sin, l, after_ffn=None, next_layer=None, first_layer=True):
    n = f"l{l}_"
    S = dx2.shape[0]
    g = {}
    g_next_pre = None
    if next_layer is None:
        (df,), (g["norm_post_ffn"],) = _rows_vjp(_rms_fn, [sv["f"]], [W["norm_post_ffn"]], [dx2], [0], [0],
                                                 n + "post_ffn_norm_bwd", row_dtype=bf16)
    else:
        (df, dx2), (g["norm_post_ffn"], g_next_pre) = _rows_vjp(
            _res_rms_pre_fn, [sv["f"], sv["x1"]], [W["norm_post_ffn"], next_layer[1]], [dx2, next_layer[0]], [0, 1], [0, 1],
            n + "post_ffn_next_pre_mix_norm_bwd", row_dtype=[bf16, f32])
    dact = _mm(df, W["ffn_w_out"], "nt", 512, 1408, f32, n + "ffn_out_dx")
    g["ffn_w_out"] = _mm(sv["act"], df, "tn", 256, 1024, f32, n + "ffn_out_dw")
    nb_ff = D_FF // 256
    u0 = sv["u0"]
    dxs, dws, dbs = _colconv_bwd([(u0, 0), (u0, nb_ff)], [(W["ffn_conv_w"], 0), (W["ffn_conv_w"], nb_ff)],
                                 [(W["ffn_conv_b"], 0), (W["ffn_conv_b"], nb_ff)], 3, _geglu_fn, [dact], nb_ff, 256,
                                 n + "ffn_conv_glu_bwd", dx_dtype=bf16)
    du0 = jnp.concatenate(dxs, axis=1)
    g["ffn_conv_w"] = jnp.concatenate(dws, axis=1)
    g["ffn_conv_b"] = jnp.concatenate(dbs, axis=1)
    dh2 = _mm(du0, W["ffn_w_in"], "nt", 512, 512, f32, n + "ffn_in_dx")
    g["ffn_w_in"] = _mm(sv["h2"], du0, "tn", 512, D_FF // 2, f32, n + "ffn_in_dw", column_shards=True)
    if after_ffn is not None:
        W = dict(W, norm_post_mix=W["norm_post_mix"] + after_ffn(g, dh2))
    (dmix, dx1), (g["norm_post_mix"], g["norm_pre_ffn"]) = _rows_vjp(
        _res_rms_pre_fn, [sv["mix"], sv["x"]], [W["norm_post_mix"], W["norm_pre_ffn"]], [dx2, dh2], [0, 1], [0, 1],
        n + "post_mix_pre_ffn_norm_bwd", row_dtype=[bf16, f32])
    dmix_in = _mm(dmix, W["w_out"], "nt", 512, 512, f32, n + "out_proj_dx")
    g["w_out"] = _mm(sv["mix_in"], dmix, "tn", 512, 512, f32, n + "out_proj_dw")

    (ddn_o, dz), (g["dn_norm_w"],) = _rows_vjp(
        _dn_post_fn, [(sv["dn_o"], DK, 0), (sv["proj"], DK, 3072 // DK)], [W["dn_norm_w"]], [(dmix_in, DK, ATTN_W // DK)],
        [0, 1], [0], n + "dn_post_bwd", ncol=N_HEADS_D, tm=4 * ROW_TILE)
    dpre = _dn_scan_bwd(sv["dn_pre"], sv["dn_states"], ddn_o, n + "dn_scan_bwd")
    dq, dk, dv, dba, g["dn_a_log"], g["dn_dt_bias"] = _dn_prep_bwd(
        sv["qkv"], sv["proj"], W["dn_a_log"], W["dn_dt_bias"], dpre, n + "dn_prep_bwd")
    dqkv = jnp.concatenate([dq, dk, dv], axis=1)
    (dqkv0,), (g["dn_conv_w"],), _ = _colconv_bwd([(sv["proj"], 3)], [(W["dn_conv_w"], 0)], None, 4, _silu_fn,
                                                 [dqkv], 3, 512, n + "dn_conv_bwd")

    daq, dak, dav = _attn_bwd(sv["proj"], cos, sgn_sin, dmix_in, sv["attn_out"], sv["lse"], n + "attn_bwd")
    dproj = jnp.concatenate([daq, dak, dav, dqkv0, dz, dba, jnp.zeros((S, PROJ_W - 3712), f32)], axis=1).astype(bf16)
    dh1 = _mm(dproj, W["w_in"], "nt", 512, 512, f32, n + "in_proj_dx")
    g["w_in"] = _mm(sv["h1"], dproj, "tn", 512, 768, f32, n + "in_proj_dw")
    if not first_layer:
        return (dx1, dh1), g, g_next_pre
    (dx,), (g["norm_pre_mix"],) = _rows_vjp(_rms_fn, [sv["x"]], [W["norm_pre_mix"]], [dh1], [0], [0],
                                            n + "pre_mix_norm_bwd", adds={0: dx1})
    return dx, g, g_next_pre


def _local_step(x, target, layers):
    cos, sgn_sin = _rope_tables(x.shape[0])
    saved, h1 = [], None
    for l, W in enumerate(layers):
        nxt = layers[l + 1]["norm_pre_mix"] if l + 1 < len(layers) else None
        x, h1, sv = _layer_fwd(x, W, cos, sgn_sin, l, h1=h1, next_pre_mix=nxt)
        saved.append(sv)
    loss, dx = _loss_head(x, target, "loss_head")
    grads = [None] * len(layers)
    nxt = None
    for l in reversed(range(len(layers))):
        dx, grads[l], g_pre = _layer_bwd(dx, saved[l], layers[l], cos, sgn_sin, l, next_layer=nxt, first_layer=(l == 0))
        if g_pre is not None:
            grads[l + 1]["norm_pre_mix"] = g_pre
        if l > 0:
            dx, dh1 = dx
            nxt = (dh1, layers[l]["norm_pre_mix"])
    return loss, dx, grads


def _pos():
    x, y, c = lax.axis_index("x"), lax.axis_index("y"), lax.axis_index("c")
    return x, y, c, [(1 - x, y), (x, 1 - y), (1 - x, 1 - y)]


def _rcopy(src, dst, send_sem, recv_sem, dev):
    return pltpu.make_async_remote_copy(src_ref=src, dst_ref=dst, send_sem=send_sem, recv_sem=recv_sem,
                                        device_id=dev, device_id_type=MESH)


def _half_rows(ref, h, which, axis):
    if h is None:
        return ref
    rows = pl.ds(pl.multiple_of(which * h, 16), h)
    return ref.at[:, rows, :] if axis == 1 else ref.at[rows, :]


def _dma_sems(*counts):
    return [pltpu.SemaphoreType.DMA((k,)) for k in counts]


def _all_gather(arrs, halves, name):
    n = len(arrs)

    def body(*refs):
        ins, outs = refs[:n], refs[n:2 * n]
        send1, recv1, send2, recv2 = refs[2 * n:]
        x, y, c, chips = _pos()
        me, sib, s_me = (x, y, c), (x, y, 1 - c), 2 * x + y
        sends = []
        for i in range(n):
            for j, chip in enumerate(chips):
                cp = _rcopy(_half_rows(ins[i], halves[i], c, 1), _half_rows(outs[i].at[s_me], halves[i], c, 1),
                            send1.at[3 * i + j], recv1.at[3 * i + j], (*chip, c))
                cp.start()
                sends.append(cp)
        for i in range(n):
            for j, (px, py) in enumerate(chips):
                k = 3 * i + j
                landed = _half_rows(outs[i].at[2 * px + py], halves[i], c, 1)
                _rcopy(landed, landed, send1.at[k], recv1.at[k], me).wait_recv()
                if halves[i] is not None:
                    cp = _rcopy(landed, landed, send2.at[k], recv2.at[k], sib)
                    cp.start()
                    sends.append(cp)
        for i in range(n):
            if halves[i] is None:
                continue
            for j, (px, py) in enumerate(chips):
                k = 3 * i + j
                other = _half_rows(outs[i].at[2 * px + py], halves[i], 1 - c, 1)
                _rcopy(other, other, send2.at[k], recv2.at[k], me).wait_recv()
        for cp in sends:
            cp.wait_send()

    return pl.pallas_call(
        body, in_specs=[ANY] * n, out_specs=[ANY] * n,
        out_shape=[SDS((4,) + a.shape, a.dtype) for a in arrs],
        scratch_shapes=_dma_sems(3 * n, 3 * n, 3 * n, 3 * n), name=name)(*arrs)


HBM = pl.BlockSpec(memory_space=pltpu.HBM)
SEM = pl.BlockSpec(memory_space=pltpu.SEMAPHORE)
_EFFECT = pltpu.SideEffectType.DATAFLOW_SIDE_EFFECTING


def _in_hbm(a):
    return pltpu.with_memory_space_constraint(a, pltpu.HBM)


def _split_copy(srcs, land_shapes, plan, per, after, name):
    n = len(srcs)
    k = per * n

    def body(*refs):
        ins, lands, token = refs[:n], refs[n:2 * n], refs[-1]
        send, recv = refs[2 * n + 1], refs[2 * n + 2]
        for i, (src, dst, dev, _) in enumerate(plan(ins, lands)):
            _rcopy(src, dst, send.at[i], recv.at[i], dev).start()
        token[...] = jnp.zeros_like(token)

    lands = [_in_hbm(lax.empty(s.shape, s.dtype)) for s in land_shapes]
    return pl.pallas_call(
        body, name=name,
        out_shape=(pltpu.SemaphoreType.DMA((k,)), pltpu.SemaphoreType.DMA((k,)),
                   *[pltpu.HBM(a.shape, a.dtype) for a in srcs], *[pltpu.HBM(s.shape, s.dtype) for s in land_shapes],
                   SDS((8, 128), f32)),
        in_specs=[HBM] * (2 * n) + [ANY], out_specs=(SEM, SEM, *[HBM] * (2 * n), pl.BlockSpec(memory_space=pltpu.VMEM)),
        input_output_aliases={i: 2 + i for i in range(2 * n)},
        compiler_params=pltpu.CompilerParams(has_side_effects=_EFFECT))(*[_in_hbm(a) for a in srcs], *lands, after)


def _split_wait(started, n, plan, after, name):
    send, recv = started[0], started[1]
    thru = started[2:2 + 2 * n]

    def body(*refs):
        ins, lands = refs[:n], refs[n:2 * n]
        send_ref, recv_ref = refs[2 * n], refs[2 * n + 1]
        for i, (src, _, dev, mine) in enumerate(plan(ins, lands)):
            cp = _rcopy(src, mine, send_ref.at[i], recv_ref.at[i], dev)
            cp.wait_send()
            cp.wait_recv()

    res = pl.pallas_call(
        body, name=name, out_shape=tuple(pltpu.HBM(a.shape, a.dtype) for a in thru),
        in_specs=[HBM] * (2 * n) + [SEM, SEM, ANY], out_specs=tuple([HBM] * (2 * n)),
        input_output_aliases={i: i for i in range(2 * n)},
        compiler_params=pltpu.CompilerParams(has_side_effects=_EFFECT))(*thru, send, recv, after)
    return res[:n], res[n:]


def _gather_plan(halves):
    def plan(ins, lands):
        x, y, c, chips = _pos()
        out = []
        for i in range(len(ins)):
            for px, py in chips:
                out.append((_half_rows(ins[i], halves[i], c, 1), _half_rows(lands[i].at[2 * x + y], halves[i], c, 1),
                            (px, py, c), _half_rows(lands[i].at[2 * px + py], halves[i], c, 1)))
        return out
    return plan


def _scatter_plan(ins, lands):
    x, y, c, chips = _pos()
    out = []
    for i in range(len(ins)):
        for j, (px, py) in enumerate(chips):
            out.append((ins[i].at[2 * px + py], lands[i].at[j], (px, py, c), lands[i].at[j]))
    return out


def _exchange_plan(ins, lands):
    x, y, c, _ = _pos()
    return [(_half_rows(g, g.shape[1] // 2, 1 - c, 1), land, (x, y, 1 - c), land) for g, land in zip(ins, lands)]


def _pass_to_sibling(lands, halves, name):
    n = len(lands)

    def body(*refs):
        outs = refs[n:2 * n]
        send, recv = refs[2 * n:]
        x, y, c, chips = _pos()
        sends = []
        for i in range(n):
            for j, (px, py) in enumerate(chips):
                landed = _half_rows(outs[i].at[2 * px + py], halves[i], c, 1)
                cp = _rcopy(landed, landed, send.at[3 * i + j], recv.at[3 * i + j], (x, y, 1 - c))
                cp.start()
                sends.append(cp)
        for i in range(n):
            for j, (px, py) in enumerate(chips):
                other = _half_rows(outs[i].at[2 * px + py], halves[i], 1 - c, 1)
                _rcopy(other, other, send.at[3 * i + j], recv.at[3 * i + j], (x, y, c)).wait_recv()
        for cp in sends:
            cp.wait_send()

    return pl.pallas_call(
        body, in_specs=[ANY] * n, out_specs=[ANY] * n, out_shape=[SDS(a.shape, a.dtype) for a in lands],
        input_output_aliases={k: k for k in range(n)}, scratch_shapes=_dma_sems(3 * n, 3 * n), name=name)(*lands)


def _exchange_halves(gs, name):
    n = len(gs)

    def body(*refs):
        ins, outs = refs[:n], refs[n:2 * n]
        send, recv = refs[2 * n:]
        x, y, c, _ = _pos()
        sends = []
        for k in range(n):
            cp = _rcopy(_half_rows(ins[k], gs[k].shape[1] // 2, 1 - c, 1), outs[k], send.at[k], recv.at[k], (x, y, 1 - c))
            cp.start()
            sends.append(cp)
        for k in range(n):
            _rcopy(outs[k], outs[k], send.at[k], recv.at[k], (x, y, c)).wait_recv()
        for cp in sends:
            cp.wait_send()

    return pl.pallas_call(
        body, in_specs=[ANY] * n, out_specs=[ANY] * n,
        out_shape=[SDS((4, g.shape[1] // 2, g.shape[2]), g.dtype) for g in gs],
        scratch_shapes=_dma_sems(n, n), name=name)(*gs)


def _scatter_partials(ps, name):
    n = len(ps)

    def body(*refs):
        ins, outs = refs[:n], refs[n:2 * n]
        send, recv = refs[2 * n:]
        x, y, c, chips = _pos()
        sends = []
        for k in range(n):
            for j, (px, py) in enumerate(chips):
                cp = _rcopy(ins[k].at[2 * px + py], outs[k].at[j], send.at[3 * k + j], recv.at[3 * k + j], (px, py, c))
                cp.start()
                sends.append(cp)
        for k in range(n):
            for j in range(3):
                _rcopy(outs[k].at[j], outs[k].at[j], send.at[3 * k + j], recv.at[3 * k + j], (x, y, c)).wait_recv()
        for cp in sends:
            cp.wait_send()

    return pl.pallas_call(
        body, in_specs=[ANY] * n, out_specs=[ANY] * n,
        out_shape=[SDS((3,) + p.shape[1:], p.dtype) for p in ps],
        scratch_shapes=_dma_sems(3 * n, 3 * n), name=name)(*ps)


def _join_halves(rs, layers, name):
    n = len(rs)

    def body(*refs):
        outs = refs[n:2 * n]
        send, recv = refs[2 * n:]
        x, y, c, _ = _pos()

        def half(k, which):
            h = rs[k].shape[1] // 2
            return _half_rows(outs[k], h, which, 1) if layers[k] is None else _half_rows(outs[k].at[layers[k]], h, which, 0)

        sends = []
        for k in range(n):
            cp = _rcopy(half(k, c), half(k, c), send.at[k], recv.at[k], (x, y, 1 - c))
            cp.start()
            sends.append(cp)
        for k in range(n):
            _rcopy(half(k, 1 - c), half(k, 1 - c), send.at[k], recv.at[k], (x, y, c)).wait_recv()
        for cp in sends:
            cp.wait_send()

    return pl.pallas_call(
        body, in_specs=[ANY] * n, out_specs=[ANY] * n, out_shape=[SDS(r.shape, r.dtype) for r in rs],
        input_output_aliases={k: k for k in range(n)}, scratch_shapes=_dma_sems(n, n), name=name)(*rs)


def _all_reduce_small(pack, name):
    R = pack.shape[0]

    def body(in_ref, out_ref, buf, send, recv):
        x, y, c, _ = _pos()
        me = 4 * x + 2 * y + c
        buf[me] = in_ref[...]
        sends = []
        for k in range(1, 8):
            peer = me ^ k
            cp = _rcopy(buf.at[me], buf.at[me], send.at[k - 1], recv.at[k - 1], ((peer >> 2) & 1, (peer >> 1) & 1, peer & 1))
            cp.start()
            sends.append(cp)
        for k in range(1, 8):
            _rcopy(buf.at[me ^ k], buf.at[me ^ k], send.at[k - 1], recv.at[k - 1], (x, y, c)).wait_recv()
        for cp in sends:
            cp.wait_send()
        acc = buf[0]
        for d in range(1, 8):
            acc = acc + buf[d]
        out_ref[...] = acc

    return pl.pallas_call(
        body, out_shape=SDS((R, 128), f32),
        in_specs=[pl.BlockSpec(memory_space=pltpu.VMEM)], out_specs=pl.BlockSpec(memory_space=pltpu.VMEM),
        scratch_shapes=[pltpu.VMEM((8, R, 128), f32)] + _dma_sems(7, 7), name=name)(pack)


def _add_sibling(g, recv, c_arr, tr, name):
    _, R, C = g.shape
    h = R // 2
    nrb = h // tr
    assert h % tr == 0

    def body(c_ref, g_ref, r_ref, o_ref):
        o_ref[...] = (g_ref[...] + r_ref[...]).astype(o_ref.dtype)

    spec = pl.BlockSpec((1, tr, C), lambda s, r, c_ref: (s, r, 0))
    grid_spec = pltpu.PrefetchScalarGridSpec(
        num_scalar_prefetch=1, grid=(4, nrb),
        in_specs=[pl.BlockSpec((1, tr, C), lambda s, r, c_ref: (s, c_ref[0] * nrb + r, 0)), spec], out_specs=spec)
    return pl.pallas_call(body, grid_spec=grid_spec, out_shape=SDS((4, h, C), bf16), name=name,
                          compiler_params=_params("parallel", "parallel"))(c_arr, g, recv)


def _add_chips(p, recv, sc_arr, tr, layer, into, name):
    _, h, C = p.shape
    nrb = h // tr
    assert h % tr == 0

    def body(sc_ref, p_ref, r_ref, *rest):
        rest[-1][...] = (p_ref[0].astype(f32) + r_ref[0].astype(f32)) + (r_ref[1].astype(f32) + r_ref[2].astype(f32))

    grid_spec = pltpu.PrefetchScalarGridSpec(
        num_scalar_prefetch=1, grid=(nrb,),
        in_specs=[pl.BlockSpec((1, tr, C), lambda r, sc_ref: (sc_ref[0], r, 0)),
                  pl.BlockSpec((3, tr, C), lambda r, sc_ref: (0, r, 0))] + ([] if into is None else [ANY]),
        out_specs=pl.BlockSpec((None, tr, C), lambda r, sc_ref: (layer, sc_ref[1] * nrb + r, 0)))
    return pl.pallas_call(body, grid_spec=grid_spec, out_shape=SDS((2, 2 * h, C), f32), name=name,
                          input_output_aliases={} if into is None else {3: 0},
                          compiler_params=_params("parallel"))(sc_arr, p, recv, *([] if into is None else [into]))


_BIG = (("w_in", 1024, 256), ("w_out", 256, 128), ("ffn_w_in", 1024, 256), ("ffn_w_out", 704, 352))
_SMALL = ("dn_conv_w", "ffn_conv_w", "ffn_conv_b", "norm_pre_mix", "norm_post_mix", "norm_pre_ffn", "norm_post_ffn",
          "dn_norm_w", "dn_a_log", "dn_dt_bias")
_WEIGHTS = ("w_in", "dn_conv_w", "dn_a_log", "dn_dt_bias", "dn_norm_w", "w_out", "ffn_w_in", "ffn_conv_w", "ffn_conv_b",
            "ffn_w_out", "norm_pre_mix", "norm_post_mix", "norm_pre_ffn", "norm_post_ffn")
_ADAM_ROWS = {"w_in": 256, "w_out": 256, "ffn_w_in": 128, "ffn_w_out": 176}


def _shard_major(name, g):
    if name == "w_in":
        return jnp.stack([g[:, 898 * s:898 * (s + 1)] for s in range(4)])
    if name == "ffn_w_in":
        return g
    return g.reshape(4, g.shape[0] // 4, g.shape[1])


def kernel(x, w_in, dn_conv_w, dn_a_log, dn_dt_bias, dn_norm_w, w_out, ffn_w_in, ffn_conv_w, ffn_conv_b, ffn_w_out, norm_pre_mix, norm_post_mix, norm_pre_ffn, norm_post_ffn, loss_target, m_w_in, m_dn_conv_w, m_dn_a_log, m_dn_dt_bias, m_dn_norm_w, m_w_out, m_ffn_w_in, m_ffn_conv_w, m_ffn_conv_b, m_ffn_w_out, m_norm_pre_mix, m_norm_post_mix, m_norm_pre_ffn, m_norm_post_ffn, v_w_in, v_dn_conv_w, v_dn_a_log, v_dn_dt_bias, v_dn_norm_w, v_w_out, v_ffn_w_in, v_ffn_conv_w, v_ffn_conv_b, v_ffn_w_out, v_norm_pre_mix, v_norm_post_mix, v_norm_pre_ffn, v_norm_post_ffn):
    w = dict(w_in=w_in, dn_conv_w=dn_conv_w, dn_a_log=dn_a_log, dn_dt_bias=dn_dt_bias, dn_norm_w=dn_norm_w, w_out=w_out,
             ffn_w_in=ffn_w_in, ffn_conv_w=ffn_conv_w, ffn_conv_b=ffn_conv_b, ffn_w_out=ffn_w_out, norm_pre_mix=norm_pre_mix,
             norm_post_mix=norm_post_mix, norm_pre_ffn=norm_pre_ffn, norm_post_ffn=norm_post_ffn)
    m = dict(w_in=m_w_in, dn_conv_w=m_dn_conv_w, dn_a_log=m_dn_a_log, dn_dt_bias=m_dn_dt_bias, dn_norm_w=m_dn_norm_w,
             w_out=m_w_out, ffn_w_in=m_ffn_w_in, ffn_conv_w=m_ffn_conv_w, ffn_conv_b=m_ffn_conv_b, ffn_w_out=m_ffn_w_out,
             norm_pre_mix=m_norm_pre_mix, norm_post_mix=m_norm_post_mix, norm_pre_ffn=m_norm_pre_ffn,
             norm_post_ffn=m_norm_post_ffn)
    v = dict(w_in=v_w_in, dn_conv_w=v_dn_conv_w, dn_a_log=v_dn_a_log, dn_dt_bias=v_dn_dt_bias, dn_norm_w=v_dn_norm_w,
             w_out=v_w_out, ffn_w_in=v_ffn_w_in, ffn_conv_w=v_ffn_conv_w, ffn_conv_b=v_ffn_conv_b, ffn_w_out=v_ffn_w_out,
             norm_pre_mix=v_norm_pre_mix, norm_post_mix=v_norm_post_mix, norm_pre_ffn=v_norm_pre_ffn,
             norm_post_ffn=v_norm_post_ffn)
    xi, yi, ci = lax.axis_index("x"), lax.axis_index("y"), lax.axis_index("c")
    s_me = 2 * xi + yi
    c_arr = jnp.reshape(ci, (1,)).astype(jnp.int32)
    sc_arr = jnp.stack([s_me, ci]).astype(jnp.int32)

    mats = [name for name, _, _ in _BIG]
    rest = mats[1:]
    half_of = {name: rows // 2 for name, rows, _ in _BIG}
    tiles = {name: tr for name, _, tr in _BIG}
    gathered_shape = lambda a: SDS((4,) + a.shape, a.dtype)

    own = {k: w[k].astype(bf16) for k in mats}
    plan_in = _gather_plan([half_of["w_in"]])
    src_in = [own["w_in"][0:1]]
    started_in = _split_copy(src_in, [gathered_shape(a) for a in src_in], plan_in, 3, src_in[0], "weights_gather_w_in0_start")
    got_taps = _all_gather([dn_conv_w + started_in[-1][0, 0], ffn_conv_w], [None, None], "conv_taps_gather")
    plan0 = _gather_plan([half_of[k] for k in rest])
    src0 = [own[k][0:1] for k in rest]
    started0 = _split_copy(src0, [gathered_shape(a) for a in src0], plan0, 3, got_taps[0], "weights_gather_l0_start")
    plan1 = _gather_plan([half_of[k] for k in mats])
    src1 = [own[k][1:2] for k in mats]
    started1 = _split_copy(src1, [gathered_shape(a) for a in src1], plan1, 3, started0[-1], "weights_gather_l1_start")
    _, landed_in = _split_wait(started_in, 1, plan_in, started1[-1], "weights_gather_w_in0_wait")
    got_in = list(_pass_to_sibling(landed_in, [half_of["w_in"]], "weights_gather_w_in0_sibling")) + list(got_taps)

    def pick(mine, gathered):
        return [jnp.where(s_me == s, mine, gathered[s]) for s in range(4)]

    conv = {"dn_conv_w": jnp.concatenate(pick(dn_conv_w, got_in[1]), axis=-1),
            "ffn_conv_w": jnp.concatenate(pick(ffn_conv_w, got_in[2]), axis=-1)}
    lanes = lambda a: jnp.pad(a, ((0, 0), (0, 128 - a.shape[1])))
    vec = dict(dn_a_log=lanes(dn_a_log), dn_dt_bias=lanes(dn_dt_bias), dn_norm_w=dn_norm_w, ffn_conv_b=ffn_conv_b,
               norm_pre_mix=norm_pre_mix, norm_post_mix=norm_post_mix, norm_pre_ffn=norm_pre_ffn, norm_post_ffn=norm_post_ffn)

    def matrices(l, names, gathered):
        W = {}
        for k, a in zip(names, gathered):
            if k in ("w_out", "ffn_w_out"):
                rows_, cols = own[k].shape[1:]
                W[k] = lax.dynamic_update_slice(a[:, 0], own[k][l][None], (s_me, 0, 0)).reshape(4 * rows_, cols)
            else:
                cat = jnp.concatenate(pick(own[k][l], a[:, 0]), axis=-1)
                W[k] = jnp.pad(cat, ((0, 0), (0, PROJ_W - IN_COLS))) if k == "w_in" else cat
        return W

    def small_weights(l):
        return {**{k: a[l] for k, a in conv.items()}, **{k: a[l:l + 1] for k, a in vec.items()}}

    def late_l0(mix_in):
        _, landed = _split_wait(started0, len(rest), plan0, mix_in, "weights_gather_l0_wait")
        return matrices(0, rest, _pass_to_sibling(landed, [half_of[k] for k in rest], "weights_gather_l0_sibling"))

    cos, sgn_sin = _rope_tables(x.shape[1])
    W0 = {**small_weights(0), **matrices(0, ["w_in"], got_in[:1])}
    W0_first = dict(W0, norm_pre_mix=W0["norm_pre_mix"] + started1[-1][0, 0])
    x1, h1_l1, saved0 = _layer_fwd(x[0], W0_first, cos, sgn_sin, 0, late_weights=late_l0,
                                   next_pre_mix=norm_pre_mix[1:2])
    _, landed1 = _split_wait(started1, len(mats), plan1, x1, "weights_gather_l1_wait")
    W1 = {**small_weights(1),
          **matrices(1, mats, _pass_to_sibling(landed1, [half_of[k] for k in mats], "weights_gather_l1_sibling"))}
    x2, _, saved1 = _layer_fwd(x1, W1, cos, sgn_sin, 1, h1=h1_l1)
    loss_local, dy = _loss_head(x2, loss_target[0], "loss_head")
    loss = lax.psum(loss_local, ("x", "y", "c"))

    def shard_major(names, grads_l):
        return [_shard_major(name, grads_l[name]) for name in names]

    def add_siblings(l, names, gs, from_sib):
        return [_add_sibling(g, r, c_arr, tiles[name], f"add_sibling_{name}{l}") for g, r, name in zip(gs, from_sib, names)]

    def scatter_start(l, names, parts, after, tag):
        return _split_copy(parts, [SDS((3,) + p.shape[1:], p.dtype) for p in parts], _scatter_plan, 3, after,
                           f"grads_l{l}{tag}_scatter_start")

    def owner_sums(l, names, sent, after, tag, into):
        parts, recvd = _split_wait(sent, len(names), _scatter_plan, after, f"grads_l{l}{tag}_scatter_wait")
        return {name: _add_chips(p, r, sc_arr, tiles[name], l, into.get(name), f"add_chips_{name}{l}")
                for p, r, name in zip(parts, recvd, names)}

    (dx1, dh1_l1), grads1, _ = _layer_bwd(dy, saved1, W1, cos, sgn_sin, 1, first_layer=False)
    gs1 = shard_major(mats, grads1)
    swap1 = _split_copy(gs1, [SDS((4, g.shape[1] // 2, g.shape[2]), g.dtype) for g in gs1], _exchange_plan, 1, dx1,
                        "grads_l1_sibling_start")
    ffn = ["ffn_w_in", "ffn_w_out"]
    launched = {}

    def after_ffn_l0(g_ffn, dx_mid):
        gs1_, from_sib1 = _split_wait(swap1, len(mats), _exchange_plan, dx_mid, "grads_l1_sibling_wait")
        launched["l1"] = scatter_start(1, mats, add_siblings(1, mats, gs1_, from_sib1), dx_mid, "")
        gs0 = shard_major(ffn, g_ffn)
        from_sib0 = _exchange_halves(gs0, "grads_l0_ffn_to_sibling")
        launched["l0_ffn"] = scatter_start(0, ffn, add_siblings(0, ffn, gs0, from_sib0), launched["l1"][-1], "_ffn")
        return launched["l0_ffn"][-1][0, 0]

    W0_last = dict(W0, **saved0["late"], norm_post_ffn=W0["norm_post_ffn"] + swap1[-1][0, 0])
    dx, grads0, grads1["norm_pre_mix"] = _layer_bwd(dx1, saved0, W0_last, cos, sgn_sin, 0, after_ffn=after_ffn_l0,
                                                    next_layer=(dh1_l1, norm_pre_mix[1:2]))
    mix = ["w_in", "w_out"]
    gs0 = shard_major(mix, grads0)
    part0 = add_siblings(0, mix, gs0, _exchange_halves(gs0, "grads_l0_mix_to_sibling"))
    sent0 = scatter_start(0, mix, part0, dx, "_mix")
    red = owner_sums(0, ffn, launched["l0_ffn"], sent0[-1], "_ffn", {})
    red = owner_sums(1, mats, launched["l1"], sent0[-1], "", red)
    joined = dict(zip(mats, _join_halves([red[k] for k in mats], [1 if k in mix else None for k in mats],
                                         "grads_join_early")))
    grads = [grads0, grads1]

    small = {}
    for name in _SMALL:
        per_layer = [grads[l][name] for l in range(2)]
        if name in ("dn_a_log", "dn_dt_bias"):
            per_layer = [p[:, :N_HEADS_D] for p in per_layer]
        small[name] = jnp.stack(per_layer).reshape((2,) + (w[name].shape[1:] if name not in ("dn_conv_w", "ffn_conv_w")
                                                           else per_layer[0].shape))
    flat = jnp.concatenate([small[name].reshape(-1) for name in _SMALL])
    n_rows = -(-flat.shape[0] // 1024) * 8
    summed = _all_reduce_small(jnp.pad(flat, (0, n_rows * 128 - flat.shape[0])).reshape(n_rows, 128),
                               "small_grads_all_reduce").reshape(-1)
    off = 0
    g_out = {}
    for name in _SMALL:
        size = small[name].size
        g_out[name] = summed[off:off + size].reshape(small[name].shape)
        off += size
    g_out["dn_conv_w"] = lax.dynamic_slice_in_dim(g_out["dn_conv_w"], s_me * 384, 384, axis=2)
    g_out["ffn_conv_w"] = lax.dynamic_slice_in_dim(g_out["ffn_conv_w"], s_me * 1408, 1408, axis=2)
    for k in ffn:
        g_out[k] = joined[k]

    deltas, new_m, new_v = {}, {}, {}

    def step(name):
        shape = w[name].shape
        as3 = (lambda a: a) if len(shape) == 3 else (lambda a: a.reshape(shape[0], 1, shape[1]))
        tr = _ADAM_ROWS.get(name, as3(w[name]).shape[1])
        d_, m_, v_ = _adamw(as3(w[name]), as3(g_out[name]), as3(m[name]), as3(v[name]), tr, f"adamw_{name}")
        deltas[name], new_m[name], new_v[name] = d_.reshape(shape), m_.reshape(shape), v_.reshape(shape)

    for name in ffn:
        step(name)
    tiny = [name for name in _WEIGHTS if name not in mats]
    stepped = _adamw_small(*[[d[name] for name in tiny] for d in (w, g_out, m, v)], "adamw_small")
    for out, vals in zip((deltas, new_m, new_v), stepped):
        out.update(zip(tiny, vals))
    done = jnp.reshape(deltas["ffn_w_in"][0, 0, 0] + deltas["ffn_w_out"][0, 0, 0] + deltas["norm_post_ffn"][0, 0], (1,))
    red = owner_sums(0, mix, sent0, done, "_mix", joined)
    for k, a in zip(mix, _join_halves([red[k] for k in mix], [0] * len(mix), "grads_join_late")):
        g_out[k] = a
        step(k)

    return (loss, dx[None], *[g_out[k] for k in _WEIGHTS], *[deltas[k] for k in _WEIGHTS],
            *[new_m[k] for k in _WEIGHTS], *[new_v[k] for k in _WEIGHTS])
```

```python
import jax
import jax.numpy as jnp
from jax import lax
from jax.experimental import pallas as pl
from jax.experimental.pallas import tpu as pltpu

f32, bf16 = jnp.float32, jnp.bfloat16
SDS = jax.ShapeDtypeStruct
HI = lax.Precision.HIGH
MESH = pl.DeviceIdType.MESH
ANY = pl.BlockSpec(memory_space=pl.ANY)

D_MODEL = 1024
N_HEADS_A, HEAD_DIM = 8, 64
ATTN_W = 512
N_HEADS_D, DK = 4, 128
CHUNK = 64
D_FF = 2816
IN_COLS = 3592
PROJ_W = 3840
BRANCHES = ((1, 16), (4, 4), (16, 1))
EPS = 1e-6
NEG = -1e30
ROW_TILE = 256
VMEM_LIMIT = 56 * 1024 * 1024

ADAM_LR, ADAM_B1, ADAM_B2, ADAM_EPS, ADAM_WD, ADAM_STEP = 0.001, 0.9, 0.999, 1e-08, 0.01, 10


def _params(*sem):
    return pltpu.CompilerParams(dimension_semantics=sem, vmem_limit_bytes=VMEM_LIMIT)


def _mm(a, b, mode, tm, tn, out_dtype, name, column_shards=False):
    if mode == "nn":
        (M, K), N = a.shape, b.shape[1]
        dims = (((1,), (0,)), ((), ()))
        a_spec = pl.BlockSpec((tm, K), lambda i, j: (i, 0))
        b_spec = pl.BlockSpec((K, tn), lambda i, j: (0, j))
    elif mode == "nt":
        (M, K), N = a.shape, b.shape[0]
        dims = (((1,), (1,)), ((), ()))
        a_spec = pl.BlockSpec((tm, K), lambda i, j: (i, 0))
        b_spec = pl.BlockSpec((tn, K), lambda i, j: (j, 0))
    else:
        (K, M), N = a.shape, b.shape[1]
        dims = (((0,), (0,)), ((), ()))
        a_spec = pl.BlockSpec((K, tm), lambda i, j: (0, i))
        b_spec = pl.BlockSpec((K, tn), lambda i, j: (0, j))
    assert M % tm == 0 and N % tn == 0, (name, M, N, tm, tn)

    def body(a_ref, b_ref, o_ref):
        o_ref[...] = lax.dot_general(a_ref[...].astype(bf16), b_ref[...].astype(bf16), dims,
                                     preferred_element_type=f32).astype(o_ref.dtype)

    if column_shards:
        out_spec, out_shape = pl.BlockSpec((None, tm, tn), lambda i, j: (j, i, 0)), SDS((N // tn, M, tn), out_dtype)
    else:
        out_spec, out_shape = pl.BlockSpec((tm, tn), lambda i, j: (i, j)), SDS((M, N), out_dtype)
    return pl.pallas_call(body, grid=(M // tm, N // tn), in_specs=[a_spec, b_spec], out_specs=out_spec,
                          out_shape=out_shape, name=name, compiler_params=_params("parallel", "arbitrary"))(a, b)


def _row_spec(r, tm):
    if isinstance(r, tuple):
        arr, width, cb = r
        return arr, pl.BlockSpec((tm, width), lambda i, j, cb=cb: (i, cb + j))
    return r, pl.BlockSpec((tm, r.shape[1]), lambda i, j: (i, j))


def _full_spec(p):
    return pl.BlockSpec(p.shape, lambda i, j: (0,) * p.ndim)


def _rows(fn, rows, params, outs, name, tm=ROW_TILE, ncol=1):
    arrs, specs = zip(*[_row_spec(r, tm) for r in rows])
    S = arrs[0].shape[0]
    nr, npar = len(rows), len(params)

    def body(*refs):
        vals = fn(*[r[...].astype(f32) for r in refs[:nr]], *[p[...] for p in refs[nr:nr + npar]])
        for o_ref, v in zip(refs[nr + npar:], vals):
            o_ref[...] = v.astype(o_ref.dtype)

    return pl.pallas_call(
        body, grid=(S // tm, ncol), in_specs=list(specs) + [_full_spec(p) for p in params],
        out_specs=[pl.BlockSpec((tm, w), lambda i, j: (i, j)) for w, _ in outs],
        out_shape=[SDS((S, w * ncol), dt) for w, dt in outs], name=name,
        compiler_params=_params("parallel", "parallel"))(*arrs, *params)


def _rows_vjp(fn, rows, params, cts, wrt_rows, wrt_params, name, adds=None, tm=ROW_TILE, ncol=1, row_dtype=f32):
    adds = adds or {}
    arrs, specs = zip(*[_row_spec(r, tm) for r in rows])
    carrs, cspecs = zip(*[_row_spec(c, tm) for c in cts])
    add_keys = sorted(adds)
    aarrs = [adds[k] for k in add_keys]
    S = arrs[0].shape[0]
    nr, npar, nc, na = len(rows), len(params), len(cts), len(aarrs)
    widths = [specs[k].block_shape[1] for k in wrt_rows]
    row_dtypes = row_dtype if isinstance(row_dtype, (list, tuple)) else [row_dtype] * len(wrt_rows)

    def body(*refs):
        first = jnp.logical_and(pl.program_id(0) == 0, pl.program_id(1) == 0)
        rv = [r[...].astype(f32) for r in refs[:nr]]
        pv = [p[...] for p in refs[nr:nr + npar]]
        cv = tuple(c[...].astype(f32) for c in refs[nr + npar:nr + npar + nc])
        av = dict(zip(add_keys, refs[nr + npar + nc:nr + npar + nc + na]))
        o = refs[nr + npar + nc + na:]
        _, vjp = jax.vjp(fn, *rv, *pv)
        g = vjp(cv)
        for n, k in enumerate(wrt_rows):
            val = g[k]
            if k in av:
                val = val + av[k][...]
            o[n][...] = val.astype(o[n].dtype)
        for n, k in enumerate(wrt_params):
            ref = o[len(wrt_rows) + n]

            @pl.when(first)
            def _(ref=ref):
                ref[...] = jnp.zeros_like(ref)

            ref[...] += g[nr + k]

    res = pl.pallas_call(
        body, grid=(S // tm, ncol),
        in_specs=list(specs) + [_full_spec(p) for p in params] + list(cspecs)
        + [pl.BlockSpec((tm, a.shape[1] // ncol), lambda i, j: (i, j)) for a in aarrs],
        out_specs=[pl.BlockSpec((tm, w), lambda i, j: (i, j)) for w in widths] + [_full_spec(params[k]) for k in wrt_params],
        out_shape=[SDS((S, w * ncol), dt) for w, dt in zip(widths, row_dtypes)]
        + [SDS(params[k].shape, f32) for k in wrt_params],
        name=name, compiler_params=_params("arbitrary", "arbitrary"))(*arrs, *params, *carrs, *aarrs)
    return res[:len(wrt_rows)], res[len(wrt_rows):]


def _rms(x, w):
    return x * lax.rsqrt(jnp.mean(x * x, axis=-1, keepdims=True) + EPS) * w


def _rms_fn(x, w):
    return (_rms(x, w),)


def _res_rms_fn(f, res, w):
    return (res + _rms(f, w),)


def _res_rms_pre_fn(f, res, w_post, w_pre):
    x1 = res + _rms(f, w_post)
    return x1, _rms(x1, w_pre)


def _swap_halves(x):
    lane = lax.broadcasted_iota(jnp.int32, x.shape, 1)
    first = (lane % HEAD_DIM) < (HEAD_DIM // 2)
    n = x.shape[1]
    return jnp.where(first, pltpu.roll(x, n - HEAD_DIM // 2, 1), pltpu.roll(x, HEAD_DIM // 2, 1))


def _rope_fwd_fn(q, k, cos, sgn_sin):
    scale = HEAD_DIM ** -0.5
    return ((q * cos + _swap_halves(q) * sgn_sin) * scale, k * cos + _swap_halves(k) * sgn_sin)


def _rope_bwd_fn(dq, dk, cos, sgn_sin):
    dq = dq * (HEAD_DIM ** -0.5)
    return (dq * cos + _swap_halves(dq * sgn_sin), dk * cos + _swap_halves(dk * sgn_sin))


def _nt(a, b):
    return lax.dot_general(a, b, (((1,), (1,)), ((), ())), preferred_element_type=f32)


def _tn(a, b):
    return lax.dot_general(a, b, (((0,), (0,)), ((), ())), preferred_element_type=f32)


def _band_rows(j, d, nb):
    r, i = j // nb, j % nb
    if d == 1:
        cur = pl.ds(pl.multiple_of(i * 128, 128), 128)
        prev = pl.ds(pl.multiple_of(jnp.maximum(i - 1, 0) * 128, 128), 128)
    else:
        cur = pl.ds(i * (128 * d) + r, 128, stride=d)
        prev = pl.ds(jnp.maximum(i - 1, 0) * (128 * d) + r, 128, stride=d)
    return cur, prev, (i == 0).astype(jnp.int32)


def _band_bias(bias_ref):
    a = lax.broadcasted_iota(jnp.int32, (256, 256), 0) % 128
    c = lax.broadcasted_iota(jnp.int32, (256, 256), 1)
    own = jnp.logical_and(c < 128, c <= a)
    before = jnp.logical_and(c >= 128, c - 128 >= a)
    bias_ref[0] = jnp.where(jnp.logical_or(own, before), 0.0, NEG)
    bias_ref[1] = jnp.where(own, 0.0, NEG)


def _stack_heads(x, head_a):
    return jnp.concatenate([jnp.where(head_a, x, 0.0), jnp.where(head_a, 0.0, x)], axis=0)


def _unstack_heads(x2, head_a):
    return jnp.where(head_a, x2[:128], x2[128:])


def _pair_at(S, first_col):
    return pl.BlockSpec((S, 128), lambda h: (0, first_col // 128 + h))


def _attn_fwd(proj, cos, sgn_sin, name):
    S = proj.shape[0]
    nblk = S // 128

    def body(qp_ref, kp_ref, v_ref, cos_ref, sin_ref, out_ref, lse_ref, bias_ref, q_ref, k_ref, *scr):
        head_a = lax.broadcasted_iota(jnp.int32, (1, 128), 1) < HEAD_DIM
        _band_bias(bias_ref)
        q_ref[...], k_ref[...] = _rope_fwd_fn(qp_ref[...], kp_ref[...], cos_ref[...], sin_ref[...])
        for b, (d, nb) in enumerate(BRANCHES):
            ob_ref, lb_ref = scr[2 * b], scr[2 * b + 1]

            def blk(j, carry, d=d, nb=nb, ob_ref=ob_ref, lb_ref=lb_ref):
                cur, prev, first = _band_rows(j, d, nb)
                q2 = _stack_heads(q_ref[cur, :], head_a).astype(bf16)
                if nb == 1:
                    k2, v2, bias = k_ref[cur, :].astype(bf16), v_ref[cur, :].astype(bf16), bias_ref[1][:, :128]
                else:
                    k2 = jnp.concatenate([k_ref[cur, :], k_ref[prev, :]], axis=0).astype(bf16)
                    v2 = jnp.concatenate([v_ref[cur, :], v_ref[prev, :]], axis=0).astype(bf16)
                    bias = bias_ref[first]
                s = _nt(q2, k2) + bias
                mx = jnp.max(s, axis=1, keepdims=True)
                p = jnp.exp(s - mx)
                l = jnp.sum(p, axis=1, keepdims=True)
                o = jnp.dot(p.astype(bf16), v2, preferred_element_type=f32) / l
                ob_ref[cur, :] = _unstack_heads(o, head_a)
                lb_ref[cur, :] = _unstack_heads(jnp.broadcast_to(mx + jnp.log(l), (256, 128)), head_a)
                return carry

            lax.fori_loop(0, nblk, blk, 0, unroll=16)
        l0, l1, l2 = scr[1][...], scr[3][...], scr[5][...]
        mx = jnp.maximum(jnp.maximum(l0, l1), l2)
        e0, e1, e2 = jnp.exp(l0 - mx), jnp.exp(l1 - mx), jnp.exp(l2 - mx)
        den = e0 + e1 + e2
        out_ref[...] = ((e0 * scr[0][...] + e1 * scr[2][...] + e2 * scr[4][...]) / den).astype(out_ref.dtype)
        lse_ref[...] = mx + jnp.log(den)

    pair = pl.BlockSpec((S, 128), lambda h: (0, h))
    return pl.pallas_call(
        body, grid=(N_HEADS_A // 2,),
        in_specs=[pair, _pair_at(S, ATTN_W), _pair_at(S, 2 * ATTN_W), pair, pair], out_specs=[pair, pair],
        out_shape=[SDS((S, ATTN_W), bf16), SDS((S, ATTN_W), f32)],
        scratch_shapes=[pltpu.VMEM((2, 256, 256), f32)] + [pltpu.VMEM((S, 128), f32)] * 8,
        name=name, compiler_params=_params("parallel"))(proj, proj, proj, cos, sgn_sin)


def _attn_bwd(proj, cos, sgn_sin, dmix_in, out, lse, name):
    S = proj.shape[0]
    nblk = S // 128

    def body(qp_ref, kp_ref, v_ref, cos_ref, sin_ref, do_ref, out_ref, lse_ref, dq_ref, dk_ref, dv_ref,
             bias_ref, t_ref, q_ref, k_ref):
        head_a = lax.broadcasted_iota(jnp.int32, (1, 128), 1) < HEAD_DIM
        _band_bias(bias_ref)
        q_ref[...], k_ref[...] = _rope_fwd_fn(qp_ref[...], kp_ref[...], cos_ref[...], sin_ref[...])
        x = do_ref[...] * out_ref[...].astype(f32)
        t_ref[...] = jnp.where(head_a, jnp.sum(jnp.where(head_a, x, 0.0), axis=1, keepdims=True),
                               jnp.sum(jnp.where(head_a, 0.0, x), axis=1, keepdims=True))
        dq_ref[...] = jnp.zeros_like(dq_ref)
        dk_ref[...] = jnp.zeros_like(dk_ref)
        dv_ref[...] = jnp.zeros_like(dv_ref)
        for d, nb in BRANCHES:
            def blk(j, carry, d=d, nb=nb):
                cur, prev, first = _band_rows(j, d, nb)
                q2 = _stack_heads(q_ref[cur, :], head_a).astype(bf16)
                do2 = _stack_heads(do_ref[cur, :], head_a).astype(bf16)
                t, lse_b = t_ref[cur, :], lse_ref[cur, :]
                t2 = jnp.concatenate([t[:, :1], t[:, HEAD_DIM:HEAD_DIM + 1]], axis=0)
                lse2 = jnp.concatenate([lse_b[:, :1], lse_b[:, HEAD_DIM:HEAD_DIM + 1]], axis=0)
                if nb == 1:
                    k2, v2, bias = k_ref[cur, :].astype(bf16), v_ref[cur, :].astype(bf16), bias_ref[1][:, :128]
                else:
                    k2 = jnp.concatenate([k_ref[cur, :], k_ref[prev, :]], axis=0).astype(bf16)
                    v2 = jnp.concatenate([v_ref[cur, :], v_ref[prev, :]], axis=0).astype(bf16)
                    bias = bias_ref[first]
                p = jnp.exp(_nt(q2, k2) + bias - lse2)
                ds = (p * (_nt(do2, v2) - t2)).astype(bf16)
                dq_ref[cur, :] += _unstack_heads(jnp.dot(ds, k2, preferred_element_type=f32), head_a)
                dk2, dv2 = _tn(ds, q2), _tn(p.astype(bf16), do2)
                dk_ref[cur, :] += dk2[:128]
                dv_ref[cur, :] += dv2[:128]
                if nb != 1:
                    dk_ref[prev, :] += dk2[128:]
                    dv_ref[prev, :] += dv2[128:]
                return carry

            lax.fori_loop(0, nblk, blk, 0, unroll=16)
        dq_ref[...], dk_ref[...] = _rope_bwd_fn(dq_ref[...], dk_ref[...], cos_ref[...], sin_ref[...])

    pair = pl.BlockSpec((S, 128), lambda h: (0, h))
    return pl.pallas_call(
        body, grid=(N_HEADS_A // 2,),
        in_specs=[pair, _pair_at(S, ATTN_W), _pair_at(S, 2 * ATTN_W), pair, pair, pair, pair, pair],
        out_specs=[pair] * 3, out_shape=[SDS((S, ATTN_W), f32)] * 3,
        scratch_shapes=[pltpu.VMEM((2, 256, 256), f32)] + [pltpu.VMEM((S, 128), f32)] * 3,
        name=name, compiler_params=_params("parallel"))(proj, proj, proj, cos, sgn_sin, dmix_in, out, lse)


def _conv_val(x, w, K, rows):
    acc = x * w[K - 1:K, :]
    for s in range(1, K):
        acc = acc + jnp.where(rows >= s, pltpu.roll(x, s, 0), 0.0) * w[K - 1 - s:K - s, :]
    return acc


def _colconv_fwd(xs, ws, bs, K, fn, nblk, tc, outs, name):
    S = xs[0][0].shape[0]
    n = len(xs)
    has_b = bs is not None

    def body(*refs):
        rows = lax.broadcasted_iota(jnp.int32, (S, tc), 0)
        cs = []
        for k in range(n):
            c = _conv_val(refs[k][...].astype(f32), refs[n + k][...], K, rows)
            if has_b:
                c = c + refs[2 * n + k][...]
            cs.append(c)
        for o_ref, val in zip(refs[(3 if has_b else 2) * n:], fn(*cs)):
            o_ref[...] = val.astype(o_ref.dtype)

    def cspec(rows_, cb0):
        return pl.BlockSpec((rows_, tc), lambda j, cb0=cb0: (0, cb0 + j))

    in_specs = [cspec(S, cb) for _, cb in xs] + [cspec(K, cb) for _, cb in ws]
    args = [a for a, _ in xs] + [a for a, _ in ws]
    if has_b:
        in_specs += [cspec(1, cb) for _, cb in bs]
        args += [a for a, _ in bs]
    return pl.pallas_call(
        body, grid=(nblk,), in_specs=in_specs, out_specs=[cspec(S, 0) for _ in outs],
        out_shape=[SDS((S, nblk * tc), dt) for dt in outs], name=name, compiler_params=_params("parallel"))(*args)


def _colconv_bwd(xs, ws, bs, K, fn, douts, nblk, tc, name, dx_dtype=f32):
    S = xs[0][0].shape[0]
    n, nd = len(xs), len(douts)
    has_b = bs is not None
    nin = (3 if has_b else 2) * n

    def body(*refs):
        rows = lax.broadcasted_iota(jnp.int32, (S, tc), 0)
        x = [refs[k][...].astype(f32) for k in range(n)]
        w = [refs[n + k][...] for k in range(n)]
        cs = []
        for k in range(n):
            c = _conv_val(x[k], w[k], K, rows)
            if has_b:
                c = c + refs[2 * n + k][...]
            cs.append(c)
        _, vjp = jax.vjp(fn, *cs)
        dcs = vjp(tuple(r[...].astype(f32) for r in refs[nin:nin + nd]))
        o = refs[nin + nd:]
        for k in range(n):
            dc = dcs[k]
            dx = dc * w[k][K - 1:K, :]
            o[n + k][K - 1:K, :] = jnp.sum(dc * x[k], axis=0, keepdims=True)
            for s in range(1, K):
                dx = dx + jnp.where(rows < S - s, pltpu.roll(dc, S - s, 0), 0.0) * w[k][K - 1 - s:K - s, :]
                xsh = jnp.where(rows >= s, pltpu.roll(x[k], s, 0), 0.0)
                o[n + k][K - 1 - s:K - s, :] = jnp.sum(dc * xsh, axis=0, keepdims=True)
            o[k][...] = dx.astype(o[k].dtype)
            if has_b:
                o[2 * n + k][...] = jnp.sum(dc, axis=0, keepdims=True)

    def cspec(rows_, cb0):
        return pl.BlockSpec((rows_, tc), lambda j, cb0=cb0: (0, cb0 + j))

    in_specs = [cspec(S, cb) for _, cb in xs] + [cspec(K, cb) for _, cb in ws]
    args = [a for a, _ in xs] + [a for a, _ in ws]
    if has_b:
        in_specs += [cspec(1, cb) for _, cb in bs]
        args += [a for a, _ in bs]
    in_specs += [cspec(S, 0) for _ in douts]
    args += list(douts)
    W = nblk * tc
    out_specs = [cspec(S, 0)] * n + [cspec(K, 0)] * n + ([cspec(1, 0)] * n if has_b else [])
    out_shape = [SDS((S, W), dx_dtype)] * n + [SDS((K, W), f32)] * n + ([SDS((1, W), f32)] * n if has_b else [])
    res = pl.pallas_call(body, grid=(nblk,), in_specs=in_specs, out_specs=out_specs, out_shape=out_shape,
                         name=name, compiler_params=_params("parallel"))(*args)
    return res[:n], res[n:2 * n], res[2 * n:]


def _silu_fn(c):
    return (c * jax.nn.sigmoid(c),)


_GELU_C, _GELU_A = 0.7978845608028654, 0.044715


@jax.custom_vjp
def _geglu(gate, up):
    return 0.5 * gate * (1.0 + jnp.tanh(_GELU_C * (gate + _GELU_A * gate * gate * gate))) * up


def _geglu_vjp_fwd(gate, up):
    return _geglu(gate, up), (gate, up)


def _geglu_vjp_bwd(res, d):
    gate, up = res
    g2 = gate * gate
    t = jnp.tanh(_GELU_C * gate * (1.0 + _GELU_A * g2))
    h = 0.5 * (1.0 + t)
    dgelu = h + (0.5 * _GELU_C) * gate * (1.0 - t * t) * (1.0 + (3.0 * _GELU_A) * g2)
    return d * up * dgelu, d * (gate * h)


_geglu.defvjp(_geglu_vjp_fwd, _geglu_vjp_bwd)


def _geglu_fn(gate, up):
    return (_geglu(gate, up),)


def _softplus(x):
    u = jnp.exp(jnp.minimum(x, 20.0))
    small = u * (1.0 - 0.5 * u)
    return jnp.where(x > 20.0, x, jnp.where(u < 1e-4, small, jnp.log(1.0 + u)))


def _bmm(a, b, precision=None):
    return lax.dot_general(a, b, (((2,), (1,)), ((0,), (0,))), precision=precision, preferred_element_type=f32)


def _bnt(a, b, precision=None):
    return lax.dot_general(a, b, (((2,), (2,)), ((0,), (0,))), precision=precision, preferred_element_type=f32)


def _btn(a, b, precision=None):
    return lax.dot_general(a, b, (((1,), (1,)), ((0,), (0,))), precision=precision, preferred_element_type=f32)


@jax.custom_vjp
def _unit_lower_inverse(A):
    n = A.shape[-1]
    eye = (lax.broadcasted_iota(jnp.int32, (1, n, n), 1) == lax.broadcasted_iota(jnp.int32, (1, n, n), 2)).astype(f32)
    P = -A
    T = eye + P
    for _ in range(5):
        P = _bmm(P, P, HI)
        T = T + _bmm(T, P, HI)
    return T


def _unit_lower_inverse_fwd(A):
    T = _unit_lower_inverse(A)
    return T, T


def _unit_lower_inverse_bwd(T, dT):
    return (-_btn(T, _bnt(dT, T, HI), HI),)


_unit_lower_inverse.defvjp(_unit_lower_inverse_fwd, _unit_lower_inverse_bwd)


def _dn_prep_fn(q, k, v, ba, alog, dtb, h):
    G, C = q.shape[0], CHUNK
    lane = lax.broadcasted_iota(jnp.int32, (1, 1, 128), 2)

    def sel(arr, idx):
        return jnp.sum(jnp.where(lane == idx, arr, 0.0), axis=-1, keepdims=True)

    beta = jax.nn.sigmoid(sel(ba, h))
    g = -jnp.exp(sel(alog[None], h)) * _softplus(sel(ba, N_HEADS_D + h) + sel(dtb[None], h))
    qn = q * lax.rsqrt(jnp.sum(q * q, axis=-1, keepdims=True) + EPS) * (DK ** -0.5)
    kn = k * lax.rsqrt(jnp.sum(k * k, axis=-1, keepdims=True) + EPS)
    ii = lax.broadcasted_iota(jnp.int32, (1, C, C), 1)
    jj = lax.broadcasted_iota(jnp.int32, (1, C, C), 2)
    tril, strict = ii >= jj, ii > jj
    gsq = jnp.broadcast_to(g, (G, C, C))
    gcol = _bmm(jnp.broadcast_to(tril.astype(f32), (G, C, C)), gsq, HI)
    grow = _bmm(jnp.ones((G, C, C), f32), jnp.where(ii <= jj, gsq, 0.0), HI)
    decay = jnp.exp(jnp.where(tril, gcol - grow, NEG))
    gc = gcol[:, :, :1]
    glast = gcol[:, C - 1:C, :1]
    kb = kn * beta
    A = jnp.where(strict, _bnt(kb.astype(bf16), kn.astype(bf16)) * decay, 0.0)
    T = _unit_lower_inverse(A).astype(bf16)
    u = _bmm(T, (v * beta).astype(bf16))
    w = _bmm(T, (kb * jnp.exp(gc)).astype(bf16))
    qk = _bnt(qn.astype(bf16), kn.astype(bf16)) * decay
    qd = qn * jnp.exp(gc)
    kd = kn * jnp.exp(glast - gc)
    return u, w, qk, qd, kd, jnp.broadcast_to(jnp.exp(glast), (G, C, DK))


def _dn_scan_fn(u, w, qk, qd, kd, eg, St):
    b = lambda a: a.astype(bf16)
    vnew = u - _bmm(b(w), b(St))
    o = _bmm(b(qd), b(St)) + _bmm(b(qk), b(vnew))
    return o, St * eg[:, :1, :] + _btn(b(kd), b(vnew))


def _dn_post_fn(o, z, nw):
    return (_rms(o, nw) * (z * jax.nn.sigmoid(z)),)


DN_GROUP = 8


def _dn_prep_specs(S, rows):
    def col(first):
        return pl.BlockSpec((rows, DK), lambda i, h, first=first: (i, first // DK + h))

    par = pl.BlockSpec((1, 128), lambda i, h: (0, 0))
    return [col(0), col(N_HEADS_D * DK), col(2 * N_HEADS_D * DK),
            pl.BlockSpec((rows, 128), lambda i, h: (i, 3584 // 128)), par, par]


def _dn_prep(qkv, proj, alog, dtb, name):
    S = qkv.shape[0]
    G = DN_GROUP
    rows = G * CHUNK

    def body(q_ref, k_ref, v_ref, ba_ref, al_ref, dt_ref, u_ref, w_ref, qk_ref, qd_ref, kd_ref, eg_ref):
        h = pl.program_id(1)
        r3 = lambda ref: ref[...].reshape(G, CHUNK, 128)
        u, w, qk, qd, kd, eg = _dn_prep_fn(r3(q_ref), r3(k_ref), r3(v_ref), r3(ba_ref), al_ref[...], dt_ref[...], h)
        for ref, val in ((u_ref, u), (w_ref, w), (qd_ref, qd), (kd_ref, kd), (eg_ref, eg)):
            ref[...] = val.reshape(rows, DK)
        qk_ref[:, :CHUNK] = qk.reshape(rows, CHUNK)
        qk_ref[:, CHUNK:] = jnp.zeros((rows, DK - CHUNK), f32)

    out = pl.BlockSpec((rows, DK), lambda i, h: (i, h))
    return pl.pallas_call(
        body, grid=(S // rows, N_HEADS_D), in_specs=_dn_prep_specs(S, rows), out_specs=[out] * 6,
        out_shape=[SDS((S, N_HEADS_D * DK), f32)] * 6, name=name,
        compiler_params=_params("parallel", "parallel"))(qkv, qkv, qkv, proj, alog, dtb)


def _dn_prep_bwd(qkv, proj, alog, dtb, cts, name):
    S = qkv.shape[0]
    G = DN_GROUP
    rows = G * CHUNK

    def body(q_ref, k_ref, v_ref, ba_ref, al_ref, dt_ref, du_ref, dw_ref, dqk_ref, dqd_ref, dkd_ref, deg_ref,
             dq_ref, dk_ref, dv_ref, dba_ref, dal_ref, ddt_ref):
        i, h = pl.program_id(0), pl.program_id(1)
        r3 = lambda ref: ref[...].reshape(G, CHUNK, 128)
        _, vjp = jax.vjp(lambda q, k, v, ba, al, dt: _dn_prep_fn(q, k, v, ba, al, dt, h),
                         r3(q_ref), r3(k_ref), r3(v_ref), r3(ba_ref), al_ref[...], dt_ref[...])
        dqk = dqk_ref[:, :CHUNK].reshape(G, CHUNK, CHUNK)
        dq, dk, dv, dba, dal, ddt = vjp((r3(du_ref), r3(dw_ref), dqk, r3(dqd_ref), r3(dkd_ref), r3(deg_ref)))
        dq_ref[...] = dq.reshape(rows, DK)
        dk_ref[...] = dk.reshape(rows, DK)
        dv_ref[...] = dv.reshape(rows, DK)

        @pl.when(h == 0)
        def _():
            dba_ref[...] = jnp.zeros_like(dba_ref)

        @pl.when(jnp.logical_and(i == 0, h == 0))
        def _():
            dal_ref[...] = jnp.zeros_like(dal_ref)
            ddt_ref[...] = jnp.zeros_like(ddt_ref)

        dba_ref[...] += dba.reshape(rows, 128)
        dal_ref[...] += dal
        ddt_ref[...] += ddt

    hcol = pl.BlockSpec((rows, DK), lambda i, h: (i, h))
    par = pl.BlockSpec((1, 128), lambda i, h: (0, 0))
    W = N_HEADS_D * DK
    return pl.pallas_call(
        body, grid=(S // rows, N_HEADS_D), in_specs=_dn_prep_specs(S, rows) + [hcol] * 6,
        out_specs=[hcol] * 3 + [pl.BlockSpec((rows, 128), lambda i, h: (i, 0)), par, par],
        out_shape=[SDS((S, W), f32)] * 3 + [SDS((S, 128), f32), SDS((1, 128), f32), SDS((1, 128), f32)], name=name,
        compiler_params=_params("arbitrary", "arbitrary"))(qkv, qkv, qkv, proj, alog, dtb, *cts)


def _heads(x):
    return jnp.stack([x[:, DK * h:DK * (h + 1)] for h in range(N_HEADS_D)])


SCAN_CHUNKS = 8


def _dn_scan(pre, name):
    S = pre[0].shape[0]
    NCH = S // CHUNK
    rows = SCAN_CHUNKS * CHUNK

    def body(u_ref, w_ref, qk_ref, qd_ref, kd_ref, eg_ref, o_ref, st_ref, s_ref):
        @pl.when(pl.program_id(0) == 0)
        def _():
            s_ref[...] = jnp.zeros_like(s_ref)

        St = s_ref[...]
        for k in range(SCAN_CHUNKS):
            r = slice(k * CHUNK, (k + 1) * CHUNK)
            st_ref[k] = St
            o, St = _dn_scan_fn(_heads(u_ref[r, :]), _heads(w_ref[r, :]), _heads(qk_ref[r, :])[:, :, :CHUNK],
                                _heads(qd_ref[r, :]), _heads(kd_ref[r, :]), _heads(eg_ref[r, :]), St)
            for h in range(N_HEADS_D):
                o_ref[r, DK * h:DK * (h + 1)] = o[h]
        s_ref[...] = St

    blk = pl.BlockSpec((rows, N_HEADS_D * DK), lambda n: (n, 0))
    return pl.pallas_call(
        body, grid=(S // rows,), in_specs=[blk] * 6,
        out_specs=[blk, pl.BlockSpec((SCAN_CHUNKS, N_HEADS_D, DK, DK), lambda n: (n, 0, 0, 0))],
        out_shape=[SDS((S, N_HEADS_D * DK), f32), SDS((NCH, N_HEADS_D, DK, DK), f32)],
        scratch_shapes=[pltpu.VMEM((N_HEADS_D, DK, DK), f32)], name=name, compiler_params=_params("arbitrary"))(*pre)


def _dn_scan_bwd(pre, states, do, name):
    S = do.shape[0]
    rows = SCAN_CHUNKS * CHUNK
    steps = S // rows

    def body(u_ref, w_ref, qk_ref, qd_ref, kd_ref, eg_ref, st_ref, do_ref,
             du_ref, dw_ref, dqk_ref, dqd_ref, dkd_ref, deg_ref, ds_ref):
        @pl.when(pl.program_id(0) == 0)
        def _():
            ds_ref[...] = jnp.zeros_like(ds_ref)

        dS = ds_ref[...]
        for k in reversed(range(SCAN_CHUNKS)):
            r = slice(k * CHUNK, (k + 1) * CHUNK)
            _, vjp = jax.vjp(_dn_scan_fn, _heads(u_ref[r, :]), _heads(w_ref[r, :]), _heads(qk_ref[r, :])[:, :, :CHUNK],
                             _heads(qd_ref[r, :]), _heads(kd_ref[r, :]), _heads(eg_ref[r, :]), st_ref[k])
            du, dw, dqk, dqd, dkd, deg, dS = vjp((_heads(do_ref[r, :]), dS))
            for h in range(N_HEADS_D):
                c = slice(DK * h, DK * (h + 1))
                for ref, val in ((du_ref, du), (dw_ref, dw), (dqd_ref, dqd), (dkd_ref, dkd), (deg_ref, deg)):
                    ref[r, c] = val[h]
                dqk_ref[r, DK * h:DK * h + CHUNK] = dqk[h]
                dqk_ref[r, DK * h + CHUNK:DK * (h + 1)] = jnp.zeros((CHUNK, DK - CHUNK), f32)
        ds_ref[...] = dS

    blk = pl.BlockSpec((rows, N_HEADS_D * DK), lambda n: (steps - 1 - n, 0))
    return pl.pallas_call(
        body, grid=(steps,),
        in_specs=[blk] * 6 + [pl.BlockSpec((SCAN_CHUNKS, N_HEADS_D, DK, DK), lambda n: (steps - 1 - n, 0, 0, 0)), blk],
        out_specs=[blk] * 6, out_shape=[SDS((S, N_HEADS_D * DK), f32)] * 6,
        scratch_shapes=[pltpu.VMEM((N_HEADS_D, DK, DK), f32)], name=name,
        compiler_params=_params("arbitrary"))(*pre, states, do)


def _loss_head(y, t, name):
    S, D = y.shape
    tm = ROW_TILE

    def body(y_ref, t_ref, dy_ref, l_ref):
        i = pl.program_id(0)
        d = y_ref[...] - t_ref[...]
        dy_ref[...] = d * (1.0 / D)
        part = jnp.sum(jnp.sum(d * d, axis=1, keepdims=True), axis=0, keepdims=True) * (0.5 / D)

        @pl.when(i == 0)
        def _():
            l_ref[...] = jnp.zeros_like(l_ref)

        l_ref[...] += jnp.broadcast_to(part, l_ref.shape)

    spec = pl.BlockSpec((tm, D), lambda i: (i, 0))
    dy, l = pl.pallas_call(body, grid=(S // tm,), in_specs=[spec, spec],
                           out_specs=[spec, pl.BlockSpec((1, 128), lambda i: (0, 0))],
                           out_shape=[SDS((S, D), f32), SDS((1, 128), f32)], name=name,
                           compiler_params=_params("arbitrary"))(y, t)
    return l[0, 0], dy


def _adamw_refs(w_ref, g_ref, m_ref, v_ref, d_ref, mo_ref, vo_ref):
    gv = g_ref[...]
    m2 = ADAM_B1 * m_ref[...] + (1.0 - ADAM_B1) * gv
    v2 = ADAM_B2 * v_ref[...] + (1.0 - ADAM_B2) * (gv * gv)
    m_hat = m2 / (1.0 - ADAM_B1 ** ADAM_STEP)
    v_hat = v2 / (1.0 - ADAM_B2 ** ADAM_STEP)
    d_ref[...] = -ADAM_LR * (m_hat / (jnp.sqrt(v_hat) + ADAM_EPS) + ADAM_WD * w_ref[...])
    mo_ref[...] = m2
    vo_ref[...] = v2


def _adamw_small(ws, gs, ms, vs, name):
    n = len(ws)

    def body(*refs):
        for i in range(n):
            _adamw_refs(*[refs[k * n + i] for k in range(7)])

    res = pl.pallas_call(body, out_shape=[SDS(a.shape, f32) for a in ws] * 3, name=name)(*ws, *gs, *ms, *vs)
    return res[:n], res[n:2 * n], res[2 * n:]


def _adamw(w, g, m, v, tr, name):
    L, R, C = w.shape
    assert R % tr == 0

    def body(*refs):
        _adamw_refs(*refs)

    spec = pl.BlockSpec((1, tr, C), lambda l, i: (l, i, 0))
    return pl.pallas_call(body, grid=(L, R // tr), in_specs=[spec] * 4, out_specs=[spec] * 3,
                          out_shape=[SDS((L, R, C), f32)] * 3, name=name,
                          compiler_params=_params("parallel", "parallel"))(w, g, m, v)


def _rope_tables(S):
    inv = 1.0 / (10000.0 ** (jnp.arange(0, HEAD_DIM, 2, dtype=f32) / HEAD_DIM))
    ang = jnp.arange(S, dtype=f32)[:, None] * inv[None, :]
    cos, sin = jnp.cos(ang), jnp.sin(ang)
    return (jnp.tile(jnp.concatenate([cos, cos], axis=1), (1, N_HEADS_A)),
            jnp.tile(jnp.concatenate([-sin, sin], axis=1), (1, N_HEADS_A)))


def _layer_fwd(x, W, cos, sgn_sin, l, late_weights=None, h1=None, next_pre_mix=None):
    n = f"l{l}_"
    if h1 is None:
        (h1,) = _rows(_rms_fn, [x], [W["norm_pre_mix"]], [(D_MODEL, bf16)], n + "pre_mix_norm")
    proj = _mm(h1, W["w_in"], "nn", 1024, 768, f32, n + "in_proj")
    attn_out, lse = _attn_fwd(proj, cos, sgn_sin, n + "attn_fwd")
    (qkv,) = _colconv_fwd([(proj, 3)], [(W["dn_conv_w"], 0)], None, 4, _silu_fn, 3, 512, [f32], n + "dn_conv")
    dn_pre = _dn_prep(qkv, proj, W["dn_a_log"], W["dn_dt_bias"], n + "dn_prep")
    dn_o, dn_states = _dn_scan(dn_pre, n + "dn_scan")
    (dn_out,) = _rows(_dn_post_fn, [(dn_o, DK, 0), (proj, DK, 3072 // DK)], [W["dn_norm_w"]], [(DK, bf16)], n + "dn_post",
                      ncol=N_HEADS_D, tm=4 * ROW_TILE)
    mix_in = jnp.concatenate([attn_out, dn_out], axis=1)
    late = late_weights(mix_in) if late_weights is not None else {}
    W = {**W, **late}
    mix = _mm(mix_in, W["w_out"], "nn", 512, 512, f32, n + "out_proj")
    x1, h2 = _rows(_res_rms_pre_fn, [mix, x], [W["norm_post_mix"], W["norm_pre_ffn"]],
                   [(D_MODEL, f32), (D_MODEL, bf16)], n + "post_mix_pre_ffn_norm")
    u0 = _mm(h2, W["ffn_w_in"], "nn", 1024, D_FF // 2, bf16, n + "ffn_in")
    nb_ff = D_FF // 256
    (act,) = _colconv_fwd([(u0, 0), (u0, nb_ff)], [(W["ffn_conv_w"], 0), (W["ffn_conv_w"], nb_ff)],
                          [(W["ffn_conv_b"], 0), (W["ffn_conv_b"], nb_ff)], 3, _geglu_fn, nb_ff, 256, [bf16],
                          n + "ffn_conv_glu")
    f = _mm(act, W["ffn_w_out"], "nn", 512, 512, f32, n + "ffn_out")
    if next_pre_mix is None:
        (x2,), h1_next = _rows(_res_rms_fn, [f, x1], [W["norm_post_ffn"]], [(D_MODEL, f32)], n + "post_ffn_norm"), None
    else:
        x2, h1_next = _rows(_res_rms_pre_fn, [f, x1], [W["norm_post_ffn"], next_pre_mix],
                            [(D_MODEL, f32), (D_MODEL, bf16)], n + "post_ffn_next_pre_mix_norm")
    saved = dict(x=x, h1=h1, proj=proj, attn_out=attn_out, lse=lse, qkv=qkv, dn_pre=dn_pre, dn_o=dn_o,
                 dn_states=dn_states, mix_in=mix_in, mix=mix, x1=x1, h2=h2, u0=u0, act=act, f=f, late=late)
    return x2, h1_next, saved


def _layer_bwd(dx2, sv, W, cos, sgn_sin, l, after_ffn=None, next_layer=None, first_layer=True):
    n = f"l{l}_"
    S = dx2.shape[0]
    g = {}
    g_next_pre = None
    if next_layer is None:
        (df,), (g["norm_post_ffn"],) = _rows_vjp(_rms_fn, [sv["f"]], [W["norm_post_ffn"]], [dx2], [0], [0],
                                                 n + "post_ffn_norm_bwd", row_dtype=bf16)
    else:
        (df, dx2), (g["norm_post_ffn"], g_next_pre) = _rows_vjp(
            _res_rms_pre_fn, [sv["f"], sv["x1"]], [W["norm_post_ffn"], next_layer[1]], [dx2, next_layer[0]], [0, 1], [0, 1],
            n + "post_ffn_next_pre_mix_norm_bwd", row_dtype=[bf16, f32])
    dact = _mm(df, W["ffn_w_out"], "nt", 512, 1408, f32, n + "ffn_out_dx")
    g["ffn_w_out"] = _mm(sv["act"], df, "tn", 256, 1024, f32, n + "ffn_out_dw")
    nb_ff = D_FF // 256
    u0 = sv["u0"]
    dxs, dws, dbs = _colconv_bwd([(u0, 0), (u0, nb_ff)], [(W["ffn_conv_w"], 0), (W["ffn_conv_w"], nb_ff)],
                                 [(W["ffn_conv_b"], 0), (W["ffn_conv_b"], nb_ff)], 3, _geglu_fn, [dact], nb_ff, 256,
                                 n + "ffn_conv_glu_bwd", dx_dtype=bf16)
    du0 = jnp.concatenate(dxs, axis=1)
    g["ffn_conv_w"] = jnp.concatenate(dws, axis=1)
    g["ffn_conv_b"] = jnp.concatenate(dbs, axis=1)
    dh2 = _mm(du0, W["ffn_w_in"], "nt", 512, 512, f32, n + "ffn_in_dx")
    g["ffn_w_in"] = _mm(sv["h2"], du0, "tn", 512, D_FF // 2, f32, n + "ffn_in_dw", column_shards=True)
    if after_ffn is not None:
        W = dict(W, norm_post_mix=W["norm_post_mix"] + after_ffn(g, dh2))
    (dmix, dx1), (g["norm_post_mix"], g["norm_pre_ffn"]) = _rows_vjp(
        _res_rms_pre_fn, [sv["mix"], sv["x"]], [W["norm_post_mix"], W["norm_pre_ffn"]], [dx2, dh2], [0, 1], [0, 1],
        n + "post_mix_pre_ffn_norm_bwd", row_dtype=[bf16, f32])
    dmix_in = _mm(dmix, W["w_out"], "nt", 512, 512, f32, n + "out_proj_dx")
    g["w_out"] = _mm(sv["mix_in"], dmix, "tn", 512, 512, f32, n + "out_proj_dw")

    (ddn_o, dz), (g["dn_norm_w"],) = _rows_vjp(
        _dn_post_fn, [(sv["dn_o"], DK, 0), (sv["proj"], DK, 3072 // DK)], [W["dn_norm_w"]], [(dmix_in, DK, ATTN_W // DK)],
        [0, 1], [0], n + "dn_post_bwd", ncol=N_HEADS_D, tm=4 * ROW_TILE)
    dpre = _dn_scan_bwd(sv["dn_pre"], sv["dn_states"], ddn_o, n + "dn_scan_bwd")
    dq, dk, dv, dba, g["dn_a_log"], g["dn_dt_bias"] = _dn_prep_bwd(
        sv["qkv"], sv["proj"], W["dn_a_log"], W["dn_dt_bias"], dpre, n + "dn_prep_bwd")
    dqkv = jnp.concatenate([dq, dk, dv], axis=1)
    (dqkv0,), (g["dn_conv_w"],), _ = _colconv_bwd([(sv["proj"], 3)], [(W["dn_conv_w"], 0)], None, 4, _silu_fn,
                                                 [dqkv], 3, 512, n + "dn_conv_bwd")

    daq, dak, dav = _attn_bwd(sv["proj"], cos, sgn_sin, dmix_in, sv["attn_out"], sv["lse"], n + "attn_bwd")
    dproj = jnp.concatenate([daq, dak, dav, dqkv0, dz, dba, jnp.zeros((S, PROJ_W - 3712), f32)], axis=1).astype(bf16)
    dh1 = _mm(dproj, W["w_in"], "nt", 512, 512, f32, n + "in_proj_dx")
    g["w_in"] = _mm(sv["h1"], dproj, "tn", 512, 768, f32, n + "in_proj_dw")
    if not first_layer:
        return (dx1, dh1), g, g_next_pre
    (dx,), (g["norm_pre_mix"],) = _rows_vjp(_rms_fn, [sv["x"]], [W["norm_pre_mix"]], [dh1], [0], [0],
                                            n + "pre_mix_norm_bwd", adds={0: dx1})
    return dx, g, g_next_pre


def _local_step(x, target, layers):
    cos, sgn_sin = _rope_tables(x.shape[0])
    saved, h1 = [], None
    for l, W in enumerate(layers):
        nxt = layers[l + 1]["norm_pre_mix"] if l + 1 < len(layers) else None
        x, h1, sv = _layer_fwd(x, W, cos, sgn_sin, l, h1=h1, next_pre_mix=nxt)
        saved.append(sv)
    loss, dx = _loss_head(x, target, "loss_head")
    grads = [None] * len(layers)
    nxt = None
    for l in reversed(range(len(layers))):
        dx, grads[l], g_pre = _layer_bwd(dx, saved[l], layers[l], cos, sgn_sin, l, next_layer=nxt, first_layer=(l == 0))
        if g_pre is not None:
            grads[l + 1]["norm_pre_mix"] = g_pre
        if l > 0:
            dx, dh1 = dx
            nxt = (dh1, layers[l]["norm_pre_mix"])
    return loss, dx, grads


def _pos():
    x, y, c = lax.axis_index("x"), lax.axis_index("y"), lax.axis_index("c")
    return x, y, c, [(1 - x, y), (x, 1 - y), (1 - x, 1 - y)]


def _rcopy(src, dst, send_sem, recv_sem, dev):
    return pltpu.make_async_remote_copy(src_ref=src, dst_ref=dst, send_sem=send_sem, recv_sem=recv_sem,
                                        device_id=dev, device_id_type=MESH)


def _half_rows(ref, h, which, axis):
    if h is None:
        return ref
    rows = pl.ds(pl.multiple_of(which * h, 16), h)
    return ref.at[:, rows, :] if axis == 1 else ref.at[rows, :]


def _dma_sems(*counts):
    return [pltpu.SemaphoreType.DMA((k,)) for k in counts]


def _all_gather(arrs, halves, name):
    n = len(arrs)

    def body(*refs):
        ins, outs = refs[:n], refs[n:2 * n]
        send1, recv1, send2, recv2 = refs[2 * n:]
        x, y, c, chips = _pos()
        me, sib, s_me = (x, y, c), (x, y, 1 - c), 2 * x + y
        sends = []
        for i in range(n):
            for j, chip in enumerate(chips):
                cp = _rcopy(_half_rows(ins[i], halves[i], c, 1), _half_rows(outs[i].at[s_me], halves[i], c, 1),
                            send1.at[3 * i + j], recv1.at[3 * i + j], (*chip, c))
                cp.start()
                sends.append(cp)
        for i in range(n):
            for j, (px, py) in enumerate(chips):
                k = 3 * i + j
                landed = _half_rows(outs[i].at[2 * px + py], halves[i], c, 1)
                _rcopy(landed, landed, send1.at[k], recv1.at[k], me).wait_recv()
                if halves[i] is not None:
                    cp = _rcopy(landed, landed, send2.at[k], recv2.at[k], sib)
                    cp.start()
                    sends.append(cp)
        for i in range(n):
            if halves[i] is None:
                continue
            for j, (px, py) in enumerate(chips):
                k = 3 * i + j
                other = _half_rows(outs[i].at[2 * px + py], halves[i], 1 - c, 1)
                _rcopy(other, other, send2.at[k], recv2.at[k], me).wait_recv()
        for cp in sends:
            cp.wait_send()

    return pl.pallas_call(
        body, in_specs=[ANY] * n, out_specs=[ANY] * n,
        out_shape=[SDS((4,) + a.shape, a.dtype) for a in arrs],
        scratch_shapes=_dma_sems(3 * n, 3 * n, 3 * n, 3 * n), name=name)(*arrs)


HBM = pl.BlockSpec(memory_space=pltpu.HBM)
SEM = pl.BlockSpec(memory_space=pltpu.SEMAPHORE)
_EFFECT = pltpu.SideEffectType.DATAFLOW_SIDE_EFFECTING


def _in_hbm(a):
    return pltpu.with_memory_space_constraint(a, pltpu.HBM)


def _split_copy(srcs, land_shapes, plan, per, after, name):
    n = len(srcs)
    k = per * n

    def body(*refs):
        ins, lands, token = refs[:n], refs[n:2 * n], refs[-1]
        send, recv = refs[2 * n + 1], refs[2 * n + 2]
        for i, (src, dst, dev, _) in enumerate(plan(ins, lands)):
            _rcopy(src, dst, send.at[i], recv.at[i], dev).start()
        token[...] = jnp.zeros_like(token)

    lands = [_in_hbm(lax.empty(s.shape, s.dtype)) for s in land_shapes]
    return pl.pallas_call(
        body, name=name,
        out_shape=(pltpu.SemaphoreType.DMA((k,)), pltpu.SemaphoreType.DMA((k,)),
                   *[pltpu.HBM(a.shape, a.dtype) for a in srcs], *[pltpu.HBM(s.shape, s.dtype) for s in land_shapes],
                   SDS((8, 128), f32)),
        in_specs=[HBM] * (2 * n) + [ANY], out_specs=(SEM, SEM, *[HBM] * (2 * n), pl.BlockSpec(memory_space=pltpu.VMEM)),
        input_output_aliases={i: 2 + i for i in range(2 * n)},
        compiler_params=pltpu.CompilerParams(has_side_effects=_EFFECT))(*[_in_hbm(a) for a in srcs], *lands, after)


def _split_wait(started, n, plan, after, name):
    send, recv = started[0], started[1]
    thru = started[2:2 + 2 * n]

    def body(*refs):
        ins, lands = refs[:n], refs[n:2 * n]
        send_ref, recv_ref = refs[2 * n], refs[2 * n + 1]
        for i, (src, _, dev, mine) in enumerate(plan(ins, lands)):
            cp = _rcopy(src, mine, send_ref.at[i], recv_ref.at[i], dev)
            cp.wait_send()
            cp.wait_recv()

    res = pl.pallas_call(
        body, name=name, out_shape=tuple(pltpu.HBM(a.shape, a.dtype) for a in thru),
        in_specs=[HBM] * (2 * n) + [SEM, SEM, ANY], out_specs=tuple([HBM] * (2 * n)),
        input_output_aliases={i: i for i in range(2 * n)},
        compiler_params=pltpu.CompilerParams(has_side_effects=_EFFECT))(*thru, send, recv, after)
    return res[:n], res[n:]


def _gather_plan(halves):
    def plan(ins, lands):
        x, y, c, chips = _pos()
        out = []
        for i in range(len(ins)):
            for px, py in chips:
                out.append((_half_rows(ins[i], halves[i], c, 1), _half_rows(lands[i].at[2 * x + y], halves[i], c, 1),
                            (px, py, c), _half_rows(lands[i].at[2 * px + py], halves[i], c, 1)))
        return out
    return plan


def _scatter_plan(ins, lands):
    x, y, c, chips = _pos()
    out = []
    for i in range(len(ins)):
        for j, (px, py) in enumerate(chips):
            out.append((ins[i].at[2 * px + py], lands[i].at[j], (px, py, c), lands[i].at[j]))
    return out


def _exchange_plan(ins, lands):
    x, y, c, _ = _pos()
    return [(_half_rows(g, g.shape[1] // 2, 1 - c, 1), land, (x, y, 1 - c), land) for g, land in zip(ins, lands)]


def _pass_to_sibling(lands, halves, name):
    n = len(lands)

    def body(*refs):
        outs = refs[n:2 * n]
        send, recv = refs[2 * n:]
        x, y, c, chips = _pos()
        sends = []
        for i in range(n):
            for j, (px, py) in enumerate(chips):
                landed = _half_rows(outs[i].at[2 * px + py], halves[i], c, 1)
                cp = _rcopy(landed, landed, send.at[3 * i + j], recv.at[3 * i + j], (x, y, 1 - c))
                cp.start()
                sends.append(cp)
        for i in range(n):
            for j, (px, py) in enumerate(chips):
                other = _half_rows(outs[i].at[2 * px + py], halves[i], 1 - c, 1)
                _rcopy(other, other, send.at[3 * i + j], recv.at[3 * i + j], (x, y, c)).wait_recv()
        for cp in sends:
            cp.wait_send()

    return pl.pallas_call(
        body, in_specs=[ANY] * n, out_specs=[ANY] * n, out_shape=[SDS(a.shape, a.dtype) for a in lands],
        input_output_aliases={k: k for k in range(n)}, scratch_shapes=_dma_sems(3 * n, 3 * n), name=name)(*lands)


def _exchange_halves(gs, name):
    n = len(gs)

    def body(*refs):
        ins, outs = refs[:n], refs[n:2 * n]
        send, recv = refs[2 * n:]
        x, y, c, _ = _pos()
        sends = []
        for k in range(n):
            cp = _rcopy(_half_rows(ins[k], gs[k].shape[1] // 2, 1 - c, 1), outs[k], send.at[k], recv.at[k], (x, y, 1 - c))
            cp.start()
            sends.append(cp)
        for k in range(n):
            _rcopy(outs[k], outs[k], send.at[k], recv.at[k], (x, y, c)).wait_recv()
        for cp in sends:
            cp.wait_send()

    return pl.pallas_call(
        body, in_specs=[ANY] * n, out_specs=[ANY] * n,
        out_shape=[SDS((4, g.shape[1] // 2, g.shape[2]), g.dtype) for g in gs],
        scratch_shapes=_dma_sems(n, n), name=name)(*gs)


def _scatter_partials(ps, name):
    n = len(ps)

    def body(*refs):
        ins, outs = refs[:n], refs[n:2 * n]
        send, recv = refs[2 * n:]
        x, y, c, chips = _pos()
        sends = []
        for k in range(n):
            for j, (px, py) in enumerate(chips):
                cp = _rcopy(ins[k].at[2 * px + py], outs[k].at[j], send.at[3 * k + j], recv.at[3 * k + j], (px, py, c))
                cp.start()
                sends.append(cp)
        for k in range(n):
            for j in range(3):
                _rcopy(outs[k].at[j], outs[k].at[j], send.at[3 * k + j], recv.at[3 * k + j], (x, y, c)).wait_recv()
        for cp in sends:
            cp.wait_send()

    return pl.pallas_call(
        body, in_specs=[ANY] * n, out_specs=[ANY] * n,
        out_shape=[SDS((3,) + p.shape[1:], p.dtype) for p in ps],
        scratch_shapes=_dma_sems(3 * n, 3 * n), name=name)(*ps)


def _join_halves(rs, layers, name):
    n = len(rs)

    def body(*refs):
        outs = refs[n:2 * n]
        send, recv = refs[2 * n:]
        x, y, c, _ = _pos()

        def half(k, which):
            h = rs[k].shape[1] // 2
            return _half_rows(outs[k], h, which, 1) if layers[k] is None else _half_rows(outs[k].at[layers[k]], h, which, 0)

        sends = []
        for k in range(n):
            cp = _rcopy(half(k, c), half(k, c), send.at[k], recv.at[k], (x, y, 1 - c))
            cp.start()
            sends.append(cp)
        for k in range(n):
            _rcopy(half(k, 1 - c), half(k, 1 - c), send.at[k], recv.at[k], (x, y, c)).wait_recv()
        for cp in sends:
            cp.wait_send()

    return pl.pallas_call(
        body, in_specs=[ANY] * n, out_specs=[ANY] * n, out_shape=[SDS(r.shape, r.dtype) for r in rs],
        input_output_aliases={k: k for k in range(n)}, scratch_shapes=_dma_sems(n, n), name=name)(*rs)


def _all_reduce_small(pack, name):
    R = pack.shape[0]

    def body(in_ref, out_ref, buf, send, recv):
        x, y, c, _ = _pos()
        me = 4 * x + 2 * y + c
        buf[me] = in_ref[...]
        sends = []
        for k in range(1, 8):
            peer = me ^ k
            cp = _rcopy(buf.at[me], buf.at[me], send.at[k - 1], recv.at[k - 1], ((peer >> 2) & 1, (peer >> 1) & 1, peer & 1))
            cp.start()
            sends.append(cp)
        for k in range(1, 8):
            _rcopy(buf.at[me ^ k], buf.at[me ^ k], send.at[k - 1], recv.at[k - 1], (x, y, c)).wait_recv()
        for cp in sends:
            cp.wait_send()
        acc = buf[0]
        for d in range(1, 8):
            acc = acc + buf[d]
        out_ref[...] = acc

    return pl.pallas_call(
        body, out_shape=SDS((R, 128), f32),
        in_specs=[pl.BlockSpec(memory_space=pltpu.VMEM)], out_specs=pl.BlockSpec(memory_space=pltpu.VMEM),
        scratch_shapes=[pltpu.VMEM((8, R, 128), f32)] + _dma_sems(7, 7), name=name)(pack)


def _add_sibling(g, recv, c_arr, tr, name):
    _, R, C = g.shape
    h = R // 2
    nrb = h // tr
    assert h % tr == 0

    def body(c_ref, g_ref, r_ref, o_ref):
        o_ref[...] = (g_ref[...] + r_ref[...]).astype(o_ref.dtype)

    spec = pl.BlockSpec((1, tr, C), lambda s, r, c_ref: (s, r, 0))
    grid_spec = pltpu.PrefetchScalarGridSpec(
        num_scalar_prefetch=1, grid=(4, nrb),
        in_specs=[pl.BlockSpec((1, tr, C), lambda s, r, c_ref: (s, c_ref[0] * nrb + r, 0)), spec], out_specs=spec)
    return pl.pallas_call(body, grid_spec=grid_spec, out_shape=SDS((4, h, C), bf16), name=name,
                          compiler_params=_params("parallel", "parallel"))(c_arr, g, recv)


def _add_chips(p, recv, sc_arr, tr, layer, into, name):
    _, h, C = p.shape
    nrb = h // tr
    assert h % tr == 0

    def body(sc_ref, p_ref, r_ref, *rest):
        rest[-1][...] = (p_ref[0].astype(f32) + r_ref[0].astype(f32)) + (r_ref[1].astype(f32) + r_ref[2].astype(f32))

    grid_spec = pltpu.PrefetchScalarGridSpec(
        num_scalar_prefetch=1, grid=(nrb,),
        in_specs=[pl.BlockSpec((1, tr, C), lambda r, sc_ref: (sc_ref[0], r, 0)),
                  pl.BlockSpec((3, tr, C), lambda r, sc_ref: (0, r, 0))] + ([] if into is None else [ANY]),
        out_specs=pl.BlockSpec((None, tr, C), lambda r, sc_ref: (layer, sc_ref[1] * nrb + r, 0)))
    return pl.pallas_call(body, grid_spec=grid_spec, out_shape=SDS((2, 2 * h, C), f32), name=name,
                          input_output_aliases={} if into is None else {3: 0},
                          compiler_params=_params("parallel"))(sc_arr, p, recv, *([] if into is None else [into]))


_BIG = (("w_in", 1024, 256), ("w_out", 256, 128), ("ffn_w_in", 1024, 256), ("ffn_w_out", 704, 352))
_SMALL = ("dn_conv_w", "ffn_conv_w", "ffn_conv_b", "norm_pre_mix", "norm_post_mix", "norm_pre_ffn", "norm_post_ffn",
          "dn_norm_w", "dn_a_log", "dn_dt_bias")
_WEIGHTS = ("w_in", "dn_conv_w", "dn_a_log", "dn_dt_bias", "dn_norm_w", "w_out", "ffn_w_in", "ffn_conv_w", "ffn_conv_b",
            "ffn_w_out", "norm_pre_mix", "norm_post_mix", "norm_pre_ffn", "norm_post_ffn")
_ADAM_ROWS = {"w_in": 256, "w_out": 256, "ffn_w_in": 128, "ffn_w_out": 176}


def _shard_major(name, g):
    if name == "w_in":
        return jnp.stack([g[:, 898 * s:898 * (s + 1)] for s in range(4)])
    if name == "ffn_w_in":
        return g
    return g.reshape(4, g.shape[0] // 4, g.shape[1])


def kernel(x, w_in, dn_conv_w, dn_a_log, dn_dt_bias, dn_norm_w, w_out, ffn_w_in, ffn_conv_w, ffn_conv_b, ffn_w_out, norm_pre_mix, norm_post_mix, norm_pre_ffn, norm_post_ffn, loss_target, m_w_in, m_dn_conv_w, m_dn_a_log, m_dn_dt_bias, m_dn_norm_w, m_w_out, m_ffn_w_in, m_ffn_conv_w, m_ffn_conv_b, m_ffn_w_out, m_norm_pre_mix, m_norm_post_mix, m_norm_pre_ffn, m_norm_post_ffn, v_w_in, v_dn_conv_w, v_dn_a_log, v_dn_dt_bias, v_dn_norm_w, v_w_out, v_ffn_w_in, v_ffn_conv_w, v_ffn_conv_b, v_ffn_w_out, v_norm_pre_mix, v_norm_post_mix, v_norm_pre_ffn, v_norm_post_ffn):
    w = dict(w_in=w_in, dn_conv_w=dn_conv_w, dn_a_log=dn_a_log, dn_dt_bias=dn_dt_bias, dn_norm_w=dn_norm_w, w_out=w_out,
             ffn_w_in=ffn_w_in, ffn_conv_w=ffn_conv_w, ffn_conv_b=ffn_conv_b, ffn_w_out=ffn_w_out, norm_pre_mix=norm_pre_mix,
             norm_post_mix=norm_post_mix, norm_pre_ffn=norm_pre_ffn, norm_post_ffn=norm_post_ffn)
    m = dict(w_in=m_w_in, dn_conv_w=m_dn_conv_w, dn_a_log=m_dn_a_log, dn_dt_bias=m_dn_dt_bias, dn_norm_w=m_dn_norm_w,
             w_out=m_w_out, ffn_w_in=m_ffn_w_in, ffn_conv_w=m_ffn_conv_w, ffn_conv_b=m_ffn_conv_b, ffn_w_out=m_ffn_w_out,
             norm_pre_mix=m_norm_pre_mix, norm_post_mix=m_norm_post_mix, norm_pre_ffn=m_norm_pre_ffn,
             norm_post_ffn=m_norm_post_ffn)
    v = dict(w_in=v_w_in, dn_conv_w=v_dn_conv_w, dn_a_log=v_dn_a_log, dn_dt_bias=v_dn_dt_bias, dn_norm_w=v_dn_norm_w,
             w_out=v_w_out, ffn_w_in=v_ffn_w_in, ffn_conv_w=v_ffn_conv_w, ffn_conv_b=v_ffn_conv_b, ffn_w_out=v_ffn_w_out,
             norm_pre_mix=v_norm_pre_mix, norm_post_mix=v_norm_post_mix, norm_pre_ffn=v_norm_pre_ffn,
             norm_post_ffn=v_norm_post_ffn)
    xi, yi, ci = lax.axis_index("x"), lax.axis_index("y"), lax.axis_index("c")
    s_me = 2 * xi + yi
    c_arr = jnp.reshape(ci, (1,)).astype(jnp.int32)
    sc_arr = jnp.stack([s_me, ci]).astype(jnp.int32)

    mats = [name for name, _, _ in _BIG]
    rest = mats[1:]
    half_of = {name: rows // 2 for name, rows, _ in _BIG}
    tiles = {name: tr for name, _, tr in _BIG}
    gathered_shape = lambda a: SDS((4,) + a.shape, a.dtype)

    own = {k: w[k].astype(bf16) for k in mats}
    plan_in = _gather_plan([half_of["w_in"], None, None])
    src_in = [own["w_in"][0:1], dn_conv_w, ffn_conv_w]
    started_in = _split_copy(src_in, [gathered_shape(a) for a in src_in], plan_in, 3, src_in[0], "weights_gather_w_in0_start")
    plan0 = _gather_plan([half_of[k] for k in rest])
    src0 = [own[k][0:1] for k in rest]
    started0 = _split_copy(src0, [gathered_shape(a) for a in src0], plan0, 3, started_in[-1], "weights_gather_l0_start")
    plan1 = _gather_plan([half_of[k] for k in mats])
    src1 = [own[k][1:2] for k in mats]
    started1 = _split_copy(src1, [gathered_shape(a) for a in src1], plan1, 3, started0[-1], "weights_gather_l1_start")
    _, landed_in = _split_wait(started_in, len(src_in), plan_in, started1[-1], "weights_gather_w_in0_wait")
    got_in = list(_pass_to_sibling(landed_in[:1], [half_of["w_in"]], "weights_gather_w_in0_sibling")) + list(landed_in[1:])

    def pick(mine, gathered):
        return [jnp.where(s_me == s, mine, gathered[s]) for s in range(4)]

    conv = {"dn_conv_w": jnp.concatenate(pick(dn_conv_w, got_in[1]), axis=-1),
            "ffn_conv_w": jnp.concatenate(pick(ffn_conv_w, got_in[2]), axis=-1)}
    lanes = lambda a: jnp.pad(a, ((0, 0), (0, 128 - a.shape[1])))
    vec = dict(dn_a_log=lanes(dn_a_log), dn_dt_bias=lanes(dn_dt_bias), dn_norm_w=dn_norm_w, ffn_conv_b=ffn_conv_b,
               norm_pre_mix=norm_pre_mix, norm_post_mix=norm_post_mix, norm_pre_ffn=norm_pre_ffn, norm_post_ffn=norm_post_ffn)

    def matrices(l, names, gathered):
        W = {}
        for k, a in zip(names, gathered):
            if k in ("w_out", "ffn_w_out"):
                rows_, cols = own[k].shape[1:]
                W[k] = lax.dynamic_update_slice(a[:, 0], own[k][l][None], (s_me, 0, 0)).reshape(4 * rows_, cols)
            else:
                cat = jnp.concatenate(pick(own[k][l], a[:, 0]), axis=-1)
                W[k] = jnp.pad(cat, ((0, 0), (0, PROJ_W - IN_COLS))) if k == "w_in" else cat
        return W

    def small_weights(l):
        return {**{k: a[l] for k, a in conv.items()}, **{k: a[l:l + 1] for k, a in vec.items()}}

    def late_l0(mix_in):
        _, landed = _split_wait(started0, len(rest), plan0, mix_in, "weights_gather_l0_wait")
        return matrices(0, rest, _pass_to_sibling(landed, [half_of[k] for k in rest], "weights_gather_l0_sibling"))

    cos, sgn_sin = _rope_tables(x.shape[1])
    W0 = {**small_weights(0), **matrices(0, ["w_in"], got_in[:1])}
    W0_first = dict(W0, norm_pre_mix=W0["norm_pre_mix"] + started1[-1][0, 0])
    x1, h1_l1, saved0 = _layer_fwd(x[0], W0_first, cos, sgn_sin, 0, late_weights=late_l0,
                                   next_pre_mix=norm_pre_mix[1:2])
    _, landed1 = _split_wait(started1, len(mats), plan1, x1, "weights_gather_l1_wait")
    W1 = {**small_weights(1),
          **matrices(1, mats, _pass_to_sibling(landed1, [half_of[k] for k in mats], "weights_gather_l1_sibling"))}
    x2, _, saved1 = _layer_fwd(x1, W1, cos, sgn_sin, 1, h1=h1_l1)
    loss_local, dy = _loss_head(x2, loss_target[0], "loss_head")
    loss = lax.psum(loss_local, ("x", "y", "c"))

    def shard_major(names, grads_l):
        return [_shard_major(name, grads_l[name]) for name in names]

    def add_siblings(l, names, gs, from_sib):
        return [_add_sibling(g, r, c_arr, tiles[name], f"add_sibling_{name}{l}") for g, r, name in zip(gs, from_sib, names)]

    def scatter_start(l, names, parts, after, tag):
        return _split_copy(parts, [SDS((3,) + p.shape[1:], p.dtype) for p in parts], _scatter_plan, 3, after,
                           f"grads_l{l}{tag}_scatter_start")

    def owner_sums(l, names, sent, after, tag, into):
        parts, recvd = _split_wait(sent, len(names), _scatter_plan, after, f"grads_l{l}{tag}_scatter_wait")
        return {name: _add_chips(p, r, sc_arr, tiles[name], l, into.get(name), f"add_chips_{name}{l}")
                for p, r, name in zip(parts, recvd, names)}

    (dx1, dh1_l1), grads1, _ = _layer_bwd(dy, saved1, W1, cos, sgn_sin, 1, first_layer=False)
    gs1 = shard_major(mats, grads1)
    swap1 = _split_copy(gs1, [SDS((4, g.shape[1] // 2, g.shape[2]), g.dtype) for g in gs1], _exchange_plan, 1, dx1,
                        "grads_l1_sibling_start")
    ffn = ["ffn_w_in", "ffn_w_out"]
    launched = {}

    def after_ffn_l0(g_ffn, dx_mid):
        gs1_, from_sib1 = _split_wait(swap1, len(mats), _exchange_plan, dx_mid, "grads_l1_sibling_wait")
        launched["l1"] = scatter_start(1, mats, add_siblings(1, mats, gs1_, from_sib1), dx_mid, "")
        gs0 = shard_major(ffn, g_ffn)
        from_sib0 = _exchange_halves(gs0, "grads_l0_ffn_to_sibling")
        launched["l0_ffn"] = scatter_start(0, ffn, add_siblings(0, ffn, gs0, from_sib0), launched["l1"][-1], "_ffn")
        return launched["l0_ffn"][-1][0, 0]

    W0_last = dict(W0, **saved0["late"], norm_post_ffn=W0["norm_post_ffn"] + swap1[-1][0, 0])
    dx, grads0, grads1["norm_pre_mix"] = _layer_bwd(dx1, saved0, W0_last, cos, sgn_sin, 0, after_ffn=after_ffn_l0,
                                                    next_layer=(dh1_l1, norm_pre_mix[1:2]))
    mix = ["w_in", "w_out"]
    gs0 = shard_major(mix, grads0)
    part0 = add_siblings(0, mix, gs0, _exchange_halves(gs0, "grads_l0_mix_to_sibling"))
    sent0 = scatter_start(0, mix, part0, dx, "_mix")
    red = owner_sums(0, ffn, launched["l0_ffn"], sent0[-1], "_ffn", {})
    red = owner_sums(1, mats, launched["l1"], sent0[-1], "", red)
    joined = dict(zip(mats, _join_halves([red[k] for k in mats], [1 if k in mix else None for k in mats],
                                         "grads_join_early")))
    grads = [grads0, grads1]

    small = {}
    for name in _SMALL:
        per_layer = [grads[l][name] for l in range(2)]
        if name in ("dn_a_log", "dn_dt_bias"):
            per_layer = [p[:, :N_HEADS_D] for p in per_layer]
        small[name] = jnp.stack(per_layer).reshape((2,) + (w[name].shape[1:] if name not in ("dn_conv_w", "ffn_conv_w")
                                                           else per_layer[0].shape))
    flat = jnp.concatenate([small[name].reshape(-1) for name in _SMALL])
    n_rows = -(-flat.shape[0] // 1024) * 8
    summed = _all_reduce_small(jnp.pad(flat, (0, n_rows * 128 - flat.shape[0])).reshape(n_rows, 128),
                               "small_grads_all_reduce").reshape(-1)
    off = 0
    g_out = {}
    for name in _SMALL:
        size = small[name].size
        g_out[name] = summed[off:off + size].reshape(small[name].shape)
        off += size
    g_out["dn_conv_w"] = lax.dynamic_slice_in_dim(g_out["dn_conv_w"], s_me * 384, 384, axis=2)
    g_out["ffn_conv_w"] = lax.dynamic_slice_in_dim(g_out["ffn_conv_w"], s_me * 1408, 1408, axis=2)
    for k in ffn:
        g_out[k] = joined[k]

    deltas, new_m, new_v = {}, {}, {}

    def step(name):
        shape = w[name].shape
        as3 = (lambda a: a) if len(shape) == 3 else (lambda a: a.reshape(shape[0], 1, shape[1]))
        tr = _ADAM_ROWS.get(name, as3(w[name]).shape[1])
        d_, m_, v_ = _adamw(as3(w[name]), as3(g_out[name]), as3(m[name]), as3(v[name]), tr, f"adamw_{name}")
        deltas[name], new_m[name], new_v[name] = d_.reshape(shape), m_.reshape(shape), v_.reshape(shape)

    for name in ffn:
        step(name)
    tiny = [name for name in _WEIGHTS if name not in mats]
    stepped = _adamw_small(*[[d[name] for name in tiny] for d in (w, g_out, m, v)], "adamw_small")
    for out, vals in zip((deltas, new_m, new_v), stepped):
        out.update(zip(tiny, vals))
    done = jnp.reshape(deltas["ffn_w_in"][0, 0, 0] + deltas["ffn_w_out"][0, 0, 0] + deltas["norm_post_ffn"][0, 0], (1,))
    red = owner_sums(0, mix, sent0, done, "_mix", joined)
    for k, a in zip(mix, _join_halves([red[k] for k in mix], [0] * len(mix), "grads_join_late")):
        g_out[k] = a
        step(k)

    return (loss, dx[None], *[g_out[k] for k in _WEIGHTS], *[deltas[k] for k in _WEIGHTS],
            *[new_m[k] for k in _WEIGHTS], *[new_v[k] for k in _WEIGHTS])
```

```python
import jax
import jax.numpy as jnp
from jax import lax
from jax.experimental import pallas as pl
from jax.experimental.pallas import tpu as pltpu

f32, bf16 = jnp.float32, jnp.bfloat16
SDS = jax.ShapeDtypeStruct
HI = lax.Precision.HIGH
MESH = pl.DeviceIdType.MESH
ANY = pl.BlockSpec(memory_space=pl.ANY)

D_MODEL = 1024
N_HEADS_A, HEAD_DIM = 8, 64
ATTN_W = 512
N_HEADS_D, DK = 4, 128
CHUNK = 64
D_FF = 2816
IN_COLS = 3592
PROJ_W = 3840
DN_QKV_COL = 3 * ATTN_W
DN_Z_COL = DN_QKV_COL + 3 * N_HEADS_D * DK
DN_BA_COL = DN_Z_COL + N_HEADS_D * DK
N_SHARDS = 4
BRANCHES = ((1, 16), (4, 4), (16, 1))
EPS = 1e-6
NEG = -1e30
ROW_TILE = 256
VMEM_LIMIT = 56 * 1024 * 1024

ADAM_LR, ADAM_B1, ADAM_B2, ADAM_EPS, ADAM_WD, ADAM_STEP = 0.001, 0.9, 0.999, 1e-08, 0.01, 10


def _params(*sem):
    return pltpu.CompilerParams(dimension_semantics=sem, vmem_limit_bytes=VMEM_LIMIT)


def _mm(a, b, mode, tm, tn, out_dtype, name, column_shards=False):
    if mode == "nn":
        (M, K), N = a.shape, b.shape[1]
        dims = (((1,), (0,)), ((), ()))
        a_spec = pl.BlockSpec((tm, K), lambda i, j: (i, 0))
        b_spec = pl.BlockSpec((K, tn), lambda i, j: (0, j))
    elif mode == "nt":
        (M, K), N = a.shape, b.shape[0]
        dims = (((1,), (1,)), ((), ()))
        a_spec = pl.BlockSpec((tm, K), lambda i, j: (i, 0))
        b_spec = pl.BlockSpec((tn, K), lambda i, j: (j, 0))
    else:
        (K, M), N = a.shape, b.shape[1]
        dims = (((0,), (0,)), ((), ()))
        a_spec = pl.BlockSpec((K, tm), lambda i, j: (0, i))
        b_spec = pl.BlockSpec((K, tn), lambda i, j: (0, j))
    assert M % tm == 0 and N % tn == 0, (name, M, N, tm, tn)

    def body(a_ref, b_ref, o_ref):
        o_ref[...] = lax.dot_general(a_ref[...].astype(bf16), b_ref[...].astype(bf16), dims,
                                     preferred_element_type=f32).astype(o_ref.dtype)

    if column_shards:
        out_spec, out_shape = pl.BlockSpec((None, tm, tn), lambda i, j: (j, i, 0)), SDS((N // tn, M, tn), out_dtype)
    else:
        out_spec, out_shape = pl.BlockSpec((tm, tn), lambda i, j: (i, j)), SDS((M, N), out_dtype)
    return pl.pallas_call(body, grid=(M // tm, N // tn), in_specs=[a_spec, b_spec], out_specs=out_spec,
                          out_shape=out_shape, name=name, compiler_params=_params("parallel", "arbitrary"))(a, b)


def _row_spec(r, tm):
    if isinstance(r, tuple):
        arr, width, cb = r
        return arr, pl.BlockSpec((tm, width), lambda i, j, cb=cb: (i, cb + j))
    return r, pl.BlockSpec((tm, r.shape[1]), lambda i, j: (i, j))


def _full_spec(p):
    return pl.BlockSpec(p.shape, lambda i, j: (0,) * p.ndim)


def _rows(fn, rows, params, outs, name, tm=ROW_TILE, ncol=1):
    arrs, specs = zip(*[_row_spec(r, tm) for r in rows])
    S = arrs[0].shape[0]
    nr, npar = len(rows), len(params)

    def body(*refs):
        vals = fn(*[r[...].astype(f32) for r in refs[:nr]], *[p[...] for p in refs[nr:nr + npar]])
        for o_ref, v in zip(refs[nr + npar:], vals):
            o_ref[...] = v.astype(o_ref.dtype)

    return pl.pallas_call(
        body, grid=(S // tm, ncol), in_specs=list(specs) + [_full_spec(p) for p in params],
        out_specs=[pl.BlockSpec((tm, w), lambda i, j: (i, j)) for w, _ in outs],
        out_shape=[SDS((S, w * ncol), dt) for w, dt in outs], name=name,
        compiler_params=_params("parallel", "parallel"))(*arrs, *params)


def _rows_vjp(fn, rows, params, cts, wrt_rows, wrt_params, name, adds=None, tm=ROW_TILE, ncol=1, row_dtype=f32):
    adds = adds or {}
    arrs, specs = zip(*[_row_spec(r, tm) for r in rows])
    carrs, cspecs = zip(*[_row_spec(c, tm) for c in cts])
    add_keys = sorted(adds)
    aarrs = [adds[k] for k in add_keys]
    S = arrs[0].shape[0]
    nr, npar, nc, na = len(rows), len(params), len(cts), len(aarrs)
    widths = [specs[k].block_shape[1] for k in wrt_rows]
    row_dtypes = row_dtype if isinstance(row_dtype, (list, tuple)) else [row_dtype] * len(wrt_rows)

    def body(*refs):
        first = jnp.logical_and(pl.program_id(0) == 0, pl.program_id(1) == 0)
        rv = [r[...].astype(f32) for r in refs[:nr]]
        pv = [p[...] for p in refs[nr:nr + npar]]
        cv = tuple(c[...].astype(f32) for c in refs[nr + npar:nr + npar + nc])
        av = dict(zip(add_keys, refs[nr + npar + nc:nr + npar + nc + na]))
        o = refs[nr + npar + nc + na:]
        _, vjp = jax.vjp(fn, *rv, *pv)
        g = vjp(cv)
        for n, k in enumerate(wrt_rows):
            val = g[k]
            if k in av:
                val = val + av[k][...]
            o[n][...] = val.astype(o[n].dtype)
        for n, k in enumerate(wrt_params):
            ref = o[len(wrt_rows) + n]

            @pl.when(first)
            def _(ref=ref):
                ref[...] = jnp.zeros_like(ref)

            ref[...] += g[nr + k]

    res = pl.pallas_call(
        body, grid=(S // tm, ncol),
        in_specs=list(specs) + [_full_spec(p) for p in params] + list(cspecs)
        + [pl.BlockSpec((tm, a.shape[1] // ncol), lambda i, j: (i, j)) for a in aarrs],
        out_specs=[pl.BlockSpec((tm, w), lambda i, j: (i, j)) for w in widths] + [_full_spec(params[k]) for k in wrt_params],
        out_shape=[SDS((S, w * ncol), dt) for w, dt in zip(widths, row_dtypes)]
        + [SDS(params[k].shape, f32) for k in wrt_params],
        name=name, compiler_params=_params("arbitrary", "arbitrary"))(*arrs, *params, *carrs, *aarrs)
    return res[:len(wrt_rows)], res[len(wrt_rows):]


def _rms(x, w):
    return x * lax.rsqrt(jnp.mean(x * x, axis=-1, keepdims=True) + EPS) * w


def _rms_fn(x, w):
    return (_rms(x, w),)


def _res_rms_fn(f, res, w):
    return (res + _rms(f, w),)


def _res_rms_pre_fn(f, res, w_post, w_pre):
    x1 = res + _rms(f, w_post)
    return x1, _rms(x1, w_pre)


def _swap_halves(x):
    lane = lax.broadcasted_iota(jnp.int32, x.shape, 1)
    first = (lane % HEAD_DIM) < (HEAD_DIM // 2)
    n = x.shape[1]
    return jnp.where(first, pltpu.roll(x, n - HEAD_DIM // 2, 1), pltpu.roll(x, HEAD_DIM // 2, 1))


def _rope_fwd_fn(q, k, cos, sgn_sin):
    scale = HEAD_DIM ** -0.5
    return ((q * cos + _swap_halves(q) * sgn_sin) * scale, k * cos + _swap_halves(k) * sgn_sin)


def _rope_bwd_fn(dq, dk, cos, sgn_sin):
    dq = dq * (HEAD_DIM ** -0.5)
    return (dq * cos + _swap_halves(dq * sgn_sin), dk * cos + _swap_halves(dk * sgn_sin))


def _nt(a, b):
    return lax.dot_general(a, b, (((1,), (1,)), ((), ())), preferred_element_type=f32)


def _tn(a, b):
    return lax.dot_general(a, b, (((0,), (0,)), ((), ())), preferred_element_type=f32)


def _band_rows(j, d, nb):
    r, i = j // nb, j % nb
    if d == 1:
        cur = pl.ds(pl.multiple_of(i * 128, 128), 128)
        prev = pl.ds(pl.multiple_of(jnp.maximum(i - 1, 0) * 128, 128), 128)
    else:
        cur = pl.ds(i * (128 * d) + r, 128, stride=d)
        prev = pl.ds(jnp.maximum(i - 1, 0) * (128 * d) + r, 128, stride=d)
    return cur, prev, (i == 0).astype(jnp.int32)


def _band_bias(bias_ref):
    a = lax.broadcasted_iota(jnp.int32, (256, 256), 0) % 128
    c = lax.broadcasted_iota(jnp.int32, (256, 256), 1)
    own = jnp.logical_and(c < 128, c <= a)
    before = jnp.logical_and(c >= 128, c - 128 >= a)
    bias_ref[0] = jnp.where(jnp.logical_or(own, before), 0.0, NEG)
    bias_ref[1] = jnp.where(own, 0.0, NEG)


def _stack_heads(x, head_a):
    return jnp.concatenate([jnp.where(head_a, x, 0.0), jnp.where(head_a, 0.0, x)], axis=0)


def _unstack_heads(x2, head_a):
    return jnp.where(head_a, x2[:128], x2[128:])


def _pair_at(S, first_col):
    return pl.BlockSpec((S, 128), lambda h: (0, first_col // 128 + h))


def _attn_fwd(proj, cos, sgn_sin, name):
    S = proj.shape[0]
    nblk = S // 128

    def body(qp_ref, kp_ref, v_ref, cos_ref, sin_ref, out_ref, lse_ref, bias_ref, q_ref, k_ref, *scr):
        head_a = lax.broadcasted_iota(jnp.int32, (1, 128), 1) < HEAD_DIM
        _band_bias(bias_ref)
        q_ref[...], k_ref[...] = _rope_fwd_fn(qp_ref[...], kp_ref[...], cos_ref[...], sin_ref[...])
        for b, (d, nb) in enumerate(BRANCHES):
            ob_ref, lb_ref = scr[2 * b], scr[2 * b + 1]

            def blk(j, carry, d=d, nb=nb, ob_ref=ob_ref, lb_ref=lb_ref):
                cur, prev, first = _band_rows(j, d, nb)
                q2 = _stack_heads(q_ref[cur, :], head_a).astype(bf16)
                if nb == 1:
                    k2, v2, bias = k_ref[cur, :].astype(bf16), v_ref[cur, :].astype(bf16), bias_ref[1][:, :128]
                else:
                    k2 = jnp.concatenate([k_ref[cur, :], k_ref[prev, :]], axis=0).astype(bf16)
                    v2 = jnp.concatenate([v_ref[cur, :], v_ref[prev, :]], axis=0).astype(bf16)
                    bias = bias_ref[first]
                s = _nt(q2, k2) + bias
                mx = jnp.max(s, axis=1, keepdims=True)
                p = jnp.exp(s - mx)
                l = jnp.sum(p, axis=1, keepdims=True)
                o = jnp.dot(p.astype(bf16), v2, preferred_element_type=f32) / l
                ob_ref[cur, :] = _unstack_heads(o, head_a)
                lb_ref[cur, :] = _unstack_heads(jnp.broadcast_to(mx + jnp.log(l), (256, 128)), head_a)
                return carry

            lax.fori_loop(0, nblk, blk, 0, unroll=16)
        l0, l1, l2 = scr[1][...], scr[3][...], scr[5][...]
        mx = jnp.maximum(jnp.maximum(l0, l1), l2)
        e0, e1, e2 = jnp.exp(l0 - mx), jnp.exp(l1 - mx), jnp.exp(l2 - mx)
        den = e0 + e1 + e2
        out_ref[...] = ((e0 * scr[0][...] + e1 * scr[2][...] + e2 * scr[4][...]) / den).astype(out_ref.dtype)
        lse_ref[...] = mx + jnp.log(den)

    pair = pl.BlockSpec((S, 128), lambda h: (0, h))
    return pl.pallas_call(
        body, grid=(N_HEADS_A // 2,),
        in_specs=[pair, _pair_at(S, ATTN_W), _pair_at(S, 2 * ATTN_W), pair, pair], out_specs=[pair, pair],
        out_shape=[SDS((S, ATTN_W), bf16), SDS((S, ATTN_W), f32)],
        scratch_shapes=[pltpu.VMEM((2, 256, 256), f32)] + [pltpu.VMEM((S, 128), f32)] * 8,
        name=name, compiler_params=_params("parallel"))(proj, proj, proj, cos, sgn_sin)


def _attn_bwd(proj, cos, sgn_sin, dmix_in, out, lse, name):
    S = proj.shape[0]
    nblk = S // 128

    def body(qp_ref, kp_ref, v_ref, cos_ref, sin_ref, do_ref, out_ref, lse_ref, dq_ref, dk_ref, dv_ref,
             bias_ref, t_ref, q_ref, k_ref):
        head_a = lax.broadcasted_iota(jnp.int32, (1, 128), 1) < HEAD_DIM
        _band_bias(bias_ref)
        q_ref[...], k_ref[...] = _rope_fwd_fn(qp_ref[...], kp_ref[...], cos_ref[...], sin_ref[...])
        x = do_ref[...] * out_ref[...].astype(f32)
        t_ref[...] = jnp.where(head_a, jnp.sum(jnp.where(head_a, x, 0.0), axis=1, keepdims=True),
                               jnp.sum(jnp.where(head_a, 0.0, x), axis=1, keepdims=True))
        dq_ref[...] = jnp.zeros_like(dq_ref)
        dk_ref[...] = jnp.zeros_like(dk_ref)
        dv_ref[...] = jnp.zeros_like(dv_ref)
        for d, nb in BRANCHES:
            def blk(j, carry, d=d, nb=nb):
                cur, prev, first = _band_rows(j, d, nb)
                q2 = _stack_heads(q_ref[cur, :], head_a).astype(bf16)
                do2 = _stack_heads(do_ref[cur, :], head_a).astype(bf16)
                t, lse_b = t_ref[cur, :], lse_ref[cur, :]
                t2 = jnp.concatenate([t[:, :1], t[:, HEAD_DIM:HEAD_DIM + 1]], axis=0)
                lse2 = jnp.concatenate([lse_b[:, :1], lse_b[:, HEAD_DIM:HEAD_DIM + 1]], axis=0)
                if nb == 1:
                    k2, v2, bias = k_ref[cur, :].astype(bf16), v_ref[cur, :].astype(bf16), bias_ref[1][:, :128]
                else:
                    k2 = jnp.concatenate([k_ref[cur, :], k_ref[prev, :]], axis=0).astype(bf16)
                    v2 = jnp.concatenate([v_ref[cur, :], v_ref[prev, :]], axis=0).astype(bf16)
                    bias = bias_ref[first]
                p = jnp.exp(_nt(q2, k2) + bias - lse2)
                ds = (p * (_nt(do2, v2) - t2)).astype(bf16)
                dq_ref[cur, :] += _unstack_heads(jnp.dot(ds, k2, preferred_element_type=f32), head_a)
                dk2, dv2 = _tn(ds, q2), _tn(p.astype(bf16), do2)
                dk_ref[cur, :] += dk2[:128]
                dv_ref[cur, :] += dv2[:128]
                if nb != 1:
                    dk_ref[prev, :] += dk2[128:]
                    dv_ref[prev, :] += dv2[128:]
                return carry

            lax.fori_loop(0, nblk, blk, 0, unroll=16)
        dq_ref[...], dk_ref[...] = _rope_bwd_fn(dq_ref[...], dk_ref[...], cos_ref[...], sin_ref[...])

    pair = pl.BlockSpec((S, 128), lambda h: (0, h))
    return pl.pallas_call(
        body, grid=(N_HEADS_A // 2,),
        in_specs=[pair, _pair_at(S, ATTN_W), _pair_at(S, 2 * ATTN_W), pair, pair, pair, pair, pair],
        out_specs=[pair] * 3, out_shape=[SDS((S, ATTN_W), f32)] * 3,
        scratch_shapes=[pltpu.VMEM((2, 256, 256), f32)] + [pltpu.VMEM((S, 128), f32)] * 3,
        name=name, compiler_params=_params("parallel"))(proj, proj, proj, cos, sgn_sin, dmix_in, out, lse)


def _conv_val(x, w, K, rows):
    acc = x * w[K - 1:K, :]
    for s in range(1, K):
        acc = acc + jnp.where(rows >= s, pltpu.roll(x, s, 0), 0.0) * w[K - 1 - s:K - s, :]
    return acc


def _colconv_fwd(xs, ws, bs, K, fn, nblk, tc, outs, name):
    S = xs[0][0].shape[0]
    n = len(xs)
    has_b = bs is not None

    def body(*refs):
        rows = lax.broadcasted_iota(jnp.int32, (S, tc), 0)
        cs = []
        for k in range(n):
            c = _conv_val(refs[k][...].astype(f32), refs[n + k][...], K, rows)
            if has_b:
                c = c + refs[2 * n + k][...]
            cs.append(c)
        for o_ref, val in zip(refs[(3 if has_b else 2) * n:], fn(*cs)):
            o_ref[...] = val.astype(o_ref.dtype)

    def cspec(rows_, cb0):
        return pl.BlockSpec((rows_, tc), lambda j, cb0=cb0: (0, cb0 + j))

    in_specs = [cspec(S, cb) for _, cb in xs] + [cspec(K, cb) for _, cb in ws]
    args = [a for a, _ in xs] + [a for a, _ in ws]
    if has_b:
        in_specs += [cspec(1, cb) for _, cb in bs]
        args += [a for a, _ in bs]
    return pl.pallas_call(
        body, grid=(nblk,), in_specs=in_specs, out_specs=[cspec(S, 0) for _ in outs],
        out_shape=[SDS((S, nblk * tc), dt) for dt in outs], name=name, compiler_params=_params("parallel"))(*args)


def _colconv_bwd(xs, ws, bs, K, fn, douts, nblk, tc, name, dx_dtype=f32):
    S = xs[0][0].shape[0]
    n, nd = len(xs), len(douts)
    has_b = bs is not None
    nin = (3 if has_b else 2) * n

    def body(*refs):
        rows = lax.broadcasted_iota(jnp.int32, (S, tc), 0)
        x = [refs[k][...].astype(f32) for k in range(n)]
        w = [refs[n + k][...] for k in range(n)]
        cs = []
        for k in range(n):
            c = _conv_val(x[k], w[k], K, rows)
            if has_b:
                c = c + refs[2 * n + k][...]
            cs.append(c)
        _, vjp = jax.vjp(fn, *cs)
        dcs = vjp(tuple(r[...].astype(f32) for r in refs[nin:nin + nd]))
        o = refs[nin + nd:]
        for k in range(n):
            dc = dcs[k]
            dx = dc * w[k][K - 1:K, :]
            o[n + k][K - 1:K, :] = jnp.sum(dc * x[k], axis=0, keepdims=True)
            for s in range(1, K):
                dx = dx + jnp.where(rows < S - s, pltpu.roll(dc, S - s, 0), 0.0) * w[k][K - 1 - s:K - s, :]
                xsh = jnp.where(rows >= s, pltpu.roll(x[k], s, 0), 0.0)
                o[n + k][K - 1 - s:K - s, :] = jnp.sum(dc * xsh, axis=0, keepdims=True)
            o[k][...] = dx.astype(o[k].dtype)
            if has_b:
                o[2 * n + k][...] = jnp.sum(dc, axis=0, keepdims=True)

    def cspec(rows_, cb0):
        return pl.BlockSpec((rows_, tc), lambda j, cb0=cb0: (0, cb0 + j))

    in_specs = [cspec(S, cb) for _, cb in xs] + [cspec(K, cb) for _, cb in ws]
    args = [a for a, _ in xs] + [a for a, _ in ws]
    if has_b:
        in_specs += [cspec(1, cb) for _, cb in bs]
        args += [a for a, _ in bs]
    in_specs += [cspec(S, 0) for _ in douts]
    args += list(douts)
    W = nblk * tc
    out_specs = [cspec(S, 0)] * n + [cspec(K, 0)] * n + ([cspec(1, 0)] * n if has_b else [])
    out_shape = [SDS((S, W), dx_dtype)] * n + [SDS((K, W), f32)] * n + ([SDS((1, W), f32)] * n if has_b else [])
    res = pl.pallas_call(body, grid=(nblk,), in_specs=in_specs, out_specs=out_specs, out_shape=out_shape,
                         name=name, compiler_params=_params("parallel"))(*args)
    return res[:n], res[n:2 * n], res[2 * n:]


def _silu_fn(c):
    return (c * jax.nn.sigmoid(c),)


_GELU_C, _GELU_A = 0.7978845608028654, 0.044715


@jax.custom_vjp
def _geglu(gate, up):
    return 0.5 * gate * (1.0 + jnp.tanh(_GELU_C * (gate + _GELU_A * gate * gate * gate))) * up


def _geglu_vjp_fwd(gate, up):
    return _geglu(gate, up), (gate, up)


def _geglu_vjp_bwd(res, d):
    gate, up = res
    g2 = gate * gate
    t = jnp.tanh(_GELU_C * gate * (1.0 + _GELU_A * g2))
    h = 0.5 * (1.0 + t)
    dgelu = h + (0.5 * _GELU_C) * gate * (1.0 - t * t) * (1.0 + (3.0 * _GELU_A) * g2)
    return d * up * dgelu, d * (gate * h)


_geglu.defvjp(_geglu_vjp_fwd, _geglu_vjp_bwd)


def _geglu_fn(gate, up):
    return (_geglu(gate, up),)


def _softplus(x):
    u = jnp.exp(jnp.minimum(x, 20.0))
    small = u * (1.0 - 0.5 * u)
    return jnp.where(x > 20.0, x, jnp.where(u < 1e-4, small, jnp.log(1.0 + u)))


def _bmm(a, b, precision=None):
    return lax.dot_general(a, b, (((2,), (1,)), ((0,), (0,))), precision=precision, preferred_element_type=f32)


def _bnt(a, b, precision=None):
    return lax.dot_general(a, b, (((2,), (2,)), ((0,), (0,))), precision=precision, preferred_element_type=f32)


def _btn(a, b, precision=None):
    return lax.dot_general(a, b, (((1,), (1,)), ((0,), (0,))), precision=precision, preferred_element_type=f32)


@jax.custom_vjp
def _unit_lower_inverse(A):
    n = A.shape[-1]
    eye = (lax.broadcasted_iota(jnp.int32, (1, n, n), 1) == lax.broadcasted_iota(jnp.int32, (1, n, n), 2)).astype(f32)
    P = -A
    T = eye + P
    for _ in range(5):
        P = _bmm(P, P, HI)
        T = T + _bmm(T, P, HI)
    return T


def _unit_lower_inverse_fwd(A):
    T = _unit_lower_inverse(A)
    return T, T


def _unit_lower_inverse_bwd(T, dT):
    return (-_btn(T, _bnt(dT, T, HI), HI),)


_unit_lower_inverse.defvjp(_unit_lower_inverse_fwd, _unit_lower_inverse_bwd)


def _dn_prep_fn(q, k, v, ba, alog, dtb, h):
    G, C = q.shape[0], CHUNK
    lane = lax.broadcasted_iota(jnp.int32, (1, 1, 128), 2)

    def sel(arr, idx):
        return jnp.sum(jnp.where(lane == idx, arr, 0.0), axis=-1, keepdims=True)

    beta = jax.nn.sigmoid(sel(ba, h))
    g = -jnp.exp(sel(alog[None], h)) * _softplus(sel(ba, N_HEADS_D + h) + sel(dtb[None], h))
    qn = q * lax.rsqrt(jnp.sum(q * q, axis=-1, keepdims=True) + EPS) * (DK ** -0.5)
    kn = k * lax.rsqrt(jnp.sum(k * k, axis=-1, keepdims=True) + EPS)
    ii = lax.broadcasted_iota(jnp.int32, (1, C, C), 1)
    jj = lax.broadcasted_iota(jnp.int32, (1, C, C), 2)
    tril, strict = ii >= jj, ii > jj
    gsq = jnp.broadcast_to(g, (G, C, C))
    gcol = _bmm(jnp.broadcast_to(tril.astype(f32), (G, C, C)), gsq, HI)
    grow = _bmm(jnp.ones((G, C, C), f32), jnp.where(ii <= jj, gsq, 0.0), HI)
    decay = jnp.exp(jnp.where(tril, gcol - grow, NEG))
    gc = gcol[:, :, :1]
    glast = gcol[:, C - 1:C, :1]
    kb = kn * beta
    A = jnp.where(strict, _bnt(kb.astype(bf16), kn.astype(bf16)) * decay, 0.0)
    T = _unit_lower_inverse(A).astype(bf16)
    u = _bmm(T, (v * beta).astype(bf16))
    w = _bmm(T, (kb * jnp.exp(gc)).astype(bf16))
    qk = _bnt(qn.astype(bf16), kn.astype(bf16)) * decay
    qd = qn * jnp.exp(gc)
    kd = kn * jnp.exp(glast - gc)
    return u, w, qk, qd, kd, jnp.broadcast_to(jnp.exp(glast), (G, C, DK))


def _dn_scan_fn(u, w, qk, qd, kd, eg, St):
    b = lambda a: a.astype(bf16)
    vnew = u - _bmm(b(w), b(St))
    o = _bmm(b(qd), b(St)) + _bmm(b(qk), b(vnew))
    return o, St * eg[:, :1, :] + _btn(b(kd), b(vnew))


def _dn_post_fn(o, z, nw):
    return (_rms(o, nw) * (z * jax.nn.sigmoid(z)),)


DN_GROUP = 8


def _dn_prep_specs(S, rows):
    def col(first):
        return pl.BlockSpec((rows, DK), lambda i, h, first=first: (i, first // DK + h))

    par = pl.BlockSpec((1, 128), lambda i, h: (0, 0))
    return [col(0), col(N_HEADS_D * DK), col(2 * N_HEADS_D * DK),
            pl.BlockSpec((rows, 128), lambda i, h: (i, DN_BA_COL // 128)), par, par]


def _dn_prep(qkv, proj, alog, dtb, name):
    S = qkv.shape[0]
    G = DN_GROUP
    rows = G * CHUNK

    def body(q_ref, k_ref, v_ref, ba_ref, al_ref, dt_ref, u_ref, w_ref, qk_ref, qd_ref, kd_ref, eg_ref):
        h = pl.program_id(1)
        r3 = lambda ref: ref[...].reshape(G, CHUNK, 128)
        u, w, qk, qd, kd, eg = _dn_prep_fn(r3(q_ref), r3(k_ref), r3(v_ref), r3(ba_ref), al_ref[...], dt_ref[...], h)
        for ref, val in ((u_ref, u), (w_ref, w), (qd_ref, qd), (kd_ref, kd), (eg_ref, eg)):
            ref[...] = val.reshape(rows, DK)
        qk_ref[:, :CHUNK] = qk.reshape(rows, CHUNK)
        qk_ref[:, CHUNK:] = jnp.zeros((rows, DK - CHUNK), f32)

    out = pl.BlockSpec((rows, DK), lambda i, h: (i, h))
    return pl.pallas_call(
        body, grid=(S // rows, N_HEADS_D), in_specs=_dn_prep_specs(S, rows), out_specs=[out] * 6,
        out_shape=[SDS((S, N_HEADS_D * DK), f32)] * 6, name=name,
        compiler_params=_params("parallel", "parallel"))(qkv, qkv, qkv, proj, alog, dtb)


def _dn_prep_bwd(qkv, proj, alog, dtb, cts, name):
    S = qkv.shape[0]
    G = DN_GROUP
    rows = G * CHUNK

    def body(q_ref, k_ref, v_ref, ba_ref, al_ref, dt_ref, du_ref, dw_ref, dqk_ref, dqd_ref, dkd_ref, deg_ref,
             dq_ref, dk_ref, dv_ref, dba_ref, dal_ref, ddt_ref):
        i, h = pl.program_id(0), pl.program_id(1)
        r3 = lambda ref: ref[...].reshape(G, CHUNK, 128)
        _, vjp = jax.vjp(lambda q, k, v, ba, al, dt: _dn_prep_fn(q, k, v, ba, al, dt, h),
                         r3(q_ref), r3(k_ref), r3(v_ref), r3(ba_ref), al_ref[...], dt_ref[...])
        dqk = dqk_ref[:, :CHUNK].reshape(G, CHUNK, CHUNK)
        dq, dk, dv, dba, dal, ddt = vjp((r3(du_ref), r3(dw_ref), dqk, r3(dqd_ref), r3(dkd_ref), r3(deg_ref)))
        dq_ref[...] = dq.reshape(rows, DK)
        dk_ref[...] = dk.reshape(rows, DK)
        dv_ref[...] = dv.reshape(rows, DK)

        @pl.when(h == 0)
        def _():
            dba_ref[...] = jnp.zeros_like(dba_ref)

        @pl.when(jnp.logical_and(i == 0, h == 0))
        def _():
            dal_ref[...] = jnp.zeros_like(dal_ref)
            ddt_ref[...] = jnp.zeros_like(ddt_ref)

        dba_ref[...] += dba.reshape(rows, 128)
        dal_ref[...] += dal
        ddt_ref[...] += ddt

    hcol = pl.BlockSpec((rows, DK), lambda i, h: (i, h))
    par = pl.BlockSpec((1, 128), lambda i, h: (0, 0))
    W = N_HEADS_D * DK
    return pl.pallas_call(
        body, grid=(S // rows, N_HEADS_D), in_specs=_dn_prep_specs(S, rows) + [hcol] * 6,
        out_specs=[hcol] * 3 + [pl.BlockSpec((rows, 128), lambda i, h: (i, 0)), par, par],
        out_shape=[SDS((S, W), f32)] * 3 + [SDS((S, 128), f32), SDS((1, 128), f32), SDS((1, 128), f32)], name=name,
        compiler_params=_params("arbitrary", "arbitrary"))(qkv, qkv, qkv, proj, alog, dtb, *cts)


def _heads(x):
    return jnp.stack([x[:, DK * h:DK * (h + 1)] for h in range(N_HEADS_D)])


SCAN_CHUNKS = 8


def _dn_scan(pre, name):
    S = pre[0].shape[0]
    NCH = S // CHUNK
    rows = SCAN_CHUNKS * CHUNK

    def body(u_ref, w_ref, qk_ref, qd_ref, kd_ref, eg_ref, o_ref, st_ref, s_ref):
        @pl.when(pl.program_id(0) == 0)
        def _():
            s_ref[...] = jnp.zeros_like(s_ref)

        St = s_ref[...]
        for k in range(SCAN_CHUNKS):
            r = slice(k * CHUNK, (k + 1) * CHUNK)
            st_ref[k] = St
            o, St = _dn_scan_fn(_heads(u_ref[r, :]), _heads(w_ref[r, :]), _heads(qk_ref[r, :])[:, :, :CHUNK],
                                _heads(qd_ref[r, :]), _heads(kd_ref[r, :]), _heads(eg_ref[r, :]), St)
            for h in range(N_HEADS_D):
                o_ref[r, DK * h:DK * (h + 1)] = o[h]
        s_ref[...] = St

    blk = pl.BlockSpec((rows, N_HEADS_D * DK), lambda n: (n, 0))
    return pl.pallas_call(
        body, grid=(S // rows,), in_specs=[blk] * 6,
        out_specs=[blk, pl.BlockSpec((SCAN_CHUNKS, N_HEADS_D, DK, DK), lambda n: (n, 0, 0, 0))],
        out_shape=[SDS((S, N_HEADS_D * DK), f32), SDS((NCH, N_HEADS_D, DK, DK), f32)],
        scratch_shapes=[pltpu.VMEM((N_HEADS_D, DK, DK), f32)], name=name, compiler_params=_params("arbitrary"))(*pre)


def _dn_scan_bwd(pre, states, do, name):
    S = do.shape[0]
    rows = SCAN_CHUNKS * CHUNK
    steps = S // rows

    def body(u_ref, w_ref, qk_ref, qd_ref, kd_ref, eg_ref, st_ref, do_ref,
             du_ref, dw_ref, dqk_ref, dqd_ref, dkd_ref, deg_ref, ds_ref):
        @pl.when(pl.program_id(0) == 0)
        def _():
            ds_ref[...] = jnp.zeros_like(ds_ref)

        dS = ds_ref[...]
        for k in reversed(range(SCAN_CHUNKS)):
            r = slice(k * CHUNK, (k + 1) * CHUNK)
            _, vjp = jax.vjp(_dn_scan_fn, _heads(u_ref[r, :]), _heads(w_ref[r, :]), _heads(qk_ref[r, :])[:, :, :CHUNK],
                             _heads(qd_ref[r, :]), _heads(kd_ref[r, :]), _heads(eg_ref[r, :]), st_ref[k])
            du, dw, dqk, dqd, dkd, deg, dS = vjp((_heads(do_ref[r, :]), dS))
            for h in range(N_HEADS_D):
                c = slice(DK * h, DK * (h + 1))
                for ref, val in ((du_ref, du), (dw_ref, dw), (dqd_ref, dqd), (dkd_ref, dkd), (deg_ref, deg)):
                    ref[r, c] = val[h]
                dqk_ref[r, DK * h:DK * h + CHUNK] = dqk[h]
                dqk_ref[r, DK * h + CHUNK:DK * (h + 1)] = jnp.zeros((CHUNK, DK - CHUNK), f32)
        ds_ref[...] = dS

    blk = pl.BlockSpec((rows, N_HEADS_D * DK), lambda n: (steps - 1 - n, 0))
    return pl.pallas_call(
        body, grid=(steps,),
        in_specs=[blk] * 6 + [pl.BlockSpec((SCAN_CHUNKS, N_HEADS_D, DK, DK), lambda n: (steps - 1 - n, 0, 0, 0)), blk],
        out_specs=[blk] * 6, out_shape=[SDS((S, N_HEADS_D * DK), f32)] * 6,
        scratch_shapes=[pltpu.VMEM((N_HEADS_D, DK, DK), f32)], name=name,
        compiler_params=_params("arbitrary"))(*pre, states, do)


def _loss_head(y, t, name):
    S, D = y.shape
    tm = ROW_TILE

    def body(y_ref, t_ref, dy_ref, l_ref):
        i = pl.program_id(0)
        d = y_ref[...] - t_ref[...]
        dy_ref[...] = d * (1.0 / D)
        part = jnp.sum(jnp.sum(d * d, axis=1, keepdims=True), axis=0, keepdims=True) * (0.5 / D)

        @pl.when(i == 0)
        def _():
            l_ref[...] = jnp.zeros_like(l_ref)

        l_ref[...] += jnp.broadcast_to(part, l_ref.shape)

    spec = pl.BlockSpec((tm, D), lambda i: (i, 0))
    dy, l = pl.pallas_call(body, grid=(S // tm,), in_specs=[spec, spec],
                           out_specs=[spec, pl.BlockSpec((1, 128), lambda i: (0, 0))],
                           out_shape=[SDS((S, D), f32), SDS((1, 128), f32)], name=name,
                           compiler_params=_params("arbitrary"))(y, t)
    return l[0, 0], dy


def _adamw_refs(w_ref, g_ref, m_ref, v_ref, d_ref, mo_ref, vo_ref):
    gv = g_ref[...]
    m2 = ADAM_B1 * m_ref[...] + (1.0 - ADAM_B1) * gv
    v2 = ADAM_B2 * v_ref[...] + (1.0 - ADAM_B2) * (gv * gv)
    m_hat = m2 / (1.0 - ADAM_B1 ** ADAM_STEP)
    v_hat = v2 / (1.0 - ADAM_B2 ** ADAM_STEP)
    d_ref[...] = -ADAM_LR * (m_hat / (jnp.sqrt(v_hat) + ADAM_EPS) + ADAM_WD * w_ref[...])
    mo_ref[...] = m2
    vo_ref[...] = v2


def _adamw_small(ws, gs, ms, vs, name):
    n = len(ws)

    def body(*refs):
        for i in range(n):
            _adamw_refs(*[refs[k * n + i] for k in range(7)])

    res = pl.pallas_call(body, out_shape=[SDS(a.shape, f32) for a in ws] * 3, name=name)(*ws, *gs, *ms, *vs)
    return res[:n], res[n:2 * n], res[2 * n:]


def _adamw(w, g, m, v, tr, name):
    L, R, C = w.shape
    assert R % tr == 0

    def body(*refs):
        _adamw_refs(*refs)

    spec = pl.BlockSpec((1, tr, C), lambda l, i: (l, i, 0))
    return pl.pallas_call(body, grid=(L, R // tr), in_specs=[spec] * 4, out_specs=[spec] * 3,
                          out_shape=[SDS((L, R, C), f32)] * 3, name=name,
                          compiler_params=_params("parallel", "parallel"))(w, g, m, v)


def _rope_tables(S):
    inv = 1.0 / (10000.0 ** (jnp.arange(0, HEAD_DIM, 2, dtype=f32) / HEAD_DIM))
    ang = jnp.arange(S, dtype=f32)[:, None] * inv[None, :]
    cos, sin = jnp.cos(ang), jnp.sin(ang)
    return (jnp.tile(jnp.concatenate([cos, cos], axis=1), (1, N_HEADS_A)),
            jnp.tile(jnp.concatenate([-sin, sin], axis=1), (1, N_HEADS_A)))


def _layer_fwd(x, W, cos, sgn_sin, l, late_weights=None, h1=None, next_pre_mix=None, mid_mixer=None, after_ffn_in=None):
    n = f"l{l}_"
    if h1 is None:
        (h1,) = _rows(_rms_fn, [x], [W["norm_pre_mix"]], [(D_MODEL, bf16)], n + "pre_mix_norm")
    proj = _mm(h1, W["w_in"], "nn", 1024, 768, f32, n + "in_proj")
    attn_out, lse = _attn_fwd(proj, cos, sgn_sin, n + "attn_fwd")
    (qkv,) = _colconv_fwd([(proj, DN_QKV_COL // 512)], [(W["dn_conv_w"], 0)], None, 4, _silu_fn, 3, 512, [f32], n + "dn_conv")
    dn_pre = _dn_prep(qkv, proj, W["dn_a_log"], W["dn_dt_bias"], n + "dn_prep")
    if mid_mixer is not None:
        W = dict(W, dn_norm_w=W["dn_norm_w"] + mid_mixer(dn_pre[0]))
    dn_o, dn_states = _dn_scan(dn_pre, n + "dn_scan")
    (dn_out,) = _rows(_dn_post_fn, [(dn_o, DK, 0), (proj, DK, DN_Z_COL // DK)], [W["dn_norm_w"]], [(DK, bf16)], n + "dn_post",
                      ncol=N_HEADS_D, tm=4 * ROW_TILE)
    mix_in = jnp.concatenate([attn_out, dn_out], axis=1)
    late = late_weights(mix_in) if late_weights is not None else {}
    W = {**W, **late}
    mix = _mm(mix_in, W["w_out"], "nn", 512, 512, f32, n + "out_proj")
    x1, h2 = _rows(_res_rms_pre_fn, [mix, x], [W["norm_post_mix"], W["norm_pre_ffn"]],
                   [(D_MODEL, f32), (D_MODEL, bf16)], n + "post_mix_pre_ffn_norm")
    u0 = _mm(h2, W["ffn_w_in"], "nn", 1024, D_FF // 2, bf16, n + "ffn_in")
    if after_ffn_in is not None:
        W = dict(W, ffn_conv_b=W["ffn_conv_b"] + after_ffn_in(u0))
    nb_ff = D_FF // 256
    (act,) = _colconv_fwd([(u0, 0), (u0, nb_ff)], [(W["ffn_conv_w"], 0), (W["ffn_conv_w"], nb_ff)],
                          [(W["ffn_conv_b"], 0), (W["ffn_conv_b"], nb_ff)], 3, _geglu_fn, nb_ff, 256, [bf16],
                          n + "ffn_conv_glu")
    f = _mm(act, W["ffn_w_out"], "nn", 512, 512, f32, n + "ffn_out")
    if next_pre_mix is None:
        (x2,), h1_next = _rows(_res_rms_fn, [f, x1], [W["norm_post_ffn"]], [(D_MODEL, f32)], n + "post_ffn_norm"), None
    else:
        x2, h1_next = _rows(_res_rms_pre_fn, [f, x1], [W["norm_post_ffn"], next_pre_mix],
                            [(D_MODEL, f32), (D_MODEL, bf16)], n + "post_ffn_next_pre_mix_norm")
    saved = dict(x=x, h1=h1, proj=proj, attn_out=attn_out, lse=lse, qkv=qkv, dn_pre=dn_pre, dn_o=dn_o,
                 dn_states=dn_states, mix_in=mix_in, mix=mix, x1=x1, h2=h2, u0=u0, act=act, f=f, late=late)
    return x2, h1_next, saved


def _layer_bwd(dx2, sv, W, cos, sgn_sin, l, after_ffn=None, next_layer=None, first_layer=True):
    n = f"l{l}_"
    S = dx2.shape[0]
    g = {}
    g_next_pre = None
    if next_layer is None:
        (df,), (g["norm_post_ffn"],) = _rows_vjp(_rms_fn, [sv["f"]], [W["norm_post_ffn"]], [dx2], [0], [0],
                                                 n + "post_ffn_norm_bwd", row_dtype=bf16)
    else:
        (df, dx2), (g["norm_post_ffn"], g_next_pre) = _rows_vjp(
            _res_rms_pre_fn, [sv["f"], sv["x1"]], [W["norm_post_ffn"], next_layer[1]], [dx2, next_layer[0]], [0, 1], [0, 1],
            n + "post_ffn_next_pre_mix_norm_bwd", row_dtype=[bf16, f32])
    dact = _mm(df, W["ffn_w_out"], "nt", 512, D_FF // 2, f32, n + "ffn_out_dx")
    g["ffn_w_out"] = _mm(sv["act"], df, "tn", 256, 1024, f32, n + "ffn_out_dw")
    nb_ff = D_FF // 256
    u0 = sv["u0"]
    dxs, dws, dbs = _colconv_bwd([(u0, 0), (u0, nb_ff)], [(W["ffn_conv_w"], 0), (W["ffn_conv_w"], nb_ff)],
                                 [(W["ffn_conv_b"], 0), (W["ffn_conv_b"], nb_ff)], 3, _geglu_fn, [dact], nb_ff, 256,
                                 n + "ffn_conv_glu_bwd", dx_dtype=bf16)
    du0 = jnp.concatenate(dxs, axis=1)
    g["ffn_conv_w"] = jnp.concatenate(dws, axis=1)
    g["ffn_conv_b"] = jnp.concatenate(dbs, axis=1)
    dh2 = _mm(du0, W["ffn_w_in"], "nt", 512, 512, f32, n + "ffn_in_dx")
    g["ffn_w_in"] = _mm(sv["h2"], du0, "tn", 512, D_FF // 2, f32, n + "ffn_in_dw", column_shards=True)
    if after_ffn is not None:
        W = dict(W, norm_post_mix=W["norm_post_mix"] + after_ffn(g, dh2))
    (dmix, dx1), (g["norm_post_mix"], g["norm_pre_ffn"]) = _rows_vjp(
        _res_rms_pre_fn, [sv["mix"], sv["x"]], [W["norm_post_mix"], W["norm_pre_ffn"]], [dx2, dh2], [0, 1], [0, 1],
        n + "post_mix_pre_ffn_norm_bwd", row_dtype=[bf16, f32])
    dmix_in = _mm(dmix, W["w_out"], "nt", 512, 512, f32, n + "out_proj_dx")
    g["w_out"] = _mm(sv["mix_in"], dmix, "tn", 512, 512, f32, n + "out_proj_dw")

    (ddn_o, dz), (g["dn_norm_w"],) = _rows_vjp(
        _dn_post_fn, [(sv["dn_o"], DK, 0), (sv["proj"], DK, DN_Z_COL // DK)], [W["dn_norm_w"]], [(dmix_in, DK, ATTN_W // DK)],
        [0, 1], [0], n + "dn_post_bwd", ncol=N_HEADS_D, tm=4 * ROW_TILE)
    dpre = _dn_scan_bwd(sv["dn_pre"], sv["dn_states"], ddn_o, n + "dn_scan_bwd")
    dq, dk, dv, dba, g["dn_a_log"], g["dn_dt_bias"] = _dn_prep_bwd(
        sv["qkv"], sv["proj"], W["dn_a_log"], W["dn_dt_bias"], dpre, n + "dn_prep_bwd")
    dqkv = jnp.concatenate([dq, dk, dv], axis=1)
    (dqkv0,), (g["dn_conv_w"],), _ = _colconv_bwd([(sv["proj"], DN_QKV_COL // 512)], [(W["dn_conv_w"], 0)], None, 4, _silu_fn,
                                                 [dqkv], 3, 512, n + "dn_conv_bwd")

    daq, dak, dav = _attn_bwd(sv["proj"], cos, sgn_sin, dmix_in, sv["attn_out"], sv["lse"], n + "attn_bwd")
    dproj = jnp.concatenate([daq, dak, dav, dqkv0, dz, dba, jnp.zeros((S, PROJ_W - DN_BA_COL - 128), f32)], axis=1).astype(bf16)
    dh1 = _mm(dproj, W["w_in"], "nt", 512, 512, f32, n + "in_proj_dx")
    g["w_in"] = _mm(sv["h1"], dproj, "tn", 512, 768, f32, n + "in_proj_dw")
    if not first_layer:
        return (dx1, dh1), g, g_next_pre
    (dx,), (g["norm_pre_mix"],) = _rows_vjp(_rms_fn, [sv["x"]], [W["norm_pre_mix"]], [dh1], [0], [0],
                                            n + "pre_mix_norm_bwd", adds={0: dx1})
    return dx, g, g_next_pre


def _local_step(x, target, layers):
    cos, sgn_sin = _rope_tables(x.shape[0])
    saved, h1 = [], None
    for l, W in enumerate(layers):
        nxt = layers[l + 1]["norm_pre_mix"] if l + 1 < len(layers) else None
        x, h1, sv = _layer_fwd(x, W, cos, sgn_sin, l, h1=h1, next_pre_mix=nxt)
        saved.append(sv)
    loss, dx = _loss_head(x, target, "loss_head")
    grads = [None] * len(layers)
    nxt = None
    for l in reversed(range(len(layers))):
        dx, grads[l], g_pre = _layer_bwd(dx, saved[l], layers[l], cos, sgn_sin, l, next_layer=nxt, first_layer=(l == 0))
        if g_pre is not None:
            grads[l + 1]["norm_pre_mix"] = g_pre
        if l > 0:
            dx, dh1 = dx
            nxt = (dh1, layers[l]["norm_pre_mix"])
    return loss, dx, grads


def _pos():
    x, y, c = lax.axis_index("x"), lax.axis_index("y"), lax.axis_index("c")
    return x, y, c, [(1 - x, y), (x, 1 - y), (1 - x, 1 - y)]


def _rcopy(src, dst, send_sem, recv_sem, dev):
    return pltpu.make_async_remote_copy(src_ref=src, dst_ref=dst, send_sem=send_sem, recv_sem=recv_sem,
                                        device_id=dev, device_id_type=MESH)


def _half_rows(ref, h, which, axis):
    if h is None:
        return ref
    rows = pl.ds(pl.multiple_of(which * h, 16), h)
    return ref.at[:, rows, :] if axis == 1 else ref.at[rows, :]


def _dma_sems(*counts):
    return [pltpu.SemaphoreType.DMA((k,)) for k in counts]


def _all_gather(arrs, halves, name):
    n = len(arrs)

    def body(*refs):
        ins, outs = refs[:n], refs[n:2 * n]
        send1, recv1, send2, recv2 = refs[2 * n:]
        x, y, c, chips = _pos()
        me, sib, s_me = (x, y, c), (x, y, 1 - c), 2 * x + y
        sends = []
        for i in range(n):
            for j, chip in enumerate(chips):
                cp = _rcopy(_half_rows(ins[i], halves[i], c, 1), _half_rows(outs[i].at[s_me], halves[i], c, 1),
                            send1.at[3 * i + j], recv1.at[3 * i + j], (*chip, c))
                cp.start()
                sends.append(cp)
        for i in range(n):
            for j, (px, py) in enumerate(chips):
                k = 3 * i + j
                landed = _half_rows(outs[i].at[2 * px + py], halves[i], c, 1)
                _rcopy(landed, landed, send1.at[k], recv1.at[k], me).wait_recv()
                if halves[i] is not None:
                    cp = _rcopy(landed, landed, send2.at[k], recv2.at[k], sib)
                    cp.start()
                    sends.append(cp)
        for i in range(n):
            if halves[i] is None:
                continue
            for j, (px, py) in enumerate(chips):
                k = 3 * i + j
                other = _half_rows(outs[i].at[2 * px + py], halves[i], 1 - c, 1)
                _rcopy(other, other, send2.at[k], recv2.at[k], me).wait_recv()
        for cp in sends:
            cp.wait_send()

    return pl.pallas_call(
        body, in_specs=[ANY] * n, out_specs=[ANY] * n,
        out_shape=[SDS((4,) + a.shape, a.dtype) for a in arrs],
        scratch_shapes=_dma_sems(3 * n, 3 * n, 3 * n, 3 * n), name=name)(*arrs)


HBM = pl.BlockSpec(memory_space=pltpu.HBM)
SEM = pl.BlockSpec(memory_space=pltpu.SEMAPHORE)
_EFFECT = pltpu.SideEffectType.DATAFLOW_SIDE_EFFECTING


def _in_hbm(a):
    return pltpu.with_memory_space_constraint(a, pltpu.HBM)


def _split_copy(srcs, land_shapes, plan, per, after, name):
    n, nl = len(srcs), len(land_shapes)
    k = per * n

    def body(*refs):
        ins, lands, token = refs[:n], refs[n:n + nl], refs[-1]
        send, recv = refs[n + nl + 1], refs[n + nl + 2]
        for i, (src, dst, dev, _) in enumerate(plan(ins, lands)):
            _rcopy(src, dst, send.at[i], recv.at[i], dev).start()
        token[...] = jnp.zeros_like(token)

    lands = [_in_hbm(lax.empty(s.shape, s.dtype)) for s in land_shapes]
    return pl.pallas_call(
        body, name=name,
        out_shape=(pltpu.SemaphoreType.DMA((k,)), pltpu.SemaphoreType.DMA((k,)),
                   *[pltpu.HBM(a.shape, a.dtype) for a in srcs], *[pltpu.HBM(s.shape, s.dtype) for s in land_shapes],
                   SDS((8, 128), f32)),
        in_specs=[HBM] * (n + nl) + [ANY], out_specs=(SEM, SEM, *[HBM] * (n + nl), pl.BlockSpec(memory_space=pltpu.VMEM)),
        input_output_aliases={i: 2 + i for i in range(n + nl)},
        compiler_params=pltpu.CompilerParams(has_side_effects=_EFFECT))(*[_in_hbm(a) for a in srcs], *lands, after)


def _split_wait(started, n, plan, after, name, nl=None):
    nl = n if nl is None else nl
    send, recv = started[0], started[1]
    thru = started[2:2 + n + nl]

    def body(*refs):
        ins, lands = refs[:n], refs[n:n + nl]
        send_ref, recv_ref = refs[n + nl], refs[n + nl + 1]
        for i, (src, _, dev, mine) in enumerate(plan(ins, lands)):
            cp = _rcopy(src, mine, send_ref.at[i], recv_ref.at[i], dev)
            cp.wait_send()
            cp.wait_recv()

    res = pl.pallas_call(
        body, name=name, out_shape=tuple(pltpu.HBM(a.shape, a.dtype) for a in thru),
        in_specs=[HBM] * (n + nl) + [SEM, SEM, ANY], out_specs=tuple([HBM] * (n + nl)),
        input_output_aliases={i: i for i in range(n + nl)},
        compiler_params=pltpu.CompilerParams(has_side_effects=_EFFECT))(*thru, send, recv, after)
    return res[:n], res[n:]


def _gather_plan(halves):
    def plan(ins, lands):
        x, y, c, chips = _pos()
        out = []
        for i in range(len(ins)):
            for px, py in chips:
                out.append((_half_rows(ins[i], halves[i], c, 1), _half_rows(lands[i].at[2 * x + y], halves[i], c, 1),
                            (px, py, c), _half_rows(lands[i].at[2 * px + py], halves[i], c, 1)))
        return out
    return plan


def _scatter_plan(ins, lands):
    x, y, c, chips = _pos()
    out = []
    for i in range(len(ins)):
        for j, (px, py) in enumerate(chips):
            out.append((ins[i].at[2 * px + py], lands[i].at[j], (px, py, c), lands[i].at[j]))
    return out


def _exchange_plan(ins, lands):
    x, y, c, _ = _pos()
    return [(_half_rows(g, g.shape[1] // 2, 1 - c, 1), land, (x, y, 1 - c), land) for g, land in zip(ins, lands)]


def _pass_plan(halves):
    def plan(ins, lands):
        x, y, c, chips = _pos()
        out = []
        for i in range(len(ins)):
            for px, py in chips:
                slot = ins[i].at[2 * px + py]
                out.append((_half_rows(slot, halves[i], c, 1), _half_rows(slot, halves[i], c, 1), (x, y, 1 - c),
                            _half_rows(slot, halves[i], 1 - c, 1)))
        return out
    return plan


def _exchange_halves(gs, name):
    n = len(gs)

    def body(*refs):
        ins, outs = refs[:n], refs[n:2 * n]
        send, recv = refs[2 * n:]
        x, y, c, _ = _pos()
        sends = []
        for k in range(n):
            cp = _rcopy(_half_rows(ins[k], gs[k].shape[1] // 2, 1 - c, 1), outs[k], send.at[k], recv.at[k], (x, y, 1 - c))
            cp.start()
            sends.append(cp)
        for k in range(n):
            _rcopy(outs[k], outs[k], send.at[k], recv.at[k], (x, y, c)).wait_recv()
        for cp in sends:
            cp.wait_send()

    return pl.pallas_call(
        body, in_specs=[ANY] * n, out_specs=[ANY] * n,
        out_shape=[SDS((4, g.shape[1] // 2, g.shape[2]), g.dtype) for g in gs],
        scratch_shapes=_dma_sems(n, n), name=name)(*gs)


def _scatter_partials(ps, name):
    n = len(ps)

    def body(*refs):
        ins, outs = refs[:n], refs[n:2 * n]
        send, recv = refs[2 * n:]
        x, y, c, chips = _pos()
        sends = []
        for k in range(n):
            for j, (px, py) in enumerate(chips):
                cp = _rcopy(ins[k].at[2 * px + py], outs[k].at[j], send.at[3 * k + j], recv.at[3 * k + j], (px, py, c))
                cp.start()
                sends.append(cp)
        for k in range(n):
            for j in range(3):
                _rcopy(outs[k].at[j], outs[k].at[j], send.at[3 * k + j], recv.at[3 * k + j], (x, y, c)).wait_recv()
        for cp in sends:
            cp.wait_send()

    return pl.pallas_call(
        body, in_specs=[ANY] * n, out_specs=[ANY] * n,
        out_shape=[SDS((3,) + p.shape[1:], p.dtype) for p in ps],
        scratch_shapes=_dma_sems(3 * n, 3 * n), name=name)(*ps)


def _join_halves(rs, layers, name):
    n = len(rs)

    def body(*refs):
        outs = refs[n:2 * n]
        send, recv = refs[2 * n:]
        x, y, c, _ = _pos()

        def half(k, which):
            h = rs[k].shape[1] // 2
            return _half_rows(outs[k], h, which, 1) if layers[k] is None else _half_rows(outs[k].at[layers[k]], h, which, 0)

        sends = []
        for k in range(n):
            cp = _rcopy(half(k, c), half(k, c), send.at[k], recv.at[k], (x, y, 1 - c))
            cp.start()
            sends.append(cp)
        for k in range(n):
            _rcopy(half(k, 1 - c), half(k, 1 - c), send.at[k], recv.at[k], (x, y, c)).wait_recv()
        for cp in sends:
            cp.wait_send()

    return pl.pallas_call(
        body, in_specs=[ANY] * n, out_specs=[ANY] * n, out_shape=[SDS(r.shape, r.dtype) for r in rs],
        input_output_aliases={k: k for k in range(n)}, scratch_shapes=_dma_sems(n, n), name=name)(*rs)


def _all_reduce_small(pack, name):
    R = pack.shape[0]

    def body(in_ref, out_ref, buf, send, recv):
        x, y, c, _ = _pos()
        me = 4 * x + 2 * y + c
        buf[me] = in_ref[...]
        sends = []
        for k in range(1, 8):
            peer = me ^ k
            cp = _rcopy(buf.at[me], buf.at[me], send.at[k - 1], recv.at[k - 1], ((peer >> 2) & 1, (peer >> 1) & 1, peer & 1))
            cp.start()
            sends.append(cp)
        for k in range(1, 8):
            _rcopy(buf.at[me ^ k], buf.at[me ^ k], send.at[k - 1], recv.at[k - 1], (x, y, c)).wait_recv()
        for cp in sends:
            cp.wait_send()
        acc = buf[0]
        for d in range(1, 8):
            acc = acc + buf[d]
        out_ref[...] = acc

    return pl.pallas_call(
        body, out_shape=SDS((R, 128), f32),
        in_specs=[pl.BlockSpec(memory_space=pltpu.VMEM)], out_specs=pl.BlockSpec(memory_space=pltpu.VMEM),
        scratch_shapes=[pltpu.VMEM((8, R, 128), f32)] + _dma_sems(7, 7), name=name)(pack)


def _add_sibling(g, recv, c_arr, tr, name):
    _, R, C = g.shape
    h = R // 2
    nrb = h // tr
    assert h % tr == 0

    def body(c_ref, g_ref, r_ref, o_ref):
        o_ref[...] = (g_ref[...] + r_ref[...]).astype(o_ref.dtype)

    spec = pl.BlockSpec((1, tr, C), lambda s, r, c_ref: (s, r, 0))
    grid_spec = pltpu.PrefetchScalarGridSpec(
        num_scalar_prefetch=1, grid=(4, nrb),
        in_specs=[pl.BlockSpec((1, tr, C), lambda s, r, c_ref: (s, c_ref[0] * nrb + r, 0)), spec], out_specs=spec)
    return pl.pallas_call(body, grid_spec=grid_spec, out_shape=SDS((4, h, C), bf16), name=name,
                          compiler_params=_params("parallel", "parallel"))(c_arr, g, recv)


def _add_chips(p, recv, sc_arr, tr, layer, into, name):
    _, h, C = p.shape
    nrb = h // tr
    assert h % tr == 0

    def body(sc_ref, p_ref, r_ref, *rest):
        rest[-1][...] = (p_ref[0].astype(f32) + r_ref[0].astype(f32)) + (r_ref[1].astype(f32) + r_ref[2].astype(f32))

    grid_spec = pltpu.PrefetchScalarGridSpec(
        num_scalar_prefetch=1, grid=(nrb,),
        in_specs=[pl.BlockSpec((1, tr, C), lambda r, sc_ref: (sc_ref[0], r, 0)),
                  pl.BlockSpec((3, tr, C), lambda r, sc_ref: (0, r, 0))] + ([] if into is None else [ANY]),
        out_specs=pl.BlockSpec((None, tr, C), lambda r, sc_ref: (layer, sc_ref[1] * nrb + r, 0)))
    return pl.pallas_call(body, grid_spec=grid_spec, out_shape=SDS((2, 2 * h, C), f32), name=name,
                          input_output_aliases={} if into is None else {3: 0},
                          compiler_params=_params("parallel"))(sc_arr, p, recv, *([] if into is None else [into]))


_BIG = (("w_in", 1024, 256), ("w_out", 256, 128), ("ffn_w_in", 1024, 256), ("ffn_w_out", 704, 352))
_SMALL = ("dn_conv_w", "ffn_conv_w", "ffn_conv_b", "norm_pre_mix", "norm_post_mix", "norm_pre_ffn", "norm_post_ffn",
          "dn_norm_w", "dn_a_log", "dn_dt_bias")
_WEIGHTS = ("w_in", "dn_conv_w", "dn_a_log", "dn_dt_bias", "dn_norm_w", "w_out", "ffn_w_in", "ffn_conv_w", "ffn_conv_b",
            "ffn_w_out", "norm_pre_mix", "norm_post_mix", "norm_pre_ffn", "norm_post_ffn")
_ADAM_ROWS = {"w_in": 256, "w_out": 256, "ffn_w_in": 128, "ffn_w_out": 176}


def _shard_major(name, g):
    if name == "w_in":
        width = IN_COLS // N_SHARDS
        return jnp.stack([g[:, width * s:width * (s + 1)] for s in range(N_SHARDS)])
    if name == "ffn_w_in":
        return g
    return g.reshape(4, g.shape[0] // 4, g.shape[1])


def kernel(x, w_in, dn_conv_w, dn_a_log, dn_dt_bias, dn_norm_w, w_out, ffn_w_in, ffn_conv_w, ffn_conv_b, ffn_w_out, norm_pre_mix, norm_post_mix, norm_pre_ffn, norm_post_ffn, loss_target, m_w_in, m_dn_conv_w, m_dn_a_log, m_dn_dt_bias, m_dn_norm_w, m_w_out, m_ffn_w_in, m_ffn_conv_w, m_ffn_conv_b, m_ffn_w_out, m_norm_pre_mix, m_norm_post_mix, m_norm_pre_ffn, m_norm_post_ffn, v_w_in, v_dn_conv_w, v_dn_a_log, v_dn_dt_bias, v_dn_norm_w, v_w_out, v_ffn_w_in, v_ffn_conv_w, v_ffn_conv_b, v_ffn_w_out, v_norm_pre_mix, v_norm_post_mix, v_norm_pre_ffn, v_norm_post_ffn):
    w = dict(w_in=w_in, dn_conv_w=dn_conv_w, dn_a_log=dn_a_log, dn_dt_bias=dn_dt_bias, dn_norm_w=dn_norm_w, w_out=w_out,
             ffn_w_in=ffn_w_in, ffn_conv_w=ffn_conv_w, ffn_conv_b=ffn_conv_b, ffn_w_out=ffn_w_out, norm_pre_mix=norm_pre_mix,
             norm_post_mix=norm_post_mix, norm_pre_ffn=norm_pre_ffn, norm_post_ffn=norm_post_ffn)
    m = dict(w_in=m_w_in, dn_conv_w=m_dn_conv_w, dn_a_log=m_dn_a_log, dn_dt_bias=m_dn_dt_bias, dn_norm_w=m_dn_norm_w,
             w_out=m_w_out, ffn_w_in=m_ffn_w_in, ffn_conv_w=m_ffn_conv_w, ffn_conv_b=m_ffn_conv_b, ffn_w_out=m_ffn_w_out,
             norm_pre_mix=m_norm_pre_mix, norm_post_mix=m_norm_post_mix, norm_pre_ffn=m_norm_pre_ffn,
             norm_post_ffn=m_norm_post_ffn)
    v = dict(w_in=v_w_in, dn_conv_w=v_dn_conv_w, dn_a_log=v_dn_a_log, dn_dt_bias=v_dn_dt_bias, dn_norm_w=v_dn_norm_w,
             w_out=v_w_out, ffn_w_in=v_ffn_w_in, ffn_conv_w=v_ffn_conv_w, ffn_conv_b=v_ffn_conv_b, ffn_w_out=v_ffn_w_out,
             norm_pre_mix=v_norm_pre_mix, norm_post_mix=v_norm_post_mix, norm_pre_ffn=v_norm_pre_ffn,
             norm_post_ffn=v_norm_post_ffn)
    xi, yi, ci = lax.axis_index("x"), lax.axis_index("y"), lax.axis_index("c")
    s_me = 2 * xi + yi
    c_arr = jnp.reshape(ci, (1,)).astype(jnp.int32)
    sc_arr = jnp.stack([s_me, ci]).astype(jnp.int32)

    mats = [name for name, _, _ in _BIG]
    rest = mats[1:]
    half_of = {name: rows // 2 for name, rows, _ in _BIG}
    tiles = {name: tr for name, _, tr in _BIG}
    gathered_shape = lambda a: SDS((4,) + a.shape, a.dtype)

    own = {k: w[k].astype(bf16) for k in mats}
    plan_in = _gather_plan([half_of["w_in"], None, None])
    src_in = [own["w_in"][0:1], dn_conv_w, ffn_conv_w]
    started_in = _split_copy(src_in, [gathered_shape(a) for a in src_in], plan_in, 3, src_in[0], "weights_gather_w_in0_start")
    plan0 = _gather_plan([half_of[k] for k in rest])
    src0 = [own[k][0:1] for k in rest]
    started0 = _split_copy(src0, [gathered_shape(a) for a in src0], plan0, 3, started_in[-1], "weights_gather_l0_start")
    plan1 = _gather_plan([half_of[k] for k in mats])
    src1 = [own[k][1:2] for k in mats]
    started1 = _split_copy(src1, [gathered_shape(a) for a in src1], plan1, 3, started0[-1], "weights_gather_l1_start")
    _, landed_in = _split_wait(started_in, len(src_in), plan_in, started1[-1], "weights_gather_w_in0_wait")
    pass_in = _pass_plan([half_of["w_in"]])
    passed_in = _split_copy(landed_in[:1], [], pass_in, 3, landed_in[0], "weights_pass_w_in0_start")
    got_in = list(_split_wait(passed_in, 1, pass_in, passed_in[-1], "weights_pass_w_in0_wait", nl=0)[0]) + list(landed_in[1:])

    def pick(mine, gathered):
        return [jnp.where(s_me == s, mine, gathered[s]) for s in range(4)]

    conv = {"dn_conv_w": jnp.concatenate(pick(dn_conv_w, got_in[1]), axis=-1),
            "ffn_conv_w": jnp.concatenate(pick(ffn_conv_w, got_in[2]), axis=-1)}
    lanes = lambda a: jnp.pad(a, ((0, 0), (0, 128 - a.shape[1])))
    vec = dict(dn_a_log=lanes(dn_a_log), dn_dt_bias=lanes(dn_dt_bias), dn_norm_w=dn_norm_w, ffn_conv_b=ffn_conv_b,
               norm_pre_mix=norm_pre_mix, norm_post_mix=norm_post_mix, norm_pre_ffn=norm_pre_ffn, norm_post_ffn=norm_post_ffn)

    def matrices(l, names, gathered):
        W = {}
        for k, a in zip(names, gathered):
            if k in ("w_out", "ffn_w_out"):
                rows_, cols = own[k].shape[1:]
                W[k] = lax.dynamic_update_slice(a[:, 0], own[k][l][None], (s_me, 0, 0)).reshape(4 * rows_, cols)
            else:
                cat = jnp.concatenate(pick(own[k][l], a[:, 0]), axis=-1)
                W[k] = jnp.pad(cat, ((0, 0), (0, PROJ_W - IN_COLS))) if k == "w_in" else cat
        return W

    def small_weights(l):
        return {**{k: a[l] for k, a in conv.items()}, **{k: a[l:l + 1] for k, a in vec.items()}}

    pass0, pass1 = _pass_plan([half_of[k] for k in rest]), _pass_plan([half_of[k] for k in mats])
    passing = {}

    def mid_mixer_l0(marker):
        _, landed = _split_wait(started0, len(rest), plan0, marker, "weights_gather_l0_wait")
        passing["l0"] = _split_copy(landed, [], pass0, 3, marker, "weights_pass_l0_start")
        return passing["l0"][-1][0, 0]

    def late_l0(mix_in):
        return matrices(0, rest, _split_wait(passing["l0"], len(rest), pass0, mix_in, "weights_pass_l0_wait", nl=0)[0])

    def after_ffn_in_l0(marker):
        _, landed = _split_wait(started1, len(mats), plan1, marker, "weights_gather_l1_wait")
        passing["l1"] = _split_copy(landed, [], pass1, 3, marker, "weights_pass_l1_start")
        return passing["l1"][-1][0, 0]

    cos, sgn_sin = _rope_tables(x.shape[1])
    W0 = {**small_weights(0), **matrices(0, ["w_in"], got_in[:1])}
    W0_first = dict(W0, norm_pre_mix=W0["norm_pre_mix"] + started1[-1][0, 0])
    x1, h1_l1, saved0 = _layer_fwd(x[0], W0_first, cos, sgn_sin, 0, late_weights=late_l0, next_pre_mix=norm_pre_mix[1:2],
                                   mid_mixer=mid_mixer_l0, after_ffn_in=after_ffn_in_l0)
    W1 = {**small_weights(1),
          **matrices(1, mats, _split_wait(passing["l1"], len(mats), pass1, x1, "weights_pass_l1_wait", nl=0)[0])}
    x2, _, saved1 = _layer_fwd(x1, W1, cos, sgn_sin, 1, h1=h1_l1)
    loss_local, dy = _loss_head(x2, loss_target[0], "loss_head")
    loss = lax.psum(loss_local, ("x", "y", "c"))

    def shard_major(names, grads_l):
        return [_shard_major(name, grads_l[name]) for name in names]

    def add_siblings(l, names, gs, from_sib):
        return [_add_sibling(g, r, c_arr, tiles[name], f"add_sibling_{name}{l}") for g, r, name in zip(gs, from_sib, names)]

    def scatter_start(l, names, parts, after, tag):
        return _split_copy(parts, [SDS((3,) + p.shape[1:], p.dtype) for p in parts], _scatter_plan, 3, after,
                           f"grads_l{l}{tag}_scatter_start")

    def owner_sums(l, names, sent, after, tag, into):
        parts, recvd = _split_wait(sent, len(names), _scatter_plan, after, f"grads_l{l}{tag}_scatter_wait")
        return {name: _add_chips(p, r, sc_arr, tiles[name], l, into.get(name), f"add_chips_{name}{l}")
                for p, r, name in zip(parts, recvd, names)}

    (dx1, dh1_l1), grads1, _ = _layer_bwd(dy, saved1, W1, cos, sgn_sin, 1, first_layer=False)
    gs1 = shard_major(mats, grads1)
    swap1 = _split_copy(gs1, [SDS((4, g.shape[1] // 2, g.shape[2]), g.dtype) for g in gs1], _exchange_plan, 1, dx1,
                        "grads_l1_sibling_start")
    ffn = ["ffn_w_in", "ffn_w_out"]
    launched = {}

    def after_ffn_l0(g_ffn, dx_mid):
        gs1_, from_sib1 = _split_wait(swap1, len(mats), _exchange_plan, dx_mid, "grads_l1_sibling_wait")
        launched["l1"] = scatter_start(1, mats, add_siblings(1, mats, gs1_, from_sib1), dx_mid, "")
        gs0 = shard_major(ffn, g_ffn)
        from_sib0 = _exchange_halves(gs0, "grads_l0_ffn_to_sibling")
        launched["l0_ffn"] = scatter_start(0, ffn, add_siblings(0, ffn, gs0, from_sib0), launched["l1"][-1], "_ffn")
        return launched["l0_ffn"][-1][0, 0]

    W0_last = dict(W0, **saved0["late"], norm_post_ffn=W0["norm_post_ffn"] + swap1[-1][0, 0])
    dx, grads0, grads1["norm_pre_mix"] = _layer_bwd(dx1, saved0, W0_last, cos, sgn_sin, 0, after_ffn=after_ffn_l0,
                                                    next_layer=(dh1_l1, norm_pre_mix[1:2]))
    mix = ["w_in", "w_out"]
    gs0 = shard_major(mix, grads0)
    part0 = add_siblings(0, mix, gs0, _exchange_halves(gs0, "grads_l0_mix_to_sibling"))
    sent0 = scatter_start(0, mix, part0, dx, "_mix")
    red = owner_sums(0, ffn, launched["l0_ffn"], sent0[-1], "_ffn", {})
    red = owner_sums(1, mats, launched["l1"], sent0[-1], "", red)
    joined = dict(zip(mats, _join_halves([red[k] for k in mats], [1 if k in mix else None for k in mats],
                                         "grads_join_early")))
    grads = [grads0, grads1]

    small = {}
    for name in _SMALL:
        per_layer = [grads[l][name] for l in range(2)]
        if name in ("dn_a_log", "dn_dt_bias"):
            per_layer = [p[:, :N_HEADS_D] for p in per_layer]
        small[name] = jnp.stack(per_layer).reshape((2,) + (w[name].shape[1:] if name not in ("dn_conv_w", "ffn_conv_w")
                                                           else per_layer[0].shape))
    flat = jnp.concatenate([small[name].reshape(-1) for name in _SMALL])
    n_rows = -(-flat.shape[0] // 1024) * 8
    summed = _all_reduce_small(jnp.pad(flat, (0, n_rows * 128 - flat.shape[0])).reshape(n_rows, 128),
                               "small_grads_all_reduce").reshape(-1)
    off = 0
    g_out = {}
    for name in _SMALL:
        size = small[name].size
        g_out[name] = summed[off:off + size].reshape(small[name].shape)
        off += size
    for k in ("dn_conv_w", "ffn_conv_w"):
        width = w[k].shape[2]
        g_out[k] = lax.dynamic_slice_in_dim(g_out[k], s_me * width, width, axis=2)
    for k in ffn:
        g_out[k] = joined[k]

    deltas, new_m, new_v = {}, {}, {}

    def step(name):
        shape = w[name].shape
        as3 = (lambda a: a) if len(shape) == 3 else (lambda a: a.reshape(shape[0], 1, shape[1]))
        tr = _ADAM_ROWS.get(name, as3(w[name]).shape[1])
        d_, m_, v_ = _adamw(as3(w[name]), as3(g_out[name]), as3(m[name]), as3(v[name]), tr, f"adamw_{name}")
        deltas[name], new_m[name], new_v[name] = d_.reshape(shape), m_.reshape(shape), v_.reshape(shape)

    for name in ffn:
        step(name)
    tiny = [name for name in _WEIGHTS if name not in mats]
    stepped = _adamw_small(*[[d[name] for name in tiny] for d in (w, g_out, m, v)], "adamw_small")
    for out, vals in zip((deltas, new_m, new_v), stepped):
        out.update(zip(tiny, vals))
    done = jnp.reshape(deltas["ffn_w_in"][0, 0, 0] + deltas["ffn_w_out"][0, 0, 0] + deltas["norm_post_ffn"][0, 0], (1,))
    red = owner_sums(0, mix, sent0, done, "_mix", joined)
    for k, a in zip(mix, _join_halves([red[k] for k in mix], [0] * len(mix), "grads_join_late")):
        g_out[k] = a
        step(k)

    return (loss, dx[None], *[g_out[k] for k in _WEIGHTS], *[deltas[k] for k in _WEIGHTS],
            *[new_m[k] for k in _WEIGHTS], *[new_v[k] for k in _WEIGHTS])
```

```python
import jax
import jax.numpy as jnp
from jax import lax
from jax.experimental import pallas as pl
from jax.experimental.pallas import tpu as pltpu

f32, bf16 = jnp.float32, jnp.bfloat16
SDS = jax.ShapeDtypeStruct
HI = lax.Precision.HIGH
MESH = pl.DeviceIdType.MESH
ANY = pl.BlockSpec(memory_space=pl.ANY)

D_MODEL = 1024
N_HEADS_A, HEAD_DIM = 8, 64
ATTN_W = 512
N_HEADS_D, DK = 4, 128
CHUNK = 64
D_FF = 2816
IN_COLS = 3592
PROJ_W = 3840
DN_QKV_COL = 3 * ATTN_W
DN_Z_COL = DN_QKV_COL + 3 * N_HEADS_D * DK
DN_BA_COL = DN_Z_COL + N_HEADS_D * DK
N_SHARDS = 4
BRANCHES = ((1, 16), (4, 4), (16, 1))
EPS = 1e-6
NEG = -1e30
ROW_TILE = 256
VMEM_LIMIT = 56 * 1024 * 1024

ADAM_LR, ADAM_B1, ADAM_B2, ADAM_EPS, ADAM_WD, ADAM_STEP = 0.001, 0.9, 0.999, 1e-08, 0.01, 10


def _params(*sem):
    return pltpu.CompilerParams(dimension_semantics=sem, vmem_limit_bytes=VMEM_LIMIT)


def _mm(a, b, mode, tm, tn, out_dtype, name, column_shards=False):
    if mode == "nn":
        (M, K), N = a.shape, b.shape[1]
        dims = (((1,), (0,)), ((), ()))
        a_spec = pl.BlockSpec((tm, K), lambda i, j: (i, 0))
        b_spec = pl.BlockSpec((K, tn), lambda i, j: (0, j))
    elif mode == "nt":
        (M, K), N = a.shape, b.shape[0]
        dims = (((1,), (1,)), ((), ()))
        a_spec = pl.BlockSpec((tm, K), lambda i, j: (i, 0))
        b_spec = pl.BlockSpec((tn, K), lambda i, j: (j, 0))
    else:
        (K, M), N = a.shape, b.shape[1]
        dims = (((0,), (0,)), ((), ()))
        a_spec = pl.BlockSpec((K, tm), lambda i, j: (0, i))
        b_spec = pl.BlockSpec((K, tn), lambda i, j: (0, j))
    assert M % tm == 0 and N % tn == 0, (name, M, N, tm, tn)

    def body(a_ref, b_ref, o_ref):
        o_ref[...] = lax.dot_general(a_ref[...].astype(bf16), b_ref[...].astype(bf16), dims,
                                     preferred_element_type=f32).astype(o_ref.dtype)

    if column_shards:
        out_spec, out_shape = pl.BlockSpec((None, tm, tn), lambda i, j: (j, i, 0)), SDS((N // tn, M, tn), out_dtype)
    else:
        out_spec, out_shape = pl.BlockSpec((tm, tn), lambda i, j: (i, j)), SDS((M, N), out_dtype)
    return pl.pallas_call(body, grid=(M // tm, N // tn), in_specs=[a_spec, b_spec], out_specs=out_spec,
                          out_shape=out_shape, name=name, compiler_params=_params("parallel", "arbitrary"))(a, b)


def _row_spec(r, tm):
    if isinstance(r, tuple):
        arr, width, cb = r
        return arr, pl.BlockSpec((tm, width), lambda i, j, cb=cb: (i, cb + j))
    return r, pl.BlockSpec((tm, r.shape[1]), lambda i, j: (i, j))


def _full_spec(p):
    return pl.BlockSpec(p.shape, lambda i, j: (0,) * p.ndim)


def _rows(fn, rows, params, outs, name, tm=ROW_TILE, ncol=1):
    arrs, specs = zip(*[_row_spec(r, tm) for r in rows])
    S = arrs[0].shape[0]
    nr, npar = len(rows), len(params)

    def body(*refs):
        vals = fn(*[r[...].astype(f32) for r in refs[:nr]], *[p[...] for p in refs[nr:nr + npar]])
        for o_ref, v in zip(refs[nr + npar:], vals):
            o_ref[...] = v.astype(o_ref.dtype)

    return pl.pallas_call(
        body, grid=(S // tm, ncol), in_specs=list(specs) + [_full_spec(p) for p in params],
        out_specs=[pl.BlockSpec((tm, w), lambda i, j: (i, j)) for w, _ in outs],
        out_shape=[SDS((S, w * ncol), dt) for w, dt in outs], name=name,
        compiler_params=_params("parallel", "parallel"))(*arrs, *params)


def _rows_vjp(fn, rows, params, cts, wrt_rows, wrt_params, name, adds=None, tm=ROW_TILE, ncol=1, row_dtype=f32):
    adds = adds or {}
    arrs, specs = zip(*[_row_spec(r, tm) for r in rows])
    carrs, cspecs = zip(*[_row_spec(c, tm) for c in cts])
    add_keys = sorted(adds)
    aarrs = [adds[k] for k in add_keys]
    S = arrs[0].shape[0]
    nr, npar, nc, na = len(rows), len(params), len(cts), len(aarrs)
    widths = [specs[k].block_shape[1] for k in wrt_rows]
    row_dtypes = row_dtype if isinstance(row_dtype, (list, tuple)) else [row_dtype] * len(wrt_rows)

    def body(*refs):
        first = jnp.logical_and(pl.program_id(0) == 0, pl.program_id(1) == 0)
        rv = [r[...].astype(f32) for r in refs[:nr]]
        pv = [p[...] for p in refs[nr:nr + npar]]
        cv = tuple(c[...].astype(f32) for c in refs[nr + npar:nr + npar + nc])
        av = dict(zip(add_keys, refs[nr + npar + nc:nr + npar + nc + na]))
        o = refs[nr + npar + nc + na:]
        _, vjp = jax.vjp(fn, *rv, *pv)
        g = vjp(cv)
        for n, k in enumerate(wrt_rows):
            val = g[k]
            if k in av:
                val = val + av[k][...]
            o[n][...] = val.astype(o[n].dtype)
        for n, k in enumerate(wrt_params):
            ref = o[len(wrt_rows) + n]

            @pl.when(first)
            def _(ref=ref):
                ref[...] = jnp.zeros_like(ref)

            ref[...] += g[nr + k]

    res = pl.pallas_call(
        body, grid=(S // tm, ncol),
        in_specs=list(specs) + [_full_spec(p) for p in params] + list(cspecs)
        + [pl.BlockSpec((tm, a.shape[1] // ncol), lambda i, j: (i, j)) for a in aarrs],
        out_specs=[pl.BlockSpec((tm, w), lambda i, j: (i, j)) for w in widths] + [_full_spec(params[k]) for k in wrt_params],
        out_shape=[SDS((S, w * ncol), dt) for w, dt in zip(widths, row_dtypes)]
        + [SDS(params[k].shape, f32) for k in wrt_params],
        name=name, compiler_params=_params("arbitrary", "arbitrary"))(*arrs, *params, *carrs, *aarrs)
    return res[:len(wrt_rows)], res[len(wrt_rows):]


def _rms(x, w):
    return x * lax.rsqrt(jnp.mean(x * x, axis=-1, keepdims=True) + EPS) * w


def _rms_fn(x, w):
    return (_rms(x, w),)


def _res_rms_fn(f, res, w):
    return (res + _rms(f, w),)


def _res_rms_pre_fn(f, res, w_post, w_pre):
    x1 = res + _rms(f, w_post)
    return x1, _rms(x1, w_pre)


def _swap_halves(x):
    lane = lax.broadcasted_iota(jnp.int32, x.shape, 1)
    first = (lane % HEAD_DIM) < (HEAD_DIM // 2)
    n = x.shape[1]
    return jnp.where(first, pltpu.roll(x, n - HEAD_DIM // 2, 1), pltpu.roll(x, HEAD_DIM // 2, 1))


def _rope_fwd_fn(q, k, cos, sgn_sin):
    scale = HEAD_DIM ** -0.5
    return ((q * cos + _swap_halves(q) * sgn_sin) * scale, k * cos + _swap_halves(k) * sgn_sin)


def _rope_bwd_fn(dq, dk, cos, sgn_sin):
    dq = dq * (HEAD_DIM ** -0.5)
    return (dq * cos + _swap_halves(dq * sgn_sin), dk * cos + _swap_halves(dk * sgn_sin))


def _nt(a, b):
    return lax.dot_general(a, b, (((1,), (1,)), ((), ())), preferred_element_type=f32)


def _tn(a, b):
    return lax.dot_general(a, b, (((0,), (0,)), ((), ())), preferred_element_type=f32)


def _band_rows(j, d, nb):
    r, i = j // nb, j % nb
    if d == 1:
        cur = pl.ds(pl.multiple_of(i * 128, 128), 128)
        prev = pl.ds(pl.multiple_of(jnp.maximum(i - 1, 0) * 128, 128), 128)
    else:
        cur = pl.ds(i * (128 * d) + r, 128, stride=d)
        prev = pl.ds(jnp.maximum(i - 1, 0) * (128 * d) + r, 128, stride=d)
    return cur, prev, (i == 0).astype(jnp.int32)


def _band_bias(bias_ref):
    a = lax.broadcasted_iota(jnp.int32, (256, 256), 0) % 128
    c = lax.broadcasted_iota(jnp.int32, (256, 256), 1)
    own = jnp.logical_and(c < 128, c <= a)
    before = jnp.logical_and(c >= 128, c - 128 >= a)
    bias_ref[0] = jnp.where(jnp.logical_or(own, before), 0.0, NEG)
    bias_ref[1] = jnp.where(own, 0.0, NEG)


def _stack_heads(x, head_a):
    return jnp.concatenate([jnp.where(head_a, x, 0.0), jnp.where(head_a, 0.0, x)], axis=0)


def _unstack_heads(x2, head_a):
    return jnp.where(head_a, x2[:128], x2[128:])


def _pair_at(S, first_col):
    return pl.BlockSpec((S, 128), lambda h: (0, first_col // 128 + h))


def _attn_fwd(proj, cos, sgn_sin, name):
    S = proj.shape[0]
    nblk = S // 128

    def body(qp_ref, kp_ref, v_ref, cos_ref, sin_ref, out_ref, lse_ref, bias_ref, q_ref, k_ref, *scr):
        head_a = lax.broadcasted_iota(jnp.int32, (1, 128), 1) < HEAD_DIM
        _band_bias(bias_ref)
        q_ref[...], k_ref[...] = _rope_fwd_fn(qp_ref[...], kp_ref[...], cos_ref[...], sin_ref[...])
        for b, (d, nb) in enumerate(BRANCHES):
            ob_ref, lb_ref = scr[2 * b], scr[2 * b + 1]

            def blk(j, carry, d=d, nb=nb, ob_ref=ob_ref, lb_ref=lb_ref):
                cur, prev, first = _band_rows(j, d, nb)
                q2 = _stack_heads(q_ref[cur, :], head_a).astype(bf16)
                if nb == 1:
                    k2, v2, bias = k_ref[cur, :].astype(bf16), v_ref[cur, :].astype(bf16), bias_ref[1][:, :128]
                else:
                    k2 = jnp.concatenate([k_ref[cur, :], k_ref[prev, :]], axis=0).astype(bf16)
                    v2 = jnp.concatenate([v_ref[cur, :], v_ref[prev, :]], axis=0).astype(bf16)
                    bias = bias_ref[first]
                s = _nt(q2, k2) + bias
                mx = jnp.max(s, axis=1, keepdims=True)
                p = jnp.exp(s - mx)
                l = jnp.sum(p, axis=1, keepdims=True)
                o = jnp.dot(p.astype(bf16), v2, preferred_element_type=f32) / l
                ob_ref[cur, :] = _unstack_heads(o, head_a)
                lb_ref[cur, :] = _unstack_heads(jnp.broadcast_to(mx + jnp.log(l), (256, 128)), head_a)
                return carry

            lax.fori_loop(0, nblk, blk, 0, unroll=16)
        l0, l1, l2 = scr[1][...], scr[3][...], scr[5][...]
        mx = jnp.maximum(jnp.maximum(l0, l1), l2)
        e0, e1, e2 = jnp.exp(l0 - mx), jnp.exp(l1 - mx), jnp.exp(l2 - mx)
        den = e0 + e1 + e2
        out_ref[...] = ((e0 * scr[0][...] + e1 * scr[2][...] + e2 * scr[4][...]) / den).astype(out_ref.dtype)
        lse_ref[...] = mx + jnp.log(den)

    pair = pl.BlockSpec((S, 128), lambda h: (0, h))
    return pl.pallas_call(
        body, grid=(N_HEADS_A // 2,),
        in_specs=[pair, _pair_at(S, ATTN_W), _pair_at(S, 2 * ATTN_W), pair, pair], out_specs=[pair, pair],
        out_shape=[SDS((S, ATTN_W), bf16), SDS((S, ATTN_W), f32)],
        scratch_shapes=[pltpu.VMEM((2, 256, 256), f32)] + [pltpu.VMEM((S, 128), f32)] * 8,
        name=name, compiler_params=_params("parallel"))(proj, proj, proj, cos, sgn_sin)


def _attn_bwd(proj, cos, sgn_sin, dmix_in, out, lse, name):
    S = proj.shape[0]
    nblk = S // 128

    def body(qp_ref, kp_ref, v_ref, cos_ref, sin_ref, do_ref, out_ref, lse_ref, dq_ref, dk_ref, dv_ref,
             bias_ref, t_ref, q_ref, k_ref):
        head_a = lax.broadcasted_iota(jnp.int32, (1, 128), 1) < HEAD_DIM
        _band_bias(bias_ref)
        q_ref[...], k_ref[...] = _rope_fwd_fn(qp_ref[...], kp_ref[...], cos_ref[...], sin_ref[...])
        x = do_ref[...] * out_ref[...].astype(f32)
        t_ref[...] = jnp.where(head_a, jnp.sum(jnp.where(head_a, x, 0.0), axis=1, keepdims=True),
                               jnp.sum(jnp.where(head_a, 0.0, x), axis=1, keepdims=True))
        dq_ref[...] = jnp.zeros_like(dq_ref)
        dk_ref[...] = jnp.zeros_like(dk_ref)
        dv_ref[...] = jnp.zeros_like(dv_ref)
        for d, nb in BRANCHES:
            def blk(j, carry, d=d, nb=nb):
                cur, prev, first = _band_rows(j, d, nb)
                q2 = _stack_heads(q_ref[cur, :], head_a).astype(bf16)
                do2 = _stack_heads(do_ref[cur, :], head_a).astype(bf16)
                t, lse_b = t_ref[cur, :], lse_ref[cur, :]
                t2 = jnp.concatenate([t[:, :1], t[:, HEAD_DIM:HEAD_DIM + 1]], axis=0)
                lse2 = jnp.concatenate([lse_b[:, :1], lse_b[:, HEAD_DIM:HEAD_DIM + 1]], axis=0)
                if nb == 1:
                    k2, v2, bias = k_ref[cur, :].astype(bf16), v_ref[cur, :].astype(bf16), bias_ref[1][:, :128]
                else:
                    k2 = jnp.concatenate([k_ref[cur, :], k_ref[prev, :]], axis=0).astype(bf16)
                    v2 = jnp.concatenate([v_ref[cur, :], v_ref[prev, :]], axis=0).astype(bf16)
                    bias = bias_ref[first]
                p = jnp.exp(_nt(q2, k2) + bias - lse2)
                ds = (p * (_nt(do2, v2) - t2)).astype(bf16)
                dq_ref[cur, :] += _unstack_heads(jnp.dot(ds, k2, preferred_element_type=f32), head_a)
                dk2, dv2 = _tn(ds, q2), _tn(p.astype(bf16), do2)
                dk_ref[cur, :] += dk2[:128]
                dv_ref[cur, :] += dv2[:128]
                if nb != 1:
                    dk_ref[prev, :] += dk2[128:]
                    dv_ref[prev, :] += dv2[128:]
                return carry

            lax.fori_loop(0, nblk, blk, 0, unroll=16)
        dq_ref[...], dk_ref[...] = _rope_bwd_fn(dq_ref[...], dk_ref[...], cos_ref[...], sin_ref[...])

    pair = pl.BlockSpec((S, 128), lambda h: (0, h))
    return pl.pallas_call(
        body, grid=(N_HEADS_A // 2,),
        in_specs=[pair, _pair_at(S, ATTN_W), _pair_at(S, 2 * ATTN_W), pair, pair, pair, pair, pair],
        out_specs=[pair] * 3, out_shape=[SDS((S, ATTN_W), f32)] * 3,
        scratch_shapes=[pltpu.VMEM((2, 256, 256), f32)] + [pltpu.VMEM((S, 128), f32)] * 3,
        name=name, compiler_params=_params("parallel"))(proj, proj, proj, cos, sgn_sin, dmix_in, out, lse)


def _conv_val(x, w, K, rows):
    acc = x * w[K - 1:K, :]
    for s in range(1, K):
        acc = acc + jnp.where(rows >= s, pltpu.roll(x, s, 0), 0.0) * w[K - 1 - s:K - s, :]
    return acc


def _colconv_fwd(xs, ws, bs, K, fn, nblk, tc, outs, name):
    S = xs[0][0].shape[0]
    n = len(xs)
    has_b = bs is not None

    def body(*refs):
        rows = lax.broadcasted_iota(jnp.int32, (S, tc), 0)
        cs = []
        for k in range(n):
            c = _conv_val(refs[k][...].astype(f32), refs[n + k][...], K, rows)
            if has_b:
                c = c + refs[2 * n + k][...]
            cs.append(c)
        for o_ref, val in zip(refs[(3 if has_b else 2) * n:], fn(*cs)):
            o_ref[...] = val.astype(o_ref.dtype)

    def cspec(rows_, cb0):
        return pl.BlockSpec((rows_, tc), lambda j, cb0=cb0: (0, cb0 + j))

    in_specs = [cspec(S, cb) for _, cb in xs] + [cspec(K, cb) for _, cb in ws]
    args = [a for a, _ in xs] + [a for a, _ in ws]
    if has_b:
        in_specs += [cspec(1, cb) for _, cb in bs]
        args += [a for a, _ in bs]
    return pl.pallas_call(
        body, grid=(nblk,), in_specs=in_specs, out_specs=[cspec(S, 0) for _ in outs],
        out_shape=[SDS((S, nblk * tc), dt) for dt in outs], name=name, compiler_params=_params("parallel"))(*args)


def _colconv_bwd(xs, ws, bs, K, fn, douts, nblk, tc, name, dx_dtype=f32):
    S = xs[0][0].shape[0]
    n, nd = len(xs), len(douts)
    has_b = bs is not None
    nin = (3 if has_b else 2) * n

    def body(*refs):
        rows = lax.broadcasted_iota(jnp.int32, (S, tc), 0)
        x = [refs[k][...].astype(f32) for k in range(n)]
        w = [refs[n + k][...] for k in range(n)]
        cs = []
        for k in range(n):
            c = _conv_val(x[k], w[k], K, rows)
            if has_b:
                c = c + refs[2 * n + k][...]
            cs.append(c)
        _, vjp = jax.vjp(fn, *cs)
        dcs = vjp(tuple(r[...].astype(f32) for r in refs[nin:nin + nd]))
        o = refs[nin + nd:]
        for k in range(n):
            dc = dcs[k]
            dx = dc * w[k][K - 1:K, :]
            o[n + k][K - 1:K, :] = jnp.sum(dc * x[k], axis=0, keepdims=True)
            for s in range(1, K):
                dx = dx + jnp.where(rows < S - s, pltpu.roll(dc, S - s, 0), 0.0) * w[k][K - 1 - s:K - s, :]
                xsh = jnp.where(rows >= s, pltpu.roll(x[k], s, 0), 0.0)
                o[n + k][K - 1 - s:K - s, :] = jnp.sum(dc * xsh, axis=0, keepdims=True)
            o[k][...] = dx.astype(o[k].dtype)
            if has_b:
                o[2 * n + k][...] = jnp.sum(dc, axis=0, keepdims=True)

    def cspec(rows_, cb0):
        return pl.BlockSpec((rows_, tc), lambda j, cb0=cb0: (0, cb0 + j))

    in_specs = [cspec(S, cb) for _, cb in xs] + [cspec(K, cb) for _, cb in ws]
    args = [a for a, _ in xs] + [a for a, _ in ws]
    if has_b:
        in_specs += [cspec(1, cb) for _, cb in bs]
        args += [a for a, _ in bs]
    in_specs += [cspec(S, 0) for _ in douts]
    args += list(douts)
    W = nblk * tc
    out_specs = [cspec(S, 0)] * n + [cspec(K, 0)] * n + ([cspec(1, 0)] * n if has_b else [])
    out_shape = [SDS((S, W), dx_dtype)] * n + [SDS((K, W), f32)] * n + ([SDS((1, W), f32)] * n if has_b else [])
    res = pl.pallas_call(body, grid=(nblk,), in_specs=in_specs, out_specs=out_specs, out_shape=out_shape,
                         name=name, compiler_params=_params("parallel"))(*args)
    return res[:n], res[n:2 * n], res[2 * n:]


def _silu_fn(c):
    return (c * jax.nn.sigmoid(c),)


_GELU_C, _GELU_A = 0.7978845608028654, 0.044715


@jax.custom_vjp
def _geglu(gate, up):
    return 0.5 * gate * (1.0 + jnp.tanh(_GELU_C * (gate + _GELU_A * gate * gate * gate))) * up


def _geglu_vjp_fwd(gate, up):
    return _geglu(gate, up), (gate, up)


def _geglu_vjp_bwd(res, d):
    gate, up = res
    g2 = gate * gate
    t = jnp.tanh(_GELU_C * gate * (1.0 + _GELU_A * g2))
    h = 0.5 * (1.0 + t)
    dgelu = h + (0.5 * _GELU_C) * gate * (1.0 - t * t) * (1.0 + (3.0 * _GELU_A) * g2)
    return d * up * dgelu, d * (gate * h)


_geglu.defvjp(_geglu_vjp_fwd, _geglu_vjp_bwd)


def _geglu_fn(gate, up):
    return (_geglu(gate, up),)


def _softplus(x):
    u = jnp.exp(jnp.minimum(x, 20.0))
    small = u * (1.0 - 0.5 * u)
    return jnp.where(x > 20.0, x, jnp.where(u < 1e-4, small, jnp.log(1.0 + u)))


def _bmm(a, b, precision=None):
    return lax.dot_general(a, b, (((2,), (1,)), ((0,), (0,))), precision=precision, preferred_element_type=f32)


def _bnt(a, b, precision=None):
    return lax.dot_general(a, b, (((2,), (2,)), ((0,), (0,))), precision=precision, preferred_element_type=f32)


def _btn(a, b, precision=None):
    return lax.dot_general(a, b, (((1,), (1,)), ((0,), (0,))), precision=precision, preferred_element_type=f32)


@jax.custom_vjp
def _unit_lower_inverse(A):
    n = A.shape[-1]
    eye = (lax.broadcasted_iota(jnp.int32, (1, n, n), 1) == lax.broadcasted_iota(jnp.int32, (1, n, n), 2)).astype(f32)
    P = -A
    T = eye + P
    for _ in range(5):
        P = _bmm(P, P, HI)
        T = T + _bmm(T, P, HI)
    return T


def _unit_lower_inverse_fwd(A):
    T = _unit_lower_inverse(A)
    return T, T


def _unit_lower_inverse_bwd(T, dT):
    return (-_btn(T, _bnt(dT, T, HI), HI),)


_unit_lower_inverse.defvjp(_unit_lower_inverse_fwd, _unit_lower_inverse_bwd)


def _dn_prep_fn(q, k, v, ba, alog, dtb, h):
    G, C = q.shape[0], CHUNK
    lane = lax.broadcasted_iota(jnp.int32, (1, 1, 128), 2)

    def sel(arr, idx):
        return jnp.sum(jnp.where(lane == idx, arr, 0.0), axis=-1, keepdims=True)

    beta = jax.nn.sigmoid(sel(ba, h))
    g = -jnp.exp(sel(alog[None], h)) * _softplus(sel(ba, N_HEADS_D + h) + sel(dtb[None], h))
    qn = q * lax.rsqrt(jnp.sum(q * q, axis=-1, keepdims=True) + EPS) * (DK ** -0.5)
    kn = k * lax.rsqrt(jnp.sum(k * k, axis=-1, keepdims=True) + EPS)
    ii = lax.broadcasted_iota(jnp.int32, (1, C, C), 1)
    jj = lax.broadcasted_iota(jnp.int32, (1, C, C), 2)
    tril, strict = ii >= jj, ii > jj
    gsq = jnp.broadcast_to(g, (G, C, C))
    gcol = _bmm(jnp.broadcast_to(tril.astype(f32), (G, C, C)), gsq, HI)
    grow = _bmm(jnp.ones((G, C, C), f32), jnp.where(ii <= jj, gsq, 0.0), HI)
    decay = jnp.exp(jnp.where(tril, gcol - grow, NEG))
    gc = gcol[:, :, :1]
    glast = gcol[:, C - 1:C, :1]
    kb = kn * beta
    A = jnp.where(strict, _bnt(kb.astype(bf16), kn.astype(bf16)) * decay, 0.0)
    T = _unit_lower_inverse(A).astype(bf16)
    u = _bmm(T, (v * beta).astype(bf16))
    w = _bmm(T, (kb * jnp.exp(gc)).astype(bf16))
    qk = _bnt(qn.astype(bf16), kn.astype(bf16)) * decay
    qd = qn * jnp.exp(gc)
    kd = kn * jnp.exp(glast - gc)
    return u, w, qk, qd, kd, jnp.broadcast_to(jnp.exp(glast), (G, C, DK))


def _dn_scan_fn(u, w, qk, qd, kd, eg, St):
    b = lambda a: a.astype(bf16)
    vnew = u - _bmm(b(w), b(St))
    o = _bmm(b(qd), b(St)) + _bmm(b(qk), b(vnew))
    return o, St * eg[:, :1, :] + _btn(b(kd), b(vnew))


def _dn_post_fn(o, z, nw):
    return (_rms(o, nw) * (z * jax.nn.sigmoid(z)),)


DN_GROUP = 16


def _dn_prep_specs(S, rows):
    def col(first):
        return pl.BlockSpec((rows, DK), lambda i, h, first=first: (i, first // DK + h))

    par = pl.BlockSpec((1, 128), lambda i, h: (0, 0))
    return [col(0), col(N_HEADS_D * DK), col(2 * N_HEADS_D * DK),
            pl.BlockSpec((rows, 128), lambda i, h: (i, DN_BA_COL // 128)), par, par]


def _dn_prep(qkv, proj, alog, dtb, name):
    S = qkv.shape[0]
    G = DN_GROUP
    rows = G * CHUNK

    def body(q_ref, k_ref, v_ref, ba_ref, al_ref, dt_ref, u_ref, w_ref, qk_ref, qd_ref, kd_ref, eg_ref):
        h = pl.program_id(1)
        r3 = lambda ref: ref[...].reshape(G, CHUNK, 128)
        u, w, qk, qd, kd, eg = _dn_prep_fn(r3(q_ref), r3(k_ref), r3(v_ref), r3(ba_ref), al_ref[...], dt_ref[...], h)
        for ref, val in ((u_ref, u), (w_ref, w), (qd_ref, qd), (kd_ref, kd), (eg_ref, eg)):
            ref[...] = val.reshape(rows, DK)
        qk_ref[:, :CHUNK] = qk.reshape(rows, CHUNK)
        qk_ref[:, CHUNK:] = jnp.zeros((rows, DK - CHUNK), f32)

    out = pl.BlockSpec((rows, DK), lambda i, h: (i, h))
    return pl.pallas_call(
        body, grid=(S // rows, N_HEADS_D), in_specs=_dn_prep_specs(S, rows), out_specs=[out] * 6,
        out_shape=[SDS((S, N_HEADS_D * DK), f32)] * 6, name=name,
        compiler_params=_params("parallel", "parallel"))(qkv, qkv, qkv, proj, alog, dtb)


def _dn_prep_bwd(qkv, proj, alog, dtb, cts, name):
    S = qkv.shape[0]
    G = DN_GROUP
    rows = G * CHUNK

    def body(q_ref, k_ref, v_ref, ba_ref, al_ref, dt_ref, du_ref, dw_ref, dqk_ref, dqd_ref, dkd_ref, deg_ref,
             dq_ref, dk_ref, dv_ref, dba_ref, dal_ref, ddt_ref):
        i, h = pl.program_id(0), pl.program_id(1)
        r3 = lambda ref: ref[...].reshape(G, CHUNK, 128)
        _, vjp = jax.vjp(lambda q, k, v, ba, al, dt: _dn_prep_fn(q, k, v, ba, al, dt, h),
                         r3(q_ref), r3(k_ref), r3(v_ref), r3(ba_ref), al_ref[...], dt_ref[...])
        dqk = dqk_ref[:, :CHUNK].reshape(G, CHUNK, CHUNK)
        dq, dk, dv, dba, dal, ddt = vjp((r3(du_ref), r3(dw_ref), dqk, r3(dqd_ref), r3(dkd_ref), r3(deg_ref)))
        dq_ref[...] = dq.reshape(rows, DK)
        dk_ref[...] = dk.reshape(rows, DK)
        dv_ref[...] = dv.reshape(rows, DK)

        @pl.when(h == 0)
        def _():
            dba_ref[...] = jnp.zeros_like(dba_ref)

        @pl.when(jnp.logical_and(i == 0, h == 0))
        def _():
            dal_ref[...] = jnp.zeros_like(dal_ref)
            ddt_ref[...] = jnp.zeros_like(ddt_ref)

        dba_ref[...] += dba.reshape(rows, 128)
        dal_ref[...] += dal
        ddt_ref[...] += ddt

    hcol = pl.BlockSpec((rows, DK), lambda i, h: (i, h))
    par = pl.BlockSpec((1, 128), lambda i, h: (0, 0))
    W = N_HEADS_D * DK
    return pl.pallas_call(
        body, grid=(S // rows, N_HEADS_D), in_specs=_dn_prep_specs(S, rows) + [hcol] * 6,
        out_specs=[hcol] * 3 + [pl.BlockSpec((rows, 128), lambda i, h: (i, 0)), par, par],
        out_shape=[SDS((S, W), f32)] * 3 + [SDS((S, 128), f32), SDS((1, 128), f32), SDS((1, 128), f32)], name=name,
        compiler_params=_params("arbitrary", "arbitrary"))(qkv, qkv, qkv, proj, alog, dtb, *cts)


def _heads(x):
    return jnp.stack([x[:, DK * h:DK * (h + 1)] for h in range(N_HEADS_D)])


SCAN_CHUNKS = 8


def _dn_scan(pre, name):
    S = pre[0].shape[0]
    NCH = S // CHUNK
    rows = SCAN_CHUNKS * CHUNK

    def body(u_ref, w_ref, qk_ref, qd_ref, kd_ref, eg_ref, o_ref, st_ref, s_ref):
        @pl.when(pl.program_id(0) == 0)
        def _():
            s_ref[...] = jnp.zeros_like(s_ref)

        St = s_ref[...]
        for k in range(SCAN_CHUNKS):
            r = slice(k * CHUNK, (k + 1) * CHUNK)
            st_ref[k] = St
            o, St = _dn_scan_fn(_heads(u_ref[r, :]), _heads(w_ref[r, :]), _heads(qk_ref[r, :])[:, :, :CHUNK],
                                _heads(qd_ref[r, :]), _heads(kd_ref[r, :]), _heads(eg_ref[r, :]), St)
            for h in range(N_HEADS_D):
                o_ref[r, DK * h:DK * (h + 1)] = o[h]
        s_ref[...] = St

    blk = pl.BlockSpec((rows, N_HEADS_D * DK), lambda n: (n, 0))
    return pl.pallas_call(
        body, grid=(S // rows,), in_specs=[blk] * 6,
        out_specs=[blk, pl.BlockSpec((SCAN_CHUNKS, N_HEADS_D, DK, DK), lambda n: (n, 0, 0, 0))],
        out_shape=[SDS((S, N_HEADS_D * DK), f32), SDS((NCH, N_HEADS_D, DK, DK), f32)],
        scratch_shapes=[pltpu.VMEM((N_HEADS_D, DK, DK), f32)], name=name, compiler_params=_params("arbitrary"))(*pre)


def _dn_scan_bwd(pre, states, do, name):
    S = do.shape[0]
    rows = SCAN_CHUNKS * CHUNK
    steps = S // rows

    def body(u_ref, w_ref, qk_ref, qd_ref, kd_ref, eg_ref, st_ref, do_ref,
             du_ref, dw_ref, dqk_ref, dqd_ref, dkd_ref, deg_ref, ds_ref):
        @pl.when(pl.program_id(0) == 0)
        def _():
            ds_ref[...] = jnp.zeros_like(ds_ref)

        dS = ds_ref[...]
        for k in reversed(range(SCAN_CHUNKS)):
            r = slice(k * CHUNK, (k + 1) * CHUNK)
            _, vjp = jax.vjp(_dn_scan_fn, _heads(u_ref[r, :]), _heads(w_ref[r, :]), _heads(qk_ref[r, :])[:, :, :CHUNK],
                             _heads(qd_ref[r, :]), _heads(kd_ref[r, :]), _heads(eg_ref[r, :]), st_ref[k])
            du, dw, dqk, dqd, dkd, deg, dS = vjp((_heads(do_ref[r, :]), dS))
            for h in range(N_HEADS_D):
                c = slice(DK * h, DK * (h + 1))
                for ref, val in ((du_ref, du), (dw_ref, dw), (dqd_ref, dqd), (dkd_ref, dkd), (deg_ref, deg)):
                    ref[r, c] = val[h]
                dqk_ref[r, DK * h:DK * h + CHUNK] = dqk[h]
                dqk_ref[r, DK * h + CHUNK:DK * (h + 1)] = jnp.zeros((CHUNK, DK - CHUNK), f32)
        ds_ref[...] = dS

    blk = pl.BlockSpec((rows, N_HEADS_D * DK), lambda n: (steps - 1 - n, 0))
    return pl.pallas_call(
        body, grid=(steps,),
        in_specs=[blk] * 6 + [pl.BlockSpec((SCAN_CHUNKS, N_HEADS_D, DK, DK), lambda n: (steps - 1 - n, 0, 0, 0)), blk],
        out_specs=[blk] * 6, out_shape=[SDS((S, N_HEADS_D * DK), f32)] * 6,
        scratch_shapes=[pltpu.VMEM((N_HEADS_D, DK, DK), f32)], name=name,
        compiler_params=_params("arbitrary"))(*pre, states, do)


def _loss_head(y, t, name):
    S, D = y.shape
    tm = ROW_TILE

    def body(y_ref, t_ref, dy_ref, l_ref):
        i = pl.program_id(0)
        d = y_ref[...] - t_ref[...]
        dy_ref[...] = d * (1.0 / D)
        part = jnp.sum(jnp.sum(d * d, axis=1, keepdims=True), axis=0, keepdims=True) * (0.5 / D)

        @pl.when(i == 0)
        def _():
            l_ref[...] = jnp.zeros_like(l_ref)

        l_ref[...] += jnp.broadcast_to(part, l_ref.shape)

    spec = pl.BlockSpec((tm, D), lambda i: (i, 0))
    dy, l = pl.pallas_call(body, grid=(S // tm,), in_specs=[spec, spec],
                           out_specs=[spec, pl.BlockSpec((1, 128), lambda i: (0, 0))],
                           out_shape=[SDS((S, D), f32), SDS((1, 128), f32)], name=name,
                           compiler_params=_params("arbitrary"))(y, t)
    return l[0, 0], dy


def _adamw_refs(w_ref, g_ref, m_ref, v_ref, d_ref, mo_ref, vo_ref):
    gv = g_ref[...]
    m2 = ADAM_B1 * m_ref[...] + (1.0 - ADAM_B1) * gv
    v2 = ADAM_B2 * v_ref[...] + (1.0 - ADAM_B2) * (gv * gv)
    m_hat = m2 / (1.0 - ADAM_B1 ** ADAM_STEP)
    v_hat = v2 / (1.0 - ADAM_B2 ** ADAM_STEP)
    d_ref[...] = -ADAM_LR * (m_hat / (jnp.sqrt(v_hat) + ADAM_EPS) + ADAM_WD * w_ref[...])
    mo_ref[...] = m2
    vo_ref[...] = v2


def _adamw_small(ws, gs, ms, vs, name):
    n = len(ws)

    def body(*refs):
        for i in range(n):
            _adamw_refs(*[refs[k * n + i] for k in range(7)])

    res = pl.pallas_call(body, out_shape=[SDS(a.shape, f32) for a in ws] * 3, name=name)(*ws, *gs, *ms, *vs)
    return res[:n], res[n:2 * n], res[2 * n:]


def _adamw(w, g, m, v, tr, name):
    L, R, C = w.shape
    assert R % tr == 0

    def body(*refs):
        _adamw_refs(*refs)

    spec = pl.BlockSpec((1, tr, C), lambda l, i: (l, i, 0))
    return pl.pallas_call(body, grid=(L, R // tr), in_specs=[spec] * 4, out_specs=[spec] * 3,
                          out_shape=[SDS((L, R, C), f32)] * 3, name=name,
                          compiler_params=_params("parallel", "parallel"))(w, g, m, v)


def _rope_tables(S):
    inv = 1.0 / (10000.0 ** (jnp.arange(0, HEAD_DIM, 2, dtype=f32) / HEAD_DIM))
    ang = jnp.arange(S, dtype=f32)[:, None] * inv[None, :]
    cos, sin = jnp.cos(ang), jnp.sin(ang)
    return (jnp.tile(jnp.concatenate([cos, cos], axis=1), (1, N_HEADS_A)),
            jnp.tile(jnp.concatenate([-sin, sin], axis=1), (1, N_HEADS_A)))


def _layer_fwd(x, W, cos, sgn_sin, l, late_weights=None, h1=None, next_pre_mix=None, mid_mixer=None, after_ffn_in=None):
    n = f"l{l}_"
    if h1 is None:
        (h1,) = _rows(_rms_fn, [x], [W["norm_pre_mix"]], [(D_MODEL, bf16)], n + "pre_mix_norm")
    proj = _mm(h1, W["w_in"], "nn", 1024, 768, f32, n + "in_proj")
    attn_out, lse = _attn_fwd(proj, cos, sgn_sin, n + "attn_fwd")
    (qkv,) = _colconv_fwd([(proj, DN_QKV_COL // 512)], [(W["dn_conv_w"], 0)], None, 4, _silu_fn, 3, 512, [f32], n + "dn_conv")
    dn_pre = _dn_prep(qkv, proj, W["dn_a_log"], W["dn_dt_bias"], n + "dn_prep")
    if mid_mixer is not None:
        W = dict(W, dn_norm_w=W["dn_norm_w"] + mid_mixer(dn_pre[0]))
    dn_o, dn_states = _dn_scan(dn_pre, n + "dn_scan")
    (dn_out,) = _rows(_dn_post_fn, [(dn_o, DK, 0), (proj, DK, DN_Z_COL // DK)], [W["dn_norm_w"]], [(DK, bf16)], n + "dn_post",
                      ncol=N_HEADS_D, tm=4 * ROW_TILE)
    mix_in = jnp.concatenate([attn_out, dn_out], axis=1)
    late = late_weights(mix_in) if late_weights is not None else {}
    W = {**W, **late}
    mix = _mm(mix_in, W["w_out"], "nn", 512, 512, f32, n + "out_proj")
    x1, h2 = _rows(_res_rms_pre_fn, [mix, x], [W["norm_post_mix"], W["norm_pre_ffn"]],
                   [(D_MODEL, f32), (D_MODEL, bf16)], n + "post_mix_pre_ffn_norm")
    u0 = _mm(h2, W["ffn_w_in"], "nn", 1024, D_FF // 2, bf16, n + "ffn_in")
    if after_ffn_in is not None:
        W = dict(W, ffn_conv_b=W["ffn_conv_b"] + after_ffn_in(u0))
    nb_ff = D_FF // 256
    (act,) = _colconv_fwd([(u0, 0), (u0, nb_ff)], [(W["ffn_conv_w"], 0), (W["ffn_conv_w"], nb_ff)],
                          [(W["ffn_conv_b"], 0), (W["ffn_conv_b"], nb_ff)], 3, _geglu_fn, nb_ff, 256, [bf16],
                          n + "ffn_conv_glu")
    f = _mm(act, W["ffn_w_out"], "nn", 512, 512, f32, n + "ffn_out")
    if next_pre_mix is None:
        (x2,), h1_next = _rows(_res_rms_fn, [f, x1], [W["norm_post_ffn"]], [(D_MODEL, f32)], n + "post_ffn_norm"), None
    else:
        x2, h1_next = _rows(_res_rms_pre_fn, [f, x1], [W["norm_post_ffn"], next_pre_mix],
                            [(D_MODEL, f32), (D_MODEL, bf16)], n + "post_ffn_next_pre_mix_norm")
    saved = dict(x=x, h1=h1, proj=proj, attn_out=attn_out, lse=lse, qkv=qkv, dn_pre=dn_pre, dn_o=dn_o,
                 dn_states=dn_states, mix_in=mix_in, mix=mix, x1=x1, h2=h2, u0=u0, act=act, f=f, late=late)
    return x2, h1_next, saved


def _layer_bwd(dx2, sv, W, cos, sgn_sin, l, after_ffn=None, next_layer=None, first_layer=True):
    n = f"l{l}_"
    S = dx2.shape[0]
    g = {}
    g_next_pre = None
    if next_layer is None:
        (df,), (g["norm_post_ffn"],) = _rows_vjp(_rms_fn, [sv["f"]], [W["norm_post_ffn"]], [dx2], [0], [0],
                                                 n + "post_ffn_norm_bwd", row_dtype=bf16)
    else:
        (df, dx2), (g["norm_post_ffn"], g_next_pre) = _rows_vjp(
            _res_rms_pre_fn, [sv["f"], sv["x1"]], [W["norm_post_ffn"], next_layer[1]], [dx2, next_layer[0]], [0, 1], [0, 1],
            n + "post_ffn_next_pre_mix_norm_bwd", row_dtype=[bf16, f32])
    dact = _mm(df, W["ffn_w_out"], "nt", 512, D_FF // 2, f32, n + "ffn_out_dx")
    g["ffn_w_out"] = _mm(sv["act"], df, "tn", 256, 1024, f32, n + "ffn_out_dw")
    nb_ff = D_FF // 256
    u0 = sv["u0"]
    dxs, dws, dbs = _colconv_bwd([(u0, 0), (u0, nb_ff)], [(W["ffn_conv_w"], 0), (W["ffn_conv_w"], nb_ff)],
                                 [(W["ffn_conv_b"], 0), (W["ffn_conv_b"], nb_ff)], 3, _geglu_fn, [dact], nb_ff, 256,
                                 n + "ffn_conv_glu_bwd", dx_dtype=bf16)
    du0 = jnp.concatenate(dxs, axis=1)
    g["ffn_conv_w"] = jnp.concatenate(dws, axis=1)
    g["ffn_conv_b"] = jnp.concatenate(dbs, axis=1)
    dh2 = _mm(du0, W["ffn_w_in"], "nt", 512, 512, f32, n + "ffn_in_dx")
    g["ffn_w_in"] = _mm(sv["h2"], du0, "tn", 512, D_FF // 2, f32, n + "ffn_in_dw", column_shards=True)
    if after_ffn is not None:
        W = dict(W, norm_post_mix=W["norm_post_mix"] + after_ffn(g, dh2))
    (dmix, dx1), (g["norm_post_mix"], g["norm_pre_ffn"]) = _rows_vjp(
        _res_rms_pre_fn, [sv["mix"], sv["x"]], [W["norm_post_mix"], W["norm_pre_ffn"]], [dx2, dh2], [0, 1], [0, 1],
        n + "post_mix_pre_ffn_norm_bwd", row_dtype=[bf16, f32])
    dmix_in = _mm(dmix, W["w_out"], "nt", 512, 512, f32, n + "out_proj_dx")
    g["w_out"] = _mm(sv["mix_in"], dmix, "tn", 512, 512, f32, n + "out_proj_dw")

    (ddn_o, dz), (g["dn_norm_w"],) = _rows_vjp(
        _dn_post_fn, [(sv["dn_o"], DK, 0), (sv["proj"], DK, DN_Z_COL // DK)], [W["dn_norm_w"]], [(dmix_in, DK, ATTN_W // DK)],
        [0, 1], [0], n + "dn_post_bwd", ncol=N_HEADS_D, tm=4 * ROW_TILE)
    dpre = _dn_scan_bwd(sv["dn_pre"], sv["dn_states"], ddn_o, n + "dn_scan_bwd")
    dq, dk, dv, dba, g["dn_a_log"], g["dn_dt_bias"] = _dn_prep_bwd(
        sv["qkv"], sv["proj"], W["dn_a_log"], W["dn_dt_bias"], dpre, n + "dn_prep_bwd")
    dqkv = jnp.concatenate([dq, dk, dv], axis=1)
    (dqkv0,), (g["dn_conv_w"],), _ = _colconv_bwd([(sv["proj"], DN_QKV_COL // 512)], [(W["dn_conv_w"], 0)], None, 4, _silu_fn,
                                                 [dqkv], 3, 512, n + "dn_conv_bwd")

    daq, dak, dav = _attn_bwd(sv["proj"], cos, sgn_sin, dmix_in, sv["attn_out"], sv["lse"], n + "attn_bwd")
    dproj = jnp.concatenate([daq, dak, dav, dqkv0, dz, dba, jnp.zeros((S, PROJ_W - DN_BA_COL - 128), f32)], axis=1).astype(bf16)
    dh1 = _mm(dproj, W["w_in"], "nt", 512, 512, f32, n + "in_proj_dx")
    g["w_in"] = _mm(sv["h1"], dproj, "tn", 512, 768, f32, n + "in_proj_dw")
    if not first_layer:
        return (dx1, dh1), g, g_next_pre
    (dx,), (g["norm_pre_mix"],) = _rows_vjp(_rms_fn, [sv["x"]], [W["norm_pre_mix"]], [dh1], [0], [0],
                                            n + "pre_mix_norm_bwd", adds={0: dx1})
    return dx, g, g_next_pre


def _local_step(x, target, layers):
    cos, sgn_sin = _rope_tables(x.shape[0])
    saved, h1 = [], None
    for l, W in enumerate(layers):
        nxt = layers[l + 1]["norm_pre_mix"] if l + 1 < len(layers) else None
        x, h1, sv = _layer_fwd(x, W, cos, sgn_sin, l, h1=h1, next_pre_mix=nxt)
        saved.append(sv)
    loss, dx = _loss_head(x, target, "loss_head")
    grads = [None] * len(layers)
    nxt = None
    for l in reversed(range(len(layers))):
        dx, grads[l], g_pre = _layer_bwd(dx, saved[l], layers[l], cos, sgn_sin, l, next_layer=nxt, first_layer=(l == 0))
        if g_pre is not None:
            grads[l + 1]["norm_pre_mix"] = g_pre
        if l > 0:
            dx, dh1 = dx
            nxt = (dh1, layers[l]["norm_pre_mix"])
    return loss, dx, grads


def _pos():
    x, y, c = lax.axis_index("x"), lax.axis_index("y"), lax.axis_index("c")
    return x, y, c, [(1 - x, y), (x, 1 - y), (1 - x, 1 - y)]


def _rcopy(src, dst, send_sem, recv_sem, dev):
    return pltpu.make_async_remote_copy(src_ref=src, dst_ref=dst, send_sem=send_sem, recv_sem=recv_sem,
                                        device_id=dev, device_id_type=MESH)


def _half_rows(ref, h, which, axis):
    if h is None:
        return ref
    rows = pl.ds(pl.multiple_of(which * h, 16), h)
    return ref.at[:, rows, :] if axis == 1 else ref.at[rows, :]


def _dma_sems(*counts):
    return [pltpu.SemaphoreType.DMA((k,)) for k in counts]


def _all_gather(arrs, halves, name):
    n = len(arrs)

    def body(*refs):
        ins, outs = refs[:n], refs[n:2 * n]
        send1, recv1, send2, recv2 = refs[2 * n:]
        x, y, c, chips = _pos()
        me, sib, s_me = (x, y, c), (x, y, 1 - c), 2 * x + y
        sends = []
        for i in range(n):
            for j, chip in enumerate(chips):
                cp = _rcopy(_half_rows(ins[i], halves[i], c, 1), _half_rows(outs[i].at[s_me], halves[i], c, 1),
                            send1.at[3 * i + j], recv1.at[3 * i + j], (*chip, c))
                cp.start()
                sends.append(cp)
        for i in range(n):
            for j, (px, py) in enumerate(chips):
                k = 3 * i + j
                landed = _half_rows(outs[i].at[2 * px + py], halves[i], c, 1)
                _rcopy(landed, landed, send1.at[k], recv1.at[k], me).wait_recv()
                if halves[i] is not None:
                    cp = _rcopy(landed, landed, send2.at[k], recv2.at[k], sib)
                    cp.start()
                    sends.append(cp)
        for i in range(n):
            if halves[i] is None:
                continue
            for j, (px, py) in enumerate(chips):
                k = 3 * i + j
                other = _half_rows(outs[i].at[2 * px + py], halves[i], 1 - c, 1)
                _rcopy(other, other, send2.at[k], recv2.at[k], me).wait_recv()
        for cp in sends:
            cp.wait_send()

    return pl.pallas_call(
        body, in_specs=[ANY] * n, out_specs=[ANY] * n,
        out_shape=[SDS((4,) + a.shape, a.dtype) for a in arrs],
        scratch_shapes=_dma_sems(3 * n, 3 * n, 3 * n, 3 * n), name=name)(*arrs)


HBM = pl.BlockSpec(memory_space=pltpu.HBM)
SEM = pl.BlockSpec(memory_space=pltpu.SEMAPHORE)
_EFFECT = pltpu.SideEffectType.DATAFLOW_SIDE_EFFECTING


def _in_hbm(a):
    return pltpu.with_memory_space_constraint(a, pltpu.HBM)


def _split_copy(srcs, land_shapes, plan, per, after, name):
    n, nl = len(srcs), len(land_shapes)
    k = per * n

    def body(*refs):
        ins, lands, token = refs[:n], refs[n:n + nl], refs[-1]
        send, recv = refs[n + nl + 1], refs[n + nl + 2]
        for i, (src, dst, dev, _) in enumerate(plan(ins, lands)):
            _rcopy(src, dst, send.at[i], recv.at[i], dev).start()
        token[...] = jnp.zeros_like(token)

    lands = [_in_hbm(lax.empty(s.shape, s.dtype)) for s in land_shapes]
    return pl.pallas_call(
        body, name=name,
        out_shape=(pltpu.SemaphoreType.DMA((k,)), pltpu.SemaphoreType.DMA((k,)),
                   *[pltpu.HBM(a.shape, a.dtype) for a in srcs], *[pltpu.HBM(s.shape, s.dtype) for s in land_shapes],
                   SDS((8, 128), f32)),
        in_specs=[HBM] * (n + nl) + [ANY], out_specs=(SEM, SEM, *[HBM] * (n + nl), pl.BlockSpec(memory_space=pltpu.VMEM)),
        input_output_aliases={i: 2 + i for i in range(n + nl)},
        compiler_params=pltpu.CompilerParams(has_side_effects=_EFFECT))(*[_in_hbm(a) for a in srcs], *lands, after)


def _split_wait(started, n, plan, after, name, nl=None):
    nl = n if nl is None else nl
    send, recv = started[0], started[1]
    thru = started[2:2 + n + nl]

    def body(*refs):
        ins, lands = refs[:n], refs[n:n + nl]
        send_ref, recv_ref = refs[n + nl], refs[n + nl + 1]
        for i, (src, _, dev, mine) in enumerate(plan(ins, lands)):
            cp = _rcopy(src, mine, send_ref.at[i], recv_ref.at[i], dev)
            cp.wait_send()
            cp.wait_recv()

    res = pl.pallas_call(
        body, name=name, out_shape=tuple(pltpu.HBM(a.shape, a.dtype) for a in thru),
        in_specs=[HBM] * (n + nl) + [SEM, SEM, ANY], out_specs=tuple([HBM] * (n + nl)),
        input_output_aliases={i: i for i in range(n + nl)},
        compiler_params=pltpu.CompilerParams(has_side_effects=_EFFECT))(*thru, send, recv, after)
    return res[:n], res[n:]


def _gather_plan(halves):
    def plan(ins, lands):
        x, y, c, chips = _pos()
        out = []
        for i in range(len(ins)):
            for px, py in chips:
                out.append((_half_rows(ins[i], halves[i], c, 1), _half_rows(lands[i].at[2 * x + y], halves[i], c, 1),
                            (px, py, c), _half_rows(lands[i].at[2 * px + py], halves[i], c, 1)))
        return out
    return plan


def _scatter_plan(ins, lands):
    x, y, c, chips = _pos()
    out = []
    for i in range(len(ins)):
        for j, (px, py) in enumerate(chips):
            out.append((ins[i].at[2 * px + py], lands[i].at[j], (px, py, c), lands[i].at[j]))
    return out


def _exchange_plan(ins, lands):
    x, y, c, _ = _pos()
    return [(_half_rows(g, g.shape[1] // 2, 1 - c, 1), land, (x, y, 1 - c), land) for g, land in zip(ins, lands)]


def _pass_plan(halves):
    def plan(ins, lands):
        x, y, c, chips = _pos()
        out = []
        for i in range(len(ins)):
            for px, py in chips:
                slot = ins[i].at[2 * px + py]
                out.append((_half_rows(slot, halves[i], c, 1), _half_rows(slot, halves[i], c, 1), (x, y, 1 - c),
                            _half_rows(slot, halves[i], 1 - c, 1)))
        return out
    return plan


def _exchange_halves(gs, name):
    n = len(gs)

    def body(*refs):
        ins, outs = refs[:n], refs[n:2 * n]
        send, recv = refs[2 * n:]
        x, y, c, _ = _pos()
        sends = []
        for k in range(n):
            cp = _rcopy(_half_rows(ins[k], gs[k].shape[1] // 2, 1 - c, 1), outs[k], send.at[k], recv.at[k], (x, y, 1 - c))
            cp.start()
            sends.append(cp)
        for k in range(n):
            _rcopy(outs[k], outs[k], send.at[k], recv.at[k], (x, y, c)).wait_recv()
        for cp in sends:
            cp.wait_send()

    return pl.pallas_call(
        body, in_specs=[ANY] * n, out_specs=[ANY] * n,
        out_shape=[SDS((4, g.shape[1] // 2, g.shape[2]), g.dtype) for g in gs],
        scratch_shapes=_dma_sems(n, n), name=name)(*gs)


def _scatter_partials(ps, name):
    n = len(ps)

    def body(*refs):
        ins, outs = refs[:n], refs[n:2 * n]
        send, recv = refs[2 * n:]
        x, y, c, chips = _pos()
        sends = []
        for k in range(n):
            for j, (px, py) in enumerate(chips):
                cp = _rcopy(ins[k].at[2 * px + py], outs[k].at[j], send.at[3 * k + j], recv.at[3 * k + j], (px, py, c))
                cp.start()
                sends.append(cp)
        for k in range(n):
            for j in range(3):
                _rcopy(outs[k].at[j], outs[k].at[j], send.at[3 * k + j], recv.at[3 * k + j], (x, y, c)).wait_recv()
        for cp in sends:
            cp.wait_send()

    return pl.pallas_call(
        body, in_specs=[ANY] * n, out_specs=[ANY] * n,
        out_shape=[SDS((3,) + p.shape[1:], p.dtype) for p in ps],
        scratch_shapes=_dma_sems(3 * n, 3 * n), name=name)(*ps)


def _join_halves(rs, layers, name):
    n = len(rs)

    def body(*refs):
        outs = refs[n:2 * n]
        send, recv = refs[2 * n:]
        x, y, c, _ = _pos()

        def half(k, which):
            h = rs[k].shape[1] // 2
            return _half_rows(outs[k], h, which, 1) if layers[k] is None else _half_rows(outs[k].at[layers[k]], h, which, 0)

        sends = []
        for k in range(n):
            cp = _rcopy(half(k, c), half(k, c), send.at[k], recv.at[k], (x, y, 1 - c))
            cp.start()
            sends.append(cp)
        for k in range(n):
            _rcopy(half(k, 1 - c), half(k, 1 - c), send.at[k], recv.at[k], (x, y, c)).wait_recv()
        for cp in sends:
            cp.wait_send()

    return pl.pallas_call(
        body, in_specs=[ANY] * n, out_specs=[ANY] * n, out_shape=[SDS(r.shape, r.dtype) for r in rs],
        input_output_aliases={k: k for k in range(n)}, scratch_shapes=_dma_sems(n, n), name=name)(*rs)


def _all_reduce_small(pack, name):
    R = pack.shape[0]

    def body(in_ref, out_ref, buf, send, recv):
        x, y, c, _ = _pos()
        me = 4 * x + 2 * y + c
        buf[me] = in_ref[...]
        sends = []
        for k in range(1, 8):
            peer = me ^ k
            cp = _rcopy(buf.at[me], buf.at[me], send.at[k - 1], recv.at[k - 1], ((peer >> 2) & 1, (peer >> 1) & 1, peer & 1))
            cp.start()
            sends.append(cp)
        for k in range(1, 8):
            _rcopy(buf.at[me ^ k], buf.at[me ^ k], send.at[k - 1], recv.at[k - 1], (x, y, c)).wait_recv()
        for cp in sends:
            cp.wait_send()
        acc = buf[0]
        for d in range(1, 8):
            acc = acc + buf[d]
        out_ref[...] = acc

    return pl.pallas_call(
        body, out_shape=SDS((R, 128), f32),
        in_specs=[pl.BlockSpec(memory_space=pltpu.VMEM)], out_specs=pl.BlockSpec(memory_space=pltpu.VMEM),
        scratch_shapes=[pltpu.VMEM((8, R, 128), f32)] + _dma_sems(7, 7), name=name)(pack)


def _add_sibling(g, recv, c_arr, tr, name):
    _, R, C = g.shape
    h = R // 2
    nrb = h // tr
    assert h % tr == 0

    def body(c_ref, g_ref, r_ref, o_ref):
        o_ref[...] = (g_ref[...] + r_ref[...]).astype(o_ref.dtype)

    spec = pl.BlockSpec((1, tr, C), lambda s, r, c_ref: (s, r, 0))
    grid_spec = pltpu.PrefetchScalarGridSpec(
        num_scalar_prefetch=1, grid=(4, nrb),
        in_specs=[pl.BlockSpec((1, tr, C), lambda s, r, c_ref: (s, c_ref[0] * nrb + r, 0)), spec], out_specs=spec)
    return pl.pallas_call(body, grid_spec=grid_spec, out_shape=SDS((4, h, C), bf16), name=name,
                          compiler_params=_params("parallel", "parallel"))(c_arr, g, recv)


def _add_chips(p, recv, sc_arr, tr, layer, into, name):
    _, h, C = p.shape
    nrb = h // tr
    assert h % tr == 0

    def body(sc_ref, p_ref, r_ref, *rest):
        rest[-1][...] = (p_ref[0].astype(f32) + r_ref[0].astype(f32)) + (r_ref[1].astype(f32) + r_ref[2].astype(f32))

    grid_spec = pltpu.PrefetchScalarGridSpec(
        num_scalar_prefetch=1, grid=(nrb,),
        in_specs=[pl.BlockSpec((1, tr, C), lambda r, sc_ref: (sc_ref[0], r, 0)),
                  pl.BlockSpec((3, tr, C), lambda r, sc_ref: (0, r, 0))] + ([] if into is None else [ANY]),
        out_specs=pl.BlockSpec((None, tr, C), lambda r, sc_ref: (layer, sc_ref[1] * nrb + r, 0)))
    return pl.pallas_call(body, grid_spec=grid_spec, out_shape=SDS((2, 2 * h, C), f32), name=name,
                          input_output_aliases={} if into is None else {3: 0},
                          compiler_params=_params("parallel"))(sc_arr, p, recv, *([] if into is None else [into]))


_BIG = (("w_in", 1024, 256), ("w_out", 256, 128), ("ffn_w_in", 1024, 256), ("ffn_w_out", 704, 352))
_SMALL = ("dn_conv_w", "ffn_conv_w", "ffn_conv_b", "norm_pre_mix", "norm_post_mix", "norm_pre_ffn", "norm_post_ffn",
          "dn_norm_w", "dn_a_log", "dn_dt_bias")
_WEIGHTS = ("w_in", "dn_conv_w", "dn_a_log", "dn_dt_bias", "dn_norm_w", "w_out", "ffn_w_in", "ffn_conv_w", "ffn_conv_b",
            "ffn_w_out", "norm_pre_mix", "norm_post_mix", "norm_pre_ffn", "norm_post_ffn")
_ADAM_ROWS = {"w_in": 256, "w_out": 256, "ffn_w_in": 128, "ffn_w_out": 176}


def _shard_major(name, g):
    if name == "w_in":
        width = IN_COLS // N_SHARDS
        return jnp.stack([g[:, width * s:width * (s + 1)] for s in range(N_SHARDS)])
    if name == "ffn_w_in":
        return g
    return g.reshape(4, g.shape[0] // 4, g.shape[1])


def kernel(x, w_in, dn_conv_w, dn_a_log, dn_dt_bias, dn_norm_w, w_out, ffn_w_in, ffn_conv_w, ffn_conv_b, ffn_w_out, norm_pre_mix, norm_post_mix, norm_pre_ffn, norm_post_ffn, loss_target, m_w_in, m_dn_conv_w, m_dn_a_log, m_dn_dt_bias, m_dn_norm_w, m_w_out, m_ffn_w_in, m_ffn_conv_w, m_ffn_conv_b, m_ffn_w_out, m_norm_pre_mix, m_norm_post_mix, m_norm_pre_ffn, m_norm_post_ffn, v_w_in, v_dn_conv_w, v_dn_a_log, v_dn_dt_bias, v_dn_norm_w, v_w_out, v_ffn_w_in, v_ffn_conv_w, v_ffn_conv_b, v_ffn_w_out, v_norm_pre_mix, v_norm_post_mix, v_norm_pre_ffn, v_norm_post_ffn):
    w = dict(w_in=w_in, dn_conv_w=dn_conv_w, dn_a_log=dn_a_log, dn_dt_bias=dn_dt_bias, dn_norm_w=dn_norm_w, w_out=w_out,
             ffn_w_in=ffn_w_in, ffn_conv_w=ffn_conv_w, ffn_conv_b=ffn_conv_b, ffn_w_out=ffn_w_out, norm_pre_mix=norm_pre_mix,
             norm_post_mix=norm_post_mix, norm_pre_ffn=norm_pre_ffn, norm_post_ffn=norm_post_ffn)
    m = dict(w_in=m_w_in, dn_conv_w=m_dn_conv_w, dn_a_log=m_dn_a_log, dn_dt_bias=m_dn_dt_bias, dn_norm_w=m_dn_norm_w,
             w_out=m_w_out, ffn_w_in=m_ffn_w_in, ffn_conv_w=m_ffn_conv_w, ffn_conv_b=m_ffn_conv_b, ffn_w_out=m_ffn_w_out,
             norm_pre_mix=m_norm_pre_mix, norm_post_mix=m_norm_post_mix, norm_pre_ffn=m_norm_pre_ffn,
             norm_post_ffn=m_norm_post_ffn)
    v = dict(w_in=v_w_in, dn_conv_w=v_dn_conv_w, dn_a_log=v_dn_a_log, dn_dt_bias=v_dn_dt_bias, dn_norm_w=v_dn_norm_w,
             w_out=v_w_out, ffn_w_in=v_ffn_w_in, ffn_conv_w=v_ffn_conv_w, ffn_conv_b=v_ffn_conv_b, ffn_w_out=v_ffn_w_out,
             norm_pre_mix=v_norm_pre_mix, norm_post_mix=v_norm_post_mix, norm_pre_ffn=v_norm_pre_ffn,
             norm_post_ffn=v_norm_post_ffn)
    xi, yi, ci = lax.axis_index("x"), lax.axis_index("y"), lax.axis_index("c")
    s_me = 2 * xi + yi
    c_arr = jnp.reshape(ci, (1,)).astype(jnp.int32)
    sc_arr = jnp.stack([s_me, ci]).astype(jnp.int32)

    mats = [name for name, _, _ in _BIG]
    rest = mats[1:]
    half_of = {name: rows // 2 for name, rows, _ in _BIG}
    tiles = {name: tr for name, _, tr in _BIG}
    gathered_shape = lambda a: SDS((4,) + a.shape, a.dtype)

    own = {k: w[k].astype(bf16) for k in mats}
    plan_in = _gather_plan([half_of["w_in"], None, None])
    src_in = [own["w_in"][0:1], dn_conv_w, ffn_conv_w]
    started_in = _split_copy(src_in, [gathered_shape(a) for a in src_in], plan_in, 3, src_in[0], "weights_gather_w_in0_start")
    plan0 = _gather_plan([half_of[k] for k in rest])
    src0 = [own[k][0:1] for k in rest]
    started0 = _split_copy(src0, [gathered_shape(a) for a in src0], plan0, 3, started_in[-1], "weights_gather_l0_start")
    plan1 = _gather_plan([half_of[k] for k in mats])
    src1 = [own[k][1:2] for k in mats]
    started1 = _split_copy(src1, [gathered_shape(a) for a in src1], plan1, 3, started0[-1], "weights_gather_l1_start")
    _, landed_in = _split_wait(started_in, len(src_in), plan_in, started1[-1], "weights_gather_w_in0_wait")
    pass_in = _pass_plan([half_of["w_in"]])
    passed_in = _split_copy(landed_in[:1], [], pass_in, 3, landed_in[0], "weights_pass_w_in0_start")
    got_in = list(_split_wait(passed_in, 1, pass_in, passed_in[-1], "weights_pass_w_in0_wait", nl=0)[0]) + list(landed_in[1:])

    def pick(mine, gathered):
        return [jnp.where(s_me == s, mine, gathered[s]) for s in range(4)]

    conv = {"dn_conv_w": jnp.concatenate(pick(dn_conv_w, got_in[1]), axis=-1),
            "ffn_conv_w": jnp.concatenate(pick(ffn_conv_w, got_in[2]), axis=-1)}
    lanes = lambda a: jnp.pad(a, ((0, 0), (0, 128 - a.shape[1])))
    vec = dict(dn_a_log=lanes(dn_a_log), dn_dt_bias=lanes(dn_dt_bias), dn_norm_w=dn_norm_w, ffn_conv_b=ffn_conv_b,
               norm_pre_mix=norm_pre_mix, norm_post_mix=norm_post_mix, norm_pre_ffn=norm_pre_ffn, norm_post_ffn=norm_post_ffn)

    def matrices(l, names, gathered):
        W = {}
        for k, a in zip(names, gathered):
            if k in ("w_out", "ffn_w_out"):
                rows_, cols = own[k].shape[1:]
                W[k] = lax.dynamic_update_slice(a[:, 0], own[k][l][None], (s_me, 0, 0)).reshape(4 * rows_, cols)
            else:
                cat = jnp.concatenate(pick(own[k][l], a[:, 0]), axis=-1)
                W[k] = jnp.pad(cat, ((0, 0), (0, PROJ_W - IN_COLS))) if k == "w_in" else cat
        return W

    def small_weights(l):
        return {**{k: a[l] for k, a in conv.items()}, **{k: a[l:l + 1] for k, a in vec.items()}}

    pass0, pass1 = _pass_plan([half_of[k] for k in rest]), _pass_plan([half_of[k] for k in mats])
    passing = {}

    def mid_mixer_l0(marker):
        _, landed = _split_wait(started0, len(rest), plan0, marker, "weights_gather_l0_wait")
        passing["l0"] = _split_copy(landed, [], pass0, 3, marker, "weights_pass_l0_start")
        return passing["l0"][-1][0, 0]

    def late_l0(mix_in):
        return matrices(0, rest, _split_wait(passing["l0"], len(rest), pass0, mix_in, "weights_pass_l0_wait", nl=0)[0])

    def after_ffn_in_l0(marker):
        _, landed = _split_wait(started1, len(mats), plan1, marker, "weights_gather_l1_wait")
        passing["l1"] = _split_copy(landed, [], pass1, 3, marker, "weights_pass_l1_start")
        return passing["l1"][-1][0, 0]

    cos, sgn_sin = _rope_tables(x.shape[1])
    W0 = {**small_weights(0), **matrices(0, ["w_in"], got_in[:1])}
    W0_first = dict(W0, norm_pre_mix=W0["norm_pre_mix"] + started1[-1][0, 0])
    x1, h1_l1, saved0 = _layer_fwd(x[0], W0_first, cos, sgn_sin, 0, late_weights=late_l0, next_pre_mix=norm_pre_mix[1:2],
                                   mid_mixer=mid_mixer_l0, after_ffn_in=after_ffn_in_l0)
    W1 = {**small_weights(1),
          **matrices(1, mats, _split_wait(passing["l1"], len(mats), pass1, x1, "weights_pass_l1_wait", nl=0)[0])}
    x2, _, saved1 = _layer_fwd(x1, W1, cos, sgn_sin, 1, h1=h1_l1)
    loss_local, dy = _loss_head(x2, loss_target[0], "loss_head")
    loss = lax.psum(loss_local, ("x", "y", "c"))

    def shard_major(names, grads_l):
        return [_shard_major(name, grads_l[name]) for name in names]

    def add_siblings(l, names, gs, from_sib):
        return [_add_sibling(g, r, c_arr, tiles[name], f"add_sibling_{name}{l}") for g, r, name in zip(gs, from_sib, names)]

    def scatter_start(l, names, parts, after, tag):
        return _split_copy(parts, [SDS((3,) + p.shape[1:], p.dtype) for p in parts], _scatter_plan, 3, after,
                           f"grads_l{l}{tag}_scatter_start")

    def owner_sums(l, names, sent, after, tag, into):
        parts, recvd = _split_wait(sent, len(names), _scatter_plan, after, f"grads_l{l}{tag}_scatter_wait")
        return {name: _add_chips(p, r, sc_arr, tiles[name], l, into.get(name), f"add_chips_{name}{l}")
                for p, r, name in zip(parts, recvd, names)}

    (dx1, dh1_l1), grads1, _ = _layer_bwd(dy, saved1, W1, cos, sgn_sin, 1, first_layer=False)
    gs1 = shard_major(mats, grads1)
    swap1 = _split_copy(gs1, [SDS((4, g.shape[1] // 2, g.shape[2]), g.dtype) for g in gs1], _exchange_plan, 1, dx1,
                        "grads_l1_sibling_start")
    ffn = ["ffn_w_in", "ffn_w_out"]
    launched = {}

    def after_ffn_l0(g_ffn, dx_mid):
        gs1_, from_sib1 = _split_wait(swap1, len(mats), _exchange_plan, dx_mid, "grads_l1_sibling_wait")
        launched["l1"] = scatter_start(1, mats, add_siblings(1, mats, gs1_, from_sib1), dx_mid, "")
        gs0 = shard_major(ffn, g_ffn)
        from_sib0 = _exchange_halves(gs0, "grads_l0_ffn_to_sibling")
        launched["l0_ffn"] = scatter_start(0, ffn, add_siblings(0, ffn, gs0, from_sib0), launched["l1"][-1], "_ffn")
        return launched["l0_ffn"][-1][0, 0]

    W0_last = dict(W0, **saved0["late"], norm_post_ffn=W0["norm_post_ffn"] + swap1[-1][0, 0])
    dx, grads0, grads1["norm_pre_mix"] = _layer_bwd(dx1, saved0, W0_last, cos, sgn_sin, 0, after_ffn=after_ffn_l0,
                                                    next_layer=(dh1_l1, norm_pre_mix[1:2]))
    mix = ["w_in", "w_out"]
    gs0 = shard_major(mix, grads0)
    part0 = add_siblings(0, mix, gs0, _exchange_halves(gs0, "grads_l0_mix_to_sibling"))
    sent0 = scatter_start(0, mix, part0, dx, "_mix")
    red = owner_sums(0, ffn, launched["l0_ffn"], sent0[-1], "_ffn", {})
    red = owner_sums(1, mats, launched["l1"], sent0[-1], "", red)
    joined = dict(zip(mats, _join_halves([red[k] for k in mats], [1 if k in mix else None for k in mats],
                                         "grads_join_early")))
    grads = [grads0, grads1]

    small = {}
    for name in _SMALL:
        per_layer = [grads[l][name] for l in range(2)]
        if name in ("dn_a_log", "dn_dt_bias"):
            per_layer = [p[:, :N_HEADS_D] for p in per_layer]
        small[name] = jnp.stack(per_layer).reshape((2,) + (w[name].shape[1:] if name not in ("dn_conv_w", "ffn_conv_w")
                                                           else per_layer[0].shape))
    flat = jnp.concatenate([small[name].reshape(-1) for name in _SMALL])
    n_rows = -(-flat.shape[0] // 1024) * 8
    summed = _all_reduce_small(jnp.pad(flat, (0, n_rows * 128 - flat.shape[0])).reshape(n_rows, 128),
                               "small_grads_all_reduce").reshape(-1)
    off = 0
    g_out = {}
    for name in _SMALL:
        size = small[name].size
        g_out[name] = summed[off:off + size].reshape(small[name].shape)
        off += size
    for k in ("dn_conv_w", "ffn_conv_w"):
        width = w[k].shape[2]
        g_out[k] = lax.dynamic_slice_in_dim(g_out[k], s_me * width, width, axis=2)
    for k in ffn:
        g_out[k] = joined[k]

    deltas, new_m, new_v = {}, {}, {}

    def step(name):
        shape = w[name].shape
        as3 = (lambda a: a) if len(shape) == 3 else (lambda a: a.reshape(shape[0], 1, shape[1]))
        tr = _ADAM_ROWS.get(name, as3(w[name]).shape[1])
        d_, m_, v_ = _adamw(as3(w[name]), as3(g_out[name]), as3(m[name]), as3(v[name]), tr, f"adamw_{name}")
        deltas[name], new_m[name], new_v[name] = d_.reshape(shape), m_.reshape(shape), v_.reshape(shape)

    for name in ffn:
        step(name)
    tiny = [name for name in _WEIGHTS if name not in mats]
    stepped = _adamw_small(*[[d[name] for name in tiny] for d in (w, g_out, m, v)], "adamw_small")
    for out, vals in zip((deltas, new_m, new_v), stepped):
        out.update(zip(tiny, vals))
    done = jnp.reshape(deltas["ffn_w_in"][0, 0, 0] + deltas["ffn_w_out"][0, 0, 0] + deltas["norm_post_ffn"][0, 0], (1,))
    red = owner_sums(0, mix, sent0, done, "_mix", joined)
    for k, a in zip(mix, _join_halves([red[k] for k in mix], [0] * len(mix), "grads_join_late")):
        g_out[k] = a
        step(k)

    return (loss, dx[None], *[g_out[k] for k in _WEIGHTS], *[deltas[k] for k in _WEIGHTS],
            *[new_m[k] for k in _WEIGHTS], *[new_v[k] for k in _WEIGHTS])
```

```python
import jax
import jax.numpy as jnp
from jax import lax
from jax.experimental import pallas as pl
from jax.experimental.pallas import tpu as pltpu

f32, bf16 = jnp.float32, jnp.bfloat16
SDS = jax.ShapeDtypeStruct
HI = lax.Precision.HIGH
MESH = pl.DeviceIdType.MESH
ANY = pl.BlockSpec(memory_space=pl.ANY)

D_MODEL = 1024
N_HEADS_A, HEAD_DIM = 8, 64
ATTN_W = 512
N_HEADS_D, DK = 4, 128
CHUNK = 64
D_FF = 2816
IN_COLS = 3592
PROJ_W = 3840
DN_QKV_COL = 3 * ATTN_W
DN_Z_COL = DN_QKV_COL + 3 * N_HEADS_D * DK
DN_BA_COL = DN_Z_COL + N_HEADS_D * DK
N_SHARDS = 4
BRANCHES = ((1, 16), (4, 4), (16, 1))
EPS = 1e-6
NEG = -1e30
ROW_TILE = 512
VMEM_LIMIT = 56 * 1024 * 1024

ADAM_LR, ADAM_B1, ADAM_B2, ADAM_EPS, ADAM_WD, ADAM_STEP = 0.001, 0.9, 0.999, 1e-08, 0.01, 10


def _params(*sem):
    return pltpu.CompilerParams(dimension_semantics=sem, vmem_limit_bytes=VMEM_LIMIT)


def _mm(a, b, mode, tm, tn, out_dtype, name, column_shards=False):
    if mode == "nn":
        (M, K), N = a.shape, b.shape[1]
        dims = (((1,), (0,)), ((), ()))
        a_spec = pl.BlockSpec((tm, K), lambda i, j: (i, 0))
        b_spec = pl.BlockSpec((K, tn), lambda i, j: (0, j))
    elif mode == "nt":
        (M, K), N = a.shape, b.shape[0]
        dims = (((1,), (1,)), ((), ()))
        a_spec = pl.BlockSpec((tm, K), lambda i, j: (i, 0))
        b_spec = pl.BlockSpec((tn, K), lambda i, j: (j, 0))
    else:
        (K, M), N = a.shape, b.shape[1]
        dims = (((0,), (0,)), ((), ()))
        a_spec = pl.BlockSpec((K, tm), lambda i, j: (0, i))
        b_spec = pl.BlockSpec((K, tn), lambda i, j: (0, j))
    assert M % tm == 0 and N % tn == 0, (name, M, N, tm, tn)

    def body(a_ref, b_ref, o_ref):
        o_ref[...] = lax.dot_general(a_ref[...].astype(bf16), b_ref[...].astype(bf16), dims,
                                     preferred_element_type=f32).astype(o_ref.dtype)

    if column_shards:
        out_spec, out_shape = pl.BlockSpec((None, tm, tn), lambda i, j: (j, i, 0)), SDS((N // tn, M, tn), out_dtype)
    else:
        out_spec, out_shape = pl.BlockSpec((tm, tn), lambda i, j: (i, j)), SDS((M, N), out_dtype)
    return pl.pallas_call(body, grid=(M // tm, N // tn), in_specs=[a_spec, b_spec], out_specs=out_spec,
                          out_shape=out_shape, name=name, compiler_params=_params("parallel", "arbitrary"))(a, b)


def _row_spec(r, tm):
    if isinstance(r, tuple):
        arr, width, cb = r
        return arr, pl.BlockSpec((tm, width), lambda i, j, cb=cb: (i, cb + j))
    return r, pl.BlockSpec((tm, r.shape[1]), lambda i, j: (i, j))


def _full_spec(p):
    return pl.BlockSpec(p.shape, lambda i, j: (0,) * p.ndim)


def _rows(fn, rows, params, outs, name, tm=ROW_TILE, ncol=1):
    arrs, specs = zip(*[_row_spec(r, tm) for r in rows])
    S = arrs[0].shape[0]
    nr, npar = len(rows), len(params)

    def body(*refs):
        vals = fn(*[r[...].astype(f32) for r in refs[:nr]], *[p[...] for p in refs[nr:nr + npar]])
        for o_ref, v in zip(refs[nr + npar:], vals):
            o_ref[...] = v.astype(o_ref.dtype)

    return pl.pallas_call(
        body, grid=(S // tm, ncol), in_specs=list(specs) + [_full_spec(p) for p in params],
        out_specs=[pl.BlockSpec((tm, w), lambda i, j: (i, j)) for w, _ in outs],
        out_shape=[SDS((S, w * ncol), dt) for w, dt in outs], name=name,
        compiler_params=_params("parallel", "parallel"))(*arrs, *params)


def _rows_vjp(fn, rows, params, cts, wrt_rows, wrt_params, name, adds=None, tm=ROW_TILE, ncol=1, row_dtype=f32):
    adds = adds or {}
    arrs, specs = zip(*[_row_spec(r, tm) for r in rows])
    carrs, cspecs = zip(*[_row_spec(c, tm) for c in cts])
    add_keys = sorted(adds)
    aarrs = [adds[k] for k in add_keys]
    S = arrs[0].shape[0]
    nr, npar, nc, na = len(rows), len(params), len(cts), len(aarrs)
    widths = [specs[k].block_shape[1] for k in wrt_rows]
    row_dtypes = row_dtype if isinstance(row_dtype, (list, tuple)) else [row_dtype] * len(wrt_rows)

    def body(*refs):
        first = jnp.logical_and(pl.program_id(0) == 0, pl.program_id(1) == 0)
        rv = [r[...].astype(f32) for r in refs[:nr]]
        pv = [p[...] for p in refs[nr:nr + npar]]
        cv = tuple(c[...].astype(f32) for c in refs[nr + npar:nr + npar + nc])
        av = dict(zip(add_keys, refs[nr + npar + nc:nr + npar + nc + na]))
        o = refs[nr + npar + nc + na:]
        _, vjp = jax.vjp(fn, *rv, *pv)
        g = vjp(cv)
        for n, k in enumerate(wrt_rows):
            val = g[k]
            if k in av:
                val = val + av[k][...]
            o[n][...] = val.astype(o[n].dtype)
        for n, k in enumerate(wrt_params):
            ref = o[len(wrt_rows) + n]

            @pl.when(first)
            def _(ref=ref):
                ref[...] = jnp.zeros_like(ref)

            ref[...] += g[nr + k]

    res = pl.pallas_call(
        body, grid=(S // tm, ncol),
        in_specs=list(specs) + [_full_spec(p) for p in params] + list(cspecs)
        + [pl.BlockSpec((tm, a.shape[1] // ncol), lambda i, j: (i, j)) for a in aarrs],
        out_specs=[pl.BlockSpec((tm, w), lambda i, j: (i, j)) for w in widths] + [_full_spec(params[k]) for k in wrt_params],
        out_shape=[SDS((S, w * ncol), dt) for w, dt in zip(widths, row_dtypes)]
        + [SDS(params[k].shape, f32) for k in wrt_params],
        name=name, compiler_params=_params("arbitrary", "arbitrary"))(*arrs, *params, *carrs, *aarrs)
    return res[:len(wrt_rows)], res[len(wrt_rows):]


def _rms(x, w):
    return x * lax.rsqrt(jnp.mean(x * x, axis=-1, keepdims=True) + EPS) * w


def _rms_fn(x, w):
    return (_rms(x, w),)


def _res_rms_fn(f, res, w):
    return (res + _rms(f, w),)


def _res_rms_pre_fn(f, res, w_post, w_pre):
    x1 = res + _rms(f, w_post)
    return x1, _rms(x1, w_pre)


def _swap_halves(x):
    lane = lax.broadcasted_iota(jnp.int32, x.shape, 1)
    first = (lane % HEAD_DIM) < (HEAD_DIM // 2)
    n = x.shape[1]
    return jnp.where(first, pltpu.roll(x, n - HEAD_DIM // 2, 1), pltpu.roll(x, HEAD_DIM // 2, 1))


def _rope_fwd_fn(q, k, cos, sgn_sin):
    scale = HEAD_DIM ** -0.5
    return ((q * cos + _swap_halves(q) * sgn_sin) * scale, k * cos + _swap_halves(k) * sgn_sin)


def _rope_bwd_fn(dq, dk, cos, sgn_sin):
    dq = dq * (HEAD_DIM ** -0.5)
    return (dq * cos + _swap_halves(dq * sgn_sin), dk * cos + _swap_halves(dk * sgn_sin))


def _nt(a, b):
    return lax.dot_general(a, b, (((1,), (1,)), ((), ())), preferred_element_type=f32)


def _tn(a, b):
    return lax.dot_general(a, b, (((0,), (0,)), ((), ())), preferred_element_type=f32)


def _band_rows(j, d, nb):
    r, i = j // nb, j % nb
    if d == 1:
        cur = pl.ds(pl.multiple_of(i * 128, 128), 128)
        prev = pl.ds(pl.multiple_of(jnp.maximum(i - 1, 0) * 128, 128), 128)
    else:
        cur = pl.ds(i * (128 * d) + r, 128, stride=d)
        prev = pl.ds(jnp.maximum(i - 1, 0) * (128 * d) + r, 128, stride=d)
    return cur, prev, (i == 0).astype(jnp.int32)


def _band_bias(bias_ref):
    a = lax.broadcasted_iota(jnp.int32, (256, 256), 0) % 128
    c = lax.broadcasted_iota(jnp.int32, (256, 256), 1)
    own = jnp.logical_and(c < 128, c <= a)
    before = jnp.logical_and(c >= 128, c - 128 >= a)
    bias_ref[0] = jnp.where(jnp.logical_or(own, before), 0.0, NEG)
    bias_ref[1] = jnp.where(own, 0.0, NEG)


def _stack_heads(x, head_a):
    return jnp.concatenate([jnp.where(head_a, x, 0.0), jnp.where(head_a, 0.0, x)], axis=0)


def _unstack_heads(x2, head_a):
    return jnp.where(head_a, x2[:128], x2[128:])


def _pair_at(S, first_col):
    return pl.BlockSpec((S, 128), lambda h: (0, first_col // 128 + h))


def _attn_fwd(proj, cos, sgn_sin, name):
    S = proj.shape[0]
    nblk = S // 128

    def body(qp_ref, kp_ref, v_ref, cos_ref, sin_ref, out_ref, lse_ref, bias_ref, q_ref, k_ref, *scr):
        head_a = lax.broadcasted_iota(jnp.int32, (1, 128), 1) < HEAD_DIM
        _band_bias(bias_ref)
        q_ref[...], k_ref[...] = _rope_fwd_fn(qp_ref[...], kp_ref[...], cos_ref[...], sin_ref[...])
        for b, (d, nb) in enumerate(BRANCHES):
            ob_ref, lb_ref = scr[2 * b], scr[2 * b + 1]

            def blk(j, carry, d=d, nb=nb, ob_ref=ob_ref, lb_ref=lb_ref):
                cur, prev, first = _band_rows(j, d, nb)
                q2 = _stack_heads(q_ref[cur, :], head_a).astype(bf16)
                if nb == 1:
                    k2, v2, bias = k_ref[cur, :].astype(bf16), v_ref[cur, :].astype(bf16), bias_ref[1][:, :128]
                else:
                    k2 = jnp.concatenate([k_ref[cur, :], k_ref[prev, :]], axis=0).astype(bf16)
                    v2 = jnp.concatenate([v_ref[cur, :], v_ref[prev, :]], axis=0).astype(bf16)
                    bias = bias_ref[first]
                s = _nt(q2, k2) + bias
                mx = jnp.max(s, axis=1, keepdims=True)
                p = jnp.exp(s - mx)
                l = jnp.sum(p, axis=1, keepdims=True)
                o = jnp.dot(p.astype(bf16), v2, preferred_element_type=f32) / l
                ob_ref[cur, :] = _unstack_heads(o, head_a)
                lb_ref[cur, :] = _unstack_heads(jnp.broadcast_to(mx + jnp.log(l), (256, 128)), head_a)
                return carry

            lax.fori_loop(0, nblk, blk, 0, unroll=16)
        l0, l1, l2 = scr[1][...], scr[3][...], scr[5][...]
        mx = jnp.maximum(jnp.maximum(l0, l1), l2)
        e0, e1, e2 = jnp.exp(l0 - mx), jnp.exp(l1 - mx), jnp.exp(l2 - mx)
        den = e0 + e1 + e2
        out_ref[...] = ((e0 * scr[0][...] + e1 * scr[2][...] + e2 * scr[4][...]) / den).astype(out_ref.dtype)
        lse_ref[...] = mx + jnp.log(den)

    pair = pl.BlockSpec((S, 128), lambda h: (0, h))
    return pl.pallas_call(
        body, grid=(N_HEADS_A // 2,),
        in_specs=[pair, _pair_at(S, ATTN_W), _pair_at(S, 2 * ATTN_W), pair, pair], out_specs=[pair, pair],
        out_shape=[SDS((S, ATTN_W), bf16), SDS((S, ATTN_W), f32)],
        scratch_shapes=[pltpu.VMEM((2, 256, 256), f32)] + [pltpu.VMEM((S, 128), f32)] * 8,
        name=name, compiler_params=_params("parallel"))(proj, proj, proj, cos, sgn_sin)


def _attn_bwd(proj, cos, sgn_sin, dmix_in, out, lse, name):
    S = proj.shape[0]
    nblk = S // 128

    def body(qp_ref, kp_ref, v_ref, cos_ref, sin_ref, do_ref, out_ref, lse_ref, dq_ref, dk_ref, dv_ref,
             bias_ref, t_ref, q_ref, k_ref):
        head_a = lax.broadcasted_iota(jnp.int32, (1, 128), 1) < HEAD_DIM
        _band_bias(bias_ref)
        q_ref[...], k_ref[...] = _rope_fwd_fn(qp_ref[...], kp_ref[...], cos_ref[...], sin_ref[...])
        x = do_ref[...] * out_ref[...].astype(f32)
        t_ref[...] = jnp.where(head_a, jnp.sum(jnp.where(head_a, x, 0.0), axis=1, keepdims=True),
                               jnp.sum(jnp.where(head_a, 0.0, x), axis=1, keepdims=True))
        dq_ref[...] = jnp.zeros_like(dq_ref)
        dk_ref[...] = jnp.zeros_like(dk_ref)
        dv_ref[...] = jnp.zeros_like(dv_ref)
        for d, nb in BRANCHES:
            def blk(j, carry, d=d, nb=nb):
                cur, prev, first = _band_rows(j, d, nb)
                q2 = _stack_heads(q_ref[cur, :], head_a).astype(bf16)
                do2 = _stack_heads(do_ref[cur, :], head_a).astype(bf16)
                t, lse_b = t_ref[cur, :], lse_ref[cur, :]
                t2 = jnp.concatenate([t[:, :1], t[:, HEAD_DIM:HEAD_DIM + 1]], axis=0)
                lse2 = jnp.concatenate([lse_b[:, :1], lse_b[:, HEAD_DIM:HEAD_DIM + 1]], axis=0)
                if nb == 1:
                    k2, v2, bias = k_ref[cur, :].astype(bf16), v_ref[cur, :].astype(bf16), bias_ref[1][:, :128]
                else:
                    k2 = jnp.concatenate([k_ref[cur, :], k_ref[prev, :]], axis=0).astype(bf16)
                    v2 = jnp.concatenate([v_ref[cur, :], v_ref[prev, :]], axis=0).astype(bf16)
                    bias = bias_ref[first]
                p = jnp.exp(_nt(q2, k2) + bias - lse2)
                ds = (p * (_nt(do2, v2) - t2)).astype(bf16)
                dq_ref[cur, :] += _unstack_heads(jnp.dot(ds, k2, preferred_element_type=f32), head_a)
                dk2, dv2 = _tn(ds, q2), _tn(p.astype(bf16), do2)
                dk_ref[cur, :] += dk2[:128]
                dv_ref[cur, :] += dv2[:128]
                if nb != 1:
                    dk_ref[prev, :] += dk2[128:]
                    dv_ref[prev, :] += dv2[128:]
                return carry

            lax.fori_loop(0, nblk, blk, 0, unroll=16)
        dq_ref[...], dk_ref[...] = _rope_bwd_fn(dq_ref[...], dk_ref[...], cos_ref[...], sin_ref[...])

    pair = pl.BlockSpec((S, 128), lambda h: (0, h))
    return pl.pallas_call(
        body, grid=(N_HEADS_A // 2,),
        in_specs=[pair, _pair_at(S, ATTN_W), _pair_at(S, 2 * ATTN_W), pair, pair, pair, pair, pair],
        out_specs=[pair] * 3, out_shape=[SDS((S, ATTN_W), f32)] * 3,
        scratch_shapes=[pltpu.VMEM((2, 256, 256), f32)] + [pltpu.VMEM((S, 128), f32)] * 3,
        name=name, compiler_params=_params("parallel"))(proj, proj, proj, cos, sgn_sin, dmix_in, out, lse)


def _conv_val(x, w, K, rows):
    acc = x * w[K - 1:K, :]
    for s in range(1, K):
        acc = acc + jnp.where(rows >= s, pltpu.roll(x, s, 0), 0.0) * w[K - 1 - s:K - s, :]
    return acc


def _colconv_fwd(xs, ws, bs, K, fn, nblk, tc, outs, name):
    S = xs[0][0].shape[0]
    n = len(xs)
    has_b = bs is not None

    def body(*refs):
        rows = lax.broadcasted_iota(jnp.int32, (S, tc), 0)
        cs = []
        for k in range(n):
            c = _conv_val(refs[k][...].astype(f32), refs[n + k][...], K, rows)
            if has_b:
                c = c + refs[2 * n + k][...]
            cs.append(c)
        for o_ref, val in zip(refs[(3 if has_b else 2) * n:], fn(*cs)):
            o_ref[...] = val.astype(o_ref.dtype)

    def cspec(rows_, cb0):
        return pl.BlockSpec((rows_, tc), lambda j, cb0=cb0: (0, cb0 + j))

    in_specs = [cspec(S, cb) for _, cb in xs] + [cspec(K, cb) for _, cb in ws]
    args = [a for a, _ in xs] + [a for a, _ in ws]
    if has_b:
        in_specs += [cspec(1, cb) for _, cb in bs]
        args += [a for a, _ in bs]
    return pl.pallas_call(
        body, grid=(nblk,), in_specs=in_specs, out_specs=[cspec(S, 0) for _ in outs],
        out_shape=[SDS((S, nblk * tc), dt) for dt in outs], name=name, compiler_params=_params("parallel"))(*args)


def _colconv_bwd(xs, ws, bs, K, fn, douts, nblk, tc, name, dx_dtype=f32):
    S = xs[0][0].shape[0]
    n, nd = len(xs), len(douts)
    has_b = bs is not None
    nin = (3 if has_b else 2) * n

    def body(*refs):
        rows = lax.broadcasted_iota(jnp.int32, (S, tc), 0)
        x = [refs[k][...].astype(f32) for k in range(n)]
        w = [refs[n + k][...] for k in range(n)]
        cs = []
        for k in range(n):
            c = _conv_val(x[k], w[k], K, rows)
            if has_b:
                c = c + refs[2 * n + k][...]
            cs.append(c)
        _, vjp = jax.vjp(fn, *cs)
        dcs = vjp(tuple(r[...].astype(f32) for r in refs[nin:nin + nd]))
        o = refs[nin + nd:]
        for k in range(n):
            dc = dcs[k]
            dx = dc * w[k][K - 1:K, :]
            o[n + k][K - 1:K, :] = jnp.sum(dc * x[k], axis=0, keepdims=True)
            for s in range(1, K):
                dx = dx + jnp.where(rows < S - s, pltpu.roll(dc, S - s, 0), 0.0) * w[k][K - 1 - s:K - s, :]
                xsh = jnp.where(rows >= s, pltpu.roll(x[k], s, 0), 0.0)
                o[n + k][K - 1 - s:K - s, :] = jnp.sum(dc * xsh, axis=0, keepdims=True)
            o[k][...] = dx.astype(o[k].dtype)
            if has_b:
                o[2 * n + k][...] = jnp.sum(dc, axis=0, keepdims=True)

    def cspec(rows_, cb0):
        return pl.BlockSpec((rows_, tc), lambda j, cb0=cb0: (0, cb0 + j))

    in_specs = [cspec(S, cb) for _, cb in xs] + [cspec(K, cb) for _, cb in ws]
    args = [a for a, _ in xs] + [a for a, _ in ws]
    if has_b:
        in_specs += [cspec(1, cb) for _, cb in bs]
        args += [a for a, _ in bs]
    in_specs += [cspec(S, 0) for _ in douts]
    args += list(douts)
    W = nblk * tc
    out_specs = [cspec(S, 0)] * n + [cspec(K, 0)] * n + ([cspec(1, 0)] * n if has_b else [])
    out_shape = [SDS((S, W), dx_dtype)] * n + [SDS((K, W), f32)] * n + ([SDS((1, W), f32)] * n if has_b else [])
    res = pl.pallas_call(body, grid=(nblk,), in_specs=in_specs, out_specs=out_specs, out_shape=out_shape,
                         name=name, compiler_params=_params("parallel"))(*args)
    return res[:n], res[n:2 * n], res[2 * n:]


def _silu_fn(c):
    return (c * jax.nn.sigmoid(c),)


_GELU_C, _GELU_A = 0.7978845608028654, 0.044715


@jax.custom_vjp
def _geglu(gate, up):
    return 0.5 * gate * (1.0 + jnp.tanh(_GELU_C * (gate + _GELU_A * gate * gate * gate))) * up


def _geglu_vjp_fwd(gate, up):
    return _geglu(gate, up), (gate, up)


def _geglu_vjp_bwd(res, d):
    gate, up = res
    g2 = gate * gate
    t = jnp.tanh(_GELU_C * gate * (1.0 + _GELU_A * g2))
    h = 0.5 * (1.0 + t)
    dgelu = h + (0.5 * _GELU_C) * gate * (1.0 - t * t) * (1.0 + (3.0 * _GELU_A) * g2)
    return d * up * dgelu, d * (gate * h)


_geglu.defvjp(_geglu_vjp_fwd, _geglu_vjp_bwd)


def _geglu_fn(gate, up):
    return (_geglu(gate, up),)


def _softplus(x):
    u = jnp.exp(jnp.minimum(x, 20.0))
    small = u * (1.0 - 0.5 * u)
    return jnp.where(x > 20.0, x, jnp.where(u < 1e-4, small, jnp.log(1.0 + u)))


def _bmm(a, b, precision=None):
    return lax.dot_general(a, b, (((2,), (1,)), ((0,), (0,))), precision=precision, preferred_element_type=f32)


def _bnt(a, b, precision=None):
    return lax.dot_general(a, b, (((2,), (2,)), ((0,), (0,))), precision=precision, preferred_element_type=f32)


def _btn(a, b, precision=None):
    return lax.dot_general(a, b, (((1,), (1,)), ((0,), (0,))), precision=precision, preferred_element_type=f32)


@jax.custom_vjp
def _unit_lower_inverse(A):
    n = A.shape[-1]
    eye = (lax.broadcasted_iota(jnp.int32, (1, n, n), 1) == lax.broadcasted_iota(jnp.int32, (1, n, n), 2)).astype(f32)
    P = -A
    T = eye + P
    for _ in range(5):
        P = _bmm(P, P, HI)
        T = T + _bmm(T, P, HI)
    return T


def _unit_lower_inverse_fwd(A):
    T = _unit_lower_inverse(A)
    return T, T


def _unit_lower_inverse_bwd(T, dT):
    return (-_btn(T, _bnt(dT, T, HI), HI),)


_unit_lower_inverse.defvjp(_unit_lower_inverse_fwd, _unit_lower_inverse_bwd)


def _dn_prep_fn(q, k, v, ba, alog, dtb, h):
    G, C = q.shape[0], CHUNK
    lane = lax.broadcasted_iota(jnp.int32, (1, 1, 128), 2)

    def sel(arr, idx):
        return jnp.sum(jnp.where(lane == idx, arr, 0.0), axis=-1, keepdims=True)

    beta = jax.nn.sigmoid(sel(ba, h))
    g = -jnp.exp(sel(alog[None], h)) * _softplus(sel(ba, N_HEADS_D + h) + sel(dtb[None], h))
    qn = q * lax.rsqrt(jnp.sum(q * q, axis=-1, keepdims=True) + EPS) * (DK ** -0.5)
    kn = k * lax.rsqrt(jnp.sum(k * k, axis=-1, keepdims=True) + EPS)
    ii = lax.broadcasted_iota(jnp.int32, (1, C, C), 1)
    jj = lax.broadcasted_iota(jnp.int32, (1, C, C), 2)
    tril, strict = ii >= jj, ii > jj
    gsq = jnp.broadcast_to(g, (G, C, C))
    gcol = _bmm(jnp.broadcast_to(tril.astype(f32), (G, C, C)), gsq, HI)
    grow = _bmm(jnp.ones((G, C, C), f32), jnp.where(ii <= jj, gsq, 0.0), HI)
    decay = jnp.exp(jnp.where(tril, gcol - grow, NEG))
    gc = gcol[:, :, :1]
    glast = gcol[:, C - 1:C, :1]
    kb = kn * beta
    A = jnp.where(strict, _bnt(kb.astype(bf16), kn.astype(bf16)) * decay, 0.0)
    T = _unit_lower_inverse(A).astype(bf16)
    u = _bmm(T, (v * beta).astype(bf16))
    w = _bmm(T, (kb * jnp.exp(gc)).astype(bf16))
    qk = _bnt(qn.astype(bf16), kn.astype(bf16)) * decay
    qd = qn * jnp.exp(gc)
    kd = kn * jnp.exp(glast - gc)
    return u, w, qk, qd, kd, jnp.broadcast_to(jnp.exp(glast), (G, C, DK))


def _dn_scan_fn(u, w, qk, qd, kd, eg, St):
    b = lambda a: a.astype(bf16)
    vnew = u - _bmm(b(w), b(St))
    o = _bmm(b(qd), b(St)) + _bmm(b(qk), b(vnew))
    return o, St * eg[:, :1, :] + _btn(b(kd), b(vnew))


def _dn_post_fn(o, z, nw):
    return (_rms(o, nw) * (z * jax.nn.sigmoid(z)),)


DN_GROUP = 16


def _dn_prep_specs(S, rows):
    def col(first):
        return pl.BlockSpec((rows, DK), lambda i, h, first=first: (i, first // DK + h))

    par = pl.BlockSpec((1, 128), lambda i, h: (0, 0))
    return [col(0), col(N_HEADS_D * DK), col(2 * N_HEADS_D * DK),
            pl.BlockSpec((rows, 128), lambda i, h: (i, DN_BA_COL // 128)), par, par]


def _dn_prep(qkv, proj, alog, dtb, name):
    S = qkv.shape[0]
    G = DN_GROUP
    rows = G * CHUNK

    def body(q_ref, k_ref, v_ref, ba_ref, al_ref, dt_ref, u_ref, w_ref, qk_ref, qd_ref, kd_ref, eg_ref):
        h = pl.program_id(1)
        r3 = lambda ref: ref[...].reshape(G, CHUNK, 128)
        u, w, qk, qd, kd, eg = _dn_prep_fn(r3(q_ref), r3(k_ref), r3(v_ref), r3(ba_ref), al_ref[...], dt_ref[...], h)
        for ref, val in ((u_ref, u), (w_ref, w), (qd_ref, qd), (kd_ref, kd), (eg_ref, eg)):
            ref[...] = val.reshape(rows, DK)
        qk_ref[:, :CHUNK] = qk.reshape(rows, CHUNK)
        qk_ref[:, CHUNK:] = jnp.zeros((rows, DK - CHUNK), f32)

    out = pl.BlockSpec((rows, DK), lambda i, h: (i, h))
    return pl.pallas_call(
        body, grid=(S // rows, N_HEADS_D), in_specs=_dn_prep_specs(S, rows), out_specs=[out] * 6,
        out_shape=[SDS((S, N_HEADS_D * DK), f32)] * 6, name=name,
        compiler_params=_params("parallel", "parallel"))(qkv, qkv, qkv, proj, alog, dtb)


def _dn_prep_bwd(qkv, proj, alog, dtb, cts, name):
    S = qkv.shape[0]
    G = DN_GROUP
    rows = G * CHUNK

    def body(q_ref, k_ref, v_ref, ba_ref, al_ref, dt_ref, du_ref, dw_ref, dqk_ref, dqd_ref, dkd_ref, deg_ref,
             dq_ref, dk_ref, dv_ref, dba_ref, dal_ref, ddt_ref):
        i, h = pl.program_id(0), pl.program_id(1)
        r3 = lambda ref: ref[...].reshape(G, CHUNK, 128)
        _, vjp = jax.vjp(lambda q, k, v, ba, al, dt: _dn_prep_fn(q, k, v, ba, al, dt, h),
                         r3(q_ref), r3(k_ref), r3(v_ref), r3(ba_ref), al_ref[...], dt_ref[...])
        dqk = dqk_ref[:, :CHUNK].reshape(G, CHUNK, CHUNK)
        dq, dk, dv, dba, dal, ddt = vjp((r3(du_ref), r3(dw_ref), dqk, r3(dqd_ref), r3(dkd_ref), r3(deg_ref)))
        dq_ref[...] = dq.reshape(rows, DK)
        dk_ref[...] = dk.reshape(rows, DK)
        dv_ref[...] = dv.reshape(rows, DK)

        @pl.when(h == 0)
        def _():
            dba_ref[...] = jnp.zeros_like(dba_ref)

        @pl.when(jnp.logical_and(i == 0, h == 0))
        def _():
            dal_ref[...] = jnp.zeros_like(dal_ref)
            ddt_ref[...] = jnp.zeros_like(ddt_ref)

        dba_ref[...] += dba.reshape(rows, 128)
        dal_ref[...] += dal
        ddt_ref[...] += ddt

    hcol = pl.BlockSpec((rows, DK), lambda i, h: (i, h))
    par = pl.BlockSpec((1, 128), lambda i, h: (0, 0))
    W = N_HEADS_D * DK
    return pl.pallas_call(
        body, grid=(S // rows, N_HEADS_D), in_specs=_dn_prep_specs(S, rows) + [hcol] * 6,
        out_specs=[hcol] * 3 + [pl.BlockSpec((rows, 128), lambda i, h: (i, 0)), par, par],
        out_shape=[SDS((S, W), f32)] * 3 + [SDS((S, 128), f32), SDS((1, 128), f32), SDS((1, 128), f32)], name=name,
        compiler_params=_params("arbitrary", "arbitrary"))(qkv, qkv, qkv, proj, alog, dtb, *cts)


def _heads(x):
    return jnp.stack([x[:, DK * h:DK * (h + 1)] for h in range(N_HEADS_D)])


SCAN_CHUNKS = 8


def _dn_scan(pre, name):
    S = pre[0].shape[0]
    NCH = S // CHUNK
    rows = SCAN_CHUNKS * CHUNK

    def body(u_ref, w_ref, qk_ref, qd_ref, kd_ref, eg_ref, o_ref, st_ref, s_ref):
        @pl.when(pl.program_id(0) == 0)
        def _():
            s_ref[...] = jnp.zeros_like(s_ref)

        St = s_ref[...]
        for k in range(SCAN_CHUNKS):
            r = slice(k * CHUNK, (k + 1) * CHUNK)
            st_ref[k] = St
            o, St = _dn_scan_fn(_heads(u_ref[r, :]), _heads(w_ref[r, :]), _heads(qk_ref[r, :])[:, :, :CHUNK],
                                _heads(qd_ref[r, :]), _heads(kd_ref[r, :]), _heads(eg_ref[r, :]), St)
            for h in range(N_HEADS_D):
                o_ref[r, DK * h:DK * (h + 1)] = o[h]
        s_ref[...] = St

    blk = pl.BlockSpec((rows, N_HEADS_D * DK), lambda n: (n, 0))
    return pl.pallas_call(
        body, grid=(S // rows,), in_specs=[blk] * 6,
        out_specs=[blk, pl.BlockSpec((SCAN_CHUNKS, N_HEADS_D, DK, DK), lambda n: (n, 0, 0, 0))],
        out_shape=[SDS((S, N_HEADS_D * DK), f32), SDS((NCH, N_HEADS_D, DK, DK), f32)],
        scratch_shapes=[pltpu.VMEM((N_HEADS_D, DK, DK), f32)], name=name, compiler_params=_params("arbitrary"))(*pre)


def _dn_scan_bwd(pre, states, do, name):
    S = do.shape[0]
    rows = SCAN_CHUNKS * CHUNK
    steps = S // rows

    def body(u_ref, w_ref, qk_ref, qd_ref, kd_ref, eg_ref, st_ref, do_ref,
             du_ref, dw_ref, dqk_ref, dqd_ref, dkd_ref, deg_ref, ds_ref):
        @pl.when(pl.program_id(0) == 0)
        def _():
            ds_ref[...] = jnp.zeros_like(ds_ref)

        dS = ds_ref[...]
        for k in reversed(range(SCAN_CHUNKS)):
            r = slice(k * CHUNK, (k + 1) * CHUNK)
            _, vjp = jax.vjp(_dn_scan_fn, _heads(u_ref[r, :]), _heads(w_ref[r, :]), _heads(qk_ref[r, :])[:, :, :CHUNK],
                             _heads(qd_ref[r, :]), _heads(kd_ref[r, :]), _heads(eg_ref[r, :]), st_ref[k])
            du, dw, dqk, dqd, dkd, deg, dS = vjp((_heads(do_ref[r, :]), dS))
            for h in range(N_HEADS_D):
                c = slice(DK * h, DK * (h + 1))
                for ref, val in ((du_ref, du), (dw_ref, dw), (dqd_ref, dqd), (dkd_ref, dkd), (deg_ref, deg)):
                    ref[r, c] = val[h]
                dqk_ref[r, DK * h:DK * h + CHUNK] = dqk[h]
                dqk_ref[r, DK * h + CHUNK:DK * (h + 1)] = jnp.zeros((CHUNK, DK - CHUNK), f32)
        ds_ref[...] = dS

    blk = pl.BlockSpec((rows, N_HEADS_D * DK), lambda n: (steps - 1 - n, 0))
    return pl.pallas_call(
        body, grid=(steps,),
        in_specs=[blk] * 6 + [pl.BlockSpec((SCAN_CHUNKS, N_HEADS_D, DK, DK), lambda n: (steps - 1 - n, 0, 0, 0)), blk],
        out_specs=[blk] * 6, out_shape=[SDS((S, N_HEADS_D * DK), f32)] * 6,
        scratch_shapes=[pltpu.VMEM((N_HEADS_D, DK, DK), f32)], name=name,
        compiler_params=_params("arbitrary"))(*pre, states, do)


def _loss_head(y, t, name):
    S, D = y.shape
    tm = ROW_TILE

    def body(y_ref, t_ref, dy_ref, l_ref):
        i = pl.program_id(0)
        d = y_ref[...] - t_ref[...]
        dy_ref[...] = d * (1.0 / D)
        part = jnp.sum(jnp.sum(d * d, axis=1, keepdims=True), axis=0, keepdims=True) * (0.5 / D)

        @pl.when(i == 0)
        def _():
            l_ref[...] = jnp.zeros_like(l_ref)

        l_ref[...] += jnp.broadcast_to(part, l_ref.shape)

    spec = pl.BlockSpec((tm, D), lambda i: (i, 0))
    dy, l = pl.pallas_call(body, grid=(S // tm,), in_specs=[spec, spec],
                           out_specs=[spec, pl.BlockSpec((1, 128), lambda i: (0, 0))],
                           out_shape=[SDS((S, D), f32), SDS((1, 128), f32)], name=name,
                           compiler_params=_params("arbitrary"))(y, t)
    return l[0, 0], dy


def _adamw_refs(w_ref, g_ref, m_ref, v_ref, d_ref, mo_ref, vo_ref):
    gv = g_ref[...]
    m2 = ADAM_B1 * m_ref[...] + (1.0 - ADAM_B1) * gv
    v2 = ADAM_B2 * v_ref[...] + (1.0 - ADAM_B2) * (gv * gv)
    m_hat = m2 / (1.0 - ADAM_B1 ** ADAM_STEP)
    v_hat = v2 / (1.0 - ADAM_B2 ** ADAM_STEP)
    d_ref[...] = -ADAM_LR * (m_hat / (jnp.sqrt(v_hat) + ADAM_EPS) + ADAM_WD * w_ref[...])
    mo_ref[...] = m2
    vo_ref[...] = v2


def _adamw_small(ws, gs, ms, vs, name):
    n = len(ws)

    def body(*refs):
        for i in range(n):
            _adamw_refs(*[refs[k * n + i] for k in range(7)])

    res = pl.pallas_call(body, out_shape=[SDS(a.shape, f32) for a in ws] * 3, name=name)(*ws, *gs, *ms, *vs)
    return res[:n], res[n:2 * n], res[2 * n:]


def _adamw(w, g, m, v, tr, name):
    L, R, C = w.shape
    assert R % tr == 0

    def body(*refs):
        _adamw_refs(*refs)

    spec = pl.BlockSpec((1, tr, C), lambda l, i: (l, i, 0))
    return pl.pallas_call(body, grid=(L, R // tr), in_specs=[spec] * 4, out_specs=[spec] * 3,
                          out_shape=[SDS((L, R, C), f32)] * 3, name=name,
                          compiler_params=_params("parallel", "parallel"))(w, g, m, v)


def _rope_tables(S):
    inv = 1.0 / (10000.0 ** (jnp.arange(0, HEAD_DIM, 2, dtype=f32) / HEAD_DIM))
    ang = jnp.arange(S, dtype=f32)[:, None] * inv[None, :]
    cos, sin = jnp.cos(ang), jnp.sin(ang)
    return (jnp.tile(jnp.concatenate([cos, cos], axis=1), (1, N_HEADS_A)),
            jnp.tile(jnp.concatenate([-sin, sin], axis=1), (1, N_HEADS_A)))


def _layer_fwd(x, W, cos, sgn_sin, l, late_weights=None, h1=None, next_pre_mix=None, mid_mixer=None, after_ffn_in=None):
    n = f"l{l}_"
    if h1 is None:
        (h1,) = _rows(_rms_fn, [x], [W["norm_pre_mix"]], [(D_MODEL, bf16)], n + "pre_mix_norm")
    proj = _mm(h1, W["w_in"], "nn", 1024, 768, f32, n + "in_proj")
    attn_out, lse = _attn_fwd(proj, cos, sgn_sin, n + "attn_fwd")
    (qkv,) = _colconv_fwd([(proj, DN_QKV_COL // 512)], [(W["dn_conv_w"], 0)], None, 4, _silu_fn, 3, 512, [f32], n + "dn_conv")
    dn_pre = _dn_prep(qkv, proj, W["dn_a_log"], W["dn_dt_bias"], n + "dn_prep")
    if mid_mixer is not None:
        W = dict(W, dn_norm_w=W["dn_norm_w"] + mid_mixer(dn_pre[0]))
    dn_o, dn_states = _dn_scan(dn_pre, n + "dn_scan")
    (dn_out,) = _rows(_dn_post_fn, [(dn_o, DK, 0), (proj, DK, DN_Z_COL // DK)], [W["dn_norm_w"]], [(DK, bf16)], n + "dn_post",
                      ncol=N_HEADS_D, tm=4 * ROW_TILE)
    mix_in = jnp.concatenate([attn_out, dn_out], axis=1)
    late = late_weights(mix_in) if late_weights is not None else {}
    W = {**W, **late}
    mix = _mm(mix_in, W["w_out"], "nn", 512, 512, f32, n + "out_proj")
    x1, h2 = _rows(_res_rms_pre_fn, [mix, x], [W["norm_post_mix"], W["norm_pre_ffn"]],
                   [(D_MODEL, f32), (D_MODEL, bf16)], n + "post_mix_pre_ffn_norm")
    u0 = _mm(h2, W["ffn_w_in"], "nn", 1024, D_FF // 2, bf16, n + "ffn_in")
    if after_ffn_in is not None:
        W = dict(W, ffn_conv_b=W["ffn_conv_b"] + after_ffn_in(u0))
    nb_ff = D_FF // 256
    (act,) = _colconv_fwd([(u0, 0), (u0, nb_ff)], [(W["ffn_conv_w"], 0), (W["ffn_conv_w"], nb_ff)],
                          [(W["ffn_conv_b"], 0), (W["ffn_conv_b"], nb_ff)], 3, _geglu_fn, nb_ff, 256, [bf16],
                          n + "ffn_conv_glu")
    f = _mm(act, W["ffn_w_out"], "nn", 512, 512, f32, n + "ffn_out")
    if next_pre_mix is None:
        (x2,), h1_next = _rows(_res_rms_fn, [f, x1], [W["norm_post_ffn"]], [(D_MODEL, f32)], n + "post_ffn_norm"), None
    else:
        x2, h1_next = _rows(_res_rms_pre_fn, [f, x1], [W["norm_post_ffn"], next_pre_mix],
                            [(D_MODEL, f32), (D_MODEL, bf16)], n + "post_ffn_next_pre_mix_norm")
    saved = dict(x=x, h1=h1, proj=proj, attn_out=attn_out, lse=lse, qkv=qkv, dn_pre=dn_pre, dn_o=dn_o,
                 dn_states=dn_states, mix_in=mix_in, mix=mix, x1=x1, h2=h2, u0=u0, act=act, f=f, late=late)
    return x2, h1_next, saved


def _layer_bwd(dx2, sv, W, cos, sgn_sin, l, after_ffn=None, next_layer=None, first_layer=True):
    n = f"l{l}_"
    S = dx2.shape[0]
    g = {}
    g_next_pre = None
    if next_layer is None:
        (df,), (g["norm_post_ffn"],) = _rows_vjp(_rms_fn, [sv["f"]], [W["norm_post_ffn"]], [dx2], [0], [0],
                                                 n + "post_ffn_norm_bwd", row_dtype=bf16)
    else:
        (df, dx2), (g["norm_post_ffn"], g_next_pre) = _rows_vjp(
            _res_rms_pre_fn, [sv["f"], sv["x1"]], [W["norm_post_ffn"], next_layer[1]], [dx2, next_layer[0]], [0, 1], [0, 1],
            n + "post_ffn_next_pre_mix_norm_bwd", row_dtype=[bf16, f32])
    dact = _mm(df, W["ffn_w_out"], "nt", 512, D_FF // 2, f32, n + "ffn_out_dx")
    g["ffn_w_out"] = _mm(sv["act"], df, "tn", 256, 1024, f32, n + "ffn_out_dw")
    nb_ff = D_FF // 256
    u0 = sv["u0"]
    dxs, dws, dbs = _colconv_bwd([(u0, 0), (u0, nb_ff)], [(W["ffn_conv_w"], 0), (W["ffn_conv_w"], nb_ff)],
                                 [(W["ffn_conv_b"], 0), (W["ffn_conv_b"], nb_ff)], 3, _geglu_fn, [dact], nb_ff, 256,
                                 n + "ffn_conv_glu_bwd", dx_dtype=bf16)
    du0 = jnp.concatenate(dxs, axis=1)
    g["ffn_conv_w"] = jnp.concatenate(dws, axis=1)
    g["ffn_conv_b"] = jnp.concatenate(dbs, axis=1)
    dh2 = _mm(du0, W["ffn_w_in"], "nt", 512, 512, f32, n + "ffn_in_dx")
    g["ffn_w_in"] = _mm(sv["h2"], du0, "tn", 512, D_FF // 2, f32, n + "ffn_in_dw", column_shards=True)
    if after_ffn is not None:
        W = dict(W, norm_post_mix=W["norm_post_mix"] + after_ffn(g, dh2))
    (dmix, dx1), (g["norm_post_mix"], g["norm_pre_ffn"]) = _rows_vjp(
        _res_rms_pre_fn, [sv["mix"], sv["x"]], [W["norm_post_mix"], W["norm_pre_ffn"]], [dx2, dh2], [0, 1], [0, 1],
        n + "post_mix_pre_ffn_norm_bwd", row_dtype=[bf16, f32])
    dmix_in = _mm(dmix, W["w_out"], "nt", 512, 512, f32, n + "out_proj_dx")
    g["w_out"] = _mm(sv["mix_in"], dmix, "tn", 512, 512, f32, n + "out_proj_dw")

    (ddn_o, dz), (g["dn_norm_w"],) = _rows_vjp(
        _dn_post_fn, [(sv["dn_o"], DK, 0), (sv["proj"], DK, DN_Z_COL // DK)], [W["dn_norm_w"]], [(dmix_in, DK, ATTN_W // DK)],
        [0, 1], [0], n + "dn_post_bwd", ncol=N_HEADS_D, tm=4 * ROW_TILE)
    dpre = _dn_scan_bwd(sv["dn_pre"], sv["dn_states"], ddn_o, n + "dn_scan_bwd")
    dq, dk, dv, dba, g["dn_a_log"], g["dn_dt_bias"] = _dn_prep_bwd(
        sv["qkv"], sv["proj"], W["dn_a_log"], W["dn_dt_bias"], dpre, n + "dn_prep_bwd")
    dqkv = jnp.concatenate([dq, dk, dv], axis=1)
    (dqkv0,), (g["dn_conv_w"],), _ = _colconv_bwd([(sv["proj"], DN_QKV_COL // 512)], [(W["dn_conv_w"], 0)], None, 4, _silu_fn,
                                                 [dqkv], 3, 512, n + "dn_conv_bwd")

    daq, dak, dav = _attn_bwd(sv["proj"], cos, sgn_sin, dmix_in, sv["attn_out"], sv["lse"], n + "attn_bwd")
    dproj = jnp.concatenate([daq, dak, dav, dqkv0, dz, dba, jnp.zeros((S, PROJ_W - DN_BA_COL - 128), f32)], axis=1).astype(bf16)
    dh1 = _mm(dproj, W["w_in"], "nt", 512, 512, f32, n + "in_proj_dx")
    g["w_in"] = _mm(sv["h1"], dproj, "tn", 512, 768, f32, n + "in_proj_dw")
    if not first_layer:
        return (dx1, dh1), g, g_next_pre
    (dx,), (g["norm_pre_mix"],) = _rows_vjp(_rms_fn, [sv["x"]], [W["norm_pre_mix"]], [dh1], [0], [0],
                                            n + "pre_mix_norm_bwd", adds={0: dx1})
    return dx, g, g_next_pre


def _local_step(x, target, layers):
    cos, sgn_sin = _rope_tables(x.shape[0])
    saved, h1 = [], None
    for l, W in enumerate(layers):
        nxt = layers[l + 1]["norm_pre_mix"] if l + 1 < len(layers) else None
        x, h1, sv = _layer_fwd(x, W, cos, sgn_sin, l, h1=h1, next_pre_mix=nxt)
        saved.append(sv)
    loss, dx = _loss_head(x, target, "loss_head")
    grads = [None] * len(layers)
    nxt = None
    for l in reversed(range(len(layers))):
        dx, grads[l], g_pre = _layer_bwd(dx, saved[l], layers[l], cos, sgn_sin, l, next_layer=nxt, first_layer=(l == 0))
        if g_pre is not None:
            grads[l + 1]["norm_pre_mix"] = g_pre
        if l > 0:
            dx, dh1 = dx
            nxt = (dh1, layers[l]["norm_pre_mix"])
    return loss, dx, grads


def _pos():
    x, y, c = lax.axis_index("x"), lax.axis_index("y"), lax.axis_index("c")
    return x, y, c, [(1 - x, y), (x, 1 - y), (1 - x, 1 - y)]


def _rcopy(src, dst, send_sem, recv_sem, dev):
    return pltpu.make_async_remote_copy(src_ref=src, dst_ref=dst, send_sem=send_sem, recv_sem=recv_sem,
                                        device_id=dev, device_id_type=MESH)


def _half_rows(ref, h, which, axis):
    if h is None:
        return ref
    rows = pl.ds(pl.multiple_of(which * h, 16), h)
    return ref.at[:, rows, :] if axis == 1 else ref.at[rows, :]


def _dma_sems(*counts):
    return [pltpu.SemaphoreType.DMA((k,)) for k in counts]


def _all_gather(arrs, halves, name):
    n = len(arrs)

    def body(*refs):
        ins, outs = refs[:n], refs[n:2 * n]
        send1, recv1, send2, recv2 = refs[2 * n:]
        x, y, c, chips = _pos()
        me, sib, s_me = (x, y, c), (x, y, 1 - c), 2 * x + y
        sends = []
        for i in range(n):
            for j, chip in enumerate(chips):
                cp = _rcopy(_half_rows(ins[i], halves[i], c, 1), _half_rows(outs[i].at[s_me], halves[i], c, 1),
                            send1.at[3 * i + j], recv1.at[3 * i + j], (*chip, c))
                cp.start()
                sends.append(cp)
        for i in range(n):
            for j, (px, py) in enumerate(chips):
                k = 3 * i + j
                landed = _half_rows(outs[i].at[2 * px + py], halves[i], c, 1)
                _rcopy(landed, landed, send1.at[k], recv1.at[k], me).wait_recv()
                if halves[i] is not None:
                    cp = _rcopy(landed, landed, send2.at[k], recv2.at[k], sib)
                    cp.start()
                    sends.append(cp)
        for i in range(n):
            if halves[i] is None:
                continue
            for j, (px, py) in enumerate(chips):
                k = 3 * i + j
                other = _half_rows(outs[i].at[2 * px + py], halves[i], 1 - c, 1)
                _rcopy(other, other, send2.at[k], recv2.at[k], me).wait_recv()
        for cp in sends:
            cp.wait_send()

    return pl.pallas_call(
        body, in_specs=[ANY] * n, out_specs=[ANY] * n,
        out_shape=[SDS((4,) + a.shape, a.dtype) for a in arrs],
        scratch_shapes=_dma_sems(3 * n, 3 * n, 3 * n, 3 * n), name=name)(*arrs)


HBM = pl.BlockSpec(memory_space=pltpu.HBM)
SEM = pl.BlockSpec(memory_space=pltpu.SEMAPHORE)
_EFFECT = pltpu.SideEffectType.DATAFLOW_SIDE_EFFECTING


def _in_hbm(a):
    return pltpu.with_memory_space_constraint(a, pltpu.HBM)


def _split_copy(srcs, land_shapes, plan, per, after, name):
    n, nl = len(srcs), len(land_shapes)
    k = per * n

    def body(*refs):
        ins, lands, token = refs[:n], refs[n:n + nl], refs[-1]
        send, recv = refs[n + nl + 1], refs[n + nl + 2]
        for i, (src, dst, dev, _) in enumerate(plan(ins, lands)):
            _rcopy(src, dst, send.at[i], recv.at[i], dev).start()
        token[...] = jnp.zeros_like(token)

    lands = [_in_hbm(lax.empty(s.shape, s.dtype)) for s in land_shapes]
    return pl.pallas_call(
        body, name=name,
        out_shape=(pltpu.SemaphoreType.DMA((k,)), pltpu.SemaphoreType.DMA((k,)),
                   *[pltpu.HBM(a.shape, a.dtype) for a in srcs], *[pltpu.HBM(s.shape, s.dtype) for s in land_shapes],
                   SDS((8, 128), f32)),
        in_specs=[HBM] * (n + nl) + [ANY], out_specs=(SEM, SEM, *[HBM] * (n + nl), pl.BlockSpec(memory_space=pltpu.VMEM)),
        input_output_aliases={i: 2 + i for i in range(n + nl)},
        compiler_params=pltpu.CompilerParams(has_side_effects=_EFFECT))(*[_in_hbm(a) for a in srcs], *lands, after)


def _split_wait(started, n, plan, after, name, nl=None):
    nl = n if nl is None else nl
    send, recv = started[0], started[1]
    thru = started[2:2 + n + nl]

    def body(*refs):
        ins, lands = refs[:n], refs[n:n + nl]
        send_ref, recv_ref = refs[n + nl], refs[n + nl + 1]
        for i, (src, _, dev, mine) in enumerate(plan(ins, lands)):
            cp = _rcopy(src, mine, send_ref.at[i], recv_ref.at[i], dev)
            cp.wait_send()
            cp.wait_recv()

    res = pl.pallas_call(
        body, name=name, out_shape=tuple(pltpu.HBM(a.shape, a.dtype) for a in thru),
        in_specs=[HBM] * (n + nl) + [SEM, SEM, ANY], out_specs=tuple([HBM] * (n + nl)),
        input_output_aliases={i: i for i in range(n + nl)},
        compiler_params=pltpu.CompilerParams(has_side_effects=_EFFECT))(*thru, send, recv, after)
    return res[:n], res[n:]


def _gather_plan(halves):
    def plan(ins, lands):
        x, y, c, chips = _pos()
        out = []
        for i in range(len(ins)):
            for px, py in chips:
                out.append((_half_rows(ins[i], halves[i], c, 1), _half_rows(lands[i].at[2 * x + y], halves[i], c, 1),
                            (px, py, c), _half_rows(lands[i].at[2 * px + py], halves[i], c, 1)))
        return out
    return plan


def _scatter_plan(ins, lands):
    x, y, c, chips = _pos()
    out = []
    for i in range(len(ins)):
        for j, (px, py) in enumerate(chips):
            out.append((ins[i].at[2 * px + py], lands[i].at[j], (px, py, c), lands[i].at[j]))
    return out


def _exchange_plan(ins, lands):
    x, y, c, _ = _pos()
    return [(_half_rows(g, g.shape[1] // 2, 1 - c, 1), land, (x, y, 1 - c), land) for g, land in zip(ins, lands)]


def _pass_plan(halves):
    def plan(ins, lands):
        x, y, c, chips = _pos()
        out = []
        for i in range(len(ins)):
            for px, py in chips:
                slot = ins[i].at[2 * px + py]
                out.append((_half_rows(slot, halves[i], c, 1), _half_rows(slot, halves[i], c, 1), (x, y, 1 - c),
                            _half_rows(slot, halves[i], 1 - c, 1)))
        return out
    return plan


def _exchange_halves(gs, name):
    n = len(gs)

    def body(*refs):
        ins, outs = refs[:n], refs[n:2 * n]
        send, recv = refs[2 * n:]
        x, y, c, _ = _pos()
        sends = []
        for k in range(n):
            cp = _rcopy(_half_rows(ins[k], gs[k].shape[1] // 2, 1 - c, 1), outs[k], send.at[k], recv.at[k], (x, y, 1 - c))
            cp.start()
            sends.append(cp)
        for k in range(n):
            _rcopy(outs[k], outs[k], send.at[k], recv.at[k], (x, y, c)).wait_recv()
        for cp in sends:
            cp.wait_send()

    return pl.pallas_call(
        body, in_specs=[ANY] * n, out_specs=[ANY] * n,
        out_shape=[SDS((4, g.shape[1] // 2, g.shape[2]), g.dtype) for g in gs],
        scratch_shapes=_dma_sems(n, n), name=name)(*gs)


def _scatter_partials(ps, name):
    n = len(ps)

    def body(*refs):
        ins, outs = refs[:n], refs[n:2 * n]
        send, recv = refs[2 * n:]
        x, y, c, chips = _pos()
        sends = []
        for k in range(n):
            for j, (px, py) in enumerate(chips):
                cp = _rcopy(ins[k].at[2 * px + py], outs[k].at[j], send.at[3 * k + j], recv.at[3 * k + j], (px, py, c))
                cp.start()
                sends.append(cp)
        for k in range(n):
            for j in range(3):
                _rcopy(outs[k].at[j], outs[k].at[j], send.at[3 * k + j], recv.at[3 * k + j], (x, y, c)).wait_recv()
        for cp in sends:
            cp.wait_send()

    return pl.pallas_call(
        body, in_specs=[ANY] * n, out_specs=[ANY] * n,
        out_shape=[SDS((3,) + p.shape[1:], p.dtype) for p in ps],
        scratch_shapes=_dma_sems(3 * n, 3 * n), name=name)(*ps)


def _join_halves(rs, layers, name):
    n = len(rs)

    def body(*refs):
        outs = refs[n:2 * n]
        send, recv = refs[2 * n:]
        x, y, c, _ = _pos()

        def half(k, which):
            h = rs[k].shape[1] // 2
            return _half_rows(outs[k], h, which, 1) if layers[k] is None else _half_rows(outs[k].at[layers[k]], h, which, 0)

        sends = []
        for k in range(n):
            cp = _rcopy(half(k, c), half(k, c), send.at[k], recv.at[k], (x, y, 1 - c))
            cp.start()
            sends.append(cp)
        for k in range(n):
            _rcopy(half(k, 1 - c), half(k, 1 - c), send.at[k], recv.at[k], (x, y, c)).wait_recv()
        for cp in sends:
            cp.wait_send()

    return pl.pallas_call(
        body, in_specs=[ANY] * n, out_specs=[ANY] * n, out_shape=[SDS(r.shape, r.dtype) for r in rs],
        input_output_aliases={k: k for k in range(n)}, scratch_shapes=_dma_sems(n, n), name=name)(*rs)


def _all_reduce_small(pack, name):
    R = pack.shape[0]

    def body(in_ref, out_ref, buf, send, recv):
        x, y, c, _ = _pos()
        me = 4 * x + 2 * y + c
        buf[me] = in_ref[...]
        sends = []
        for k in range(1, 8):
            peer = me ^ k
            cp = _rcopy(buf.at[me], buf.at[me], send.at[k - 1], recv.at[k - 1], ((peer >> 2) & 1, (peer >> 1) & 1, peer & 1))
            cp.start()
            sends.append(cp)
        for k in range(1, 8):
            _rcopy(buf.at[me ^ k], buf.at[me ^ k], send.at[k - 1], recv.at[k - 1], (x, y, c)).wait_recv()
        for cp in sends:
            cp.wait_send()
        acc = buf[0]
        for d in range(1, 8):
            acc = acc + buf[d]
        out_ref[...] = acc

    return pl.pallas_call(
        body, out_shape=SDS((R, 128), f32),
        in_specs=[pl.BlockSpec(memory_space=pltpu.VMEM)], out_specs=pl.BlockSpec(memory_space=pltpu.VMEM),
        scratch_shapes=[pltpu.VMEM((8, R, 128), f32)] + _dma_sems(7, 7), name=name)(pack)


def _add_sibling(g, recv, c_arr, tr, name):
    _, R, C = g.shape
    h = R // 2
    nrb = h // tr
    assert h % tr == 0

    def body(c_ref, g_ref, r_ref, o_ref):
        o_ref[...] = (g_ref[...] + r_ref[...]).astype(o_ref.dtype)

    spec = pl.BlockSpec((1, tr, C), lambda s, r, c_ref: (s, r, 0))
    grid_spec = pltpu.PrefetchScalarGridSpec(
        num_scalar_prefetch=1, grid=(4, nrb),
        in_specs=[pl.BlockSpec((1, tr, C), lambda s, r, c_ref: (s, c_ref[0] * nrb + r, 0)), spec], out_specs=spec)
    return pl.pallas_call(body, grid_spec=grid_spec, out_shape=SDS((4, h, C), bf16), name=name,
                          compiler_params=_params("parallel", "parallel"))(c_arr, g, recv)


def _add_chips(p, recv, sc_arr, tr, layer, into, name):
    _, h, C = p.shape
    nrb = h // tr
    assert h % tr == 0

    def body(sc_ref, p_ref, r_ref, *rest):
        rest[-1][...] = (p_ref[0].astype(f32) + r_ref[0].astype(f32)) + (r_ref[1].astype(f32) + r_ref[2].astype(f32))

    grid_spec = pltpu.PrefetchScalarGridSpec(
        num_scalar_prefetch=1, grid=(nrb,),
        in_specs=[pl.BlockSpec((1, tr, C), lambda r, sc_ref: (sc_ref[0], r, 0)),
                  pl.BlockSpec((3, tr, C), lambda r, sc_ref: (0, r, 0))] + ([] if into is None else [ANY]),
        out_specs=pl.BlockSpec((None, tr, C), lambda r, sc_ref: (layer, sc_ref[1] * nrb + r, 0)))
    return pl.pallas_call(body, grid_spec=grid_spec, out_shape=SDS((2, 2 * h, C), f32), name=name,
                          input_output_aliases={} if into is None else {3: 0},
                          compiler_params=_params("parallel"))(sc_arr, p, recv, *([] if into is None else [into]))


_BIG = (("w_in", 1024, 256), ("w_out", 256, 128), ("ffn_w_in", 1024, 256), ("ffn_w_out", 704, 352))
_SMALL = ("dn_conv_w", "ffn_conv_w", "ffn_conv_b", "norm_pre_mix", "norm_post_mix", "norm_pre_ffn", "norm_post_ffn",
          "dn_norm_w", "dn_a_log", "dn_dt_bias")
_WEIGHTS = ("w_in", "dn_conv_w", "dn_a_log", "dn_dt_bias", "dn_norm_w", "w_out", "ffn_w_in", "ffn_conv_w", "ffn_conv_b",
            "ffn_w_out", "norm_pre_mix", "norm_post_mix", "norm_pre_ffn", "norm_post_ffn")
_ADAM_ROWS = {"w_in": 256, "w_out": 256, "ffn_w_in": 128, "ffn_w_out": 176}


def _shard_major(name, g):
    if name == "w_in":
        width = IN_COLS // N_SHARDS
        return jnp.stack([g[:, width * s:width * (s + 1)] for s in range(N_SHARDS)])
    if name == "ffn_w_in":
        return g
    return g.reshape(4, g.shape[0] // 4, g.shape[1])


def kernel(x, w_in, dn_conv_w, dn_a_log, dn_dt_bias, dn_norm_w, w_out, ffn_w_in, ffn_conv_w, ffn_conv_b, ffn_w_out, norm_pre_mix, norm_post_mix, norm_pre_ffn, norm_post_ffn, loss_target, m_w_in, m_dn_conv_w, m_dn_a_log, m_dn_dt_bias, m_dn_norm_w, m_w_out, m_ffn_w_in, m_ffn_conv_w, m_ffn_conv_b, m_ffn_w_out, m_norm_pre_mix, m_norm_post_mix, m_norm_pre_ffn, m_norm_post_ffn, v_w_in, v_dn_conv_w, v_dn_a_log, v_dn_dt_bias, v_dn_norm_w, v_w_out, v_ffn_w_in, v_ffn_conv_w, v_ffn_conv_b, v_ffn_w_out, v_norm_pre_mix, v_norm_post_mix, v_norm_pre_ffn, v_norm_post_ffn):
    w = dict(w_in=w_in, dn_conv_w=dn_conv_w, dn_a_log=dn_a_log, dn_dt_bias=dn_dt_bias, dn_norm_w=dn_norm_w, w_out=w_out,
             ffn_w_in=ffn_w_in, ffn_conv_w=ffn_conv_w, ffn_conv_b=ffn_conv_b, ffn_w_out=ffn_w_out, norm_pre_mix=norm_pre_mix,
             norm_post_mix=norm_post_mix, norm_pre_ffn=norm_pre_ffn, norm_post_ffn=norm_post_ffn)
    m = dict(w_in=m_w_in, dn_conv_w=m_dn_conv_w, dn_a_log=m_dn_a_log, dn_dt_bias=m_dn_dt_bias, dn_norm_w=m_dn_norm_w,
             w_out=m_w_out, ffn_w_in=m_ffn_w_in, ffn_conv_w=m_ffn_conv_w, ffn_conv_b=m_ffn_conv_b, ffn_w_out=m_ffn_w_out,
             norm_pre_mix=m_norm_pre_mix, norm_post_mix=m_norm_post_mix, norm_pre_ffn=m_norm_pre_ffn,
             norm_post_ffn=m_norm_post_ffn)
    v = dict(w_in=v_w_in, dn_conv_w=v_dn_conv_w, dn_a_log=v_dn_a_log, dn_dt_bias=v_dn_dt_bias, dn_norm_w=v_dn_norm_w,
             w_out=v_w_out, ffn_w_in=v_ffn_w_in, ffn_conv_w=v_ffn_conv_w, ffn_conv_b=v_ffn_conv_b, ffn_w_out=v_ffn_w_out,
             norm_pre_mix=v_norm_pre_mix, norm_post_mix=v_norm_post_mix, norm_pre_ffn=v_norm_pre_ffn,
             norm_post_ffn=v_norm_post_ffn)
    xi, yi, ci = lax.axis_index("x"), lax.axis_index("y"), lax.axis_index("c")
    s_me = 2 * xi + yi
    c_arr = jnp.reshape(ci, (1,)).astype(jnp.int32)
    sc_arr = jnp.stack([s_me, ci]).astype(jnp.int32)

    mats = [name for name, _, _ in _BIG]
    rest = mats[1:]
    half_of = {name: rows // 2 for name, rows, _ in _BIG}
    tiles = {name: tr for name, _, tr in _BIG}
    gathered_shape = lambda a: SDS((4,) + a.shape, a.dtype)

    own = {k: w[k].astype(bf16) for k in mats}
    plan_in = _gather_plan([half_of["w_in"], None, None])
    src_in = [own["w_in"][0:1], dn_conv_w, ffn_conv_w]
    started_in = _split_copy(src_in, [gathered_shape(a) for a in src_in], plan_in, 3, src_in[0], "weights_gather_w_in0_start")
    plan0 = _gather_plan([half_of[k] for k in rest])
    src0 = [own[k][0:1] for k in rest]
    started0 = _split_copy(src0, [gathered_shape(a) for a in src0], plan0, 3, started_in[-1], "weights_gather_l0_start")
    plan1 = _gather_plan([half_of[k] for k in mats])
    src1 = [own[k][1:2] for k in mats]
    started1 = _split_copy(src1, [gathered_shape(a) for a in src1], plan1, 3, started0[-1], "weights_gather_l1_start")
    _, landed_in = _split_wait(started_in, len(src_in), plan_in, started1[-1], "weights_gather_w_in0_wait")
    pass_in = _pass_plan([half_of["w_in"]])
    passed_in = _split_copy(landed_in[:1], [], pass_in, 3, landed_in[0], "weights_pass_w_in0_start")
    got_in = list(_split_wait(passed_in, 1, pass_in, passed_in[-1], "weights_pass_w_in0_wait", nl=0)[0]) + list(landed_in[1:])

    def pick(mine, gathered):
        return [jnp.where(s_me == s, mine, gathered[s]) for s in range(4)]

    conv = {"dn_conv_w": jnp.concatenate(pick(dn_conv_w, got_in[1]), axis=-1),
            "ffn_conv_w": jnp.concatenate(pick(ffn_conv_w, got_in[2]), axis=-1)}
    lanes = lambda a: jnp.pad(a, ((0, 0), (0, 128 - a.shape[1])))
    vec = dict(dn_a_log=lanes(dn_a_log), dn_dt_bias=lanes(dn_dt_bias), dn_norm_w=dn_norm_w, ffn_conv_b=ffn_conv_b,
               norm_pre_mix=norm_pre_mix, norm_post_mix=norm_post_mix, norm_pre_ffn=norm_pre_ffn, norm_post_ffn=norm_post_ffn)

    def matrices(l, names, gathered):
        W = {}
        for k, a in zip(names, gathered):
            if k in ("w_out", "ffn_w_out"):
                rows_, cols = own[k].shape[1:]
                W[k] = lax.dynamic_update_slice(a[:, 0], own[k][l][None], (s_me, 0, 0)).reshape(4 * rows_, cols)
            else:
                cat = jnp.concatenate(pick(own[k][l], a[:, 0]), axis=-1)
                W[k] = jnp.pad(cat, ((0, 0), (0, PROJ_W - IN_COLS))) if k == "w_in" else cat
        return W

    def small_weights(l):
        return {**{k: a[l] for k, a in conv.items()}, **{k: a[l:l + 1] for k, a in vec.items()}}

    pass0, pass1 = _pass_plan([half_of[k] for k in rest]), _pass_plan([half_of[k] for k in mats])
    passing = {}

    def mid_mixer_l0(marker):
        _, landed = _split_wait(started0, len(rest), plan0, marker, "weights_gather_l0_wait")
        passing["l0"] = _split_copy(landed, [], pass0, 3, marker, "weights_pass_l0_start")
        return passing["l0"][-1][0, 0]

    def late_l0(mix_in):
        return matrices(0, rest, _split_wait(passing["l0"], len(rest), pass0, mix_in, "weights_pass_l0_wait", nl=0)[0])

    def after_ffn_in_l0(marker):
        _, landed = _split_wait(started1, len(mats), plan1, marker, "weights_gather_l1_wait")
        passing["l1"] = _split_copy(landed, [], pass1, 3, marker, "weights_pass_l1_start")
        return passing["l1"][-1][0, 0]

    cos, sgn_sin = _rope_tables(x.shape[1])
    W0 = {**small_weights(0), **matrices(0, ["w_in"], got_in[:1])}
    W0_first = dict(W0, norm_pre_mix=W0["norm_pre_mix"] + started1[-1][0, 0])
    x1, h1_l1, saved0 = _layer_fwd(x[0], W0_first, cos, sgn_sin, 0, late_weights=late_l0, next_pre_mix=norm_pre_mix[1:2],
                                   mid_mixer=mid_mixer_l0, after_ffn_in=after_ffn_in_l0)
    W1 = {**small_weights(1),
          **matrices(1, mats, _split_wait(passing["l1"], len(mats), pass1, x1, "weights_pass_l1_wait", nl=0)[0])}
    x2, _, saved1 = _layer_fwd(x1, W1, cos, sgn_sin, 1, h1=h1_l1)
    loss_local, dy = _loss_head(x2, loss_target[0], "loss_head")
    loss = lax.psum(loss_local, ("x", "y", "c"))

    def shard_major(names, grads_l):
        return [_shard_major(name, grads_l[name]) for name in names]

    def add_siblings(l, names, gs, from_sib):
        return [_add_sibling(g, r, c_arr, tiles[name], f"add_sibling_{name}{l}") for g, r, name in zip(gs, from_sib, names)]

    def scatter_start(l, names, parts, after, tag):
        return _split_copy(parts, [SDS((3,) + p.shape[1:], p.dtype) for p in parts], _scatter_plan, 3, after,
                           f"grads_l{l}{tag}_scatter_start")

    def owner_sums(l, names, sent, after, tag, into):
        parts, recvd = _split_wait(sent, len(names), _scatter_plan, after, f"grads_l{l}{tag}_scatter_wait")
        return {name: _add_chips(p, r, sc_arr, tiles[name], l, into.get(name), f"add_chips_{name}{l}")
                for p, r, name in zip(parts, recvd, names)}

    (dx1, dh1_l1), grads1, _ = _layer_bwd(dy, saved1, W1, cos, sgn_sin, 1, first_layer=False)
    gs1 = shard_major(mats, grads1)
    swap1 = _split_copy(gs1, [SDS((4, g.shape[1] // 2, g.shape[2]), g.dtype) for g in gs1], _exchange_plan, 1, dx1,
                        "grads_l1_sibling_start")
    ffn = ["ffn_w_in", "ffn_w_out"]
    launched = {}

    def after_ffn_l0(g_ffn, dx_mid):
        gs1_, from_sib1 = _split_wait(swap1, len(mats), _exchange_plan, dx_mid, "grads_l1_sibling_wait")
        launched["l1"] = scatter_start(1, mats, add_siblings(1, mats, gs1_, from_sib1), dx_mid, "")
        gs0 = shard_major(ffn, g_ffn)
        from_sib0 = _exchange_halves(gs0, "grads_l0_ffn_to_sibling")
        launched["l0_ffn"] = scatter_start(0, ffn, add_siblings(0, ffn, gs0, from_sib0), launched["l1"][-1], "_ffn")
        return launched["l0_ffn"][-1][0, 0]

    W0_last = dict(W0, **saved0["late"], norm_post_ffn=W0["norm_post_ffn"] + swap1[-1][0, 0])
    dx, grads0, grads1["norm_pre_mix"] = _layer_bwd(dx1, saved0, W0_last, cos, sgn_sin, 0, after_ffn=after_ffn_l0,
                                                    next_layer=(dh1_l1, norm_pre_mix[1:2]))
    mix = ["w_in", "w_out"]
    gs0 = shard_major(mix, grads0)
    part0 = add_siblings(0, mix, gs0, _exchange_halves(gs0, "grads_l0_mix_to_sibling"))
    sent0 = scatter_start(0, mix, part0, dx, "_mix")
    red = owner_sums(0, ffn, launched["l0_ffn"], sent0[-1], "_ffn", {})
    red = owner_sums(1, mats, launched["l1"], sent0[-1], "", red)
    joined = dict(zip(mats, _join_halves([red[k] for k in mats], [1 if k in mix else None for k in mats],
                                         "grads_join_early")))
    grads = [grads0, grads1]

    small = {}
    for name in _SMALL:
        per_layer = [grads[l][name] for l in range(2)]
        if name in ("dn_a_log", "dn_dt_bias"):
            per_layer = [p[:, :N_HEADS_D] for p in per_layer]
        small[name] = jnp.stack(per_layer).reshape((2,) + (w[name].shape[1:] if name not in ("dn_conv_w", "ffn_conv_w")
                                                           else per_layer[0].shape))
    flat = jnp.concatenate([small[name].reshape(-1) for name in _SMALL])
    n_rows = -(-flat.shape[0] // 1024) * 8
    summed = _all_reduce_small(jnp.pad(flat, (0, n_rows * 128 - flat.shape[0])).reshape(n_rows, 128),
                               "small_grads_all_reduce").reshape(-1)
    off = 0
    g_out = {}
    for name in _SMALL:
        size = small[name].size
        g_out[name] = summed[off:off + size].reshape(small[name].shape)
        off += size
    for k in ("dn_conv_w", "ffn_conv_w"):
        width = w[k].shape[2]
        g_out[k] = lax.dynamic_slice_in_dim(g_out[k], s_me * width, width, axis=2)
    for k in ffn:
        g_out[k] = joined[k]

    deltas, new_m, new_v = {}, {}, {}

    def step(name):
        shape = w[name].shape
        as3 = (lambda a: a) if len(shape) == 3 else (lambda a: a.reshape(shape[0], 1, shape[1]))
        tr = _ADAM_ROWS.get(name, as3(w[name]).shape[1])
        d_, m_, v_ = _adamw(as3(w[name]), as3(g_out[name]), as3(m[name]), as3(v[name]), tr, f"adamw_{name}")
        deltas[name], new_m[name], new_v[name] = d_.reshape(shape), m_.reshape(shape), v_.reshape(shape)

    for name in ffn:
        step(name)
    tiny = [name for name in _WEIGHTS if name not in mats]
    stepped = _adamw_small(*[[d[name] for name in tiny] for d in (w, g_out, m, v)], "adamw_small")
    for out, vals in zip((deltas, new_m, new_v), stepped):
        out.update(zip(tiny, vals))
    done = jnp.reshape(deltas["ffn_w_in"][0, 0, 0] + deltas["ffn_w_out"][0, 0, 0] + deltas["norm_post_ffn"][0, 0], (1,))
    red = owner_sums(0, mix, sent0, done, "_mix", joined)
    for k, a in zip(mix, _join_halves([red[k] for k in mix], [0] * len(mix), "grads_join_late")):
        g_out[k] = a
        step(k)

    return (loss, dx[None], *[g_out[k] for k in _WEIGHTS], *[deltas[k] for k in _WEIGHTS],
            *[new_m[k] for k in _WEIGHTS], *[new_v[k] for k in _WEIGHTS])
```

```python
import jax
import jax.numpy as jnp
from jax import lax
from jax.experimental import pallas as pl
from jax.experimental.pallas import tpu as pltpu

f32, bf16 = jnp.float32, jnp.bfloat16
SDS = jax.ShapeDtypeStruct
HI = lax.Precision.HIGH
MESH = pl.DeviceIdType.MESH
ANY = pl.BlockSpec(memory_space=pl.ANY)

D_MODEL = 1024
N_HEADS_A, HEAD_DIM = 8, 64
ATTN_W = 512
N_HEADS_D, DK = 4, 128
CHUNK = 64
D_FF = 2816
IN_COLS = 3592
PROJ_W = 3840
DN_QKV_COL = 3 * ATTN_W
DN_Z_COL = DN_QKV_COL + 3 * N_HEADS_D * DK
DN_BA_COL = DN_Z_COL + N_HEADS_D * DK
N_SHARDS = 4
BRANCHES = ((1, 16), (4, 4), (16, 1))
EPS = 1e-6
NEG = -1e30
ROW_TILE = 512
VMEM_LIMIT = 56 * 1024 * 1024

ADAM_LR, ADAM_B1, ADAM_B2, ADAM_EPS, ADAM_WD, ADAM_STEP = 0.001, 0.9, 0.999, 1e-08, 0.01, 10


def _params(*sem):
    return pltpu.CompilerParams(dimension_semantics=sem, vmem_limit_bytes=VMEM_LIMIT)


def _mm(a, b, mode, tm, tn, out_dtype, name, column_shards=False):
    if mode == "nn":
        (M, K), N = a.shape, b.shape[1]
        dims = (((1,), (0,)), ((), ()))
        a_spec = pl.BlockSpec((tm, K), lambda i, j: (i, 0))
        b_spec = pl.BlockSpec((K, tn), lambda i, j: (0, j))
    elif mode == "nt":
        (M, K), N = a.shape, b.shape[0]
        dims = (((1,), (1,)), ((), ()))
        a_spec = pl.BlockSpec((tm, K), lambda i, j: (i, 0))
        b_spec = pl.BlockSpec((tn, K), lambda i, j: (j, 0))
    else:
        (K, M), N = a.shape, b.shape[1]
        dims = (((0,), (0,)), ((), ()))
        a_spec = pl.BlockSpec((K, tm), lambda i, j: (0, i))
        b_spec = pl.BlockSpec((K, tn), lambda i, j: (0, j))
    assert M % tm == 0 and N % tn == 0, (name, M, N, tm, tn)

    def body(a_ref, b_ref, o_ref):
        o_ref[...] = lax.dot_general(a_ref[...].astype(bf16), b_ref[...].astype(bf16), dims,
                                     preferred_element_type=f32).astype(o_ref.dtype)

    if column_shards:
        out_spec, out_shape = pl.BlockSpec((None, tm, tn), lambda i, j: (j, i, 0)), SDS((N // tn, M, tn), out_dtype)
    else:
        out_spec, out_shape = pl.BlockSpec((tm, tn), lambda i, j: (i, j)), SDS((M, N), out_dtype)
    return pl.pallas_call(body, grid=(M // tm, N // tn), in_specs=[a_spec, b_spec], out_specs=out_spec,
                          out_shape=out_shape, name=name, compiler_params=_params("parallel", "arbitrary"))(a, b)


def _row_spec(r, tm):
    if isinstance(r, tuple):
        arr, width, cb = r
        return arr, pl.BlockSpec((tm, width), lambda i, j, cb=cb: (i, cb + j))
    return r, pl.BlockSpec((tm, r.shape[1]), lambda i, j: (i, j))


def _full_spec(p):
    return pl.BlockSpec(p.shape, lambda i, j: (0,) * p.ndim)


def _rows(fn, rows, params, outs, name, tm=ROW_TILE, ncol=1):
    arrs, specs = zip(*[_row_spec(r, tm) for r in rows])
    S = arrs[0].shape[0]
    nr, npar = len(rows), len(params)

    def body(*refs):
        vals = fn(*[r[...].astype(f32) for r in refs[:nr]], *[p[...] for p in refs[nr:nr + npar]])
        for o_ref, v in zip(refs[nr + npar:], vals):
            o_ref[...] = v.astype(o_ref.dtype)

    return pl.pallas_call(
        body, grid=(S // tm, ncol), in_specs=list(specs) + [_full_spec(p) for p in params],
        out_specs=[pl.BlockSpec((tm, w), lambda i, j: (i, j)) for w, _ in outs],
        out_shape=[SDS((S, w * ncol), dt) for w, dt in outs], name=name,
        compiler_params=_params("parallel", "parallel"))(*arrs, *params)


def _rows_vjp(fn, rows, params, cts, wrt_rows, wrt_params, name, adds=None, tm=ROW_TILE, ncol=1, row_dtype=f32):
    adds = adds or {}
    arrs, specs = zip(*[_row_spec(r, tm) for r in rows])
    carrs, cspecs = zip(*[_row_spec(c, tm) for c in cts])
    add_keys = sorted(adds)
    aarrs = [adds[k] for k in add_keys]
    S = arrs[0].shape[0]
    nr, npar, nc, na = len(rows), len(params), len(cts), len(aarrs)
    widths = [specs[k].block_shape[1] for k in wrt_rows]
    row_dtypes = row_dtype if isinstance(row_dtype, (list, tuple)) else [row_dtype] * len(wrt_rows)

    def body(*refs):
        first = jnp.logical_and(pl.program_id(0) == 0, pl.program_id(1) == 0)
        rv = [r[...].astype(f32) for r in refs[:nr]]
        pv = [p[...] for p in refs[nr:nr + npar]]
        cv = tuple(c[...].astype(f32) for c in refs[nr + npar:nr + npar + nc])
        av = dict(zip(add_keys, refs[nr + npar + nc:nr + npar + nc + na]))
        o = refs[nr + npar + nc + na:]
        _, vjp = jax.vjp(fn, *rv, *pv)
        g = vjp(cv)
        for n, k in enumerate(wrt_rows):
            val = g[k]
            if k in av:
                val = val + av[k][...]
            o[n][...] = val.astype(o[n].dtype)
        for n, k in enumerate(wrt_params):
            ref = o[len(wrt_rows) + n]

            @pl.when(first)
            def _(ref=ref):
                ref[...] = jnp.zeros_like(ref)

            ref[...] += g[nr + k]

    res = pl.pallas_call(
        body, grid=(S // tm, ncol),
        in_specs=list(specs) + [_full_spec(p) for p in params] + list(cspecs)
        + [pl.BlockSpec((tm, a.shape[1] // ncol), lambda i, j: (i, j)) for a in aarrs],
        out_specs=[pl.BlockSpec((tm, w), lambda i, j: (i, j)) for w in widths] + [_full_spec(params[k]) for k in wrt_params],
        out_shape=[SDS((S, w * ncol), dt) for w, dt in zip(widths, row_dtypes)]
        + [SDS(params[k].shape, f32) for k in wrt_params],
        name=name, compiler_params=_params("arbitrary", "arbitrary"))(*arrs, *params, *carrs, *aarrs)
    return res[:len(wrt_rows)], res[len(wrt_rows):]


def _rms(x, w):
    return x * lax.rsqrt(jnp.mean(x * x, axis=-1, keepdims=True) + EPS) * w


def _rms_fn(x, w):
    return (_rms(x, w),)


def _res_rms_fn(f, res, w):
    return (res + _rms(f, w),)


def _res_rms_pre_fn(f, res, w_post, w_pre):
    x1 = res + _rms(f, w_post)
    return x1, _rms(x1, w_pre)


def _swap_halves(x):
    lane = lax.broadcasted_iota(jnp.int32, x.shape, 1)
    first = (lane % HEAD_DIM) < (HEAD_DIM // 2)
    n = x.shape[1]
    return jnp.where(first, pltpu.roll(x, n - HEAD_DIM // 2, 1), pltpu.roll(x, HEAD_DIM // 2, 1))


def _rope_fwd_fn(q, k, cos, sgn_sin):
    scale = HEAD_DIM ** -0.5
    return ((q * cos + _swap_halves(q) * sgn_sin) * scale, k * cos + _swap_halves(k) * sgn_sin)


def _rope_bwd_fn(dq, dk, cos, sgn_sin):
    dq = dq * (HEAD_DIM ** -0.5)
    return (dq * cos + _swap_halves(dq * sgn_sin), dk * cos + _swap_halves(dk * sgn_sin))


def _nt(a, b):
    return lax.dot_general(a, b, (((1,), (1,)), ((), ())), preferred_element_type=f32)


def _tn(a, b):
    return lax.dot_general(a, b, (((0,), (0,)), ((), ())), preferred_element_type=f32)


def _band_rows(j, d, nb):
    r, i = j // nb, j % nb
    if d == 1:
        cur = pl.ds(pl.multiple_of(i * 128, 128), 128)
        prev = pl.ds(pl.multiple_of(jnp.maximum(i - 1, 0) * 128, 128), 128)
    else:
        cur = pl.ds(i * (128 * d) + r, 128, stride=d)
        prev = pl.ds(jnp.maximum(i - 1, 0) * (128 * d) + r, 128, stride=d)
    return cur, prev, (i == 0).astype(jnp.int32)


def _band_bias(bias_ref):
    a = lax.broadcasted_iota(jnp.int32, (256, 256), 0) % 128
    c = lax.broadcasted_iota(jnp.int32, (256, 256), 1)
    own = jnp.logical_and(c < 128, c <= a)
    before = jnp.logical_and(c >= 128, c - 128 >= a)
    bias_ref[0] = jnp.where(jnp.logical_or(own, before), 0.0, NEG)
    bias_ref[1] = jnp.where(own, 0.0, NEG)


def _stack_heads(x, head_a):
    return jnp.concatenate([jnp.where(head_a, x, 0.0), jnp.where(head_a, 0.0, x)], axis=0)


def _unstack_heads(x2, head_a):
    return jnp.where(head_a, x2[:128], x2[128:])


def _pair_at(S, first_col):
    return pl.BlockSpec((S, 128), lambda h: (0, first_col // 128 + h))


def _attn_fwd(proj, cos, sgn_sin, name):
    S = proj.shape[0]
    nblk = S // 128

    def body(qp_ref, kp_ref, v_ref, cos_ref, sin_ref, out_ref, lse_ref, bias_ref, q_ref, k_ref, *scr):
        head_a = lax.broadcasted_iota(jnp.int32, (1, 128), 1) < HEAD_DIM
        _band_bias(bias_ref)
        q_ref[...], k_ref[...] = _rope_fwd_fn(qp_ref[...], kp_ref[...], cos_ref[...], sin_ref[...])
        for b, (d, nb) in enumerate(BRANCHES):
            ob_ref, lb_ref = scr[2 * b], scr[2 * b + 1]

            def blk(j, carry, d=d, nb=nb, ob_ref=ob_ref, lb_ref=lb_ref):
                cur, prev, first = _band_rows(j, d, nb)
                q2 = _stack_heads(q_ref[cur, :], head_a).astype(bf16)
                if nb == 1:
                    k2, v2, bias = k_ref[cur, :].astype(bf16), v_ref[cur, :].astype(bf16), bias_ref[1][:, :128]
                else:
                    k2 = jnp.concatenate([k_ref[cur, :], k_ref[prev, :]], axis=0).astype(bf16)
                    v2 = jnp.concatenate([v_ref[cur, :], v_ref[prev, :]], axis=0).astype(bf16)
                    bias = bias_ref[first]
                s = _nt(q2, k2) + bias
                mx = jnp.max(s, axis=1, keepdims=True)
                p = jnp.exp(s - mx)
                l = jnp.sum(p, axis=1, keepdims=True)
                o = jnp.dot(p.astype(bf16), v2, preferred_element_type=f32) / l
                ob_ref[cur, :] = _unstack_heads(o, head_a)
                lb_ref[cur, :] = _unstack_heads(jnp.broadcast_to(mx + jnp.log(l), (256, 128)), head_a)
                return carry

            lax.fori_loop(0, nblk, blk, 0, unroll=16)
        l0, l1, l2 = scr[1][...], scr[3][...], scr[5][...]
        mx = jnp.maximum(jnp.maximum(l0, l1), l2)
        e0, e1, e2 = jnp.exp(l0 - mx), jnp.exp(l1 - mx), jnp.exp(l2 - mx)
        den = e0 + e1 + e2
        out_ref[...] = ((e0 * scr[0][...] + e1 * scr[2][...] + e2 * scr[4][...]) / den).astype(out_ref.dtype)
        lse_ref[...] = mx + jnp.log(den)

    pair = pl.BlockSpec((S, 128), lambda h: (0, h))
    return pl.pallas_call(
        body, grid=(N_HEADS_A // 2,),
        in_specs=[pair, _pair_at(S, ATTN_W), _pair_at(S, 2 * ATTN_W), pair, pair], out_specs=[pair, pair],
        out_shape=[SDS((S, ATTN_W), bf16), SDS((S, ATTN_W), f32)],
        scratch_shapes=[pltpu.VMEM((2, 256, 256), f32)] + [pltpu.VMEM((S, 128), f32)] * 8,
        name=name, compiler_params=_params("parallel"))(proj, proj, proj, cos, sgn_sin)


def _attn_bwd(proj, cos, sgn_sin, dmix_in, out, lse, name):
    S = proj.shape[0]
    nblk = S // 128

    def body(qp_ref, kp_ref, v_ref, cos_ref, sin_ref, do_ref, out_ref, lse_ref, dq_ref, dk_ref, dv_ref,
             bias_ref, t_ref, q_ref, k_ref):
        head_a = lax.broadcasted_iota(jnp.int32, (1, 128), 1) < HEAD_DIM
        _band_bias(bias_ref)
        q_ref[...], k_ref[...] = _rope_fwd_fn(qp_ref[...], kp_ref[...], cos_ref[...], sin_ref[...])
        x = do_ref[...] * out_ref[...].astype(f32)
        t_ref[...] = jnp.where(head_a, jnp.sum(jnp.where(head_a, x, 0.0), axis=1, keepdims=True),
                               jnp.sum(jnp.where(head_a, 0.0, x), axis=1, keepdims=True))
        dq_ref[...] = jnp.zeros_like(dq_ref)
        dk_ref[...] = jnp.zeros_like(dk_ref)
        dv_ref[...] = jnp.zeros_like(dv_ref)
        for d, nb in BRANCHES:
            def blk(j, carry, d=d, nb=nb):
                cur, prev, first = _band_rows(j, d, nb)
                q2 = _stack_heads(q_ref[cur, :], head_a).astype(bf16)
                do2 = _stack_heads(do_ref[cur, :], head_a).astype(bf16)
                t, lse_b = t_ref[cur, :], lse_ref[cur, :]
                t2 = jnp.concatenate([t[:, :1], t[:, HEAD_DIM:HEAD_DIM + 1]], axis=0)
                lse2 = jnp.concatenate([lse_b[:, :1], lse_b[:, HEAD_DIM:HEAD_DIM + 1]], axis=0)
                if nb == 1:
                    k2, v2, bias = k_ref[cur, :].astype(bf16), v_ref[cur, :].astype(bf16), bias_ref[1][:, :128]
                else:
                    k2 = jnp.concatenate([k_ref[cur, :], k_ref[prev, :]], axis=0).astype(bf16)
                    v2 = jnp.concatenate([v_ref[cur, :], v_ref[prev, :]], axis=0).astype(bf16)
                    bias = bias_ref[first]
                p = jnp.exp(_nt(q2, k2) + bias - lse2)
                ds = (p * (_nt(do2, v2) - t2)).astype(bf16)
                dq_ref[cur, :] += _unstack_heads(jnp.dot(ds, k2, preferred_element_type=f32), head_a)
                dk2, dv2 = _tn(ds, q2), _tn(p.astype(bf16), do2)
                dk_ref[cur, :] += dk2[:128]
                dv_ref[cur, :] += dv2[:128]
                if nb != 1:
                    dk_ref[prev, :] += dk2[128:]
                    dv_ref[prev, :] += dv2[128:]
                return carry

            lax.fori_loop(0, nblk, blk, 0, unroll=16)
        dq_ref[...], dk_ref[...] = _rope_bwd_fn(dq_ref[...], dk_ref[...], cos_ref[...], sin_ref[...])

    pair = pl.BlockSpec((S, 128), lambda h: (0, h))
    return pl.pallas_call(
        body, grid=(N_HEADS_A // 2,),
        in_specs=[pair, _pair_at(S, ATTN_W), _pair_at(S, 2 * ATTN_W), pair, pair, pair, pair, pair],
        out_specs=[pair] * 3, out_shape=[SDS((S, ATTN_W), f32)] * 3,
        scratch_shapes=[pltpu.VMEM((2, 256, 256), f32)] + [pltpu.VMEM((S, 128), f32)] * 3,
        name=name, compiler_params=_params("parallel"))(proj, proj, proj, cos, sgn_sin, dmix_in, out, lse)


def _conv_val(x, w, K, rows):
    acc = x * w[K - 1:K, :]
    for s in range(1, K):
        acc = acc + jnp.where(rows >= s, pltpu.roll(x, s, 0), 0.0) * w[K - 1 - s:K - s, :]
    return acc


def _colconv_fwd(xs, ws, bs, K, fn, nblk, tc, outs, name):
    S = xs[0][0].shape[0]
    n = len(xs)
    has_b = bs is not None

    def body(*refs):
        rows = lax.broadcasted_iota(jnp.int32, (S, tc), 0)
        cs = []
        for k in range(n):
            c = _conv_val(refs[k][...].astype(f32), refs[n + k][...], K, rows)
            if has_b:
                c = c + refs[2 * n + k][...]
            cs.append(c)
        for o_ref, val in zip(refs[(3 if has_b else 2) * n:], fn(*cs)):
            o_ref[...] = val.astype(o_ref.dtype)

    def cspec(rows_, cb0):
        return pl.BlockSpec((rows_, tc), lambda j, cb0=cb0: (0, cb0 + j))

    in_specs = [cspec(S, cb) for _, cb in xs] + [cspec(K, cb) for _, cb in ws]
    args = [a for a, _ in xs] + [a for a, _ in ws]
    if has_b:
        in_specs += [cspec(1, cb) for _, cb in bs]
        args += [a for a, _ in bs]
    return pl.pallas_call(
        body, grid=(nblk,), in_specs=in_specs, out_specs=[cspec(S, 0) for _ in outs],
        out_shape=[SDS((S, nblk * tc), dt) for dt in outs], name=name, compiler_params=_params("parallel"))(*args)


def _colconv_bwd(xs, ws, bs, K, fn, douts, nblk, tc, name, dx_dtype=f32):
    S = xs[0][0].shape[0]
    n, nd = len(xs), len(douts)
    has_b = bs is not None
    nin = (3 if has_b else 2) * n

    def body(*refs):
        rows = lax.broadcasted_iota(jnp.int32, (S, tc), 0)
        x = [refs[k][...].astype(f32) for k in range(n)]
        w = [refs[n + k][...] for k in range(n)]
        cs = []
        for k in range(n):
            c = _conv_val(x[k], w[k], K, rows)
            if has_b:
                c = c + refs[2 * n + k][...]
            cs.append(c)
        _, vjp = jax.vjp(fn, *cs)
        dcs = vjp(tuple(r[...].astype(f32) for r in refs[nin:nin + nd]))
        o = refs[nin + nd:]
        for k in range(n):
            dc = dcs[k]
            dx = dc * w[k][K - 1:K, :]
            o[n + k][K - 1:K, :] = jnp.sum(dc * x[k], axis=0, keepdims=True)
            for s in range(1, K):
                dx = dx + jnp.where(rows < S - s, pltpu.roll(dc, S - s, 0), 0.0) * w[k][K - 1 - s:K - s, :]
                xsh = jnp.where(rows >= s, pltpu.roll(x[k], s, 0), 0.0)
                o[n + k][K - 1 - s:K - s, :] = jnp.sum(dc * xsh, axis=0, keepdims=True)
            o[k][...] = dx.astype(o[k].dtype)
            if has_b:
                o[2 * n + k][...] = jnp.sum(dc, axis=0, keepdims=True)

    def cspec(rows_, cb0):
        return pl.BlockSpec((rows_, tc), lambda j, cb0=cb0: (0, cb0 + j))

    in_specs = [cspec(S, cb) for _, cb in xs] + [cspec(K, cb) for _, cb in ws]
    args = [a for a, _ in xs] + [a for a, _ in ws]
    if has_b:
        in_specs += [cspec(1, cb) for _, cb in bs]
        args += [a for a, _ in bs]
    in_specs += [cspec(S, 0) for _ in douts]
    args += list(douts)
    W = nblk * tc
    out_specs = [cspec(S, 0)] * n + [cspec(K, 0)] * n + ([cspec(1, 0)] * n if has_b else [])
    out_shape = [SDS((S, W), dx_dtype)] * n + [SDS((K, W), f32)] * n + ([SDS((1, W), f32)] * n if has_b else [])
    res = pl.pallas_call(body, grid=(nblk,), in_specs=in_specs, out_specs=out_specs, out_shape=out_shape,
                         name=name, compiler_params=_params("parallel"))(*args)
    return res[:n], res[n:2 * n], res[2 * n:]


def _silu_fn(c):
    return (c * jax.nn.sigmoid(c),)


_GELU_C, _GELU_A = 0.7978845608028654, 0.044715


@jax.custom_vjp
def _geglu(gate, up):
    return 0.5 * gate * (1.0 + jnp.tanh(_GELU_C * (gate + _GELU_A * gate * gate * gate))) * up


def _geglu_vjp_fwd(gate, up):
    return _geglu(gate, up), (gate, up)


def _geglu_vjp_bwd(res, d):
    gate, up = res
    g2 = gate * gate
    t = jnp.tanh(_GELU_C * gate * (1.0 + _GELU_A * g2))
    h = 0.5 * (1.0 + t)
    dgelu = h + (0.5 * _GELU_C) * gate * (1.0 - t * t) * (1.0 + (3.0 * _GELU_A) * g2)
    return d * up * dgelu, d * (gate * h)


_geglu.defvjp(_geglu_vjp_fwd, _geglu_vjp_bwd)


def _geglu_fn(gate, up):
    return (_geglu(gate, up),)


def _softplus(x):
    u = jnp.exp(jnp.minimum(x, 20.0))
    small = u * (1.0 - 0.5 * u)
    return jnp.where(x > 20.0, x, jnp.where(u < 1e-4, small, jnp.log(1.0 + u)))


def _bmm(a, b, precision=None):
    return lax.dot_general(a, b, (((2,), (1,)), ((0,), (0,))), precision=precision, preferred_element_type=f32)


def _bnt(a, b, precision=None):
    return lax.dot_general(a, b, (((2,), (2,)), ((0,), (0,))), precision=precision, preferred_element_type=f32)


def _btn(a, b, precision=None):
    return lax.dot_general(a, b, (((1,), (1,)), ((0,), (0,))), precision=precision, preferred_element_type=f32)


@jax.custom_vjp
def _unit_lower_inverse(A):
    n = A.shape[-1]
    eye = (lax.broadcasted_iota(jnp.int32, (1, n, n), 1) == lax.broadcasted_iota(jnp.int32, (1, n, n), 2)).astype(f32)
    P = -A
    T = eye + P
    for _ in range(5):
        P = _bmm(P, P, HI)
        T = T + _bmm(T, P, HI)
    return T


def _unit_lower_inverse_fwd(A):
    T = _unit_lower_inverse(A)
    return T, T


def _unit_lower_inverse_bwd(T, dT):
    return (-_btn(T, _bnt(dT, T, HI), HI),)


_unit_lower_inverse.defvjp(_unit_lower_inverse_fwd, _unit_lower_inverse_bwd)


def _dn_prep_fn(q, k, v, ba, alog, dtb, h):
    G, C = q.shape[0], CHUNK
    lane = lax.broadcasted_iota(jnp.int32, (1, 1, 128), 2)

    def sel(arr, idx):
        return jnp.sum(jnp.where(lane == idx, arr, 0.0), axis=-1, keepdims=True)

    beta = jax.nn.sigmoid(sel(ba, h))
    g = -jnp.exp(sel(alog[None], h)) * _softplus(sel(ba, N_HEADS_D + h) + sel(dtb[None], h))
    qn = q * lax.rsqrt(jnp.sum(q * q, axis=-1, keepdims=True) + EPS) * (DK ** -0.5)
    kn = k * lax.rsqrt(jnp.sum(k * k, axis=-1, keepdims=True) + EPS)
    ii = lax.broadcasted_iota(jnp.int32, (1, C, C), 1)
    jj = lax.broadcasted_iota(jnp.int32, (1, C, C), 2)
    tril, strict = ii >= jj, ii > jj
    gsq = jnp.broadcast_to(g, (G, C, C))
    gcol = _bmm(jnp.broadcast_to(tril.astype(f32), (G, C, C)), gsq, HI)
    grow = _bmm(jnp.ones((G, C, C), f32), jnp.where(ii <= jj, gsq, 0.0), HI)
    decay = jnp.exp(jnp.where(tril, gcol - grow, NEG))
    gc = gcol[:, :, :1]
    glast = gcol[:, C - 1:C, :1]
    kb = kn * beta
    A = jnp.where(strict, _bnt(kb.astype(bf16), kn.astype(bf16)) * decay, 0.0)
    T = _unit_lower_inverse(A).astype(bf16)
    u = _bmm(T, (v * beta).astype(bf16))
    w = _bmm(T, (kb * jnp.exp(gc)).astype(bf16))
    qk = _bnt(qn.astype(bf16), kn.astype(bf16)) * decay
    qd = qn * jnp.exp(gc)
    kd = kn * jnp.exp(glast - gc)
    return u, w, qk, qd, kd, jnp.broadcast_to(jnp.exp(glast), (G, C, DK))


def _dn_scan_fn(u, w, qk, qd, kd, eg, St):
    b = lambda a: a.astype(bf16)
    vnew = u - _bmm(b(w), b(St))
    o = _bmm(b(qd), b(St)) + _bmm(b(qk), b(vnew))
    return o, St * eg[:, :1, :] + _btn(b(kd), b(vnew))


def _dn_post_fn(o, z, nw):
    return (_rms(o, nw) * (z * jax.nn.sigmoid(z)),)


DN_GROUP = 16


def _dn_prep_specs(S, rows):
    def col(first):
        return pl.BlockSpec((rows, DK), lambda i, h, first=first: (i, first // DK + h))

    par = pl.BlockSpec((1, 128), lambda i, h: (0, 0))
    return [col(0), col(N_HEADS_D * DK), col(2 * N_HEADS_D * DK),
            pl.BlockSpec((rows, 128), lambda i, h: (i, DN_BA_COL // 128)), par, par]


def _dn_prep(qkv, proj, alog, dtb, name):
    S = qkv.shape[0]
    G = DN_GROUP
    rows = G * CHUNK

    def body(q_ref, k_ref, v_ref, ba_ref, al_ref, dt_ref, u_ref, w_ref, qk_ref, qd_ref, kd_ref, eg_ref):
        h = pl.program_id(1)
        r3 = lambda ref: ref[...].reshape(G, CHUNK, 128)
        u, w, qk, qd, kd, eg = _dn_prep_fn(r3(q_ref), r3(k_ref), r3(v_ref), r3(ba_ref), al_ref[...], dt_ref[...], h)
        for ref, val in ((u_ref, u), (w_ref, w), (qd_ref, qd), (kd_ref, kd), (eg_ref, eg)):
            ref[...] = val.reshape(rows, DK)
        qk_ref[:, :CHUNK] = qk.reshape(rows, CHUNK)
        qk_ref[:, CHUNK:] = jnp.zeros((rows, DK - CHUNK), f32)

    out = pl.BlockSpec((rows, DK), lambda i, h: (i, h))
    return pl.pallas_call(
        body, grid=(S // rows, N_HEADS_D), in_specs=_dn_prep_specs(S, rows), out_specs=[out] * 6,
        out_shape=[SDS((S, N_HEADS_D * DK), f32)] * 6, name=name,
        compiler_params=_params("parallel", "parallel"))(qkv, qkv, qkv, proj, alog, dtb)


def _dn_prep_bwd(qkv, proj, alog, dtb, cts, name):
    S = qkv.shape[0]
    G = DN_GROUP
    rows = G * CHUNK

    def body(q_ref, k_ref, v_ref, ba_ref, al_ref, dt_ref, du_ref, dw_ref, dqk_ref, dqd_ref, dkd_ref, deg_ref,
             dq_ref, dk_ref, dv_ref, dba_ref, dal_ref, ddt_ref):
        i, h = pl.program_id(0), pl.program_id(1)
        r3 = lambda ref: ref[...].reshape(G, CHUNK, 128)
        _, vjp = jax.vjp(lambda q, k, v, ba, al, dt: _dn_prep_fn(q, k, v, ba, al, dt, h),
                         r3(q_ref), r3(k_ref), r3(v_ref), r3(ba_ref), al_ref[...], dt_ref[...])
        dqk = dqk_ref[:, :CHUNK].reshape(G, CHUNK, CHUNK)
        dq, dk, dv, dba, dal, ddt = vjp((r3(du_ref), r3(dw_ref), dqk, r3(dqd_ref), r3(dkd_ref), r3(deg_ref)))
        dq_ref[...] = dq.reshape(rows, DK)
        dk_ref[...] = dk.reshape(rows, DK)
        dv_ref[...] = dv.reshape(rows, DK)

        @pl.when(h == 0)
        def _():
            dba_ref[...] = jnp.zeros_like(dba_ref)

        @pl.when(jnp.logical_and(i == 0, h == 0))
        def _():
            dal_ref[...] = jnp.zeros_like(dal_ref)
            ddt_ref[...] = jnp.zeros_like(ddt_ref)

        dba_ref[...] += dba.reshape(rows, 128)
        dal_ref[...] += dal
        ddt_ref[...] += ddt

    hcol = pl.BlockSpec((rows, DK), lambda i, h: (i, h))
    par = pl.BlockSpec((1, 128), lambda i, h: (0, 0))
    W = N_HEADS_D * DK
    return pl.pallas_call(
        body, grid=(S // rows, N_HEADS_D), in_specs=_dn_prep_specs(S, rows) + [hcol] * 6,
        out_specs=[hcol] * 3 + [pl.BlockSpec((rows, 128), lambda i, h: (i, 0)), par, par],
        out_shape=[SDS((S, W), f32)] * 3 + [SDS((S, 128), f32), SDS((1, 128), f32), SDS((1, 128), f32)], name=name,
        compiler_params=_params("arbitrary", "arbitrary"))(qkv, qkv, qkv, proj, alog, dtb, *cts)


def _heads(x):
    return jnp.stack([x[:, DK * h:DK * (h + 1)] for h in range(N_HEADS_D)])


SCAN_CHUNKS = 8


def _dn_scan(pre, name):
    S = pre[0].shape[0]
    NCH = S // CHUNK
    rows = SCAN_CHUNKS * CHUNK

    def body(u_ref, w_ref, qk_ref, qd_ref, kd_ref, eg_ref, o_ref, st_ref, s_ref):
        @pl.when(pl.program_id(0) == 0)
        def _():
            s_ref[...] = jnp.zeros_like(s_ref)

        St = s_ref[...]
        for k in range(SCAN_CHUNKS):
            r = slice(k * CHUNK, (k + 1) * CHUNK)
            st_ref[k] = St
            o, St = _dn_scan_fn(_heads(u_ref[r, :]), _heads(w_ref[r, :]), _heads(qk_ref[r, :])[:, :, :CHUNK],
                                _heads(qd_ref[r, :]), _heads(kd_ref[r, :]), _heads(eg_ref[r, :]), St)
            for h in range(N_HEADS_D):
                o_ref[r, DK * h:DK * (h + 1)] = o[h]
        s_ref[...] = St

    blk = pl.BlockSpec((rows, N_HEADS_D * DK), lambda n: (n, 0))
    return pl.pallas_call(
        body, grid=(S // rows,), in_specs=[blk] * 6,
        out_specs=[blk, pl.BlockSpec((SCAN_CHUNKS, N_HEADS_D, DK, DK), lambda n: (n, 0, 0, 0))],
        out_shape=[SDS((S, N_HEADS_D * DK), f32), SDS((NCH, N_HEADS_D, DK, DK), f32)],
        scratch_shapes=[pltpu.VMEM((N_HEADS_D, DK, DK), f32)], name=name, compiler_params=_params("arbitrary"))(*pre)


def _dn_scan_bwd(pre, states, do, name):
    S = do.shape[0]
    rows = SCAN_CHUNKS * CHUNK
    steps = S // rows

    def body(u_ref, w_ref, qk_ref, qd_ref, kd_ref, eg_ref, st_ref, do_ref,
             du_ref, dw_ref, dqk_ref, dqd_ref, dkd_ref, deg_ref, ds_ref):
        @pl.when(pl.program_id(0) == 0)
        def _():
            ds_ref[...] = jnp.zeros_like(ds_ref)

        dS = ds_ref[...]
        for k in reversed(range(SCAN_CHUNKS)):
            r = slice(k * CHUNK, (k + 1) * CHUNK)
            _, vjp = jax.vjp(_dn_scan_fn, _heads(u_ref[r, :]), _heads(w_ref[r, :]), _heads(qk_ref[r, :])[:, :, :CHUNK],
                             _heads(qd_ref[r, :]), _heads(kd_ref[r, :]), _heads(eg_ref[r, :]), st_ref[k])
            du, dw, dqk, dqd, dkd, deg, dS = vjp((_heads(do_ref[r, :]), dS))
            for h in range(N_HEADS_D):
                c = slice(DK * h, DK * (h + 1))
                for ref, val in ((du_ref, du), (dw_ref, dw), (dqd_ref, dqd), (dkd_ref, dkd), (deg_ref, deg)):
                    ref[r, c] = val[h]
                dqk_ref[r, DK * h:DK * h + CHUNK] = dqk[h]
                dqk_ref[r, DK * h + CHUNK:DK * (h + 1)] = jnp.zeros((CHUNK, DK - CHUNK), f32)
        ds_ref[...] = dS

    blk = pl.BlockSpec((rows, N_HEADS_D * DK), lambda n: (steps - 1 - n, 0))
    return pl.pallas_call(
        body, grid=(steps,),
        in_specs=[blk] * 6 + [pl.BlockSpec((SCAN_CHUNKS, N_HEADS_D, DK, DK), lambda n: (steps - 1 - n, 0, 0, 0)), blk],
        out_specs=[blk] * 6, out_shape=[SDS((S, N_HEADS_D * DK), f32)] * 6,
        scratch_shapes=[pltpu.VMEM((N_HEADS_D, DK, DK), f32)], name=name,
        compiler_params=_params("arbitrary"))(*pre, states, do)


def _loss_head(y, t, name):
    S, D = y.shape
    tm = ROW_TILE

    def body(y_ref, t_ref, dy_ref, l_ref):
        i = pl.program_id(0)
        d = y_ref[...] - t_ref[...]
        dy_ref[...] = d * (1.0 / D)
        part = jnp.sum(jnp.sum(d * d, axis=1, keepdims=True), axis=0, keepdims=True) * (0.5 / D)

        @pl.when(i == 0)
        def _():
            l_ref[...] = jnp.zeros_like(l_ref)

        l_ref[...] += jnp.broadcast_to(part, l_ref.shape)

    spec = pl.BlockSpec((tm, D), lambda i: (i, 0))
    dy, l = pl.pallas_call(body, grid=(S // tm,), in_specs=[spec, spec],
                           out_specs=[spec, pl.BlockSpec((1, 128), lambda i: (0, 0))],
                           out_shape=[SDS((S, D), f32), SDS((1, 128), f32)], name=name,
                           compiler_params=_params("arbitrary"))(y, t)
    return l[0, 0], dy


def _adamw_refs(w_ref, g_ref, m_ref, v_ref, d_ref, mo_ref, vo_ref):
    gv = g_ref[...]
    m2 = ADAM_B1 * m_ref[...] + (1.0 - ADAM_B1) * gv
    v2 = ADAM_B2 * v_ref[...] + (1.0 - ADAM_B2) * (gv * gv)
    m_hat = m2 / (1.0 - ADAM_B1 ** ADAM_STEP)
    v_hat = v2 / (1.0 - ADAM_B2 ** ADAM_STEP)
    d_ref[...] = -ADAM_LR * (m_hat / (jnp.sqrt(v_hat) + ADAM_EPS) + ADAM_WD * w_ref[...])
    mo_ref[...] = m2
    vo_ref[...] = v2


def _adamw_small(ws, gs, ms, vs, name):
    n = len(ws)

    def body(*refs):
        for i in range(n):
            _adamw_refs(*[refs[k * n + i] for k in range(7)])

    res = pl.pallas_call(body, out_shape=[SDS(a.shape, f32) for a in ws] * 3, name=name)(*ws, *gs, *ms, *vs)
    return res[:n], res[n:2 * n], res[2 * n:]


def _adamw(w, g, m, v, tr, name):
    L, R, C = w.shape
    assert R % tr == 0

    def body(*refs):
        _adamw_refs(*refs)

    spec = pl.BlockSpec((1, tr, C), lambda l, i: (l, i, 0))
    return pl.pallas_call(body, grid=(L, R // tr), in_specs=[spec] * 4, out_specs=[spec] * 3,
                          out_shape=[SDS((L, R, C), f32)] * 3, name=name,
                          compiler_params=_params("parallel", "parallel"))(w, g, m, v)


def _rope_tables(S):
    inv = 1.0 / (10000.0 ** (jnp.arange(0, HEAD_DIM, 2, dtype=f32) / HEAD_DIM))
    ang = jnp.arange(S, dtype=f32)[:, None] * inv[None, :]
    cos, sin = jnp.cos(ang), jnp.sin(ang)
    return (jnp.tile(jnp.concatenate([cos, cos], axis=1), (1, N_HEADS_A)),
            jnp.tile(jnp.concatenate([-sin, sin], axis=1), (1, N_HEADS_A)))


def _layer_fwd(x, W, cos, sgn_sin, l, late_weights=None, h1=None, next_pre_mix=None, mid_mixer=None, after_ffn_in=None):
    n = f"l{l}_"
    if h1 is None:
        (h1,) = _rows(_rms_fn, [x], [W["norm_pre_mix"]], [(D_MODEL, bf16)], n + "pre_mix_norm")
    proj = _mm(h1, W["w_in"], "nn", 1024, 768, f32, n + "in_proj")
    attn_out, lse = _attn_fwd(proj, cos, sgn_sin, n + "attn_fwd")
    (qkv,) = _colconv_fwd([(proj, DN_QKV_COL // 512)], [(W["dn_conv_w"], 0)], None, 4, _silu_fn, 3, 512, [f32], n + "dn_conv")
    dn_pre = _dn_prep(qkv, proj, W["dn_a_log"], W["dn_dt_bias"], n + "dn_prep")
    if mid_mixer is not None:
        W = dict(W, dn_norm_w=W["dn_norm_w"] + mid_mixer(dn_pre[0]))
    dn_o, dn_states = _dn_scan(dn_pre, n + "dn_scan")
    (dn_out,) = _rows(_dn_post_fn, [(dn_o, DK, 0), (proj, DK, DN_Z_COL // DK)], [W["dn_norm_w"]], [(DK, bf16)], n + "dn_post",
                      ncol=N_HEADS_D, tm=4 * ROW_TILE)
    mix_in = jnp.concatenate([attn_out, dn_out], axis=1)
    late = late_weights(mix_in) if late_weights is not None else {}
    W = {**W, **late}
    mix = _mm(mix_in, W["w_out"], "nn", 1024, 512, f32, n + "out_proj")
    x1, h2 = _rows(_res_rms_pre_fn, [mix, x], [W["norm_post_mix"], W["norm_pre_ffn"]],
                   [(D_MODEL, f32), (D_MODEL, bf16)], n + "post_mix_pre_ffn_norm")
    u0 = _mm(h2, W["ffn_w_in"], "nn", 1024, D_FF // 2, bf16, n + "ffn_in")
    if after_ffn_in is not None:
        W = dict(W, ffn_conv_b=W["ffn_conv_b"] + after_ffn_in(u0))
    nb_ff = D_FF // 256
    (act,) = _colconv_fwd([(u0, 0), (u0, nb_ff)], [(W["ffn_conv_w"], 0), (W["ffn_conv_w"], nb_ff)],
                          [(W["ffn_conv_b"], 0), (W["ffn_conv_b"], nb_ff)], 3, _geglu_fn, nb_ff, 256, [bf16],
                          n + "ffn_conv_glu")
    f = _mm(act, W["ffn_w_out"], "nn", 1024, 512, f32, n + "ffn_out")
    if next_pre_mix is None:
        (x2,), h1_next = _rows(_res_rms_fn, [f, x1], [W["norm_post_ffn"]], [(D_MODEL, f32)], n + "post_ffn_norm"), None
    else:
        x2, h1_next = _rows(_res_rms_pre_fn, [f, x1], [W["norm_post_ffn"], next_pre_mix],
                            [(D_MODEL, f32), (D_MODEL, bf16)], n + "post_ffn_next_pre_mix_norm")
    saved = dict(x=x, h1=h1, proj=proj, attn_out=attn_out, lse=lse, qkv=qkv, dn_pre=dn_pre, dn_o=dn_o,
                 dn_states=dn_states, mix_in=mix_in, mix=mix, x1=x1, h2=h2, u0=u0, act=act, f=f, late=late)
    return x2, h1_next, saved


def _layer_bwd(dx2, sv, W, cos, sgn_sin, l, after_ffn=None, next_layer=None, first_layer=True):
    n = f"l{l}_"
    S = dx2.shape[0]
    g = {}
    g_next_pre = None
    if next_layer is None:
        (df,), (g["norm_post_ffn"],) = _rows_vjp(_rms_fn, [sv["f"]], [W["norm_post_ffn"]], [dx2], [0], [0],
                                                 n + "post_ffn_norm_bwd", row_dtype=bf16)
    else:
        (df, dx2), (g["norm_post_ffn"], g_next_pre) = _rows_vjp(
            _res_rms_pre_fn, [sv["f"], sv["x1"]], [W["norm_post_ffn"], next_layer[1]], [dx2, next_layer[0]], [0, 1], [0, 1],
            n + "post_ffn_next_pre_mix_norm_bwd", row_dtype=[bf16, f32])
    dact = _mm(df, W["ffn_w_out"], "nt", 512, D_FF // 2, f32, n + "ffn_out_dx")
    g["ffn_w_out"] = _mm(sv["act"], df, "tn", 256, 1024, f32, n + "ffn_out_dw")
    nb_ff = D_FF // 256
    u0 = sv["u0"]
    dxs, dws, dbs = _colconv_bwd([(u0, 0), (u0, nb_ff)], [(W["ffn_conv_w"], 0), (W["ffn_conv_w"], nb_ff)],
                                 [(W["ffn_conv_b"], 0), (W["ffn_conv_b"], nb_ff)], 3, _geglu_fn, [dact], nb_ff, 256,
                                 n + "ffn_conv_glu_bwd", dx_dtype=bf16)
    du0 = jnp.concatenate(dxs, axis=1)
    g["ffn_conv_w"] = jnp.concatenate(dws, axis=1)
    g["ffn_conv_b"] = jnp.concatenate(dbs, axis=1)
    dh2 = _mm(du0, W["ffn_w_in"], "nt", 512, 512, f32, n + "ffn_in_dx")
    g["ffn_w_in"] = _mm(sv["h2"], du0, "tn", 512, D_FF // 2, f32, n + "ffn_in_dw", column_shards=True)
    if after_ffn is not None:
        W = dict(W, norm_post_mix=W["norm_post_mix"] + after_ffn(g, dh2))
    (dmix, dx1), (g["norm_post_mix"], g["norm_pre_ffn"]) = _rows_vjp(
        _res_rms_pre_fn, [sv["mix"], sv["x"]], [W["norm_post_mix"], W["norm_pre_ffn"]], [dx2, dh2], [0, 1], [0, 1],
        n + "post_mix_pre_ffn_norm_bwd", row_dtype=[bf16, f32])
    dmix_in = _mm(dmix, W["w_out"], "nt", 512, 512, f32, n + "out_proj_dx")
    g["w_out"] = _mm(sv["mix_in"], dmix, "tn", 512, 512, f32, n + "out_proj_dw")

    (ddn_o, dz), (g["dn_norm_w"],) = _rows_vjp(
        _dn_post_fn, [(sv["dn_o"], DK, 0), (sv["proj"], DK, DN_Z_COL // DK)], [W["dn_norm_w"]], [(dmix_in, DK, ATTN_W // DK)],
        [0, 1], [0], n + "dn_post_bwd", ncol=N_HEADS_D, tm=4 * ROW_TILE)
    dpre = _dn_scan_bwd(sv["dn_pre"], sv["dn_states"], ddn_o, n + "dn_scan_bwd")
    dq, dk, dv, dba, g["dn_a_log"], g["dn_dt_bias"] = _dn_prep_bwd(
        sv["qkv"], sv["proj"], W["dn_a_log"], W["dn_dt_bias"], dpre, n + "dn_prep_bwd")
    dqkv = jnp.concatenate([dq, dk, dv], axis=1)
    (dqkv0,), (g["dn_conv_w"],), _ = _colconv_bwd([(sv["proj"], DN_QKV_COL // 512)], [(W["dn_conv_w"], 0)], None, 4, _silu_fn,
                                                 [dqkv], 3, 512, n + "dn_conv_bwd")

    daq, dak, dav = _attn_bwd(sv["proj"], cos, sgn_sin, dmix_in, sv["attn_out"], sv["lse"], n + "attn_bwd")
    dproj = jnp.concatenate([daq, dak, dav, dqkv0, dz, dba, jnp.zeros((S, PROJ_W - DN_BA_COL - 128), f32)], axis=1).astype(bf16)
    dh1 = _mm(dproj, W["w_in"], "nt", 512, 512, f32, n + "in_proj_dx")
    g["w_in"] = _mm(sv["h1"], dproj, "tn", 512, 768, f32, n + "in_proj_dw")
    if not first_layer:
        return (dx1, dh1), g, g_next_pre
    (dx,), (g["norm_pre_mix"],) = _rows_vjp(_rms_fn, [sv["x"]], [W["norm_pre_mix"]], [dh1], [0], [0],
                                            n + "pre_mix_norm_bwd", adds={0: dx1})
    return dx, g, g_next_pre


def _local_step(x, target, layers):
    cos, sgn_sin = _rope_tables(x.shape[0])
    saved, h1 = [], None
    for l, W in enumerate(layers):
        nxt = layers[l + 1]["norm_pre_mix"] if l + 1 < len(layers) else None
        x, h1, sv = _layer_fwd(x, W, cos, sgn_sin, l, h1=h1, next_pre_mix=nxt)
        saved.append(sv)
    loss, dx = _loss_head(x, target, "loss_head")
    grads = [None] * len(layers)
    nxt = None
    for l in reversed(range(len(layers))):
        dx, grads[l], g_pre = _layer_bwd(dx, saved[l], layers[l], cos, sgn_sin, l, next_layer=nxt, first_layer=(l == 0))
        if g_pre is not None:
            grads[l + 1]["norm_pre_mix"] = g_pre
        if l > 0:
            dx, dh1 = dx
            nxt = (dh1, layers[l]["norm_pre_mix"])
    return loss, dx, grads


def _pos():
    x, y, c = lax.axis_index("x"), lax.axis_index("y"), lax.axis_index("c")
    return x, y, c, [(1 - x, y), (x, 1 - y), (1 - x, 1 - y)]


def _rcopy(src, dst, send_sem, recv_sem, dev):
    return pltpu.make_async_remote_copy(src_ref=src, dst_ref=dst, send_sem=send_sem, recv_sem=recv_sem,
                                        device_id=dev, device_id_type=MESH)


def _half_rows(ref, h, which, axis):
    if h is None:
        return ref
    rows = pl.ds(pl.multiple_of(which * h, 16), h)
    return ref.at[:, rows, :] if axis == 1 else ref.at[rows, :]


def _dma_sems(*counts):
    return [pltpu.SemaphoreType.DMA((k,)) for k in counts]


def _all_gather(arrs, halves, name):
    n = len(arrs)

    def body(*refs):
        ins, outs = refs[:n], refs[n:2 * n]
        send1, recv1, send2, recv2 = refs[2 * n:]
        x, y, c, chips = _pos()
        me, sib, s_me = (x, y, c), (x, y, 1 - c), 2 * x + y
        sends = []
        for i in range(n):
            for j, chip in enumerate(chips):
                cp = _rcopy(_half_rows(ins[i], halves[i], c, 1), _half_rows(outs[i].at[s_me], halves[i], c, 1),
                            send1.at[3 * i + j], recv1.at[3 * i + j], (*chip, c))
                cp.start()
                sends.append(cp)
        for i in range(n):
            for j, (px, py) in enumerate(chips):
                k = 3 * i + j
                landed = _half_rows(outs[i].at[2 * px + py], halves[i], c, 1)
                _rcopy(landed, landed, send1.at[k], recv1.at[k], me).wait_recv()
                if halves[i] is not None:
                    cp = _rcopy(landed, landed, send2.at[k], recv2.at[k], sib)
                    cp.start()
                    sends.append(cp)
        for i in range(n):
            if halves[i] is None:
                continue
            for j, (px, py) in enumerate(chips):
                k = 3 * i + j
                other = _half_rows(outs[i].at[2 * px + py], halves[i], 1 - c, 1)
                _rcopy(other, other, send2.at[k], recv2.at[k], me).wait_recv()
        for cp in sends:
            cp.wait_send()

    return pl.pallas_call(
        body, in_specs=[ANY] * n, out_specs=[ANY] * n,
        out_shape=[SDS((4,) + a.shape, a.dtype) for a in arrs],
        scratch_shapes=_dma_sems(3 * n, 3 * n, 3 * n, 3 * n), name=name)(*arrs)


HBM = pl.BlockSpec(memory_space=pltpu.HBM)
SEM = pl.BlockSpec(memory_space=pltpu.SEMAPHORE)
_EFFECT = pltpu.SideEffectType.DATAFLOW_SIDE_EFFECTING


def _in_hbm(a):
    return pltpu.with_memory_space_constraint(a, pltpu.HBM)


def _split_copy(srcs, land_shapes, plan, per, after, name):
    n, nl = len(srcs), len(land_shapes)
    k = per * n

    def body(*refs):
        ins, lands, token = refs[:n], refs[n:n + nl], refs[-1]
        send, recv = refs[n + nl + 1], refs[n + nl + 2]
        for i, (src, dst, dev, _) in enumerate(plan(ins, lands)):
            _rcopy(src, dst, send.at[i], recv.at[i], dev).start()
        token[...] = jnp.zeros_like(token)

    lands = [_in_hbm(lax.empty(s.shape, s.dtype)) for s in land_shapes]
    return pl.pallas_call(
        body, name=name,
        out_shape=(pltpu.SemaphoreType.DMA((k,)), pltpu.SemaphoreType.DMA((k,)),
                   *[pltpu.HBM(a.shape, a.dtype) for a in srcs], *[pltpu.HBM(s.shape, s.dtype) for s in land_shapes],
                   SDS((8, 128), f32)),
        in_specs=[HBM] * (n + nl) + [ANY], out_specs=(SEM, SEM, *[HBM] * (n + nl), pl.BlockSpec(memory_space=pltpu.VMEM)),
        input_output_aliases={i: 2 + i for i in range(n + nl)},
        compiler_params=pltpu.CompilerParams(has_side_effects=_EFFECT))(*[_in_hbm(a) for a in srcs], *lands, after)


def _split_wait(started, n, plan, after, name, nl=None):
    nl = n if nl is None else nl
    send, recv = started[0], started[1]
    thru = started[2:2 + n + nl]

    def body(*refs):
        ins, lands = refs[:n], refs[n:n + nl]
        send_ref, recv_ref = refs[n + nl], refs[n + nl + 1]
        for i, (src, _, dev, mine) in enumerate(plan(ins, lands)):
            cp = _rcopy(src, mine, send_ref.at[i], recv_ref.at[i], dev)
            cp.wait_send()
            cp.wait_recv()

    res = pl.pallas_call(
        body, name=name, out_shape=tuple(pltpu.HBM(a.shape, a.dtype) for a in thru),
        in_specs=[HBM] * (n + nl) + [SEM, SEM, ANY], out_specs=tuple([HBM] * (n + nl)),
        input_output_aliases={i: i for i in range(n + nl)},
        compiler_params=pltpu.CompilerParams(has_side_effects=_EFFECT))(*thru, send, recv, after)
    return res[:n], res[n:]


def _gather_plan(halves):
    def plan(ins, lands):
        x, y, c, chips = _pos()
        out = []
        for i in range(len(ins)):
            for px, py in chips:
                out.append((_half_rows(ins[i], halves[i], c, 1), _half_rows(lands[i].at[2 * x + y], halves[i], c, 1),
                            (px, py, c), _half_rows(lands[i].at[2 * px + py], halves[i], c, 1)))
        return out
    return plan


def _scatter_plan(ins, lands):
    x, y, c, chips = _pos()
    out = []
    for i in range(len(ins)):
        for j, (px, py) in enumerate(chips):
            out.append((ins[i].at[2 * px + py], lands[i].at[j], (px, py, c), lands[i].at[j]))
    return out


def _exchange_plan(ins, lands):
    x, y, c, _ = _pos()
    return [(_half_rows(g, g.shape[1] // 2, 1 - c, 1), land, (x, y, 1 - c), land) for g, land in zip(ins, lands)]


def _pass_plan(halves):
    def plan(ins, lands):
        x, y, c, chips = _pos()
        out = []
        for i in range(len(ins)):
            for px, py in chips:
                slot = ins[i].at[2 * px + py]
                out.append((_half_rows(slot, halves[i], c, 1), _half_rows(slot, halves[i], c, 1), (x, y, 1 - c),
                            _half_rows(slot, halves[i], 1 - c, 1)))
        return out
    return plan


def _exchange_halves(gs, name):
    n = len(gs)

    def body(*refs):
        ins, outs = refs[:n], refs[n:2 * n]
        send, recv = refs[2 * n:]
        x, y, c, _ = _pos()
        sends = []
        for k in range(n):
            cp = _rcopy(_half_rows(ins[k], gs[k].shape[1] // 2, 1 - c, 1), outs[k], send.at[k], recv.at[k], (x, y, 1 - c))
            cp.start()
            sends.append(cp)
        for k in range(n):
            _rcopy(outs[k], outs[k], send.at[k], recv.at[k], (x, y, c)).wait_recv()
        for cp in sends:
            cp.wait_send()

    return pl.pallas_call(
        body, in_specs=[ANY] * n, out_specs=[ANY] * n,
        out_shape=[SDS((4, g.shape[1] // 2, g.shape[2]), g.dtype) for g in gs],
        scratch_shapes=_dma_sems(n, n), name=name)(*gs)


def _scatter_partials(ps, name):
    n = len(ps)

    def body(*refs):
        ins, outs = refs[:n], refs[n:2 * n]
        send, recv = refs[2 * n:]
        x, y, c, chips = _pos()
        sends = []
        for k in range(n):
            for j, (px, py) in enumerate(chips):
                cp = _rcopy(ins[k].at[2 * px + py], outs[k].at[j], send.at[3 * k + j], recv.at[3 * k + j], (px, py, c))
                cp.start()
                sends.append(cp)
        for k in range(n):
            for j in range(3):
                _rcopy(outs[k].at[j], outs[k].at[j], send.at[3 * k + j], recv.at[3 * k + j], (x, y, c)).wait_recv()
        for cp in sends:
            cp.wait_send()

    return pl.pallas_call(
        body, in_specs=[ANY] * n, out_specs=[ANY] * n,
        out_shape=[SDS((3,) + p.shape[1:], p.dtype) for p in ps],
        scratch_shapes=_dma_sems(3 * n, 3 * n), name=name)(*ps)


def _join_halves(rs, layers, name):
    n = len(rs)

    def body(*refs):
        outs = refs[n:2 * n]
        send, recv = refs[2 * n:]
        x, y, c, _ = _pos()

        def half(k, which):
            h = rs[k].shape[1] // 2
            return _half_rows(outs[k], h, which, 1) if layers[k] is None else _half_rows(outs[k].at[layers[k]], h, which, 0)

        sends = []
        for k in range(n):
            cp = _rcopy(half(k, c), half(k, c), send.at[k], recv.at[k], (x, y, 1 - c))
            cp.start()
            sends.append(cp)
        for k in range(n):
            _rcopy(half(k, 1 - c), half(k, 1 - c), send.at[k], recv.at[k], (x, y, c)).wait_recv()
        for cp in sends:
            cp.wait_send()

    return pl.pallas_call(
        body, in_specs=[ANY] * n, out_specs=[ANY] * n, out_shape=[SDS(r.shape, r.dtype) for r in rs],
        input_output_aliases={k: k for k in range(n)}, scratch_shapes=_dma_sems(n, n), name=name)(*rs)


def _all_reduce_small(pack, name):
    R = pack.shape[0]

    def body(in_ref, out_ref, buf, send, recv):
        x, y, c, _ = _pos()
        me = 4 * x + 2 * y + c
        buf[me] = in_ref[...]
        sends = []
        for k in range(1, 8):
            peer = me ^ k
            cp = _rcopy(buf.at[me], buf.at[me], send.at[k - 1], recv.at[k - 1], ((peer >> 2) & 1, (peer >> 1) & 1, peer & 1))
            cp.start()
            sends.append(cp)
        for k in range(1, 8):
            _rcopy(buf.at[me ^ k], buf.at[me ^ k], send.at[k - 1], recv.at[k - 1], (x, y, c)).wait_recv()
        for cp in sends:
            cp.wait_send()
        acc = buf[0]
        for d in range(1, 8):
            acc = acc + buf[d]
        out_ref[...] = acc

    return pl.pallas_call(
        body, out_shape=SDS((R, 128), f32),
        in_specs=[pl.BlockSpec(memory_space=pltpu.VMEM)], out_specs=pl.BlockSpec(memory_space=pltpu.VMEM),
        scratch_shapes=[pltpu.VMEM((8, R, 128), f32)] + _dma_sems(7, 7), name=name)(pack)


def _add_sibling(g, recv, c_arr, tr, name):
    _, R, C = g.shape
    h = R // 2
    nrb = h // tr
    assert h % tr == 0

    def body(c_ref, g_ref, r_ref, o_ref):
        o_ref[...] = (g_ref[...] + r_ref[...]).astype(o_ref.dtype)

    spec = pl.BlockSpec((1, tr, C), lambda s, r, c_ref: (s, r, 0))
    grid_spec = pltpu.PrefetchScalarGridSpec(
        num_scalar_prefetch=1, grid=(4, nrb),
        in_specs=[pl.BlockSpec((1, tr, C), lambda s, r, c_ref: (s, c_ref[0] * nrb + r, 0)), spec], out_specs=spec)
    return pl.pallas_call(body, grid_spec=grid_spec, out_shape=SDS((4, h, C), bf16), name=name,
                          compiler_params=_params("parallel", "parallel"))(c_arr, g, recv)


def _add_chips(p, recv, sc_arr, tr, layer, into, name):
    _, h, C = p.shape
    nrb = h // tr
    assert h % tr == 0

    def body(sc_ref, p_ref, r_ref, *rest):
        rest[-1][...] = (p_ref[0].astype(f32) + r_ref[0].astype(f32)) + (r_ref[1].astype(f32) + r_ref[2].astype(f32))

    grid_spec = pltpu.PrefetchScalarGridSpec(
        num_scalar_prefetch=1, grid=(nrb,),
        in_specs=[pl.BlockSpec((1, tr, C), lambda r, sc_ref: (sc_ref[0], r, 0)),
                  pl.BlockSpec((3, tr, C), lambda r, sc_ref: (0, r, 0))] + ([] if into is None else [ANY]),
        out_specs=pl.BlockSpec((None, tr, C), lambda r, sc_ref: (layer, sc_ref[1] * nrb + r, 0)))
    return pl.pallas_call(body, grid_spec=grid_spec, out_shape=SDS((2, 2 * h, C), f32), name=name,
                          input_output_aliases={} if into is None else {3: 0},
                          compiler_params=_params("parallel"))(sc_arr, p, recv, *([] if into is None else [into]))


_BIG = (("w_in", 1024, 256), ("w_out", 256, 128), ("ffn_w_in", 1024, 256), ("ffn_w_out", 704, 352))
_SMALL = ("dn_conv_w", "ffn_conv_w", "ffn_conv_b", "norm_pre_mix", "norm_post_mix", "norm_pre_ffn", "norm_post_ffn",
          "dn_norm_w", "dn_a_log", "dn_dt_bias")
_WEIGHTS = ("w_in", "dn_conv_w", "dn_a_log", "dn_dt_bias", "dn_norm_w", "w_out", "ffn_w_in", "ffn_conv_w", "ffn_conv_b",
            "ffn_w_out", "norm_pre_mix", "norm_post_mix", "norm_pre_ffn", "norm_post_ffn")
_ADAM_ROWS = {"w_in": 512, "w_out": 256, "ffn_w_in": 256, "ffn_w_out": 352}


def _shard_major(name, g):
    if name == "w_in":
        width = IN_COLS // N_SHARDS
        return jnp.stack([g[:, width * s:width * (s + 1)] for s in range(N_SHARDS)])
    if name == "ffn_w_in":
        return g
    return g.reshape(4, g.shape[0] // 4, g.shape[1])


def kernel(x, w_in, dn_conv_w, dn_a_log, dn_dt_bias, dn_norm_w, w_out, ffn_w_in, ffn_conv_w, ffn_conv_b, ffn_w_out, norm_pre_mix, norm_post_mix, norm_pre_ffn, norm_post_ffn, loss_target, m_w_in, m_dn_conv_w, m_dn_a_log, m_dn_dt_bias, m_dn_norm_w, m_w_out, m_ffn_w_in, m_ffn_conv_w, m_ffn_conv_b, m_ffn_w_out, m_norm_pre_mix, m_norm_post_mix, m_norm_pre_ffn, m_norm_post_ffn, v_w_in, v_dn_conv_w, v_dn_a_log, v_dn_dt_bias, v_dn_norm_w, v_w_out, v_ffn_w_in, v_ffn_conv_w, v_ffn_conv_b, v_ffn_w_out, v_norm_pre_mix, v_norm_post_mix, v_norm_pre_ffn, v_norm_post_ffn):
    w = dict(w_in=w_in, dn_conv_w=dn_conv_w, dn_a_log=dn_a_log, dn_dt_bias=dn_dt_bias, dn_norm_w=dn_norm_w, w_out=w_out,
             ffn_w_in=ffn_w_in, ffn_conv_w=ffn_conv_w, ffn_conv_b=ffn_conv_b, ffn_w_out=ffn_w_out, norm_pre_mix=norm_pre_mix,
             norm_post_mix=norm_post_mix, norm_pre_ffn=norm_pre_ffn, norm_post_ffn=norm_post_ffn)
    m = dict(w_in=m_w_in, dn_conv_w=m_dn_conv_w, dn_a_log=m_dn_a_log, dn_dt_bias=m_dn_dt_bias, dn_norm_w=m_dn_norm_w,
             w_out=m_w_out, ffn_w_in=m_ffn_w_in, ffn_conv_w=m_ffn_conv_w, ffn_conv_b=m_ffn_conv_b, ffn_w_out=m_ffn_w_out,
             norm_pre_mix=m_norm_pre_mix, norm_post_mix=m_norm_post_mix, norm_pre_ffn=m_norm_pre_ffn,
             norm_post_ffn=m_norm_post_ffn)
    v = dict(w_in=v_w_in, dn_conv_w=v_dn_conv_w, dn_a_log=v_dn_a_log, dn_dt_bias=v_dn_dt_bias, dn_norm_w=v_dn_norm_w,
             w_out=v_w_out, ffn_w_in=v_ffn_w_in, ffn_conv_w=v_ffn_conv_w, ffn_conv_b=v_ffn_conv_b, ffn_w_out=v_ffn_w_out,
             norm_pre_mix=v_norm_pre_mix, norm_post_mix=v_norm_post_mix, norm_pre_ffn=v_norm_pre_ffn,
             norm_post_ffn=v_norm_post_ffn)
    xi, yi, ci = lax.axis_index("x"), lax.axis_index("y"), lax.axis_index("c")
    s_me = 2 * xi + yi
    c_arr = jnp.reshape(ci, (1,)).astype(jnp.int32)
    sc_arr = jnp.stack([s_me, ci]).astype(jnp.int32)

    mats = [name for name, _, _ in _BIG]
    rest = mats[1:]
    half_of = {name: rows // 2 for name, rows, _ in _BIG}
    tiles = {name: tr for name, _, tr in _BIG}
    gathered_shape = lambda a: SDS((4,) + a.shape, a.dtype)

    own = {k: w[k].astype(bf16) for k in mats}
    plan_in = _gather_plan([half_of["w_in"], None, None])
    src_in = [own["w_in"][0:1], dn_conv_w, ffn_conv_w]
    started_in = _split_copy(src_in, [gathered_shape(a) for a in src_in], plan_in, 3, src_in[0], "weights_gather_w_in0_start")
    plan0 = _gather_plan([half_of[k] for k in rest])
    src0 = [own[k][0:1] for k in rest]
    started0 = _split_copy(src0, [gathered_shape(a) for a in src0], plan0, 3, started_in[-1], "weights_gather_l0_start")
    plan1 = _gather_plan([half_of[k] for k in mats])
    src1 = [own[k][1:2] for k in mats]
    started1 = _split_copy(src1, [gathered_shape(a) for a in src1], plan1, 3, started0[-1], "weights_gather_l1_start")
    _, landed_in = _split_wait(started_in, len(src_in), plan_in, started1[-1], "weights_gather_w_in0_wait")
    pass_in = _pass_plan([half_of["w_in"]])
    passed_in = _split_copy(landed_in[:1], [], pass_in, 3, landed_in[0], "weights_pass_w_in0_start")
    got_in = list(_split_wait(passed_in, 1, pass_in, passed_in[-1], "weights_pass_w_in0_wait", nl=0)[0]) + list(landed_in[1:])

    def pick(mine, gathered):
        return [jnp.where(s_me == s, mine, gathered[s]) for s in range(4)]

    conv = {"dn_conv_w": jnp.concatenate(pick(dn_conv_w, got_in[1]), axis=-1),
            "ffn_conv_w": jnp.concatenate(pick(ffn_conv_w, got_in[2]), axis=-1)}
    lanes = lambda a: jnp.pad(a, ((0, 0), (0, 128 - a.shape[1])))
    vec = dict(dn_a_log=lanes(dn_a_log), dn_dt_bias=lanes(dn_dt_bias), dn_norm_w=dn_norm_w, ffn_conv_b=ffn_conv_b,
               norm_pre_mix=norm_pre_mix, norm_post_mix=norm_post_mix, norm_pre_ffn=norm_pre_ffn, norm_post_ffn=norm_post_ffn)

    def matrices(l, names, gathered):
        W = {}
        for k, a in zip(names, gathered):
            if k in ("w_out", "ffn_w_out"):
                rows_, cols = own[k].shape[1:]
                W[k] = lax.dynamic_update_slice(a[:, 0], own[k][l][None], (s_me, 0, 0)).reshape(4 * rows_, cols)
            else:
                cat = jnp.concatenate(pick(own[k][l], a[:, 0]), axis=-1)
                W[k] = jnp.pad(cat, ((0, 0), (0, PROJ_W - IN_COLS))) if k == "w_in" else cat
        return W

    def small_weights(l):
        return {**{k: a[l] for k, a in conv.items()}, **{k: a[l:l + 1] for k, a in vec.items()}}

    pass0, pass1 = _pass_plan([half_of[k] for k in rest]), _pass_plan([half_of[k] for k in mats])
    passing = {}

    def mid_mixer_l0(marker):
        _, landed = _split_wait(started0, len(rest), plan0, marker, "weights_gather_l0_wait")
        passing["l0"] = _split_copy(landed, [], pass0, 3, marker, "weights_pass_l0_start")
        return passing["l0"][-1][0, 0]

    def late_l0(mix_in):
        return matrices(0, rest, _split_wait(passing["l0"], len(rest), pass0, mix_in, "weights_pass_l0_wait", nl=0)[0])

    def after_ffn_in_l0(marker):
        _, landed = _split_wait(started1, len(mats), plan1, marker, "weights_gather_l1_wait")
        passing["l1"] = _split_copy(landed, [], pass1, 3, marker, "weights_pass_l1_start")
        return passing["l1"][-1][0, 0]

    cos, sgn_sin = _rope_tables(x.shape[1])
    W0 = {**small_weights(0), **matrices(0, ["w_in"], got_in[:1])}
    W0_first = dict(W0, norm_pre_mix=W0["norm_pre_mix"] + started1[-1][0, 0])
    x1, h1_l1, saved0 = _layer_fwd(x[0], W0_first, cos, sgn_sin, 0, late_weights=late_l0, next_pre_mix=norm_pre_mix[1:2],
                                   mid_mixer=mid_mixer_l0, after_ffn_in=after_ffn_in_l0)
    W1 = {**small_weights(1),
          **matrices(1, mats, _split_wait(passing["l1"], len(mats), pass1, x1, "weights_pass_l1_wait", nl=0)[0])}
    x2, _, saved1 = _layer_fwd(x1, W1, cos, sgn_sin, 1, h1=h1_l1)
    loss_local, dy = _loss_head(x2, loss_target[0], "loss_head")
    loss = lax.psum(loss_local, ("x", "y", "c"))

    def shard_major(names, grads_l):
        return [_shard_major(name, grads_l[name]) for name in names]

    def add_siblings(l, names, gs, from_sib):
        return [_add_sibling(g, r, c_arr, tiles[name], f"add_sibling_{name}{l}") for g, r, name in zip(gs, from_sib, names)]

    def scatter_start(l, names, parts, after, tag):
        return _split_copy(parts, [SDS((3,) + p.shape[1:], p.dtype) for p in parts], _scatter_plan, 3, after,
                           f"grads_l{l}{tag}_scatter_start")

    def owner_sums(l, names, sent, after, tag, into):
        parts, recvd = _split_wait(sent, len(names), _scatter_plan, after, f"grads_l{l}{tag}_scatter_wait")
        return {name: _add_chips(p, r, sc_arr, tiles[name], l, into.get(name), f"add_chips_{name}{l}")
                for p, r, name in zip(parts, recvd, names)}

    (dx1, dh1_l1), grads1, _ = _layer_bwd(dy, saved1, W1, cos, sgn_sin, 1, first_layer=False)
    gs1 = shard_major(mats, grads1)
    swap1 = _split_copy(gs1, [SDS((4, g.shape[1] // 2, g.shape[2]), g.dtype) for g in gs1], _exchange_plan, 1, dx1,
                        "grads_l1_sibling_start")
    ffn = ["ffn_w_in", "ffn_w_out"]
    launched = {}

    def after_ffn_l0(g_ffn, dx_mid):
        gs1_, from_sib1 = _split_wait(swap1, len(mats), _exchange_plan, dx_mid, "grads_l1_sibling_wait")
        launched["l1"] = scatter_start(1, mats, add_siblings(1, mats, gs1_, from_sib1), dx_mid, "")
        gs0 = shard_major(ffn, g_ffn)
        from_sib0 = _exchange_halves(gs0, "grads_l0_ffn_to_sibling")
        launched["l0_ffn"] = scatter_start(0, ffn, add_siblings(0, ffn, gs0, from_sib0), launched["l1"][-1], "_ffn")
        return launched["l0_ffn"][-1][0, 0]

    W0_last = dict(W0, **saved0["late"], norm_post_ffn=W0["norm_post_ffn"] + swap1[-1][0, 0])
    dx, grads0, grads1["norm_pre_mix"] = _layer_bwd(dx1, saved0, W0_last, cos, sgn_sin, 0, after_ffn=after_ffn_l0,
                                                    next_layer=(dh1_l1, norm_pre_mix[1:2]))
    mix = ["w_in", "w_out"]
    gs0 = shard_major(mix, grads0)
    part0 = add_siblings(0, mix, gs0, _exchange_halves(gs0, "grads_l0_mix_to_sibling"))
    sent0 = scatter_start(0, mix, part0, dx, "_mix")
    red = owner_sums(0, ffn, launched["l0_ffn"], sent0[-1], "_ffn", {})
    red = owner_sums(1, mats, launched["l1"], sent0[-1], "", red)
    joined = dict(zip(mats, _join_halves([red[k] for k in mats], [1 if k in mix else None for k in mats],
                                         "grads_join_early")))
    grads = [grads0, grads1]

    small = {}
    for name in _SMALL:
        per_layer = [grads[l][name] for l in range(2)]
        if name in ("dn_a_log", "dn_dt_bias"):
            per_layer = [p[:, :N_HEADS_D] for p in per_layer]
        small[name] = jnp.stack(per_layer).reshape((2,) + (w[name].shape[1:] if name not in ("dn_conv_w", "ffn_conv_w")
                                                           else per_layer[0].shape))
    flat = jnp.concatenate([small[name].reshape(-1) for name in _SMALL])
    n_rows = -(-flat.shape[0] // 1024) * 8
    summed = _all_reduce_small(jnp.pad(flat, (0, n_rows * 128 - flat.shape[0])).reshape(n_rows, 128),
                               "small_grads_all_reduce").reshape(-1)
    off = 0
    g_out = {}
    for name in _SMALL:
        size = small[name].size
        g_out[name] = summed[off:off + size].reshape(small[name].shape)
        off += size
    for k in ("dn_conv_w", "ffn_conv_w"):
        width = w[k].shape[2]
        g_out[k] = lax.dynamic_slice_in_dim(g_out[k], s_me * width, width, axis=2)
    for k in ffn:
        g_out[k] = joined[k]

    deltas, new_m, new_v = {}, {}, {}

    def step(name):
        shape = w[name].shape
        as3 = (lambda a: a) if len(shape) == 3 else (lambda a: a.reshape(shape[0], 1, shape[1]))
        tr = _ADAM_ROWS.get(name, as3(w[name]).shape[1])
        d_, m_, v_ = _adamw(as3(w[name]), as3(g_out[name]), as3(m[name]), as3(v[name]), tr, f"adamw_{name}")
        deltas[name], new_m[name], new_v[name] = d_.reshape(shape), m_.reshape(shape), v_.reshape(shape)

    for name in ffn:
        step(name)
    tiny = [name for name in _WEIGHTS if name not in mats]
    stepped = _adamw_small(*[[d[name] for name in tiny] for d in (w, g_out, m, v)], "adamw_small")
    for out, vals in zip((deltas, new_m, new_v), stepped):
        out.update(zip(tiny, vals))
    done = jnp.reshape(deltas["ffn_w_in"][0, 0, 0] + deltas["ffn_w_out"][0, 0, 0] + deltas["norm_post_ffn"][0, 0], (1,))
    red = owner_sums(0, mix, sent0, done, "_mix", joined)
    for k, a in zip(mix, _join_halves([red[k] for k in mix], [0] * len(mix), "grads_join_late")):
        g_out[k] = a
        step(k)

    return (loss, dx[None], *[g_out[k] for k in _WEIGHTS], *[deltas[k] for k in _WEIGHTS],
            *[new_m[k] for k in _WEIGHTS], *[new_v[k] for k in _WEIGHTS])
```

```python
import jax
import jax.numpy as jnp
from jax import lax
from jax.experimental import pallas as pl
from jax.experimental.pallas import tpu as pltpu

f32, bf16 = jnp.float32, jnp.bfloat16
SDS = jax.ShapeDtypeStruct
HI = lax.Precision.HIGH
MESH = pl.DeviceIdType.MESH
ANY = pl.BlockSpec(memory_space=pl.ANY)

D_MODEL = 1024
N_HEADS_A, HEAD_DIM = 8, 64
ATTN_W = 512
N_HEADS_D, DK = 4, 128
CHUNK = 64
D_FF = 2816
IN_COLS = 3592
PROJ_W = 3840
DN_QKV_COL = 3 * ATTN_W
DN_Z_COL = DN_QKV_COL + 3 * N_HEADS_D * DK
DN_BA_COL = DN_Z_COL + N_HEADS_D * DK
N_SHARDS = 4
BRANCHES = ((1, 16), (4, 4), (16, 1))
EPS = 1e-6
NEG = -1e30
ROW_TILE = 512
VMEM_LIMIT = 56 * 1024 * 1024

ADAM_LR, ADAM_B1, ADAM_B2, ADAM_EPS, ADAM_WD, ADAM_STEP = 0.001, 0.9, 0.999, 1e-08, 0.01, 10


def _params(*sem):
    return pltpu.CompilerParams(dimension_semantics=sem, vmem_limit_bytes=VMEM_LIMIT)


def _mm(a, b, mode, tm, tn, out_dtype, name, column_shards=False):
    if mode == "nn":
        (M, K), N = a.shape, b.shape[1]
        dims = (((1,), (0,)), ((), ()))
        a_spec = pl.BlockSpec((tm, K), lambda i, j: (i, 0))
        b_spec = pl.BlockSpec((K, tn), lambda i, j: (0, j))
    elif mode == "nt":
        (M, K), N = a.shape, b.shape[0]
        dims = (((1,), (1,)), ((), ()))
        a_spec = pl.BlockSpec((tm, K), lambda i, j: (i, 0))
        b_spec = pl.BlockSpec((tn, K), lambda i, j: (j, 0))
    else:
        (K, M), N = a.shape, b.shape[1]
        dims = (((0,), (0,)), ((), ()))
        a_spec = pl.BlockSpec((K, tm), lambda i, j: (0, i))
        b_spec = pl.BlockSpec((K, tn), lambda i, j: (0, j))
    assert M % tm == 0 and N % tn == 0, (name, M, N, tm, tn)

    def body(a_ref, b_ref, o_ref):
        o_ref[...] = lax.dot_general(a_ref[...].astype(bf16), b_ref[...].astype(bf16), dims,
                                     preferred_element_type=f32).astype(o_ref.dtype)

    if column_shards:
        out_spec, out_shape = pl.BlockSpec((None, tm, tn), lambda i, j: (j, i, 0)), SDS((N // tn, M, tn), out_dtype)
    else:
        out_spec, out_shape = pl.BlockSpec((tm, tn), lambda i, j: (i, j)), SDS((M, N), out_dtype)
    return pl.pallas_call(body, grid=(M // tm, N // tn), in_specs=[a_spec, b_spec], out_specs=out_spec,
                          out_shape=out_shape, name=name, compiler_params=_params("parallel", "arbitrary"))(a, b)


def _row_spec(r, tm):
    if isinstance(r, tuple):
        arr, width, cb = r
        return arr, pl.BlockSpec((tm, width), lambda i, j, cb=cb: (i, cb + j))
    return r, pl.BlockSpec((tm, r.shape[1]), lambda i, j: (i, j))


def _full_spec(p):
    return pl.BlockSpec(p.shape, lambda i, j: (0,) * p.ndim)


def _rows(fn, rows, params, outs, name, tm=ROW_TILE, ncol=1):
    arrs, specs = zip(*[_row_spec(r, tm) for r in rows])
    S = arrs[0].shape[0]
    nr, npar = len(rows), len(params)

    def body(*refs):
        vals = fn(*[r[...].astype(f32) for r in refs[:nr]], *[p[...] for p in refs[nr:nr + npar]])
        for o_ref, v in zip(refs[nr + npar:], vals):
            o_ref[...] = v.astype(o_ref.dtype)

    return pl.pallas_call(
        body, grid=(S // tm, ncol), in_specs=list(specs) + [_full_spec(p) for p in params],
        out_specs=[pl.BlockSpec((tm, w), lambda i, j: (i, j)) for w, _ in outs],
        out_shape=[SDS((S, w * ncol), dt) for w, dt in outs], name=name,
        compiler_params=_params("parallel", "parallel"))(*arrs, *params)


def _rows_vjp(fn, rows, params, cts, wrt_rows, wrt_params, name, adds=None, tm=ROW_TILE, ncol=1, row_dtype=f32):
    adds = adds or {}
    arrs, specs = zip(*[_row_spec(r, tm) for r in rows])
    carrs, cspecs = zip(*[_row_spec(c, tm) for c in cts])
    add_keys = sorted(adds)
    aarrs = [adds[k] for k in add_keys]
    S = arrs[0].shape[0]
    nr, npar, nc, na = len(rows), len(params), len(cts), len(aarrs)
    widths = [specs[k].block_shape[1] for k in wrt_rows]
    row_dtypes = row_dtype if isinstance(row_dtype, (list, tuple)) else [row_dtype] * len(wrt_rows)

    def body(*refs):
        first = jnp.logical_and(pl.program_id(0) == 0, pl.program_id(1) == 0)
        rv = [r[...].astype(f32) for r in refs[:nr]]
        pv = [p[...] for p in refs[nr:nr + npar]]
        cv = tuple(c[...].astype(f32) for c in refs[nr + npar:nr + npar + nc])
        av = dict(zip(add_keys, refs[nr + npar + nc:nr + npar + nc + na]))
        o = refs[nr + npar + nc + na:]
        _, vjp = jax.vjp(fn, *rv, *pv)
        g = vjp(cv)
        for n, k in enumerate(wrt_rows):
            val = g[k]
            if k in av:
                val = val + av[k][...]
            o[n][...] = val.astype(o[n].dtype)
        for n, k in enumerate(wrt_params):
            ref = o[len(wrt_rows) + n]

            @pl.when(first)
            def _(ref=ref):
                ref[...] = jnp.zeros_like(ref)

            ref[...] += g[nr + k]

    res = pl.pallas_call(
        body, grid=(S // tm, ncol),
        in_specs=list(specs) + [_full_spec(p) for p in params] + list(cspecs)
        + [pl.BlockSpec((tm, a.shape[1] // ncol), lambda i, j: (i, j)) for a in aarrs],
        out_specs=[pl.BlockSpec((tm, w), lambda i, j: (i, j)) for w in widths] + [_full_spec(params[k]) for k in wrt_params],
        out_shape=[SDS((S, w * ncol), dt) for w, dt in zip(widths, row_dtypes)]
        + [SDS(params[k].shape, f32) for k in wrt_params],
        name=name, compiler_params=_params("arbitrary", "arbitrary"))(*arrs, *params, *carrs, *aarrs)
    return res[:len(wrt_rows)], res[len(wrt_rows):]


def _rms(x, w):
    return x * lax.rsqrt(jnp.mean(x * x, axis=-1, keepdims=True) + EPS) * w


def _rms_fn(x, w):
    return (_rms(x, w),)


def _res_rms_fn(f, res, w):
    return (res + _rms(f, w),)


def _res_rms_pre_fn(f, res, w_post, w_pre):
    x1 = res + _rms(f, w_post)
    return x1, _rms(x1, w_pre)


def _swap_halves(x):
    lane = lax.broadcasted_iota(jnp.int32, x.shape, 1)
    first = (lane % HEAD_DIM) < (HEAD_DIM // 2)
    n = x.shape[1]
    return jnp.where(first, pltpu.roll(x, n - HEAD_DIM // 2, 1), pltpu.roll(x, HEAD_DIM // 2, 1))


def _rope_fwd_fn(q, k, cos, sgn_sin):
    scale = HEAD_DIM ** -0.5
    return ((q * cos + _swap_halves(q) * sgn_sin) * scale, k * cos + _swap_halves(k) * sgn_sin)


def _rope_bwd_fn(dq, dk, cos, sgn_sin):
    dq = dq * (HEAD_DIM ** -0.5)
    return (dq * cos + _swap_halves(dq * sgn_sin), dk * cos + _swap_halves(dk * sgn_sin))


def _nt(a, b):
    return lax.dot_general(a, b, (((1,), (1,)), ((), ())), preferred_element_type=f32)


def _tn(a, b):
    return lax.dot_general(a, b, (((0,), (0,)), ((), ())), preferred_element_type=f32)


def _band_rows(j, d, nb):
    r, i = j // nb, j % nb
    if d == 1:
        cur = pl.ds(pl.multiple_of(i * 128, 128), 128)
        prev = pl.ds(pl.multiple_of(jnp.maximum(i - 1, 0) * 128, 128), 128)
    else:
        cur = pl.ds(i * (128 * d) + r, 128, stride=d)
        prev = pl.ds(jnp.maximum(i - 1, 0) * (128 * d) + r, 128, stride=d)
    return cur, prev, (i == 0).astype(jnp.int32)


def _band_bias(bias_ref):
    a = lax.broadcasted_iota(jnp.int32, (256, 256), 0) % 128
    c = lax.broadcasted_iota(jnp.int32, (256, 256), 1)
    own = jnp.logical_and(c < 128, c <= a)
    before = jnp.logical_and(c >= 128, c - 128 >= a)
    bias_ref[0] = jnp.where(jnp.logical_or(own, before), 0.0, NEG)
    bias_ref[1] = jnp.where(own, 0.0, NEG)


def _stack_heads(x, head_a):
    return jnp.concatenate([jnp.where(head_a, x, 0.0), jnp.where(head_a, 0.0, x)], axis=0)


def _unstack_heads(x2, head_a):
    return jnp.where(head_a, x2[:128], x2[128:])


def _pair_at(S, first_col):
    return pl.BlockSpec((S, 128), lambda h: (0, first_col // 128 + h))


def _attn_fwd(proj, cos, sgn_sin, name):
    S = proj.shape[0]
    nblk = S // 128

    def body(qp_ref, kp_ref, v_ref, cos_ref, sin_ref, out_ref, lse_ref, bias_ref, q_ref, k_ref, *scr):
        head_a = lax.broadcasted_iota(jnp.int32, (1, 128), 1) < HEAD_DIM
        _band_bias(bias_ref)
        q_ref[...], k_ref[...] = _rope_fwd_fn(qp_ref[...], kp_ref[...], cos_ref[...], sin_ref[...])
        for b, (d, nb) in enumerate(BRANCHES):
            ob_ref, lb_ref = scr[2 * b], scr[2 * b + 1]

            def blk(j, carry, d=d, nb=nb, ob_ref=ob_ref, lb_ref=lb_ref):
                cur, prev, first = _band_rows(j, d, nb)
                q2 = _stack_heads(q_ref[cur, :], head_a).astype(bf16)
                if nb == 1:
                    k2, v2, bias = k_ref[cur, :].astype(bf16), v_ref[cur, :].astype(bf16), bias_ref[1][:, :128]
                else:
                    k2 = jnp.concatenate([k_ref[cur, :], k_ref[prev, :]], axis=0).astype(bf16)
                    v2 = jnp.concatenate([v_ref[cur, :], v_ref[prev, :]], axis=0).astype(bf16)
                    bias = bias_ref[first]
                s = _nt(q2, k2) + bias
                mx = jnp.max(s, axis=1, keepdims=True)
                p = jnp.exp(s - mx)
                l = jnp.sum(p, axis=1, keepdims=True)
                o = jnp.dot(p.astype(bf16), v2, preferred_element_type=f32) / l
                ob_ref[cur, :] = _unstack_heads(o, head_a)
                lb_ref[cur, :] = _unstack_heads(jnp.broadcast_to(mx + jnp.log(l), (256, 128)), head_a)
                return carry

            lax.fori_loop(0, nblk, blk, 0, unroll=16)
        l0, l1, l2 = scr[1][...], scr[3][...], scr[5][...]
        mx = jnp.maximum(jnp.maximum(l0, l1), l2)
        e0, e1, e2 = jnp.exp(l0 - mx), jnp.exp(l1 - mx), jnp.exp(l2 - mx)
        den = e0 + e1 + e2
        out_ref[...] = ((e0 * scr[0][...] + e1 * scr[2][...] + e2 * scr[4][...]) / den).astype(out_ref.dtype)
        lse_ref[...] = mx + jnp.log(den)

    pair = pl.BlockSpec((S, 128), lambda h: (0, h))
    return pl.pallas_call(
        body, grid=(N_HEADS_A // 2,),
        in_specs=[pair, _pair_at(S, ATTN_W), _pair_at(S, 2 * ATTN_W), pair, pair], out_specs=[pair, pair],
        out_shape=[SDS((S, ATTN_W), bf16), SDS((S, ATTN_W), f32)],
        scratch_shapes=[pltpu.VMEM((2, 256, 256), f32)] + [pltpu.VMEM((S, 128), f32)] * 8,
        name=name, compiler_params=_params("parallel"))(proj, proj, proj, cos, sgn_sin)


def _attn_bwd(proj, cos, sgn_sin, dmix_in, out, lse, name):
    S = proj.shape[0]
    nblk = S // 128

    def body(qp_ref, kp_ref, v_ref, cos_ref, sin_ref, do_ref, out_ref, lse_ref, dq_ref, dk_ref, dv_ref,
             bias_ref, t_ref, q_ref, k_ref):
        head_a = lax.broadcasted_iota(jnp.int32, (1, 128), 1) < HEAD_DIM
        _band_bias(bias_ref)
        q_ref[...], k_ref[...] = _rope_fwd_fn(qp_ref[...], kp_ref[...], cos_ref[...], sin_ref[...])
        x = do_ref[...] * out_ref[...].astype(f32)
        t_ref[...] = jnp.where(head_a, jnp.sum(jnp.where(head_a, x, 0.0), axis=1, keepdims=True),
                               jnp.sum(jnp.where(head_a, 0.0, x), axis=1, keepdims=True))
        dq_ref[...] = jnp.zeros_like(dq_ref)
        dk_ref[...] = jnp.zeros_like(dk_ref)
        dv_ref[...] = jnp.zeros_like(dv_ref)
        for d, nb in BRANCHES:
            def blk(j, carry, d=d, nb=nb):
                cur, prev, first = _band_rows(j, d, nb)
                q2 = _stack_heads(q_ref[cur, :], head_a).astype(bf16)
                do2 = _stack_heads(do_ref[cur, :], head_a).astype(bf16)
                t, lse_b = t_ref[cur, :], lse_ref[cur, :]
                t2 = jnp.concatenate([t[:, :1], t[:, HEAD_DIM:HEAD_DIM + 1]], axis=0)
                lse2 = jnp.concatenate([lse_b[:, :1], lse_b[:, HEAD_DIM:HEAD_DIM + 1]], axis=0)
                if nb == 1:
                    k2, v2, bias = k_ref[cur, :].astype(bf16), v_ref[cur, :].astype(bf16), bias_ref[1][:, :128]
                else:
                    k2 = jnp.concatenate([k_ref[cur, :], k_ref[prev, :]], axis=0).astype(bf16)
                    v2 = jnp.concatenate([v_ref[cur, :], v_ref[prev, :]], axis=0).astype(bf16)
                    bias = bias_ref[first]
                p = jnp.exp(_nt(q2, k2) + bias - lse2)
                ds = (p * (_nt(do2, v2) - t2)).astype(bf16)
                dq_ref[cur, :] += _unstack_heads(jnp.dot(ds, k2, preferred_element_type=f32), head_a)
                dk2, dv2 = _tn(ds, q2), _tn(p.astype(bf16), do2)
                dk_ref[cur, :] += dk2[:128]
                dv_ref[cur, :] += dv2[:128]
                if nb != 1:
                    dk_ref[prev, :] += dk2[128:]
                    dv_ref[prev, :] += dv2[128:]
                return carry

            lax.fori_loop(0, nblk, blk, 0, unroll=16)
        dq_ref[...], dk_ref[...] = _rope_bwd_fn(dq_ref[...], dk_ref[...], cos_ref[...], sin_ref[...])

    pair = pl.BlockSpec((S, 128), lambda h: (0, h))
    return pl.pallas_call(
        body, grid=(N_HEADS_A // 2,),
        in_specs=[pair, _pair_at(S, ATTN_W), _pair_at(S, 2 * ATTN_W), pair, pair, pair, pair, pair],
        out_specs=[pair] * 3, out_shape=[SDS((S, ATTN_W), f32)] * 3,
        scratch_shapes=[pltpu.VMEM((2, 256, 256), f32)] + [pltpu.VMEM((S, 128), f32)] * 3,
        name=name, compiler_params=_params("parallel"))(proj, proj, proj, cos, sgn_sin, dmix_in, out, lse)


def _conv_val(x, w, K, rows):
    acc = x * w[K - 1:K, :]
    for s in range(1, K):
        acc = acc + jnp.where(rows >= s, pltpu.roll(x, s, 0), 0.0) * w[K - 1 - s:K - s, :]
    return acc


def _colconv_fwd(xs, ws, bs, K, fn, nblk, tc, outs, name):
    S = xs[0][0].shape[0]
    n = len(xs)
    has_b = bs is not None

    def body(*refs):
        rows = lax.broadcasted_iota(jnp.int32, (S, tc), 0)
        cs = []
        for k in range(n):
            c = _conv_val(refs[k][...].astype(f32), refs[n + k][...], K, rows)
            if has_b:
                c = c + refs[2 * n + k][...]
            cs.append(c)
        for o_ref, val in zip(refs[(3 if has_b else 2) * n:], fn(*cs)):
            o_ref[...] = val.astype(o_ref.dtype)

    def cspec(rows_, cb0):
        return pl.BlockSpec((rows_, tc), lambda j, cb0=cb0: (0, cb0 + j))

    in_specs = [cspec(S, cb) for _, cb in xs] + [cspec(K, cb) for _, cb in ws]
    args = [a for a, _ in xs] + [a for a, _ in ws]
    if has_b:
        in_specs += [cspec(1, cb) for _, cb in bs]
        args += [a for a, _ in bs]
    return pl.pallas_call(
        body, grid=(nblk,), in_specs=in_specs, out_specs=[cspec(S, 0) for _ in outs],
        out_shape=[SDS((S, nblk * tc), dt) for dt in outs], name=name, compiler_params=_params("parallel"))(*args)


def _colconv_bwd(xs, ws, bs, K, fn, douts, nblk, tc, name, dx_dtype=f32):
    S = xs[0][0].shape[0]
    n, nd = len(xs), len(douts)
    has_b = bs is not None
    nin = (3 if has_b else 2) * n

    def body(*refs):
        rows = lax.broadcasted_iota(jnp.int32, (S, tc), 0)
        x = [refs[k][...].astype(f32) for k in range(n)]
        w = [refs[n + k][...] for k in range(n)]
        cs = []
        for k in range(n):
            c = _conv_val(x[k], w[k], K, rows)
            if has_b:
                c = c + refs[2 * n + k][...]
            cs.append(c)
        _, vjp = jax.vjp(fn, *cs)
        dcs = vjp(tuple(r[...].astype(f32) for r in refs[nin:nin + nd]))
        o = refs[nin + nd:]
        for k in range(n):
            dc = dcs[k]
            dx = dc * w[k][K - 1:K, :]
            o[n + k][K - 1:K, :] = jnp.sum(dc * x[k], axis=0, keepdims=True)
            for s in range(1, K):
                dx = dx + jnp.where(rows < S - s, pltpu.roll(dc, S - s, 0), 0.0) * w[k][K - 1 - s:K - s, :]
                xsh = jnp.where(rows >= s, pltpu.roll(x[k], s, 0), 0.0)
                o[n + k][K - 1 - s:K - s, :] = jnp.sum(dc * xsh, axis=0, keepdims=True)
            o[k][...] = dx.astype(o[k].dtype)
            if has_b:
                o[2 * n + k][...] = jnp.sum(dc, axis=0, keepdims=True)

    def cspec(rows_, cb0):
        return pl.BlockSpec((rows_, tc), lambda j, cb0=cb0: (0, cb0 + j))

    in_specs = [cspec(S, cb) for _, cb in xs] + [cspec(K, cb) for _, cb in ws]
    args = [a for a, _ in xs] + [a for a, _ in ws]
    if has_b:
        in_specs += [cspec(1, cb) for _, cb in bs]
        args += [a for a, _ in bs]
    in_specs += [cspec(S, 0) for _ in douts]
    args += list(douts)
    W = nblk * tc
    out_specs = [cspec(S, 0)] * n + [cspec(K, 0)] * n + ([cspec(1, 0)] * n if has_b else [])
    out_shape = [SDS((S, W), dx_dtype)] * n + [SDS((K, W), f32)] * n + ([SDS((1, W), f32)] * n if has_b else [])
    res = pl.pallas_call(body, grid=(nblk,), in_specs=in_specs, out_specs=out_specs, out_shape=out_shape,
                         name=name, compiler_params=_params("parallel"))(*args)
    return res[:n], res[n:2 * n], res[2 * n:]


def _silu_fn(c):
    return (c * jax.nn.sigmoid(c),)


_GELU_C, _GELU_A = 0.7978845608028654, 0.044715


@jax.custom_vjp
def _geglu(gate, up):
    return 0.5 * gate * (1.0 + jnp.tanh(_GELU_C * (gate + _GELU_A * gate * gate * gate))) * up


def _geglu_vjp_fwd(gate, up):
    return _geglu(gate, up), (gate, up)


def _geglu_vjp_bwd(res, d):
    gate, up = res
    g2 = gate * gate
    t = jnp.tanh(_GELU_C * gate * (1.0 + _GELU_A * g2))
    h = 0.5 * (1.0 + t)
    dgelu = h + (0.5 * _GELU_C) * gate * (1.0 - t * t) * (1.0 + (3.0 * _GELU_A) * g2)
    return d * up * dgelu, d * (gate * h)


_geglu.defvjp(_geglu_vjp_fwd, _geglu_vjp_bwd)


def _geglu_fn(gate, up):
    return (_geglu(gate, up),)


def _softplus(x):
    u = jnp.exp(jnp.minimum(x, 20.0))
    small = u * (1.0 - 0.5 * u)
    return jnp.where(x > 20.0, x, jnp.where(u < 1e-4, small, jnp.log(1.0 + u)))


def _bmm(a, b, precision=None):
    return lax.dot_general(a, b, (((2,), (1,)), ((0,), (0,))), precision=precision, preferred_element_type=f32)


def _bnt(a, b, precision=None):
    return lax.dot_general(a, b, (((2,), (2,)), ((0,), (0,))), precision=precision, preferred_element_type=f32)


def _btn(a, b, precision=None):
    return lax.dot_general(a, b, (((1,), (1,)), ((0,), (0,))), precision=precision, preferred_element_type=f32)


@jax.custom_vjp
def _unit_lower_inverse(A):
    n = A.shape[-1]
    eye = (lax.broadcasted_iota(jnp.int32, (1, n, n), 1) == lax.broadcasted_iota(jnp.int32, (1, n, n), 2)).astype(f32)
    P = -A
    T = eye + P
    for _ in range(5):
        P = _bmm(P, P, HI)
        T = T + _bmm(T, P, HI)
    return T


def _unit_lower_inverse_fwd(A):
    T = _unit_lower_inverse(A)
    return T, T


def _unit_lower_inverse_bwd(T, dT):
    return (-_btn(T, _bnt(dT, T, HI), HI),)


_unit_lower_inverse.defvjp(_unit_lower_inverse_fwd, _unit_lower_inverse_bwd)


def _dn_prep_fn(q, k, v, ba, alog, dtb, h):
    G, C = q.shape[0], CHUNK
    lane = lax.broadcasted_iota(jnp.int32, (1, 1, 128), 2)

    def sel(arr, idx):
        return jnp.sum(jnp.where(lane == idx, arr, 0.0), axis=-1, keepdims=True)

    beta = jax.nn.sigmoid(sel(ba, h))
    g = -jnp.exp(sel(alog[None], h)) * _softplus(sel(ba, N_HEADS_D + h) + sel(dtb[None], h))
    qn = q * lax.rsqrt(jnp.sum(q * q, axis=-1, keepdims=True) + EPS) * (DK ** -0.5)
    kn = k * lax.rsqrt(jnp.sum(k * k, axis=-1, keepdims=True) + EPS)
    ii = lax.broadcasted_iota(jnp.int32, (1, C, C), 1)
    jj = lax.broadcasted_iota(jnp.int32, (1, C, C), 2)
    tril, strict = ii >= jj, ii > jj
    gsq = jnp.broadcast_to(g, (G, C, C))
    gcol = _bmm(jnp.broadcast_to(tril.astype(f32), (G, C, C)), gsq, HI)
    grow = _bmm(jnp.ones((G, C, C), f32), jnp.where(ii <= jj, gsq, 0.0), HI)
    decay = jnp.exp(jnp.where(tril, gcol - grow, NEG))
    gc = gcol[:, :, :1]
    glast = gcol[:, C - 1:C, :1]
    kb = kn * beta
    A = jnp.where(strict, _bnt(kb.astype(bf16), kn.astype(bf16)) * decay, 0.0)
    T = _unit_lower_inverse(A).astype(bf16)
    u = _bmm(T, (v * beta).astype(bf16))
    w = _bmm(T, (kb * jnp.exp(gc)).astype(bf16))
    qk = _bnt(qn.astype(bf16), kn.astype(bf16)) * decay
    qd = qn * jnp.exp(gc)
    kd = kn * jnp.exp(glast - gc)
    return u, w, qk, qd, kd, jnp.broadcast_to(jnp.exp(glast), (G, C, DK))


def _dn_scan_fn(u, w, qk, qd, kd, eg, St):
    b = lambda a: a.astype(bf16)
    vnew = u - _bmm(b(w), b(St))
    o = _bmm(b(qd), b(St)) + _bmm(b(qk), b(vnew))
    return o, St * eg[:, :1, :] + _btn(b(kd), b(vnew))


def _dn_post_fn(o, z, nw):
    return (_rms(o, nw) * (z * jax.nn.sigmoid(z)),)


DN_GROUP = 16


def _dn_prep_specs(S, rows):
    def col(first):
        return pl.BlockSpec((rows, DK), lambda i, h, first=first: (i, first // DK + h))

    par = pl.BlockSpec((1, 128), lambda i, h: (0, 0))
    return [col(0), col(N_HEADS_D * DK), col(2 * N_HEADS_D * DK),
            pl.BlockSpec((rows, 128), lambda i, h: (i, DN_BA_COL // 128)), par, par]


def _dn_prep(qkv, proj, alog, dtb, name):
    S = qkv.shape[0]
    G = DN_GROUP
    rows = G * CHUNK

    def body(q_ref, k_ref, v_ref, ba_ref, al_ref, dt_ref, u_ref, w_ref, qk_ref, qd_ref, kd_ref, eg_ref):
        h = pl.program_id(1)
        r3 = lambda ref: ref[...].reshape(G, CHUNK, 128)
        u, w, qk, qd, kd, eg = _dn_prep_fn(r3(q_ref), r3(k_ref), r3(v_ref), r3(ba_ref), al_ref[...], dt_ref[...], h)
        for ref, val in ((u_ref, u), (w_ref, w), (qd_ref, qd), (kd_ref, kd), (eg_ref, eg)):
            ref[...] = val.reshape(rows, DK)
        qk_ref[:, :CHUNK] = qk.reshape(rows, CHUNK)
        qk_ref[:, CHUNK:] = jnp.zeros((rows, DK - CHUNK), f32)

    out = pl.BlockSpec((rows, DK), lambda i, h: (i, h))
    return pl.pallas_call(
        body, grid=(S // rows, N_HEADS_D), in_specs=_dn_prep_specs(S, rows), out_specs=[out] * 6,
        out_shape=[SDS((S, N_HEADS_D * DK), f32)] * 6, name=name,
        compiler_params=_params("parallel", "parallel"))(qkv, qkv, qkv, proj, alog, dtb)


def _dn_prep_bwd(qkv, proj, alog, dtb, cts, name):
    S = qkv.shape[0]
    G = DN_GROUP
    rows = G * CHUNK

    def body(q_ref, k_ref, v_ref, ba_ref, al_ref, dt_ref, du_ref, dw_ref, dqk_ref, dqd_ref, dkd_ref, deg_ref,
             dq_ref, dk_ref, dv_ref, dba_ref, dal_ref, ddt_ref):
        i, h = pl.program_id(0), pl.program_id(1)
        r3 = lambda ref: ref[...].reshape(G, CHUNK, 128)
        _, vjp = jax.vjp(lambda q, k, v, ba, al, dt: _dn_prep_fn(q, k, v, ba, al, dt, h),
                         r3(q_ref), r3(k_ref), r3(v_ref), r3(ba_ref), al_ref[...], dt_ref[...])
        dqk = dqk_ref[:, :CHUNK].reshape(G, CHUNK, CHUNK)
        dq, dk, dv, dba, dal, ddt = vjp((r3(du_ref), r3(dw_ref), dqk, r3(dqd_ref), r3(dkd_ref), r3(deg_ref)))
        dq_ref[...] = dq.reshape(rows, DK)
        dk_ref[...] = dk.reshape(rows, DK)
        dv_ref[...] = dv.reshape(rows, DK)

        @pl.when(h == 0)
        def _():
            dba_ref[...] = jnp.zeros_like(dba_ref)

        @pl.when(jnp.logical_and(i == 0, h == 0))
        def _():
            dal_ref[...] = jnp.zeros_like(dal_ref)
            ddt_ref[...] = jnp.zeros_like(ddt_ref)

        dba_ref[...] += dba.reshape(rows, 128)
        dal_ref[...] += dal
        ddt_ref[...] += ddt

    hcol = pl.BlockSpec((rows, DK), lambda i, h: (i, h))
    par = pl.BlockSpec((1, 128), lambda i, h: (0, 0))
    W = N_HEADS_D * DK
    return pl.pallas_call(
        body, grid=(S // rows, N_HEADS_D), in_specs=_dn_prep_specs(S, rows) + [hcol] * 6,
        out_specs=[hcol] * 3 + [pl.BlockSpec((rows, 128), lambda i, h: (i, 0)), par, par],
        out_shape=[SDS((S, W), f32)] * 3 + [SDS((S, 128), f32), SDS((1, 128), f32), SDS((1, 128), f32)], name=name,
        compiler_params=_params("arbitrary", "arbitrary"))(qkv, qkv, qkv, proj, alog, dtb, *cts)


def _heads(x):
    return jnp.stack([x[:, DK * h:DK * (h + 1)] for h in range(N_HEADS_D)])


SCAN_CHUNKS = 8


def _dn_scan(pre, name):
    S = pre[0].shape[0]
    NCH = S // CHUNK
    rows = SCAN_CHUNKS * CHUNK

    def body(u_ref, w_ref, qk_ref, qd_ref, kd_ref, eg_ref, o_ref, st_ref, s_ref):
        @pl.when(pl.program_id(0) == 0)
        def _():
            s_ref[...] = jnp.zeros_like(s_ref)

        St = s_ref[...]
        for k in range(SCAN_CHUNKS):
            r = slice(k * CHUNK, (k + 1) * CHUNK)
            st_ref[k] = St
            o, St = _dn_scan_fn(_heads(u_ref[r, :]), _heads(w_ref[r, :]), _heads(qk_ref[r, :])[:, :, :CHUNK],
                                _heads(qd_ref[r, :]), _heads(kd_ref[r, :]), _heads(eg_ref[r, :]), St)
            for h in range(N_HEADS_D):
                o_ref[r, DK * h:DK * (h + 1)] = o[h]
        s_ref[...] = St

    blk = pl.BlockSpec((rows, N_HEADS_D * DK), lambda n: (n, 0))
    return pl.pallas_call(
        body, grid=(S // rows,), in_specs=[blk] * 6,
        out_specs=[blk, pl.BlockSpec((SCAN_CHUNKS, N_HEADS_D, DK, DK), lambda n: (n, 0, 0, 0))],
        out_shape=[SDS((S, N_HEADS_D * DK), f32), SDS((NCH, N_HEADS_D, DK, DK), f32)],
        scratch_shapes=[pltpu.VMEM((N_HEADS_D, DK, DK), f32)], name=name, compiler_params=_params("arbitrary"))(*pre)


def _dn_scan_bwd(pre, states, do, name):
    S = do.shape[0]
    rows = SCAN_CHUNKS * CHUNK
    steps = S // rows

    def body(u_ref, w_ref, qk_ref, qd_ref, kd_ref, eg_ref, st_ref, do_ref,
             du_ref, dw_ref, dqk_ref, dqd_ref, dkd_ref, deg_ref, ds_ref):
        @pl.when(pl.program_id(0) == 0)
        def _():
            ds_ref[...] = jnp.zeros_like(ds_ref)

        dS = ds_ref[...]
        for k in reversed(range(SCAN_CHUNKS)):
            r = slice(k * CHUNK, (k + 1) * CHUNK)
            _, vjp = jax.vjp(_dn_scan_fn, _heads(u_ref[r, :]), _heads(w_ref[r, :]), _heads(qk_ref[r, :])[:, :, :CHUNK],
                             _heads(qd_ref[r, :]), _heads(kd_ref[r, :]), _heads(eg_ref[r, :]), st_ref[k])
            du, dw, dqk, dqd, dkd, deg, dS = vjp((_heads(do_ref[r, :]), dS))
            for h in range(N_HEADS_D):
                c = slice(DK * h, DK * (h + 1))
                for ref, val in ((du_ref, du), (dw_ref, dw), (dqd_ref, dqd), (dkd_ref, dkd), (deg_ref, deg)):
                    ref[r, c] = val[h]
                dqk_ref[r, DK * h:DK * h + CHUNK] = dqk[h]
                dqk_ref[r, DK * h + CHUNK:DK * (h + 1)] = jnp.zeros((CHUNK, DK - CHUNK), f32)
        ds_ref[...] = dS

    blk = pl.BlockSpec((rows, N_HEADS_D * DK), lambda n: (steps - 1 - n, 0))
    return pl.pallas_call(
        body, grid=(steps,),
        in_specs=[blk] * 6 + [pl.BlockSpec((SCAN_CHUNKS, N_HEADS_D, DK, DK), lambda n: (steps - 1 - n, 0, 0, 0)), blk],
        out_specs=[blk] * 6, out_shape=[SDS((S, N_HEADS_D * DK), f32)] * 6,
        scratch_shapes=[pltpu.VMEM((N_HEADS_D, DK, DK), f32)], name=name,
        compiler_params=_params("arbitrary"))(*pre, states, do)


def _loss_head(y, t, name):
    S, D = y.shape
    tm = ROW_TILE

    def body(y_ref, t_ref, dy_ref, l_ref):
        i = pl.program_id(0)
        d = y_ref[...] - t_ref[...]
        dy_ref[...] = d * (1.0 / D)
        part = jnp.sum(jnp.sum(d * d, axis=1, keepdims=True), axis=0, keepdims=True) * (0.5 / D)

        @pl.when(i == 0)
        def _():
            l_ref[...] = jnp.zeros_like(l_ref)

        l_ref[...] += jnp.broadcast_to(part, l_ref.shape)

    spec = pl.BlockSpec((tm, D), lambda i: (i, 0))
    dy, l = pl.pallas_call(body, grid=(S // tm,), in_specs=[spec, spec],
                           out_specs=[spec, pl.BlockSpec((1, 128), lambda i: (0, 0))],
                           out_shape=[SDS((S, D), f32), SDS((1, 128), f32)], name=name,
                           compiler_params=_params("arbitrary"))(y, t)
    return l[0, 0], dy


def _adamw_refs(w_ref, g_ref, m_ref, v_ref, d_ref, mo_ref, vo_ref):
    gv = g_ref[...]
    m2 = ADAM_B1 * m_ref[...] + (1.0 - ADAM_B1) * gv
    v2 = ADAM_B2 * v_ref[...] + (1.0 - ADAM_B2) * (gv * gv)
    m_hat = m2 / (1.0 - ADAM_B1 ** ADAM_STEP)
    v_hat = v2 / (1.0 - ADAM_B2 ** ADAM_STEP)
    d_ref[...] = -ADAM_LR * (m_hat / (jnp.sqrt(v_hat) + ADAM_EPS) + ADAM_WD * w_ref[...])
    mo_ref[...] = m2
    vo_ref[...] = v2


def _adamw_small(ws, gs, ms, vs, name):
    n = len(ws)

    def body(*refs):
        for i in range(n):
            _adamw_refs(*[refs[k * n + i] for k in range(7)])

    res = pl.pallas_call(body, out_shape=[SDS(a.shape, f32) for a in ws] * 3, name=name)(*ws, *gs, *ms, *vs)
    return res[:n], res[n:2 * n], res[2 * n:]


def _adamw(w, g, m, v, tr, name):
    L, R, C = w.shape
    assert R % tr == 0

    def body(*refs):
        _adamw_refs(*refs)

    spec = pl.BlockSpec((1, tr, C), lambda l, i: (l, i, 0))
    return pl.pallas_call(body, grid=(L, R // tr), in_specs=[spec] * 4, out_specs=[spec] * 3,
                          out_shape=[SDS((L, R, C), f32)] * 3, name=name,
                          compiler_params=_params("parallel", "parallel"))(w, g, m, v)


def _rope_tables(S):
    inv = 1.0 / (10000.0 ** (jnp.arange(0, HEAD_DIM, 2, dtype=f32) / HEAD_DIM))
    ang = jnp.arange(S, dtype=f32)[:, None] * inv[None, :]
    cos, sin = jnp.cos(ang), jnp.sin(ang)
    return (jnp.tile(jnp.concatenate([cos, cos], axis=1), (1, N_HEADS_A)),
            jnp.tile(jnp.concatenate([-sin, sin], axis=1), (1, N_HEADS_A)))


def _layer_fwd(x, W, cos, sgn_sin, l, late_weights=None, h1=None, next_pre_mix=None, mid_mixer=None, after_ffn_in=None):
    n = f"l{l}_"
    if h1 is None:
        (h1,) = _rows(_rms_fn, [x], [W["norm_pre_mix"]], [(D_MODEL, bf16)], n + "pre_mix_norm")
    proj = _mm(h1, W["w_in"], "nn", 1024, 768, f32, n + "in_proj")
    attn_out, lse = _attn_fwd(proj, cos, sgn_sin, n + "attn_fwd")
    (qkv,) = _colconv_fwd([(proj, DN_QKV_COL // 512)], [(W["dn_conv_w"], 0)], None, 4, _silu_fn, 3, 512, [f32], n + "dn_conv")
    dn_pre = _dn_prep(qkv, proj, W["dn_a_log"], W["dn_dt_bias"], n + "dn_prep")
    if mid_mixer is not None:
        W = dict(W, dn_norm_w=W["dn_norm_w"] + mid_mixer(dn_pre[0]))
    dn_o, dn_states = _dn_scan(dn_pre, n + "dn_scan")
    (dn_out,) = _rows(_dn_post_fn, [(dn_o, DK, 0), (proj, DK, DN_Z_COL // DK)], [W["dn_norm_w"]], [(DK, bf16)], n + "dn_post",
                      ncol=N_HEADS_D, tm=4 * ROW_TILE)
    mix_in = jnp.concatenate([attn_out, dn_out], axis=1)
    late = late_weights(mix_in) if late_weights is not None else {}
    W = {**W, **late}
    mix = _mm(mix_in, W["w_out"], "nn", 1024, 512, f32, n + "out_proj")
    x1, h2 = _rows(_res_rms_pre_fn, [mix, x], [W["norm_post_mix"], W["norm_pre_ffn"]],
                   [(D_MODEL, f32), (D_MODEL, bf16)], n + "post_mix_pre_ffn_norm")
    u0 = _mm(h2, W["ffn_w_in"], "nn", 1024, D_FF // 2, bf16, n + "ffn_in")
    if after_ffn_in is not None:
        W = dict(W, ffn_conv_b=W["ffn_conv_b"] + after_ffn_in(u0))
    nb_ff = D_FF // 256
    (act,) = _colconv_fwd([(u0, 0), (u0, nb_ff)], [(W["ffn_conv_w"], 0), (W["ffn_conv_w"], nb_ff)],
                          [(W["ffn_conv_b"], 0), (W["ffn_conv_b"], nb_ff)], 3, _geglu_fn, nb_ff, 256, [bf16],
                          n + "ffn_conv_glu")
    f = _mm(act, W["ffn_w_out"], "nn", 1024, 512, f32, n + "ffn_out")
    if next_pre_mix is None:
        (x2,), h1_next = _rows(_res_rms_fn, [f, x1], [W["norm_post_ffn"]], [(D_MODEL, f32)], n + "post_ffn_norm"), None
    else:
        x2, h1_next = _rows(_res_rms_pre_fn, [f, x1], [W["norm_post_ffn"], next_pre_mix],
                            [(D_MODEL, f32), (D_MODEL, bf16)], n + "post_ffn_next_pre_mix_norm")
    saved = dict(x=x, h1=h1, proj=proj, attn_out=attn_out, lse=lse, qkv=qkv, dn_pre=dn_pre, dn_o=dn_o,
                 dn_states=dn_states, mix_in=mix_in, mix=mix, x1=x1, h2=h2, u0=u0, act=act, f=f, late=late)
    return x2, h1_next, saved


def _layer_bwd(dx2, sv, W, cos, sgn_sin, l, after_ffn=None, next_layer=None, first_layer=True):
    n = f"l{l}_"
    S = dx2.shape[0]
    g = {}
    g_next_pre = None
    if next_layer is None:
        (df,), (g["norm_post_ffn"],) = _rows_vjp(_rms_fn, [sv["f"]], [W["norm_post_ffn"]], [dx2], [0], [0],
                                                 n + "post_ffn_norm_bwd", row_dtype=bf16)
    else:
        (df, dx2), (g["norm_post_ffn"], g_next_pre) = _rows_vjp(
            _res_rms_pre_fn, [sv["f"], sv["x1"]], [W["norm_post_ffn"], next_layer[1]], [dx2, next_layer[0]], [0, 1], [0, 1],
            n + "post_ffn_next_pre_mix_norm_bwd", row_dtype=[bf16, f32])
    dact = _mm(df, W["ffn_w_out"], "nt", 1024, D_FF // 2, f32, n + "ffn_out_dx")
    g["ffn_w_out"] = _mm(sv["act"], df, "tn", 256, 1024, f32, n + "ffn_out_dw")
    nb_ff = D_FF // 256
    u0 = sv["u0"]
    dxs, dws, dbs = _colconv_bwd([(u0, 0), (u0, nb_ff)], [(W["ffn_conv_w"], 0), (W["ffn_conv_w"], nb_ff)],
                                 [(W["ffn_conv_b"], 0), (W["ffn_conv_b"], nb_ff)], 3, _geglu_fn, [dact], nb_ff, 256,
                                 n + "ffn_conv_glu_bwd", dx_dtype=bf16)
    du0 = jnp.concatenate(dxs, axis=1)
    g["ffn_conv_w"] = jnp.concatenate(dws, axis=1)
    g["ffn_conv_b"] = jnp.concatenate(dbs, axis=1)
    dh2 = _mm(du0, W["ffn_w_in"], "nt", 1024, 512, f32, n + "ffn_in_dx")
    g["ffn_w_in"] = _mm(sv["h2"], du0, "tn", 512, D_FF // 2, f32, n + "ffn_in_dw", column_shards=True)
    if after_ffn is not None:
        W = dict(W, norm_post_mix=W["norm_post_mix"] + after_ffn(g, dh2))
    (dmix, dx1), (g["norm_post_mix"], g["norm_pre_ffn"]) = _rows_vjp(
        _res_rms_pre_fn, [sv["mix"], sv["x"]], [W["norm_post_mix"], W["norm_pre_ffn"]], [dx2, dh2], [0, 1], [0, 1],
        n + "post_mix_pre_ffn_norm_bwd", row_dtype=[bf16, f32])
    dmix_in = _mm(dmix, W["w_out"], "nt", 1024, 512, f32, n + "out_proj_dx")
    g["w_out"] = _mm(sv["mix_in"], dmix, "tn", 1024, 512, f32, n + "out_proj_dw")

    (ddn_o, dz), (g["dn_norm_w"],) = _rows_vjp(
        _dn_post_fn, [(sv["dn_o"], DK, 0), (sv["proj"], DK, DN_Z_COL // DK)], [W["dn_norm_w"]], [(dmix_in, DK, ATTN_W // DK)],
        [0, 1], [0], n + "dn_post_bwd", ncol=N_HEADS_D, tm=4 * ROW_TILE)
    dpre = _dn_scan_bwd(sv["dn_pre"], sv["dn_states"], ddn_o, n + "dn_scan_bwd")
    dq, dk, dv, dba, g["dn_a_log"], g["dn_dt_bias"] = _dn_prep_bwd(
        sv["qkv"], sv["proj"], W["dn_a_log"], W["dn_dt_bias"], dpre, n + "dn_prep_bwd")
    dqkv = jnp.concatenate([dq, dk, dv], axis=1)
    (dqkv0,), (g["dn_conv_w"],), _ = _colconv_bwd([(sv["proj"], DN_QKV_COL // 512)], [(W["dn_conv_w"], 0)], None, 4, _silu_fn,
                                                 [dqkv], 3, 512, n + "dn_conv_bwd")

    daq, dak, dav = _attn_bwd(sv["proj"], cos, sgn_sin, dmix_in, sv["attn_out"], sv["lse"], n + "attn_bwd")
    dproj = jnp.concatenate([daq, dak, dav, dqkv0, dz, dba, jnp.zeros((S, PROJ_W - DN_BA_COL - 128), f32)], axis=1).astype(bf16)
    dh1 = _mm(dproj, W["w_in"], "nt", 1024, 512, f32, n + "in_proj_dx")
    g["w_in"] = _mm(sv["h1"], dproj, "tn", 1024, 768, f32, n + "in_proj_dw")
    if not first_layer:
        return (dx1, dh1), g, g_next_pre
    (dx,), (g["norm_pre_mix"],) = _rows_vjp(_rms_fn, [sv["x"]], [W["norm_pre_mix"]], [dh1], [0], [0],
                                            n + "pre_mix_norm_bwd", adds={0: dx1})
    return dx, g, g_next_pre


def _local_step(x, target, layers):
    cos, sgn_sin = _rope_tables(x.shape[0])
    saved, h1 = [], None
    for l, W in enumerate(layers):
        nxt = layers[l + 1]["norm_pre_mix"] if l + 1 < len(layers) else None
        x, h1, sv = _layer_fwd(x, W, cos, sgn_sin, l, h1=h1, next_pre_mix=nxt)
        saved.append(sv)
    loss, dx = _loss_head(x, target, "loss_head")
    grads = [None] * len(layers)
    nxt = None
    for l in reversed(range(len(layers))):
        dx, grads[l], g_pre = _layer_bwd(dx, saved[l], layers[l], cos, sgn_sin, l, next_layer=nxt, first_layer=(l == 0))
        if g_pre is not None:
            grads[l + 1]["norm_pre_mix"] = g_pre
        if l > 0:
            dx, dh1 = dx
            nxt = (dh1, layers[l]["norm_pre_mix"])
    return loss, dx, grads


def _pos():
    x, y, c = lax.axis_index("x"), lax.axis_index("y"), lax.axis_index("c")
    return x, y, c, [(1 - x, y), (x, 1 - y), (1 - x, 1 - y)]


def _rcopy(src, dst, send_sem, recv_sem, dev):
    return pltpu.make_async_remote_copy(src_ref=src, dst_ref=dst, send_sem=send_sem, recv_sem=recv_sem,
                                        device_id=dev, device_id_type=MESH)


def _half_rows(ref, h, which, axis):
    if h is None:
        return ref
    rows = pl.ds(pl.multiple_of(which * h, 16), h)
    return ref.at[:, rows, :] if axis == 1 else ref.at[rows, :]


def _dma_sems(*counts):
    return [pltpu.SemaphoreType.DMA((k,)) for k in counts]


def _all_gather(arrs, halves, name):
    n = len(arrs)

    def body(*refs):
        ins, outs = refs[:n], refs[n:2 * n]
        send1, recv1, send2, recv2 = refs[2 * n:]
        x, y, c, chips = _pos()
        me, sib, s_me = (x, y, c), (x, y, 1 - c), 2 * x + y
        sends = []
        for i in range(n):
            for j, chip in enumerate(chips):
                cp = _rcopy(_half_rows(ins[i], halves[i], c, 1), _half_rows(outs[i].at[s_me], halves[i], c, 1),
                            send1.at[3 * i + j], recv1.at[3 * i + j], (*chip, c))
                cp.start()
                sends.append(cp)
        for i in range(n):
            for j, (px, py) in enumerate(chips):
                k = 3 * i + j
                landed = _half_rows(outs[i].at[2 * px + py], halves[i], c, 1)
                _rcopy(landed, landed, send1.at[k], recv1.at[k], me).wait_recv()
                if halves[i] is not None:
                    cp = _rcopy(landed, landed, send2.at[k], recv2.at[k], sib)
                    cp.start()
                    sends.append(cp)
        for i in range(n):
            if halves[i] is None:
                continue
            for j, (px, py) in enumerate(chips):
                k = 3 * i + j
                other = _half_rows(outs[i].at[2 * px + py], halves[i], 1 - c, 1)
                _rcopy(other, other, send2.at[k], recv2.at[k], me).wait_recv()
        for cp in sends:
            cp.wait_send()

    return pl.pallas_call(
        body, in_specs=[ANY] * n, out_specs=[ANY] * n,
        out_shape=[SDS((4,) + a.shape, a.dtype) for a in arrs],
        scratch_shapes=_dma_sems(3 * n, 3 * n, 3 * n, 3 * n), name=name)(*arrs)


HBM = pl.BlockSpec(memory_space=pltpu.HBM)
SEM = pl.BlockSpec(memory_space=pltpu.SEMAPHORE)
_EFFECT = pltpu.SideEffectType.DATAFLOW_SIDE_EFFECTING


def _in_hbm(a):
    return pltpu.with_memory_space_constraint(a, pltpu.HBM)


def _split_copy(srcs, land_shapes, plan, per, after, name):
    n, nl = len(srcs), len(land_shapes)
    k = per * n

    def body(*refs):
        ins, lands, token = refs[:n], refs[n:n + nl], refs[-1]
        send, recv = refs[n + nl + 1], refs[n + nl + 2]
        for i, (src, dst, dev, _) in enumerate(plan(ins, lands)):
            _rcopy(src, dst, send.at[i], recv.at[i], dev).start()
        token[...] = jnp.zeros_like(token)

    lands = [_in_hbm(lax.empty(s.shape, s.dtype)) for s in land_shapes]
    return pl.pallas_call(
        body, name=name,
        out_shape=(pltpu.SemaphoreType.DMA((k,)), pltpu.SemaphoreType.DMA((k,)),
                   *[pltpu.HBM(a.shape, a.dtype) for a in srcs], *[pltpu.HBM(s.shape, s.dtype) for s in land_shapes],
                   SDS((8, 128), f32)),
        in_specs=[HBM] * (n + nl) + [ANY], out_specs=(SEM, SEM, *[HBM] * (n + nl), pl.BlockSpec(memory_space=pltpu.VMEM)),
        input_output_aliases={i: 2 + i for i in range(n + nl)},
        compiler_params=pltpu.CompilerParams(has_side_effects=_EFFECT))(*[_in_hbm(a) for a in srcs], *lands, after)


def _split_wait(started, n, plan, after, name, nl=None):
    nl = n if nl is None else nl
    send, recv = started[0], started[1]
    thru = started[2:2 + n + nl]

    def body(*refs):
        ins, lands = refs[:n], refs[n:n + nl]
        send_ref, recv_ref = refs[n + nl], refs[n + nl + 1]
        for i, (src, _, dev, mine) in enumerate(plan(ins, lands)):
            cp = _rcopy(src, mine, send_ref.at[i], recv_ref.at[i], dev)
            cp.wait_send()
            cp.wait_recv()

    res = pl.pallas_call(
        body, name=name, out_shape=tuple(pltpu.HBM(a.shape, a.dtype) for a in thru),
        in_specs=[HBM] * (n + nl) + [SEM, SEM, ANY], out_specs=tuple([HBM] * (n + nl)),
        input_output_aliases={i: i for i in range(n + nl)},
        compiler_params=pltpu.CompilerParams(has_side_effects=_EFFECT))(*thru, send, recv, after)
    return res[:n], res[n:]


def _gather_plan(halves):
    def plan(ins, lands):
        x, y, c, chips = _pos()
        out = []
        for i in range(len(ins)):
            for px, py in chips:
                out.append((_half_rows(ins[i], halves[i], c, 1), _half_rows(lands[i].at[2 * x + y], halves[i], c, 1),
                            (px, py, c), _half_rows(lands[i].at[2 * px + py], halves[i], c, 1)))
        return out
    return plan


def _scatter_plan(ins, lands):
    x, y, c, chips = _pos()
    out = []
    for i in range(len(ins)):
        for j, (px, py) in enumerate(chips):
            out.append((ins[i].at[2 * px + py], lands[i].at[j], (px, py, c), lands[i].at[j]))
    return out


def _exchange_plan(ins, lands):
    x, y, c, _ = _pos()
    return [(_half_rows(g, g.shape[1] // 2, 1 - c, 1), land, (x, y, 1 - c), land) for g, land in zip(ins, lands)]


def _pass_plan(halves):
    def plan(ins, lands):
        x, y, c, chips = _pos()
        out = []
        for i in range(len(ins)):
            for px, py in chips:
                slot = ins[i].at[2 * px + py]
                out.append((_half_rows(slot, halves[i], c, 1), _half_rows(slot, halves[i], c, 1), (x, y, 1 - c),
                            _half_rows(slot, halves[i], 1 - c, 1)))
        return out
    return plan


def _exchange_halves(gs, name):
    n = len(gs)

    def body(*refs):
        ins, outs = refs[:n], refs[n:2 * n]
        send, recv = refs[2 * n:]
        x, y, c, _ = _pos()
        sends = []
        for k in range(n):
            cp = _rcopy(_half_rows(ins[k], gs[k].shape[1] // 2, 1 - c, 1), outs[k], send.at[k], recv.at[k], (x, y, 1 - c))
            cp.start()
            sends.append(cp)
        for k in range(n):
            _rcopy(outs[k], outs[k], send.at[k], recv.at[k], (x, y, c)).wait_recv()
        for cp in sends:
            cp.wait_send()

    return pl.pallas_call(
        body, in_specs=[ANY] * n, out_specs=[ANY] * n,
        out_shape=[SDS((4, g.shape[1] // 2, g.shape[2]), g.dtype) for g in gs],
        scratch_shapes=_dma_sems(n, n), name=name)(*gs)


def _scatter_partials(ps, name):
    n = len(ps)

    def body(*refs):
        ins, outs = refs[:n], refs[n:2 * n]
        send, recv = refs[2 * n:]
        x, y, c, chips = _pos()
        sends = []
        for k in range(n):
            for j, (px, py) in enumerate(chips):
                cp = _rcopy(ins[k].at[2 * px + py], outs[k].at[j], send.at[3 * k + j], recv.at[3 * k + j], (px, py, c))
                cp.start()
                sends.append(cp)
        for k in range(n):
            for j in range(3):
                _rcopy(outs[k].at[j], outs[k].at[j], send.at[3 * k + j], recv.at[3 * k + j], (x, y, c)).wait_recv()
        for cp in sends:
            cp.wait_send()

    return pl.pallas_call(
        body, in_specs=[ANY] * n, out_specs=[ANY] * n,
        out_shape=[SDS((3,) + p.shape[1:], p.dtype) for p in ps],
        scratch_shapes=_dma_sems(3 * n, 3 * n), name=name)(*ps)


def _join_halves(rs, layers, name):
    n = len(rs)

    def body(*refs):
        outs = refs[n:2 * n]
        send, recv = refs[2 * n:]
        x, y, c, _ = _pos()

        def half(k, which):
            h = rs[k].shape[1] // 2
            return _half_rows(outs[k], h, which, 1) if layers[k] is None else _half_rows(outs[k].at[layers[k]], h, which, 0)

        sends = []
        for k in range(n):
            cp = _rcopy(half(k, c), half(k, c), send.at[k], recv.at[k], (x, y, 1 - c))
            cp.start()
            sends.append(cp)
        for k in range(n):
            _rcopy(half(k, 1 - c), half(k, 1 - c), send.at[k], recv.at[k], (x, y, c)).wait_recv()
        for cp in sends:
            cp.wait_send()

    return pl.pallas_call(
        body, in_specs=[ANY] * n, out_specs=[ANY] * n, out_shape=[SDS(r.shape, r.dtype) for r in rs],
        input_output_aliases={k: k for k in range(n)}, scratch_shapes=_dma_sems(n, n), name=name)(*rs)


def _all_reduce_small(pack, name):
    R = pack.shape[0]

    def body(in_ref, out_ref, buf, send, recv):
        x, y, c, _ = _pos()
        me = 4 * x + 2 * y + c
        buf[me] = in_ref[...]
        sends = []
        for k in range(1, 8):
            peer = me ^ k
            cp = _rcopy(buf.at[me], buf.at[me], send.at[k - 1], recv.at[k - 1], ((peer >> 2) & 1, (peer >> 1) & 1, peer & 1))
            cp.start()
            sends.append(cp)
        for k in range(1, 8):
            _rcopy(buf.at[me ^ k], buf.at[me ^ k], send.at[k - 1], recv.at[k - 1], (x, y, c)).wait_recv()
        for cp in sends:
            cp.wait_send()
        acc = buf[0]
        for d in range(1, 8):
            acc = acc + buf[d]
        out_ref[...] = acc

    return pl.pallas_call(
        body, out_shape=SDS((R, 128), f32),
        in_specs=[pl.BlockSpec(memory_space=pltpu.VMEM)], out_specs=pl.BlockSpec(memory_space=pltpu.VMEM),
        scratch_shapes=[pltpu.VMEM((8, R, 128), f32)] + _dma_sems(7, 7), name=name)(pack)


def _add_sibling(g, recv, c_arr, tr, name):
    _, R, C = g.shape
    h = R // 2
    nrb = h // tr
    assert h % tr == 0

    def body(c_ref, g_ref, r_ref, o_ref):
        o_ref[...] = (g_ref[...] + r_ref[...]).astype(o_ref.dtype)

    spec = pl.BlockSpec((1, tr, C), lambda s, r, c_ref: (s, r, 0))
    grid_spec = pltpu.PrefetchScalarGridSpec(
        num_scalar_prefetch=1, grid=(4, nrb),
        in_specs=[pl.BlockSpec((1, tr, C), lambda s, r, c_ref: (s, c_ref[0] * nrb + r, 0)), spec], out_specs=spec)
    return pl.pallas_call(body, grid_spec=grid_spec, out_shape=SDS((4, h, C), bf16), name=name,
                          compiler_params=_params("parallel", "parallel"))(c_arr, g, recv)


def _add_chips(p, recv, sc_arr, tr, layer, into, name):
    _, h, C = p.shape
    nrb = h // tr
    assert h % tr == 0

    def body(sc_ref, p_ref, r_ref, *rest):
        rest[-1][...] = (p_ref[0].astype(f32) + r_ref[0].astype(f32)) + (r_ref[1].astype(f32) + r_ref[2].astype(f32))

    grid_spec = pltpu.PrefetchScalarGridSpec(
        num_scalar_prefetch=1, grid=(nrb,),
        in_specs=[pl.BlockSpec((1, tr, C), lambda r, sc_ref: (sc_ref[0], r, 0)),
                  pl.BlockSpec((3, tr, C), lambda r, sc_ref: (0, r, 0))] + ([] if into is None else [ANY]),
        out_specs=pl.BlockSpec((None, tr, C), lambda r, sc_ref: (layer, sc_ref[1] * nrb + r, 0)))
    return pl.pallas_call(body, grid_spec=grid_spec, out_shape=SDS((2, 2 * h, C), f32), name=name,
                          input_output_aliases={} if into is None else {3: 0},
                          compiler_params=_params("parallel"))(sc_arr, p, recv, *([] if into is None else [into]))


_BIG = (("w_in", 1024, 256), ("w_out", 256, 128), ("ffn_w_in", 1024, 256), ("ffn_w_out", 704, 352))
_SMALL = ("dn_conv_w", "ffn_conv_w", "ffn_conv_b", "norm_pre_mix", "norm_post_mix", "norm_pre_ffn", "norm_post_ffn",
          "dn_norm_w", "dn_a_log", "dn_dt_bias")
_WEIGHTS = ("w_in", "dn_conv_w", "dn_a_log", "dn_dt_bias", "dn_norm_w", "w_out", "ffn_w_in", "ffn_conv_w", "ffn_conv_b",
            "ffn_w_out", "norm_pre_mix", "norm_post_mix", "norm_pre_ffn", "norm_post_ffn")
_ADAM_ROWS = {"w_in": 512, "w_out": 256, "ffn_w_in": 256, "ffn_w_out": 352}


def _shard_major(name, g):
    if name == "w_in":
        width = IN_COLS // N_SHARDS
        return jnp.stack([g[:, width * s:width * (s + 1)] for s in range(N_SHARDS)])
    if name == "ffn_w_in":
        return g
    return g.reshape(4, g.shape[0] // 4, g.shape[1])


def kernel(x, w_in, dn_conv_w, dn_a_log, dn_dt_bias, dn_norm_w, w_out, ffn_w_in, ffn_conv_w, ffn_conv_b, ffn_w_out, norm_pre_mix, norm_post_mix, norm_pre_ffn, norm_post_ffn, loss_target, m_w_in, m_dn_conv_w, m_dn_a_log, m_dn_dt_bias, m_dn_norm_w, m_w_out, m_ffn_w_in, m_ffn_conv_w, m_ffn_conv_b, m_ffn_w_out, m_norm_pre_mix, m_norm_post_mix, m_norm_pre_ffn, m_norm_post_ffn, v_w_in, v_dn_conv_w, v_dn_a_log, v_dn_dt_bias, v_dn_norm_w, v_w_out, v_ffn_w_in, v_ffn_conv_w, v_ffn_conv_b, v_ffn_w_out, v_norm_pre_mix, v_norm_post_mix, v_norm_pre_ffn, v_norm_post_ffn):
    w = dict(w_in=w_in, dn_conv_w=dn_conv_w, dn_a_log=dn_a_log, dn_dt_bias=dn_dt_bias, dn_norm_w=dn_norm_w, w_out=w_out,
             ffn_w_in=ffn_w_in, ffn_conv_w=ffn_conv_w, ffn_conv_b=ffn_conv_b, ffn_w_out=ffn_w_out, norm_pre_mix=norm_pre_mix,
             norm_post_mix=norm_post_mix, norm_pre_ffn=norm_pre_ffn, norm_post_ffn=norm_post_ffn)
    m = dict(w_in=m_w_in, dn_conv_w=m_dn_conv_w, dn_a_log=m_dn_a_log, dn_dt_bias=m_dn_dt_bias, dn_norm_w=m_dn_norm_w,
             w_out=m_w_out, ffn_w_in=m_ffn_w_in, ffn_conv_w=m_ffn_conv_w, ffn_conv_b=m_ffn_conv_b, ffn_w_out=m_ffn_w_out,
             norm_pre_mix=m_norm_pre_mix, norm_post_mix=m_norm_post_mix, norm_pre_ffn=m_norm_pre_ffn,
             norm_post_ffn=m_norm_post_ffn)
    v = dict(w_in=v_w_in, dn_conv_w=v_dn_conv_w, dn_a_log=v_dn_a_log, dn_dt_bias=v_dn_dt_bias, dn_norm_w=v_dn_norm_w,
             w_out=v_w_out, ffn_w_in=v_ffn_w_in, ffn_conv_w=v_ffn_conv_w, ffn_conv_b=v_ffn_conv_b, ffn_w_out=v_ffn_w_out,
             norm_pre_mix=v_norm_pre_mix, norm_post_mix=v_norm_post_mix, norm_pre_ffn=v_norm_pre_ffn,
             norm_post_ffn=v_norm_post_ffn)
    xi, yi, ci = lax.axis_index("x"), lax.axis_index("y"), lax.axis_index("c")
    s_me = 2 * xi + yi
    c_arr = jnp.reshape(ci, (1,)).astype(jnp.int32)
    sc_arr = jnp.stack([s_me, ci]).astype(jnp.int32)

    mats = [name for name, _, _ in _BIG]
    rest = mats[1:]
    half_of = {name: rows // 2 for name, rows, _ in _BIG}
    tiles = {name: tr for name, _, tr in _BIG}
    gathered_shape = lambda a: SDS((4,) + a.shape, a.dtype)

    own = {k: w[k].astype(bf16) for k in mats}
    plan_in = _gather_plan([half_of["w_in"], None, None])
    src_in = [own["w_in"][0:1], dn_conv_w, ffn_conv_w]
    started_in = _split_copy(src_in, [gathered_shape(a) for a in src_in], plan_in, 3, src_in[0], "weights_gather_w_in0_start")
    plan0 = _gather_plan([half_of[k] for k in rest])
    src0 = [own[k][0:1] for k in rest]
    started0 = _split_copy(src0, [gathered_shape(a) for a in src0], plan0, 3, started_in[-1], "weights_gather_l0_start")
    plan1 = _gather_plan([half_of[k] for k in mats])
    src1 = [own[k][1:2] for k in mats]
    started1 = _split_copy(src1, [gathered_shape(a) for a in src1], plan1, 3, started0[-1], "weights_gather_l1_start")
    _, landed_in = _split_wait(started_in, len(src_in), plan_in, started1[-1], "weights_gather_w_in0_wait")
    pass_in = _pass_plan([half_of["w_in"]])
    passed_in = _split_copy(landed_in[:1], [], pass_in, 3, landed_in[0], "weights_pass_w_in0_start")
    got_in = list(_split_wait(passed_in, 1, pass_in, passed_in[-1], "weights_pass_w_in0_wait", nl=0)[0]) + list(landed_in[1:])

    def pick(mine, gathered):
        return [jnp.where(s_me == s, mine, gathered[s]) for s in range(4)]

    conv = {"dn_conv_w": jnp.concatenate(pick(dn_conv_w, got_in[1]), axis=-1),
            "ffn_conv_w": jnp.concatenate(pick(ffn_conv_w, got_in[2]), axis=-1)}
    lanes = lambda a: jnp.pad(a, ((0, 0), (0, 128 - a.shape[1])))
    vec = dict(dn_a_log=lanes(dn_a_log), dn_dt_bias=lanes(dn_dt_bias), dn_norm_w=dn_norm_w, ffn_conv_b=ffn_conv_b,
               norm_pre_mix=norm_pre_mix, norm_post_mix=norm_post_mix, norm_pre_ffn=norm_pre_ffn, norm_post_ffn=norm_post_ffn)

    def matrices(l, names, gathered):
        W = {}
        for k, a in zip(names, gathered):
            if k in ("w_out", "ffn_w_out"):
                rows_, cols = own[k].shape[1:]
                W[k] = lax.dynamic_update_slice(a[:, 0], own[k][l][None], (s_me, 0, 0)).reshape(4 * rows_, cols)
            else:
                cat = jnp.concatenate(pick(own[k][l], a[:, 0]), axis=-1)
                W[k] = jnp.pad(cat, ((0, 0), (0, PROJ_W - IN_COLS))) if k == "w_in" else cat
        return W

    def small_weights(l):
        return {**{k: a[l] for k, a in conv.items()}, **{k: a[l:l + 1] for k, a in vec.items()}}

    pass0, pass1 = _pass_plan([half_of[k] for k in rest]), _pass_plan([half_of[k] for k in mats])
    passing = {}

    def mid_mixer_l0(marker):
        _, landed = _split_wait(started0, len(rest), plan0, marker, "weights_gather_l0_wait")
        passing["l0"] = _split_copy(landed, [], pass0, 3, marker, "weights_pass_l0_start")
        return passing["l0"][-1][0, 0]

    def late_l0(mix_in):
        return matrices(0, rest, _split_wait(passing["l0"], len(rest), pass0, mix_in, "weights_pass_l0_wait", nl=0)[0])

    def after_ffn_in_l0(marker):
        _, landed = _split_wait(started1, len(mats), plan1, marker, "weights_gather_l1_wait")
        passing["l1"] = _split_copy(landed, [], pass1, 3, marker, "weights_pass_l1_start")
        return passing["l1"][-1][0, 0]

    cos, sgn_sin = _rope_tables(x.shape[1])
    W0 = {**small_weights(0), **matrices(0, ["w_in"], got_in[:1])}
    W0_first = dict(W0, norm_pre_mix=W0["norm_pre_mix"] + started1[-1][0, 0])
    x1, h1_l1, saved0 = _layer_fwd(x[0], W0_first, cos, sgn_sin, 0, late_weights=late_l0, next_pre_mix=norm_pre_mix[1:2],
                                   mid_mixer=mid_mixer_l0, after_ffn_in=after_ffn_in_l0)
    W1 = {**small_weights(1),
          **matrices(1, mats, _split_wait(passing["l1"], len(mats), pass1, x1, "weights_pass_l1_wait", nl=0)[0])}
    x2, _, saved1 = _layer_fwd(x1, W1, cos, sgn_sin, 1, h1=h1_l1)
    loss_local, dy = _loss_head(x2, loss_target[0], "loss_head")
    loss = lax.psum(loss_local, ("x", "y", "c"))

    def shard_major(names, grads_l):
        return [_shard_major(name, grads_l[name]) for name in names]

    def add_siblings(l, names, gs, from_sib):
        return [_add_sibling(g, r, c_arr, tiles[name], f"add_sibling_{name}{l}") for g, r, name in zip(gs, from_sib, names)]

    def scatter_start(l, names, parts, after, tag):
        return _split_copy(parts, [SDS((3,) + p.shape[1:], p.dtype) for p in parts], _scatter_plan, 3, after,
                           f"grads_l{l}{tag}_scatter_start")

    def owner_sums(l, names, sent, after, tag, into):
        parts, recvd = _split_wait(sent, len(names), _scatter_plan, after, f"grads_l{l}{tag}_scatter_wait")
        return {name: _add_chips(p, r, sc_arr, tiles[name], l, into.get(name), f"add_chips_{name}{l}")
                for p, r, name in zip(parts, recvd, names)}

    (dx1, dh1_l1), grads1, _ = _layer_bwd(dy, saved1, W1, cos, sgn_sin, 1, first_layer=False)
    gs1 = shard_major(mats, grads1)
    swap1 = _split_copy(gs1, [SDS((4, g.shape[1] // 2, g.shape[2]), g.dtype) for g in gs1], _exchange_plan, 1, dx1,
                        "grads_l1_sibling_start")
    ffn = ["ffn_w_in", "ffn_w_out"]
    launched = {}

    def after_ffn_l0(g_ffn, dx_mid):
        gs1_, from_sib1 = _split_wait(swap1, len(mats), _exchange_plan, dx_mid, "grads_l1_sibling_wait")
        launched["l1"] = scatter_start(1, mats, add_siblings(1, mats, gs1_, from_sib1), dx_mid, "")
        gs0 = shard_major(ffn, g_ffn)
        from_sib0 = _exchange_halves(gs0, "grads_l0_ffn_to_sibling")
        launched["l0_ffn"] = scatter_start(0, ffn, add_siblings(0, ffn, gs0, from_sib0), launched["l1"][-1], "_ffn")
        return launched["l0_ffn"][-1][0, 0]

    W0_last = dict(W0, **saved0["late"], norm_post_ffn=W0["norm_post_ffn"] + swap1[-1][0, 0])
    dx, grads0, grads1["norm_pre_mix"] = _layer_bwd(dx1, saved0, W0_last, cos, sgn_sin, 0, after_ffn=after_ffn_l0,
                                                    next_layer=(dh1_l1, norm_pre_mix[1:2]))
    mix = ["w_in", "w_out"]
    gs0 = shard_major(mix, grads0)
    part0 = add_siblings(0, mix, gs0, _exchange_halves(gs0, "grads_l0_mix_to_sibling"))
    sent0 = scatter_start(0, mix, part0, dx, "_mix")
    red = owner_sums(0, ffn, launched["l0_ffn"], sent0[-1], "_ffn", {})
    red = owner_sums(1, mats, launched["l1"], sent0[-1], "", red)
    joined = dict(zip(mats, _join_halves([red[k] for k in mats], [1 if k in mix else None for k in mats],
                                         "grads_join_early")))
    grads = [grads0, grads1]

    small = {}
    for name in _SMALL:
        per_layer = [grads[l][name] for l in range(2)]
        if name in ("dn_a_log", "dn_dt_bias"):
            per_layer = [p[:, :N_HEADS_D] for p in per_layer]
        small[name] = jnp.stack(per_layer).reshape((2,) + (w[name].shape[1:] if name not in ("dn_conv_w", "ffn_conv_w")
                                                           else per_layer[0].shape))
    flat = jnp.concatenate([small[name].reshape(-1) for name in _SMALL])
    n_rows = -(-flat.shape[0] // 1024) * 8
    summed = _all_reduce_small(jnp.pad(flat, (0, n_rows * 128 - flat.shape[0])).reshape(n_rows, 128),
                               "small_grads_all_reduce").reshape(-1)
    off = 0
    g_out = {}
    for name in _SMALL:
        size = small[name].size
        g_out[name] = summed[off:off + size].reshape(small[name].shape)
        off += size
    for k in ("dn_conv_w", "ffn_conv_w"):
        width = w[k].shape[2]
        g_out[k] = lax.dynamic_slice_in_dim(g_out[k], s_me * width, width, axis=2)
    for k in ffn:
        g_out[k] = joined[k]

    deltas, new_m, new_v = {}, {}, {}

    def step(name):
        shape = w[name].shape
        as3 = (lambda a: a) if len(shape) == 3 else (lambda a: a.reshape(shape[0], 1, shape[1]))
        tr = _ADAM_ROWS.get(name, as3(w[name]).shape[1])
        d_, m_, v_ = _adamw(as3(w[name]), as3(g_out[name]), as3(m[name]), as3(v[name]), tr, f"adamw_{name}")
        deltas[name], new_m[name], new_v[name] = d_.reshape(shape), m_.reshape(shape), v_.reshape(shape)

    for name in ffn:
        step(name)
    tiny = [name for name in _WEIGHTS if name not in mats]
    stepped = _adamw_small(*[[d[name] for name in tiny] for d in (w, g_out, m, v)], "adamw_small")
    for out, vals in zip((deltas, new_m, new_v), stepped):
        out.update(zip(tiny, vals))
    done = jnp.reshape(deltas["ffn_w_in"][0, 0, 0] + deltas["ffn_w_out"][0, 0, 0] + deltas["norm_post_ffn"][0, 0], (1,))
    red = owner_sums(0, mix, sent0, done, "_mix", joined)
    for k, a in zip(mix, _join_halves([red[k] for k in mix], [0] * len(mix), "grads_join_late")):
        g_out[k] = a
        step(k)

    return (loss, dx[None], *[g_out[k] for k in _WEIGHTS], *[deltas[k] for k in _WEIGHTS],
            *[new_m[k] for k in _WEIGHTS], *[new_v[k] for k in _WEIGHTS])
```

```python
import jax
import jax.numpy as jnp
from jax import lax
from jax.experimental import pallas as pl
from jax.experimental.pallas import tpu as pltpu

f32, bf16 = jnp.float32, jnp.bfloat16
SDS = jax.ShapeDtypeStruct
HI = lax.Precision.HIGH
MESH = pl.DeviceIdType.MESH
ANY = pl.BlockSpec(memory_space=pl.ANY)

D_MODEL = 1024
N_HEADS_A, HEAD_DIM = 8, 64
ATTN_W = 512
N_HEADS_D, DK = 4, 128
CHUNK = 64
D_FF = 2816
IN_COLS = 3592
PROJ_W = 3840
DN_QKV_COL = 3 * ATTN_W
DN_Z_COL = DN_QKV_COL + 3 * N_HEADS_D * DK
DN_BA_COL = DN_Z_COL + N_HEADS_D * DK
N_SHARDS = 4
BRANCHES = ((1, 16), (4, 4), (16, 1))
EPS = 1e-6
NEG = -1e30
ROW_TILE = 512
VMEM_LIMIT = 56 * 1024 * 1024

ADAM_LR, ADAM_B1, ADAM_B2, ADAM_EPS, ADAM_WD, ADAM_STEP = 0.001, 0.9, 0.999, 1e-08, 0.01, 10


def _params(*sem):
    return pltpu.CompilerParams(dimension_semantics=sem, vmem_limit_bytes=VMEM_LIMIT)


def _mm(a, b, mode, tm, tn, out_dtype, name, column_shards=False):
    if mode == "nn":
        (M, K), N = a.shape, b.shape[1]
        dims = (((1,), (0,)), ((), ()))
        a_spec = pl.BlockSpec((tm, K), lambda i, j: (i, 0))
        b_spec = pl.BlockSpec((K, tn), lambda i, j: (0, j))
    elif mode == "nt":
        (M, K), N = a.shape, b.shape[0]
        dims = (((1,), (1,)), ((), ()))
        a_spec = pl.BlockSpec((tm, K), lambda i, j: (i, 0))
        b_spec = pl.BlockSpec((tn, K), lambda i, j: (j, 0))
    else:
        (K, M), N = a.shape, b.shape[1]
        dims = (((0,), (0,)), ((), ()))
        a_spec = pl.BlockSpec((K, tm), lambda i, j: (0, i))
        b_spec = pl.BlockSpec((K, tn), lambda i, j: (0, j))
    assert M % tm == 0 and N % tn == 0, (name, M, N, tm, tn)

    def body(a_ref, b_ref, o_ref):
        o_ref[...] = lax.dot_general(a_ref[...].astype(bf16), b_ref[...].astype(bf16), dims,
                                     preferred_element_type=f32).astype(o_ref.dtype)

    if column_shards:
        out_spec, out_shape = pl.BlockSpec((None, tm, tn), lambda i, j: (j, i, 0)), SDS((N // tn, M, tn), out_dtype)
    else:
        out_spec, out_shape = pl.BlockSpec((tm, tn), lambda i, j: (i, j)), SDS((M, N), out_dtype)
    return pl.pallas_call(body, grid=(M // tm, N // tn), in_specs=[a_spec, b_spec], out_specs=out_spec,
                          out_shape=out_shape, name=name, compiler_params=_params("parallel", "arbitrary"))(a, b)


def _row_spec(r, tm):
    if isinstance(r, tuple):
        arr, width, cb = r
        return arr, pl.BlockSpec((tm, width), lambda i, j, cb=cb: (i, cb + j))
    return r, pl.BlockSpec((tm, r.shape[1]), lambda i, j: (i, j))


def _full_spec(p):
    return pl.BlockSpec(p.shape, lambda i, j: (0,) * p.ndim)


def _rows(fn, rows, params, outs, name, tm=ROW_TILE, ncol=1):
    arrs, specs = zip(*[_row_spec(r, tm) for r in rows])
    S = arrs[0].shape[0]
    nr, npar = len(rows), len(params)

    def body(*refs):
        vals = fn(*[r[...].astype(f32) for r in refs[:nr]], *[p[...] for p in refs[nr:nr + npar]])
        for o_ref, v in zip(refs[nr + npar:], vals):
            o_ref[...] = v.astype(o_ref.dtype)

    return pl.pallas_call(
        body, grid=(S // tm, ncol), in_specs=list(specs) + [_full_spec(p) for p in params],
        out_specs=[pl.BlockSpec((tm, w), lambda i, j: (i, j)) for w, _ in outs],
        out_shape=[SDS((S, w * ncol), dt) for w, dt in outs], name=name,
        compiler_params=_params("parallel", "parallel"))(*arrs, *params)


def _rows_vjp(fn, rows, params, cts, wrt_rows, wrt_params, name, adds=None, tm=ROW_TILE, ncol=1, row_dtype=f32):
    adds = adds or {}
    arrs, specs = zip(*[_row_spec(r, tm) for r in rows])
    carrs, cspecs = zip(*[_row_spec(c, tm) for c in cts])
    add_keys = sorted(adds)
    aarrs = [adds[k] for k in add_keys]
    S = arrs[0].shape[0]
    nr, npar, nc, na = len(rows), len(params), len(cts), len(aarrs)
    widths = [specs[k].block_shape[1] for k in wrt_rows]
    row_dtypes = row_dtype if isinstance(row_dtype, (list, tuple)) else [row_dtype] * len(wrt_rows)

    def body(*refs):
        first = jnp.logical_and(pl.program_id(0) == 0, pl.program_id(1) == 0)
        rv = [r[...].astype(f32) for r in refs[:nr]]
        pv = [p[...] for p in refs[nr:nr + npar]]
        cv = tuple(c[...].astype(f32) for c in refs[nr + npar:nr + npar + nc])
        av = dict(zip(add_keys, refs[nr + npar + nc:nr + npar + nc + na]))
        o = refs[nr + npar + nc + na:]
        _, vjp = jax.vjp(fn, *rv, *pv)
        g = vjp(cv)
        for n, k in enumerate(wrt_rows):
            val = g[k]
            if k in av:
                val = val + av[k][...]
            o[n][...] = val.astype(o[n].dtype)
        for n, k in enumerate(wrt_params):
            ref = o[len(wrt_rows) + n]

            @pl.when(first)
            def _(ref=ref):
                ref[...] = jnp.zeros_like(ref)

            ref[...] += g[nr + k]

    res = pl.pallas_call(
        body, grid=(S // tm, ncol),
        in_specs=list(specs) + [_full_spec(p) for p in params] + list(cspecs)
        + [pl.BlockSpec((tm, a.shape[1] // ncol), lambda i, j: (i, j)) for a in aarrs],
        out_specs=[pl.BlockSpec((tm, w), lambda i, j: (i, j)) for w in widths] + [_full_spec(params[k]) for k in wrt_params],
        out_shape=[SDS((S, w * ncol), dt) for w, dt in zip(widths, row_dtypes)]
        + [SDS(params[k].shape, f32) for k in wrt_params],
        name=name, compiler_params=_params("arbitrary", "arbitrary"))(*arrs, *params, *carrs, *aarrs)
    return res[:len(wrt_rows)], res[len(wrt_rows):]


def _rms(x, w):
    return x * lax.rsqrt(jnp.mean(x * x, axis=-1, keepdims=True) + EPS) * w


def _rms_fn(x, w):
    return (_rms(x, w),)


def _res_rms_fn(f, res, w):
    return (res + _rms(f, w),)


def _res_rms_pre_fn(f, res, w_post, w_pre):
    x1 = res + _rms(f, w_post)
    return x1, _rms(x1, w_pre)


def _swap_halves(x):
    lane = lax.broadcasted_iota(jnp.int32, x.shape, 1)
    first = (lane % HEAD_DIM) < (HEAD_DIM // 2)
    n = x.shape[1]
    return jnp.where(first, pltpu.roll(x, n - HEAD_DIM // 2, 1), pltpu.roll(x, HEAD_DIM // 2, 1))


def _rope_fwd_fn(q, k, cos, sgn_sin):
    scale = HEAD_DIM ** -0.5
    return ((q * cos + _swap_halves(q) * sgn_sin) * scale, k * cos + _swap_halves(k) * sgn_sin)


def _rope_bwd_fn(dq, dk, cos, sgn_sin):
    dq = dq * (HEAD_DIM ** -0.5)
    return (dq * cos + _swap_halves(dq * sgn_sin), dk * cos + _swap_halves(dk * sgn_sin))


def _nt(a, b):
    return lax.dot_general(a, b, (((1,), (1,)), ((), ())), preferred_element_type=f32)


def _tn(a, b):
    return lax.dot_general(a, b, (((0,), (0,)), ((), ())), preferred_element_type=f32)


def _band_rows(j, d, nb):
    r, i = j // nb, j % nb
    if d == 1:
        cur = pl.ds(pl.multiple_of(i * 128, 128), 128)
        prev = pl.ds(pl.multiple_of(jnp.maximum(i - 1, 0) * 128, 128), 128)
    else:
        cur = pl.ds(i * (128 * d) + r, 128, stride=d)
        prev = pl.ds(jnp.maximum(i - 1, 0) * (128 * d) + r, 128, stride=d)
    return cur, prev, (i == 0).astype(jnp.int32)


def _band_bias(bias_ref):
    a = lax.broadcasted_iota(jnp.int32, (256, 256), 0) % 128
    c = lax.broadcasted_iota(jnp.int32, (256, 256), 1)
    own = jnp.logical_and(c < 128, c <= a)
    before = jnp.logical_and(c >= 128, c - 128 >= a)
    bias_ref[0] = jnp.where(jnp.logical_or(own, before), 0.0, NEG)
    bias_ref[1] = jnp.where(own, 0.0, NEG)


def _stack_heads(x, head_a):
    return jnp.concatenate([jnp.where(head_a, x, 0.0), jnp.where(head_a, 0.0, x)], axis=0)


def _unstack_heads(x2, head_a):
    return jnp.where(head_a, x2[:128], x2[128:])


def _pair_at(S, first_col):
    return pl.BlockSpec((S, 128), lambda h: (0, first_col // 128 + h))


def _attn_fwd(proj, cos, sgn_sin, name):
    S = proj.shape[0]
    nblk = S // 128

    def body(qp_ref, kp_ref, v_ref, cos_ref, sin_ref, out_ref, lse_ref, bias_ref, q_ref, k_ref, *scr):
        head_a = lax.broadcasted_iota(jnp.int32, (1, 128), 1) < HEAD_DIM
        _band_bias(bias_ref)
        q_ref[...], k_ref[...] = _rope_fwd_fn(qp_ref[...], kp_ref[...], cos_ref[...], sin_ref[...])
        for b, (d, nb) in enumerate(BRANCHES):
            ob_ref, lb_ref = scr[2 * b], scr[2 * b + 1]

            def blk(j, carry, d=d, nb=nb, ob_ref=ob_ref, lb_ref=lb_ref):
                cur, prev, first = _band_rows(j, d, nb)
                q2 = _stack_heads(q_ref[cur, :], head_a).astype(bf16)
                if nb == 1:
                    k2, v2, bias = k_ref[cur, :].astype(bf16), v_ref[cur, :].astype(bf16), bias_ref[1][:, :128]
                else:
                    k2 = jnp.concatenate([k_ref[cur, :], k_ref[prev, :]], axis=0).astype(bf16)
                    v2 = jnp.concatenate([v_ref[cur, :], v_ref[prev, :]], axis=0).astype(bf16)
                    bias = bias_ref[first]
                s = _nt(q2, k2) + bias
                mx = jnp.max(s, axis=1, keepdims=True)
                p = jnp.exp(s - mx)
                l = jnp.sum(p, axis=1, keepdims=True)
                o = jnp.dot(p.astype(bf16), v2, preferred_element_type=f32) / l
                ob_ref[cur, :] = _unstack_heads(o, head_a)
                lb_ref[cur, :] = _unstack_heads(jnp.broadcast_to(mx + jnp.log(l), (256, 128)), head_a)
                return carry

            lax.fori_loop(0, nblk, blk, 0, unroll=16)
        l0, l1, l2 = scr[1][...], scr[3][...], scr[5][...]
        mx = jnp.maximum(jnp.maximum(l0, l1), l2)
        e0, e1, e2 = jnp.exp(l0 - mx), jnp.exp(l1 - mx), jnp.exp(l2 - mx)
        den = e0 + e1 + e2
        out_ref[...] = ((e0 * scr[0][...] + e1 * scr[2][...] + e2 * scr[4][...]) / den).astype(out_ref.dtype)
        lse_ref[...] = mx + jnp.log(den)

    pair = pl.BlockSpec((S, 128), lambda h: (0, h))
    return pl.pallas_call(
        body, grid=(N_HEADS_A // 2,),
        in_specs=[pair, _pair_at(S, ATTN_W), _pair_at(S, 2 * ATTN_W), pair, pair], out_specs=[pair, pair],
        out_shape=[SDS((S, ATTN_W), bf16), SDS((S, ATTN_W), f32)],
        scratch_shapes=[pltpu.VMEM((2, 256, 256), f32)] + [pltpu.VMEM((S, 128), f32)] * 8,
        name=name, compiler_params=_params("parallel"))(proj, proj, proj, cos, sgn_sin)


def _attn_bwd(proj, cos, sgn_sin, dmix_in, out, lse, name):
    S = proj.shape[0]
    nblk = S // 128

    def body(qp_ref, kp_ref, v_ref, cos_ref, sin_ref, do_ref, out_ref, lse_ref, dq_ref, dk_ref, dv_ref,
             bias_ref, t_ref, q_ref, k_ref):
        head_a = lax.broadcasted_iota(jnp.int32, (1, 128), 1) < HEAD_DIM
        _band_bias(bias_ref)
        q_ref[...], k_ref[...] = _rope_fwd_fn(qp_ref[...], kp_ref[...], cos_ref[...], sin_ref[...])
        x = do_ref[...] * out_ref[...].astype(f32)
        t_ref[...] = jnp.where(head_a, jnp.sum(jnp.where(head_a, x, 0.0), axis=1, keepdims=True),
                               jnp.sum(jnp.where(head_a, 0.0, x), axis=1, keepdims=True))
        dq_ref[...] = jnp.zeros_like(dq_ref)
        dk_ref[...] = jnp.zeros_like(dk_ref)
        dv_ref[...] = jnp.zeros_like(dv_ref)
        for d, nb in BRANCHES:
            def blk(j, carry, d=d, nb=nb):
                cur, prev, first = _band_rows(j, d, nb)
                q2 = _stack_heads(q_ref[cur, :], head_a).astype(bf16)
                do2 = _stack_heads(do_ref[cur, :], head_a).astype(bf16)
                t, lse_b = t_ref[cur, :], lse_ref[cur, :]
                t2 = jnp.concatenate([t[:, :1], t[:, HEAD_DIM:HEAD_DIM + 1]], axis=0)
                lse2 = jnp.concatenate([lse_b[:, :1], lse_b[:, HEAD_DIM:HEAD_DIM + 1]], axis=0)
                if nb == 1:
                    k2, v2, bias = k_ref[cur, :].astype(bf16), v_ref[cur, :].astype(bf16), bias_ref[1][:, :128]
                else:
                    k2 = jnp.concatenate([k_ref[cur, :], k_ref[prev, :]], axis=0).astype(bf16)
                    v2 = jnp.concatenate([v_ref[cur, :], v_ref[prev, :]], axis=0).astype(bf16)
                    bias = bias_ref[first]
                p = jnp.exp(_nt(q2, k2) + bias - lse2)
                ds = (p * (_nt(do2, v2) - t2)).astype(bf16)
                dq_ref[cur, :] += _unstack_heads(jnp.dot(ds, k2, preferred_element_type=f32), head_a)
                dk2, dv2 = _tn(ds, q2), _tn(p.astype(bf16), do2)
                dk_ref[cur, :] += dk2[:128]
                dv_ref[cur, :] += dv2[:128]
                if nb != 1:
                    dk_ref[prev, :] += dk2[128:]
                    dv_ref[prev, :] += dv2[128:]
                return carry

            lax.fori_loop(0, nblk, blk, 0, unroll=16)
        dq_ref[...], dk_ref[...] = _rope_bwd_fn(dq_ref[...], dk_ref[...], cos_ref[...], sin_ref[...])

    pair = pl.BlockSpec((S, 128), lambda h: (0, h))
    return pl.pallas_call(
        body, grid=(N_HEADS_A // 2,),
        in_specs=[pair, _pair_at(S, ATTN_W), _pair_at(S, 2 * ATTN_W), pair, pair, pair, pair, pair],
        out_specs=[pair] * 3, out_shape=[SDS((S, ATTN_W), f32)] * 3,
        scratch_shapes=[pltpu.VMEM((2, 256, 256), f32)] + [pltpu.VMEM((S, 128), f32)] * 3,
        name=name, compiler_params=_params("parallel"))(proj, proj, proj, cos, sgn_sin, dmix_in, out, lse)


def _conv_val(x, w, K, rows):
    acc = x * w[K - 1:K, :]
    for s in range(1, K):
        acc = acc + jnp.where(rows >= s, pltpu.roll(x, s, 0), 0.0) * w[K - 1 - s:K - s, :]
    return acc


def _colconv_fwd(xs, ws, bs, K, fn, nblk, tc, outs, name):
    S = xs[0][0].shape[0]
    n = len(xs)
    has_b = bs is not None

    def body(*refs):
        rows = lax.broadcasted_iota(jnp.int32, (S, tc), 0)
        cs = []
        for k in range(n):
            c = _conv_val(refs[k][...].astype(f32), refs[n + k][...], K, rows)
            if has_b:
                c = c + refs[2 * n + k][...]
            cs.append(c)
        for o_ref, val in zip(refs[(3 if has_b else 2) * n:], fn(*cs)):
            o_ref[...] = val.astype(o_ref.dtype)

    def cspec(rows_, cb0):
        return pl.BlockSpec((rows_, tc), lambda j, cb0=cb0: (0, cb0 + j))

    in_specs = [cspec(S, cb) for _, cb in xs] + [cspec(K, cb) for _, cb in ws]
    args = [a for a, _ in xs] + [a for a, _ in ws]
    if has_b:
        in_specs += [cspec(1, cb) for _, cb in bs]
        args += [a for a, _ in bs]
    return pl.pallas_call(
        body, grid=(nblk,), in_specs=in_specs, out_specs=[cspec(S, 0) for _ in outs],
        out_shape=[SDS((S, nblk * tc), dt) for dt in outs], name=name, compiler_params=_params("parallel"))(*args)


def _colconv_bwd(xs, ws, bs, K, fn, douts, nblk, tc, name, dx_dtype=f32):
    S = xs[0][0].shape[0]
    n, nd = len(xs), len(douts)
    has_b = bs is not None
    nin = (3 if has_b else 2) * n

    def body(*refs):
        rows = lax.broadcasted_iota(jnp.int32, (S, tc), 0)
        x = [refs[k][...].astype(f32) for k in range(n)]
        w = [refs[n + k][...] for k in range(n)]
        cs = []
        for k in range(n):
            c = _conv_val(x[k], w[k], K, rows)
            if has_b:
                c = c + refs[2 * n + k][...]
            cs.append(c)
        _, vjp = jax.vjp(fn, *cs)
        dcs = vjp(tuple(r[...].astype(f32) for r in refs[nin:nin + nd]))
        o = refs[nin + nd:]
        for k in range(n):
            dc = dcs[k]
            dx = dc * w[k][K - 1:K, :]
            o[n + k][K - 1:K, :] = jnp.sum(dc * x[k], axis=0, keepdims=True)
            for s in range(1, K):
                dx = dx + jnp.where(rows < S - s, pltpu.roll(dc, S - s, 0), 0.0) * w[k][K - 1 - s:K - s, :]
                xsh = jnp.where(rows >= s, pltpu.roll(x[k], s, 0), 0.0)
                o[n + k][K - 1 - s:K - s, :] = jnp.sum(dc * xsh, axis=0, keepdims=True)
            o[k][...] = dx.astype(o[k].dtype)
            if has_b:
                o[2 * n + k][...] = jnp.sum(dc, axis=0, keepdims=True)

    def cspec(rows_, cb0):
        return pl.BlockSpec((rows_, tc), lambda j, cb0=cb0: (0, cb0 + j))

    in_specs = [cspec(S, cb) for _, cb in xs] + [cspec(K, cb) for _, cb in ws]
    args = [a for a, _ in xs] + [a for a, _ in ws]
    if has_b:
        in_specs += [cspec(1, cb) for _, cb in bs]
        args += [a for a, _ in bs]
    in_specs += [cspec(S, 0) for _ in douts]
    args += list(douts)
    W = nblk * tc
    out_specs = [cspec(S, 0)] * n + [cspec(K, 0)] * n + ([cspec(1, 0)] * n if has_b else [])
    out_shape = [SDS((S, W), dx_dtype)] * n + [SDS((K, W), f32)] * n + ([SDS((1, W), f32)] * n if has_b else [])
    res = pl.pallas_call(body, grid=(nblk,), in_specs=in_specs, out_specs=out_specs, out_shape=out_shape,
                         name=name, compiler_params=_params("parallel"))(*args)
    return res[:n], res[n:2 * n], res[2 * n:]


def _silu_fn(c):
    return (c * jax.nn.sigmoid(c),)


_GELU_C, _GELU_A = 0.7978845608028654, 0.044715


@jax.custom_vjp
def _geglu(gate, up):
    return 0.5 * gate * (1.0 + jnp.tanh(_GELU_C * (gate + _GELU_A * gate * gate * gate))) * up


def _geglu_vjp_fwd(gate, up):
    return _geglu(gate, up), (gate, up)


def _geglu_vjp_bwd(res, d):
    gate, up = res
    g2 = gate * gate
    t = jnp.tanh(_GELU_C * gate * (1.0 + _GELU_A * g2))
    h = 0.5 * (1.0 + t)
    dgelu = h + (0.5 * _GELU_C) * gate * (1.0 - t * t) * (1.0 + (3.0 * _GELU_A) * g2)
    return d * up * dgelu, d * (gate * h)


_geglu.defvjp(_geglu_vjp_fwd, _geglu_vjp_bwd)


def _geglu_fn(gate, up):
    return (_geglu(gate, up),)


def _softplus(x):
    u = jnp.exp(jnp.minimum(x, 20.0))
    small = u * (1.0 - 0.5 * u)
    return jnp.where(x > 20.0, x, jnp.where(u < 1e-4, small, jnp.log(1.0 + u)))


def _bmm(a, b, precision=None):
    return lax.dot_general(a, b, (((2,), (1,)), ((0,), (0,))), precision=precision, preferred_element_type=f32)


def _bnt(a, b, precision=None):
    return lax.dot_general(a, b, (((2,), (2,)), ((0,), (0,))), precision=precision, preferred_element_type=f32)


def _btn(a, b, precision=None):
    return lax.dot_general(a, b, (((1,), (1,)), ((0,), (0,))), precision=precision, preferred_element_type=f32)


@jax.custom_vjp
def _unit_lower_inverse(A):
    n = A.shape[-1]
    eye = (lax.broadcasted_iota(jnp.int32, (1, n, n), 1) == lax.broadcasted_iota(jnp.int32, (1, n, n), 2)).astype(f32)
    P = -A
    T = eye + P
    for _ in range(5):
        P = _bmm(P, P, HI)
        T = T + _bmm(T, P, HI)
    return T


def _unit_lower_inverse_fwd(A):
    T = _unit_lower_inverse(A)
    return T, T


def _unit_lower_inverse_bwd(T, dT):
    return (-_btn(T, _bnt(dT, T, HI), HI),)


_unit_lower_inverse.defvjp(_unit_lower_inverse_fwd, _unit_lower_inverse_bwd)


def _dn_prep_fn(q, k, v, ba, alog, dtb, h):
    G, C = q.shape[0], CHUNK
    lane = lax.broadcasted_iota(jnp.int32, (1, 1, 128), 2)

    def sel(arr, idx):
        return jnp.sum(jnp.where(lane == idx, arr, 0.0), axis=-1, keepdims=True)

    beta = jax.nn.sigmoid(sel(ba, h))
    g = -jnp.exp(sel(alog[None], h)) * _softplus(sel(ba, N_HEADS_D + h) + sel(dtb[None], h))
    qn = q * lax.rsqrt(jnp.sum(q * q, axis=-1, keepdims=True) + EPS) * (DK ** -0.5)
    kn = k * lax.rsqrt(jnp.sum(k * k, axis=-1, keepdims=True) + EPS)
    ii = lax.broadcasted_iota(jnp.int32, (1, C, C), 1)
    jj = lax.broadcasted_iota(jnp.int32, (1, C, C), 2)
    tril, strict = ii >= jj, ii > jj
    gsq = jnp.broadcast_to(g, (G, C, C))
    gcol = _bmm(jnp.broadcast_to(tril.astype(f32), (G, C, C)), gsq, HI)
    grow = _bmm(jnp.ones((G, C, C), f32), jnp.where(ii <= jj, gsq, 0.0), HI)
    decay = jnp.exp(jnp.where(tril, gcol - grow, NEG))
    gc = gcol[:, :, :1]
    glast = gcol[:, C - 1:C, :1]
    kb = kn * beta
    A = jnp.where(strict, _bnt(kb.astype(bf16), kn.astype(bf16)) * decay, 0.0)
    T = _unit_lower_inverse(A).astype(bf16)
    u = _bmm(T, (v * beta).astype(bf16))
    w = _bmm(T, (kb * jnp.exp(gc)).astype(bf16))
    qk = _bnt(qn.astype(bf16), kn.astype(bf16)) * decay
    qd = qn * jnp.exp(gc)
    kd = kn * jnp.exp(glast - gc)
    return u, w, qk, qd, kd, jnp.broadcast_to(jnp.exp(glast), (G, C, DK))


def _dn_scan_fn(u, w, qk, qd, kd, eg, St):
    b = lambda a: a.astype(bf16)
    vnew = u - _bmm(b(w), b(St))
    o = _bmm(b(qd), b(St)) + _bmm(b(qk), b(vnew))
    return o, St * eg[:, :1, :] + _btn(b(kd), b(vnew))


def _dn_post_fn(o, z, nw):
    return (_rms(o, nw) * (z * jax.nn.sigmoid(z)),)


DN_GROUP = 16


def _dn_prep_specs(S, rows):
    def col(first):
        return pl.BlockSpec((rows, DK), lambda i, h, first=first: (i, first // DK + h))

    par = pl.BlockSpec((1, 128), lambda i, h: (0, 0))
    return [col(0), col(N_HEADS_D * DK), col(2 * N_HEADS_D * DK),
            pl.BlockSpec((rows, 128), lambda i, h: (i, DN_BA_COL // 128)), par, par]


def _dn_prep(qkv, proj, alog, dtb, name):
    S = qkv.shape[0]
    G = DN_GROUP
    rows = G * CHUNK

    def body(q_ref, k_ref, v_ref, ba_ref, al_ref, dt_ref, u_ref, w_ref, qk_ref, qd_ref, kd_ref, eg_ref):
        h = pl.program_id(1)
        r3 = lambda ref: ref[...].reshape(G, CHUNK, 128)
        u, w, qk, qd, kd, eg = _dn_prep_fn(r3(q_ref), r3(k_ref), r3(v_ref), r3(ba_ref), al_ref[...], dt_ref[...], h)
        for ref, val in ((u_ref, u), (w_ref, w), (qd_ref, qd), (kd_ref, kd), (eg_ref, eg)):
            ref[...] = val.reshape(rows, DK)
        qk_ref[:, :CHUNK] = qk.reshape(rows, CHUNK)
        qk_ref[:, CHUNK:] = jnp.zeros((rows, DK - CHUNK), f32)

    out = pl.BlockSpec((rows, DK), lambda i, h: (i, h))
    return pl.pallas_call(
        body, grid=(S // rows, N_HEADS_D), in_specs=_dn_prep_specs(S, rows), out_specs=[out] * 6,
        out_shape=[SDS((S, N_HEADS_D * DK), f32)] * 6, name=name,
        compiler_params=_params("parallel", "parallel"))(qkv, qkv, qkv, proj, alog, dtb)


def _dn_prep_bwd(qkv, proj, alog, dtb, cts, name):
    S = qkv.shape[0]
    G = DN_GROUP
    rows = G * CHUNK

    def body(q_ref, k_ref, v_ref, ba_ref, al_ref, dt_ref, du_ref, dw_ref, dqk_ref, dqd_ref, dkd_ref, deg_ref,
             dq_ref, dk_ref, dv_ref, dba_ref, dal_ref, ddt_ref):
        i, h = pl.program_id(0), pl.program_id(1)
        r3 = lambda ref: ref[...].reshape(G, CHUNK, 128)
        _, vjp = jax.vjp(lambda q, k, v, ba, al, dt: _dn_prep_fn(q, k, v, ba, al, dt, h),
                         r3(q_ref), r3(k_ref), r3(v_ref), r3(ba_ref), al_ref[...], dt_ref[...])
        dqk = dqk_ref[:, :CHUNK].reshape(G, CHUNK, CHUNK)
        dq, dk, dv, dba, dal, ddt = vjp((r3(du_ref), r3(dw_ref), dqk, r3(dqd_ref), r3(dkd_ref), r3(deg_ref)))
        dq_ref[...] = dq.reshape(rows, DK)
        dk_ref[...] = dk.reshape(rows, DK)
        dv_ref[...] = dv.reshape(rows, DK)

        @pl.when(h == 0)
        def _():
            dba_ref[...] = jnp.zeros_like(dba_ref)

        @pl.when(jnp.logical_and(i == 0, h == 0))
        def _():
            dal_ref[...] = jnp.zeros_like(dal_ref)
            ddt_ref[...] = jnp.zeros_like(ddt_ref)

        dba_ref[...] += dba.reshape(rows, 128)
        dal_ref[...] += dal
        ddt_ref[...] += ddt

    hcol = pl.BlockSpec((rows, DK), lambda i, h: (i, h))
    par = pl.BlockSpec((1, 128), lambda i, h: (0, 0))
    W = N_HEADS_D * DK
    return pl.pallas_call(
        body, grid=(S // rows, N_HEADS_D), in_specs=_dn_prep_specs(S, rows) + [hcol] * 6,
        out_specs=[hcol] * 3 + [pl.BlockSpec((rows, 128), lambda i, h: (i, 0)), par, par],
        out_shape=[SDS((S, W), f32)] * 3 + [SDS((S, 128), f32), SDS((1, 128), f32), SDS((1, 128), f32)], name=name,
        compiler_params=_params("arbitrary", "arbitrary"))(qkv, qkv, qkv, proj, alog, dtb, *cts)


def _heads(x):
    return jnp.stack([x[:, DK * h:DK * (h + 1)] for h in range(N_HEADS_D)])


SCAN_CHUNKS = 8


def _dn_scan(pre, name):
    S = pre[0].shape[0]
    NCH = S // CHUNK
    rows = SCAN_CHUNKS * CHUNK

    def body(u_ref, w_ref, qk_ref, qd_ref, kd_ref, eg_ref, o_ref, st_ref, s_ref):
        @pl.when(pl.program_id(0) == 0)
        def _():
            s_ref[...] = jnp.zeros_like(s_ref)

        St = s_ref[...]
        for k in range(SCAN_CHUNKS):
            r = slice(k * CHUNK, (k + 1) * CHUNK)
            st_ref[k] = St
            o, St = _dn_scan_fn(_heads(u_ref[r, :]), _heads(w_ref[r, :]), _heads(qk_ref[r, :])[:, :, :CHUNK],
                                _heads(qd_ref[r, :]), _heads(kd_ref[r, :]), _heads(eg_ref[r, :]), St)
            for h in range(N_HEADS_D):
                o_ref[r, DK * h:DK * (h + 1)] = o[h]
        s_ref[...] = St

    blk = pl.BlockSpec((rows, N_HEADS_D * DK), lambda n: (n, 0))
    return pl.pallas_call(
        body, grid=(S // rows,), in_specs=[blk] * 6,
        out_specs=[blk, pl.BlockSpec((SCAN_CHUNKS, N_HEADS_D, DK, DK), lambda n: (n, 0, 0, 0))],
        out_shape=[SDS((S, N_HEADS_D * DK), f32), SDS((NCH, N_HEADS_D, DK, DK), f32)],
        scratch_shapes=[pltpu.VMEM((N_HEADS_D, DK, DK), f32)], name=name, compiler_params=_params("arbitrary"))(*pre)


def _dn_scan_bwd(pre, states, do, name):
    S = do.shape[0]
    rows = SCAN_CHUNKS * CHUNK
    steps = S // rows

    def body(u_ref, w_ref, qk_ref, qd_ref, kd_ref, eg_ref, st_ref, do_ref,
             du_ref, dw_ref, dqk_ref, dqd_ref, dkd_ref, deg_ref, ds_ref):
        @pl.when(pl.program_id(0) == 0)
        def _():
            ds_ref[...] = jnp.zeros_like(ds_ref)

        dS = ds_ref[...]
        for k in reversed(range(SCAN_CHUNKS)):
            r = slice(k * CHUNK, (k + 1) * CHUNK)
            _, vjp = jax.vjp(_dn_scan_fn, _heads(u_ref[r, :]), _heads(w_ref[r, :]), _heads(qk_ref[r, :])[:, :, :CHUNK],
                             _heads(qd_ref[r, :]), _heads(kd_ref[r, :]), _heads(eg_ref[r, :]), st_ref[k])
            du, dw, dqk, dqd, dkd, deg, dS = vjp((_heads(do_ref[r, :]), dS))
            for h in range(N_HEADS_D):
                c = slice(DK * h, DK * (h + 1))
                for ref, val in ((du_ref, du), (dw_ref, dw), (dqd_ref, dqd), (dkd_ref, dkd), (deg_ref, deg)):
                    ref[r, c] = val[h]
                dqk_ref[r, DK * h:DK * h + CHUNK] = dqk[h]
                dqk_ref[r, DK * h + CHUNK:DK * (h + 1)] = jnp.zeros((CHUNK, DK - CHUNK), f32)
        ds_ref[...] = dS

    blk = pl.BlockSpec((rows, N_HEADS_D * DK), lambda n: (steps - 1 - n, 0))
    return pl.pallas_call(
        body, grid=(steps,),
        in_specs=[blk] * 6 + [pl.BlockSpec((SCAN_CHUNKS, N_HEADS_D, DK, DK), lambda n: (steps - 1 - n, 0, 0, 0)), blk],
        out_specs=[blk] * 6, out_shape=[SDS((S, N_HEADS_D * DK), f32)] * 6,
        scratch_shapes=[pltpu.VMEM((N_HEADS_D, DK, DK), f32)], name=name,
        compiler_params=_params("arbitrary"))(*pre, states, do)


def _loss_head(y, t, name):
    S, D = y.shape
    tm = ROW_TILE

    def body(y_ref, t_ref, dy_ref, l_ref):
        i = pl.program_id(0)
        d = y_ref[...] - t_ref[...]
        dy_ref[...] = d * (1.0 / D)
        part = jnp.sum(jnp.sum(d * d, axis=1, keepdims=True), axis=0, keepdims=True) * (0.5 / D)

        @pl.when(i == 0)
        def _():
            l_ref[...] = jnp.zeros_like(l_ref)

        l_ref[...] += jnp.broadcast_to(part, l_ref.shape)

    spec = pl.BlockSpec((tm, D), lambda i: (i, 0))
    dy, l = pl.pallas_call(body, grid=(S // tm,), in_specs=[spec, spec],
                           out_specs=[spec, pl.BlockSpec((1, 128), lambda i: (0, 0))],
                           out_shape=[SDS((S, D), f32), SDS((1, 128), f32)], name=name,
                           compiler_params=_params("arbitrary"))(y, t)
    return l[0, 0], dy


def _adamw_refs(w_ref, g_ref, m_ref, v_ref, d_ref, mo_ref, vo_ref):
    gv = g_ref[...]
    m2 = ADAM_B1 * m_ref[...] + (1.0 - ADAM_B1) * gv
    v2 = ADAM_B2 * v_ref[...] + (1.0 - ADAM_B2) * (gv * gv)
    m_hat = m2 / (1.0 - ADAM_B1 ** ADAM_STEP)
    v_hat = v2 / (1.0 - ADAM_B2 ** ADAM_STEP)
    d_ref[...] = -ADAM_LR * (m_hat / (jnp.sqrt(v_hat) + ADAM_EPS) + ADAM_WD * w_ref[...])
    mo_ref[...] = m2
    vo_ref[...] = v2


def _adamw_small(ws, gs, ms, vs, name):
    n = len(ws)

    def body(*refs):
        for i in range(n):
            _adamw_refs(*[refs[k * n + i] for k in range(7)])

    res = pl.pallas_call(body, out_shape=[SDS(a.shape, f32) for a in ws] * 3, name=name)(*ws, *gs, *ms, *vs)
    return res[:n], res[n:2 * n], res[2 * n:]


def _adamw(w, g, m, v, tr, name):
    L, R, C = w.shape
    assert R % tr == 0

    def body(*refs):
        _adamw_refs(*refs)

    spec = pl.BlockSpec((1, tr, C), lambda l, i: (l, i, 0))
    return pl.pallas_call(body, grid=(L, R // tr), in_specs=[spec] * 4, out_specs=[spec] * 3,
                          out_shape=[SDS((L, R, C), f32)] * 3, name=name,
                          compiler_params=_params("parallel", "parallel"))(w, g, m, v)


def _rope_tables(S):
    inv = 1.0 / (10000.0 ** (jnp.arange(0, HEAD_DIM, 2, dtype=f32) / HEAD_DIM))
    ang = jnp.arange(S, dtype=f32)[:, None] * inv[None, :]
    cos, sin = jnp.cos(ang), jnp.sin(ang)
    return (jnp.tile(jnp.concatenate([cos, cos], axis=1), (1, N_HEADS_A)),
            jnp.tile(jnp.concatenate([-sin, sin], axis=1), (1, N_HEADS_A)))


def _layer_fwd(x, W, cos, sgn_sin, l, late_weights=None, h1=None, next_pre_mix=None, mid_mixer=None, after_ffn_in=None):
    n = f"l{l}_"
    if h1 is None:
        (h1,) = _rows(_rms_fn, [x], [W["norm_pre_mix"]], [(D_MODEL, bf16)], n + "pre_mix_norm")
    proj = _mm(h1, W["w_in"], "nn", 1024, 768, f32, n + "in_proj")
    attn_out, lse = _attn_fwd(proj, cos, sgn_sin, n + "attn_fwd")
    (qkv,) = _colconv_fwd([(proj, DN_QKV_COL // 512)], [(W["dn_conv_w"], 0)], None, 4, _silu_fn, 3, 512, [f32], n + "dn_conv")
    dn_pre = _dn_prep(qkv, proj, W["dn_a_log"], W["dn_dt_bias"], n + "dn_prep")
    if mid_mixer is not None:
        W = dict(W, dn_norm_w=W["dn_norm_w"] + mid_mixer(dn_pre[0]))
    dn_o, dn_states = _dn_scan(dn_pre, n + "dn_scan")
    (dn_out,) = _rows(_dn_post_fn, [(dn_o, DK, 0), (proj, DK, DN_Z_COL // DK)], [W["dn_norm_w"]], [(DK, bf16)], n + "dn_post",
                      ncol=N_HEADS_D, tm=4 * ROW_TILE)
    mix_in = jnp.concatenate([attn_out, dn_out], axis=1)
    late = late_weights(mix_in) if late_weights is not None else {}
    W = {**W, **late}
    mix = _mm(mix_in, W["w_out"], "nn", 1024, 512, f32, n + "out_proj")
    x1, h2 = _rows(_res_rms_pre_fn, [mix, x], [W["norm_post_mix"], W["norm_pre_ffn"]],
                   [(D_MODEL, f32), (D_MODEL, bf16)], n + "post_mix_pre_ffn_norm")
    u0 = _mm(h2, W["ffn_w_in"], "nn", 1024, D_FF // 2, bf16, n + "ffn_in")
    if after_ffn_in is not None:
        W = dict(W, ffn_conv_b=W["ffn_conv_b"] + after_ffn_in(u0))
    nb_ff = D_FF // 256
    (act,) = _colconv_fwd([(u0, 0), (u0, nb_ff)], [(W["ffn_conv_w"], 0), (W["ffn_conv_w"], nb_ff)],
                          [(W["ffn_conv_b"], 0), (W["ffn_conv_b"], nb_ff)], 3, _geglu_fn, nb_ff, 256, [bf16],
                          n + "ffn_conv_glu")
    f = _mm(act, W["ffn_w_out"], "nn", 1024, 512, f32, n + "ffn_out")
    if next_pre_mix is None:
        (x2,), h1_next = _rows(_res_rms_fn, [f, x1], [W["norm_post_ffn"]], [(D_MODEL, f32)], n + "post_ffn_norm"), None
    else:
        x2, h1_next = _rows(_res_rms_pre_fn, [f, x1], [W["norm_post_ffn"], next_pre_mix],
                            [(D_MODEL, f32), (D_MODEL, bf16)], n + "post_ffn_next_pre_mix_norm")
    saved = dict(x=x, h1=h1, proj=proj, attn_out=attn_out, lse=lse, qkv=qkv, dn_pre=dn_pre, dn_o=dn_o,
                 dn_states=dn_states, mix_in=mix_in, mix=mix, x1=x1, h2=h2, u0=u0, act=act, f=f, late=late)
    return x2, h1_next, saved


def _layer_bwd(dx2, sv, W, cos, sgn_sin, l, after_ffn=None, next_layer=None, first_layer=True):
    n = f"l{l}_"
    S = dx2.shape[0]
    g = {}
    g_next_pre = None
    if next_layer is None:
        (df,), (g["norm_post_ffn"],) = _rows_vjp(_rms_fn, [sv["f"]], [W["norm_post_ffn"]], [dx2], [0], [0],
                                                 n + "post_ffn_norm_bwd", row_dtype=bf16)
    else:
        (df, dx2), (g["norm_post_ffn"], g_next_pre) = _rows_vjp(
            _res_rms_pre_fn, [sv["f"], sv["x1"]], [W["norm_post_ffn"], next_layer[1]], [dx2, next_layer[0]], [0, 1], [0, 1],
            n + "post_ffn_next_pre_mix_norm_bwd", row_dtype=[bf16, f32])
    dact = _mm(df, W["ffn_w_out"], "nt", 1024, D_FF // 2, f32, n + "ffn_out_dx")
    g["ffn_w_out"] = _mm(sv["act"], df, "tn", D_FF // 2, 512, f32, n + "ffn_out_dw")
    nb_ff = D_FF // 256
    u0 = sv["u0"]
    dxs, dws, dbs = _colconv_bwd([(u0, 0), (u0, nb_ff)], [(W["ffn_conv_w"], 0), (W["ffn_conv_w"], nb_ff)],
                                 [(W["ffn_conv_b"], 0), (W["ffn_conv_b"], nb_ff)], 3, _geglu_fn, [dact], nb_ff, 256,
                                 n + "ffn_conv_glu_bwd", dx_dtype=bf16)
    du0 = jnp.concatenate(dxs, axis=1)
    g["ffn_conv_w"] = jnp.concatenate(dws, axis=1)
    g["ffn_conv_b"] = jnp.concatenate(dbs, axis=1)
    dh2 = _mm(du0, W["ffn_w_in"], "nt", 1024, 512, f32, n + "ffn_in_dx")
    g["ffn_w_in"] = _mm(sv["h2"], du0, "tn", 1024, D_FF // 2, f32, n + "ffn_in_dw", column_shards=True)
    if after_ffn is not None:
        W = dict(W, norm_post_mix=W["norm_post_mix"] + after_ffn(g, dh2))
    (dmix, dx1), (g["norm_post_mix"], g["norm_pre_ffn"]) = _rows_vjp(
        _res_rms_pre_fn, [sv["mix"], sv["x"]], [W["norm_post_mix"], W["norm_pre_ffn"]], [dx2, dh2], [0, 1], [0, 1],
        n + "post_mix_pre_ffn_norm_bwd", row_dtype=[bf16, f32])
    dmix_in = _mm(dmix, W["w_out"], "nt", 1024, 512, f32, n + "out_proj_dx")
    g["w_out"] = _mm(sv["mix_in"], dmix, "tn", 1024, 512, f32, n + "out_proj_dw")

    (ddn_o, dz), (g["dn_norm_w"],) = _rows_vjp(
        _dn_post_fn, [(sv["dn_o"], DK, 0), (sv["proj"], DK, DN_Z_COL // DK)], [W["dn_norm_w"]], [(dmix_in, DK, ATTN_W // DK)],
        [0, 1], [0], n + "dn_post_bwd", ncol=N_HEADS_D, tm=4 * ROW_TILE)
    dpre = _dn_scan_bwd(sv["dn_pre"], sv["dn_states"], ddn_o, n + "dn_scan_bwd")
    dq, dk, dv, dba, g["dn_a_log"], g["dn_dt_bias"] = _dn_prep_bwd(
        sv["qkv"], sv["proj"], W["dn_a_log"], W["dn_dt_bias"], dpre, n + "dn_prep_bwd")
    dqkv = jnp.concatenate([dq, dk, dv], axis=1)
    (dqkv0,), (g["dn_conv_w"],), _ = _colconv_bwd([(sv["proj"], DN_QKV_COL // 512)], [(W["dn_conv_w"], 0)], None, 4, _silu_fn,
                                                 [dqkv], 3, 512, n + "dn_conv_bwd")

    daq, dak, dav = _attn_bwd(sv["proj"], cos, sgn_sin, dmix_in, sv["attn_out"], sv["lse"], n + "attn_bwd")
    dproj = jnp.concatenate([daq, dak, dav, dqkv0, dz, dba, jnp.zeros((S, PROJ_W - DN_BA_COL - 128), f32)], axis=1).astype(bf16)
    dh1 = _mm(dproj, W["w_in"], "nt", 1024, 512, f32, n + "in_proj_dx")
    g["w_in"] = _mm(sv["h1"], dproj, "tn", 1024, 768, f32, n + "in_proj_dw")
    if not first_layer:
        return (dx1, dh1), g, g_next_pre
    (dx,), (g["norm_pre_mix"],) = _rows_vjp(_rms_fn, [sv["x"]], [W["norm_pre_mix"]], [dh1], [0], [0],
                                            n + "pre_mix_norm_bwd", adds={0: dx1})
    return dx, g, g_next_pre


def _local_step(x, target, layers):
    cos, sgn_sin = _rope_tables(x.shape[0])
    saved, h1 = [], None
    for l, W in enumerate(layers):
        nxt = layers[l + 1]["norm_pre_mix"] if l + 1 < len(layers) else None
        x, h1, sv = _layer_fwd(x, W, cos, sgn_sin, l, h1=h1, next_pre_mix=nxt)
        saved.append(sv)
    loss, dx = _loss_head(x, target, "loss_head")
    grads = [None] * len(layers)
    nxt = None
    for l in reversed(range(len(layers))):
        dx, grads[l], g_pre = _layer_bwd(dx, saved[l], layers[l], cos, sgn_sin, l, next_layer=nxt, first_layer=(l == 0))
        if g_pre is not None:
            grads[l + 1]["norm_pre_mix"] = g_pre
        if l > 0:
            dx, dh1 = dx
            nxt = (dh1, layers[l]["norm_pre_mix"])
    return loss, dx, grads


def _pos():
    x, y, c = lax.axis_index("x"), lax.axis_index("y"), lax.axis_index("c")
    return x, y, c, [(1 - x, y), (x, 1 - y), (1 - x, 1 - y)]


def _rcopy(src, dst, send_sem, recv_sem, dev):
    return pltpu.make_async_remote_copy(src_ref=src, dst_ref=dst, send_sem=send_sem, recv_sem=recv_sem,
                                        device_id=dev, device_id_type=MESH)


def _half_rows(ref, h, which, axis):
    if h is None:
        return ref
    rows = pl.ds(pl.multiple_of(which * h, 16), h)
    return ref.at[:, rows, :] if axis == 1 else ref.at[rows, :]


def _dma_sems(*counts):
    return [pltpu.SemaphoreType.DMA((k,)) for k in counts]


def _all_gather(arrs, halves, name):
    n = len(arrs)

    def body(*refs):
        ins, outs = refs[:n], refs[n:2 * n]
        send1, recv1, send2, recv2 = refs[2 * n:]
        x, y, c, chips = _pos()
        me, sib, s_me = (x, y, c), (x, y, 1 - c), 2 * x + y
        sends = []
        for i in range(n):
            for j, chip in enumerate(chips):
                cp = _rcopy(_half_rows(ins[i], halves[i], c, 1), _half_rows(outs[i].at[s_me], halves[i], c, 1),
                            send1.at[3 * i + j], recv1.at[3 * i + j], (*chip, c))
                cp.start()
                sends.append(cp)
        for i in range(n):
            for j, (px, py) in enumerate(chips):
                k = 3 * i + j
                landed = _half_rows(outs[i].at[2 * px + py], halves[i], c, 1)
                _rcopy(landed, landed, send1.at[k], recv1.at[k], me).wait_recv()
                if halves[i] is not None:
                    cp = _rcopy(landed, landed, send2.at[k], recv2.at[k], sib)
                    cp.start()
                    sends.append(cp)
        for i in range(n):
            if halves[i] is None:
                continue
            for j, (px, py) in enumerate(chips):
                k = 3 * i + j
                other = _half_rows(outs[i].at[2 * px + py], halves[i], 1 - c, 1)
                _rcopy(other, other, send2.at[k], recv2.at[k], me).wait_recv()
        for cp in sends:
            cp.wait_send()

    return pl.pallas_call(
        body, in_specs=[ANY] * n, out_specs=[ANY] * n,
        out_shape=[SDS((4,) + a.shape, a.dtype) for a in arrs],
        scratch_shapes=_dma_sems(3 * n, 3 * n, 3 * n, 3 * n), name=name)(*arrs)


HBM = pl.BlockSpec(memory_space=pltpu.HBM)
SEM = pl.BlockSpec(memory_space=pltpu.SEMAPHORE)
_EFFECT = pltpu.SideEffectType.DATAFLOW_SIDE_EFFECTING


def _in_hbm(a):
    return pltpu.with_memory_space_constraint(a, pltpu.HBM)


def _split_copy(srcs, land_shapes, plan, per, after, name):
    n, nl = len(srcs), len(land_shapes)
    k = per * n

    def body(*refs):
        ins, lands, token = refs[:n], refs[n:n + nl], refs[-1]
        send, recv = refs[n + nl + 1], refs[n + nl + 2]
        for i, (src, dst, dev, _) in enumerate(plan(ins, lands)):
            _rcopy(src, dst, send.at[i], recv.at[i], dev).start()
        token[...] = jnp.zeros_like(token)

    lands = [_in_hbm(lax.empty(s.shape, s.dtype)) for s in land_shapes]
    return pl.pallas_call(
        body, name=name,
        out_shape=(pltpu.SemaphoreType.DMA((k,)), pltpu.SemaphoreType.DMA((k,)),
                   *[pltpu.HBM(a.shape, a.dtype) for a in srcs], *[pltpu.HBM(s.shape, s.dtype) for s in land_shapes],
                   SDS((8, 128), f32)),
        in_specs=[HBM] * (n + nl) + [ANY], out_specs=(SEM, SEM, *[HBM] * (n + nl), pl.BlockSpec(memory_space=pltpu.VMEM)),
        input_output_aliases={i: 2 + i for i in range(n + nl)},
        compiler_params=pltpu.CompilerParams(has_side_effects=_EFFECT))(*[_in_hbm(a) for a in srcs], *lands, after)


def _split_wait(started, n, plan, after, name, nl=None):
    nl = n if nl is None else nl
    send, recv = started[0], started[1]
    thru = started[2:2 + n + nl]

    def body(*refs):
        ins, lands = refs[:n], refs[n:n + nl]
        send_ref, recv_ref = refs[n + nl], refs[n + nl + 1]
        for i, (src, _, dev, mine) in enumerate(plan(ins, lands)):
            cp = _rcopy(src, mine, send_ref.at[i], recv_ref.at[i], dev)
            cp.wait_send()
            cp.wait_recv()

    res = pl.pallas_call(
        body, name=name, out_shape=tuple(pltpu.HBM(a.shape, a.dtype) for a in thru),
        in_specs=[HBM] * (n + nl) + [SEM, SEM, ANY], out_specs=tuple([HBM] * (n + nl)),
        input_output_aliases={i: i for i in range(n + nl)},
        compiler_params=pltpu.CompilerParams(has_side_effects=_EFFECT))(*thru, send, recv, after)
    return res[:n], res[n:]


def _gather_plan(halves):
    def plan(ins, lands):
        x, y, c, chips = _pos()
        out = []
        for i in range(len(ins)):
            for px, py in chips:
                out.append((_half_rows(ins[i], halves[i], c, 1), _half_rows(lands[i].at[2 * x + y], halves[i], c, 1),
                            (px, py, c), _half_rows(lands[i].at[2 * px + py], halves[i], c, 1)))
        return out
    return plan


def _scatter_plan(ins, lands):
    x, y, c, chips = _pos()
    out = []
    for i in range(len(ins)):
        for j, (px, py) in enumerate(chips):
            out.append((ins[i].at[2 * px + py], lands[i].at[j], (px, py, c), lands[i].at[j]))
    return out


def _exchange_plan(ins, lands):
    x, y, c, _ = _pos()
    return [(_half_rows(g, g.shape[1] // 2, 1 - c, 1), land, (x, y, 1 - c), land) for g, land in zip(ins, lands)]


def _pass_plan(halves):
    def plan(ins, lands):
        x, y, c, chips = _pos()
        out = []
        for i in range(len(ins)):
            for px, py in chips:
                slot = ins[i].at[2 * px + py]
                out.append((_half_rows(slot, halves[i], c, 1), _half_rows(slot, halves[i], c, 1), (x, y, 1 - c),
                            _half_rows(slot, halves[i], 1 - c, 1)))
        return out
    return plan


def _exchange_halves(gs, name):
    n = len(gs)

    def body(*refs):
        ins, outs = refs[:n], refs[n:2 * n]
        send, recv = refs[2 * n:]
        x, y, c, _ = _pos()
        sends = []
        for k in range(n):
            cp = _rcopy(_half_rows(ins[k], gs[k].shape[1] // 2, 1 - c, 1), outs[k], send.at[k], recv.at[k], (x, y, 1 - c))
            cp.start()
            sends.append(cp)
        for k in range(n):
            _rcopy(outs[k], outs[k], send.at[k], recv.at[k], (x, y, c)).wait_recv()
        for cp in sends:
            cp.wait_send()

    return pl.pallas_call(
        body, in_specs=[ANY] * n, out_specs=[ANY] * n,
        out_shape=[SDS((4, g.shape[1] // 2, g.shape[2]), g.dtype) for g in gs],
        scratch_shapes=_dma_sems(n, n), name=name)(*gs)


def _scatter_partials(ps, name):
    n = len(ps)

    def body(*refs):
        ins, outs = refs[:n], refs[n:2 * n]
        send, recv = refs[2 * n:]
        x, y, c, chips = _pos()
        sends = []
        for k in range(n):
            for j, (px, py) in enumerate(chips):
                cp = _rcopy(ins[k].at[2 * px + py], outs[k].at[j], send.at[3 * k + j], recv.at[3 * k + j], (px, py, c))
                cp.start()
                sends.append(cp)
        for k in range(n):
            for j in range(3):
                _rcopy(outs[k].at[j], outs[k].at[j], send.at[3 * k + j], recv.at[3 * k + j], (x, y, c)).wait_recv()
        for cp in sends:
            cp.wait_send()

    return pl.pallas_call(
        body, in_specs=[ANY] * n, out_specs=[ANY] * n,
        out_shape=[SDS((3,) + p.shape[1:], p.dtype) for p in ps],
        scratch_shapes=_dma_sems(3 * n, 3 * n), name=name)(*ps)


def _join_halves(rs, layers, name):
    n = len(rs)

    def body(*refs):
        outs = refs[n:2 * n]
        send, recv = refs[2 * n:]
        x, y, c, _ = _pos()

        def half(k, which):
            h = rs[k].shape[1] // 2
            return _half_rows(outs[k], h, which, 1) if layers[k] is None else _half_rows(outs[k].at[layers[k]], h, which, 0)

        sends = []
        for k in range(n):
            cp = _rcopy(half(k, c), half(k, c), send.at[k], recv.at[k], (x, y, 1 - c))
            cp.start()
            sends.append(cp)
        for k in range(n):
            _rcopy(half(k, 1 - c), half(k, 1 - c), send.at[k], recv.at[k], (x, y, c)).wait_recv()
        for cp in sends:
            cp.wait_send()

    return pl.pallas_call(
        body, in_specs=[ANY] * n, out_specs=[ANY] * n, out_shape=[SDS(r.shape, r.dtype) for r in rs],
        input_output_aliases={k: k for k in range(n)}, scratch_shapes=_dma_sems(n, n), name=name)(*rs)


def _all_reduce_small(pack, name):
    R = pack.shape[0]

    def body(in_ref, out_ref, buf, send, recv):
        x, y, c, _ = _pos()
        me = 4 * x + 2 * y + c
        buf[me] = in_ref[...]
        sends = []
        for k in range(1, 8):
            peer = me ^ k
            cp = _rcopy(buf.at[me], buf.at[me], send.at[k - 1], recv.at[k - 1], ((peer >> 2) & 1, (peer >> 1) & 1, peer & 1))
            cp.start()
            sends.append(cp)
        for k in range(1, 8):
            _rcopy(buf.at[me ^ k], buf.at[me ^ k], send.at[k - 1], recv.at[k - 1], (x, y, c)).wait_recv()
        for cp in sends:
            cp.wait_send()
        acc = buf[0]
        for d in range(1, 8):
            acc = acc + buf[d]
        out_ref[...] = acc

    return pl.pallas_call(
        body, out_shape=SDS((R, 128), f32),
        in_specs=[pl.BlockSpec(memory_space=pltpu.VMEM)], out_specs=pl.BlockSpec(memory_space=pltpu.VMEM),
        scratch_shapes=[pltpu.VMEM((8, R, 128), f32)] + _dma_sems(7, 7), name=name)(pack)


def _add_sibling(g, recv, c_arr, tr, name):
    _, R, C = g.shape
    h = R // 2
    nrb = h // tr
    assert h % tr == 0

    def body(c_ref, g_ref, r_ref, o_ref):
        o_ref[...] = (g_ref[...] + r_ref[...]).astype(o_ref.dtype)

    spec = pl.BlockSpec((1, tr, C), lambda s, r, c_ref: (s, r, 0))
    grid_spec = pltpu.PrefetchScalarGridSpec(
        num_scalar_prefetch=1, grid=(4, nrb),
        in_specs=[pl.BlockSpec((1, tr, C), lambda s, r, c_ref: (s, c_ref[0] * nrb + r, 0)), spec], out_specs=spec)
    return pl.pallas_call(body, grid_spec=grid_spec, out_shape=SDS((4, h, C), bf16), name=name,
                          compiler_params=_params("parallel", "parallel"))(c_arr, g, recv)


def _add_chips(p, recv, sc_arr, tr, layer, into, name):
    _, h, C = p.shape
    nrb = h // tr
    assert h % tr == 0

    def body(sc_ref, p_ref, r_ref, *rest):
        rest[-1][...] = (p_ref[0].astype(f32) + r_ref[0].astype(f32)) + (r_ref[1].astype(f32) + r_ref[2].astype(f32))

    grid_spec = pltpu.PrefetchScalarGridSpec(
        num_scalar_prefetch=1, grid=(nrb,),
        in_specs=[pl.BlockSpec((1, tr, C), lambda r, sc_ref: (sc_ref[0], r, 0)),
                  pl.BlockSpec((3, tr, C), lambda r, sc_ref: (0, r, 0))] + ([] if into is None else [ANY]),
        out_specs=pl.BlockSpec((None, tr, C), lambda r, sc_ref: (layer, sc_ref[1] * nrb + r, 0)))
    return pl.pallas_call(body, grid_spec=grid_spec, out_shape=SDS((2, 2 * h, C), f32), name=name,
                          input_output_aliases={} if into is None else {3: 0},
                          compiler_params=_params("parallel"))(sc_arr, p, recv, *([] if into is None else [into]))


_BIG = (("w_in", 1024, 256), ("w_out", 256, 128), ("ffn_w_in", 1024, 256), ("ffn_w_out", 704, 352))
_SMALL = ("dn_conv_w", "ffn_conv_w", "ffn_conv_b", "norm_pre_mix", "norm_post_mix", "norm_pre_ffn", "norm_post_ffn",
          "dn_norm_w", "dn_a_log", "dn_dt_bias")
_WEIGHTS = ("w_in", "dn_conv_w", "dn_a_log", "dn_dt_bias", "dn_norm_w", "w_out", "ffn_w_in", "ffn_conv_w", "ffn_conv_b",
            "ffn_w_out", "norm_pre_mix", "norm_post_mix", "norm_pre_ffn", "norm_post_ffn")
_ADAM_ROWS = {"w_in": 512, "w_out": 256, "ffn_w_in": 256, "ffn_w_out": 352}


def _shard_major(name, g):
    if name == "w_in":
        width = IN_COLS // N_SHARDS
        return jnp.stack([g[:, width * s:width * (s + 1)] for s in range(N_SHARDS)])
    if name == "ffn_w_in":
        return g
    return g.reshape(4, g.shape[0] // 4, g.shape[1])


def kernel(x, w_in, dn_conv_w, dn_a_log, dn_dt_bias, dn_norm_w, w_out, ffn_w_in, ffn_conv_w, ffn_conv_b, ffn_w_out, norm_pre_mix, norm_post_mix, norm_pre_ffn, norm_post_ffn, loss_target, m_w_in, m_dn_conv_w, m_dn_a_log, m_dn_dt_bias, m_dn_norm_w, m_w_out, m_ffn_w_in, m_ffn_conv_w, m_ffn_conv_b, m_ffn_w_out, m_norm_pre_mix, m_norm_post_mix, m_norm_pre_ffn, m_norm_post_ffn, v_w_in, v_dn_conv_w, v_dn_a_log, v_dn_dt_bias, v_dn_norm_w, v_w_out, v_ffn_w_in, v_ffn_conv_w, v_ffn_conv_b, v_ffn_w_out, v_norm_pre_mix, v_norm_post_mix, v_norm_pre_ffn, v_norm_post_ffn):
    w = dict(w_in=w_in, dn_conv_w=dn_conv_w, dn_a_log=dn_a_log, dn_dt_bias=dn_dt_bias, dn_norm_w=dn_norm_w, w_out=w_out,
             ffn_w_in=ffn_w_in, ffn_conv_w=ffn_conv_w, ffn_conv_b=ffn_conv_b, ffn_w_out=ffn_w_out, norm_pre_mix=norm_pre_mix,
             norm_post_mix=norm_post_mix, norm_pre_ffn=norm_pre_ffn, norm_post_ffn=norm_post_ffn)
    m = dict(w_in=m_w_in, dn_conv_w=m_dn_conv_w, dn_a_log=m_dn_a_log, dn_dt_bias=m_dn_dt_bias, dn_norm_w=m_dn_norm_w,
             w_out=m_w_out, ffn_w_in=m_ffn_w_in, ffn_conv_w=m_ffn_conv_w, ffn_conv_b=m_ffn_conv_b, ffn_w_out=m_ffn_w_out,
             norm_pre_mix=m_norm_pre_mix, norm_post_mix=m_norm_post_mix, norm_pre_ffn=m_norm_pre_ffn,
             norm_post_ffn=m_norm_post_ffn)
    v = dict(w_in=v_w_in, dn_conv_w=v_dn_conv_w, dn_a_log=v_dn_a_log, dn_dt_bias=v_dn_dt_bias, dn_norm_w=v_dn_norm_w,
             w_out=v_w_out, ffn_w_in=v_ffn_w_in, ffn_conv_w=v_ffn_conv_w, ffn_conv_b=v_ffn_conv_b, ffn_w_out=v_ffn_w_out,
             norm_pre_mix=v_norm_pre_mix, norm_post_mix=v_norm_post_mix, norm_pre_ffn=v_norm_pre_ffn,
             norm_post_ffn=v_norm_post_ffn)
    xi, yi, ci = lax.axis_index("x"), lax.axis_index("y"), lax.axis_index("c")
    s_me = 2 * xi + yi
    c_arr = jnp.reshape(ci, (1,)).astype(jnp.int32)
    sc_arr = jnp.stack([s_me, ci]).astype(jnp.int32)

    mats = [name for name, _, _ in _BIG]
    rest = mats[1:]
    half_of = {name: rows // 2 for name, rows, _ in _BIG}
    tiles = {name: tr for name, _, tr in _BIG}
    gathered_shape = lambda a: SDS((4,) + a.shape, a.dtype)

    own = {k: w[k].astype(bf16) for k in mats}
    plan_in = _gather_plan([half_of["w_in"], None, None])
    src_in = [own["w_in"][0:1], dn_conv_w, ffn_conv_w]
    started_in = _split_copy(src_in, [gathered_shape(a) for a in src_in], plan_in, 3, src_in[0], "weights_gather_w_in0_start")
    plan0 = _gather_plan([half_of[k] for k in rest])
    src0 = [own[k][0:1] for k in rest]
    started0 = _split_copy(src0, [gathered_shape(a) for a in src0], plan0, 3, started_in[-1], "weights_gather_l0_start")
    plan1 = _gather_plan([half_of[k] for k in mats])
    src1 = [own[k][1:2] for k in mats]
    started1 = _split_copy(src1, [gathered_shape(a) for a in src1], plan1, 3, started0[-1], "weights_gather_l1_start")
    _, landed_in = _split_wait(started_in, len(src_in), plan_in, started1[-1], "weights_gather_w_in0_wait")
    pass_in = _pass_plan([half_of["w_in"]])
    passed_in = _split_copy(landed_in[:1], [], pass_in, 3, landed_in[0], "weights_pass_w_in0_start")
    got_in = list(_split_wait(passed_in, 1, pass_in, passed_in[-1], "weights_pass_w_in0_wait", nl=0)[0]) + list(landed_in[1:])

    def pick(mine, gathered):
        return [jnp.where(s_me == s, mine, gathered[s]) for s in range(4)]

    conv = {"dn_conv_w": jnp.concatenate(pick(dn_conv_w, got_in[1]), axis=-1),
            "ffn_conv_w": jnp.concatenate(pick(ffn_conv_w, got_in[2]), axis=-1)}
    lanes = lambda a: jnp.pad(a, ((0, 0), (0, 128 - a.shape[1])))
    vec = dict(dn_a_log=lanes(dn_a_log), dn_dt_bias=lanes(dn_dt_bias), dn_norm_w=dn_norm_w, ffn_conv_b=ffn_conv_b,
               norm_pre_mix=norm_pre_mix, norm_post_mix=norm_post_mix, norm_pre_ffn=norm_pre_ffn, norm_post_ffn=norm_post_ffn)

    def matrices(l, names, gathered):
        W = {}
        for k, a in zip(names, gathered):
            if k in ("w_out", "ffn_w_out"):
                rows_, cols = own[k].shape[1:]
                W[k] = lax.dynamic_update_slice(a[:, 0], own[k][l][None], (s_me, 0, 0)).reshape(4 * rows_, cols)
            else:
                cat = jnp.concatenate(pick(own[k][l], a[:, 0]), axis=-1)
                W[k] = jnp.pad(cat, ((0, 0), (0, PROJ_W - IN_COLS))) if k == "w_in" else cat
        return W

    def small_weights(l):
        return {**{k: a[l] for k, a in conv.items()}, **{k: a[l:l + 1] for k, a in vec.items()}}

    pass0, pass1 = _pass_plan([half_of[k] for k in rest]), _pass_plan([half_of[k] for k in mats])
    passing = {}

    def mid_mixer_l0(marker):
        _, landed = _split_wait(started0, len(rest), plan0, marker, "weights_gather_l0_wait")
        passing["l0"] = _split_copy(landed, [], pass0, 3, marker, "weights_pass_l0_start")
        return passing["l0"][-1][0, 0]

    def late_l0(mix_in):
        return matrices(0, rest, _split_wait(passing["l0"], len(rest), pass0, mix_in, "weights_pass_l0_wait", nl=0)[0])

    def after_ffn_in_l0(marker):
        _, landed = _split_wait(started1, len(mats), plan1, marker, "weights_gather_l1_wait")
        passing["l1"] = _split_copy(landed, [], pass1, 3, marker, "weights_pass_l1_start")
        return passing["l1"][-1][0, 0]

    cos, sgn_sin = _rope_tables(x.shape[1])
    W0 = {**small_weights(0), **matrices(0, ["w_in"], got_in[:1])}
    W0_first = dict(W0, norm_pre_mix=W0["norm_pre_mix"] + started1[-1][0, 0])
    x1, h1_l1, saved0 = _layer_fwd(x[0], W0_first, cos, sgn_sin, 0, late_weights=late_l0, next_pre_mix=norm_pre_mix[1:2],
                                   mid_mixer=mid_mixer_l0, after_ffn_in=after_ffn_in_l0)
    W1 = {**small_weights(1),
          **matrices(1, mats, _split_wait(passing["l1"], len(mats), pass1, x1, "weights_pass_l1_wait", nl=0)[0])}
    x2, _, saved1 = _layer_fwd(x1, W1, cos, sgn_sin, 1, h1=h1_l1)
    loss_local, dy = _loss_head(x2, loss_target[0], "loss_head")
    loss = lax.psum(loss_local, ("x", "y", "c"))

    def shard_major(names, grads_l):
        return [_shard_major(name, grads_l[name]) for name in names]

    def add_siblings(l, names, gs, from_sib):
        return [_add_sibling(g, r, c_arr, tiles[name], f"add_sibling_{name}{l}") for g, r, name in zip(gs, from_sib, names)]

    def scatter_start(l, names, parts, after, tag):
        return _split_copy(parts, [SDS((3,) + p.shape[1:], p.dtype) for p in parts], _scatter_plan, 3, after,
                           f"grads_l{l}{tag}_scatter_start")

    def owner_sums(l, names, sent, after, tag, into):
        parts, recvd = _split_wait(sent, len(names), _scatter_plan, after, f"grads_l{l}{tag}_scatter_wait")
        return {name: _add_chips(p, r, sc_arr, tiles[name], l, into.get(name), f"add_chips_{name}{l}")
                for p, r, name in zip(parts, recvd, names)}

    (dx1, dh1_l1), grads1, _ = _layer_bwd(dy, saved1, W1, cos, sgn_sin, 1, first_layer=False)
    gs1 = shard_major(mats, grads1)
    swap1 = _split_copy(gs1, [SDS((4, g.shape[1] // 2, g.shape[2]), g.dtype) for g in gs1], _exchange_plan, 1, dx1,
                        "grads_l1_sibling_start")
    ffn = ["ffn_w_in", "ffn_w_out"]
    launched = {}

    def after_ffn_l0(g_ffn, dx_mid):
        gs1_, from_sib1 = _split_wait(swap1, len(mats), _exchange_plan, dx_mid, "grads_l1_sibling_wait")
        launched["l1"] = scatter_start(1, mats, add_siblings(1, mats, gs1_, from_sib1), dx_mid, "")
        gs0 = shard_major(ffn, g_ffn)
        from_sib0 = _exchange_halves(gs0, "grads_l0_ffn_to_sibling")
        launched["l0_ffn"] = scatter_start(0, ffn, add_siblings(0, ffn, gs0, from_sib0), launched["l1"][-1], "_ffn")
        return launched["l0_ffn"][-1][0, 0]

    W0_last = dict(W0, **saved0["late"], norm_post_ffn=W0["norm_post_ffn"] + swap1[-1][0, 0])
    dx, grads0, grads1["norm_pre_mix"] = _layer_bwd(dx1, saved0, W0_last, cos, sgn_sin, 0, after_ffn=after_ffn_l0,
                                                    next_layer=(dh1_l1, norm_pre_mix[1:2]))
    mix = ["w_in", "w_out"]
    gs0 = shard_major(mix, grads0)
    part0 = add_siblings(0, mix, gs0, _exchange_halves(gs0, "grads_l0_mix_to_sibling"))
    sent0 = scatter_start(0, mix, part0, dx, "_mix")
    red = owner_sums(0, ffn, launched["l0_ffn"], sent0[-1], "_ffn", {})
    red = owner_sums(1, mats, launched["l1"], sent0[-1], "", red)
    joined = dict(zip(mats, _join_halves([red[k] for k in mats], [1 if k in mix else None for k in mats],
                                         "grads_join_early")))
    grads = [grads0, grads1]

    small = {}
    for name in _SMALL:
        per_layer = [grads[l][name] for l in range(2)]
        if name in ("dn_a_log", "dn_dt_bias"):
            per_layer = [p[:, :N_HEADS_D] for p in per_layer]
        small[name] = jnp.stack(per_layer).reshape((2,) + (w[name].shape[1:] if name not in ("dn_conv_w", "ffn_conv_w")
                                                           else per_layer[0].shape))
    flat = jnp.concatenate([small[name].reshape(-1) for name in _SMALL])
    n_rows = -(-flat.shape[0] // 1024) * 8
    summed = _all_reduce_small(jnp.pad(flat, (0, n_rows * 128 - flat.shape[0])).reshape(n_rows, 128),
                               "small_grads_all_reduce").reshape(-1)
    off = 0
    g_out = {}
    for name in _SMALL:
        size = small[name].size
        g_out[name] = summed[off:off + size].reshape(small[name].shape)
        off += size
    for k in ("dn_conv_w", "ffn_conv_w"):
        width = w[k].shape[2]
        g_out[k] = lax.dynamic_slice_in_dim(g_out[k], s_me * width, width, axis=2)
    for k in ffn:
        g_out[k] = joined[k]

    deltas, new_m, new_v = {}, {}, {}

    def step(name):
        shape = w[name].shape
        as3 = (lambda a: a) if len(shape) == 3 else (lambda a: a.reshape(shape[0], 1, shape[1]))
        tr = _ADAM_ROWS.get(name, as3(w[name]).shape[1])
        d_, m_, v_ = _adamw(as3(w[name]), as3(g_out[name]), as3(m[name]), as3(v[name]), tr, f"adamw_{name}")
        deltas[name], new_m[name], new_v[name] = d_.reshape(shape), m_.reshape(shape), v_.reshape(shape)

    for name in ffn:
        step(name)
    tiny = [name for name in _WEIGHTS if name not in mats]
    stepped = _adamw_small(*[[d[name] for name in tiny] for d in (w, g_out, m, v)], "adamw_small")
    for out, vals in zip((deltas, new_m, new_v), stepped):
        out.update(zip(tiny, vals))
    done = jnp.reshape(deltas["ffn_w_in"][0, 0, 0] + deltas["ffn_w_out"][0, 0, 0] + deltas["norm_post_ffn"][0, 0], (1,))
    red = owner_sums(0, mix, sent0, done, "_mix", joined)
    for k, a in zip(mix, _join_halves([red[k] for k in mix], [0] * len(mix), "grads_join_late")):
        g_out[k] = a
        step(k)

    return (loss, dx[None], *[g_out[k] for k in _WEIGHTS], *[deltas[k] for k in _WEIGHTS],
            *[new_m[k] for k in _WEIGHTS], *[new_v[k] for k in _WEIGHTS])
```
